```python
import jax, jax.numpy as jnp
from jax import lax
import numpy as np

D_MODEL = 1024
BATCH = 16
SEQ = 256
DEPTH = 2
DEC_BATCH = 8
DEC_SEQ = 1024
PAST_LEN = 256

GRID_W = 64
N_HEADS = 8
QK_NOPE = 64
QK_ROPE = 32
QK_DIM = QK_NOPE + QK_ROPE
V_HEAD = 64
Q_LORA = 384
KV_LORA = 256
CONV_CH = 512
CONV_K = 3
FNET_GROUPS = 4
N_EXPERTS = 16
EXPERT_FF = 512
CAPACITY_FACTOR = 2
ROPE_THETA = 10000.0
EPS = 1e-6
Q_BLOCK = 128
IN0_COLS = Q_LORA + KV_LORA + QK_ROPE + 3 * CONV_CH
MIX0_OUT = N_HEADS * V_HEAD + CONV_CH

kernel_name = 'hybrid_mla_shortconv_fnet_ec_diffusion_step'


def rmsnorm(x, g):
    xf = x.astype(jnp.float32)
    y = xf * lax.rsqrt(jnp.mean(xf * xf, axis=-1, keepdims=True) + EPS)
    return (y * g.astype(jnp.float32)).astype(x.dtype)


def modulation(cond, w_mod, b_mod):
    m = jnp.einsum('bd,df->bf', jax.nn.silu(cond), w_mod) + b_mod
    return jnp.split(m[:, None, :], 6, axis=-1)


def axial_rope(n_tokens):
    rows = n_tokens // GRID_W
    row = jnp.repeat(jnp.arange(rows, dtype=jnp.float32), GRID_W)
    col = jnp.tile(jnp.arange(GRID_W, dtype=jnp.float32), rows)
    axis_dim = QK_ROPE // 2
    inv_freq = ROPE_THETA ** (-jnp.arange(0, axis_dim, 2, dtype=jnp.float32) / axis_dim)
    ang = jnp.concatenate([row[:, None] * inv_freq, col[:, None] * inv_freq], axis=-1)
    return jnp.cos(ang), jnp.sin(ang)


def rotate_rope_part(t, cos, sin):
    nope, pe = t[..., :QK_NOPE], t[..., QK_NOPE:]
    pe = pe.reshape(pe.shape[:-1] + (QK_ROPE // 2, 2))
    x1, x2 = pe[..., 0], pe[..., 1]
    c = cos[None, :, None, :].astype(t.dtype)
    s = sin[None, :, None, :].astype(t.dtype)
    pe = jnp.stack([x1 * c - x2 * s, x1 * s + x2 * c], axis=-1).reshape(nope.shape[:-1] + (QK_ROPE,))
    return jnp.concatenate([nope, pe], axis=-1)


def softmax_attention(q, k, v):
    b, nq, h, dk = q.shape
    scale = dk ** -0.5
    q_blocks = q.reshape(b, nq // Q_BLOCK, Q_BLOCK, h, dk).transpose(1, 0, 2, 3, 4)

    def one_block(qb):
        s = jnp.einsum('bqhd,bkhd->bhqk', qb, k, preferred_element_type=jnp.float32) * scale
        p = jax.nn.softmax(s, axis=-1).astype(v.dtype)
        return jnp.einsum('bhqk,bkhd->bqhd', p, v)

    o = lax.map(one_block, q_blocks)
    return o.transpose(1, 0, 2, 3, 4).reshape(b, nq, h * v.shape[-1])


def mla_expand_kv(c_kv, k_rope, w_kv_up, k_norm):
    b, n, _ = c_kv.shape
    kv = jnp.einsum('bnr,rf->bnf', c_kv, w_kv_up).reshape(b, n, N_HEADS, QK_NOPE + V_HEAD)
    k_nope, v = kv[..., :QK_NOPE], kv[..., QK_NOPE:]
    k_pe = jnp.broadcast_to(k_rope[:, :, None, :], (b, n, N_HEADS, QK_ROPE))
    k = rmsnorm(jnp.concatenate([k_nope, k_pe], axis=-1), k_norm)
    return k, v


def short_conv(u, w):
    pad = CONV_K // 2
    return lax.conv_general_dilated(u, w[:, None, :].astype(u.dtype), window_strides=(1,),
                                    padding=((pad, pad),), dimension_numbers=('NWC', 'WIO', 'NWC'),
                                    feature_group_count=u.shape[-1])


def attn_conv_mixer(h, p, ctx_cache, rope):
    b, n, _ = h.shape
    proj = jnp.einsum('bnd,df->bnf', h, p['w_in'])
    cuts = [Q_LORA, Q_LORA + KV_LORA, Q_LORA + KV_LORA + QK_ROPE,
            Q_LORA + KV_LORA + QK_ROPE + CONV_CH, Q_LORA + KV_LORA + QK_ROPE + 2 * CONV_CH]
    c_q, c_kv, k_rope, gate_b, gate_c, u = jnp.split(proj, cuts, axis=-1)
    q = jnp.einsum('bnr,rf->bnf', rmsnorm(c_q, p['q_a_norm']), p['w_q_up']).reshape(b, n, N_HEADS, QK_DIM)
    q = rmsnorm(q, p['q_norm'])
    c_kv = rmsnorm(c_kv, p['kv_a_norm'])
    k, v = mla_expand_kv(c_kv, k_rope, p['w_kv_up'], p['k_norm'])
    if rope is not None:
        cos, sin = rope
        q = rotate_rope_part(q, cos, sin)
        k = rotate_rope_part(k, cos, sin)
        k_ctx, v_ctx = mla_expand_kv(ctx_cache[0], ctx_cache[1], p['w_kv_up'], p['k_norm'])
        k = jnp.concatenate([k_ctx, k], axis=1)
        v = jnp.concatenate([v_ctx, v], axis=1)
    attn = softmax_attention(q, k, v)
    conv = gate_b * short_conv(gate_c * u, p['conv_w'])
    out = jnp.einsum('bnf,fd->bnd', jnp.concatenate([attn, conv], axis=-1), p['w_o'])
    return out, (c_kv, k_rope)


def fourier_mixer(h, w_f):
    b, n, d = h.shape
    hg = h.astype(jnp.float32).reshape(b, n, FNET_GROUPS, d // FNET_GROUPS)
    f = jnp.fft.fftn(hg, axes=(1, 3), norm='ortho').real.reshape(b, n, d).astype(h.dtype)
    return jnp.einsum('bnd,de->bne', f, w_f)


def expert_choice_ffn(x, w_router, w_gate, w_up, w_down):
    b, n, d = x.shape
    t = x.reshape(b * n, d)
    cap = CAPACITY_FACTOR * (b * n) // N_EXPERTS
    aff = jax.nn.softmax(jnp.einsum('td,de->te', t, w_router).astype(jnp.float32), axis=-1)
    g, idx = lax.top_k(aff.T, cap)
    xe = t[idx]
    hid = jax.nn.silu(jnp.einsum('ecd,edf->ecf', xe, w_gate)) * jnp.einsum('ecd,edf->ecf', xe, w_up)
    ye = jnp.einsum('ecf,efd->ecd', hid, w_down) * g[..., None].astype(x.dtype)
    out = jnp.zeros_like(t).at[idx.reshape(-1)].add(ye.reshape(-1, d))
    return out.reshape(b, n, d)


def trunk(x, cond, layers, caches, rope):
    new_caches = []
    for l in range(DEPTH):
        p = layers[l]
        sh1, sc1, g1, sh2, sc2, g2 = modulation(cond, p['w_mod'], p['b_mod'])
        h = rmsnorm(x, p['norm1']) * (1 + sc1) + sh1
        if l % 2 == 0:
            mix, ctx_kv = attn_conv_mixer(h, p, caches[l], rope)
            new_caches.append(ctx_kv)
        else:
            mix = fourier_mixer(h, p['w_f'])
        x = x + g1 * mix
        h = rmsnorm(x, p['norm2']) * (1 + sc2) + sh2
        x = x + g2 * expert_choice_ffn(h, p['w_router'], p['w_gate'], p['w_up'], p['w_down'])
    return x, new_caches


def setup_inputs(seed: int = 0) -> dict:
    key = jax.random.key(seed)
    ks = iter(jax.random.split(key, 40))
    f32 = jnp.float32

    def nrm(shape, scale):
        return jax.random.normal(next(ks), shape, f32) * scale

    def gain(n):
        return 1.0 + 0.05 * jax.random.normal(next(ks), (n,), f32)

    inp = {}
    inp['x_prompt'] = nrm((BATCH, SEQ, D_MODEL), 1.0)
    inp['x_sample'] = nrm((DEC_BATCH, DEC_SEQ, D_MODEL), 1.0)
    inp['c'] = nrm((DEC_BATCH, D_MODEL), 1.0)
    inp['cache_c_kv_l0'] = nrm((DEC_BATCH, PAST_LEN, KV_LORA), 1.0)
    inp['cache_k_rope_l0'] = nrm((DEC_BATCH, PAST_LEN, QK_ROPE), 1.0)
    inp['c_ctx'] = nrm((D_MODEL,), 1.0)
    inp['norm1_l0'] = gain(D_MODEL)
    inp['norm2_l0'] = gain(D_MODEL)
    inp['w_mod_l0'] = nrm((D_MODEL, 6 * D_MODEL), D_MODEL ** -0.5)
    inp['b_mod_l0'] = nrm((6 * D_MODEL,), 0.02)
    inp['w_in_l0'] = nrm((D_MODEL, IN0_COLS), D_MODEL ** -0.5)
    inp['q_a_norm_l0'] = gain(Q_LORA)
    inp['w_q_up_l0'] = nrm((Q_LORA, N_HEADS * QK_DIM), Q_LORA ** -0.5)
    inp['q_norm_l0'] = gain(QK_DIM)
    inp['kv_a_norm_l0'] = gain(KV_LORA)
    inp['w_kv_up_l0'] = nrm((KV_LORA, N_HEADS * (QK_NOPE + V_HEAD)), KV_LORA ** -0.5)
    inp['k_norm_l0'] = gain(QK_DIM)
    inp['conv_w_l0'] = nrm((CONV_K, CONV_CH), CONV_K ** -0.5)
    inp['w_o_l0'] = nrm((MIX0_OUT, D_MODEL), MIX0_OUT ** -0.5)
    inp['w_router_l0'] = nrm((D_MODEL, N_EXPERTS), D_MODEL ** -0.5)
    inp['w_gate_l0'] = nrm((N_EXPERTS, D_MODEL, EXPERT_FF), D_MODEL ** -0.5)
    inp['w_up_l0'] = nrm((N_EXPERTS, D_MODEL, EXPERT_FF), D_MODEL ** -0.5)
    inp['w_down_l0'] = nrm((N_EXPERTS, EXPERT_FF, D_MODEL), EXPERT_FF ** -0.5)
    inp['norm1_l1'] = gain(D_MODEL)
    inp['norm2_l1'] = gain(D_MODEL)
    inp['w_mod_l1'] = nrm((D_MODEL, 6 * D_MODEL), D_MODEL ** -0.5)
    inp['b_mod_l1'] = nrm((6 * D_MODEL,), 0.02)
    inp['w_f_l1'] = nrm((D_MODEL, D_MODEL), D_MODEL ** -0.5)
    inp['w_router_l1'] = nrm((D_MODEL, N_EXPERTS), D_MODEL ** -0.5)
    inp['w_gate_l1'] = nrm((N_EXPERTS, D_MODEL, EXPERT_FF), D_MODEL ** -0.5)
    inp['w_up_l1'] = nrm((N_EXPERTS, D_MODEL, EXPERT_FF), D_MODEL ** -0.5)
    inp['w_down_l1'] = nrm((N_EXPERTS, EXPERT_FF, D_MODEL), EXPERT_FF ** -0.5)
    return inp


def reference(x_prompt, x_sample, c, cache_c_kv_l0, cache_k_rope_l0, c_ctx,
              norm1_l0, norm2_l0, w_mod_l0, b_mod_l0, w_in_l0, q_a_norm_l0, w_q_up_l0, q_norm_l0,
              kv_a_norm_l0, w_kv_up_l0, k_norm_l0, conv_w_l0, w_o_l0,
              w_router_l0, w_gate_l0, w_up_l0, w_down_l0,
              norm1_l1, norm2_l1, w_mod_l1, b_mod_l1, w_f_l1,
              w_router_l1, w_gate_l1, w_up_l1, w_down_l1):
    layers = [
        dict(norm1=norm1_l0, norm2=norm2_l0, w_mod=w_mod_l0, b_mod=b_mod_l0, w_in=w_in_l0,
             q_a_norm=q_a_norm_l0, w_q_up=w_q_up_l0, q_norm=q_norm_l0, kv_a_norm=kv_a_norm_l0,
             w_kv_up=w_kv_up_l0, k_norm=k_norm_l0, conv_w=conv_w_l0, w_o=w_o_l0,
             w_router=w_router_l0, w_gate=w_gate_l0, w_up=w_up_l0, w_down=w_down_l0),
        dict(norm1=norm1_l1, norm2=norm2_l1, w_mod=w_mod_l1, b_mod=b_mod_l1, w_f=w_f_l1,
             w_router=w_router_l1, w_gate=w_gate_l1, w_up=w_up_l1, w_down=w_down_l1),
    ]
    y_prompt, ctx_state = trunk(x_prompt, c_ctx[None, :], layers, [None, None], None)
    new_c_kv_l0, new_k_rope_l0 = ctx_state[0]
    rope = axial_rope(x_sample.shape[1])
    y_sample, _ = trunk(x_sample, c, layers, [(cache_c_kv_l0, cache_k_rope_l0), None], rope)
    return (y_prompt, y_sample, new_c_kv_l0, new_k_rope_l0)
```

```python
import functools

import jax
import jax.numpy as jnp
import numpy as np
from jax import lax
from jax.experimental import pallas as pl
from jax.experimental.pallas import tpu as pltpu

D_MODEL = 1024
GRID_W = 64
N_HEADS = 8
QK_NOPE = 64
QK_ROPE = 32
QK_DIM = QK_NOPE + QK_ROPE
V_HEAD = 64
Q_LORA = 384
KV_LORA = 256
CONV_CH = 512
FNET_GROUPS = 4
FNET_CH = D_MODEL // FNET_GROUPS
N_EXPERTS = 16
EXPERT_FF = 512
CAPACITY_FACTOR = 2
ROPE_THETA = 10000.0
EPS = 1e-6

LANES = 128
HEAD_PAD = LANES
ROWS = 256
IN0_PAD = Q_LORA + KV_LORA + LANES + 3 * CONV_CH
ROPE_LANE0 = QK_NOPE
VMEM_LIMIT = 56 * 1024 * 1024

F32 = jnp.float32
BF16 = jnp.bfloat16


def _dot(a, b):
    return jnp.dot(a, b, preferred_element_type=F32)


def _dot_nt(a, b):
    return lax.dot_general(a, b, (((1,), (1,)), ((), ())), preferred_element_type=F32)


def _split_hi_lo(x):
    hi = x.astype(BF16)
    lo = (x - hi.astype(F32)).astype(BF16)
    return hi, lo


def _params(n_axes):
    return pltpu.CompilerParams(dimension_semantics=("arbitrary",) * n_axes,
                                vmem_limit_bytes=VMEM_LIMIT)


def _rms(x, gain):
    return x * lax.rsqrt(jnp.mean(x * x, axis=-1, keepdims=True) + EPS) * gain


def _mod_kernel(cond_ref, w_ref, b_ref, o_ref):
    c = cond_ref[...]
    s = c * (1.0 / (1.0 + jnp.exp(-c)))
    s_hi, s_lo = _split_hi_lo(s)
    w_hi, w_lo = _split_hi_lo(w_ref[...])
    o_ref[...] = _dot(s_hi, w_hi) + _dot(s_lo, w_hi) + _dot(s_hi, w_lo) + b_ref[...]


def _modulation(cond, w_mod, b_mod):
    n_rows = cond.shape[0]
    tn = 1536
    out = pl.pallas_call(
        _mod_kernel,
        out_shape=jax.ShapeDtypeStruct((n_rows, 6 * D_MODEL), F32),
        grid=(6 * D_MODEL // tn,),
        in_specs=[pl.BlockSpec((n_rows, D_MODEL), lambda i: (0, 0)),
                  pl.BlockSpec((D_MODEL, tn), lambda i: (0, i)),
                  pl.BlockSpec((1, tn), lambda i: (0, i))],
        out_specs=pl.BlockSpec((n_rows, tn), lambda i: (0, i)),
        compiler_params=_params(1),
        name="modulation",
    )(cond, w_mod, b_mod.reshape(1, -1))
    return out.reshape(n_rows, 6, D_MODEL)


def _swap_pairs(x):
    lane = lax.broadcasted_iota(jnp.int32, x.shape, 1)
    return jnp.where((lane & 1) == 0, pltpu.roll(x, LANES - 1, 1), pltpu.roll(x, 1, 1))


def _head_norm_rope(xh, gain, cos, sin):
    ss = jnp.sum(xh * xh, axis=-1, keepdims=True) * (1.0 / QK_DIM)
    y = xh * lax.rsqrt(ss + EPS) * gain
    if cos is not None:
        y = y * cos + _swap_pairs(y) * sin
    return y


def _expand_kv(ckv_bf, r, wk_ref, wv_ref, kn_ref, cos, sin, k_ref, v_ref, rows):
    kf = _dot(ckv_bf, wk_ref[...])
    v_ref[rows, :] = _dot(ckv_bf, wv_ref[...]).astype(BF16)
    for h in range(N_HEADS):
        sl = slice(h * HEAD_PAD, (h + 1) * HEAD_PAD)
        kh = _head_norm_rope(kf[:, sl] + r, kn_ref[:, sl], cos, sin)
        k_ref[rows, sl] = kh.astype(BF16)


def _front0_kernel(*refs, n_tok, rope):
    (x_ref, mod_ref, n1_ref, win_ref, qan_ref, wq_ref, qn_ref, kvan_ref, wk_ref, wv_ref,
     kn_ref, cw_ref) = refs[:12]
    pos = 12
    if rope:
        cos_ref, sin_ref = refs[pos:pos + 2]
        pos += 2
    q_ref, k_ref, v_ref, conv_ref, ckv_ref, kr_ref, cu_s, gb_s = refs[pos:]
    sh1, sc1 = mod_ref[0:1, :], mod_ref[1:2, :]
    for c in range(n_tok // ROWS):
        rows = pl.ds(c * ROWS, ROWS)
        cos = cos_ref[rows, :] if rope else None
        sin = sin_ref[rows, :] if rope else None
        h = _rms(x_ref[rows, :], n1_ref[...]) * (1.0 + sc1) + sh1
        proj = _dot(h.astype(BF16), win_ref[...])
        cq = _rms(proj[:, :Q_LORA], qan_ref[...])
        qf = _dot(cq.astype(BF16), wq_ref[...])
        for hd in range(N_HEADS):
            sl = slice(hd * HEAD_PAD, (hd + 1) * HEAD_PAD)
            q_ref[rows, sl] = _head_norm_rope(qf[:, sl], qn_ref[:, sl], cos, sin).astype(BF16)
        ckv = _rms(proj[:, Q_LORA:Q_LORA + KV_LORA], kvan_ref[...])
        ckv_ref[rows, :] = ckv
        r = proj[:, Q_LORA + KV_LORA:Q_LORA + KV_LORA + LANES]
        kr_ref[rows, :] = r
        _expand_kv(ckv.astype(BF16), r, wk_ref, wv_ref, kn_ref, cos, sin, k_ref, v_ref, rows)
        c0 = Q_LORA + KV_LORA + LANES
        gb_s[rows, :] = proj[:, c0:c0 + CONV_CH]
        cu_s[rows, :] = proj[:, c0 + CONV_CH:c0 + 2 * CONV_CH] * proj[:, c0 + 2 * CONV_CH:c0 + 3 * CONV_CH]
    cu = cu_s[...]
    row = lax.broadcasted_iota(jnp.int32, cu.shape, 0)
    prev = jnp.where(row == 0, 0.0, pltpu.roll(cu, 1, 0))
    nxt = jnp.where(row == n_tok - 1, 0.0, pltpu.roll(cu, n_tok - 1, 0))
    conv = gb_s[...] * (cw_ref[0:1, :] * prev + cw_ref[1:2, :] * cu + cw_ref[2:3, :] * nxt)
    conv_ref[...] = conv.astype(BF16)


def _const_spec(shape):
    return pl.BlockSpec(shape, lambda b: (0,) * len(shape))


def _front0(x, mod, w, rope_tabs):
    n_b, n_tok, _ = x.shape
    shared_mod = mod.shape[0] == 1
    seq = lambda width: pl.BlockSpec((None, n_tok, width), lambda b: (b, 0, 0))
    in_specs = [seq(D_MODEL),
                pl.BlockSpec((None, 6, D_MODEL), (lambda b: (0, 0, 0)) if shared_mod else (lambda b: (b, 0, 0))),
                _const_spec((1, D_MODEL)), _const_spec((D_MODEL, IN0_PAD)), _const_spec((1, Q_LORA)),
                _const_spec((Q_LORA, N_HEADS * HEAD_PAD)), _const_spec((1, N_HEADS * HEAD_PAD)),
                _const_spec((1, KV_LORA)), _const_spec((KV_LORA, N_HEADS * HEAD_PAD)),
                _const_spec((KV_LORA, N_HEADS * V_HEAD)), _const_spec((1, N_HEADS * HEAD_PAD)),
                _const_spec((3, CONV_CH))]
    args = [x, mod, w["norm1"], w["w_in"], w["q_a_norm"], w["w_q"], w["q_norm"], w["kv_a_norm"],
            w["w_k"], w["w_v"], w["k_norm"], w["conv_w"]]
    if rope_tabs is not None:
        in_specs += [_const_spec((n_tok, LANES))] * 2
        args += list(rope_tabs)
    out_shape = [jax.ShapeDtypeStruct((n_b, n_tok, N_HEADS * HEAD_PAD), BF16),
                 jax.ShapeDtypeStruct((n_b, n_tok, N_HEADS * HEAD_PAD), BF16),
                 jax.ShapeDtypeStruct((n_b, n_tok, N_HEADS * V_HEAD), BF16),
                 jax.ShapeDtypeStruct((n_b, n_tok, CONV_CH), BF16),
                 jax.ShapeDtypeStruct((n_b, n_tok, KV_LORA), F32),
                 jax.ShapeDtypeStruct((n_b, n_tok, LANES), F32)]
    out_specs = [seq(N_HEADS * HEAD_PAD), seq(N_HEADS * HEAD_PAD), seq(N_HEADS * V_HEAD), seq(CONV_CH),
                 seq(KV_LORA), seq(LANES)]
    return pl.pallas_call(
        functools.partial(_front0_kernel, n_tok=n_tok, rope=rope_tabs is not None),
        out_shape=out_shape, grid=(n_b,), in_specs=in_specs, out_specs=out_specs,
        scratch_shapes=[pltpu.VMEM((n_tok, CONV_CH), F32), pltpu.VMEM((n_tok, CONV_CH), F32)],
        compiler_params=_params(1), name="front0",
    )(*args)


def _ctx_kv_kernel(ckv_ref, r_ref, wk_ref, wv_ref, kn_ref, k_ref, v_ref):
    rows = pl.ds(0, ckv_ref.shape[0])
    _expand_kv(ckv_ref[...].astype(BF16), r_ref[...], wk_ref, wv_ref, kn_ref, None, None, k_ref, v_ref, rows)


def _ctx_kv(cache_c_kv, cache_k_rope_pad, w):
    n_b, n_ctx, _ = cache_c_kv.shape
    seq = lambda width: pl.BlockSpec((None, n_ctx, width), lambda b: (b, 0, 0))
    return pl.pallas_call(
        _ctx_kv_kernel,
        out_shape=[jax.ShapeDtypeStruct((n_b, n_ctx, N_HEADS * HEAD_PAD), BF16),
                   jax.ShapeDtypeStruct((n_b, n_ctx, N_HEADS * V_HEAD), BF16)],
        grid=(n_b,),
        in_specs=[seq(KV_LORA), seq(LANES), _const_spec((KV_LORA, N_HEADS * HEAD_PAD)),
                  _const_spec((KV_LORA, N_HEADS * V_HEAD)), _const_spec((1, N_HEADS * HEAD_PAD))],
        out_specs=[seq(N_HEADS * HEAD_PAD), seq(N_HEADS * V_HEAD)],
        compiler_params=_params(1), name="ctx_kv",
    )(cache_c_kv, cache_k_rope_pad, w["w_k"], w["w_v"], w["k_norm"])


def _attn_kernel(*refs, with_ctx):
    if with_ctx:
        q_ref, k_ref, v_ref, kc_ref, vc_ref, o_ref = refs
    else:
        q_ref, k_ref, v_ref, o_ref = refs
    lane = lax.broadcasted_iota(jnp.int32, (q_ref.shape[0], LANES), 1)
    for pair in range(N_HEADS // 2):
        vsl = slice(pair * LANES, (pair + 1) * LANES)
        outs = []
        for hd in (2 * pair, 2 * pair + 1):
            sl = slice(hd * HEAD_PAD, (hd + 1) * HEAD_PAD)
            qh = q_ref[:, sl]
            s = _dot_nt(qh, k_ref[:, sl])
            m = jnp.max(s, axis=-1, keepdims=True)
            if with_ctx:
                sc = _dot_nt(qh, kc_ref[:, sl])
                m = jnp.maximum(m, jnp.max(sc, axis=-1, keepdims=True))
                pc = jnp.exp(sc - m)
            p = jnp.exp(s - m)
            den = jnp.sum(p, axis=-1, keepdims=True)
            o = _dot(p.astype(BF16), v_ref[:, vsl])
            if with_ctx:
                den = den + jnp.sum(pc, axis=-1, keepdims=True)
                o = o + _dot(pc.astype(BF16), vc_ref[:, vsl])
            outs.append(o / den)
        o_ref[:, vsl] = jnp.where(lane < V_HEAD, outs[0], outs[1]).astype(BF16)


def _attention(q, k, v, kc=None, vc=None):
    n_b, n_tok, _ = q.shape
    with_ctx = kc is not None
    qspec = lambda width: pl.BlockSpec((None, ROWS, width), lambda b, i: (b, i, 0))
    kvspec = lambda n, width: pl.BlockSpec((None, n, width), lambda b, i: (b, 0, 0))
    in_specs = [qspec(N_HEADS * HEAD_PAD), kvspec(n_tok, N_HEADS * HEAD_PAD), kvspec(n_tok, N_HEADS * V_HEAD)]
    args = [q, k, v]
    if with_ctx:
        n_ctx = kc.shape[1]
        in_specs += [kvspec(n_ctx, N_HEADS * HEAD_PAD), kvspec(n_ctx, N_HEADS * V_HEAD)]
        args += [kc, vc]
    return pl.pallas_call(
        functools.partial(_attn_kernel, with_ctx=with_ctx),
        out_shape=jax.ShapeDtypeStruct((n_b, n_tok, N_HEADS * V_HEAD), BF16),
        grid=(n_b, n_tok // ROWS), in_specs=in_specs, out_specs=qspec(N_HEADS * V_HEAD),
        compiler_params=_params(2), name="attention",
    )(*args)


def _moe_front(x1, mod_ref, n2_ref, wrh_ref, wrl_ref):
    sh2, sc2 = mod_ref[3:4, :], mod_ref[4:5, :]
    h2 = _rms(x1, n2_ref[...]) * (1.0 + sc2) + sh2
    h_hi, h_lo = _split_hi_lo(h2)
    logit = _dot_nt(wrh_ref[...], h_hi) + _dot_nt(wrh_ref[...], h_lo) + _dot_nt(wrl_ref[...], h_hi)
    e = jnp.exp(logit - jnp.max(logit, axis=0, keepdims=True))
    return h_hi, e / jnp.sum(e, axis=0, keepdims=True)


def _post0_kernel(x_ref, attn_ref, conv_ref, wo_ref, mod_ref, n2_ref, wrh_ref, wrl_ref,
                  x1_ref, h2_ref, aff_ref):
    n_attn = N_HEADS * V_HEAD
    mix = _dot(attn_ref[...], wo_ref[0:n_attn, :]) + _dot(conv_ref[...], wo_ref[n_attn:, :])
    x1 = x_ref[...] + mod_ref[2:3, :] * mix
    x1_ref[...] = x1
    h2, aff = _moe_front(x1, mod_ref, n2_ref, wrh_ref, wrl_ref)
    h2_ref[...] = h2
    aff_ref[...] = aff


def _post0(x, attn, conv, mod, w, n_tok):
    n_all = x.shape[0]
    tiles_per_seq = n_tok // ROWS
    shared_mod = mod.shape[0] == 1
    tile = lambda width: pl.BlockSpec((ROWS, width), lambda i: (i, 0))
    const = lambda shape: pl.BlockSpec(shape, lambda i: (0,) * len(shape))
    mod_map = (lambda i: (0, 0, 0)) if shared_mod else (lambda i: (i // tiles_per_seq, 0, 0))
    return pl.pallas_call(
        _post0_kernel,
        out_shape=[jax.ShapeDtypeStruct((n_all, D_MODEL), F32),
                   jax.ShapeDtypeStruct((n_all, D_MODEL), BF16),
                   jax.ShapeDtypeStruct((n_all // ROWS, N_EXPERTS, ROWS), F32)],
        grid=(n_all // ROWS,),
        in_specs=[tile(D_MODEL), tile(N_HEADS * V_HEAD), tile(CONV_CH), const((D_MODEL, D_MODEL)),
                  pl.BlockSpec((None, 6, D_MODEL), mod_map), const((1, D_MODEL)),
                  const((N_EXPERTS, D_MODEL)), const((N_EXPERTS, D_MODEL))],
        out_specs=[tile(D_MODEL), tile(D_MODEL), pl.BlockSpec((None, N_EXPERTS, ROWS), lambda i: (i, 0, 0))],
        compiler_params=_params(1), name="post0",
    )(x, attn, conv, w["w_o"], mod, w["norm2"], w["wr_hi"], w["wr_lo"])


def _fnet_kernel(x_ref, mod_ref, n1_ref, cc_ref, sc_ref, dft_ref, wf_ref, n2_ref, wrh_ref, wrl_ref,
                 x1_ref, h2_ref, aff_ref, y_s, *, n_tok):
    sh1, sc1, g1 = mod_ref[0:1, :], mod_ref[1:2, :], mod_ref[2:3, :]
    n_chunks = n_tok // ROWS
    cc, sc = cc_ref[...].astype(BF16), sc_ref[...].astype(BF16)
    for c in range(n_chunks):
        rows = pl.ds(c * ROWS, ROWS)
        h = (_rms(x_ref[rows, :], n1_ref[...]) * (1.0 + sc1) + sh1).astype(BF16)
        for g in range(FNET_GROUPS):
            sl = slice(g * FNET_CH, (g + 1) * FNET_CH)
            y_s[pl.ds(c * ROWS, ROWS), sl] = _dot(h[:, sl], cc).astype(BF16)
            y_s[pl.ds(n_tok + c * ROWS, ROWS), sl] = _dot(h[:, sl], sc).astype(BF16)
    scale = 1.0 / float(np.sqrt(n_tok * FNET_CH))
    for c in range(n_chunks):
        rows = pl.ds(c * ROWS, ROWS)
        f = _dot(dft_ref[rows, :].astype(BF16), y_s[...]) * scale
        x1 = x_ref[rows, :] + g1 * _dot(f.astype(BF16), wf_ref[...])
        x1_ref[rows, :] = x1
        h2, aff = _moe_front(x1, mod_ref, n2_ref, wrh_ref, wrl_ref)
        h2_ref[rows, :] = h2
        aff_ref[c] = aff


def _fnet(x, mod, w, dft):
    n_b, n_tok, _ = x.shape
    shared_mod = mod.shape[0] == 1
    seq = lambda width: pl.BlockSpec((None, n_tok, width), lambda b: (b, 0, 0))
    cc, sc, dft_n = dft
    tiles = n_tok // ROWS
    return pl.pallas_call(
        functools.partial(_fnet_kernel, n_tok=n_tok),
        out_shape=[jax.ShapeDtypeStruct((n_b, n_tok, D_MODEL), F32),
                   jax.ShapeDtypeStruct((n_b, n_tok, D_MODEL), BF16),
                   jax.ShapeDtypeStruct((n_b * tiles, N_EXPERTS, ROWS), F32)],
        grid=(n_b,),
        in_specs=[seq(D_MODEL),
                  pl.BlockSpec((None, 6, D_MODEL), (lambda b: (0, 0, 0)) if shared_mod else (lambda b: (b, 0, 0))),
                  _const_spec((1, D_MODEL)), _const_spec((FNET_CH, FNET_CH)), _const_spec((FNET_CH, FNET_CH)),
                  pl.BlockSpec((n_tok, 2 * n_tok), lambda b: (0, 0), pipeline_mode=pl.Buffered(1)),
                  _const_spec((D_MODEL, D_MODEL)), _const_spec((1, D_MODEL)),
                  _const_spec((N_EXPERTS, D_MODEL)), _const_spec((N_EXPERTS, D_MODEL))],
        out_specs=[seq(D_MODEL), seq(D_MODEL), pl.BlockSpec((tiles, N_EXPERTS, ROWS), lambda b: (b, 0, 0))],
        scratch_shapes=[pltpu.VMEM((2 * n_tok, D_MODEL), BF16)],
        compiler_params=_params(1), name="fnet",
    )(x, mod, w["norm1"], cc, sc, dft_n, w["w_f"], w["norm2"], w["wr_hi"], w["wr_lo"])


def _route_kernel(aff_ref, tri_ref, posm_ref, s_ref, *, n_blk, cap):
    def count(pred):
        acc = jnp.zeros((N_EXPERTS, ROWS), F32)
        for b in range(n_blk):
            acc = acc + jnp.where(pred(aff_ref[b]), 1.0, 0.0)
        return jnp.sum(acc, axis=1, keepdims=True)

    def as_f32(bits):
        return pltpu.bitcast(bits, F32)

    def search(_, carry):
        lo, hi = carry
        mid = lo + ((hi - lo) >> 1)
        mid_f = as_f32(mid)
        ok = count(lambda a: a >= mid_f) >= cap
        return jnp.where(ok, mid, lo), jnp.where(ok, hi, mid)

    one_bits = 0x3F800000
    lo0 = jnp.zeros((N_EXPERTS, 1), jnp.int32)
    hi0 = jnp.full((N_EXPERTS, 1), one_bits + 1, jnp.int32)
    lo, hi = lax.fori_loop(0, 31, search, (lo0, hi0))
    lo_f, ub = as_f32(lo), as_f32(hi)
    thr = lo_f
    pending = jnp.ones((N_EXPERTS, 1), F32)
    for _ in range(3):
        cur = jnp.full((N_EXPERTS, ROWS), -1.0, F32)
        for b in range(n_blk):
            a = aff_ref[b]
            cur = jnp.maximum(cur, jnp.where((a >= lo_f) & (a < ub), a, -1.0))
        cur = jnp.max(cur, axis=1, keepdims=True)
        take = (count(lambda a: a >= cur) >= cap) & (pending > 0.0)
        thr = jnp.where(take, cur, thr)
        pending = jnp.where(take, 0.0, pending)
        ub = cur
    need = cap - count(lambda a: a > thr)

    lane = lax.broadcasted_iota(jnp.int32, (N_EXPERTS, LANES), 1)
    carry_tie = jnp.zeros((N_EXPERTS, 1), F32)
    carry_pos = jnp.zeros((N_EXPERTS, 1), F32)
    s_acc = jnp.zeros((N_EXPERTS, LANES), jnp.int32)
    for b in range(n_blk):
        v = aff_ref[b]
        eq = jnp.where(v == thr, 1.0, 0.0)
        tie_rank = _dot(eq.astype(BF16), tri_ref[...]) + carry_tie
        sel = jnp.where((v > thr) | ((v == thr) & (tie_rank < need)), 1.0, 0.0)
        pos = _dot(sel.astype(BF16), tri_ref[...]) + carry_pos
        posm_ref[b] = jnp.where(sel > 0.0, pos, -1.0)
        s_acc = jnp.where(lane == b, carry_pos.astype(jnp.int32), s_acc)
        carry_tie = carry_tie + jnp.sum(eq, axis=1, keepdims=True)
        carry_pos = carry_pos + jnp.sum(sel, axis=1, keepdims=True)
    s_ref[...] = jnp.where(lane == n_blk, carry_pos.astype(jnp.int32), s_acc)


def _route(aff, tri, cap):
    n_blk = aff.shape[0]
    assert n_blk < LANES
    full = lambda shape: pl.BlockSpec(shape, lambda i: (0,) * len(shape))
    posm, s_tab = pl.pallas_call(
        functools.partial(_route_kernel, n_blk=n_blk, cap=cap),
        out_shape=[jax.ShapeDtypeStruct((n_blk, N_EXPERTS, ROWS), F32),
                   jax.ShapeDtypeStruct((N_EXPERTS, LANES), jnp.int32)],
        grid=(1,),
        in_specs=[full((n_blk, N_EXPERTS, ROWS)), full((ROWS, ROWS))],
        out_specs=[full((n_blk, N_EXPERTS, ROWS)), full((N_EXPERTS, LANES))],
        compiler_params=_params(1), name="route",
    )(aff, tri)
    return posm, s_tab[:, :n_blk + 1]


def _band(s_ref, e, j, n_blk):
    def count_le(x):
        return lax.fori_loop(1, n_blk + 1, lambda b, c: c + jnp.where(s_ref[e, b] <= x, 1, 0), 0)
    return count_le(j * ROWS), count_le(j * ROWS + ROWS - 1)


def _slot_onehot(posm_ref, b, e, j):
    slot = (lax.broadcasted_iota(jnp.int32, (ROWS, ROWS), 0) + j * ROWS).astype(F32)
    return posm_ref[b, pl.ds(e, 1), :] == slot


def _ffn_up_kernel(s_ref, h_ref, posm_ref, aff_ref, wg_ref, wu_ref, hid_ref, x_s, g_s, *, n_blk):
    e, j = pl.program_id(0), pl.program_id(1)
    b_lo, b_hi = _band(s_ref, e, j, n_blk)
    x_s[...] = jnp.zeros_like(x_s)
    g_s[...] = jnp.zeros_like(g_s)
    sub = lax.broadcasted_iota(jnp.int32, (16, ROWS), 0)

    def gather(b, _):
        p = jnp.where(_slot_onehot(posm_ref, b, e, j), 1.0, 0.0).astype(BF16)
        x_s[...] += _dot(p, h_ref[pl.ds(pl.multiple_of(b * ROWS, ROWS), ROWS), :])
        a = aff_ref[b, pl.ds(e, 1), :]
        a_hi = a.astype(BF16).astype(F32)
        a_mid = (a - a_hi).astype(BF16).astype(F32)
        a_lo = a - a_hi - a_mid
        parts = jnp.where(sub == 0, a_hi, jnp.where(sub == 1, a_mid, jnp.where(sub == 2, a_lo, 0.0)))
        g_s[...] += _dot_nt(p, parts.astype(BF16))
        return 0

    lax.fori_loop(b_lo, b_hi + 1, gather, 0)
    x = x_s[...].astype(BF16)
    a = _dot(x, wg_ref[...])
    u = _dot(x, wu_ref[...])
    gate = g_s[:, 0:1] + g_s[:, 1:2] + g_s[:, 2:3]
    hid_ref[...] = (a * (1.0 / (1.0 + jnp.exp(-a))) * u * gate).astype(BF16)


def _ffn_down_kernel(s_ref, hid_ref, posm_ref, wd_ref, out_ref, *, n_blk):
    e, j = pl.program_id(1), pl.program_id(2)

    @pl.when((e == 0) & (j == 0))
    def _():
        out_ref[...] = jnp.zeros_like(out_ref)

    b_lo, b_hi = _band(s_ref, e, j, n_blk)
    y = _dot(hid_ref[...], wd_ref[...]).astype(BF16)

    def scatter(b, _):
        p = jnp.where(_slot_onehot(posm_ref, b, e, j), 1.0, 0.0)
        rows = pl.ds(pl.multiple_of(b * ROWS, ROWS), ROWS)
        out_ref[rows, :] += _dot(p.T.astype(BF16), y)
        return 0

    lax.fori_loop(b_lo, b_hi + 1, scatter, 0)


def _expert_ffn(h2, posm, aff, s_tab, wg, wu, wd, cap):
    n_all = h2.shape[0]
    n_blk = n_all // ROWS
    tiles = cap // ROWS
    half = D_MODEL // 2
    hid = pl.pallas_call(
        functools.partial(_ffn_up_kernel, n_blk=n_blk),
        out_shape=jax.ShapeDtypeStruct((N_EXPERTS, cap, EXPERT_FF), BF16),
        grid_spec=pltpu.PrefetchScalarGridSpec(
            num_scalar_prefetch=1, grid=(N_EXPERTS, tiles),
            in_specs=[pl.BlockSpec((n_all, D_MODEL), lambda e, j, s: (0, 0), pipeline_mode=pl.Buffered(1)),
                      pl.BlockSpec((n_blk, N_EXPERTS, ROWS), lambda e, j, s: (0, 0, 0)),
                      pl.BlockSpec((n_blk, N_EXPERTS, ROWS), lambda e, j, s: (0, 0, 0)),
                      pl.BlockSpec((None, D_MODEL, EXPERT_FF), lambda e, j, s: (e, 0, 0)),
                      pl.BlockSpec((None, D_MODEL, EXPERT_FF), lambda e, j, s: (e, 0, 0))],
            out_specs=pl.BlockSpec((None, ROWS, EXPERT_FF), lambda e, j, s: (e, j, 0)),
            scratch_shapes=[pltpu.VMEM((ROWS, D_MODEL), F32), pltpu.VMEM((ROWS, 16), F32)]),
        compiler_params=_params(2), name="ffn_up",
    )(s_tab, h2, posm, aff, wg, wu)
    return pl.pallas_call(
        functools.partial(_ffn_down_kernel, n_blk=n_blk),
        out_shape=jax.ShapeDtypeStruct((n_all, D_MODEL), F32),
        grid_spec=pltpu.PrefetchScalarGridSpec(
            num_scalar_prefetch=1, grid=(2, N_EXPERTS, tiles),
            in_specs=[pl.BlockSpec((None, ROWS, EXPERT_FF), lambda c, e, j, s: (e, j, 0)),
                      pl.BlockSpec((n_blk, N_EXPERTS, ROWS), lambda c, e, j, s: (0, 0, 0)),
                      pl.BlockSpec((None, EXPERT_FF, half), lambda c, e, j, s: (e, 0, c))],
            out_specs=pl.BlockSpec((n_all, half), lambda c, e, j, s: (0, c))),
        compiler_params=_params(3), name="ffn_down",
    )(s_tab, hid, posm, wd)


def _resid_kernel(x_ref, moe_ref, mod_ref, o_ref):
    o_ref[...] = x_ref[...] + mod_ref[5:6, :] * moe_ref[...]


def _resid(x1, moe, mod, n_tok):
    n_all = x1.shape[0]
    tiles_per_seq = n_tok // ROWS
    shared_mod = mod.shape[0] == 1
    tile = pl.BlockSpec((ROWS, D_MODEL), lambda i: (i, 0))
    mod_map = (lambda i: (0, 0, 0)) if shared_mod else (lambda i: (i // tiles_per_seq, 0, 0))
    return pl.pallas_call(
        _resid_kernel, out_shape=jax.ShapeDtypeStruct((n_all, D_MODEL), F32), grid=(n_all // ROWS,),
        in_specs=[tile, tile, pl.BlockSpec((None, 6, D_MODEL), mod_map)], out_specs=tile,
        compiler_params=_params(1), name="resid",
    )(x1, moe, mod)


def _rope_tables(n_tok):
    rows = n_tok // GRID_W
    row = np.repeat(np.arange(rows, dtype=np.float64), GRID_W)
    col = np.tile(np.arange(GRID_W, dtype=np.float64), rows)
    axis_dim = QK_ROPE // 2
    inv_freq = ROPE_THETA ** (-np.arange(0, axis_dim, 2, dtype=np.float64) / axis_dim)
    ang = np.concatenate([row[:, None] * inv_freq, col[:, None] * inv_freq], axis=-1)
    cos = np.ones((n_tok, LANES))
    sin = np.zeros((n_tok, LANES))
    cos[:, ROPE_LANE0:ROPE_LANE0 + QK_ROPE] = np.repeat(np.cos(ang), 2, axis=1)
    sgn = np.tile(np.array([-1.0, 1.0]), QK_ROPE // 2)
    sin[:, ROPE_LANE0:ROPE_LANE0 + QK_ROPE] = np.repeat(np.sin(ang), 2, axis=1) * sgn
    return jnp.asarray(cos, F32), jnp.asarray(sin, F32)


def _dft_tables(n_tok):
    def cs(n):
        k = np.arange(n)
        ang = 2.0 * np.pi * ((k[:, None] * k[None, :]) % n) / n
        return np.cos(ang), np.sin(ang)
    cc, sc = cs(FNET_CH)
    cn, sn = cs(n_tok)
    return tuple(jnp.asarray(t, F32) for t in (cc, sc, np.concatenate([cn, -sn], axis=1)))


def _pad_heads(w, width):
    lead = w.shape[:-1]
    w = w.reshape(lead + (N_HEADS, width))
    w = jnp.pad(w, [(0, 0)] * len(lead) + [(0, 0), (0, HEAD_PAD - width)])
    return w.reshape(lead + (N_HEADS * HEAD_PAD,))


def _layer0_weights(norm1, norm2, w_in, q_a_norm, w_q_up, q_norm, kv_a_norm, w_kv_up, k_norm, conv_w, w_o):
    c0 = Q_LORA + KV_LORA
    rope_cols = jnp.pad(w_in[:, c0:c0 + QK_ROPE], ((0, 0), (ROPE_LANE0, LANES - ROPE_LANE0 - QK_ROPE)))
    w_in_pad = jnp.concatenate([w_in[:, :c0], rope_cols, w_in[:, c0 + QK_ROPE:]], axis=1)
    kv = w_kv_up.reshape(KV_LORA, N_HEADS, QK_NOPE + V_HEAD)
    head_gain = lambda g: jnp.tile(jnp.pad(g, (0, HEAD_PAD - QK_DIM)), N_HEADS).reshape(1, -1)
    return dict(
        norm1=norm1.reshape(1, -1), norm2=norm2.reshape(1, -1), w_in=w_in_pad.astype(BF16),
        q_a_norm=q_a_norm.reshape(1, -1), w_q=_pad_heads(w_q_up, QK_DIM).astype(BF16),
        q_norm=head_gain(q_norm) * (QK_DIM ** -0.5), kv_a_norm=kv_a_norm.reshape(1, -1),
        w_k=_pad_heads(kv[:, :, :QK_NOPE].reshape(KV_LORA, -1), QK_NOPE).astype(BF16),
        w_v=kv[:, :, QK_NOPE:].reshape(KV_LORA, -1).astype(BF16), k_norm=head_gain(k_norm),
        conv_w=conv_w, w_o=w_o.astype(BF16))


def _router_weights(w_router):
    hi, lo = _split_hi_lo(w_router.T)
    return dict(wr_hi=hi, wr_lo=lo)


def _moe(x1, h2, aff, mod, tri, wg, wu, wd, n_tok):
    n_all = x1.shape[0]
    cap = CAPACITY_FACTOR * n_all // N_EXPERTS
    posm, s_tab = _route(aff, tri, cap)
    moe = _expert_ffn(h2, posm, aff, s_tab, wg, wu, wd, cap)
    return _resid(x1, moe, mod, n_tok)


def _trunk(x, mods, l0, l1, tri, ctx, rope_tabs, dft):
    n_b, n_tok, _ = x.shape
    mod0, mod1 = mods
    q, k, v, conv, ckv, kr = _front0(x, mod0, l0, rope_tabs)
    if ctx is not None:
        kc, vc = _ctx_kv(ctx[0], ctx[1], l0)
        attn = _attention(q, k, v, kc, vc)
    else:
        attn = _attention(q, k, v)
    flat = lambda a: a.reshape(n_b * n_tok, a.shape[-1])
    x1, h2, aff = _post0(flat(x), flat(attn), flat(conv), mod0, l0, n_tok)
    x = _moe(x1, h2, aff, mod0, tri, l0["wg"], l0["wu"], l0["wd"], n_tok)
    x1, h2, aff = _fnet(x.reshape(n_b, n_tok, D_MODEL), mod1, l1, dft)
    x = _moe(flat(x1), flat(h2), aff, mod1, tri, l1["wg"], l1["wu"], l1["wd"], n_tok)
    return x.reshape(n_b, n_tok, D_MODEL), ckv, kr[:, :, ROPE_LANE0:ROPE_LANE0 + QK_ROPE]


def kernel(x_prompt, x_sample, c, cache_c_kv_l0, cache_k_rope_l0, c_ctx, norm1_l0, norm2_l0, w_mod_l0, b_mod_l0, w_in_l0, q_a_norm_l0, w_q_up_l0, q_norm_l0, kv_a_norm_l0, w_kv_up_l0, k_norm_l0, conv_w_l0, w_o_l0, w_router_l0, w_gate_l0, w_up_l0, w_down_l0, norm1_l1, norm2_l1, w_mod_l1, b_mod_l1, w_f_l1, w_router_l1, w_gate_l1, w_up_l1, w_down_l1):
    n_dec = c.shape[0]
    cond = jnp.concatenate([c_ctx[None, :], c, jnp.zeros((16 - 1 - n_dec, D_MODEL), F32)], axis=0)
    m0 = _modulation(cond, w_mod_l0, b_mod_l0)
    m1 = _modulation(cond, w_mod_l1, b_mod_l1)
    mods_prompt = (m0[0:1], m1[0:1])
    mods_sample = (m0[1:1 + n_dec], m1[1:1 + n_dec])

    l0 = _layer0_weights(norm1_l0, norm2_l0, w_in_l0, q_a_norm_l0, w_q_up_l0, q_norm_l0, kv_a_norm_l0,
                         w_kv_up_l0, k_norm_l0, conv_w_l0, w_o_l0)
    l0.update(_router_weights(w_router_l0))
    l0.update(wg=w_gate_l0.astype(BF16), wu=w_up_l0.astype(BF16), wd=w_down_l0.astype(BF16))
    l1 = dict(norm1=norm1_l1.reshape(1, -1), norm2=norm2_l1.reshape(1, -1), w_f=w_f_l1.astype(BF16))
    l1.update(_router_weights(w_router_l1))
    l1.update(wg=w_gate_l1.astype(BF16), wu=w_up_l1.astype(BF16), wd=w_down_l1.astype(BF16))

    tri = jnp.asarray(np.triu(np.ones((ROWS, ROWS)), 1), BF16)
    y_prompt, new_c_kv, new_k_rope = _trunk(x_prompt, mods_prompt, l0, l1, tri, None, None,
                                            _dft_tables(x_prompt.shape[1]))
    k_rope_pad = jnp.pad(cache_k_rope_l0, ((0, 0), (0, 0), (ROPE_LANE0, LANES - ROPE_LANE0 - QK_ROPE)))
    y_sample, _, _ = _trunk(x_sample, mods_sample, l0, l1, tri, (cache_c_kv_l0, k_rope_pad),
                            _rope_tables(x_sample.shape[1]), _dft_tables(x_sample.shape[1]))
    return (y_prompt, y_sample, new_c_kv, new_k_rope)
```

```python
import functools

import jax
import jax.numpy as jnp
import numpy as np
from jax import lax
from jax.experimental import pallas as pl
from jax.experimental.pallas import tpu as pltpu

D_MODEL = 1024
GRID_W = 64
N_HEADS = 8
QK_NOPE = 64
QK_ROPE = 32
QK_DIM = QK_NOPE + QK_ROPE
V_HEAD = 64
Q_LORA = 384
KV_LORA = 256
CONV_CH = 512
FNET_GROUPS = 4
FNET_CH = D_MODEL // FNET_GROUPS
N_EXPERTS = 16
EXPERT_FF = 512
CAPACITY_FACTOR = 2
ROPE_THETA = 10000.0
EPS = 1e-6

LANES = 128
HEAD_PAD = LANES
ROWS = 256
ALIGN = 16
WIN = 64
IN0_PAD = Q_LORA + KV_LORA + LANES + 3 * CONV_CH
ROPE_LANE0 = QK_NOPE
VMEM_LIMIT = 56 * 1024 * 1024

F32 = jnp.float32
BF16 = jnp.bfloat16


def _dot(a, b):
    return jnp.dot(a, b, preferred_element_type=F32)


def _dot_nt(a, b):
    return lax.dot_general(a, b, (((1,), (1,)), ((), ())), preferred_element_type=F32)


def _split_hi_lo(x):
    hi = x.astype(BF16)
    lo = (x - hi.astype(F32)).astype(BF16)
    return hi, lo


def _params(n_axes):
    return pltpu.CompilerParams(dimension_semantics=("arbitrary",) * n_axes,
                                vmem_limit_bytes=VMEM_LIMIT)


def _rms(x, gain):
    return x * lax.rsqrt(jnp.mean(x * x, axis=-1, keepdims=True) + EPS) * gain


def _mod_kernel(cond_ref, w_ref, b_ref, o_ref):
    c = cond_ref[...]
    s = c * (1.0 / (1.0 + jnp.exp(-c)))
    s_hi, s_lo = _split_hi_lo(s)
    w_hi, w_lo = _split_hi_lo(w_ref[...])
    o_ref[...] = _dot(s_hi, w_hi) + _dot(s_lo, w_hi) + _dot(s_hi, w_lo) + b_ref[...]


def _modulation(cond, w_mod, b_mod):
    n_rows = cond.shape[0]
    tn = 1536
    out = pl.pallas_call(
        _mod_kernel,
        out_shape=jax.ShapeDtypeStruct((n_rows, 6 * D_MODEL), F32),
        grid=(6 * D_MODEL // tn,),
        in_specs=[pl.BlockSpec((n_rows, D_MODEL), lambda i: (0, 0)),
                  pl.BlockSpec((D_MODEL, tn), lambda i: (0, i)),
                  pl.BlockSpec((1, tn), lambda i: (0, i))],
        out_specs=pl.BlockSpec((n_rows, tn), lambda i: (0, i)),
        compiler_params=_params(1),
        name="modulation",
    )(cond, w_mod, b_mod.reshape(1, -1))
    return out.reshape(n_rows, 6, D_MODEL)


def _swap_pairs(x):
    lane = lax.broadcasted_iota(jnp.int32, x.shape, 1)
    return jnp.where((lane & 1) == 0, pltpu.roll(x, LANES - 1, 1), pltpu.roll(x, 1, 1))


def _head_norm_rope(xh, gain, cos, sin):
    ss = jnp.sum(xh * xh, axis=-1, keepdims=True) * (1.0 / QK_DIM)
    y = xh * lax.rsqrt(ss + EPS) * gain
    if cos is not None:
        y = y * cos + _swap_pairs(y) * sin
    return y


def _expand_kv(ckv_bf, r, wk_ref, wv_ref, kn_ref, cos, sin, k_ref, v_ref, rows):
    kf = _dot(ckv_bf, wk_ref[...])
    v_ref[rows, :] = _dot(ckv_bf, wv_ref[...]).astype(BF16)
    for h in range(N_HEADS):
        sl = slice(h * HEAD_PAD, (h + 1) * HEAD_PAD)
        kh = _head_norm_rope(kf[:, sl] + r, kn_ref[:, sl], cos, sin)
        k_ref[rows, sl] = kh.astype(BF16)


def _front0_kernel(*refs, n_tok, rope):
    (x_ref, mod_ref, n1_ref, win_ref, qan_ref, wq_ref, qn_ref, kvan_ref, wk_ref, wv_ref,
     kn_ref, cw_ref) = refs[:12]
    pos = 12
    if rope:
        cos_ref, sin_ref = refs[pos:pos + 2]
        pos += 2
    q_ref, k_ref, v_ref, conv_ref, ckv_ref, kr_ref, cu_s, gb_s = refs[pos:]
    sh1, sc1 = mod_ref[0:1, :], mod_ref[1:2, :]
    for c in range(n_tok // ROWS):
        rows = pl.ds(c * ROWS, ROWS)
        cos = cos_ref[rows, :] if rope else None
        sin = sin_ref[rows, :] if rope else None
        h = _rms(x_ref[rows, :], n1_ref[...]) * (1.0 + sc1) + sh1
        proj = _dot(h.astype(BF16), win_ref[...])
        cq = _rms(proj[:, :Q_LORA], qan_ref[...])
        qf = _dot(cq.astype(BF16), wq_ref[...])
        for hd in range(N_HEADS):
            sl = slice(hd * HEAD_PAD, (hd + 1) * HEAD_PAD)
            q_ref[rows, sl] = _head_norm_rope(qf[:, sl], qn_ref[:, sl], cos, sin).astype(BF16)
        ckv = _rms(proj[:, Q_LORA:Q_LORA + KV_LORA], kvan_ref[...])
        ckv_ref[rows, :] = ckv
        r = proj[:, Q_LORA + KV_LORA:Q_LORA + KV_LORA + LANES]
        kr_ref[rows, :] = r
        _expand_kv(ckv.astype(BF16), r, wk_ref, wv_ref, kn_ref, cos, sin, k_ref, v_ref, rows)
        c0 = Q_LORA + KV_LORA + LANES
        gb_s[rows, :] = proj[:, c0:c0 + CONV_CH]
        cu_s[rows, :] = proj[:, c0 + CONV_CH:c0 + 2 * CONV_CH] * proj[:, c0 + 2 * CONV_CH:c0 + 3 * CONV_CH]
    cu = cu_s[...]
    row = lax.broadcasted_iota(jnp.int32, cu.shape, 0)
    prev = jnp.where(row == 0, 0.0, pltpu.roll(cu, 1, 0))
    nxt = jnp.where(row == n_tok - 1, 0.0, pltpu.roll(cu, n_tok - 1, 0))
    conv = gb_s[...] * (cw_ref[0:1, :] * prev + cw_ref[1:2, :] * cu + cw_ref[2:3, :] * nxt)
    conv_ref[...] = conv.astype(BF16)


def _const_spec(shape):
    return pl.BlockSpec(shape, lambda b: (0,) * len(shape))


def _front0(x, mod, w, rope_tabs):
    n_b, n_tok, _ = x.shape
    shared_mod = mod.shape[0] == 1
    seq = lambda width: pl.BlockSpec((None, n_tok, width), lambda b: (b, 0, 0))
    in_specs = [seq(D_MODEL),
                pl.BlockSpec((None, 6, D_MODEL), (lambda b: (0, 0, 0)) if shared_mod else (lambda b: (b, 0, 0))),
                _const_spec((1, D_MODEL)), _const_spec((D_MODEL, IN0_PAD)), _const_spec((1, Q_LORA)),
                _const_spec((Q_LORA, N_HEADS * HEAD_PAD)), _const_spec((1, N_HEADS * HEAD_PAD)),
                _const_spec((1, KV_LORA)), _const_spec((KV_LORA, N_HEADS * HEAD_PAD)),
                _const_spec((KV_LORA, N_HEADS * V_HEAD)), _const_spec((1, N_HEADS * HEAD_PAD)),
                _const_spec((3, CONV_CH))]
    args = [x, mod, w["norm1"], w["w_in"], w["q_a_norm"], w["w_q"], w["q_norm"], w["kv_a_norm"],
            w["w_k"], w["w_v"], w["k_norm"], w["conv_w"]]
    if rope_tabs is not None:
        in_specs += [_const_spec((n_tok, LANES))] * 2
        args += list(rope_tabs)
    out_shape = [jax.ShapeDtypeStruct((n_b, n_tok, N_HEADS * HEAD_PAD), BF16),
                 jax.ShapeDtypeStruct((n_b, n_tok, N_HEADS * HEAD_PAD), BF16),
                 jax.ShapeDtypeStruct((n_b, n_tok, N_HEADS * V_HEAD), BF16),
                 jax.ShapeDtypeStruct((n_b, n_tok, CONV_CH), BF16),
                 jax.ShapeDtypeStruct((n_b, n_tok, KV_LORA), F32),
                 jax.ShapeDtypeStruct((n_b, n_tok, LANES), F32)]
    out_specs = [seq(N_HEADS * HEAD_PAD), seq(N_HEADS * HEAD_PAD), seq(N_HEADS * V_HEAD), seq(CONV_CH),
                 seq(KV_LORA), seq(LANES)]
    return pl.pallas_call(
        functools.partial(_front0_kernel, n_tok=n_tok, rope=rope_tabs is not None),
        out_shape=out_shape, grid=(n_b,), in_specs=in_specs, out_specs=out_specs,
        scratch_shapes=[pltpu.VMEM((n_tok, CONV_CH), F32), pltpu.VMEM((n_tok, CONV_CH), F32)],
        compiler_params=_params(1), name="front0",
    )(*args)


def _ctx_kv_kernel(ckv_ref, r_ref, wk_ref, wv_ref, kn_ref, k_ref, v_ref):
    rows = pl.ds(0, ckv_ref.shape[0])
    _expand_kv(ckv_ref[...].astype(BF16), r_ref[...], wk_ref, wv_ref, kn_ref, None, None, k_ref, v_ref, rows)


def _ctx_kv(cache_c_kv, cache_k_rope_pad, w):
    n_b, n_ctx, _ = cache_c_kv.shape
    seq = lambda width: pl.BlockSpec((None, n_ctx, width), lambda b: (b, 0, 0))
    return pl.pallas_call(
        _ctx_kv_kernel,
        out_shape=[jax.ShapeDtypeStruct((n_b, n_ctx, N_HEADS * HEAD_PAD), BF16),
                   jax.ShapeDtypeStruct((n_b, n_ctx, N_HEADS * V_HEAD), BF16)],
        grid=(n_b,),
        in_specs=[seq(KV_LORA), seq(LANES), _const_spec((KV_LORA, N_HEADS * HEAD_PAD)),
                  _const_spec((KV_LORA, N_HEADS * V_HEAD)), _const_spec((1, N_HEADS * HEAD_PAD))],
        out_specs=[seq(N_HEADS * HEAD_PAD), seq(N_HEADS * V_HEAD)],
        compiler_params=_params(1), name="ctx_kv",
    )(cache_c_kv, cache_k_rope_pad, w["w_k"], w["w_v"], w["k_norm"])


def _attn_kernel(*refs, with_ctx):
    if with_ctx:
        q_ref, k_ref, v_ref, kc_ref, vc_ref, o_ref = refs
    else:
        q_ref, k_ref, v_ref, o_ref = refs
    lane = lax.broadcasted_iota(jnp.int32, (q_ref.shape[0], LANES), 1)
    for pair in range(N_HEADS // 2):
        vsl = slice(pair * LANES, (pair + 1) * LANES)
        outs = []
        for hd in (2 * pair, 2 * pair + 1):
            sl = slice(hd * HEAD_PAD, (hd + 1) * HEAD_PAD)
            qh = q_ref[:, sl]
            s = _dot_nt(qh, k_ref[:, sl])
            m = jnp.max(s, axis=-1, keepdims=True)
            if with_ctx:
                sc = _dot_nt(qh, kc_ref[:, sl])
                m = jnp.maximum(m, jnp.max(sc, axis=-1, keepdims=True))
                pc = jnp.exp(sc - m)
            p = jnp.exp(s - m)
            den = jnp.sum(p, axis=-1, keepdims=True)
            o = _dot(p.astype(BF16), v_ref[:, vsl])
            if with_ctx:
                den = den + jnp.sum(pc, axis=-1, keepdims=True)
                o = o + _dot(pc.astype(BF16), vc_ref[:, vsl])
            outs.append(o / den)
        o_ref[:, vsl] = jnp.where(lane < V_HEAD, outs[0], outs[1]).astype(BF16)


def _attention(q, k, v, kc=None, vc=None):
    n_b, n_tok, _ = q.shape
    with_ctx = kc is not None
    qspec = lambda width: pl.BlockSpec((None, ROWS, width), lambda b, i: (b, i, 0))
    kvspec = lambda n, width: pl.BlockSpec((None, n, width), lambda b, i: (b, 0, 0))
    in_specs = [qspec(N_HEADS * HEAD_PAD), kvspec(n_tok, N_HEADS * HEAD_PAD), kvspec(n_tok, N_HEADS * V_HEAD)]
    args = [q, k, v]
    if with_ctx:
        n_ctx = kc.shape[1]
        in_specs += [kvspec(n_ctx, N_HEADS * HEAD_PAD), kvspec(n_ctx, N_HEADS * V_HEAD)]
        args += [kc, vc]
    return pl.pallas_call(
        functools.partial(_attn_kernel, with_ctx=with_ctx),
        out_shape=jax.ShapeDtypeStruct((n_b, n_tok, N_HEADS * V_HEAD), BF16),
        grid=(n_b, n_tok // ROWS), in_specs=in_specs, out_specs=qspec(N_HEADS * V_HEAD),
        compiler_params=_params(2), name="attention",
    )(*args)


def _moe_front(x1, mod_ref, n2_ref, wrh_ref, wrl_ref):
    sh2, sc2 = mod_ref[3:4, :], mod_ref[4:5, :]
    h2 = _rms(x1, n2_ref[...]) * (1.0 + sc2) + sh2
    h_hi, h_lo = _split_hi_lo(h2)
    logit = _dot_nt(wrh_ref[...], h_hi) + _dot_nt(wrh_ref[...], h_lo) + _dot_nt(wrl_ref[...], h_hi)
    e = jnp.exp(logit - jnp.max(logit, axis=0, keepdims=True))
    return h_hi, e / jnp.sum(e, axis=0, keepdims=True)


def _post0_kernel(x_ref, attn_ref, conv_ref, wo_ref, mod_ref, n2_ref, wrh_ref, wrl_ref,
                  x1_ref, h2_ref, aff_ref):
    n_attn = N_HEADS * V_HEAD
    mix = _dot(attn_ref[...], wo_ref[0:n_attn, :]) + _dot(conv_ref[...], wo_ref[n_attn:, :])
    x1 = x_ref[...] + mod_ref[2:3, :] * mix
    x1_ref[...] = x1
    h2, aff = _moe_front(x1, mod_ref, n2_ref, wrh_ref, wrl_ref)
    h2_ref[...] = h2
    aff_ref[...] = aff


def _post0(x, attn, conv, mod, w, n_tok):
    n_all = x.shape[0]
    tiles_per_seq = n_tok // ROWS
    shared_mod = mod.shape[0] == 1
    tile = lambda width: pl.BlockSpec((ROWS, width), lambda i: (i, 0))
    const = lambda shape: pl.BlockSpec(shape, lambda i: (0,) * len(shape))
    mod_map = (lambda i: (0, 0, 0)) if shared_mod else (lambda i: (i // tiles_per_seq, 0, 0))
    return pl.pallas_call(
        _post0_kernel,
        out_shape=[jax.ShapeDtypeStruct((n_all, D_MODEL), F32),
                   jax.ShapeDtypeStruct((n_all, D_MODEL), BF16),
                   jax.ShapeDtypeStruct((n_all // ROWS, N_EXPERTS, ROWS), F32)],
        grid=(n_all // ROWS,),
        in_specs=[tile(D_MODEL), tile(N_HEADS * V_HEAD), tile(CONV_CH), const((D_MODEL, D_MODEL)),
                  pl.BlockSpec((None, 6, D_MODEL), mod_map), const((1, D_MODEL)),
                  const((N_EXPERTS, D_MODEL)), const((N_EXPERTS, D_MODEL))],
        out_specs=[tile(D_MODEL), tile(D_MODEL), pl.BlockSpec((None, N_EXPERTS, ROWS), lambda i: (i, 0, 0))],
        compiler_params=_params(1), name="post0",
    )(x, attn, conv, w["w_o"], mod, w["norm2"], w["wr_hi"], w["wr_lo"])


def _fnet_kernel(x_ref, mod_ref, n1_ref, cc_ref, sc_ref, dft_ref, wf_ref, n2_ref, wrh_ref, wrl_ref,
                 x1_ref, h2_ref, aff_ref, y_s, *, n_tok):
    sh1, sc1, g1 = mod_ref[0:1, :], mod_ref[1:2, :], mod_ref[2:3, :]
    n_chunks = n_tok // ROWS
    cc, sc = cc_ref[...].astype(BF16), sc_ref[...].astype(BF16)
    for c in range(n_chunks):
        rows = pl.ds(c * ROWS, ROWS)
        h = (_rms(x_ref[rows, :], n1_ref[...]) * (1.0 + sc1) + sh1).astype(BF16)
        for g in range(FNET_GROUPS):
            sl = slice(g * FNET_CH, (g + 1) * FNET_CH)
            y_s[pl.ds(c * ROWS, ROWS), sl] = _dot(h[:, sl], cc).astype(BF16)
            y_s[pl.ds(n_tok + c * ROWS, ROWS), sl] = _dot(h[:, sl], sc).astype(BF16)
    scale = 1.0 / float(np.sqrt(n_tok * FNET_CH))
    for c in range(n_chunks):
        rows = pl.ds(c * ROWS, ROWS)
        f = _dot(dft_ref[rows, :].astype(BF16), y_s[...]) * scale
        x1 = x_ref[rows, :] + g1 * _dot(f.astype(BF16), wf_ref[...])
        x1_ref[rows, :] = x1
        h2, aff = _moe_front(x1, mod_ref, n2_ref, wrh_ref, wrl_ref)
        h2_ref[rows, :] = h2
        aff_ref[c] = aff


def _fnet(x, mod, w, dft):
    n_b, n_tok, _ = x.shape
    shared_mod = mod.shape[0] == 1
    seq = lambda width: pl.BlockSpec((None, n_tok, width), lambda b: (b, 0, 0))
    cc, sc, dft_n = dft
    tiles = n_tok // ROWS
    return pl.pallas_call(
        functools.partial(_fnet_kernel, n_tok=n_tok),
        out_shape=[jax.ShapeDtypeStruct((n_b, n_tok, D_MODEL), F32),
                   jax.ShapeDtypeStruct((n_b, n_tok, D_MODEL), BF16),
                   jax.ShapeDtypeStruct((n_b * tiles, N_EXPERTS, ROWS), F32)],
        grid=(n_b,),
        in_specs=[seq(D_MODEL),
                  pl.BlockSpec((None, 6, D_MODEL), (lambda b: (0, 0, 0)) if shared_mod else (lambda b: (b, 0, 0))),
                  _const_spec((1, D_MODEL)), _const_spec((FNET_CH, FNET_CH)), _const_spec((FNET_CH, FNET_CH)),
                  pl.BlockSpec((n_tok, 2 * n_tok), lambda b: (0, 0), pipeline_mode=pl.Buffered(1)),
                  _const_spec((D_MODEL, D_MODEL)), _const_spec((1, D_MODEL)),
                  _const_spec((N_EXPERTS, D_MODEL)), _const_spec((N_EXPERTS, D_MODEL))],
        out_specs=[seq(D_MODEL), seq(D_MODEL), pl.BlockSpec((tiles, N_EXPERTS, ROWS), lambda b: (b, 0, 0))],
        scratch_shapes=[pltpu.VMEM((2 * n_tok, D_MODEL), BF16)],
        compiler_params=_params(1), name="fnet",
    )(x, mod, w["norm1"], cc, sc, dft_n, w["w_f"], w["norm2"], w["wr_hi"], w["wr_lo"])


def _route_kernel(aff_ref, tri_ref, posm_ref, s_ref, *, n_blk, cap):
    def count(pred):
        acc = jnp.zeros((N_EXPERTS, ROWS), F32)
        for b in range(n_blk):
            acc = acc + jnp.where(pred(aff_ref[b]), 1.0, 0.0)
        return jnp.sum(acc, axis=1, keepdims=True)

    def as_f32(bits):
        return pltpu.bitcast(bits, F32)

    def search(_, carry):
        lo, hi = carry
        mid = lo + ((hi - lo) >> 1)
        mid_f = as_f32(mid)
        ok = count(lambda a: a >= mid_f) >= cap
        return jnp.where(ok, mid, lo), jnp.where(ok, hi, mid)

    one_bits = 0x3F800000
    lo0 = jnp.zeros((N_EXPERTS, 1), jnp.int32)
    hi0 = jnp.full((N_EXPERTS, 1), one_bits + 1, jnp.int32)
    lo, hi = lax.fori_loop(0, 31, search, (lo0, hi0))
    lo_f, ub = as_f32(lo), as_f32(hi)
    thr = lo_f
    pending = jnp.ones((N_EXPERTS, 1), F32)
    for _ in range(3):
        cur = jnp.full((N_EXPERTS, ROWS), -1.0, F32)
        for b in range(n_blk):
            a = aff_ref[b]
            cur = jnp.maximum(cur, jnp.where((a >= lo_f) & (a < ub), a, -1.0))
        cur = jnp.max(cur, axis=1, keepdims=True)
        take = (count(lambda a: a >= cur) >= cap) & (pending > 0.0)
        thr = jnp.where(take, cur, thr)
        pending = jnp.where(take, 0.0, pending)
        ub = cur
    need = cap - count(lambda a: a > thr)

    lane = lax.broadcasted_iota(jnp.int32, (N_EXPERTS, LANES), 1)
    carry_tie = jnp.zeros((N_EXPERTS, 1), F32)
    carry_pos = jnp.zeros((N_EXPERTS, 1), F32)
    s_acc = jnp.zeros((N_EXPERTS, LANES), jnp.int32)
    for b in range(n_blk):
        v = aff_ref[b]
        eq = jnp.where(v == thr, 1.0, 0.0)
        tie_rank = _dot(eq.astype(BF16), tri_ref[...]) + carry_tie
        sel = jnp.where((v > thr) | ((v == thr) & (tie_rank < need)), 1.0, 0.0)
        pos = _dot(sel.astype(BF16), tri_ref[...]) + carry_pos
        posm_ref[b] = jnp.where(sel > 0.0, pos, -1.0)
        s_acc = jnp.where(lane == b, carry_pos.astype(jnp.int32), s_acc)
        carry_tie = carry_tie + jnp.sum(eq, axis=1, keepdims=True)
        n_sel = jnp.sum(sel, axis=1, keepdims=True)
        carry_pos = carry_pos + jnp.floor((n_sel + (ALIGN - 1)) * (1.0 / ALIGN)) * ALIGN
    s_ref[...] = jnp.where(lane == n_blk, carry_pos.astype(jnp.int32), s_acc)


def _route(aff, tri, cap):
    n_blk = aff.shape[0]
    assert n_blk < LANES
    full = lambda shape: pl.BlockSpec(shape, lambda i: (0,) * len(shape))
    posm, s_tab = pl.pallas_call(
        functools.partial(_route_kernel, n_blk=n_blk, cap=cap),
        out_shape=[jax.ShapeDtypeStruct((n_blk, N_EXPERTS, ROWS), F32),
                   jax.ShapeDtypeStruct((N_EXPERTS, LANES), jnp.int32)],
        grid=(1,),
        in_specs=[full((n_blk, N_EXPERTS, ROWS)), full((ROWS, ROWS))],
        out_specs=[full((n_blk, N_EXPERTS, ROWS)), full((N_EXPERTS, LANES))],
        compiler_params=_params(1), name="route",
    )(aff, tri)
    return posm, s_tab[:, :n_blk + 1]


def _block_segments(s_ref, b):
    base = [s_ref[e, b] for e in range(N_EXPERTS)]
    count = [s_ref[e, b + 1] - base[e] for e in range(N_EXPERTS)]
    longest = functools.reduce(jnp.maximum, count)
    return base, count, (longest + (WIN - 1)) // WIN


def _window_onehot(posm_ref, e, first, lo):
    slot = lax.broadcasted_iota(jnp.int32, (WIN, ROWS), 0) + first
    p = posm_ref[e:e + 1, :].astype(jnp.int32)
    return jnp.where((p == slot) & (p >= lo), 1.0, 0.0)


def _chunk_copies(src_ref, src_row, dst_ref, e, dst_row, sem):
    src = src_ref.at[pl.ds(pl.multiple_of(src_row, ALIGN), ALIGN), :]
    dst = dst_ref.at[e, pl.ds(pl.multiple_of(dst_row, ALIGN), ALIGN), :]
    return pltpu.make_async_copy(src, dst, sem)


def _dispatch_kernel(s_ref, h_ref, posm_ref, aff_ref, xe_ref, ge_ref, sel_s, x_s, g_s, sem,
                     *, n_blk, cap_pad):
    b = pl.program_id(0)
    base, count, rounds = _block_segments(s_ref, b)

    a = jnp.concatenate([aff_ref[...], jnp.zeros((LANES - N_EXPERTS, ROWS), F32)], axis=0).T
    a_hi = a.astype(BF16).astype(F32)
    a_mid = (a - a_hi).astype(BF16).astype(F32)
    a_lo = a - a_hi - a_mid
    parts = (a_hi + pltpu.roll(a_mid, N_EXPERTS, 1) + pltpu.roll(a_lo, 2 * N_EXPERTS, 1)).astype(BF16)
    lane = lax.broadcasted_iota(jnp.int32, (WIN, LANES), 1)

    def copies(e, r, c):
        src_row = e * WIN + c * ALIGN
        dst_row = base[e] + r * WIN + c * ALIGN
        return (_chunk_copies(x_s, src_row, xe_ref, e, dst_row, sem.at[0]),
                _chunk_copies(g_s, src_row, ge_ref, e, dst_row, sem.at[1]))

    def for_chunks(r, fn):
        for e in range(N_EXPERTS):
            n_chunks = jnp.clip(count[e] - r * WIN, 0, WIN) // ALIGN
            lax.fori_loop(0, n_chunks, lambda c, _: fn(copies(e, r, c)), 0)

    def one_round(r, _):
        for e in range(N_EXPERTS):
            first = base[e] + r * WIN
            sel_s[e * WIN:(e + 1) * WIN, :] = _window_onehot(posm_ref, e, first, first).astype(BF16)
        sel = sel_s[...]
        x_s[...] = _dot(sel, h_ref[...]).astype(BF16)
        gp = _dot(sel, parts)
        for e in range(N_EXPERTS):
            mine = ((lane & (N_EXPERTS - 1)) == e) & (lane < 3 * N_EXPERTS)
            gate = jnp.sum(jnp.where(mine, gp[e * WIN:(e + 1) * WIN, :], 0.0), axis=1, keepdims=True)
            g_s[e * WIN:(e + 1) * WIN, :] = jnp.broadcast_to(gate, (WIN, LANES))

        def start(cs):
            cs[0].start()
            cs[1].start()
            return 0

        def wait(cs):
            cs[0].wait()
            cs[1].wait()
            return 0

        for_chunks(r, start)
        for_chunks(r, wait)
        return 0

    lax.fori_loop(0, rounds, one_round, 0)

    @pl.when(b == n_blk - 1)
    def _():
        x_s[0:ALIGN, :] = jnp.zeros((ALIGN, D_MODEL), BF16)
        g_s[0:ALIGN, :] = jnp.zeros((ALIGN, LANES), F32)

        def tail(fn):
            for e in range(N_EXPERTS):
                total = s_ref[e, n_blk]
                def body(c, _):
                    dst_row = total + c * ALIGN
                    fn((_chunk_copies(x_s, 0, xe_ref, e, dst_row, sem.at[0]),
                        _chunk_copies(g_s, 0, ge_ref, e, dst_row, sem.at[1])))
                    return 0
                lax.fori_loop(0, (cap_pad - total) // ALIGN, body, 0)

        tail(lambda cs: (cs[0].start(), cs[1].start()))
        tail(lambda cs: (cs[0].wait(), cs[1].wait()))


def _ffn_kernel(s_ref, x_ref, g_ref, wg_ref, wu_ref, wd_ref, y_ref, wg_s, wu_s, wd_s, *, n_blk):
    e, j = pl.program_id(0), pl.program_id(1)

    @pl.when(j == 0)
    def _():
        wg_s[...] = wg_ref[...].astype(BF16)
        wu_s[...] = wu_ref[...].astype(BF16)
        wd_s[...] = wd_ref[...].astype(BF16)

    used = j * ROWS < s_ref[e, n_blk]

    @pl.when(used)
    def _():
        x = x_ref[...]
        a = _dot(x, wg_s[...])
        u = _dot(x, wu_s[...])
        gate = jnp.tile(g_ref[...], (1, EXPERT_FF // LANES))
        hid = a * (1.0 / (1.0 + jnp.exp(-a))) * u * gate
        y_ref[...] = _dot(hid.astype(BF16), wd_s[...]).astype(BF16)

    @pl.when(jnp.logical_not(used))
    def _():
        y_ref[...] = jnp.zeros_like(y_ref)


def _combine_kernel(s_ref, y_ref, posm_ref, x1_ref, mod_ref, o_ref, yw_s, sel_s, acc_s, sem, *, cap_pad):
    b = pl.program_id(0)
    base, _, rounds = _block_segments(s_ref, b)
    acc_s[...] = jnp.zeros_like(acc_s)

    def one_round(r, _):
        def bounds(e):
            lo = base[e] + r * WIN
            return lo, jnp.minimum(lo, cap_pad - WIN)

        def window_copy(e):
            src = y_ref.at[e, pl.ds(pl.multiple_of(bounds(e)[1], ALIGN), WIN), :]
            return pltpu.make_async_copy(src, yw_s.at[pl.ds(e * WIN, WIN), :], sem.at[0])

        for e in range(N_EXPERTS):
            window_copy(e).start()
        for e in range(N_EXPERTS):
            lo, first = bounds(e)
            sel_s[e * WIN:(e + 1) * WIN, :] = _window_onehot(posm_ref, e, first, lo)
        for e in range(N_EXPERTS):
            window_copy(e).wait()
        acc_s[...] += _dot(sel_s[...].T.astype(BF16), yw_s[...])
        return 0

    lax.fori_loop(0, rounds, one_round, 0)
    o_ref[...] = x1_ref[...] + mod_ref[5:6, :] * acc_s[...]


def _padded_capacity(cap, n_blk):
    return -(-(cap + (ALIGN - 1) * n_blk) // ROWS) * ROWS


def _expert_ffn(x1, h2, posm, aff, s_tab, mod, wg, wu, wd, cap, n_tok):
    n_all = h2.shape[0]
    n_blk = n_all // ROWS
    cap_pad = _padded_capacity(cap, n_blk)
    blk = lambda *shape: pl.BlockSpec((None,) + shape, lambda b, s: (b,) + (0,) * len(shape))
    xe, ge = pl.pallas_call(
        functools.partial(_dispatch_kernel, n_blk=n_blk, cap_pad=cap_pad),
        out_shape=[jax.ShapeDtypeStruct((N_EXPERTS, cap_pad, D_MODEL), BF16),
                   jax.ShapeDtypeStruct((N_EXPERTS, cap_pad, LANES), F32)],
        grid_spec=pltpu.PrefetchScalarGridSpec(
            num_scalar_prefetch=1, grid=(n_blk,),
            in_specs=[pl.BlockSpec((ROWS, D_MODEL), lambda b, s: (b, 0)), blk(N_EXPERTS, ROWS), blk(N_EXPERTS, ROWS)],
            out_specs=[pl.BlockSpec(memory_space=pl.ANY), pl.BlockSpec(memory_space=pl.ANY)],
            scratch_shapes=[pltpu.VMEM((N_EXPERTS * WIN, ROWS), BF16), pltpu.VMEM((N_EXPERTS * WIN, D_MODEL), BF16),
                            pltpu.VMEM((N_EXPERTS * WIN, LANES), F32), pltpu.SemaphoreType.DMA((2,))]),
        compiler_params=_params(1), name="dispatch",
    )(s_tab, h2, posm, aff)

    per_expert = lambda rows, cols: pl.BlockSpec((None, rows, cols), lambda e, j, s: (e, 0, 0))
    tile = lambda cols: pl.BlockSpec((None, ROWS, cols), lambda e, j, s: (e, j, 0))
    y = pl.pallas_call(
        functools.partial(_ffn_kernel, n_blk=n_blk),
        out_shape=jax.ShapeDtypeStruct((N_EXPERTS, cap_pad, D_MODEL), BF16),
        grid_spec=pltpu.PrefetchScalarGridSpec(
            num_scalar_prefetch=1, grid=(N_EXPERTS, cap_pad // ROWS),
            in_specs=[tile(D_MODEL), tile(LANES), per_expert(D_MODEL, EXPERT_FF), per_expert(D_MODEL, EXPERT_FF),
                      per_expert(EXPERT_FF, D_MODEL)],
            out_specs=tile(D_MODEL),
            scratch_shapes=[pltpu.VMEM((D_MODEL, EXPERT_FF), BF16), pltpu.VMEM((D_MODEL, EXPERT_FF), BF16),
                            pltpu.VMEM((EXPERT_FF, D_MODEL), BF16)]),
        compiler_params=_params(2), name="ffn",
    )(s_tab, xe, ge, wg, wu, wd)

    tiles_per_seq = n_tok // ROWS
    mod_map = (lambda b, s: (0, 0, 0)) if mod.shape[0] == 1 else (lambda b, s: (b // tiles_per_seq, 0, 0))
    rows = pl.BlockSpec((ROWS, D_MODEL), lambda b, s: (b, 0))
    return pl.pallas_call(
        functools.partial(_combine_kernel, cap_pad=cap_pad),
        out_shape=jax.ShapeDtypeStruct((n_all, D_MODEL), F32),
        grid_spec=pltpu.PrefetchScalarGridSpec(
            num_scalar_prefetch=1, grid=(n_blk,),
            in_specs=[pl.BlockSpec(memory_space=pl.ANY), blk(N_EXPERTS, ROWS), rows,
                      pl.BlockSpec((None, 6, D_MODEL), mod_map)],
            out_specs=rows,
            scratch_shapes=[pltpu.VMEM((N_EXPERTS * WIN, D_MODEL), BF16), pltpu.VMEM((N_EXPERTS * WIN, ROWS), F32),
                            pltpu.VMEM((ROWS, D_MODEL), F32), pltpu.SemaphoreType.DMA((1,))]),
        compiler_params=_params(1), name="combine",
    )(s_tab, y, posm, x1, mod)


def _rope_tables(n_tok):
    rows = n_tok // GRID_W
    row = np.repeat(np.arange(rows, dtype=np.float64), GRID_W)
    col = np.tile(np.arange(GRID_W, dtype=np.float64), rows)
    axis_dim = QK_ROPE // 2
    inv_freq = ROPE_THETA ** (-np.arange(0, axis_dim, 2, dtype=np.float64) / axis_dim)
    ang = np.concatenate([row[:, None] * inv_freq, col[:, None] * inv_freq], axis=-1)
    cos = np.ones((n_tok, LANES))
    sin = np.zeros((n_tok, LANES))
    cos[:, ROPE_LANE0:ROPE_LANE0 + QK_ROPE] = np.repeat(np.cos(ang), 2, axis=1)
    sgn = np.tile(np.array([-1.0, 1.0]), QK_ROPE // 2)
    sin[:, ROPE_LANE0:ROPE_LANE0 + QK_ROPE] = np.repeat(np.sin(ang), 2, axis=1) * sgn
    return jnp.asarray(cos, F32), jnp.asarray(sin, F32)


def _dft_tables(n_tok):
    def cs(n):
        k = np.arange(n)
        ang = 2.0 * np.pi * ((k[:, None] * k[None, :]) % n) / n
        return np.cos(ang), np.sin(ang)
    cc, sc = cs(FNET_CH)
    cn, sn = cs(n_tok)
    return tuple(jnp.asarray(t, F32) for t in (cc, sc, np.concatenate([cn, -sn], axis=1)))


def _pad_heads(w, width):
    lead = w.shape[:-1]
    w = w.reshape(lead + (N_HEADS, width))
    w = jnp.pad(w, [(0, 0)] * len(lead) + [(0, 0), (0, HEAD_PAD - width)])
    return w.reshape(lead + (N_HEADS * HEAD_PAD,))


def _layer0_weights(norm1, norm2, w_in, q_a_norm, w_q_up, q_norm, kv_a_norm, w_kv_up, k_norm, conv_w, w_o):
    c0 = Q_LORA + KV_LORA
    rope_cols = jnp.pad(w_in[:, c0:c0 + QK_ROPE], ((0, 0), (ROPE_LANE0, LANES - ROPE_LANE0 - QK_ROPE)))
    w_in_pad = jnp.concatenate([w_in[:, :c0], rope_cols, w_in[:, c0 + QK_ROPE:]], axis=1)
    kv = w_kv_up.reshape(KV_LORA, N_HEADS, QK_NOPE + V_HEAD)
    head_gain = lambda g: jnp.tile(jnp.pad(g, (0, HEAD_PAD - QK_DIM)), N_HEADS).reshape(1, -1)
    return dict(
        norm1=norm1.reshape(1, -1), norm2=norm2.reshape(1, -1), w_in=w_in_pad.astype(BF16),
        q_a_norm=q_a_norm.reshape(1, -1), w_q=_pad_heads(w_q_up, QK_DIM).astype(BF16),
        q_norm=head_gain(q_norm) * (QK_DIM ** -0.5), kv_a_norm=kv_a_norm.reshape(1, -1),
        w_k=_pad_heads(kv[:, :, :QK_NOPE].reshape(KV_LORA, -1), QK_NOPE).astype(BF16),
        w_v=kv[:, :, QK_NOPE:].reshape(KV_LORA, -1).astype(BF16), k_norm=head_gain(k_norm),
        conv_w=conv_w, w_o=w_o.astype(BF16))


def _router_weights(w_router):
    hi, lo = _split_hi_lo(w_router.T)
    return dict(wr_hi=hi, wr_lo=lo)


def _moe(x1, h2, aff, mod, tri, wg, wu, wd, n_tok):
    n_all = x1.shape[0]
    cap = CAPACITY_FACTOR * n_all // N_EXPERTS
    posm, s_tab = _route(aff, tri, cap)
    return _expert_ffn(x1, h2, posm, aff, s_tab, mod, wg, wu, wd, cap, n_tok)


def _trunk(x, mods, l0, l1, tri, ctx, rope_tabs, dft):
    n_b, n_tok, _ = x.shape
    mod0, mod1 = mods
    q, k, v, conv, ckv, kr = _front0(x, mod0, l0, rope_tabs)
    if ctx is not None:
        kc, vc = _ctx_kv(ctx[0], ctx[1], l0)
        attn = _attention(q, k, v, kc, vc)
    else:
        attn = _attention(q, k, v)
    flat = lambda a: a.reshape(n_b * n_tok, a.shape[-1])
    x1, h2, aff = _post0(flat(x), flat(attn), flat(conv), mod0, l0, n_tok)
    x = _moe(x1, h2, aff, mod0, tri, l0["wg"], l0["wu"], l0["wd"], n_tok)
    x1, h2, aff = _fnet(x.reshape(n_b, n_tok, D_MODEL), mod1, l1, dft)
    x = _moe(flat(x1), flat(h2), aff, mod1, tri, l1["wg"], l1["wu"], l1["wd"], n_tok)
    return x.reshape(n_b, n_tok, D_MODEL), ckv, kr[:, :, ROPE_LANE0:ROPE_LANE0 + QK_ROPE]


def kernel(x_prompt, x_sample, c, cache_c_kv_l0, cache_k_rope_l0, c_ctx, norm1_l0, norm2_l0, w_mod_l0, b_mod_l0, w_in_l0, q_a_norm_l0, w_q_up_l0, q_norm_l0, kv_a_norm_l0, w_kv_up_l0, k_norm_l0, conv_w_l0, w_o_l0, w_router_l0, w_gate_l0, w_up_l0, w_down_l0, norm1_l1, norm2_l1, w_mod_l1, b_mod_l1, w_f_l1, w_router_l1, w_gate_l1, w_up_l1, w_down_l1):
    n_dec = c.shape[0]
    cond = jnp.concatenate([c_ctx[None, :], c, jnp.zeros((16 - 1 - n_dec, D_MODEL), F32)], axis=0)
    m0 = _modulation(cond, w_mod_l0, b_mod_l0)
    m1 = _modulation(cond, w_mod_l1, b_mod_l1)
    mods_prompt = (m0[0:1], m1[0:1])
    mods_sample = (m0[1:1 + n_dec], m1[1:1 + n_dec])

    l0 = _layer0_weights(norm1_l0, norm2_l0, w_in_l0, q_a_norm_l0, w_q_up_l0, q_norm_l0, kv_a_norm_l0,
                         w_kv_up_l0, k_norm_l0, conv_w_l0, w_o_l0)
    l0.update(_router_weights(w_router_l0))
    l0.update(wg=w_gate_l0, wu=w_up_l0, wd=w_down_l0)
    l1 = dict(norm1=norm1_l1.reshape(1, -1), norm2=norm2_l1.reshape(1, -1), w_f=w_f_l1.astype(BF16))
    l1.update(_router_weights(w_router_l1))
    l1.update(wg=w_gate_l1, wu=w_up_l1, wd=w_down_l1)

    tri = jnp.asarray(np.triu(np.ones((ROWS, ROWS)), 1), BF16)
    y_prompt, new_c_kv, new_k_rope = _trunk(x_prompt, mods_prompt, l0, l1, tri, None, None,
                                            _dft_tables(x_prompt.shape[1]))
    k_rope_pad = jnp.pad(cache_k_rope_l0, ((0, 0), (0, 0), (ROPE_LANE0, LANES - ROPE_LANE0 - QK_ROPE)))
    y_sample, _, _ = _trunk(x_sample, mods_sample, l0, l1, tri, (cache_c_kv_l0, k_rope_pad),
                            _rope_tables(x_sample.shape[1]), _dft_tables(x_sample.shape[1]))
    return (y_prompt, y_sample, new_c_kv, new_k_rope)
```

```python
import functools

import jax
import jax.numpy as jnp
import numpy as np
from jax import lax
from jax.experimental import pallas as pl
from jax.experimental.pallas import tpu as pltpu

D_MODEL = 1024
GRID_W = 64
N_HEADS = 8
QK_NOPE = 64
QK_ROPE = 32
QK_DIM = QK_NOPE + QK_ROPE
V_HEAD = 64
Q_LORA = 384
KV_LORA = 256
CONV_CH = 512
FNET_GROUPS = 4
FNET_CH = D_MODEL // FNET_GROUPS
N_EXPERTS = 16
EXPERT_FF = 512
CAPACITY_FACTOR = 2
ROPE_THETA = 10000.0
EPS = 1e-6

LANES = 128
HEAD_PAD = LANES
ROWS = 256
ALIGN = 16
WIN = 64
XE_W = D_MODEL + LANES
IN0_PAD = Q_LORA + KV_LORA + LANES + 3 * CONV_CH
ROPE_LANE0 = QK_NOPE
VMEM_LIMIT = 56 * 1024 * 1024

F32 = jnp.float32
BF16 = jnp.bfloat16


def _dot(a, b):
    return jnp.dot(a, b, preferred_element_type=F32)


def _dot_nt(a, b):
    return lax.dot_general(a, b, (((1,), (1,)), ((), ())), preferred_element_type=F32)


def _split_hi_lo(x):
    hi = x.astype(BF16)
    lo = (x - hi.astype(F32)).astype(BF16)
    return hi, lo


def _params(n_axes):
    return pltpu.CompilerParams(dimension_semantics=("arbitrary",) * n_axes,
                                vmem_limit_bytes=VMEM_LIMIT)


def _rms(x, gain):
    return x * lax.rsqrt(jnp.mean(x * x, axis=-1, keepdims=True) + EPS) * gain


def _mod_kernel(cond_ref, w_ref, b_ref, o_ref):
    c = cond_ref[...]
    s = c * (1.0 / (1.0 + jnp.exp(-c)))
    s_hi, s_lo = _split_hi_lo(s)
    w_hi, w_lo = _split_hi_lo(w_ref[...])
    o_ref[...] = _dot(s_hi, w_hi) + _dot(s_lo, w_hi) + _dot(s_hi, w_lo) + b_ref[...]


def _modulation(cond, w_mod, b_mod):
    n_rows = cond.shape[0]
    tn = 1536
    out = pl.pallas_call(
        _mod_kernel,
        out_shape=jax.ShapeDtypeStruct((n_rows, 6 * D_MODEL), F32),
        grid=(6 * D_MODEL // tn,),
        in_specs=[pl.BlockSpec((n_rows, D_MODEL), lambda i: (0, 0)),
                  pl.BlockSpec((D_MODEL, tn), lambda i: (0, i)),
                  pl.BlockSpec((1, tn), lambda i: (0, i))],
        out_specs=pl.BlockSpec((n_rows, tn), lambda i: (0, i)),
        compiler_params=_params(1),
        name="modulation",
    )(cond, w_mod, b_mod.reshape(1, -1))
    return out.reshape(n_rows, 6, D_MODEL)


def _swap_pairs(x):
    lane = lax.broadcasted_iota(jnp.int32, x.shape, 1)
    return jnp.where((lane & 1) == 0, pltpu.roll(x, LANES - 1, 1), pltpu.roll(x, 1, 1))


def _head_norm_rope(xh, gain, cos, sin):
    ss = jnp.sum(xh * xh, axis=-1, keepdims=True) * (1.0 / QK_DIM)
    y = xh * lax.rsqrt(ss + EPS) * gain
    if cos is not None:
        y = y * cos + _swap_pairs(y) * sin
    return y


def _expand_kv(ckv_bf, r, wk_ref, wv_ref, kn_ref, cos, sin, k_ref, v_ref, rows):
    kf = _dot(ckv_bf, wk_ref[...])
    v_ref[rows, :] = _dot(ckv_bf, wv_ref[...]).astype(BF16)
    for h in range(N_HEADS):
        sl = slice(h * HEAD_PAD, (h + 1) * HEAD_PAD)
        kh = _head_norm_rope(kf[:, sl] + r, kn_ref[:, sl], cos, sin)
        k_ref[rows, sl] = kh.astype(BF16)


def _front0_kernel(*refs, n_tok, rope):
    (x_ref, mod_ref, n1_ref, win_ref, qan_ref, wq_ref, qn_ref, kvan_ref, wk_ref, wv_ref,
     kn_ref, cw_ref) = refs[:12]
    pos = 12
    if rope:
        cos_ref, sin_ref = refs[pos:pos + 2]
        pos += 2
    q_ref, k_ref, v_ref, conv_ref, ckv_ref, kr_ref, cu_s, gb_s = refs[pos:]
    sh1, sc1 = mod_ref[0:1, :], mod_ref[1:2, :]
    for c in range(n_tok // ROWS):
        rows = pl.ds(c * ROWS, ROWS)
        cos = cos_ref[rows, :] if rope else None
        sin = sin_ref[rows, :] if rope else None
        h = _rms(x_ref[rows, :], n1_ref[...]) * (1.0 + sc1) + sh1
        proj = _dot(h.astype(BF16), win_ref[...])
        cq = _rms(proj[:, :Q_LORA], qan_ref[...])
        qf = _dot(cq.astype(BF16), wq_ref[...])
        for hd in range(N_HEADS):
            sl = slice(hd * HEAD_PAD, (hd + 1) * HEAD_PAD)
            q_ref[rows, sl] = _head_norm_rope(qf[:, sl], qn_ref[:, sl], cos, sin).astype(BF16)
        ckv = _rms(proj[:, Q_LORA:Q_LORA + KV_LORA], kvan_ref[...])
        ckv_ref[rows, :] = ckv
        r = proj[:, Q_LORA + KV_LORA:Q_LORA + KV_LORA + LANES]
        kr_ref[rows, :] = r
        _expand_kv(ckv.astype(BF16), r, wk_ref, wv_ref, kn_ref, cos, sin, k_ref, v_ref, rows)
        c0 = Q_LORA + KV_LORA + LANES
        gb_s[rows, :] = proj[:, c0:c0 + CONV_CH]
        cu_s[rows, :] = proj[:, c0 + CONV_CH:c0 + 2 * CONV_CH] * proj[:, c0 + 2 * CONV_CH:c0 + 3 * CONV_CH]
    cu = cu_s[...]
    row = lax.broadcasted_iota(jnp.int32, cu.shape, 0)
    prev = jnp.where(row == 0, 0.0, pltpu.roll(cu, 1, 0))
    nxt = jnp.where(row == n_tok - 1, 0.0, pltpu.roll(cu, n_tok - 1, 0))
    conv = gb_s[...] * (cw_ref[0:1, :] * prev + cw_ref[1:2, :] * cu + cw_ref[2:3, :] * nxt)
    conv_ref[...] = conv.astype(BF16)


def _const_spec(shape):
    return pl.BlockSpec(shape, lambda b: (0,) * len(shape))


def _front0(x, mod, w, rope_tabs):
    n_b, n_tok, _ = x.shape
    shared_mod = mod.shape[0] == 1
    seq = lambda width: pl.BlockSpec((None, n_tok, width), lambda b: (b, 0, 0))
    in_specs = [seq(D_MODEL),
                pl.BlockSpec((None, 6, D_MODEL), (lambda b: (0, 0, 0)) if shared_mod else (lambda b: (b, 0, 0))),
                _const_spec((1, D_MODEL)), _const_spec((D_MODEL, IN0_PAD)), _const_spec((1, Q_LORA)),
                _const_spec((Q_LORA, N_HEADS * HEAD_PAD)), _const_spec((1, N_HEADS * HEAD_PAD)),
                _const_spec((1, KV_LORA)), _const_spec((KV_LORA, N_HEADS * HEAD_PAD)),
                _const_spec((KV_LORA, N_HEADS * V_HEAD)), _const_spec((1, N_HEADS * HEAD_PAD)),
                _const_spec((3, CONV_CH))]
    args = [x, mod, w["norm1"], w["w_in"], w["q_a_norm"], w["w_q"], w["q_norm"], w["kv_a_norm"],
            w["w_k"], w["w_v"], w["k_norm"], w["conv_w"]]
    if rope_tabs is not None:
        in_specs += [_const_spec((n_tok, LANES))] * 2
        args += list(rope_tabs)
    out_shape = [jax.ShapeDtypeStruct((n_b, n_tok, N_HEADS * HEAD_PAD), BF16),
                 jax.ShapeDtypeStruct((n_b, n_tok, N_HEADS * HEAD_PAD), BF16),
                 jax.ShapeDtypeStruct((n_b, n_tok, N_HEADS * V_HEAD), BF16),
                 jax.ShapeDtypeStruct((n_b, n_tok, CONV_CH), BF16),
                 jax.ShapeDtypeStruct((n_b, n_tok, KV_LORA), F32),
                 jax.ShapeDtypeStruct((n_b, n_tok, LANES), F32)]
    out_specs = [seq(N_HEADS * HEAD_PAD), seq(N_HEADS * HEAD_PAD), seq(N_HEADS * V_HEAD), seq(CONV_CH),
                 seq(KV_LORA), seq(LANES)]
    return pl.pallas_call(
        functools.partial(_front0_kernel, n_tok=n_tok, rope=rope_tabs is not None),
        out_shape=out_shape, grid=(n_b,), in_specs=in_specs, out_specs=out_specs,
        scratch_shapes=[pltpu.VMEM((n_tok, CONV_CH), F32), pltpu.VMEM((n_tok, CONV_CH), F32)],
        compiler_params=_params(1), name="front0",
    )(*args)


def _ctx_kv_kernel(ckv_ref, r_ref, wk_ref, wv_ref, kn_ref, k_ref, v_ref):
    rows = pl.ds(0, ckv_ref.shape[0])
    _expand_kv(ckv_ref[...].astype(BF16), r_ref[...], wk_ref, wv_ref, kn_ref, None, None, k_ref, v_ref, rows)


def _ctx_kv(cache_c_kv, cache_k_rope_pad, w):
    n_b, n_ctx, _ = cache_c_kv.shape
    seq = lambda width: pl.BlockSpec((None, n_ctx, width), lambda b: (b, 0, 0))
    return pl.pallas_call(
        _ctx_kv_kernel,
        out_shape=[jax.ShapeDtypeStruct((n_b, n_ctx, N_HEADS * HEAD_PAD), BF16),
                   jax.ShapeDtypeStruct((n_b, n_ctx, N_HEADS * V_HEAD), BF16)],
        grid=(n_b,),
        in_specs=[seq(KV_LORA), seq(LANES), _const_spec((KV_LORA, N_HEADS * HEAD_PAD)),
                  _const_spec((KV_LORA, N_HEADS * V_HEAD)), _const_spec((1, N_HEADS * HEAD_PAD))],
        out_specs=[seq(N_HEADS * HEAD_PAD), seq(N_HEADS * V_HEAD)],
        compiler_params=_params(1), name="ctx_kv",
    )(cache_c_kv, cache_k_rope_pad, w["w_k"], w["w_v"], w["k_norm"])


def _attn_kernel(*refs, with_ctx):
    if with_ctx:
        q_ref, k_ref, v_ref, kc_ref, vc_ref, o_ref = refs
    else:
        q_ref, k_ref, v_ref, o_ref = refs
    lane = lax.broadcasted_iota(jnp.int32, (q_ref.shape[0], LANES), 1)
    for pair in range(N_HEADS // 2):
        vsl = slice(pair * LANES, (pair + 1) * LANES)
        outs = []
        for hd in (2 * pair, 2 * pair + 1):
            sl = slice(hd * HEAD_PAD, (hd + 1) * HEAD_PAD)
            qh = q_ref[:, sl]
            s = _dot_nt(qh, k_ref[:, sl])
            m = jnp.max(s, axis=-1, keepdims=True)
            if with_ctx:
                sc = _dot_nt(qh, kc_ref[:, sl])
                m = jnp.maximum(m, jnp.max(sc, axis=-1, keepdims=True))
                pc = jnp.exp(sc - m)
            p = jnp.exp(s - m)
            den = jnp.sum(p, axis=-1, keepdims=True)
            o = _dot(p.astype(BF16), v_ref[:, vsl])
            if with_ctx:
                den = den + jnp.sum(pc, axis=-1, keepdims=True)
                o = o + _dot(pc.astype(BF16), vc_ref[:, vsl])
            outs.append(o / den)
        o_ref[:, vsl] = jnp.where(lane < V_HEAD, outs[0], outs[1]).astype(BF16)


def _attention(q, k, v, kc=None, vc=None):
    n_b, n_tok, _ = q.shape
    with_ctx = kc is not None
    qspec = lambda width: pl.BlockSpec((None, ROWS, width), lambda b, i: (b, i, 0))
    kvspec = lambda n, width: pl.BlockSpec((None, n, width), lambda b, i: (b, 0, 0))
    in_specs = [qspec(N_HEADS * HEAD_PAD), kvspec(n_tok, N_HEADS * HEAD_PAD), kvspec(n_tok, N_HEADS * V_HEAD)]
    args = [q, k, v]
    if with_ctx:
        n_ctx = kc.shape[1]
        in_specs += [kvspec(n_ctx, N_HEADS * HEAD_PAD), kvspec(n_ctx, N_HEADS * V_HEAD)]
        args += [kc, vc]
    return pl.pallas_call(
        functools.partial(_attn_kernel, with_ctx=with_ctx),
        out_shape=jax.ShapeDtypeStruct((n_b, n_tok, N_HEADS * V_HEAD), BF16),
        grid=(n_b, n_tok // ROWS), in_specs=in_specs, out_specs=qspec(N_HEADS * V_HEAD),
        compiler_params=_params(2), name="attention",
    )(*args)


def _moe_front(x1, mod_ref, n2_ref, wrh_ref, wrl_ref):
    sh2, sc2 = mod_ref[3:4, :], mod_ref[4:5, :]
    h2 = _rms(x1, n2_ref[...]) * (1.0 + sc2) + sh2
    h_hi, h_lo = _split_hi_lo(h2)
    logit = _dot_nt(wrh_ref[...], h_hi) + _dot_nt(wrh_ref[...], h_lo) + _dot_nt(wrl_ref[...], h_hi)
    e = jnp.exp(logit - jnp.max(logit, axis=0, keepdims=True))
    return h_hi, e / jnp.sum(e, axis=0, keepdims=True)


def _post0_kernel(x_ref, attn_ref, conv_ref, wo_ref, mod_ref, n2_ref, wrh_ref, wrl_ref,
                  x1_ref, h2_ref, aff_ref):
    n_attn = N_HEADS * V_HEAD
    mix = _dot(attn_ref[...], wo_ref[0:n_attn, :]) + _dot(conv_ref[...], wo_ref[n_attn:, :])
    x1 = x_ref[...] + mod_ref[2:3, :] * mix
    x1_ref[...] = x1
    h2, aff = _moe_front(x1, mod_ref, n2_ref, wrh_ref, wrl_ref)
    h2_ref[...] = h2
    aff_ref[...] = aff


def _post0(x, attn, conv, mod, w, n_tok):
    n_all = x.shape[0]
    tiles_per_seq = n_tok // ROWS
    shared_mod = mod.shape[0] == 1
    tile = lambda width: pl.BlockSpec((ROWS, width), lambda i: (i, 0))
    const = lambda shape: pl.BlockSpec(shape, lambda i: (0,) * len(shape))
    mod_map = (lambda i: (0, 0, 0)) if shared_mod else (lambda i: (i // tiles_per_seq, 0, 0))
    return pl.pallas_call(
        _post0_kernel,
        out_shape=[jax.ShapeDtypeStruct((n_all, D_MODEL), F32),
                   jax.ShapeDtypeStruct((n_all, D_MODEL), BF16),
                   jax.ShapeDtypeStruct((n_all // ROWS, N_EXPERTS, ROWS), F32)],
        grid=(n_all // ROWS,),
        in_specs=[tile(D_MODEL), tile(N_HEADS * V_HEAD), tile(CONV_CH), const((D_MODEL, D_MODEL)),
                  pl.BlockSpec((None, 6, D_MODEL), mod_map), const((1, D_MODEL)),
                  const((N_EXPERTS, D_MODEL)), const((N_EXPERTS, D_MODEL))],
        out_specs=[tile(D_MODEL), tile(D_MODEL), pl.BlockSpec((None, N_EXPERTS, ROWS), lambda i: (i, 0, 0))],
        compiler_params=_params(1), name="post0",
    )(x, attn, conv, w["w_o"], mod, w["norm2"], w["wr_hi"], w["wr_lo"])


def _fnet_kernel(x_ref, mod_ref, n1_ref, cc_ref, sc_ref, dft_ref, wf_ref, n2_ref, wrh_ref, wrl_ref,
                 x1_ref, h2_ref, aff_ref, y_s, *, n_tok):
    sh1, sc1, g1 = mod_ref[0:1, :], mod_ref[1:2, :], mod_ref[2:3, :]
    n_chunks = n_tok // ROWS
    cc, sc = cc_ref[...].astype(BF16), sc_ref[...].astype(BF16)
    for c in range(n_chunks):
        rows = pl.ds(c * ROWS, ROWS)
        h = (_rms(x_ref[rows, :], n1_ref[...]) * (1.0 + sc1) + sh1).astype(BF16)
        for g in range(FNET_GROUPS):
            sl = slice(g * FNET_CH, (g + 1) * FNET_CH)
            y_s[pl.ds(c * ROWS, ROWS), sl] = _dot(h[:, sl], cc).astype(BF16)
            y_s[pl.ds(n_tok + c * ROWS, ROWS), sl] = _dot(h[:, sl], sc).astype(BF16)
    scale = 1.0 / float(np.sqrt(n_tok * FNET_CH))
    for c in range(n_chunks):
        rows = pl.ds(c * ROWS, ROWS)
        f = _dot(dft_ref[rows, :].astype(BF16), y_s[...]) * scale
        x1 = x_ref[rows, :] + g1 * _dot(f.astype(BF16), wf_ref[...])
        x1_ref[rows, :] = x1
        h2, aff = _moe_front(x1, mod_ref, n2_ref, wrh_ref, wrl_ref)
        h2_ref[rows, :] = h2
        aff_ref[c] = aff


def _fnet(x, mod, w, dft):
    n_b, n_tok, _ = x.shape
    shared_mod = mod.shape[0] == 1
    seq = lambda width: pl.BlockSpec((None, n_tok, width), lambda b: (b, 0, 0))
    cc, sc, dft_n = dft
    tiles = n_tok // ROWS
    return pl.pallas_call(
        functools.partial(_fnet_kernel, n_tok=n_tok),
        out_shape=[jax.ShapeDtypeStruct((n_b, n_tok, D_MODEL), F32),
                   jax.ShapeDtypeStruct((n_b, n_tok, D_MODEL), BF16),
                   jax.ShapeDtypeStruct((n_b * tiles, N_EXPERTS, ROWS), F32)],
        grid=(n_b,),
        in_specs=[seq(D_MODEL),
                  pl.BlockSpec((None, 6, D_MODEL), (lambda b: (0, 0, 0)) if shared_mod else (lambda b: (b, 0, 0))),
                  _const_spec((1, D_MODEL)), _const_spec((FNET_CH, FNET_CH)), _const_spec((FNET_CH, FNET_CH)),
                  pl.BlockSpec((n_tok, 2 * n_tok), lambda b: (0, 0), pipeline_mode=pl.Buffered(1)),
                  _const_spec((D_MODEL, D_MODEL)), _const_spec((1, D_MODEL)),
                  _const_spec((N_EXPERTS, D_MODEL)), _const_spec((N_EXPERTS, D_MODEL))],
        out_specs=[seq(D_MODEL), seq(D_MODEL), pl.BlockSpec((tiles, N_EXPERTS, ROWS), lambda b: (b, 0, 0))],
        scratch_shapes=[pltpu.VMEM((2 * n_tok, D_MODEL), BF16)],
        compiler_params=_params(1), name="fnet",
    )(x, mod, w["norm1"], cc, sc, dft_n, w["w_f"], w["norm2"], w["wr_hi"], w["wr_lo"])


def _route_kernel(aff_ref, tri_ref, posm_ref, s_ref, *, n_blk, cap):
    def count(pred):
        acc = jnp.zeros((N_EXPERTS, ROWS), F32)
        for b in range(n_blk):
            acc = acc + jnp.where(pred(aff_ref[b]), 1.0, 0.0)
        return jnp.sum(acc, axis=1, keepdims=True)

    def as_f32(bits):
        return pltpu.bitcast(bits, F32)

    def search(_, carry):
        lo, hi = carry
        mid = lo + ((hi - lo) >> 1)
        mid_f = as_f32(mid)
        ok = count(lambda a: a >= mid_f) >= cap
        return jnp.where(ok, mid, lo), jnp.where(ok, hi, mid)

    one_bits = 0x3F800000
    lo0 = jnp.zeros((N_EXPERTS, 1), jnp.int32)
    hi0 = jnp.full((N_EXPERTS, 1), one_bits + 1, jnp.int32)
    lo, hi = lax.fori_loop(0, 31, search, (lo0, hi0))
    lo_f, ub = as_f32(lo), as_f32(hi)
    thr = lo_f
    pending = jnp.ones((N_EXPERTS, 1), F32)
    for _ in range(3):
        cur = jnp.full((N_EXPERTS, ROWS), -1.0, F32)
        for b in range(n_blk):
            a = aff_ref[b]
            cur = jnp.maximum(cur, jnp.where((a >= lo_f) & (a < ub), a, -1.0))
        cur = jnp.max(cur, axis=1, keepdims=True)
        take = (count(lambda a: a >= cur) >= cap) & (pending > 0.0)
        thr = jnp.where(take, cur, thr)
        pending = jnp.where(take, 0.0, pending)
        ub = cur
    need = cap - count(lambda a: a > thr)

    lane = lax.broadcasted_iota(jnp.int32, (N_EXPERTS, LANES), 1)
    carry_tie = jnp.zeros((N_EXPERTS, 1), F32)
    carry_pos = jnp.zeros((N_EXPERTS, 1), F32)
    s_acc = jnp.zeros((N_EXPERTS, LANES), jnp.int32)
    for b in range(n_blk):
        v = aff_ref[b]
        eq = jnp.where(v == thr, 1.0, 0.0)
        tie_rank = _dot(eq.astype(BF16), tri_ref[...]) + carry_tie
        sel = jnp.where((v > thr) | ((v == thr) & (tie_rank < need)), 1.0, 0.0)
        pos = _dot(sel.astype(BF16), tri_ref[...]) + carry_pos
        posm_ref[b] = jnp.where(sel > 0.0, pos, -1.0)
        s_acc = jnp.where(lane == b, carry_pos.astype(jnp.int32), s_acc)
        carry_tie = carry_tie + jnp.sum(eq, axis=1, keepdims=True)
        n_sel = jnp.sum(sel, axis=1, keepdims=True)
        carry_pos = carry_pos + jnp.floor((n_sel + (ALIGN - 1)) * (1.0 / ALIGN)) * ALIGN
    s_ref[...] = jnp.where(lane == n_blk, carry_pos.astype(jnp.int32), s_acc)


def _route(aff, tri, cap):
    n_blk = aff.shape[0]
    assert n_blk < LANES
    full = lambda shape: pl.BlockSpec(shape, lambda i: (0,) * len(shape))
    posm, s_tab = pl.pallas_call(
        functools.partial(_route_kernel, n_blk=n_blk, cap=cap),
        out_shape=[jax.ShapeDtypeStruct((n_blk, N_EXPERTS, ROWS), F32),
                   jax.ShapeDtypeStruct((N_EXPERTS, LANES), jnp.int32)],
        grid=(1,),
        in_specs=[full((n_blk, N_EXPERTS, ROWS)), full((ROWS, ROWS))],
        out_specs=[full((n_blk, N_EXPERTS, ROWS)), full((N_EXPERTS, LANES))],
        compiler_params=_params(1), name="route",
    )(aff, tri)
    return posm, s_tab[:, :n_blk + 1]


def _block_segments(s_ref, b):
    base = [s_ref[e, b] for e in range(N_EXPERTS)]
    count = [s_ref[e, b + 1] - base[e] for e in range(N_EXPERTS)]
    longest = functools.reduce(jnp.maximum, count)
    return base, count, (longest + (WIN - 1)) // WIN


def _window_onehot(posm_ref, e, first, lo):
    slot = lax.broadcasted_iota(jnp.int32, (WIN, ROWS), 0) + first
    p = posm_ref[e:e + 1, :].astype(jnp.int32)
    return jnp.where((p == slot) & (p >= lo), 1.0, 0.0)


def _gate_lanes(e):
    lane = lax.broadcasted_iota(jnp.int32, (1, LANES), 1)
    return ((lane & (N_EXPERTS - 1)) == e) & (lane < 3 * N_EXPERTS)


def _dispatch_kernel(s_ref, h_ref, posm_ref, aff_ref, xe_ref, sel_s, x_s, sem, *, n_blk, cap_pad):
    b = pl.program_id(0)
    buf = b % 2
    base, _, rounds = _block_segments(s_ref, b)

    a = jnp.concatenate([aff_ref[...], jnp.zeros((LANES - N_EXPERTS, ROWS), F32)], axis=0).T
    a_hi = a.astype(BF16).astype(F32)
    a_mid = (a - a_hi).astype(BF16).astype(F32)
    a_lo = a - a_hi - a_mid
    parts = (a_hi + pltpu.roll(a_mid, N_EXPERTS, 1) + pltpu.roll(a_lo, 2 * N_EXPERTS, 1)).astype(BF16)

    def window_copy(slot, blk, e, r):
        first = pl.multiple_of(s_ref[e, blk] + r * WIN, ALIGN)
        return pltpu.make_async_copy(x_s.at[slot, pl.ds(e * WIN, WIN), :],
                                     xe_ref.at[e, pl.ds(first, WIN), :], sem.at[slot])

    def fill(r):
        for e in range(N_EXPERTS):
            first = base[e] + r * WIN
            sel_s[e * WIN:(e + 1) * WIN, :] = _window_onehot(posm_ref, e, first, first).astype(BF16)
        sel = sel_s[...]
        x_s[buf, :, 0:D_MODEL] = _dot(sel, h_ref[...]).astype(BF16)
        x_s[buf, :, D_MODEL:XE_W] = _dot(sel, parts).astype(BF16)

    def wait_all(slot, blk):
        for e in range(N_EXPERTS):
            window_copy(slot, blk, e, 0).wait()

    fill(0)

    @pl.when(b > 0)
    def _():
        wait_all(1 - buf, b - 1)

    for e in range(N_EXPERTS):
        window_copy(buf, b, e, 0).start()

    def more(r, _):
        wait_all(buf, b)
        fill(r)
        for e in range(N_EXPERTS):
            window_copy(buf, b, e, r).start()
        return 0

    lax.fori_loop(1, rounds, more, 0)

    @pl.when(b == n_blk - 1)
    def _():
        wait_all(buf, b)
        x_s[buf, 0:WIN, :] = jnp.zeros((WIN, XE_W), BF16)

        def tail(fn):
            for e in range(N_EXPERTS):
                total = s_ref[e, n_blk]
                n_win = (cap_pad - total) // WIN

                def wide(c, _):
                    row = pl.multiple_of(total + c * WIN, ALIGN)
                    fn(pltpu.make_async_copy(x_s.at[buf, pl.ds(0, WIN), :],
                                             xe_ref.at[e, pl.ds(row, WIN), :], sem.at[buf]))
                    return 0

                def narrow(c, _):
                    row = pl.multiple_of(total + n_win * WIN + c * ALIGN, ALIGN)
                    fn(pltpu.make_async_copy(x_s.at[buf, pl.ds(0, ALIGN), :],
                                             xe_ref.at[e, pl.ds(row, ALIGN), :], sem.at[buf]))
                    return 0

                lax.fori_loop(0, n_win, wide, 0)
                lax.fori_loop(0, (cap_pad - total - n_win * WIN) // ALIGN, narrow, 0)

        tail(lambda c: c.start())
        tail(lambda c: c.wait())


def _ffn_kernel(s_ref, x_ref, wg_ref, wu_ref, wd_ref, y_ref, wg_s, wu_s, wd_s, *, n_blk, n_tiles):
    e = pl.program_id(0)
    wg_s[...] = wg_ref[...].astype(BF16)
    wu_s[...] = wu_ref[...].astype(BF16)
    wd_s[...] = wd_ref[...].astype(BF16)
    n_used = (s_ref[e, n_blk] + (ROWS - 1)) // ROWS
    mine = _gate_lanes(e)

    def tile(j, _):
        rows = pl.ds(pl.multiple_of(j * ROWS, ROWS), ROWS)
        x = x_ref[rows, 0:D_MODEL]
        pieces = x_ref[rows, D_MODEL:XE_W].astype(F32)
        gate = jnp.sum(jnp.where(mine, pieces, 0.0), axis=1, keepdims=True)
        a = _dot(x, wg_s[...])
        u = _dot(x, wu_s[...])
        hid = a * (1.0 / (1.0 + jnp.exp(-a))) * u * gate
        y_ref[rows, :] = _dot(hid.astype(BF16), wd_s[...]).astype(BF16)
        return 0

    def blank(j, _):
        y_ref[pl.ds(pl.multiple_of(j * ROWS, ROWS), ROWS), :] = jnp.zeros((ROWS, D_MODEL), BF16)
        return 0

    lax.fori_loop(0, n_used, tile, 0)
    lax.fori_loop(n_used, n_tiles, blank, 0)


def _combine_kernel(s_ref, y_ref, posm_ref, x1_ref, mod_ref, o_ref, yw_s, sel_s, acc_s, sem,
                    *, n_blk, cap_pad):
    b = pl.program_id(0)
    buf = b % 2
    _, _, rounds = _block_segments(s_ref, b)

    def bounds(blk, e, r):
        lo = s_ref[e, blk] + r * WIN
        return lo, jnp.minimum(lo, cap_pad - WIN)

    def window_copy(slot, blk, e, r):
        first = pl.multiple_of(bounds(blk, e, r)[1], ALIGN)
        return pltpu.make_async_copy(y_ref.at[e, pl.ds(first, WIN), :],
                                     yw_s.at[slot, pl.ds(e * WIN, WIN), :], sem.at[slot])

    def fetch(slot, blk, r):
        for e in range(N_EXPERTS):
            window_copy(slot, blk, e, r).start()

    def add_round(r):
        for e in range(N_EXPERTS):
            lo, first = bounds(b, e, r)
            sel_s[e * WIN:(e + 1) * WIN, :] = _window_onehot(posm_ref, e, first, lo)
        for e in range(N_EXPERTS):
            window_copy(buf, b, e, r).wait()
        return _dot(sel_s[...].T.astype(BF16), yw_s[buf])

    @pl.when(b == 0)
    def _():
        fetch(0, 0, 0)

    @pl.when(b + 1 < n_blk)
    def _():
        fetch(1 - buf, b + 1, 0)

    acc_s[...] = add_round(0)

    def more(r, _):
        fetch(buf, b, r)
        acc_s[...] += add_round(r)
        return 0

    lax.fori_loop(1, rounds, more, 0)
    o_ref[...] = x1_ref[...] + mod_ref[5:6, :] * acc_s[...]


def _padded_capacity(cap, n_blk):
    return -(-(cap + (ALIGN - 1) * n_blk + WIN) // ROWS) * ROWS


def _expert_ffn(x1, h2, posm, aff, s_tab, mod, wg, wu, wd, cap, n_tok):
    n_all = h2.shape[0]
    n_blk = n_all // ROWS
    cap_pad = _padded_capacity(cap, n_blk)
    blk = lambda *shape: pl.BlockSpec((None,) + shape, lambda b, s: (b,) + (0,) * len(shape))
    xe = pl.pallas_call(
        functools.partial(_dispatch_kernel, n_blk=n_blk, cap_pad=cap_pad),
        out_shape=jax.ShapeDtypeStruct((N_EXPERTS, cap_pad, XE_W), BF16),
        grid_spec=pltpu.PrefetchScalarGridSpec(
            num_scalar_prefetch=1, grid=(n_blk,),
            in_specs=[pl.BlockSpec((ROWS, D_MODEL), lambda b, s: (b, 0)), blk(N_EXPERTS, ROWS), blk(N_EXPERTS, ROWS)],
            out_specs=pl.BlockSpec(memory_space=pl.ANY),
            scratch_shapes=[pltpu.VMEM((N_EXPERTS * WIN, ROWS), BF16), pltpu.VMEM((2, N_EXPERTS * WIN, XE_W), BF16),
                            pltpu.SemaphoreType.DMA((2,))]),
        compiler_params=_params(1), name="dispatch",
    )(s_tab, h2, posm, aff)

    per_expert = lambda rows, cols: pl.BlockSpec((None, rows, cols), lambda e, s: (e, 0, 0))
    y = pl.pallas_call(
        functools.partial(_ffn_kernel, n_blk=n_blk, n_tiles=cap_pad // ROWS),
        out_shape=jax.ShapeDtypeStruct((N_EXPERTS, cap_pad, D_MODEL), BF16),
        grid_spec=pltpu.PrefetchScalarGridSpec(
            num_scalar_prefetch=1, grid=(N_EXPERTS,),
            in_specs=[per_expert(cap_pad, XE_W), per_expert(D_MODEL, EXPERT_FF), per_expert(D_MODEL, EXPERT_FF),
                      per_expert(EXPERT_FF, D_MODEL)],
            out_specs=per_expert(cap_pad, D_MODEL),
            scratch_shapes=[pltpu.VMEM((D_MODEL, EXPERT_FF), BF16), pltpu.VMEM((D_MODEL, EXPERT_FF), BF16),
                            pltpu.VMEM((EXPERT_FF, D_MODEL), BF16)]),
        compiler_params=_params(1), name="ffn",
    )(s_tab, xe, wg, wu, wd)

    tiles_per_seq = n_tok // ROWS
    mod_map = (lambda b, s: (0, 0, 0)) if mod.shape[0] == 1 else (lambda b, s: (b // tiles_per_seq, 0, 0))
    rows = pl.BlockSpec((ROWS, D_MODEL), lambda b, s: (b, 0))
    return pl.pallas_call(
        functools.partial(_combine_kernel, n_blk=n_blk, cap_pad=cap_pad),
        out_shape=jax.ShapeDtypeStruct((n_all, D_MODEL), F32),
        grid_spec=pltpu.PrefetchScalarGridSpec(
            num_scalar_prefetch=1, grid=(n_blk,),
            in_specs=[pl.BlockSpec(memory_space=pl.ANY), blk(N_EXPERTS, ROWS), rows,
                      pl.BlockSpec((None, 6, D_MODEL), mod_map)],
            out_specs=rows,
            scratch_shapes=[pltpu.VMEM((2, N_EXPERTS * WIN, D_MODEL), BF16), pltpu.VMEM((N_EXPERTS * WIN, ROWS), F32),
                            pltpu.VMEM((ROWS, D_MODEL), F32), pltpu.SemaphoreType.DMA((2,))]),
        compiler_params=_params(1), name="combine",
    )(s_tab, y, posm, x1, mod)


def _rope_tables(n_tok):
    rows = n_tok // GRID_W
    row = np.repeat(np.arange(rows, dtype=np.float64), GRID_W)
    col = np.tile(np.arange(GRID_W, dtype=np.float64), rows)
    axis_dim = QK_ROPE // 2
    inv_freq = ROPE_THETA ** (-np.arange(0, axis_dim, 2, dtype=np.float64) / axis_dim)
    ang = np.concatenate([row[:, None] * inv_freq, col[:, None] * inv_freq], axis=-1)
    cos = np.ones((n_tok, LANES))
    sin = np.zeros((n_tok, LANES))
    cos[:, ROPE_LANE0:ROPE_LANE0 + QK_ROPE] = np.repeat(np.cos(ang), 2, axis=1)
    sgn = np.tile(np.array([-1.0, 1.0]), QK_ROPE // 2)
    sin[:, ROPE_LANE0:ROPE_LANE0 + QK_ROPE] = np.repeat(np.sin(ang), 2, axis=1) * sgn
    return jnp.asarray(cos, F32), jnp.asarray(sin, F32)


def _dft_tables(n_tok):
    def cs(n):
        k = np.arange(n)
        ang = 2.0 * np.pi * ((k[:, None] * k[None, :]) % n) / n
        return np.cos(ang), np.sin(ang)
    cc, sc = cs(FNET_CH)
    cn, sn = cs(n_tok)
    return tuple(jnp.asarray(t, F32) for t in (cc, sc, np.concatenate([cn, -sn], axis=1)))


def _pad_heads(w, width):
    lead = w.shape[:-1]
    w = w.reshape(lead + (N_HEADS, width))
    w = jnp.pad(w, [(0, 0)] * len(lead) + [(0, 0), (0, HEAD_PAD - width)])
    return w.reshape(lead + (N_HEADS * HEAD_PAD,))


def _layer0_weights(norm1, norm2, w_in, q_a_norm, w_q_up, q_norm, kv_a_norm, w_kv_up, k_norm, conv_w, w_o):
    c0 = Q_LORA + KV_LORA
    rope_cols = jnp.pad(w_in[:, c0:c0 + QK_ROPE], ((0, 0), (ROPE_LANE0, LANES - ROPE_LANE0 - QK_ROPE)))
    w_in_pad = jnp.concatenate([w_in[:, :c0], rope_cols, w_in[:, c0 + QK_ROPE:]], axis=1)
    kv = w_kv_up.reshape(KV_LORA, N_HEADS, QK_NOPE + V_HEAD)
    head_gain = lambda g: jnp.tile(jnp.pad(g, (0, HEAD_PAD - QK_DIM)), N_HEADS).reshape(1, -1)
    return dict(
        norm1=norm1.reshape(1, -1), norm2=norm2.reshape(1, -1), w_in=w_in_pad.astype(BF16),
        q_a_norm=q_a_norm.reshape(1, -1), w_q=_pad_heads(w_q_up, QK_DIM).astype(BF16),
        q_norm=head_gain(q_norm) * (QK_DIM ** -0.5), kv_a_norm=kv_a_norm.reshape(1, -1),
        w_k=_pad_heads(kv[:, :, :QK_NOPE].reshape(KV_LORA, -1), QK_NOPE).astype(BF16),
        w_v=kv[:, :, QK_NOPE:].reshape(KV_LORA, -1).astype(BF16), k_norm=head_gain(k_norm),
        conv_w=conv_w, w_o=w_o.astype(BF16))


def _router_weights(w_router):
    hi, lo = _split_hi_lo(w_router.T)
    return dict(wr_hi=hi, wr_lo=lo)


def _moe(x1, h2, aff, mod, tri, wg, wu, wd, n_tok):
    n_all = x1.shape[0]
    cap = CAPACITY_FACTOR * n_all // N_EXPERTS
    posm, s_tab = _route(aff, tri, cap)
    return _expert_ffn(x1, h2, posm, aff, s_tab, mod, wg, wu, wd, cap, n_tok)


def _trunk(x, mods, l0, l1, tri, ctx, rope_tabs, dft):
    n_b, n_tok, _ = x.shape
    mod0, mod1 = mods
    q, k, v, conv, ckv, kr = _front0(x, mod0, l0, rope_tabs)
    if ctx is not None:
        kc, vc = _ctx_kv(ctx[0], ctx[1], l0)
        attn = _attention(q, k, v, kc, vc)
    else:
        attn = _attention(q, k, v)
    flat = lambda a: a.reshape(n_b * n_tok, a.shape[-1])
    x1, h2, aff = _post0(flat(x), flat(attn), flat(conv), mod0, l0, n_tok)
    x = _moe(x1, h2, aff, mod0, tri, l0["wg"], l0["wu"], l0["wd"], n_tok)
    x1, h2, aff = _fnet(x.reshape(n_b, n_tok, D_MODEL), mod1, l1, dft)
    x = _moe(flat(x1), flat(h2), aff, mod1, tri, l1["wg"], l1["wu"], l1["wd"], n_tok)
    return x.reshape(n_b, n_tok, D_MODEL), ckv, kr[:, :, ROPE_LANE0:ROPE_LANE0 + QK_ROPE]


def kernel(x_prompt, x_sample, c, cache_c_kv_l0, cache_k_rope_l0, c_ctx, norm1_l0, norm2_l0, w_mod_l0, b_mod_l0, w_in_l0, q_a_norm_l0, w_q_up_l0, q_norm_l0, kv_a_norm_l0, w_kv_up_l0, k_norm_l0, conv_w_l0, w_o_l0, w_router_l0, w_gate_l0, w_up_l0, w_down_l0, norm1_l1, norm2_l1, w_mod_l1, b_mod_l1, w_f_l1, w_router_l1, w_gate_l1, w_up_l1, w_down_l1):
    n_dec = c.shape[0]
    cond = jnp.concatenate([c_ctx[None, :], c, jnp.zeros((16 - 1 - n_dec, D_MODEL), F32)], axis=0)
    m0 = _modulation(cond, w_mod_l0, b_mod_l0)
    m1 = _modulation(cond, w_mod_l1, b_mod_l1)
    mods_prompt = (m0[0:1], m1[0:1])
    mods_sample = (m0[1:1 + n_dec], m1[1:1 + n_dec])

    l0 = _layer0_weights(norm1_l0, norm2_l0, w_in_l0, q_a_norm_l0, w_q_up_l0, q_norm_l0, kv_a_norm_l0,
                         w_kv_up_l0, k_norm_l0, conv_w_l0, w_o_l0)
    l0.update(_router_weights(w_router_l0))
    l0.update(wg=w_gate_l0, wu=w_up_l0, wd=w_down_l0)
    l1 = dict(norm1=norm1_l1.reshape(1, -1), norm2=norm2_l1.reshape(1, -1), w_f=w_f_l1.astype(BF16))
    l1.update(_router_weights(w_router_l1))
    l1.update(wg=w_gate_l1, wu=w_up_l1, wd=w_down_l1)

    tri = jnp.asarray(np.triu(np.ones((ROWS, ROWS)), 1), BF16)
    y_prompt, new_c_kv, new_k_rope = _trunk(x_prompt, mods_prompt, l0, l1, tri, None, None,
                                            _dft_tables(x_prompt.shape[1]))
    k_rope_pad = jnp.pad(cache_k_rope_l0, ((0, 0), (0, 0), (ROPE_LANE0, LANES - ROPE_LANE0 - QK_ROPE)))
    y_sample, _, _ = _trunk(x_sample, mods_sample, l0, l1, tri, (cache_c_kv_l0, k_rope_pad),
                            _rope_tables(x_sample.shape[1]), _dft_tables(x_sample.shape[1]))
    return (y_prompt, y_sample, new_c_kv, new_k_rope)
```

```python
import functools

import jax
import jax.numpy as jnp
import numpy as np
from jax import lax
from jax.experimental import pallas as pl
from jax.experimental.pallas import tpu as pltpu

D_MODEL = 1024
GRID_W = 64
N_HEADS = 8
QK_NOPE = 64
QK_ROPE = 32
QK_DIM = QK_NOPE + QK_ROPE
V_HEAD = 64
Q_LORA = 384
KV_LORA = 256
CONV_CH = 512
FNET_GROUPS = 4
FNET_CH = D_MODEL // FNET_GROUPS
N_EXPERTS = 16
EXPERT_FF = 512
CAPACITY_FACTOR = 2
ROPE_THETA = 10000.0
EPS = 1e-6

LANES = 128
HEAD_PAD = LANES
ROWS = 256
ALIGN = 16
WIN = 64
MAX_UNITS = N_EXPERTS * (ROWS // WIN)
XE_W = D_MODEL + LANES
IN0_PAD = Q_LORA + KV_LORA + LANES + 3 * CONV_CH
ROPE_LANE0 = QK_NOPE
VMEM_LIMIT = 56 * 1024 * 1024

F32 = jnp.float32
BF16 = jnp.bfloat16


def _dot(a, b):
    return jnp.dot(a, b, preferred_element_type=F32)


def _dot_nt(a, b):
    return lax.dot_general(a, b, (((1,), (1,)), ((), ())), preferred_element_type=F32)


def _split_hi_lo(x):
    hi = x.astype(BF16)
    lo = (x - hi.astype(F32)).astype(BF16)
    return hi, lo


def _params(n_axes):
    return pltpu.CompilerParams(dimension_semantics=("arbitrary",) * n_axes,
                                vmem_limit_bytes=VMEM_LIMIT)


def _rms(x, gain):
    return x * lax.rsqrt(jnp.mean(x * x, axis=-1, keepdims=True) + EPS) * gain


def _mod_kernel(cond_ref, w_ref, b_ref, o_ref):
    c = cond_ref[...]
    s = c * (1.0 / (1.0 + jnp.exp(-c)))
    s_hi, s_lo = _split_hi_lo(s)
    w_hi, w_lo = _split_hi_lo(w_ref[...])
    o_ref[...] = _dot(s_hi, w_hi) + _dot(s_lo, w_hi) + _dot(s_hi, w_lo) + b_ref[...]


def _modulation(cond, w_mod, b_mod):
    n_rows = cond.shape[0]
    tn = 1536
    out = pl.pallas_call(
        _mod_kernel,
        out_shape=jax.ShapeDtypeStruct((n_rows, 6 * D_MODEL), F32),
        grid=(6 * D_MODEL // tn,),
        in_specs=[pl.BlockSpec((n_rows, D_MODEL), lambda i: (0, 0)),
                  pl.BlockSpec((D_MODEL, tn), lambda i: (0, i)),
                  pl.BlockSpec((1, tn), lambda i: (0, i))],
        out_specs=pl.BlockSpec((n_rows, tn), lambda i: (0, i)),
        compiler_params=_params(1),
        name="modulation",
    )(cond, w_mod, b_mod.reshape(1, -1))
    return out.reshape(n_rows, 6, D_MODEL)


def _swap_pairs(x):
    lane = lax.broadcasted_iota(jnp.int32, x.shape, 1)
    return jnp.where((lane & 1) == 0, pltpu.roll(x, LANES - 1, 1), pltpu.roll(x, 1, 1))


def _head_norm_rope(xh, gain, cos, sin):
    ss = jnp.sum(xh * xh, axis=-1, keepdims=True) * (1.0 / QK_DIM)
    y = xh * lax.rsqrt(ss + EPS) * gain
    if cos is not None:
        y = y * cos + _swap_pairs(y) * sin
    return y


def _expand_kv(ckv_bf, r, wk_ref, wv_ref, kn_ref, cos, sin, k_ref, v_ref, rows):
    kf = _dot(ckv_bf, wk_ref[...])
    v_ref[rows, :] = _dot(ckv_bf, wv_ref[...]).astype(BF16)
    for h in range(N_HEADS):
        sl = slice(h * HEAD_PAD, (h + 1) * HEAD_PAD)
        kh = _head_norm_rope(kf[:, sl] + r, kn_ref[:, sl], cos, sin)
        k_ref[rows, sl] = kh.astype(BF16)


def _front0_kernel(*refs, n_tok, rope):
    (x_ref, mod_ref, n1_ref, win_ref, qan_ref, wq_ref, qn_ref, kvan_ref, wk_ref, wv_ref,
     kn_ref, cw_ref) = refs[:12]
    pos = 12
    if rope:
        cos_ref, sin_ref = refs[pos:pos + 2]
        pos += 2
    q_ref, k_ref, v_ref, conv_ref, ckv_ref, kr_ref, cu_s, gb_s = refs[pos:]
    sh1, sc1 = mod_ref[0:1, :], mod_ref[1:2, :]
    for c in range(n_tok // ROWS):
        rows = pl.ds(c * ROWS, ROWS)
        cos = cos_ref[rows, :] if rope else None
        sin = sin_ref[rows, :] if rope else None
        h = _rms(x_ref[rows, :], n1_ref[...]) * (1.0 + sc1) + sh1
        proj = _dot(h.astype(BF16), win_ref[...])
        cq = _rms(proj[:, :Q_LORA], qan_ref[...])
        qf = _dot(cq.astype(BF16), wq_ref[...])
        for hd in range(N_HEADS):
            sl = slice(hd * HEAD_PAD, (hd + 1) * HEAD_PAD)
            q_ref[rows, sl] = _head_norm_rope(qf[:, sl], qn_ref[:, sl], cos, sin).astype(BF16)
        ckv = _rms(proj[:, Q_LORA:Q_LORA + KV_LORA], kvan_ref[...])
        ckv_ref[rows, :] = ckv
        r = proj[:, Q_LORA + KV_LORA:Q_LORA + KV_LORA + LANES]
        kr_ref[rows, :] = r
        _expand_kv(ckv.astype(BF16), r, wk_ref, wv_ref, kn_ref, cos, sin, k_ref, v_ref, rows)
        c0 = Q_LORA + KV_LORA + LANES
        gb_s[rows, :] = proj[:, c0:c0 + CONV_CH]
        cu_s[rows, :] = proj[:, c0 + CONV_CH:c0 + 2 * CONV_CH] * proj[:, c0 + 2 * CONV_CH:c0 + 3 * CONV_CH]
    cu = cu_s[...]
    row = lax.broadcasted_iota(jnp.int32, cu.shape, 0)
    prev = jnp.where(row == 0, 0.0, pltpu.roll(cu, 1, 0))
    nxt = jnp.where(row == n_tok - 1, 0.0, pltpu.roll(cu, n_tok - 1, 0))
    conv = gb_s[...] * (cw_ref[0:1, :] * prev + cw_ref[1:2, :] * cu + cw_ref[2:3, :] * nxt)
    conv_ref[...] = conv.astype(BF16)


def _const_spec(shape):
    return pl.BlockSpec(shape, lambda b: (0,) * len(shape))


def _front0(x, mod, w, rope_tabs):
    n_b, n_tok, _ = x.shape
    shared_mod = mod.shape[0] == 1
    seq = lambda width: pl.BlockSpec((None, n_tok, width), lambda b: (b, 0, 0))
    in_specs = [seq(D_MODEL),
                pl.BlockSpec((None, 6, D_MODEL), (lambda b: (0, 0, 0)) if shared_mod else (lambda b: (b, 0, 0))),
                _const_spec((1, D_MODEL)), _const_spec((D_MODEL, IN0_PAD)), _const_spec((1, Q_LORA)),
                _const_spec((Q_LORA, N_HEADS * HEAD_PAD)), _const_spec((1, N_HEADS * HEAD_PAD)),
                _const_spec((1, KV_LORA)), _const_spec((KV_LORA, N_HEADS * HEAD_PAD)),
                _const_spec((KV_LORA, N_HEADS * V_HEAD)), _const_spec((1, N_HEADS * HEAD_PAD)),
                _const_spec((3, CONV_CH))]
    args = [x, mod, w["norm1"], w["w_in"], w["q_a_norm"], w["w_q"], w["q_norm"], w["kv_a_norm"],
            w["w_k"], w["w_v"], w["k_norm"], w["conv_w"]]
    if rope_tabs is not None:
        in_specs += [_const_spec((n_tok, LANES))] * 2
        args += list(rope_tabs)
    out_shape = [jax.ShapeDtypeStruct((n_b, n_tok, N_HEADS * HEAD_PAD), BF16),
                 jax.ShapeDtypeStruct((n_b, n_tok, N_HEADS * HEAD_PAD), BF16),
                 jax.ShapeDtypeStruct((n_b, n_tok, N_HEADS * V_HEAD), BF16),
                 jax.ShapeDtypeStruct((n_b, n_tok, CONV_CH), BF16),
                 jax.ShapeDtypeStruct((n_b, n_tok, KV_LORA), F32),
                 jax.ShapeDtypeStruct((n_b, n_tok, LANES), F32)]
    out_specs = [seq(N_HEADS * HEAD_PAD), seq(N_HEADS * HEAD_PAD), seq(N_HEADS * V_HEAD), seq(CONV_CH),
                 seq(KV_LORA), seq(LANES)]
    return pl.pallas_call(
        functools.partial(_front0_kernel, n_tok=n_tok, rope=rope_tabs is not None),
        out_shape=out_shape, grid=(n_b,), in_specs=in_specs, out_specs=out_specs,
        scratch_shapes=[pltpu.VMEM((n_tok, CONV_CH), F32), pltpu.VMEM((n_tok, CONV_CH), F32)],
        compiler_params=_params(1), name="front0",
    )(*args)


def _ctx_kv_kernel(ckv_ref, r_ref, wk_ref, wv_ref, kn_ref, k_ref, v_ref):
    rows = pl.ds(0, ckv_ref.shape[0])
    _expand_kv(ckv_ref[...].astype(BF16), r_ref[...], wk_ref, wv_ref, kn_ref, None, None, k_ref, v_ref, rows)


def _ctx_kv(cache_c_kv, cache_k_rope_pad, w):
    n_b, n_ctx, _ = cache_c_kv.shape
    seq = lambda width: pl.BlockSpec((None, n_ctx, width), lambda b: (b, 0, 0))
    return pl.pallas_call(
        _ctx_kv_kernel,
        out_shape=[jax.ShapeDtypeStruct((n_b, n_ctx, N_HEADS * HEAD_PAD), BF16),
                   jax.ShapeDtypeStruct((n_b, n_ctx, N_HEADS * V_HEAD), BF16)],
        grid=(n_b,),
        in_specs=[seq(KV_LORA), seq(LANES), _const_spec((KV_LORA, N_HEADS * HEAD_PAD)),
                  _const_spec((KV_LORA, N_HEADS * V_HEAD)), _const_spec((1, N_HEADS * HEAD_PAD))],
        out_specs=[seq(N_HEADS * HEAD_PAD), seq(N_HEADS * V_HEAD)],
        compiler_params=_params(1), name="ctx_kv",
    )(cache_c_kv, cache_k_rope_pad, w["w_k"], w["w_v"], w["k_norm"])


def _attn_kernel(*refs, with_ctx):
    if with_ctx:
        q_ref, k_ref, v_ref, kc_ref, vc_ref, o_ref = refs
    else:
        q_ref, k_ref, v_ref, o_ref = refs
    lane = lax.broadcasted_iota(jnp.int32, (q_ref.shape[0], LANES), 1)
    for pair in range(N_HEADS // 2):
        vsl = slice(pair * LANES, (pair + 1) * LANES)
        outs = []
        for hd in (2 * pair, 2 * pair + 1):
            sl = slice(hd * HEAD_PAD, (hd + 1) * HEAD_PAD)
            qh = q_ref[:, sl]
            s = _dot_nt(qh, k_ref[:, sl])
            m = jnp.max(s, axis=-1, keepdims=True)
            if with_ctx:
                sc = _dot_nt(qh, kc_ref[:, sl])
                m = jnp.maximum(m, jnp.max(sc, axis=-1, keepdims=True))
                pc = jnp.exp(sc - m)
            p = jnp.exp(s - m)
            den = jnp.sum(p, axis=-1, keepdims=True)
            o = _dot(p.astype(BF16), v_ref[:, vsl])
            if with_ctx:
                den = den + jnp.sum(pc, axis=-1, keepdims=True)
                o = o + _dot(pc.astype(BF16), vc_ref[:, vsl])
            outs.append(o / den)
        o_ref[:, vsl] = jnp.where(lane < V_HEAD, outs[0], outs[1]).astype(BF16)


def _attention(q, k, v, kc=None, vc=None):
    n_b, n_tok, _ = q.shape
    with_ctx = kc is not None
    qspec = lambda width: pl.BlockSpec((None, ROWS, width), lambda b, i: (b, i, 0))
    kvspec = lambda n, width: pl.BlockSpec((None, n, width), lambda b, i: (b, 0, 0))
    in_specs = [qspec(N_HEADS * HEAD_PAD), kvspec(n_tok, N_HEADS * HEAD_PAD), kvspec(n_tok, N_HEADS * V_HEAD)]
    args = [q, k, v]
    if with_ctx:
        n_ctx = kc.shape[1]
        in_specs += [kvspec(n_ctx, N_HEADS * HEAD_PAD), kvspec(n_ctx, N_HEADS * V_HEAD)]
        args += [kc, vc]
    return pl.pallas_call(
        functools.partial(_attn_kernel, with_ctx=with_ctx),
        out_shape=jax.ShapeDtypeStruct((n_b, n_tok, N_HEADS * V_HEAD), BF16),
        grid=(n_b, n_tok // ROWS), in_specs=in_specs, out_specs=qspec(N_HEADS * V_HEAD),
        compiler_params=_params(2), name="attention",
    )(*args)


def _moe_front(x1, mod_ref, n2_ref, wrh_ref, wrl_ref):
    sh2, sc2 = mod_ref[3:4, :], mod_ref[4:5, :]
    h2 = _rms(x1, n2_ref[...]) * (1.0 + sc2) + sh2
    h_hi, h_lo = _split_hi_lo(h2)
    logit = _dot_nt(wrh_ref[...], h_hi) + _dot_nt(wrh_ref[...], h_lo) + _dot_nt(wrl_ref[...], h_hi)
    e = jnp.exp(logit - jnp.max(logit, axis=0, keepdims=True))
    return h_hi, e / jnp.sum(e, axis=0, keepdims=True)


def _post0_kernel(x_ref, attn_ref, conv_ref, wo_ref, mod_ref, n2_ref, wrh_ref, wrl_ref,
                  x1_ref, h2_ref, aff_ref):
    n_attn = N_HEADS * V_HEAD
    mix = _dot(attn_ref[...], wo_ref[0:n_attn, :]) + _dot(conv_ref[...], wo_ref[n_attn:, :])
    x1 = x_ref[...] + mod_ref[2:3, :] * mix
    x1_ref[...] = x1
    h2, aff = _moe_front(x1, mod_ref, n2_ref, wrh_ref, wrl_ref)
    h2_ref[...] = h2
    aff_ref[...] = aff


def _post0(x, attn, conv, mod, w, n_tok):
    n_all = x.shape[0]
    tiles_per_seq = n_tok // ROWS
    shared_mod = mod.shape[0] == 1
    tile = lambda width: pl.BlockSpec((ROWS, width), lambda i: (i, 0))
    const = lambda shape: pl.BlockSpec(shape, lambda i: (0,) * len(shape))
    mod_map = (lambda i: (0, 0, 0)) if shared_mod else (lambda i: (i // tiles_per_seq, 0, 0))
    return pl.pallas_call(
        _post0_kernel,
        out_shape=[jax.ShapeDtypeStruct((n_all, D_MODEL), F32),
                   jax.ShapeDtypeStruct((n_all, D_MODEL), BF16),
                   jax.ShapeDtypeStruct((n_all // ROWS, N_EXPERTS, ROWS), F32)],
        grid=(n_all // ROWS,),
        in_specs=[tile(D_MODEL), tile(N_HEADS * V_HEAD), tile(CONV_CH), const((D_MODEL, D_MODEL)),
                  pl.BlockSpec((None, 6, D_MODEL), mod_map), const((1, D_MODEL)),
                  const((N_EXPERTS, D_MODEL)), const((N_EXPERTS, D_MODEL))],
        out_specs=[tile(D_MODEL), tile(D_MODEL), pl.BlockSpec((None, N_EXPERTS, ROWS), lambda i: (i, 0, 0))],
        compiler_params=_params(1), name="post0",
    )(x, attn, conv, w["w_o"], mod, w["norm2"], w["wr_hi"], w["wr_lo"])


def _fnet_kernel(x_ref, mod_ref, n1_ref, cc_ref, sc_ref, dft_ref, wf_ref, n2_ref, wrh_ref, wrl_ref,
                 x1_ref, h2_ref, aff_ref, y_s, *, n_tok):
    sh1, sc1, g1 = mod_ref[0:1, :], mod_ref[1:2, :], mod_ref[2:3, :]
    n_chunks = n_tok // ROWS
    cc, sc = cc_ref[...].astype(BF16), sc_ref[...].astype(BF16)
    for c in range(n_chunks):
        rows = pl.ds(c * ROWS, ROWS)
        h = (_rms(x_ref[rows, :], n1_ref[...]) * (1.0 + sc1) + sh1).astype(BF16)
        for g in range(FNET_GROUPS):
            sl = slice(g * FNET_CH, (g + 1) * FNET_CH)
            y_s[pl.ds(c * ROWS, ROWS), sl] = _dot(h[:, sl], cc).astype(BF16)
            y_s[pl.ds(n_tok + c * ROWS, ROWS), sl] = _dot(h[:, sl], sc).astype(BF16)
    scale = 1.0 / float(np.sqrt(n_tok * FNET_CH))
    for c in range(n_chunks):
        rows = pl.ds(c * ROWS, ROWS)
        f = _dot(dft_ref[rows, :].astype(BF16), y_s[...]) * scale
        x1 = x_ref[rows, :] + g1 * _dot(f.astype(BF16), wf_ref[...])
        x1_ref[rows, :] = x1
        h2, aff = _moe_front(x1, mod_ref, n2_ref, wrh_ref, wrl_ref)
        h2_ref[rows, :] = h2
        aff_ref[c] = aff


def _fnet(x, mod, w, dft):
    n_b, n_tok, _ = x.shape
    shared_mod = mod.shape[0] == 1
    seq = lambda width: pl.BlockSpec((None, n_tok, width), lambda b: (b, 0, 0))
    cc, sc, dft_n = dft
    tiles = n_tok // ROWS
    return pl.pallas_call(
        functools.partial(_fnet_kernel, n_tok=n_tok),
        out_shape=[jax.ShapeDtypeStruct((n_b, n_tok, D_MODEL), F32),
                   jax.ShapeDtypeStruct((n_b, n_tok, D_MODEL), BF16),
                   jax.ShapeDtypeStruct((n_b * tiles, N_EXPERTS, ROWS), F32)],
        grid=(n_b,),
        in_specs=[seq(D_MODEL),
                  pl.BlockSpec((None, 6, D_MODEL), (lambda b: (0, 0, 0)) if shared_mod else (lambda b: (b, 0, 0))),
                  _const_spec((1, D_MODEL)), _const_spec((FNET_CH, FNET_CH)), _const_spec((FNET_CH, FNET_CH)),
                  pl.BlockSpec((n_tok, 2 * n_tok), lambda b: (0, 0), pipeline_mode=pl.Buffered(1)),
                  _const_spec((D_MODEL, D_MODEL)), _const_spec((1, D_MODEL)),
                  _const_spec((N_EXPERTS, D_MODEL)), _const_spec((N_EXPERTS, D_MODEL))],
        out_specs=[seq(D_MODEL), seq(D_MODEL), pl.BlockSpec((tiles, N_EXPERTS, ROWS), lambda b: (b, 0, 0))],
        scratch_shapes=[pltpu.VMEM((2 * n_tok, D_MODEL), BF16)],
        compiler_params=_params(1), name="fnet",
    )(x, mod, w["norm1"], cc, sc, dft_n, w["w_f"], w["norm2"], w["wr_hi"], w["wr_lo"])


def _route_kernel(aff_ref, tri_ref, posm_ref, s_ref, *, n_blk, cap):
    def count(pred):
        acc = jnp.zeros((N_EXPERTS, ROWS), F32)
        for b in range(n_blk):
            acc = acc + jnp.where(pred(aff_ref[b]), 1.0, 0.0)
        return jnp.sum(acc, axis=1, keepdims=True)

    def as_f32(bits):
        return pltpu.bitcast(bits, F32)

    def search(_, carry):
        lo, hi = carry
        mid = lo + ((hi - lo) >> 1)
        mid_f = as_f32(mid)
        ok = count(lambda a: a >= mid_f) >= cap
        return jnp.where(ok, mid, lo), jnp.where(ok, hi, mid)

    one_bits = 0x3F800000
    lo0 = jnp.zeros((N_EXPERTS, 1), jnp.int32)
    hi0 = jnp.full((N_EXPERTS, 1), one_bits + 1, jnp.int32)
    lo, hi = lax.fori_loop(0, 31, search, (lo0, hi0))
    lo_f, ub = as_f32(lo), as_f32(hi)
    thr = lo_f
    pending = jnp.ones((N_EXPERTS, 1), F32)
    for _ in range(3):
        cur = jnp.full((N_EXPERTS, ROWS), -1.0, F32)
        for b in range(n_blk):
            a = aff_ref[b]
            cur = jnp.maximum(cur, jnp.where((a >= lo_f) & (a < ub), a, -1.0))
        cur = jnp.max(cur, axis=1, keepdims=True)
        take = (count(lambda a: a >= cur) >= cap) & (pending > 0.0)
        thr = jnp.where(take, cur, thr)
        pending = jnp.where(take, 0.0, pending)
        ub = cur
    need = cap - count(lambda a: a > thr)

    lane = lax.broadcasted_iota(jnp.int32, (N_EXPERTS, LANES), 1)
    carry_tie = jnp.zeros((N_EXPERTS, 1), F32)
    carry_pos = jnp.zeros((N_EXPERTS, 1), F32)
    s_acc = jnp.zeros((N_EXPERTS, LANES), jnp.int32)
    for b in range(n_blk):
        v = aff_ref[b]
        eq = jnp.where(v == thr, 1.0, 0.0)
        tie_rank = _dot(eq.astype(BF16), tri_ref[...]) + carry_tie
        sel = jnp.where((v > thr) | ((v == thr) & (tie_rank < need)), 1.0, 0.0)
        pos = _dot(sel.astype(BF16), tri_ref[...]) + carry_pos
        posm_ref[b] = jnp.where(sel > 0.0, pos, -1.0)
        s_acc = jnp.where(lane == b, carry_pos.astype(jnp.int32), s_acc)
        carry_tie = carry_tie + jnp.sum(eq, axis=1, keepdims=True)
        n_sel = jnp.sum(sel, axis=1, keepdims=True)
        carry_pos = carry_pos + jnp.floor((n_sel + (ALIGN - 1)) * (1.0 / ALIGN)) * ALIGN
    s_ref[...] = jnp.where(lane == n_blk, carry_pos.astype(jnp.int32), s_acc)


def _route(aff, tri, cap):
    n_blk = aff.shape[0]
    assert n_blk < LANES
    full = lambda shape: pl.BlockSpec(shape, lambda i: (0,) * len(shape))
    posm, s_tab = pl.pallas_call(
        functools.partial(_route_kernel, n_blk=n_blk, cap=cap),
        out_shape=[jax.ShapeDtypeStruct((n_blk, N_EXPERTS, ROWS), F32),
                   jax.ShapeDtypeStruct((N_EXPERTS, LANES), jnp.int32)],
        grid=(1,),
        in_specs=[full((n_blk, N_EXPERTS, ROWS)), full((ROWS, ROWS))],
        out_specs=[full((n_blk, N_EXPERTS, ROWS)), full((N_EXPERTS, LANES))],
        compiler_params=_params(1), name="route",
    )(aff, tri)
    return posm, s_tab[:, :n_blk + 1]


def _block_units(s_ref, b):
    base = [s_ref[e, b] for e in range(N_EXPERTS)]
    units = [(s_ref[e, b + 1] - base[e] + (WIN - 1)) // WIN for e in range(N_EXPERTS)]
    first_unit, total = [], 0
    for e in range(N_EXPERTS):
        first_unit.append(total)
        total = total + units[e]
    return base, units, first_unit, total


def _for_units(units, fn):
    for e in range(N_EXPERTS):
        def body(u, _, e=e):
            fn(e, u)
            return 0
        lax.fori_loop(0, units[e], body, 0)


def _window_onehot(posm_ref, e, first, lo):
    slot = lax.broadcasted_iota(jnp.int32, (WIN, ROWS), 0) + first
    p = posm_ref[e:e + 1, :].astype(jnp.int32)
    return jnp.where((p == slot) & (p >= lo), 1.0, 0.0)


def _gate_lanes(e):
    lane = lax.broadcasted_iota(jnp.int32, (1, LANES), 1)
    return ((lane & (N_EXPERTS - 1)) == e) & (lane < 3 * N_EXPERTS)


def _dispatch_kernel(s_ref, h_ref, posm_ref, aff_ref, xe_ref, sel_s, x_s, sem, *, n_blk, cap_pad):
    b = pl.program_id(0)
    buf = b % 2
    base, units, first_unit, n_units = _block_units(s_ref, b)

    @pl.when(b == 0)
    def _():
        sel_s[...] = jnp.zeros_like(sel_s)

    a = jnp.concatenate([aff_ref[...], jnp.zeros((LANES - N_EXPERTS, ROWS), F32)], axis=0).T
    a_hi = a.astype(BF16).astype(F32)
    a_mid = (a - a_hi).astype(BF16).astype(F32)
    a_lo = a - a_hi - a_mid
    parts = (a_hi + pltpu.roll(a_mid, N_EXPERTS, 1) + pltpu.roll(a_lo, 2 * N_EXPERTS, 1)).astype(BF16)

    def unit_rows(first_unit_e, u):
        return pl.ds(pl.multiple_of((first_unit_e + u) * WIN, WIN), WIN)

    def pick(e, u):
        first = base[e] + u * WIN
        sel_s[unit_rows(first_unit[e], u), :] = _window_onehot(posm_ref, e, first, first).astype(BF16)

    _for_units(units, pick)

    def gather(c, _):
        rows = pl.ds(pl.multiple_of(c * ROWS, ROWS), ROWS)
        sel = sel_s[rows, :]
        x_s[buf, rows, 0:D_MODEL] = _dot(sel, h_ref[...]).astype(BF16)
        x_s[buf, rows, D_MODEL:XE_W] = _dot(sel, parts).astype(BF16)
        return 0

    lax.fori_loop(0, (n_units * WIN + (ROWS - 1)) // ROWS, gather, 0)

    def copies(slot, blk, fn):
        blk_base, blk_units, blk_first, _ = _block_units(s_ref, blk)

        def one(e, u):
            dst = pl.ds(pl.multiple_of(blk_base[e] + u * WIN, ALIGN), WIN)
            fn(pltpu.make_async_copy(x_s.at[slot, unit_rows(blk_first[e], u), :],
                                     xe_ref.at[e, dst, :], sem.at[slot]))

        _for_units(blk_units, one)

    @pl.when(b > 0)
    def _():
        copies(1 - buf, b - 1, lambda c: c.wait())

    copies(buf, b, lambda c: c.start())

    @pl.when(b == n_blk - 1)
    def _():
        copies(buf, b, lambda c: c.wait())
        x_s[buf, 0:WIN, :] = jnp.zeros((WIN, XE_W), BF16)

        def tail(fn):
            for e in range(N_EXPERTS):
                total = s_ref[e, n_blk]
                n_win = (cap_pad - total) // WIN

                def wide(c, _):
                    row = pl.multiple_of(total + c * WIN, ALIGN)
                    fn(pltpu.make_async_copy(x_s.at[buf, pl.ds(0, WIN), :],
                                             xe_ref.at[e, pl.ds(row, WIN), :], sem.at[buf]))
                    return 0

                def narrow(c, _):
                    row = pl.multiple_of(total + n_win * WIN + c * ALIGN, ALIGN)
                    fn(pltpu.make_async_copy(x_s.at[buf, pl.ds(0, ALIGN), :],
                                             xe_ref.at[e, pl.ds(row, ALIGN), :], sem.at[buf]))
                    return 0

                lax.fori_loop(0, n_win, wide, 0)
                lax.fori_loop(0, (cap_pad - total - n_win * WIN) // ALIGN, narrow, 0)

        tail(lambda c: c.start())
        tail(lambda c: c.wait())


def _ffn_kernel(s_ref, x_ref, wg_ref, wu_ref, wd_ref, y_ref, wg_s, wu_s, wd_s, *, n_blk, n_tiles):
    e = pl.program_id(0)
    wg_s[...] = wg_ref[...].astype(BF16)
    wu_s[...] = wu_ref[...].astype(BF16)
    wd_s[...] = wd_ref[...].astype(BF16)
    n_used = (s_ref[e, n_blk] + (ROWS - 1)) // ROWS
    mine = _gate_lanes(e)

    def tile(j, _):
        rows = pl.ds(pl.multiple_of(j * ROWS, ROWS), ROWS)
        x = x_ref[rows, 0:D_MODEL]
        pieces = x_ref[rows, D_MODEL:XE_W].astype(F32)
        gate = jnp.sum(jnp.where(mine, pieces, 0.0), axis=1, keepdims=True)
        a = _dot(x, wg_s[...])
        u = _dot(x, wu_s[...])
        hid = a * (1.0 / (1.0 + jnp.exp(-a))) * u * gate
        y_ref[rows, :] = _dot(hid.astype(BF16), wd_s[...]).astype(BF16)
        return 0

    def blank(j, _):
        y_ref[pl.ds(pl.multiple_of(j * ROWS, ROWS), ROWS), :] = jnp.zeros((ROWS, D_MODEL), BF16)
        return 0

    lax.fori_loop(0, n_used, tile, 0)
    lax.fori_loop(n_used, n_tiles, blank, 0)


def _combine_kernel(s_ref, y_ref, posm_ref, x1_ref, mod_ref, o_ref, yw_s, sel_s, acc_s, sem,
                    *, n_blk, cap_pad):
    b = pl.program_id(0)
    buf = b % 2
    base, units, first_unit, n_units = _block_units(s_ref, b)

    def unit_rows(first_unit_e, u):
        return pl.ds(pl.multiple_of((first_unit_e + u) * WIN, WIN), WIN)

    def bounds(base_e, u):
        lo = base_e + u * WIN
        return lo, jnp.minimum(lo, cap_pad - WIN)

    def copies(slot, blk, fn):
        blk_base, blk_units, blk_first, _ = _block_units(s_ref, blk)

        def one(e, u):
            src = pl.ds(pl.multiple_of(bounds(blk_base[e], u)[1], ALIGN), WIN)
            fn(pltpu.make_async_copy(y_ref.at[e, src, :], yw_s.at[slot, unit_rows(blk_first[e], u), :],
                                     sem.at[slot]))

        _for_units(blk_units, one)

    @pl.when(b == 0)
    def _():
        yw_s[...] = jnp.zeros_like(yw_s)
        copies(0, 0, lambda c: c.start())

    @pl.when(b + 1 < n_blk)
    def _():
        copies(1 - buf, b + 1, lambda c: c.start())

    def pick(e, u):
        lo, first = bounds(base[e], u)
        sel_s[unit_rows(first_unit[e], u), :] = _window_onehot(posm_ref, e, first, lo)

    _for_units(units, pick)
    n_chunks = (n_units * WIN + (ROWS - 1)) // ROWS

    def blank(u, _):
        sel_s[unit_rows(u, 0), :] = jnp.zeros((WIN, ROWS), F32)
        return 0

    lax.fori_loop(n_units, n_chunks * (ROWS // WIN), blank, 0)
    copies(buf, b, lambda c: c.wait())
    acc_s[...] = jnp.zeros_like(acc_s)

    def add(c, _):
        rows = pl.ds(pl.multiple_of(c * ROWS, ROWS), ROWS)
        acc_s[...] += _dot(sel_s[rows, :].T.astype(BF16), yw_s[buf, rows, :])
        return 0

    lax.fori_loop(0, n_chunks, add, 0)
    o_ref[...] = x1_ref[...] + mod_ref[5:6, :] * acc_s[...]


def _padded_capacity(cap, n_blk):
    return -(-(cap + (ALIGN - 1) * n_blk + WIN) // ROWS) * ROWS


def _expert_ffn(x1, h2, posm, aff, s_tab, mod, wg, wu, wd, cap, n_tok):
    n_all = h2.shape[0]
    n_blk = n_all // ROWS
    cap_pad = _padded_capacity(cap, n_blk)
    blk = lambda *shape: pl.BlockSpec((None,) + shape, lambda b, s: (b,) + (0,) * len(shape))
    xe = pl.pallas_call(
        functools.partial(_dispatch_kernel, n_blk=n_blk, cap_pad=cap_pad),
        out_shape=jax.ShapeDtypeStruct((N_EXPERTS, cap_pad, XE_W), BF16),
        grid_spec=pltpu.PrefetchScalarGridSpec(
            num_scalar_prefetch=1, grid=(n_blk,),
            in_specs=[pl.BlockSpec((ROWS, D_MODEL), lambda b, s: (b, 0)), blk(N_EXPERTS, ROWS), blk(N_EXPERTS, ROWS)],
            out_specs=pl.BlockSpec(memory_space=pl.ANY),
            scratch_shapes=[pltpu.VMEM((MAX_UNITS * WIN, ROWS), BF16), pltpu.VMEM((2, MAX_UNITS * WIN, XE_W), BF16),
                            pltpu.SemaphoreType.DMA((2,))]),
        compiler_params=_params(1), name="dispatch",
    )(s_tab, h2, posm, aff)

    per_expert = lambda rows, cols: pl.BlockSpec((None, rows, cols), lambda e, s: (e, 0, 0))
    y = pl.pallas_call(
        functools.partial(_ffn_kernel, n_blk=n_blk, n_tiles=cap_pad // ROWS),
        out_shape=jax.ShapeDtypeStruct((N_EXPERTS, cap_pad, D_MODEL), BF16),
        grid_spec=pltpu.PrefetchScalarGridSpec(
            num_scalar_prefetch=1, grid=(N_EXPERTS,),
            in_specs=[per_expert(cap_pad, XE_W), per_expert(D_MODEL, EXPERT_FF), per_expert(D_MODEL, EXPERT_FF),
                      per_expert(EXPERT_FF, D_MODEL)],
            out_specs=per_expert(cap_pad, D_MODEL),
            scratch_shapes=[pltpu.VMEM((D_MODEL, EXPERT_FF), BF16), pltpu.VMEM((D_MODEL, EXPERT_FF), BF16),
                            pltpu.VMEM((EXPERT_FF, D_MODEL), BF16)]),
        compiler_params=_params(1), name="ffn",
    )(s_tab, xe, wg, wu, wd)

    tiles_per_seq = n_tok // ROWS
    mod_map = (lambda b, s: (0, 0, 0)) if mod.shape[0] == 1 else (lambda b, s: (b // tiles_per_seq, 0, 0))
    rows = pl.BlockSpec((ROWS, D_MODEL), lambda b, s: (b, 0))
    return pl.pallas_call(
        functools.partial(_combine_kernel, n_blk=n_blk, cap_pad=cap_pad),
        out_shape=jax.ShapeDtypeStruct((n_all, D_MODEL), F32),
        grid_spec=pltpu.PrefetchScalarGridSpec(
            num_scalar_prefetch=1, grid=(n_blk,),
            in_specs=[pl.BlockSpec(memory_space=pl.ANY), blk(N_EXPERTS, ROWS), rows,
                      pl.BlockSpec((None, 6, D_MODEL), mod_map)],
            out_specs=rows,
            scratch_shapes=[pltpu.VMEM((2, MAX_UNITS * WIN, D_MODEL), BF16), pltpu.VMEM((MAX_UNITS * WIN, ROWS), F32),
                            pltpu.VMEM((ROWS, D_MODEL), F32), pltpu.SemaphoreType.DMA((2,))]),
        compiler_params=_params(1), name="combine",
    )(s_tab, y, posm, x1, mod)


def _rope_tables(n_tok):
    rows = n_tok // GRID_W
    row = np.repeat(np.arange(rows, dtype=np.float64), GRID_W)
    col = np.tile(np.arange(GRID_W, dtype=np.float64), rows)
    axis_dim = QK_ROPE // 2
    inv_freq = ROPE_THETA ** (-np.arange(0, axis_dim, 2, dtype=np.float64) / axis_dim)
    ang = np.concatenate([row[:, None] * inv_freq, col[:, None] * inv_freq], axis=-1)
    cos = np.ones((n_tok, LANES))
    sin = np.zeros((n_tok, LANES))
    cos[:, ROPE_LANE0:ROPE_LANE0 + QK_ROPE] = np.repeat(np.cos(ang), 2, axis=1)
    sgn = np.tile(np.array([-1.0, 1.0]), QK_ROPE // 2)
    sin[:, ROPE_LANE0:ROPE_LANE0 + QK_ROPE] = np.repeat(np.sin(ang), 2, axis=1) * sgn
    return jnp.asarray(cos, F32), jnp.asarray(sin, F32)


def _dft_tables(n_tok):
    def cs(n):
        k = np.arange(n)
        ang = 2.0 * np.pi * ((k[:, None] * k[None, :]) % n) / n
        return np.cos(ang), np.sin(ang)
    cc, sc = cs(FNET_CH)
    cn, sn = cs(n_tok)
    return tuple(jnp.asarray(t, F32) for t in (cc, sc, np.concatenate([cn, -sn], axis=1)))


def _pad_heads(w, width):
    lead = w.shape[:-1]
    w = w.reshape(lead + (N_HEADS, width))
    w = jnp.pad(w, [(0, 0)] * len(lead) + [(0, 0), (0, HEAD_PAD - width)])
    return w.reshape(lead + (N_HEADS * HEAD_PAD,))


def _layer0_weights(norm1, norm2, w_in, q_a_norm, w_q_up, q_norm, kv_a_norm, w_kv_up, k_norm, conv_w, w_o):
    c0 = Q_LORA + KV_LORA
    rope_cols = jnp.pad(w_in[:, c0:c0 + QK_ROPE], ((0, 0), (ROPE_LANE0, LANES - ROPE_LANE0 - QK_ROPE)))
    w_in_pad = jnp.concatenate([w_in[:, :c0], rope_cols, w_in[:, c0 + QK_ROPE:]], axis=1)
    kv = w_kv_up.reshape(KV_LORA, N_HEADS, QK_NOPE + V_HEAD)
    head_gain = lambda g: jnp.tile(jnp.pad(g, (0, HEAD_PAD - QK_DIM)), N_HEADS).reshape(1, -1)
    return dict(
        norm1=norm1.reshape(1, -1), norm2=norm2.reshape(1, -1), w_in=w_in_pad.astype(BF16),
        q_a_norm=q_a_norm.reshape(1, -1), w_q=_pad_heads(w_q_up, QK_DIM).astype(BF16),
        q_norm=head_gain(q_norm) * (QK_DIM ** -0.5), kv_a_norm=kv_a_norm.reshape(1, -1),
        w_k=_pad_heads(kv[:, :, :QK_NOPE].reshape(KV_LORA, -1), QK_NOPE).astype(BF16),
        w_v=kv[:, :, QK_NOPE:].reshape(KV_LORA, -1).astype(BF16), k_norm=head_gain(k_norm),
        conv_w=conv_w, w_o=w_o.astype(BF16))


def _router_weights(w_router):
    hi, lo = _split_hi_lo(w_router.T)
    return dict(wr_hi=hi, wr_lo=lo)


def _moe(x1, h2, aff, mod, tri, wg, wu, wd, n_tok):
    n_all = x1.shape[0]
    cap = CAPACITY_FACTOR * n_all // N_EXPERTS
    posm, s_tab = _route(aff, tri, cap)
    return _expert_ffn(x1, h2, posm, aff, s_tab, mod, wg, wu, wd, cap, n_tok)


def _trunk(x, mods, l0, l1, tri, ctx, rope_tabs, dft):
    n_b, n_tok, _ = x.shape
    mod0, mod1 = mods
    q, k, v, conv, ckv, kr = _front0(x, mod0, l0, rope_tabs)
    if ctx is not None:
        kc, vc = _ctx_kv(ctx[0], ctx[1], l0)
        attn = _attention(q, k, v, kc, vc)
    else:
        attn = _attention(q, k, v)
    flat = lambda a: a.reshape(n_b * n_tok, a.shape[-1])
    x1, h2, aff = _post0(flat(x), flat(attn), flat(conv), mod0, l0, n_tok)
    x = _moe(x1, h2, aff, mod0, tri, l0["wg"], l0["wu"], l0["wd"], n_tok)
    x1, h2, aff = _fnet(x.reshape(n_b, n_tok, D_MODEL), mod1, l1, dft)
    x = _moe(flat(x1), flat(h2), aff, mod1, tri, l1["wg"], l1["wu"], l1["wd"], n_tok)
    return x.reshape(n_b, n_tok, D_MODEL), ckv, kr[:, :, ROPE_LANE0:ROPE_LANE0 + QK_ROPE]


def kernel(x_prompt, x_sample, c, cache_c_kv_l0, cache_k_rope_l0, c_ctx, norm1_l0, norm2_l0, w_mod_l0, b_mod_l0, w_in_l0, q_a_norm_l0, w_q_up_l0, q_norm_l0, kv_a_norm_l0, w_kv_up_l0, k_norm_l0, conv_w_l0, w_o_l0, w_router_l0, w_gate_l0, w_up_l0, w_down_l0, norm1_l1, norm2_l1, w_mod_l1, b_mod_l1, w_f_l1, w_router_l1, w_gate_l1, w_up_l1, w_down_l1):
    n_dec = c.shape[0]
    cond = jnp.concatenate([c_ctx[None, :], c, jnp.zeros((16 - 1 - n_dec, D_MODEL), F32)], axis=0)
    m0 = _modulation(cond, w_mod_l0, b_mod_l0)
    m1 = _modulation(cond, w_mod_l1, b_mod_l1)
    mods_prompt = (m0[0:1], m1[0:1])
    mods_sample = (m0[1:1 + n_dec], m1[1:1 + n_dec])

    l0 = _layer0_weights(norm1_l0, norm2_l0, w_in_l0, q_a_norm_l0, w_q_up_l0, q_norm_l0, kv_a_norm_l0,
                         w_kv_up_l0, k_norm_l0, conv_w_l0, w_o_l0)
    l0.update(_router_weights(w_router_l0))
    l0.update(wg=w_gate_l0, wu=w_up_l0, wd=w_down_l0)
    l1 = dict(norm1=norm1_l1.reshape(1, -1), norm2=norm2_l1.reshape(1, -1), w_f=w_f_l1.astype(BF16))
    l1.update(_router_weights(w_router_l1))
    l1.update(wg=w_gate_l1, wu=w_up_l1, wd=w_down_l1)

    tri = jnp.asarray(np.triu(np.ones((ROWS, ROWS)), 1), BF16)
    y_prompt, new_c_kv, new_k_rope = _trunk(x_prompt, mods_prompt, l0, l1, tri, None, None,
                                            _dft_tables(x_prompt.shape[1]))
    k_rope_pad = jnp.pad(cache_k_rope_l0, ((0, 0), (0, 0), (ROPE_LANE0, LANES - ROPE_LANE0 - QK_ROPE)))
    y_sample, _, _ = _trunk(x_sample, mods_sample, l0, l1, tri, (cache_c_kv_l0, k_rope_pad),
                            _rope_tables(x_sample.shape[1]), _dft_tables(x_sample.shape[1]))
    return (y_prompt, y_sample, new_c_kv, new_k_rope)
```

```python
import functools

import jax
import jax.numpy as jnp
import numpy as np
from jax import lax
from jax.experimental import pallas as pl
from jax.experimental.pallas import tpu as pltpu

D_MODEL = 1024
GRID_W = 64
N_HEADS = 8
QK_NOPE = 64
QK_ROPE = 32
QK_DIM = QK_NOPE + QK_ROPE
V_HEAD = 64
Q_LORA = 384
KV_LORA = 256
CONV_CH = 512
FNET_GROUPS = 4
FNET_CH = D_MODEL // FNET_GROUPS
N_EXPERTS = 16
EXPERT_FF = 512
CAPACITY_FACTOR = 2
ROPE_THETA = 10000.0
EPS = 1e-6

LANES = 128
HEAD_PAD = LANES
ROWS = 256
ALIGN = 16
WIN = 64
MAX_UNITS = N_EXPERTS * (ROWS // WIN)
COMMON_ROWS = N_EXPERTS * WIN
XE_W = D_MODEL + LANES
IN0_PAD = Q_LORA + KV_LORA + 2 * LANES + 3 * CONV_CH
ROPE_LANE0 = QK_NOPE
VMEM_LIMIT = 56 * 1024 * 1024

F32 = jnp.float32
BF16 = jnp.bfloat16


def _dot(a, b):
    return jnp.dot(a, b, preferred_element_type=F32)


def _dot_nt(a, b):
    return lax.dot_general(a, b, (((1,), (1,)), ((), ())), preferred_element_type=F32)


def _split_hi_lo(x):
    hi = x.astype(BF16)
    lo = (x - hi.astype(F32)).astype(BF16)
    return hi, lo


def _params(n_axes):
    return pltpu.CompilerParams(dimension_semantics=("arbitrary",) * n_axes,
                                vmem_limit_bytes=VMEM_LIMIT)


def _rms(x, gain):
    return x * lax.rsqrt(jnp.mean(x * x, axis=-1, keepdims=True) + EPS) * gain


def _mod_kernel(cond_ref, w_ref, b_ref, o_ref):
    c = cond_ref[...]
    s = c * (1.0 / (1.0 + jnp.exp(-c)))
    s_hi, s_lo = _split_hi_lo(s)
    w_hi, w_lo = _split_hi_lo(w_ref[...])
    o_ref[...] = _dot(s_hi, w_hi) + _dot(s_lo, w_hi) + _dot(s_hi, w_lo) + b_ref[...]


def _modulation(cond, w_mod, b_mod):
    n_rows = cond.shape[0]
    tn = 1536
    out = pl.pallas_call(
        _mod_kernel,
        out_shape=jax.ShapeDtypeStruct((n_rows, 6 * D_MODEL), F32),
        grid=(6 * D_MODEL // tn,),
        in_specs=[pl.BlockSpec((n_rows, D_MODEL), lambda i: (0, 0)),
                  pl.BlockSpec((D_MODEL, tn), lambda i: (0, i)),
                  pl.BlockSpec((1, tn), lambda i: (0, i))],
        out_specs=pl.BlockSpec((n_rows, tn), lambda i: (0, i)),
        compiler_params=_params(1),
        name="modulation",
    )(cond, w_mod, b_mod.reshape(1, -1))
    return out.reshape(n_rows, 6, D_MODEL)


def _head_norm_rope(xh, gain_cos, swapped_sin):
    ss = jnp.sum(xh * xh, axis=-1, keepdims=True) * (1.0 / QK_DIM)
    y = xh * gain_cos
    if swapped_sin is not None:
        y = y + swapped_sin
    return y * lax.rsqrt(ss + EPS)


def _expand_kv(ckv_bf, r, swapped_sin, wk_ref, wv_ref, gain_cos, k_ref, v_ref, rows):
    kf = _dot(ckv_bf, wk_ref[...])
    v_ref[rows, :] = _dot(ckv_bf, wv_ref[...]).astype(BF16)
    for h in range(N_HEADS):
        sl = slice(h * HEAD_PAD, (h + 1) * HEAD_PAD)
        k_ref[rows, sl] = _head_norm_rope(kf[:, sl] + r, gain_cos, swapped_sin).astype(BF16)


def _front0_kernel(*refs, n_tok, rope):
    (x_ref, mod_ref, n1_ref, win_ref, qan_ref, wq_ref, kvan_ref, wk_ref, wv_ref, cw_ref,
     qa_ref, ka_ref) = refs[:12]
    pos = 12
    if rope:
        wqs_ref, qb_ref, kb_ref = refs[pos:pos + 3]
        pos += 3
    q_ref, k_ref, v_ref, conv_ref, ckv_ref, kr_ref, cu_s, gb_s = refs[pos:]
    sh1, sc1 = mod_ref[0:1, :], mod_ref[1:2, :]
    c_rope = Q_LORA + KV_LORA
    for c in range(n_tok // ROWS):
        rows = pl.ds(c * ROWS, ROWS)
        h = _rms(x_ref[rows, :], n1_ref[...]) * (1.0 + sc1) + sh1
        proj = _dot(h.astype(BF16), win_ref[...])
        cq = _rms(proj[:, :Q_LORA], qan_ref[...]).astype(BF16)
        qf = _dot(cq, wq_ref[...])
        qfs = _dot(cq, wqs_ref[...]) if rope else None
        for hd in range(N_HEADS):
            sl = slice(hd * HEAD_PAD, (hd + 1) * HEAD_PAD)
            q_sin = qfs[:, sl] * qb_ref[rows, :] if rope else None
            q_ref[rows, sl] = _head_norm_rope(qf[:, sl], qa_ref[rows, :], q_sin).astype(BF16)
        ckv = _rms(proj[:, Q_LORA:c_rope], kvan_ref[...])
        ckv_ref[rows, :] = ckv
        r = proj[:, c_rope:c_rope + LANES]
        kr_ref[rows, :] = r
        k_sin = proj[:, c_rope + LANES:c_rope + 2 * LANES] * kb_ref[rows, :] if rope else None
        _expand_kv(ckv.astype(BF16), r, k_sin, wk_ref, wv_ref, ka_ref[rows, :], k_ref, v_ref, rows)
        c0 = c_rope + 2 * LANES
        gb_s[rows, :] = proj[:, c0:c0 + CONV_CH]
        cu_s[rows, :] = proj[:, c0 + CONV_CH:c0 + 2 * CONV_CH] * proj[:, c0 + 2 * CONV_CH:c0 + 3 * CONV_CH]
    cu = cu_s[...]
    row = lax.broadcasted_iota(jnp.int32, cu.shape, 0)
    prev = jnp.where(row == 0, 0.0, pltpu.roll(cu, 1, 0))
    nxt = jnp.where(row == n_tok - 1, 0.0, pltpu.roll(cu, n_tok - 1, 0))
    conv = gb_s[...] * (cw_ref[0:1, :] * prev + cw_ref[1:2, :] * cu + cw_ref[2:3, :] * nxt)
    conv_ref[...] = conv.astype(BF16)


def _const_spec(shape):
    return pl.BlockSpec(shape, lambda b: (0,) * len(shape))


def _front0(x, mod, w, rope_tabs):
    n_b, n_tok, _ = x.shape
    shared_mod = mod.shape[0] == 1
    seq = lambda width: pl.BlockSpec((None, n_tok, width), lambda b: (b, 0, 0))
    in_specs = [seq(D_MODEL),
                pl.BlockSpec((None, 6, D_MODEL), (lambda b: (0, 0, 0)) if shared_mod else (lambda b: (b, 0, 0))),
                _const_spec((1, D_MODEL)), _const_spec((D_MODEL, IN0_PAD)), _const_spec((1, Q_LORA)),
                _const_spec((Q_LORA, N_HEADS * HEAD_PAD)),
                _const_spec((1, KV_LORA)), _const_spec((KV_LORA, N_HEADS * HEAD_PAD)),
                _const_spec((KV_LORA, N_HEADS * V_HEAD)), _const_spec((3, CONV_CH)),
                _const_spec((n_tok, LANES)), _const_spec((n_tok, LANES))]
    q_tabs, k_tabs = _head_tables(w["q_gain"], rope_tabs, n_tok), _head_tables(w["k_gain"], rope_tabs, n_tok)
    args = [x, mod, w["norm1"], w["w_in"], w["q_a_norm"], w["w_q"], w["kv_a_norm"],
            w["w_k"], w["w_v"], w["conv_w"], q_tabs[0], k_tabs[0]]
    if rope_tabs is not None:
        in_specs += [_const_spec((Q_LORA, N_HEADS * HEAD_PAD)), _const_spec((n_tok, LANES)), _const_spec((n_tok, LANES))]
        args += [w["w_q_swap"], q_tabs[1], k_tabs[1]]
    out_shape = [jax.ShapeDtypeStruct((n_b, n_tok, N_HEADS * HEAD_PAD), BF16),
                 jax.ShapeDtypeStruct((n_b, n_tok, N_HEADS * HEAD_PAD), BF16),
                 jax.ShapeDtypeStruct((n_b, n_tok, N_HEADS * V_HEAD), BF16),
                 jax.ShapeDtypeStruct((n_b, n_tok, CONV_CH), BF16),
                 jax.ShapeDtypeStruct((n_b, n_tok, KV_LORA), F32),
                 jax.ShapeDtypeStruct((n_b, n_tok, LANES), F32)]
    out_specs = [seq(N_HEADS * HEAD_PAD), seq(N_HEADS * HEAD_PAD), seq(N_HEADS * V_HEAD), seq(CONV_CH),
                 seq(KV_LORA), seq(LANES)]
    return pl.pallas_call(
        functools.partial(_front0_kernel, n_tok=n_tok, rope=rope_tabs is not None),
        out_shape=out_shape, grid=(n_b,), in_specs=in_specs, out_specs=out_specs,
        scratch_shapes=[pltpu.VMEM((n_tok, CONV_CH), F32), pltpu.VMEM((n_tok, CONV_CH), F32)],
        compiler_params=_params(1), name="front0",
    )(*args)


def _ctx_kv_kernel(ckv_ref, r_ref, wk_ref, wv_ref, ka_ref, k_ref, v_ref):
    rows = pl.ds(0, ckv_ref.shape[0])
    _expand_kv(ckv_ref[...].astype(BF16), r_ref[...], None, wk_ref, wv_ref, ka_ref[...], k_ref, v_ref, rows)


def _ctx_kv(cache_c_kv, cache_k_rope_pad, w):
    n_b, n_ctx, _ = cache_c_kv.shape
    seq = lambda width: pl.BlockSpec((None, n_ctx, width), lambda b: (b, 0, 0))
    return pl.pallas_call(
        _ctx_kv_kernel,
        out_shape=[jax.ShapeDtypeStruct((n_b, n_ctx, N_HEADS * HEAD_PAD), BF16),
                   jax.ShapeDtypeStruct((n_b, n_ctx, N_HEADS * V_HEAD), BF16)],
        grid=(n_b,),
        in_specs=[seq(KV_LORA), seq(LANES), _const_spec((KV_LORA, N_HEADS * HEAD_PAD)),
                  _const_spec((KV_LORA, N_HEADS * V_HEAD)), _const_spec((n_ctx, LANES))],
        out_specs=[seq(N_HEADS * HEAD_PAD), seq(N_HEADS * V_HEAD)],
        compiler_params=_params(1), name="ctx_kv",
    )(cache_c_kv, cache_k_rope_pad, w["w_k"], w["w_v"], _head_tables(w["k_gain"], None, n_ctx)[0])


def _attn_kernel(*refs, with_ctx):
    if with_ctx:
        q_ref, k_ref, v_ref, kc_ref, vc_ref, o_ref = refs
    else:
        q_ref, k_ref, v_ref, o_ref = refs
    lane = lax.broadcasted_iota(jnp.int32, (q_ref.shape[0], LANES), 1)
    for pair in range(N_HEADS // 2):
        vsl = slice(pair * LANES, (pair + 1) * LANES)
        outs = []
        for hd in (2 * pair, 2 * pair + 1):
            sl = slice(hd * HEAD_PAD, (hd + 1) * HEAD_PAD)
            qh = q_ref[:, sl]
            s = _dot_nt(qh, k_ref[:, sl])
            m = jnp.max(s, axis=-1, keepdims=True)
            if with_ctx:
                sc = _dot_nt(qh, kc_ref[:, sl])
                m = jnp.maximum(m, jnp.max(sc, axis=-1, keepdims=True))
                pc = jnp.exp(sc - m)
            p = jnp.exp(s - m)
            den = jnp.sum(p, axis=-1, keepdims=True)
            o = _dot(p.astype(BF16), v_ref[:, vsl])
            if with_ctx:
                den = den + jnp.sum(pc, axis=-1, keepdims=True)
                o = o + _dot(pc.astype(BF16), vc_ref[:, vsl])
            outs.append(o / den)
        o_ref[:, vsl] = jnp.where(lane < V_HEAD, outs[0], outs[1]).astype(BF16)


def _attention(q, k, v, kc=None, vc=None):
    n_b, n_tok, _ = q.shape
    with_ctx = kc is not None
    qspec = lambda width: pl.BlockSpec((None, ROWS, width), lambda b, i: (b, i, 0))
    kvspec = lambda n, width: pl.BlockSpec((None, n, width), lambda b, i: (b, 0, 0))
    in_specs = [qspec(N_HEADS * HEAD_PAD), kvspec(n_tok, N_HEADS * HEAD_PAD), kvspec(n_tok, N_HEADS * V_HEAD)]
    args = [q, k, v]
    if with_ctx:
        n_ctx = kc.shape[1]
        in_specs += [kvspec(n_ctx, N_HEADS * HEAD_PAD), kvspec(n_ctx, N_HEADS * V_HEAD)]
        args += [kc, vc]
    return pl.pallas_call(
        functools.partial(_attn_kernel, with_ctx=with_ctx),
        out_shape=jax.ShapeDtypeStruct((n_b, n_tok, N_HEADS * V_HEAD), BF16),
        grid=(n_b, n_tok // ROWS), in_specs=in_specs, out_specs=qspec(N_HEADS * V_HEAD),
        compiler_params=_params(2), name="attention",
    )(*args)


def _moe_front(x1, mod_ref, n2_ref, wrh_ref, wrl_ref):
    sh2, sc2 = mod_ref[3:4, :], mod_ref[4:5, :]
    h2 = _rms(x1, n2_ref[...]) * (1.0 + sc2) + sh2
    h_hi, h_lo = _split_hi_lo(h2)
    logit = _dot_nt(wrh_ref[...], h_hi) + _dot_nt(wrh_ref[...], h_lo) + _dot_nt(wrl_ref[...], h_hi)
    e = jnp.exp(logit - jnp.max(logit, axis=0, keepdims=True))
    return h_hi, e / jnp.sum(e, axis=0, keepdims=True)


def _post0_kernel(x_ref, attn_ref, conv_ref, wo_ref, mod_ref, n2_ref, wrh_ref, wrl_ref,
                  x1_ref, h2_ref, aff_ref):
    n_attn = N_HEADS * V_HEAD
    mix = _dot(attn_ref[...], wo_ref[0:n_attn, :]) + _dot(conv_ref[...], wo_ref[n_attn:, :])
    x1 = x_ref[...] + mod_ref[2:3, :] * mix
    x1_ref[...] = x1
    h2, aff = _moe_front(x1, mod_ref, n2_ref, wrh_ref, wrl_ref)
    h2_ref[...] = h2
    aff_ref[...] = aff


def _post0(x, attn, conv, mod, w, n_tok):
    n_all = x.shape[0]
    tiles_per_seq = n_tok // ROWS
    shared_mod = mod.shape[0] == 1
    tile = lambda width: pl.BlockSpec((ROWS, width), lambda i: (i, 0))
    const = lambda shape: pl.BlockSpec(shape, lambda i: (0,) * len(shape))
    mod_map = (lambda i: (0, 0, 0)) if shared_mod else (lambda i: (i // tiles_per_seq, 0, 0))
    return pl.pallas_call(
        _post0_kernel,
        out_shape=[jax.ShapeDtypeStruct((n_all, D_MODEL), F32),
                   jax.ShapeDtypeStruct((n_all, D_MODEL), BF16),
                   jax.ShapeDtypeStruct((n_all // ROWS, N_EXPERTS, ROWS), F32)],
        grid=(n_all // ROWS,),
        in_specs=[tile(D_MODEL), tile(N_HEADS * V_HEAD), tile(CONV_CH), const((D_MODEL, D_MODEL)),
                  pl.BlockSpec((None, 6, D_MODEL), mod_map), const((1, D_MODEL)),
                  const((N_EXPERTS, D_MODEL)), const((N_EXPERTS, D_MODEL))],
        out_specs=[tile(D_MODEL), tile(D_MODEL), pl.BlockSpec((None, N_EXPERTS, ROWS), lambda i: (i, 0, 0))],
        compiler_params=_params(1), name="post0",
    )(x, attn, conv, w["w_o"], mod, w["norm2"], w["wr_hi"], w["wr_lo"])


def _fnet_kernel(x_ref, mod_ref, n1_ref, cc_ref, sc_ref, dft_ref, wf_ref, n2_ref, wrh_ref, wrl_ref,
                 x1_ref, h2_ref, aff_ref, y_s, *, n_tok):
    sh1, sc1, g1 = mod_ref[0:1, :], mod_ref[1:2, :], mod_ref[2:3, :]
    n_chunks = n_tok // ROWS
    cc, sc = cc_ref[...].astype(BF16), sc_ref[...].astype(BF16)
    for c in range(n_chunks):
        rows = pl.ds(c * ROWS, ROWS)
        h = (_rms(x_ref[rows, :], n1_ref[...]) * (1.0 + sc1) + sh1).astype(BF16)
        for g in range(FNET_GROUPS):
            sl = slice(g * FNET_CH, (g + 1) * FNET_CH)
            y_s[pl.ds(c * ROWS, ROWS), sl] = _dot(h[:, sl], cc).astype(BF16)
            y_s[pl.ds(n_tok + c * ROWS, ROWS), sl] = _dot(h[:, sl], sc).astype(BF16)
    scale = 1.0 / float(np.sqrt(n_tok * FNET_CH))
    for c in range(n_chunks):
        rows = pl.ds(c * ROWS, ROWS)
        f = _dot(dft_ref[rows, :].astype(BF16), y_s[...]) * scale
        x1 = x_ref[rows, :] + g1 * _dot(f.astype(BF16), wf_ref[...])
        x1_ref[rows, :] = x1
        h2, aff = _moe_front(x1, mod_ref, n2_ref, wrh_ref, wrl_ref)
        h2_ref[rows, :] = h2
        aff_ref[c] = aff


def _fnet(x, mod, w, dft):
    n_b, n_tok, _ = x.shape
    shared_mod = mod.shape[0] == 1
    seq = lambda width: pl.BlockSpec((None, n_tok, width), lambda b: (b, 0, 0))
    cc, sc, dft_n = dft
    tiles = n_tok // ROWS
    return pl.pallas_call(
        functools.partial(_fnet_kernel, n_tok=n_tok),
        out_shape=[jax.ShapeDtypeStruct((n_b, n_tok, D_MODEL), F32),
                   jax.ShapeDtypeStruct((n_b, n_tok, D_MODEL), BF16),
                   jax.ShapeDtypeStruct((n_b * tiles, N_EXPERTS, ROWS), F32)],
        grid=(n_b,),
        in_specs=[seq(D_MODEL),
                  pl.BlockSpec((None, 6, D_MODEL), (lambda b: (0, 0, 0)) if shared_mod else (lambda b: (b, 0, 0))),
                  _const_spec((1, D_MODEL)), _const_spec((FNET_CH, FNET_CH)), _const_spec((FNET_CH, FNET_CH)),
                  pl.BlockSpec((n_tok, 2 * n_tok), lambda b: (0, 0), pipeline_mode=pl.Buffered(1)),
                  _const_spec((D_MODEL, D_MODEL)), _const_spec((1, D_MODEL)),
                  _const_spec((N_EXPERTS, D_MODEL)), _const_spec((N_EXPERTS, D_MODEL))],
        out_specs=[seq(D_MODEL), seq(D_MODEL), pl.BlockSpec((tiles, N_EXPERTS, ROWS), lambda b: (b, 0, 0))],
        scratch_shapes=[pltpu.VMEM((2 * n_tok, D_MODEL), BF16)],
        compiler_params=_params(1), name="fnet",
    )(x, mod, w["norm1"], cc, sc, dft_n, w["w_f"], w["norm2"], w["wr_hi"], w["wr_lo"])


def _route_kernel(aff_ref, tri_ref, posm_ref, s_ref, *, n_blk, cap):
    def count(pred):
        acc = jnp.zeros((N_EXPERTS, ROWS), F32)
        for b in range(n_blk):
            acc = acc + jnp.where(pred(aff_ref[b]), 1.0, 0.0)
        return jnp.sum(acc, axis=1, keepdims=True)

    def as_f32(bits):
        return pltpu.bitcast(bits, F32)

    def search(_, carry):
        lo, hi = carry
        mid = lo + ((hi - lo) >> 1)
        mid_f = as_f32(mid)
        ok = count(lambda a: a >= mid_f) >= cap
        return jnp.where(ok, mid, lo), jnp.where(ok, hi, mid)

    one_bits = 0x3F800000
    lo0 = jnp.zeros((N_EXPERTS, 1), jnp.int32)
    hi0 = jnp.full((N_EXPERTS, 1), one_bits + 1, jnp.int32)
    lo, hi = lax.fori_loop(0, 31, search, (lo0, hi0))
    lo_f, ub = as_f32(lo), as_f32(hi)
    thr = lo_f
    pending = jnp.ones((N_EXPERTS, 1), F32)
    for _ in range(3):
        cur = jnp.full((N_EXPERTS, ROWS), -1.0, F32)
        for b in range(n_blk):
            a = aff_ref[b]
            cur = jnp.maximum(cur, jnp.where((a >= lo_f) & (a < ub), a, -1.0))
        cur = jnp.max(cur, axis=1, keepdims=True)
        take = (count(lambda a: a >= cur) >= cap) & (pending > 0.0)
        thr = jnp.where(take, cur, thr)
        pending = jnp.where(take, 0.0, pending)
        ub = cur
    need = cap - count(lambda a: a > thr)

    lane = lax.broadcasted_iota(jnp.int32, (N_EXPERTS, LANES), 1)
    carry_tie = jnp.zeros((N_EXPERTS, 1), F32)
    carry_pos = jnp.zeros((N_EXPERTS, 1), F32)
    s_acc = jnp.zeros((N_EXPERTS, LANES), jnp.int32)
    for b in range(n_blk):
        v = aff_ref[b]
        eq = jnp.where(v == thr, 1.0, 0.0)
        tie_rank = _dot(eq.astype(BF16), tri_ref[...]) + carry_tie
        sel = jnp.where((v > thr) | ((v == thr) & (tie_rank < need)), 1.0, 0.0)
        pos = _dot(sel.astype(BF16), tri_ref[...]) + carry_pos
        posm_ref[b] = jnp.where(sel > 0.0, pos, -1.0)
        s_acc = jnp.where(lane == b, carry_pos.astype(jnp.int32), s_acc)
        carry_tie = carry_tie + jnp.sum(eq, axis=1, keepdims=True)
        n_sel = jnp.sum(sel, axis=1, keepdims=True)
        carry_pos = carry_pos + jnp.floor((n_sel + (ALIGN - 1)) * (1.0 / ALIGN)) * ALIGN
    s_ref[...] = jnp.where(lane == n_blk, carry_pos.astype(jnp.int32), s_acc)


def _route(aff, tri, cap):
    n_blk = aff.shape[0]
    assert n_blk < LANES
    full = lambda shape: pl.BlockSpec(shape, lambda i: (0,) * len(shape))
    posm, s_tab = pl.pallas_call(
        functools.partial(_route_kernel, n_blk=n_blk, cap=cap),
        out_shape=[jax.ShapeDtypeStruct((n_blk, N_EXPERTS, ROWS), F32),
                   jax.ShapeDtypeStruct((N_EXPERTS, LANES), jnp.int32)],
        grid=(1,),
        in_specs=[full((n_blk, N_EXPERTS, ROWS)), full((ROWS, ROWS))],
        out_specs=[full((n_blk, N_EXPERTS, ROWS)), full((N_EXPERTS, LANES))],
        compiler_params=_params(1), name="route",
    )(aff, tri)
    return posm, s_tab[:, :n_blk + 1]


def _ceil_div_pow2(x, d):
    return lax.shift_right_logical(x + (d - 1), d.bit_length() - 1)


def _block_units(s_ref, b):
    base = [s_ref[e, b] for e in range(N_EXPERTS)]
    units = [_ceil_div_pow2(s_ref[e, b + 1] - base[e], WIN) for e in range(N_EXPERTS)]
    first_unit, total = [], 0
    for e in range(N_EXPERTS):
        first_unit.append(total)
        total = total + units[e]
    return base, units, first_unit, total


def _for_units(units, fn):
    for e in range(N_EXPERTS):
        def body(u, _, e=e):
            fn(e, u)
            return 0
        lax.fori_loop(0, units[e], body, 0)


def _window_onehot(posm_ref, e, first, lo):
    slot = lax.broadcasted_iota(jnp.int32, (WIN, ROWS), 0) + first
    p = posm_ref[e:e + 1, :].astype(jnp.int32)
    return jnp.where((p == slot) & (p >= lo), 1.0, 0.0)


def _gate_lanes(e):
    lane = lax.broadcasted_iota(jnp.int32, (1, LANES), 1)
    return ((lane & (N_EXPERTS - 1)) == e) & (lane < 3 * N_EXPERTS)


def _dispatch_kernel(s_ref, h_ref, posm_ref, aff_ref, xe_ref, sel_s, x_s, sem, *, n_blk, cap_pad):
    b = pl.program_id(0)
    buf = b % 2
    base, units, first_unit, n_units = _block_units(s_ref, b)

    @pl.when(b == 0)
    def _():
        sel_s[...] = jnp.zeros_like(sel_s)

    a = jnp.concatenate([aff_ref[...], jnp.zeros((LANES - N_EXPERTS, ROWS), F32)], axis=0).T
    a_hi = a.astype(BF16).astype(F32)
    a_mid = (a - a_hi).astype(BF16).astype(F32)
    a_lo = a - a_hi - a_mid
    parts = (a_hi + pltpu.roll(a_mid, N_EXPERTS, 1) + pltpu.roll(a_lo, 2 * N_EXPERTS, 1)).astype(BF16)

    def unit_rows(first_unit_e, u):
        return pl.ds(pl.multiple_of((first_unit_e + u) * WIN, WIN), WIN)

    def pick(e, u):
        first = base[e] + u * WIN
        sel_s[unit_rows(first_unit[e], u), :] = _window_onehot(posm_ref, e, first, first).astype(BF16)

    _for_units(units, pick)

    def gather(rows):
        sel = sel_s[rows, :]
        x_s[buf, rows, 0:D_MODEL] = _dot(sel, h_ref[...]).astype(BF16)
        x_s[buf, rows, D_MODEL:XE_W] = _dot(sel, parts).astype(BF16)

    def gather_chunk(c, _):
        gather(pl.ds(pl.multiple_of(c * ROWS, ROWS), ROWS))
        return 0

    gather(pl.ds(0, COMMON_ROWS))
    lax.fori_loop(COMMON_ROWS // ROWS, _ceil_div_pow2(n_units * WIN, ROWS), gather_chunk, 0)

    def copies(slot, blk, fn):
        blk_base, blk_units, blk_first, _ = _block_units(s_ref, blk)

        def one(e, u):
            dst = pl.ds(pl.multiple_of(blk_base[e] + u * WIN, ALIGN), WIN)
            fn(pltpu.make_async_copy(x_s.at[slot, unit_rows(blk_first[e], u), :],
                                     xe_ref.at[e, dst, :], sem.at[slot]))

        _for_units(blk_units, one)

    @pl.when(b > 0)
    def _():
        copies(1 - buf, b - 1, lambda c: c.wait())

    copies(buf, b, lambda c: c.start())

    @pl.when(b == n_blk - 1)
    def _():
        copies(buf, b, lambda c: c.wait())
        x_s[buf, 0:WIN, :] = jnp.zeros((WIN, XE_W), BF16)

        def tail(fn):
            for e in range(N_EXPERTS):
                total = s_ref[e, n_blk]
                n_win = (cap_pad - total) // WIN

                def wide(c, _):
                    row = pl.multiple_of(total + c * WIN, ALIGN)
                    fn(pltpu.make_async_copy(x_s.at[buf, pl.ds(0, WIN), :],
                                             xe_ref.at[e, pl.ds(row, WIN), :], sem.at[buf]))
                    return 0

                def narrow(c, _):
                    row = pl.multiple_of(total + n_win * WIN + c * ALIGN, ALIGN)
                    fn(pltpu.make_async_copy(x_s.at[buf, pl.ds(0, ALIGN), :],
                                             xe_ref.at[e, pl.ds(row, ALIGN), :], sem.at[buf]))
                    return 0

                lax.fori_loop(0, n_win, wide, 0)
                lax.fori_loop(0, (cap_pad - total - n_win * WIN) // ALIGN, narrow, 0)

        tail(lambda c: c.start())
        tail(lambda c: c.wait())


def _ffn_kernel(s_ref, x_ref, wg_ref, wu_ref, wd_ref, y_ref, wg_s, wu_s, wd_s, *, n_blk, n_tiles):
    e = pl.program_id(0)
    wg_s[...] = wg_ref[...].astype(BF16)
    wu_s[...] = wu_ref[...].astype(BF16)
    wd_s[...] = wd_ref[...].astype(BF16)
    n_used = _ceil_div_pow2(s_ref[e, n_blk], ROWS)
    mine = _gate_lanes(e)

    def tile(j, _):
        rows = pl.ds(pl.multiple_of(j * ROWS, ROWS), ROWS)
        x = x_ref[rows, 0:D_MODEL]
        pieces = x_ref[rows, D_MODEL:XE_W].astype(F32)
        gate = jnp.sum(jnp.where(mine, pieces, 0.0), axis=1, keepdims=True)
        a = _dot(x, wg_s[...])
        u = _dot(x, wu_s[...])
        hid = a * (1.0 / (1.0 + jnp.exp(-a))) * u * gate
        y_ref[rows, :] = _dot(hid.astype(BF16), wd_s[...]).astype(BF16)
        return 0

    def blank(j, _):
        y_ref[pl.ds(pl.multiple_of(j * ROWS, ROWS), ROWS), :] = jnp.zeros((ROWS, D_MODEL), BF16)
        return 0

    lax.fori_loop(0, n_used, tile, 0)
    lax.fori_loop(n_used, n_tiles, blank, 0)


def _combine_kernel(s_ref, y_ref, posm_ref, x1_ref, mod_ref, o_ref, yw_s, sel_s, acc_s, sem,
                    *, n_blk, cap_pad):
    b = pl.program_id(0)
    buf = b % 2
    base, units, first_unit, n_units = _block_units(s_ref, b)

    def unit_rows(first_unit_e, u):
        return pl.ds(pl.multiple_of((first_unit_e + u) * WIN, WIN), WIN)

    def bounds(base_e, u):
        lo = base_e + u * WIN
        return lo, jnp.minimum(lo, cap_pad - WIN)

    def copies(slot, blk, fn):
        blk_base, blk_units, blk_first, _ = _block_units(s_ref, blk)

        def one(e, u):
            src = pl.ds(pl.multiple_of(bounds(blk_base[e], u)[1], ALIGN), WIN)
            fn(pltpu.make_async_copy(y_ref.at[e, src, :], yw_s.at[slot, unit_rows(blk_first[e], u), :],
                                     sem.at[slot]))

        _for_units(blk_units, one)

    @pl.when(b == 0)
    def _():
        yw_s[...] = jnp.zeros_like(yw_s)
        copies(0, 0, lambda c: c.start())

    @pl.when(b + 1 < n_blk)
    def _():
        copies(1 - buf, b + 1, lambda c: c.start())

    def pick(e, u):
        lo, first = bounds(base[e], u)
        sel_s[unit_rows(first_unit[e], u), :] = _window_onehot(posm_ref, e, first, lo)

    _for_units(units, pick)
    n_chunks = jnp.maximum(_ceil_div_pow2(n_units * WIN, ROWS), COMMON_ROWS // ROWS)

    def blank(u, _):
        sel_s[unit_rows(u, 0), :] = jnp.zeros((WIN, ROWS), F32)
        return 0

    lax.fori_loop(n_units, n_chunks * (ROWS // WIN), blank, 0)
    copies(buf, b, lambda c: c.wait())

    def weighted(rows):
        return _dot(sel_s[rows, :].T.astype(BF16), yw_s[buf, rows, :])

    def add_chunk(c, _):
        acc_s[...] += weighted(pl.ds(pl.multiple_of(c * ROWS, ROWS), ROWS))
        return 0

    acc_s[...] = weighted(pl.ds(0, COMMON_ROWS))
    lax.fori_loop(COMMON_ROWS // ROWS, n_chunks, add_chunk, 0)
    o_ref[...] = x1_ref[...] + mod_ref[5:6, :] * acc_s[...]


def _padded_capacity(cap, n_blk):
    return -(-(cap + (ALIGN - 1) * n_blk + WIN) // ROWS) * ROWS


def _expert_ffn(x1, h2, posm, aff, s_tab, mod, wg, wu, wd, cap, n_tok):
    n_all = h2.shape[0]
    n_blk = n_all // ROWS
    cap_pad = _padded_capacity(cap, n_blk)
    blk = lambda *shape: pl.BlockSpec((None,) + shape, lambda b, s: (b,) + (0,) * len(shape))
    xe = pl.pallas_call(
        functools.partial(_dispatch_kernel, n_blk=n_blk, cap_pad=cap_pad),
        out_shape=jax.ShapeDtypeStruct((N_EXPERTS, cap_pad, XE_W), BF16),
        grid_spec=pltpu.PrefetchScalarGridSpec(
            num_scalar_prefetch=1, grid=(n_blk,),
            in_specs=[pl.BlockSpec((ROWS, D_MODEL), lambda b, s: (b, 0)), blk(N_EXPERTS, ROWS), blk(N_EXPERTS, ROWS)],
            out_specs=pl.BlockSpec(memory_space=pl.ANY),
            scratch_shapes=[pltpu.VMEM((MAX_UNITS * WIN, ROWS), BF16), pltpu.VMEM((2, MAX_UNITS * WIN, XE_W), BF16),
                            pltpu.SemaphoreType.DMA((2,))]),
        compiler_params=_params(1), name="dispatch",
    )(s_tab, h2, posm, aff)

    per_expert = lambda rows, cols: pl.BlockSpec((None, rows, cols), lambda e, s: (e, 0, 0))
    y = pl.pallas_call(
        functools.partial(_ffn_kernel, n_blk=n_blk, n_tiles=cap_pad // ROWS),
        out_shape=jax.ShapeDtypeStruct((N_EXPERTS, cap_pad, D_MODEL), BF16),
        grid_spec=pltpu.PrefetchScalarGridSpec(
            num_scalar_prefetch=1, grid=(N_EXPERTS,),
            in_specs=[per_expert(cap_pad, XE_W), per_expert(D_MODEL, EXPERT_FF), per_expert(D_MODEL, EXPERT_FF),
                      per_expert(EXPERT_FF, D_MODEL)],
            out_specs=per_expert(cap_pad, D_MODEL),
            scratch_shapes=[pltpu.VMEM((D_MODEL, EXPERT_FF), BF16), pltpu.VMEM((D_MODEL, EXPERT_FF), BF16),
                            pltpu.VMEM((EXPERT_FF, D_MODEL), BF16)]),
        compiler_params=_params(1), name="ffn",
    )(s_tab, xe, wg, wu, wd)

    tiles_per_seq = n_tok // ROWS
    mod_map = (lambda b, s: (0, 0, 0)) if mod.shape[0] == 1 else (lambda b, s: (b // tiles_per_seq, 0, 0))
    rows = pl.BlockSpec((ROWS, D_MODEL), lambda b, s: (b, 0))
    return pl.pallas_call(
        functools.partial(_combine_kernel, n_blk=n_blk, cap_pad=cap_pad),
        out_shape=jax.ShapeDtypeStruct((n_all, D_MODEL), F32),
        grid_spec=pltpu.PrefetchScalarGridSpec(
            num_scalar_prefetch=1, grid=(n_blk,),
            in_specs=[pl.BlockSpec(memory_space=pl.ANY), blk(N_EXPERTS, ROWS), rows,
                      pl.BlockSpec((None, 6, D_MODEL), mod_map)],
            out_specs=rows,
            scratch_shapes=[pltpu.VMEM((2, MAX_UNITS * WIN, D_MODEL), BF16), pltpu.VMEM((MAX_UNITS * WIN, ROWS), F32),
                            pltpu.VMEM((ROWS, D_MODEL), F32), pltpu.SemaphoreType.DMA((2,))]),
        compiler_params=_params(1), name="combine",
    )(s_tab, y, posm, x1, mod)


def _rope_tables(n_tok):
    rows = n_tok // GRID_W
    row = np.repeat(np.arange(rows, dtype=np.float64), GRID_W)
    col = np.tile(np.arange(GRID_W, dtype=np.float64), rows)
    axis_dim = QK_ROPE // 2
    inv_freq = ROPE_THETA ** (-np.arange(0, axis_dim, 2, dtype=np.float64) / axis_dim)
    ang = np.concatenate([row[:, None] * inv_freq, col[:, None] * inv_freq], axis=-1)
    cos = np.ones((n_tok, LANES))
    sin = np.zeros((n_tok, LANES))
    cos[:, ROPE_LANE0:ROPE_LANE0 + QK_ROPE] = np.repeat(np.cos(ang), 2, axis=1)
    sgn = np.tile(np.array([-1.0, 1.0]), QK_ROPE // 2)
    sin[:, ROPE_LANE0:ROPE_LANE0 + QK_ROPE] = np.repeat(np.sin(ang), 2, axis=1) * sgn
    return jnp.asarray(cos, F32), jnp.asarray(sin, F32)


def _dft_tables(n_tok):
    def cs(n):
        k = np.arange(n)
        ang = 2.0 * np.pi * ((k[:, None] * k[None, :]) % n) / n
        return np.cos(ang), np.sin(ang)
    cc, sc = cs(FNET_CH)
    cn, sn = cs(n_tok)
    return tuple(jnp.asarray(t, F32) for t in (cc, sc, np.concatenate([cn, -sn], axis=1)))


def _pad_heads(w, width):
    lead = w.shape[:-1]
    w = w.reshape(lead + (N_HEADS, width))
    w = jnp.pad(w, [(0, 0)] * len(lead) + [(0, 0), (0, HEAD_PAD - width)])
    return w.reshape(lead + (N_HEADS * HEAD_PAD,))


def _pair_swap_lanes(n_groups):
    perm = np.arange(n_groups * LANES).reshape(n_groups, LANES)
    rot = perm[:, ROPE_LANE0:ROPE_LANE0 + QK_ROPE].reshape(n_groups, QK_ROPE // 2, 2)[:, :, ::-1]
    perm[:, ROPE_LANE0:ROPE_LANE0 + QK_ROPE] = rot.reshape(n_groups, QK_ROPE)
    return perm.reshape(-1)


def _rotary_only(n_groups):
    lane = np.arange(n_groups * LANES) % LANES
    return jnp.asarray((lane >= ROPE_LANE0) & (lane < ROPE_LANE0 + QK_ROPE), F32)


def _head_tables(gain, rope_tabs, n_tok):
    if rope_tabs is None:
        return (jnp.broadcast_to(gain, (n_tok, LANES)),)
    cos, sin = rope_tabs
    return gain * cos, gain[:, _pair_swap_lanes(1)] * sin


def _layer0_weights(norm1, norm2, w_in, q_a_norm, w_q_up, q_norm, kv_a_norm, w_kv_up, k_norm, conv_w, w_o):
    c0 = Q_LORA + KV_LORA
    rope_cols = jnp.pad(w_in[:, c0:c0 + QK_ROPE], ((0, 0), (ROPE_LANE0, LANES - ROPE_LANE0 - QK_ROPE)))
    w_in_pad = jnp.concatenate([w_in[:, :c0], rope_cols, rope_cols[:, _pair_swap_lanes(1)],
                                w_in[:, c0 + QK_ROPE:]], axis=1)
    kv = w_kv_up.reshape(KV_LORA, N_HEADS, QK_NOPE + V_HEAD)
    w_q = _pad_heads(w_q_up, QK_DIM)
    head_gain = lambda g: jnp.pad(g, (0, HEAD_PAD - QK_DIM)).reshape(1, -1)
    return dict(
        norm1=norm1.reshape(1, -1), norm2=norm2.reshape(1, -1), w_in=w_in_pad.astype(BF16),
        q_a_norm=q_a_norm.reshape(1, -1), w_q=w_q.astype(BF16),
        w_q_swap=(w_q[:, _pair_swap_lanes(N_HEADS)] * _rotary_only(N_HEADS)).astype(BF16),
        q_gain=head_gain(q_norm) * (QK_DIM ** -0.5), kv_a_norm=kv_a_norm.reshape(1, -1),
        w_k=_pad_heads(kv[:, :, :QK_NOPE].reshape(KV_LORA, -1), QK_NOPE).astype(BF16),
        w_v=kv[:, :, QK_NOPE:].reshape(KV_LORA, -1).astype(BF16), k_gain=head_gain(k_norm),
        conv_w=conv_w, w_o=w_o.astype(BF16))


def _router_weights(w_router):
    hi, lo = _split_hi_lo(w_router.T)
    return dict(wr_hi=hi, wr_lo=lo)


def _moe(x1, h2, aff, mod, tri, wg, wu, wd, n_tok):
    n_all = x1.shape[0]
    cap = CAPACITY_FACTOR * n_all // N_EXPERTS
    posm, s_tab = _route(aff, tri, cap)
    return _expert_ffn(x1, h2, posm, aff, s_tab, mod, wg, wu, wd, cap, n_tok)


def _trunk(x, mods, l0, l1, tri, ctx, rope_tabs, dft):
    n_b, n_tok, _ = x.shape
    mod0, mod1 = mods
    q, k, v, conv, ckv, kr = _front0(x, mod0, l0, rope_tabs)
    if ctx is not None:
        kc, vc = _ctx_kv(ctx[0], ctx[1], l0)
        attn = _attention(q, k, v, kc, vc)
    else:
        attn = _attention(q, k, v)
    flat = lambda a: a.reshape(n_b * n_tok, a.shape[-1])
    x1, h2, aff = _post0(flat(x), flat(attn), flat(conv), mod0, l0, n_tok)
    x = _moe(x1, h2, aff, mod0, tri, l0["wg"], l0["wu"], l0["wd"], n_tok)
    x1, h2, aff = _fnet(x.reshape(n_b, n_tok, D_MODEL), mod1, l1, dft)
    x = _moe(flat(x1), flat(h2), aff, mod1, tri, l1["wg"], l1["wu"], l1["wd"], n_tok)
    return x.reshape(n_b, n_tok, D_MODEL), ckv, kr[:, :, ROPE_LANE0:ROPE_LANE0 + QK_ROPE]


def kernel(x_prompt, x_sample, c, cache_c_kv_l0, cache_k_rope_l0, c_ctx, norm1_l0, norm2_l0, w_mod_l0, b_mod_l0, w_in_l0, q_a_norm_l0, w_q_up_l0, q_norm_l0, kv_a_norm_l0, w_kv_up_l0, k_norm_l0, conv_w_l0, w_o_l0, w_router_l0, w_gate_l0, w_up_l0, w_down_l0, norm1_l1, norm2_l1, w_mod_l1, b_mod_l1, w_f_l1, w_router_l1, w_gate_l1, w_up_l1, w_down_l1):
    n_dec = c.shape[0]
    cond = jnp.concatenate([c_ctx[None, :], c, jnp.zeros((16 - 1 - n_dec, D_MODEL), F32)], axis=0)
    m0 = _modulation(cond, w_mod_l0, b_mod_l0)
    m1 = _modulation(cond, w_mod_l1, b_mod_l1)
    mods_prompt = (m0[0:1], m1[0:1])
    mods_sample = (m0[1:1 + n_dec], m1[1:1 + n_dec])

    l0 = _layer0_weights(norm1_l0, norm2_l0, w_in_l0, q_a_norm_l0, w_q_up_l0, q_norm_l0, kv_a_norm_l0,
                         w_kv_up_l0, k_norm_l0, conv_w_l0, w_o_l0)
    l0.update(_router_weights(w_router_l0))
    l0.update(wg=w_gate_l0, wu=w_up_l0, wd=w_down_l0)
    l1 = dict(norm1=norm1_l1.reshape(1, -1), norm2=norm2_l1.reshape(1, -1), w_f=w_f_l1.astype(BF16))
    l1.update(_router_weights(w_router_l1))
    l1.update(wg=w_gate_l1, wu=w_up_l1, wd=w_down_l1)

    tri = jnp.asarray(np.triu(np.ones((ROWS, ROWS)), 1), BF16)
    y_prompt, new_c_kv, new_k_rope = _trunk(x_prompt, mods_prompt, l0, l1, tri, None, None,
                                            _dft_tables(x_prompt.shape[1]))
    k_rope_pad = jnp.pad(cache_k_rope_l0, ((0, 0), (0, 0), (ROPE_LANE0, LANES - ROPE_LANE0 - QK_ROPE)))
    y_sample, _, _ = _trunk(x_sample, mods_sample, l0, l1, tri, (cache_c_kv_l0, k_rope_pad),
                            _rope_tables(x_sample.shape[1]), _dft_tables(x_sample.shape[1]))
    return (y_prompt, y_sample, new_c_kv, new_k_rope)
```

```python
import functools

import jax
import jax.numpy as jnp
import numpy as np
from jax import lax
from jax.experimental import pallas as pl
from jax.experimental.pallas import tpu as pltpu

D_MODEL = 1024
GRID_W = 64
N_HEADS = 8
QK_NOPE = 64
QK_ROPE = 32
QK_DIM = QK_NOPE + QK_ROPE
V_HEAD = 64
Q_LORA = 384
KV_LORA = 256
CONV_CH = 512
FNET_GROUPS = 4
FNET_CH = D_MODEL // FNET_GROUPS
N_EXPERTS = 16
EXPERT_FF = 512
CAPACITY_FACTOR = 2
ROPE_THETA = 10000.0
EPS = 1e-6

LANES = 128
HEAD_PAD = LANES
ROWS = 256
WIDE_ROWS = 512
ALIGN = 16
WIN = 64
MAX_UNITS = N_EXPERTS * (ROWS // WIN)
COMMON_ROWS = N_EXPERTS * WIN
XE_W = D_MODEL + LANES
IN0_PAD = Q_LORA + KV_LORA + 2 * LANES + 3 * CONV_CH
ROPE_LANE0 = QK_NOPE
VMEM_LIMIT = 56 * 1024 * 1024

F32 = jnp.float32
BF16 = jnp.bfloat16


def _dot(a, b):
    return jnp.dot(a, b, preferred_element_type=F32)


def _dot_nt(a, b):
    return lax.dot_general(a, b, (((1,), (1,)), ((), ())), preferred_element_type=F32)


def _split_hi_lo(x):
    hi = x.astype(BF16)
    lo = (x - hi.astype(F32)).astype(BF16)
    return hi, lo


def _params(n_axes):
    return pltpu.CompilerParams(dimension_semantics=("arbitrary",) * n_axes,
                                vmem_limit_bytes=VMEM_LIMIT)


def _rms(x, gain):
    return x * lax.rsqrt(jnp.mean(x * x, axis=-1, keepdims=True) + EPS) * gain


def _mod_kernel(cond_ref, w_ref, b_ref, o_ref):
    c = cond_ref[...]
    s = c * (1.0 / (1.0 + jnp.exp(-c)))
    s_hi, s_lo = _split_hi_lo(s)
    w_hi, w_lo = _split_hi_lo(w_ref[...])
    o_ref[...] = _dot(s_hi, w_hi) + _dot(s_lo, w_hi) + _dot(s_hi, w_lo) + b_ref[...]


def _modulation(cond, w_mod, b_mod):
    n_rows = cond.shape[0]
    tn = 1536
    out = pl.pallas_call(
        _mod_kernel,
        out_shape=jax.ShapeDtypeStruct((n_rows, 6 * D_MODEL), F32),
        grid=(6 * D_MODEL // tn,),
        in_specs=[pl.BlockSpec((n_rows, D_MODEL), lambda i: (0, 0)),
                  pl.BlockSpec((D_MODEL, tn), lambda i: (0, i)),
                  pl.BlockSpec((1, tn), lambda i: (0, i))],
        out_specs=pl.BlockSpec((n_rows, tn), lambda i: (0, i)),
        compiler_params=_params(1),
        name="modulation",
    )(cond, w_mod, b_mod.reshape(1, -1))
    return out.reshape(n_rows, 6, D_MODEL)


def _head_norm_rope(xh, gain_cos, swapped_sin):
    ss = jnp.sum(xh * xh, axis=-1, keepdims=True) * (1.0 / QK_DIM)
    y = xh * gain_cos
    if swapped_sin is not None:
        y = y + swapped_sin
    return y * lax.rsqrt(ss + EPS)


def _expand_kv(ckv_bf, r, swapped_sin, wk_ref, wv_ref, gain_cos, k_ref, v_ref, rows):
    kf = _dot(ckv_bf, wk_ref[...])
    v_ref[rows, :] = _dot(ckv_bf, wv_ref[...]).astype(BF16)
    for h in range(N_HEADS):
        sl = slice(h * HEAD_PAD, (h + 1) * HEAD_PAD)
        k_ref[rows, sl] = _head_norm_rope(kf[:, sl] + r, gain_cos, swapped_sin).astype(BF16)


def _front0_kernel(*refs, n_tok, rope):
    (x_ref, mod_ref, n1_ref, win_ref, qan_ref, wq_ref, kvan_ref, wk_ref, wv_ref, cw_ref,
     qa_ref, ka_ref) = refs[:12]
    pos = 12
    if rope:
        wqs_ref, qb_ref, kb_ref = refs[pos:pos + 3]
        pos += 3
    q_ref, k_ref, v_ref, conv_ref, ckv_ref, kr_ref, cu_s, gb_s = refs[pos:]
    sh1, sc1 = mod_ref[0:1, :], mod_ref[1:2, :]
    c_rope = Q_LORA + KV_LORA
    wide = min(n_tok, WIDE_ROWS)
    for c in range(n_tok // ROWS):
        rows = pl.ds(c * ROWS, ROWS)
        if (c * ROWS) % wide == 0:
            wide_rows = pl.ds(c * ROWS, wide)
            h = _rms(x_ref[wide_rows, :], n1_ref[...]) * (1.0 + sc1) + sh1
            proj_wide = _dot(h.astype(BF16), win_ref[...])
        off = (c * ROWS) % wide
        proj = proj_wide[off:off + ROWS, :]
        cq = _rms(proj[:, :Q_LORA], qan_ref[...]).astype(BF16)
        qf = _dot(cq, wq_ref[...])
        qfs = _dot(cq, wqs_ref[...]) if rope else None
        for hd in range(N_HEADS):
            sl = slice(hd * HEAD_PAD, (hd + 1) * HEAD_PAD)
            q_sin = qfs[:, sl] * qb_ref[rows, :] if rope else None
            q_ref[rows, sl] = _head_norm_rope(qf[:, sl], qa_ref[rows, :], q_sin).astype(BF16)
        ckv = _rms(proj[:, Q_LORA:c_rope], kvan_ref[...])
        ckv_ref[rows, :] = ckv
        r = proj[:, c_rope:c_rope + LANES]
        kr_ref[rows, :] = r
        k_sin = proj[:, c_rope + LANES:c_rope + 2 * LANES] * kb_ref[rows, :] if rope else None
        _expand_kv(ckv.astype(BF16), r, k_sin, wk_ref, wv_ref, ka_ref[rows, :], k_ref, v_ref, rows)
        c0 = c_rope + 2 * LANES
        gb_s[rows, :] = proj[:, c0:c0 + CONV_CH]
        cu_s[rows, :] = proj[:, c0 + CONV_CH:c0 + 2 * CONV_CH] * proj[:, c0 + 2 * CONV_CH:c0 + 3 * CONV_CH]
    cu = cu_s[...]
    row = lax.broadcasted_iota(jnp.int32, cu.shape, 0)
    prev = jnp.where(row == 0, 0.0, pltpu.roll(cu, 1, 0))
    nxt = jnp.where(row == n_tok - 1, 0.0, pltpu.roll(cu, n_tok - 1, 0))
    conv = gb_s[...] * (cw_ref[0:1, :] * prev + cw_ref[1:2, :] * cu + cw_ref[2:3, :] * nxt)
    conv_ref[...] = conv.astype(BF16)


def _const_spec(shape):
    return pl.BlockSpec(shape, lambda b: (0,) * len(shape))


def _front0(x, mod, w, rope_tabs):
    n_b, n_tok, _ = x.shape
    shared_mod = mod.shape[0] == 1
    seq = lambda width: pl.BlockSpec((None, n_tok, width), lambda b: (b, 0, 0))
    in_specs = [seq(D_MODEL),
                pl.BlockSpec((None, 6, D_MODEL), (lambda b: (0, 0, 0)) if shared_mod else (lambda b: (b, 0, 0))),
                _const_spec((1, D_MODEL)), _const_spec((D_MODEL, IN0_PAD)), _const_spec((1, Q_LORA)),
                _const_spec((Q_LORA, N_HEADS * HEAD_PAD)),
                _const_spec((1, KV_LORA)), _const_spec((KV_LORA, N_HEADS * HEAD_PAD)),
                _const_spec((KV_LORA, N_HEADS * V_HEAD)), _const_spec((3, CONV_CH)),
                _const_spec((n_tok, LANES)), _const_spec((n_tok, LANES))]
    q_tabs, k_tabs = _head_tables(w["q_gain"], rope_tabs, n_tok), _head_tables(w["k_gain"], rope_tabs, n_tok)
    args = [x, mod, w["norm1"], w["w_in"], w["q_a_norm"], w["w_q"], w["kv_a_norm"],
            w["w_k"], w["w_v"], w["conv_w"], q_tabs[0], k_tabs[0]]
    if rope_tabs is not None:
        in_specs += [_const_spec((Q_LORA, N_HEADS * HEAD_PAD)), _const_spec((n_tok, LANES)), _const_spec((n_tok, LANES))]
        args += [w["w_q_swap"], q_tabs[1], k_tabs[1]]
    out_shape = [jax.ShapeDtypeStruct((n_b, n_tok, N_HEADS * HEAD_PAD), BF16),
                 jax.ShapeDtypeStruct((n_b, n_tok, N_HEADS * HEAD_PAD), BF16),
                 jax.ShapeDtypeStruct((n_b, n_tok, N_HEADS * V_HEAD), BF16),
                 jax.ShapeDtypeStruct((n_b, n_tok, CONV_CH), BF16),
                 jax.ShapeDtypeStruct((n_b, n_tok, KV_LORA), F32),
                 jax.ShapeDtypeStruct((n_b, n_tok, LANES), F32)]
    out_specs = [seq(N_HEADS * HEAD_PAD), seq(N_HEADS * HEAD_PAD), seq(N_HEADS * V_HEAD), seq(CONV_CH),
                 seq(KV_LORA), seq(LANES)]
    return pl.pallas_call(
        functools.partial(_front0_kernel, n_tok=n_tok, rope=rope_tabs is not None),
        out_shape=out_shape, grid=(n_b,), in_specs=in_specs, out_specs=out_specs,
        scratch_shapes=[pltpu.VMEM((n_tok, CONV_CH), F32), pltpu.VMEM((n_tok, CONV_CH), F32)],
        compiler_params=_params(1), name="front0",
    )(*args)


def _ctx_kv_kernel(ckv_ref, r_ref, wk_ref, wv_ref, ka_ref, k_ref, v_ref):
    rows = pl.ds(0, ckv_ref.shape[0])
    _expand_kv(ckv_ref[...].astype(BF16), r_ref[...], None, wk_ref, wv_ref, ka_ref[...], k_ref, v_ref, rows)


def _ctx_kv(cache_c_kv, cache_k_rope_pad, w):
    n_b, n_ctx, _ = cache_c_kv.shape
    seq = lambda width: pl.BlockSpec((None, n_ctx, width), lambda b: (b, 0, 0))
    return pl.pallas_call(
        _ctx_kv_kernel,
        out_shape=[jax.ShapeDtypeStruct((n_b, n_ctx, N_HEADS * HEAD_PAD), BF16),
                   jax.ShapeDtypeStruct((n_b, n_ctx, N_HEADS * V_HEAD), BF16)],
        grid=(n_b,),
        in_specs=[seq(KV_LORA), seq(LANES), _const_spec((KV_LORA, N_HEADS * HEAD_PAD)),
                  _const_spec((KV_LORA, N_HEADS * V_HEAD)), _const_spec((n_ctx, LANES))],
        out_specs=[seq(N_HEADS * HEAD_PAD), seq(N_HEADS * V_HEAD)],
        compiler_params=_params(1), name="ctx_kv",
    )(cache_c_kv, cache_k_rope_pad, w["w_k"], w["w_v"], _head_tables(w["k_gain"], None, n_ctx)[0])


def _attn_kernel(*refs, with_ctx):
    if with_ctx:
        q_ref, k_ref, v_ref, kc_ref, vc_ref, o_ref = refs
    else:
        q_ref, k_ref, v_ref, o_ref = refs
    lane = lax.broadcasted_iota(jnp.int32, (q_ref.shape[0], LANES), 1)
    for pair in range(N_HEADS // 2):
        vsl = slice(pair * LANES, (pair + 1) * LANES)
        outs = []
        for hd in (2 * pair, 2 * pair + 1):
            sl = slice(hd * HEAD_PAD, (hd + 1) * HEAD_PAD)
            qh = q_ref[:, sl]
            s = _dot_nt(qh, k_ref[:, sl])
            m = jnp.max(s, axis=-1, keepdims=True)
            if with_ctx:
                sc = _dot_nt(qh, kc_ref[:, sl])
                m = jnp.maximum(m, jnp.max(sc, axis=-1, keepdims=True))
                pc = jnp.exp(sc - m)
            p = jnp.exp(s - m)
            den = jnp.sum(p, axis=-1, keepdims=True)
            o = _dot(p.astype(BF16), v_ref[:, vsl])
            if with_ctx:
                den = den + jnp.sum(pc, axis=-1, keepdims=True)
                o = o + _dot(pc.astype(BF16), vc_ref[:, vsl])
            outs.append(o / den)
        o_ref[:, vsl] = jnp.where(lane < V_HEAD, outs[0], outs[1]).astype(BF16)


def _attention(q, k, v, kc=None, vc=None):
    n_b, n_tok, _ = q.shape
    with_ctx = kc is not None
    qspec = lambda width: pl.BlockSpec((None, ROWS, width), lambda b, i: (b, i, 0))
    kvspec = lambda n, width: pl.BlockSpec((None, n, width), lambda b, i: (b, 0, 0))
    in_specs = [qspec(N_HEADS * HEAD_PAD), kvspec(n_tok, N_HEADS * HEAD_PAD), kvspec(n_tok, N_HEADS * V_HEAD)]
    args = [q, k, v]
    if with_ctx:
        n_ctx = kc.shape[1]
        in_specs += [kvspec(n_ctx, N_HEADS * HEAD_PAD), kvspec(n_ctx, N_HEADS * V_HEAD)]
        args += [kc, vc]
    return pl.pallas_call(
        functools.partial(_attn_kernel, with_ctx=with_ctx),
        out_shape=jax.ShapeDtypeStruct((n_b, n_tok, N_HEADS * V_HEAD), BF16),
        grid=(n_b, n_tok // ROWS), in_specs=in_specs, out_specs=qspec(N_HEADS * V_HEAD),
        compiler_params=_params(2), name="attention",
    )(*args)


def _moe_front(x1, mod_ref, n2_ref, wrh_ref, wrl_ref):
    sh2, sc2 = mod_ref[3:4, :], mod_ref[4:5, :]
    h2 = _rms(x1, n2_ref[...]) * (1.0 + sc2) + sh2
    h_hi, h_lo = _split_hi_lo(h2)
    logit = _dot(h_hi, wrh_ref[...]) + _dot(h_lo, wrh_ref[...]) + _dot(h_hi, wrl_ref[...])
    logit = logit.T[0:N_EXPERTS, :]
    e = jnp.exp(logit - jnp.max(logit, axis=0, keepdims=True))
    return h_hi, e / jnp.sum(e, axis=0, keepdims=True)


def _post0_kernel(x_ref, attn_ref, conv_ref, wo_ref, mod_ref, n2_ref, wrh_ref, wrl_ref,
                  x1_ref, h2_ref, aff_ref):
    n_attn = N_HEADS * V_HEAD
    mix = _dot(attn_ref[...], wo_ref[0:n_attn, :]) + _dot(conv_ref[...], wo_ref[n_attn:, :])
    for t in range(x_ref.shape[0] // ROWS):
        rows = pl.ds(t * ROWS, ROWS)
        x1 = x_ref[rows, :] + mod_ref[2:3, :] * mix[t * ROWS:(t + 1) * ROWS, :]
        x1_ref[rows, :] = x1
        h2, aff = _moe_front(x1, mod_ref, n2_ref, wrh_ref, wrl_ref)
        h2_ref[rows, :] = h2
        aff_ref[t] = aff


def _post0(x, attn, conv, mod, w, n_tok):
    n_all = x.shape[0]
    rows = min(n_tok, WIDE_ROWS)
    tiles_per_seq = n_tok // rows
    shared_mod = mod.shape[0] == 1
    tile = lambda width: pl.BlockSpec((rows, width), lambda i: (i, 0))
    const = lambda shape: pl.BlockSpec(shape, lambda i: (0,) * len(shape))
    mod_map = (lambda i: (0, 0, 0)) if shared_mod else (lambda i: (i // tiles_per_seq, 0, 0))
    return pl.pallas_call(
        _post0_kernel,
        out_shape=[jax.ShapeDtypeStruct((n_all, D_MODEL), F32),
                   jax.ShapeDtypeStruct((n_all, D_MODEL), BF16),
                   jax.ShapeDtypeStruct((n_all // ROWS, N_EXPERTS, ROWS), F32)],
        grid=(n_all // rows,),
        in_specs=[tile(D_MODEL), tile(N_HEADS * V_HEAD), tile(CONV_CH), const((D_MODEL, D_MODEL)),
                  pl.BlockSpec((None, 6, D_MODEL), mod_map), const((1, D_MODEL)),
                  const((D_MODEL, LANES)), const((D_MODEL, LANES))],
        out_specs=[tile(D_MODEL), tile(D_MODEL),
                   pl.BlockSpec((rows // ROWS, N_EXPERTS, ROWS), lambda i: (i, 0, 0))],
        compiler_params=_params(1), name="post0",
    )(x, attn, conv, w["w_o"], mod, w["norm2"], w["wr_hi"], w["wr_lo"])


def _fnet_kernel(x_ref, mod_ref, n1_ref, cc_ref, sc_ref, dft_ref, wf_ref, n2_ref, wrh_ref, wrl_ref,
                 x1_ref, h2_ref, aff_ref, y_s, *, n_tok):
    sh1, sc1, g1 = mod_ref[0:1, :], mod_ref[1:2, :], mod_ref[2:3, :]
    n_chunks = n_tok // ROWS
    cc, sc = cc_ref[...].astype(BF16), sc_ref[...].astype(BF16)
    for c in range(n_chunks):
        rows = pl.ds(c * ROWS, ROWS)
        h = (_rms(x_ref[rows, :], n1_ref[...]) * (1.0 + sc1) + sh1).astype(BF16)
        for g in range(FNET_GROUPS):
            sl = slice(g * FNET_CH, (g + 1) * FNET_CH)
            y_s[pl.ds(c * ROWS, ROWS), sl] = _dot(h[:, sl], cc).astype(BF16)
            y_s[pl.ds(n_tok + c * ROWS, ROWS), sl] = _dot(h[:, sl], sc).astype(BF16)
    scale = 1.0 / float(np.sqrt(n_tok * FNET_CH))
    wide = min(n_tok, WIDE_ROWS)
    for c in range(n_chunks):
        rows = pl.ds(c * ROWS, ROWS)
        if (c * ROWS) % wide == 0:
            f = _dot(dft_ref[pl.ds(c * ROWS, wide), :].astype(BF16), y_s[...]) * scale
            mix = _dot(f.astype(BF16), wf_ref[...])
        off = (c * ROWS) % wide
        x1 = x_ref[rows, :] + g1 * mix[off:off + ROWS, :]
        x1_ref[rows, :] = x1
        h2, aff = _moe_front(x1, mod_ref, n2_ref, wrh_ref, wrl_ref)
        h2_ref[rows, :] = h2
        aff_ref[c] = aff


def _fnet(x, mod, w, dft):
    n_b, n_tok, _ = x.shape
    shared_mod = mod.shape[0] == 1
    seq = lambda width: pl.BlockSpec((None, n_tok, width), lambda b: (b, 0, 0))
    cc, sc, dft_n = dft
    tiles = n_tok // ROWS
    return pl.pallas_call(
        functools.partial(_fnet_kernel, n_tok=n_tok),
        out_shape=[jax.ShapeDtypeStruct((n_b, n_tok, D_MODEL), F32),
                   jax.ShapeDtypeStruct((n_b, n_tok, D_MODEL), BF16),
                   jax.ShapeDtypeStruct((n_b * tiles, N_EXPERTS, ROWS), F32)],
        grid=(n_b,),
        in_specs=[seq(D_MODEL),
                  pl.BlockSpec((None, 6, D_MODEL), (lambda b: (0, 0, 0)) if shared_mod else (lambda b: (b, 0, 0))),
                  _const_spec((1, D_MODEL)), _const_spec((FNET_CH, FNET_CH)), _const_spec((FNET_CH, FNET_CH)),
                  pl.BlockSpec((n_tok, 2 * n_tok), lambda b: (0, 0), pipeline_mode=pl.Buffered(1)),
                  _const_spec((D_MODEL, D_MODEL)), _const_spec((1, D_MODEL)),
                  _const_spec((D_MODEL, LANES)), _const_spec((D_MODEL, LANES))],
        out_specs=[seq(D_MODEL), seq(D_MODEL), pl.BlockSpec((tiles, N_EXPERTS, ROWS), lambda b: (b, 0, 0))],
        scratch_shapes=[pltpu.VMEM((2 * n_tok, D_MODEL), BF16)],
        compiler_params=_params(1), name="fnet",
    )(x, mod, w["norm1"], cc, sc, dft_n, w["w_f"], w["norm2"], w["wr_hi"], w["wr_lo"])


def _route_kernel(aff_ref, tri_ref, posm_ref, s_ref, *, n_blk, cap):
    def count(pred):
        acc = jnp.zeros((N_EXPERTS, ROWS), F32)
        for b in range(n_blk):
            acc = acc + jnp.where(pred(aff_ref[b]), 1.0, 0.0)
        return jnp.sum(acc, axis=1, keepdims=True)

    def as_f32(bits):
        return pltpu.bitcast(bits, F32)

    def search(_, carry):
        lo, hi = carry
        mid = lo + ((hi - lo) >> 1)
        mid_f = as_f32(mid)
        ok = count(lambda a: a >= mid_f) >= cap
        return jnp.where(ok, mid, lo), jnp.where(ok, hi, mid)

    one_bits = 0x3F800000
    lo0 = jnp.zeros((N_EXPERTS, 1), jnp.int32)
    hi0 = jnp.full((N_EXPERTS, 1), one_bits + 1, jnp.int32)
    lo, hi = lax.fori_loop(0, 31, search, (lo0, hi0))
    lo_f, ub = as_f32(lo), as_f32(hi)
    thr = lo_f
    pending = jnp.ones((N_EXPERTS, 1), F32)
    for _ in range(3):
        cur = jnp.full((N_EXPERTS, ROWS), -1.0, F32)
        for b in range(n_blk):
            a = aff_ref[b]
            cur = jnp.maximum(cur, jnp.where((a >= lo_f) & (a < ub), a, -1.0))
        cur = jnp.max(cur, axis=1, keepdims=True)
        take = (count(lambda a: a >= cur) >= cap) & (pending > 0.0)
        thr = jnp.where(take, cur, thr)
        pending = jnp.where(take, 0.0, pending)
        ub = cur
    need = cap - count(lambda a: a > thr)

    lane = lax.broadcasted_iota(jnp.int32, (N_EXPERTS, LANES), 1)
    carry_tie = jnp.zeros((N_EXPERTS, 1), F32)
    carry_pos = jnp.zeros((N_EXPERTS, 1), F32)
    s_acc = jnp.zeros((N_EXPERTS, LANES), jnp.int32)
    for b in range(n_blk):
        v = aff_ref[b]
        eq = jnp.where(v == thr, 1.0, 0.0)
        tie_rank = _dot(eq.astype(BF16), tri_ref[...]) + carry_tie
        sel = jnp.where((v > thr) | ((v == thr) & (tie_rank < need)), 1.0, 0.0)
        pos = _dot(sel.astype(BF16), tri_ref[...]) + carry_pos
        posm_ref[b] = jnp.where(sel > 0.0, pos, -1.0)
        s_acc = jnp.where(lane == b, carry_pos.astype(jnp.int32), s_acc)
        carry_tie = carry_tie + jnp.sum(eq, axis=1, keepdims=True)
        n_sel = jnp.sum(sel, axis=1, keepdims=True)
        carry_pos = carry_pos + jnp.floor((n_sel + (ALIGN - 1)) * (1.0 / ALIGN)) * ALIGN
    s_ref[...] = jnp.where(lane == n_blk, carry_pos.astype(jnp.int32), s_acc)


def _route(aff, tri, cap):
    n_blk = aff.shape[0]
    assert n_blk < LANES
    full = lambda shape: pl.BlockSpec(shape, lambda i: (0,) * len(shape))
    posm, s_tab = pl.pallas_call(
        functools.partial(_route_kernel, n_blk=n_blk, cap=cap),
        out_shape=[jax.ShapeDtypeStruct((n_blk, N_EXPERTS, ROWS), F32),
                   jax.ShapeDtypeStruct((N_EXPERTS, LANES), jnp.int32)],
        grid=(1,),
        in_specs=[full((n_blk, N_EXPERTS, ROWS)), full((ROWS, ROWS))],
        out_specs=[full((n_blk, N_EXPERTS, ROWS)), full((N_EXPERTS, LANES))],
        compiler_params=_params(1), name="route",
    )(aff, tri)
    return posm, s_tab[:, :n_blk + 1]


def _ceil_div_pow2(x, d):
    return lax.shift_right_logical(x + (d - 1), d.bit_length() - 1)


def _block_units(s_ref, b):
    base = [s_ref[e, b] for e in range(N_EXPERTS)]
    units = [_ceil_div_pow2(s_ref[e, b + 1] - base[e], WIN) for e in range(N_EXPERTS)]
    first_unit, total = [], 0
    for e in range(N_EXPERTS):
        first_unit.append(total)
        total = total + units[e]
    return base, units, first_unit, total


def _for_units(units, fn):
    for e in range(N_EXPERTS):
        def body(u, _, e=e):
            fn(e, u)
            return 0
        lax.fori_loop(0, units[e], body, 0)


def _window_onehot(posm_ref, e, first, lo):
    slot = lax.broadcasted_iota(jnp.int32, (WIN, ROWS), 0) + first
    p = posm_ref[e:e + 1, :].astype(jnp.int32)
    return jnp.where((p == slot) & (p >= lo), 1.0, 0.0)


def _gate_lanes(e):
    lane = lax.broadcasted_iota(jnp.int32, (1, LANES), 1)
    return ((lane & (N_EXPERTS - 1)) == e) & (lane < 3 * N_EXPERTS)


def _dispatch_kernel(s_ref, h_ref, posm_ref, aff_ref, xe_ref, sel_s, x_s, sem, *, n_blk, cap_pad):
    b = pl.program_id(0)
    buf = b % 2
    base, units, first_unit, n_units = _block_units(s_ref, b)

    @pl.when(b == 0)
    def _():
        sel_s[...] = jnp.zeros_like(sel_s)

    a = jnp.concatenate([aff_ref[...], jnp.zeros((LANES - N_EXPERTS, ROWS), F32)], axis=0).T
    a_hi = a.astype(BF16).astype(F32)
    a_mid = (a - a_hi).astype(BF16).astype(F32)
    a_lo = a - a_hi - a_mid
    parts = (a_hi + pltpu.roll(a_mid, N_EXPERTS, 1) + pltpu.roll(a_lo, 2 * N_EXPERTS, 1)).astype(BF16)

    def unit_rows(first_unit_e, u):
        return pl.ds(pl.multiple_of((first_unit_e + u) * WIN, WIN), WIN)

    def pick(e, u):
        first = base[e] + u * WIN
        sel_s[unit_rows(first_unit[e], u), :] = _window_onehot(posm_ref, e, first, first).astype(BF16)

    _for_units(units, pick)

    def gather(rows):
        sel = sel_s[rows, :]
        x_s[buf, rows, 0:D_MODEL] = _dot(sel, h_ref[...]).astype(BF16)
        x_s[buf, rows, D_MODEL:XE_W] = _dot(sel, parts).astype(BF16)

    def gather_chunk(c, _):
        gather(pl.ds(pl.multiple_of(c * ROWS, ROWS), ROWS))
        return 0

    gather(pl.ds(0, COMMON_ROWS))
    lax.fori_loop(COMMON_ROWS // ROWS, _ceil_div_pow2(n_units * WIN, ROWS), gather_chunk, 0)

    def copies(slot, blk, fn):
        blk_base, blk_units, blk_first, _ = _block_units(s_ref, blk)

        def one(e, u):
            dst = pl.ds(pl.multiple_of(blk_base[e] + u * WIN, ALIGN), WIN)
            fn(pltpu.make_async_copy(x_s.at[slot, unit_rows(blk_first[e], u), :],
                                     xe_ref.at[e, dst, :], sem.at[slot]))

        _for_units(blk_units, one)

    @pl.when(b > 0)
    def _():
        copies(1 - buf, b - 1, lambda c: c.wait())

    copies(buf, b, lambda c: c.start())

    @pl.when(b == n_blk - 1)
    def _():
        copies(buf, b, lambda c: c.wait())
        x_s[buf, 0:WIN, :] = jnp.zeros((WIN, XE_W), BF16)

        def tail(fn):
            for e in range(N_EXPERTS):
                total = s_ref[e, n_blk]
                n_win = (cap_pad - total) // WIN

                def wide(c, _):
                    row = pl.multiple_of(total + c * WIN, ALIGN)
                    fn(pltpu.make_async_copy(x_s.at[buf, pl.ds(0, WIN), :],
                                             xe_ref.at[e, pl.ds(row, WIN), :], sem.at[buf]))
                    return 0

                def narrow(c, _):
                    row = pl.multiple_of(total + n_win * WIN + c * ALIGN, ALIGN)
                    fn(pltpu.make_async_copy(x_s.at[buf, pl.ds(0, ALIGN), :],
                                             xe_ref.at[e, pl.ds(row, ALIGN), :], sem.at[buf]))
                    return 0

                lax.fori_loop(0, n_win, wide, 0)
                lax.fori_loop(0, (cap_pad - total - n_win * WIN) // ALIGN, narrow, 0)

        tail(lambda c: c.start())
        tail(lambda c: c.wait())


def _ffn_kernel(*refs, n_blks):
    n_sets = len(n_blks)
    s_refs, x_refs = refs[:n_sets], refs[n_sets:2 * n_sets]
    wg_ref, wu_ref, wd_ref = refs[2 * n_sets:2 * n_sets + 3]
    y_refs = refs[2 * n_sets + 3:3 * n_sets + 3]
    wg_s, wu_s, wd_s = refs[3 * n_sets + 3:]
    e = pl.program_id(0)
    wg_s[...] = wg_ref[...].astype(BF16)
    wu_s[...] = wu_ref[...].astype(BF16)
    wd_s[...] = wd_ref[...].astype(BF16)
    mine = _gate_lanes(e)

    for s_ref, x_ref, y_ref, n_blk in zip(s_refs, x_refs, y_refs, n_blks):
        n_used = _ceil_div_pow2(s_ref[e, n_blk], ROWS)

        def tile(j, _, x_ref=x_ref, y_ref=y_ref):
            rows = pl.ds(pl.multiple_of(j * ROWS, ROWS), ROWS)
            x = x_ref[rows, 0:D_MODEL]
            pieces = x_ref[rows, D_MODEL:XE_W].astype(F32)
            gate = jnp.sum(jnp.where(mine, pieces, 0.0), axis=1, keepdims=True)
            a = _dot(x, wg_s[...])
            u = _dot(x, wu_s[...])
            hid = a * (1.0 / (1.0 + jnp.exp(-a))) * u * gate
            y_ref[rows, :] = _dot(hid.astype(BF16), wd_s[...]).astype(BF16)
            return 0

        def blank(j, _, y_ref=y_ref):
            y_ref[pl.ds(pl.multiple_of(j * ROWS, ROWS), ROWS), :] = jnp.zeros((ROWS, D_MODEL), BF16)
            return 0

        lax.fori_loop(0, n_used, tile, 0)
        lax.fori_loop(n_used, y_ref.shape[0] // ROWS, blank, 0)


def _combine_kernel(s_ref, y_ref, posm_ref, x1_ref, mod_ref, o_ref, yw_s, sel_s, acc_s, sem,
                    *, n_blk, cap_pad):
    b = pl.program_id(0)
    buf = b % 2
    base, units, first_unit, n_units = _block_units(s_ref, b)

    def unit_rows(first_unit_e, u):
        return pl.ds(pl.multiple_of((first_unit_e + u) * WIN, WIN), WIN)

    def bounds(base_e, u):
        lo = base_e + u * WIN
        return lo, jnp.minimum(lo, cap_pad - WIN)

    def copies(slot, blk, fn):
        blk_base, blk_units, blk_first, _ = _block_units(s_ref, blk)

        def one(e, u):
            src = pl.ds(pl.multiple_of(bounds(blk_base[e], u)[1], ALIGN), WIN)
            fn(pltpu.make_async_copy(y_ref.at[e, src, :], yw_s.at[slot, unit_rows(blk_first[e], u), :],
                                     sem.at[slot]))

        _for_units(blk_units, one)

    @pl.when(b == 0)
    def _():
        yw_s[...] = jnp.zeros_like(yw_s)
        copies(0, 0, lambda c: c.start())

    @pl.when(b + 1 < n_blk)
    def _():
        copies(1 - buf, b + 1, lambda c: c.start())

    def pick(e, u):
        lo, first = bounds(base[e], u)
        sel_s[unit_rows(first_unit[e], u), :] = _window_onehot(posm_ref, e, first, lo)

    _for_units(units, pick)
    n_chunks = jnp.maximum(_ceil_div_pow2(n_units * WIN, ROWS), COMMON_ROWS // ROWS)

    def blank(u, _):
        sel_s[unit_rows(u, 0), :] = jnp.zeros((WIN, ROWS), F32)
        return 0

    lax.fori_loop(n_units, n_chunks * (ROWS // WIN), blank, 0)
    copies(buf, b, lambda c: c.wait())

    def weighted(rows):
        return _dot(sel_s[rows, :].T.astype(BF16), yw_s[buf, rows, :])

    def add_chunk(c, _):
        acc_s[...] += weighted(pl.ds(pl.multiple_of(c * ROWS, ROWS), ROWS))
        return 0

    acc_s[...] = weighted(pl.ds(0, COMMON_ROWS))
    lax.fori_loop(COMMON_ROWS // ROWS, n_chunks, add_chunk, 0)
    o_ref[...] = x1_ref[...] + mod_ref[5:6, :] * acc_s[...]


def _padded_capacity(cap, n_blk):
    return -(-(cap + (ALIGN - 1) * n_blk + WIN) // ROWS) * ROWS


def _block_spec(*shape):
    return pl.BlockSpec((None,) + shape, lambda b, s: (b,) + (0,) * len(shape))


def _dispatch(h2, posm, aff, s_tab, cap):
    n_blk = h2.shape[0] // ROWS
    cap_pad = _padded_capacity(cap, n_blk)
    return pl.pallas_call(
        functools.partial(_dispatch_kernel, n_blk=n_blk, cap_pad=cap_pad),
        out_shape=jax.ShapeDtypeStruct((N_EXPERTS, cap_pad, XE_W), BF16),
        grid_spec=pltpu.PrefetchScalarGridSpec(
            num_scalar_prefetch=1, grid=(n_blk,),
            in_specs=[pl.BlockSpec((ROWS, D_MODEL), lambda b, s: (b, 0)), _block_spec(N_EXPERTS, ROWS),
                      _block_spec(N_EXPERTS, ROWS)],
            out_specs=pl.BlockSpec(memory_space=pl.ANY),
            scratch_shapes=[pltpu.VMEM((MAX_UNITS * WIN, ROWS), BF16), pltpu.VMEM((2, MAX_UNITS * WIN, XE_W), BF16),
                            pltpu.SemaphoreType.DMA((2,))]),
        compiler_params=_params(1), name="dispatch",
    )(s_tab, h2, posm, aff)


def _ffn(xes, s_tabs, wg, wu, wd):
    n_sets = len(xes)
    per_expert = lambda rows, cols: pl.BlockSpec((None, rows, cols), lambda e, *s: (e, 0, 0))
    return pl.pallas_call(
        functools.partial(_ffn_kernel, n_blks=tuple(s.shape[1] - 1 for s in s_tabs)),
        out_shape=[jax.ShapeDtypeStruct(xe.shape[:2] + (D_MODEL,), BF16) for xe in xes],
        grid_spec=pltpu.PrefetchScalarGridSpec(
            num_scalar_prefetch=n_sets, grid=(N_EXPERTS,),
            in_specs=[per_expert(xe.shape[1], XE_W) for xe in xes]
            + [per_expert(D_MODEL, EXPERT_FF), per_expert(D_MODEL, EXPERT_FF), per_expert(EXPERT_FF, D_MODEL)],
            out_specs=[per_expert(xe.shape[1], D_MODEL) for xe in xes],
            scratch_shapes=[pltpu.VMEM((D_MODEL, EXPERT_FF), BF16), pltpu.VMEM((D_MODEL, EXPERT_FF), BF16),
                            pltpu.VMEM((EXPERT_FF, D_MODEL), BF16)]),
        compiler_params=_params(1), name="ffn",
    )(*s_tabs, *xes, wg, wu, wd)


def _combine(y, posm, s_tab, x1, mod, n_tok):
    n_all = x1.shape[0]
    n_blk = n_all // ROWS
    tiles_per_seq = n_tok // ROWS
    mod_map = (lambda b, s: (0, 0, 0)) if mod.shape[0] == 1 else (lambda b, s: (b // tiles_per_seq, 0, 0))
    rows = pl.BlockSpec((ROWS, D_MODEL), lambda b, s: (b, 0))
    return pl.pallas_call(
        functools.partial(_combine_kernel, n_blk=n_blk, cap_pad=y.shape[1]),
        out_shape=jax.ShapeDtypeStruct((n_all, D_MODEL), F32),
        grid_spec=pltpu.PrefetchScalarGridSpec(
            num_scalar_prefetch=1, grid=(n_blk,),
            in_specs=[pl.BlockSpec(memory_space=pl.ANY), _block_spec(N_EXPERTS, ROWS), rows,
                      pl.BlockSpec((None, 6, D_MODEL), mod_map)],
            out_specs=rows,
            scratch_shapes=[pltpu.VMEM((2, MAX_UNITS * WIN, D_MODEL), BF16), pltpu.VMEM((MAX_UNITS * WIN, ROWS), F32),
                            pltpu.VMEM((ROWS, D_MODEL), F32), pltpu.SemaphoreType.DMA((2,))]),
        compiler_params=_params(1), name="combine",
    )(s_tab, y, posm, x1, mod)


def _rope_tables(n_tok):
    rows = n_tok // GRID_W
    row = np.repeat(np.arange(rows, dtype=np.float64), GRID_W)
    col = np.tile(np.arange(GRID_W, dtype=np.float64), rows)
    axis_dim = QK_ROPE // 2
    inv_freq = ROPE_THETA ** (-np.arange(0, axis_dim, 2, dtype=np.float64) / axis_dim)
    ang = np.concatenate([row[:, None] * inv_freq, col[:, None] * inv_freq], axis=-1)
    cos = np.ones((n_tok, LANES))
    sin = np.zeros((n_tok, LANES))
    cos[:, ROPE_LANE0:ROPE_LANE0 + QK_ROPE] = np.repeat(np.cos(ang), 2, axis=1)
    sgn = np.tile(np.array([-1.0, 1.0]), QK_ROPE // 2)
    sin[:, ROPE_LANE0:ROPE_LANE0 + QK_ROPE] = np.repeat(np.sin(ang), 2, axis=1) * sgn
    return jnp.asarray(cos, F32), jnp.asarray(sin, F32)


def _dft_tables(n_tok):
    def cs(n):
        k = np.arange(n)
        ang = 2.0 * np.pi * ((k[:, None] * k[None, :]) % n) / n
        return np.cos(ang), np.sin(ang)
    cc, sc = cs(FNET_CH)
    cn, sn = cs(n_tok)
    return tuple(jnp.asarray(t, F32) for t in (cc, sc, np.concatenate([cn, -sn], axis=1)))


def _pad_heads(w, width):
    lead = w.shape[:-1]
    w = w.reshape(lead + (N_HEADS, width))
    w = jnp.pad(w, [(0, 0)] * len(lead) + [(0, 0), (0, HEAD_PAD - width)])
    return w.reshape(lead + (N_HEADS * HEAD_PAD,))


def _pair_swap_lanes(n_groups):
    perm = np.arange(n_groups * LANES).reshape(n_groups, LANES)
    rot = perm[:, ROPE_LANE0:ROPE_LANE0 + QK_ROPE].reshape(n_groups, QK_ROPE // 2, 2)[:, :, ::-1]
    perm[:, ROPE_LANE0:ROPE_LANE0 + QK_ROPE] = rot.reshape(n_groups, QK_ROPE)
    return perm.reshape(-1)


def _rotary_only(n_groups):
    lane = np.arange(n_groups * LANES) % LANES
    return jnp.asarray((lane >= ROPE_LANE0) & (lane < ROPE_LANE0 + QK_ROPE), F32)


def _head_tables(gain, rope_tabs, n_tok):
    if rope_tabs is None:
        return (jnp.broadcast_to(gain, (n_tok, LANES)),)
    cos, sin = rope_tabs
    return gain * cos, gain[:, _pair_swap_lanes(1)] * sin


def _layer0_weights(norm1, norm2, w_in, q_a_norm, w_q_up, q_norm, kv_a_norm, w_kv_up, k_norm, conv_w, w_o):
    c0 = Q_LORA + KV_LORA
    rope_cols = jnp.pad(w_in[:, c0:c0 + QK_ROPE], ((0, 0), (ROPE_LANE0, LANES - ROPE_LANE0 - QK_ROPE)))
    w_in_pad = jnp.concatenate([w_in[:, :c0], rope_cols, rope_cols[:, _pair_swap_lanes(1)],
                                w_in[:, c0 + QK_ROPE:]], axis=1)
    kv = w_kv_up.reshape(KV_LORA, N_HEADS, QK_NOPE + V_HEAD)
    w_q = _pad_heads(w_q_up, QK_DIM)
    head_gain = lambda g: jnp.pad(g, (0, HEAD_PAD - QK_DIM)).reshape(1, -1)
    return dict(
        norm1=norm1.reshape(1, -1), norm2=norm2.reshape(1, -1), w_in=w_in_pad.astype(BF16),
        q_a_norm=q_a_norm.reshape(1, -1), w_q=w_q.astype(BF16),
        w_q_swap=(w_q[:, _pair_swap_lanes(N_HEADS)] * _rotary_only(N_HEADS)).astype(BF16),
        q_gain=head_gain(q_norm) * (QK_DIM ** -0.5), kv_a_norm=kv_a_norm.reshape(1, -1),
        w_k=_pad_heads(kv[:, :, :QK_NOPE].reshape(KV_LORA, -1), QK_NOPE).astype(BF16),
        w_v=kv[:, :, QK_NOPE:].reshape(KV_LORA, -1).astype(BF16), k_gain=head_gain(k_norm),
        conv_w=conv_w, w_o=w_o.astype(BF16))


def _router_weights(w_router):
    hi, lo = _split_hi_lo(jnp.pad(w_router, ((0, 0), (0, LANES - N_EXPERTS))))
    return dict(wr_hi=hi, wr_lo=lo)


def _moe(sets, tri, w):
    routed = []
    for x1, h2, aff, mod, n_tok in sets:
        cap = CAPACITY_FACTOR * x1.shape[0] // N_EXPERTS
        posm, s_tab = _route(aff, tri, cap)
        routed.append((posm, s_tab, _dispatch(h2, posm, aff, s_tab, cap)))
    ys = _ffn([r[2] for r in routed], [r[1] for r in routed], w["wg"], w["wu"], w["wd"])
    return [_combine(y, posm, s_tab, x1, mod, n_tok)
            for y, (posm, s_tab, _), (x1, _, _, mod, n_tok) in zip(ys, routed, sets)]


def _mixer0(x, mod, l0, ctx, rope_tabs):
    n_b, n_tok, _ = x.shape
    q, k, v, conv, ckv, kr = _front0(x, mod, l0, rope_tabs)
    if ctx is not None:
        kc, vc = _ctx_kv(ctx[0], ctx[1], l0)
        attn = _attention(q, k, v, kc, vc)
    else:
        attn = _attention(q, k, v)
    flat = lambda a: a.reshape(n_b * n_tok, a.shape[-1])
    return _post0(flat(x), flat(attn), flat(conv), mod, l0, n_tok), (ckv, kr[:, :, ROPE_LANE0:ROPE_LANE0 + QK_ROPE])


def kernel(x_prompt, x_sample, c, cache_c_kv_l0, cache_k_rope_l0, c_ctx, norm1_l0, norm2_l0, w_mod_l0, b_mod_l0, w_in_l0, q_a_norm_l0, w_q_up_l0, q_norm_l0, kv_a_norm_l0, w_kv_up_l0, k_norm_l0, conv_w_l0, w_o_l0, w_router_l0, w_gate_l0, w_up_l0, w_down_l0, norm1_l1, norm2_l1, w_mod_l1, b_mod_l1, w_f_l1, w_router_l1, w_gate_l1, w_up_l1, w_down_l1):
    n_dec = c.shape[0]
    cond = jnp.concatenate([c_ctx[None, :], c, jnp.zeros((16 - 1 - n_dec, D_MODEL), F32)], axis=0)
    m0 = _modulation(cond, w_mod_l0, b_mod_l0)
    m1 = _modulation(cond, w_mod_l1, b_mod_l1)
    mods_prompt = (m0[0:1], m1[0:1])
    mods_sample = (m0[1:1 + n_dec], m1[1:1 + n_dec])

    l0 = _layer0_weights(norm1_l0, norm2_l0, w_in_l0, q_a_norm_l0, w_q_up_l0, q_norm_l0, kv_a_norm_l0,
                         w_kv_up_l0, k_norm_l0, conv_w_l0, w_o_l0)
    l0.update(_router_weights(w_router_l0))
    l0.update(wg=w_gate_l0, wu=w_up_l0, wd=w_down_l0)
    l1 = dict(norm1=norm1_l1.reshape(1, -1), norm2=norm2_l1.reshape(1, -1), w_f=w_f_l1.astype(BF16))
    l1.update(_router_weights(w_router_l1))
    l1.update(wg=w_gate_l1, wu=w_up_l1, wd=w_down_l1)

    tri = jnp.asarray(np.triu(np.ones((ROWS, ROWS)), 1), BF16)
    k_rope_pad = jnp.pad(cache_k_rope_l0, ((0, 0), (0, 0), (ROPE_LANE0, LANES - ROPE_LANE0 - QK_ROPE)))
    xs = (x_prompt, x_sample)
    mods = (mods_prompt, mods_sample)
    n_toks = tuple(x.shape[1] for x in xs)

    (front_p, (new_c_kv, new_k_rope)) = _mixer0(x_prompt, mods_prompt[0], l0, None, None)
    (front_s, _) = _mixer0(x_sample, mods_sample[0], l0, (cache_c_kv_l0, k_rope_pad), _rope_tables(n_toks[1]))
    ys = _moe([tuple(front) + (mod[0], n_tok) for front, mod, n_tok in zip((front_p, front_s), mods, n_toks)],
              tri, l0)

    sets = []
    for y, x, mod, n_tok in zip(ys, xs, mods, n_toks):
        x1, h2, aff = _fnet(y.reshape(x.shape), mod[1], l1, _dft_tables(n_tok))
        sets.append((x1.reshape(-1, D_MODEL), h2.reshape(-1, D_MODEL), aff, mod[1], n_tok))
    y_prompt, y_sample = (y.reshape(x.shape) for y, x in zip(_moe(sets, tri, l1), xs))
    return (y_prompt, y_sample, new_c_kv, new_k_rope)
```

```python
import functools

import jax
import jax.numpy as jnp
import numpy as np
from jax import lax
from jax.experimental import pallas as pl
from jax.experimental.pallas import tpu as pltpu

D_MODEL = 1024
GRID_W = 64
N_HEADS = 8
QK_NOPE = 64
QK_ROPE = 32
QK_DIM = QK_NOPE + QK_ROPE
V_HEAD = 64
Q_LORA = 384
KV_LORA = 256
CONV_CH = 512
FNET_GROUPS = 4
FNET_CH = D_MODEL // FNET_GROUPS
N_EXPERTS = 16
EXPERT_FF = 512
CAPACITY_FACTOR = 2
ROPE_THETA = 10000.0
EPS = 1e-6

LANES = 128
HEAD_PAD = LANES
ROWS = 256
WIDE_ROWS = 512
ALIGN = 8
WIN = 64
MAX_UNITS = N_EXPERTS * (ROWS // WIN)
COMMON_ROWS = N_EXPERTS * WIN
XE_W = D_MODEL + LANES
IN0_PAD = Q_LORA + KV_LORA + 2 * LANES + 3 * CONV_CH
ROPE_LANE0 = QK_NOPE
VMEM_LIMIT = 56 * 1024 * 1024

F32 = jnp.float32
BF16 = jnp.bfloat16


def _dot(a, b):
    return jnp.dot(a, b, preferred_element_type=F32)


def _dot_nt(a, b):
    return lax.dot_general(a, b, (((1,), (1,)), ((), ())), preferred_element_type=F32)


def _split_hi_lo(x):
    hi = x.astype(BF16)
    lo = (x - hi.astype(F32)).astype(BF16)
    return hi, lo


def _params(n_axes):
    return pltpu.CompilerParams(dimension_semantics=("arbitrary",) * n_axes,
                                vmem_limit_bytes=VMEM_LIMIT)


def _rms(x, gain):
    return x * lax.rsqrt(jnp.mean(x * x, axis=-1, keepdims=True) + EPS) * gain


def _mod_kernel(cond_ref, w_ref, b_ref, o_ref):
    c = cond_ref[...]
    s = c * (1.0 / (1.0 + jnp.exp(-c)))
    s_hi, s_lo = _split_hi_lo(s)
    w_hi, w_lo = _split_hi_lo(w_ref[...])
    o_ref[...] = _dot(s_hi, w_hi) + _dot(s_lo, w_hi) + _dot(s_hi, w_lo) + b_ref[...]


def _modulation(cond, w_mod, b_mod):
    n_rows = cond.shape[0]
    tn = 1536
    out = pl.pallas_call(
        _mod_kernel,
        out_shape=jax.ShapeDtypeStruct((n_rows, 6 * D_MODEL), F32),
        grid=(6 * D_MODEL // tn,),
        in_specs=[pl.BlockSpec((n_rows, D_MODEL), lambda i: (0, 0)),
                  pl.BlockSpec((D_MODEL, tn), lambda i: (0, i)),
                  pl.BlockSpec((1, tn), lambda i: (0, i))],
        out_specs=pl.BlockSpec((n_rows, tn), lambda i: (0, i)),
        compiler_params=_params(1),
        name="modulation",
    )(cond, w_mod, b_mod.reshape(1, -1))
    return out.reshape(n_rows, 6, D_MODEL)


def _head_norm_rope(xh, gain_cos, swapped_sin):
    ss = jnp.sum(xh * xh, axis=-1, keepdims=True) * (1.0 / QK_DIM)
    y = xh * gain_cos
    if swapped_sin is not None:
        y = y + swapped_sin
    return y * lax.rsqrt(ss + EPS)


def _expand_kv(ckv_bf, r, swapped_sin, wk_ref, wv_ref, gain_cos, k_ref, v_ref, rows):
    kf = _dot(ckv_bf, wk_ref[...])
    lane = lax.broadcasted_iota(jnp.int32, (1, N_HEADS * HEAD_PAD), 1)
    ones = jnp.where((lane & (HEAD_PAD - 1)) == V_HEAD, 1.0, 0.0)
    v_ref[rows, :] = (_dot(ckv_bf, wv_ref[...]) + ones).astype(BF16)
    for h in range(N_HEADS):
        sl = slice(h * HEAD_PAD, (h + 1) * HEAD_PAD)
        k_ref[rows, sl] = _head_norm_rope(kf[:, sl] + r, gain_cos, swapped_sin).astype(BF16)


def _front0_kernel(*refs, n_tok, rope):
    (x_ref, mod_ref, n1_ref, win_ref, qan_ref, wq_ref, kvan_ref, wk_ref, wv_ref, cw_ref,
     qa_ref, ka_ref) = refs[:12]
    pos = 12
    if rope:
        wqs_ref, qb_ref, kb_ref = refs[pos:pos + 3]
        pos += 3
    q_ref, k_ref, v_ref, conv_ref, ckv_ref, kr_ref, cu_s, gb_s = refs[pos:]
    sh1, sc1 = mod_ref[0:1, :], mod_ref[1:2, :]
    c_rope = Q_LORA + KV_LORA
    wide = min(n_tok, WIDE_ROWS)
    for c in range(n_tok // ROWS):
        rows = pl.ds(c * ROWS, ROWS)
        if (c * ROWS) % wide == 0:
            wide_rows = pl.ds(c * ROWS, wide)
            h = _rms(x_ref[wide_rows, :], n1_ref[...]) * (1.0 + sc1) + sh1
            proj_wide = _dot(h.astype(BF16), win_ref[...])
        off = (c * ROWS) % wide
        proj = proj_wide[off:off + ROWS, :]
        cq = _rms(proj[:, :Q_LORA], qan_ref[...]).astype(BF16)
        qf = _dot(cq, wq_ref[...])
        qfs = _dot(cq, wqs_ref[...]) if rope else None
        for hd in range(N_HEADS):
            sl = slice(hd * HEAD_PAD, (hd + 1) * HEAD_PAD)
            q_sin = qfs[:, sl] * qb_ref[rows, :] if rope else None
            q_ref[rows, sl] = _head_norm_rope(qf[:, sl], qa_ref[rows, :], q_sin).astype(BF16)
        ckv = _rms(proj[:, Q_LORA:c_rope], kvan_ref[...])
        ckv_ref[rows, :] = ckv
        r = proj[:, c_rope:c_rope + LANES]
        kr_ref[rows, :] = r
        k_sin = proj[:, c_rope + LANES:c_rope + 2 * LANES] * kb_ref[rows, :] if rope else None
        _expand_kv(ckv.astype(BF16), r, k_sin, wk_ref, wv_ref, ka_ref[rows, :], k_ref, v_ref, rows)
        c0 = c_rope + 2 * LANES
        gb_s[rows, :] = proj[:, c0:c0 + CONV_CH]
        cu_s[rows, :] = proj[:, c0 + CONV_CH:c0 + 2 * CONV_CH] * proj[:, c0 + 2 * CONV_CH:c0 + 3 * CONV_CH]
    cu = cu_s[...]
    row = lax.broadcasted_iota(jnp.int32, cu.shape, 0)
    prev = jnp.where(row == 0, 0.0, pltpu.roll(cu, 1, 0))
    nxt = jnp.where(row == n_tok - 1, 0.0, pltpu.roll(cu, n_tok - 1, 0))
    conv = gb_s[...] * (cw_ref[0:1, :] * prev + cw_ref[1:2, :] * cu + cw_ref[2:3, :] * nxt)
    conv_ref[...] = conv.astype(BF16)


def _const_spec(shape):
    return pl.BlockSpec(shape, lambda b: (0,) * len(shape))


def _front0(x, mod, w, rope_tabs):
    n_b, n_tok, _ = x.shape
    shared_mod = mod.shape[0] == 1
    seq = lambda width: pl.BlockSpec((None, n_tok, width), lambda b: (b, 0, 0))
    in_specs = [seq(D_MODEL),
                pl.BlockSpec((None, 6, D_MODEL), (lambda b: (0, 0, 0)) if shared_mod else (lambda b: (b, 0, 0))),
                _const_spec((1, D_MODEL)), _const_spec((D_MODEL, IN0_PAD)), _const_spec((1, Q_LORA)),
                _const_spec((Q_LORA, N_HEADS * HEAD_PAD)),
                _const_spec((1, KV_LORA)), _const_spec((KV_LORA, N_HEADS * HEAD_PAD)),
                _const_spec((KV_LORA, N_HEADS * HEAD_PAD)), _const_spec((3, CONV_CH)),
                _const_spec((n_tok, LANES)), _const_spec((n_tok, LANES))]
    q_tabs, k_tabs = _head_tables(w["q_gain"], rope_tabs, n_tok), _head_tables(w["k_gain"], rope_tabs, n_tok)
    args = [x, mod, w["norm1"], w["w_in"], w["q_a_norm"], w["w_q"], w["kv_a_norm"],
            w["w_k"], w["w_v"], w["conv_w"], q_tabs[0], k_tabs[0]]
    if rope_tabs is not None:
        in_specs += [_const_spec((Q_LORA, N_HEADS * HEAD_PAD)), _const_spec((n_tok, LANES)), _const_spec((n_tok, LANES))]
        args += [w["w_q_swap"], q_tabs[1], k_tabs[1]]
    out_shape = [jax.ShapeDtypeStruct((n_b, n_tok, N_HEADS * HEAD_PAD), BF16),
                 jax.ShapeDtypeStruct((n_b, n_tok, N_HEADS * HEAD_PAD), BF16),
                 jax.ShapeDtypeStruct((n_b, n_tok, N_HEADS * HEAD_PAD), BF16),
                 jax.ShapeDtypeStruct((n_b, n_tok, CONV_CH), BF16),
                 jax.ShapeDtypeStruct((n_b, n_tok, KV_LORA), F32),
                 jax.ShapeDtypeStruct((n_b, n_tok, LANES), F32)]
    out_specs = [seq(N_HEADS * HEAD_PAD), seq(N_HEADS * HEAD_PAD), seq(N_HEADS * HEAD_PAD), seq(CONV_CH),
                 seq(KV_LORA), seq(LANES)]
    return pl.pallas_call(
        functools.partial(_front0_kernel, n_tok=n_tok, rope=rope_tabs is not None),
        out_shape=out_shape, grid=(n_b,), in_specs=in_specs, out_specs=out_specs,
        scratch_shapes=[pltpu.VMEM((n_tok, CONV_CH), F32), pltpu.VMEM((n_tok, CONV_CH), F32)],
        compiler_params=_params(1), name="front0",
    )(*args)


def _ctx_kv_kernel(ckv_ref, r_ref, wk_ref, wv_ref, ka_ref, k_ref, v_ref):
    rows = pl.ds(0, ckv_ref.shape[0])
    _expand_kv(ckv_ref[...].astype(BF16), r_ref[...], None, wk_ref, wv_ref, ka_ref[...], k_ref, v_ref, rows)


def _ctx_kv(cache_c_kv, cache_k_rope_pad, w):
    n_b, n_ctx, _ = cache_c_kv.shape
    seq = lambda width: pl.BlockSpec((None, n_ctx, width), lambda b: (b, 0, 0))
    return pl.pallas_call(
        _ctx_kv_kernel,
        out_shape=[jax.ShapeDtypeStruct((n_b, n_ctx, N_HEADS * HEAD_PAD), BF16),
                   jax.ShapeDtypeStruct((n_b, n_ctx, N_HEADS * HEAD_PAD), BF16)],
        grid=(n_b,),
        in_specs=[seq(KV_LORA), seq(LANES), _const_spec((KV_LORA, N_HEADS * HEAD_PAD)),
                  _const_spec((KV_LORA, N_HEADS * HEAD_PAD)), _const_spec((n_ctx, LANES))],
        out_specs=[seq(N_HEADS * HEAD_PAD), seq(N_HEADS * HEAD_PAD)],
        compiler_params=_params(1), name="ctx_kv",
    )(cache_c_kv, cache_k_rope_pad, w["w_k"], w["w_v"], _head_tables(w["k_gain"], None, n_ctx)[0])


def _attn_kernel(*refs, with_ctx):
    if with_ctx:
        q_ref, k_ref, v_ref, kc_ref, vc_ref, o_ref = refs
    else:
        q_ref, k_ref, v_ref, o_ref = refs
    lane = lax.broadcasted_iota(jnp.int32, (q_ref.shape[0], LANES), 1)
    for pair in range(N_HEADS // 2):
        outs = []
        for hd in (2 * pair, 2 * pair + 1):
            sl = slice(hd * HEAD_PAD, (hd + 1) * HEAD_PAD)
            qh = q_ref[:, sl]
            s = _dot_nt(qh, k_ref[:, sl])
            m = jnp.max(s, axis=-1, keepdims=True)
            if with_ctx:
                sc = _dot_nt(qh, kc_ref[:, sl])
                m = jnp.maximum(m, jnp.max(sc, axis=-1, keepdims=True))
            o = _dot(jnp.exp(s - m).astype(BF16), v_ref[:, sl])
            if with_ctx:
                o = o + _dot(jnp.exp(sc - m).astype(BF16), vc_ref[:, sl])
            outs.append(o / o[:, V_HEAD:V_HEAD + 1])
        both = jnp.where(lane < V_HEAD, outs[0], pltpu.roll(outs[1], V_HEAD, 1))
        o_ref[:, pair * LANES:(pair + 1) * LANES] = both.astype(BF16)


def _attention(q, k, v, kc=None, vc=None):
    n_b, n_tok, _ = q.shape
    with_ctx = kc is not None
    tq = min(n_tok, WIDE_ROWS)
    qspec = lambda width: pl.BlockSpec((None, tq, width), lambda b, i: (b, i, 0))
    kvspec = lambda n: pl.BlockSpec((None, n, N_HEADS * HEAD_PAD), lambda b, i: (b, 0, 0))
    in_specs = [qspec(N_HEADS * HEAD_PAD), kvspec(n_tok), kvspec(n_tok)]
    args = [q, k, v]
    if with_ctx:
        in_specs += [kvspec(kc.shape[1])] * 2
        args += [kc, vc]
    return pl.pallas_call(
        functools.partial(_attn_kernel, with_ctx=with_ctx),
        out_shape=jax.ShapeDtypeStruct((n_b, n_tok, N_HEADS * V_HEAD), BF16),
        grid=(n_b, n_tok // tq), in_specs=in_specs, out_specs=qspec(N_HEADS * V_HEAD),
        compiler_params=_params(2), name="attention",
    )(*args)


def _moe_front(x1, mod_ref, n2_ref, wrh_ref, wrl_ref):
    sh2, sc2 = mod_ref[3:4, :], mod_ref[4:5, :]
    h2 = _rms(x1, n2_ref[...]) * (1.0 + sc2) + sh2
    h_hi, h_lo = _split_hi_lo(h2)
    logit = _dot(h_hi, wrh_ref[...]) + _dot(h_lo, wrh_ref[...]) + _dot(h_hi, wrl_ref[...])
    logit = logit.T[0:N_EXPERTS, :]
    e = jnp.exp(logit - jnp.max(logit, axis=0, keepdims=True))
    return h_hi, e / jnp.sum(e, axis=0, keepdims=True)


def _post0_kernel(x_ref, attn_ref, conv_ref, wo_ref, mod_ref, n2_ref, wrh_ref, wrl_ref,
                  x1_ref, h2_ref, aff_ref):
    n_attn = N_HEADS * V_HEAD
    mix = _dot(attn_ref[...], wo_ref[0:n_attn, :]) + _dot(conv_ref[...], wo_ref[n_attn:, :])
    for t in range(x_ref.shape[0] // ROWS):
        rows = pl.ds(t * ROWS, ROWS)
        x1 = x_ref[rows, :] + mod_ref[2:3, :] * mix[t * ROWS:(t + 1) * ROWS, :]
        x1_ref[rows, :] = x1
        h2, aff = _moe_front(x1, mod_ref, n2_ref, wrh_ref, wrl_ref)
        h2_ref[rows, :] = h2
        aff_ref[t] = aff


def _post0(x, attn, conv, mod, w, n_tok):
    n_all = x.shape[0]
    rows = min(n_tok, WIDE_ROWS)
    tiles_per_seq = n_tok // rows
    shared_mod = mod.shape[0] == 1
    tile = lambda width: pl.BlockSpec((rows, width), lambda i: (i, 0))
    const = lambda shape: pl.BlockSpec(shape, lambda i: (0,) * len(shape))
    mod_map = (lambda i: (0, 0, 0)) if shared_mod else (lambda i: (i // tiles_per_seq, 0, 0))
    return pl.pallas_call(
        _post0_kernel,
        out_shape=[jax.ShapeDtypeStruct((n_all, D_MODEL), F32),
                   jax.ShapeDtypeStruct((n_all, D_MODEL), BF16),
                   jax.ShapeDtypeStruct((n_all // ROWS, N_EXPERTS, ROWS), F32)],
        grid=(n_all // rows,),
        in_specs=[tile(D_MODEL), tile(N_HEADS * V_HEAD), tile(CONV_CH), const((D_MODEL, D_MODEL)),
                  pl.BlockSpec((None, 6, D_MODEL), mod_map), const((1, D_MODEL)),
                  const((D_MODEL, LANES)), const((D_MODEL, LANES))],
        out_specs=[tile(D_MODEL), tile(D_MODEL),
                   pl.BlockSpec((rows // ROWS, N_EXPERTS, ROWS), lambda i: (i, 0, 0))],
        compiler_params=_params(1), name="post0",
    )(x, attn, conv, w["w_o"], mod, w["norm2"], w["wr_hi"], w["wr_lo"])


def _fnet_kernel(x_ref, mod_ref, n1_ref, cc_ref, sc_ref, dft_ref, wf_ref, n2_ref, wrh_ref, wrl_ref,
                 x1_ref, h2_ref, aff_ref, y_s, *, n_tok):
    sh1, sc1, g1 = mod_ref[0:1, :], mod_ref[1:2, :], mod_ref[2:3, :]
    n_chunks = n_tok // ROWS
    cc, sc = cc_ref[...].astype(BF16), sc_ref[...].astype(BF16)
    for c in range(n_chunks):
        rows = pl.ds(c * ROWS, ROWS)
        h = (_rms(x_ref[rows, :], n1_ref[...]) * (1.0 + sc1) + sh1).astype(BF16)
        for g in range(FNET_GROUPS):
            sl = slice(g * FNET_CH, (g + 1) * FNET_CH)
            y_s[pl.ds(c * ROWS, ROWS), sl] = _dot(h[:, sl], cc).astype(BF16)
            y_s[pl.ds(n_tok + c * ROWS, ROWS), sl] = _dot(h[:, sl], sc).astype(BF16)
    scale = 1.0 / float(np.sqrt(n_tok * FNET_CH))
    wide = min(n_tok, WIDE_ROWS)
    for c in range(n_chunks):
        rows = pl.ds(c * ROWS, ROWS)
        if (c * ROWS) % wide == 0:
            f = _dot(dft_ref[pl.ds(c * ROWS, wide), :].astype(BF16), y_s[...]) * scale
            mix = _dot(f.astype(BF16), wf_ref[...])
        off = (c * ROWS) % wide
        x1 = x_ref[rows, :] + g1 * mix[off:off + ROWS, :]
        x1_ref[rows, :] = x1
        h2, aff = _moe_front(x1, mod_ref, n2_ref, wrh_ref, wrl_ref)
        h2_ref[rows, :] = h2
        aff_ref[c] = aff


def _fnet(x, mod, w, dft):
    n_b, n_tok, _ = x.shape
    shared_mod = mod.shape[0] == 1
    seq = lambda width: pl.BlockSpec((None, n_tok, width), lambda b: (b, 0, 0))
    cc, sc, dft_n = dft
    tiles = n_tok // ROWS
    return pl.pallas_call(
        functools.partial(_fnet_kernel, n_tok=n_tok),
        out_shape=[jax.ShapeDtypeStruct((n_b, n_tok, D_MODEL), F32),
                   jax.ShapeDtypeStruct((n_b, n_tok, D_MODEL), BF16),
                   jax.ShapeDtypeStruct((n_b * tiles, N_EXPERTS, ROWS), F32)],
        grid=(n_b,),
        in_specs=[seq(D_MODEL),
                  pl.BlockSpec((None, 6, D_MODEL), (lambda b: (0, 0, 0)) if shared_mod else (lambda b: (b, 0, 0))),
                  _const_spec((1, D_MODEL)), _const_spec((FNET_CH, FNET_CH)), _const_spec((FNET_CH, FNET_CH)),
                  pl.BlockSpec((n_tok, 2 * n_tok), lambda b: (0, 0), pipeline_mode=pl.Buffered(1)),
                  _const_spec((D_MODEL, D_MODEL)), _const_spec((1, D_MODEL)),
                  _const_spec((D_MODEL, LANES)), _const_spec((D_MODEL, LANES))],
        out_specs=[seq(D_MODEL), seq(D_MODEL), pl.BlockSpec((tiles, N_EXPERTS, ROWS), lambda b: (b, 0, 0))],
        scratch_shapes=[pltpu.VMEM((2 * n_tok, D_MODEL), BF16)],
        compiler_params=_params(1), name="fnet",
    )(x, mod, w["norm1"], cc, sc, dft_n, w["w_f"], w["norm2"], w["wr_hi"], w["wr_lo"])


def _route_kernel(aff_ref, tri_ref, posm_ref, s_ref, *, n_blk, cap):
    def count(pred):
        acc = jnp.zeros((N_EXPERTS, ROWS), F32)
        for b in range(n_blk):
            acc = acc + jnp.where(pred(aff_ref[b]), 1.0, 0.0)
        return jnp.sum(acc, axis=1, keepdims=True)

    def as_f32(bits):
        return pltpu.bitcast(bits, F32)

    def search(_, carry):
        lo, hi = carry
        mid = lo + ((hi - lo) >> 1)
        mid_f = as_f32(mid)
        ok = count(lambda a: a >= mid_f) >= cap
        return jnp.where(ok, mid, lo), jnp.where(ok, hi, mid)

    one_bits = 0x3F800000
    lo0 = jnp.zeros((N_EXPERTS, 1), jnp.int32)
    hi0 = jnp.full((N_EXPERTS, 1), one_bits + 1, jnp.int32)
    lo, hi = lax.fori_loop(0, 31, search, (lo0, hi0))
    lo_f, ub = as_f32(lo), as_f32(hi)
    thr = lo_f
    pending = jnp.ones((N_EXPERTS, 1), F32)
    for _ in range(3):
        cur = jnp.full((N_EXPERTS, ROWS), -1.0, F32)
        for b in range(n_blk):
            a = aff_ref[b]
            cur = jnp.maximum(cur, jnp.where((a >= lo_f) & (a < ub), a, -1.0))
        cur = jnp.max(cur, axis=1, keepdims=True)
        take = (count(lambda a: a >= cur) >= cap) & (pending > 0.0)
        thr = jnp.where(take, cur, thr)
        pending = jnp.where(take, 0.0, pending)
        ub = cur
    need = cap - count(lambda a: a > thr)

    lane = lax.broadcasted_iota(jnp.int32, (N_EXPERTS, LANES), 1)
    carry_tie = jnp.zeros((N_EXPERTS, 1), F32)
    carry_pos = jnp.zeros((N_EXPERTS, 1), F32)
    s_acc = jnp.zeros((N_EXPERTS, LANES), jnp.int32)
    for b in range(n_blk):
        v = aff_ref[b]
        eq = jnp.where(v == thr, 1.0, 0.0)
        tie_rank = _dot(eq.astype(BF16), tri_ref[...]) + carry_tie
        sel = jnp.where((v > thr) | ((v == thr) & (tie_rank < need)), 1.0, 0.0)
        pos = _dot(sel.astype(BF16), tri_ref[...]) + carry_pos
        posm_ref[b] = jnp.where(sel > 0.0, pos, -1.0)
        s_acc = jnp.where(lane == b, carry_pos.astype(jnp.int32), s_acc)
        carry_tie = carry_tie + jnp.sum(eq, axis=1, keepdims=True)
        n_sel = jnp.sum(sel, axis=1, keepdims=True)
        carry_pos = carry_pos + jnp.floor((n_sel + (ALIGN - 1)) * (1.0 / ALIGN)) * ALIGN
    s_ref[...] = jnp.where(lane == n_blk, carry_pos.astype(jnp.int32), s_acc)


def _route(aff, tri, cap):
    n_blk = aff.shape[0]
    assert n_blk < LANES
    full = lambda shape: pl.BlockSpec(shape, lambda i: (0,) * len(shape))
    posm, s_tab = pl.pallas_call(
        functools.partial(_route_kernel, n_blk=n_blk, cap=cap),
        out_shape=[jax.ShapeDtypeStruct((n_blk, N_EXPERTS, ROWS), F32),
                   jax.ShapeDtypeStruct((N_EXPERTS, LANES), jnp.int32)],
        grid=(1,),
        in_specs=[full((n_blk, N_EXPERTS, ROWS)), full((ROWS, ROWS))],
        out_specs=[full((n_blk, N_EXPERTS, ROWS)), full((N_EXPERTS, LANES))],
        compiler_params=_params(1), name="route",
    )(aff, tri)
    return posm, s_tab[:, :n_blk + 1]


def _ceil_div_pow2(x, d):
    return lax.shift_right_logical(x + (d - 1), d.bit_length() - 1)


def _block_units(s_ref, b):
    base = [s_ref[e, b] for e in range(N_EXPERTS)]
    units = [_ceil_div_pow2(s_ref[e, b + 1] - base[e], WIN) for e in range(N_EXPERTS)]
    first_unit, total = [], 0
    for e in range(N_EXPERTS):
        first_unit.append(total)
        total = total + units[e]
    return base, units, first_unit, total


def _for_units(units, fn):
    for e in range(N_EXPERTS):
        def body(u, _, e=e):
            fn(e, u)
            return 0
        lax.fori_loop(0, units[e], body, 0)


def _window_onehot(posm_ref, e, first, lo):
    slot = lax.broadcasted_iota(jnp.int32, (WIN, ROWS), 0) + first
    p = posm_ref[e:e + 1, :].astype(jnp.int32)
    return jnp.where((p == slot) & (p >= lo), 1.0, 0.0)


def _gate_lanes(e):
    lane = lax.broadcasted_iota(jnp.int32, (1, LANES), 1)
    return ((lane & (N_EXPERTS - 1)) == e) & (lane < 3 * N_EXPERTS)


def _dispatch_kernel(s_ref, h_ref, posm_ref, aff_ref, xe_ref, sel_s, x_s, sem, *, n_blk, cap_pad):
    b = pl.program_id(0)
    buf = b % 2
    base, units, first_unit, n_units = _block_units(s_ref, b)

    @pl.when(b == 0)
    def _():
        sel_s[...] = jnp.zeros_like(sel_s)

    a = jnp.concatenate([aff_ref[...], jnp.zeros((LANES - N_EXPERTS, ROWS), F32)], axis=0).T
    a_hi = a.astype(BF16).astype(F32)
    a_mid = (a - a_hi).astype(BF16).astype(F32)
    a_lo = a - a_hi - a_mid
    parts = (a_hi + pltpu.roll(a_mid, N_EXPERTS, 1) + pltpu.roll(a_lo, 2 * N_EXPERTS, 1)).astype(BF16)

    def unit_rows(first_unit_e, u):
        return pl.ds(pl.multiple_of((first_unit_e + u) * WIN, WIN), WIN)

    def pick(e, u):
        first = base[e] + u * WIN
        sel_s[unit_rows(first_unit[e], u), :] = _window_onehot(posm_ref, e, first, first).astype(BF16)

    _for_units(units, pick)

    def gather(rows):
        sel = sel_s[rows, :]
        x_s[buf, rows, 0:D_MODEL] = _dot(sel, h_ref[...]).astype(BF16)
        x_s[buf, rows, D_MODEL:XE_W] = _dot(sel, parts).astype(BF16)

    def gather_chunk(c, _):
        gather(pl.ds(pl.multiple_of(c * ROWS, ROWS), ROWS))
        return 0

    gather(pl.ds(0, COMMON_ROWS))
    lax.fori_loop(COMMON_ROWS // ROWS, _ceil_div_pow2(n_units * WIN, ROWS), gather_chunk, 0)

    def copies(slot, blk, fn):
        blk_base, blk_units, blk_first, _ = _block_units(s_ref, blk)

        def one(e, u):
            dst = pl.ds(pl.multiple_of(blk_base[e] + u * WIN, ALIGN), WIN)
            fn(pltpu.make_async_copy(x_s.at[slot, unit_rows(blk_first[e], u), :],
                                     xe_ref.at[e, dst, :], sem.at[slot]))

        _for_units(blk_units, one)

    @pl.when(b > 0)
    def _():
        copies(1 - buf, b - 1, lambda c: c.wait())

    copies(buf, b, lambda c: c.start())

    @pl.when(b == n_blk - 1)
    def _():
        copies(buf, b, lambda c: c.wait())
        x_s[buf, 0:WIN, :] = jnp.zeros((WIN, XE_W), BF16)

        def tail(fn):
            for e in range(N_EXPERTS):
                total = s_ref[e, n_blk]
                n_win = (cap_pad - total) // WIN

                def wide(c, _):
                    row = pl.multiple_of(total + c * WIN, ALIGN)
                    fn(pltpu.make_async_copy(x_s.at[buf, pl.ds(0, WIN), :],
                                             xe_ref.at[e, pl.ds(row, WIN), :], sem.at[buf]))
                    return 0

                def narrow(c, _):
                    row = pl.multiple_of(total + n_win * WIN + c * ALIGN, ALIGN)
                    fn(pltpu.make_async_copy(x_s.at[buf, pl.ds(0, ALIGN), :],
                                             xe_ref.at[e, pl.ds(row, ALIGN), :], sem.at[buf]))
                    return 0

                lax.fori_loop(0, n_win, wide, 0)
                lax.fori_loop(0, (cap_pad - total - n_win * WIN) // ALIGN, narrow, 0)

        tail(lambda c: c.start())
        tail(lambda c: c.wait())


def _ffn_kernel(*refs, n_blks):
    n_sets = len(n_blks)
    s_refs, x_refs = refs[:n_sets], refs[n_sets:2 * n_sets]
    wg_ref, wu_ref, wd_ref = refs[2 * n_sets:2 * n_sets + 3]
    y_refs = refs[2 * n_sets + 3:3 * n_sets + 3]
    wg_s, wu_s, wd_s = refs[3 * n_sets + 3:]
    e = pl.program_id(0)
    wg_s[...] = wg_ref[...].astype(BF16)
    wu_s[...] = wu_ref[...].astype(BF16)
    wd_s[...] = wd_ref[...].astype(BF16)
    mine = _gate_lanes(e)

    half = ROWS // 2
    for s_ref, x_ref, y_ref, n_blk in zip(s_refs, x_refs, y_refs, n_blks):
        n_half = _ceil_div_pow2(s_ref[e, n_blk], half)

        def tile(rows, x_ref=x_ref, y_ref=y_ref):
            x = x_ref[rows, 0:D_MODEL]
            pieces = x_ref[rows, D_MODEL:XE_W].astype(F32)
            gate = jnp.sum(jnp.where(mine, pieces, 0.0), axis=1, keepdims=True)
            a = _dot(x, wg_s[...])
            u = _dot(x, wu_s[...])
            hid = a * (1.0 / (1.0 + jnp.exp(-a))) * u * gate
            y_ref[rows, :] = _dot(hid.astype(BF16), wd_s[...]).astype(BF16)

        def full(j, _, tile=tile):
            tile(pl.ds(pl.multiple_of(j * ROWS, ROWS), ROWS))
            return 0

        def blank(j, _, y_ref=y_ref):
            y_ref[pl.ds(pl.multiple_of(j * half, half), half), :] = jnp.zeros((half, D_MODEL), BF16)
            return 0

        lax.fori_loop(0, lax.shift_right_logical(n_half, 1), full, 0)

        @pl.when((n_half & 1) == 1)
        def _(tile=tile, n_half=n_half):
            tile(pl.ds(pl.multiple_of((n_half - 1) * half, half), half))

        lax.fori_loop(n_half, y_ref.shape[0] // half, blank, 0)


def _combine_kernel(s_ref, y_ref, posm_ref, x1_ref, mod_ref, o_ref, yw_s, sel_s, acc_s, sem,
                    *, n_blk, cap_pad):
    b = pl.program_id(0)
    buf = b % 2
    base, units, first_unit, n_units = _block_units(s_ref, b)

    def unit_rows(first_unit_e, u):
        return pl.ds(pl.multiple_of((first_unit_e + u) * WIN, WIN), WIN)

    def bounds(base_e, u):
        lo = base_e + u * WIN
        return lo, jnp.minimum(lo, cap_pad - WIN)

    def copies(slot, blk, fn):
        blk_base, blk_units, blk_first, _ = _block_units(s_ref, blk)

        def one(e, u):
            src = pl.ds(pl.multiple_of(bounds(blk_base[e], u)[1], ALIGN), WIN)
            fn(pltpu.make_async_copy(y_ref.at[e, src, :], yw_s.at[slot, unit_rows(blk_first[e], u), :],
                                     sem.at[slot]))

        _for_units(blk_units, one)

    @pl.when(b == 0)
    def _():
        yw_s[...] = jnp.zeros_like(yw_s)
        copies(0, 0, lambda c: c.start())

    @pl.when(b + 1 < n_blk)
    def _():
        copies(1 - buf, b + 1, lambda c: c.start())

    def pick(e, u):
        lo, first = bounds(base[e], u)
        sel_s[unit_rows(first_unit[e], u), :] = _window_onehot(posm_ref, e, first, lo)

    _for_units(units, pick)
    n_chunks = jnp.maximum(_ceil_div_pow2(n_units * WIN, ROWS), COMMON_ROWS // ROWS)

    def blank(u, _):
        sel_s[unit_rows(u, 0), :] = jnp.zeros((WIN, ROWS), F32)
        return 0

    lax.fori_loop(n_units, n_chunks * (ROWS // WIN), blank, 0)
    copies(buf, b, lambda c: c.wait())

    def weighted(rows):
        return _dot(sel_s[rows, :].T.astype(BF16), yw_s[buf, rows, :])

    def add_chunk(c, _):
        acc_s[...] += weighted(pl.ds(pl.multiple_of(c * ROWS, ROWS), ROWS))
        return 0

    acc_s[...] = weighted(pl.ds(0, COMMON_ROWS))
    lax.fori_loop(COMMON_ROWS // ROWS, n_chunks, add_chunk, 0)
    o_ref[...] = x1_ref[...] + mod_ref[5:6, :] * acc_s[...]


def _padded_capacity(cap, n_blk):
    return -(-(cap + (ALIGN - 1) * n_blk + WIN) // ROWS) * ROWS


def _block_spec(*shape):
    return pl.BlockSpec((None,) + shape, lambda b, s: (b,) + (0,) * len(shape))


def _dispatch(h2, posm, aff, s_tab, cap):
    n_blk = h2.shape[0] // ROWS
    cap_pad = _padded_capacity(cap, n_blk)
    return pl.pallas_call(
        functools.partial(_dispatch_kernel, n_blk=n_blk, cap_pad=cap_pad),
        out_shape=jax.ShapeDtypeStruct((N_EXPERTS, cap_pad, XE_W), BF16),
        grid_spec=pltpu.PrefetchScalarGridSpec(
            num_scalar_prefetch=1, grid=(n_blk,),
            in_specs=[pl.BlockSpec((ROWS, D_MODEL), lambda b, s: (b, 0)), _block_spec(N_EXPERTS, ROWS),
                      _block_spec(N_EXPERTS, ROWS)],
            out_specs=pl.BlockSpec(memory_space=pl.ANY),
            scratch_shapes=[pltpu.VMEM((MAX_UNITS * WIN, ROWS), BF16), pltpu.VMEM((2, MAX_UNITS * WIN, XE_W), BF16),
                            pltpu.SemaphoreType.DMA((2,))]),
        compiler_params=_params(1), name="dispatch",
    )(s_tab, h2, posm, aff)


def _ffn(xes, s_tabs, wg, wu, wd):
    n_sets = len(xes)
    per_expert = lambda rows, cols: pl.BlockSpec((None, rows, cols), lambda e, *s: (e, 0, 0))
    return pl.pallas_call(
        functools.partial(_ffn_kernel, n_blks=tuple(s.shape[1] - 1 for s in s_tabs)),
        out_shape=[jax.ShapeDtypeStruct(xe.shape[:2] + (D_MODEL,), BF16) for xe in xes],
        grid_spec=pltpu.PrefetchScalarGridSpec(
            num_scalar_prefetch=n_sets, grid=(N_EXPERTS,),
            in_specs=[per_expert(xe.shape[1], XE_W) for xe in xes]
            + [per_expert(D_MODEL, EXPERT_FF), per_expert(D_MODEL, EXPERT_FF), per_expert(EXPERT_FF, D_MODEL)],
            out_specs=[per_expert(xe.shape[1], D_MODEL) for xe in xes],
            scratch_shapes=[pltpu.VMEM((D_MODEL, EXPERT_FF), BF16), pltpu.VMEM((D_MODEL, EXPERT_FF), BF16),
                            pltpu.VMEM((EXPERT_FF, D_MODEL), BF16)]),
        compiler_params=_params(1), name="ffn",
    )(*s_tabs, *xes, wg, wu, wd)


def _combine(y, posm, s_tab, x1, mod, n_tok):
    n_all = x1.shape[0]
    n_blk = n_all // ROWS
    tiles_per_seq = n_tok // ROWS
    mod_map = (lambda b, s: (0, 0, 0)) if mod.shape[0] == 1 else (lambda b, s: (b // tiles_per_seq, 0, 0))
    rows = pl.BlockSpec((ROWS, D_MODEL), lambda b, s: (b, 0))
    return pl.pallas_call(
        functools.partial(_combine_kernel, n_blk=n_blk, cap_pad=y.shape[1]),
        out_shape=jax.ShapeDtypeStruct((n_all, D_MODEL), F32),
        grid_spec=pltpu.PrefetchScalarGridSpec(
            num_scalar_prefetch=1, grid=(n_blk,),
            in_specs=[pl.BlockSpec(memory_space=pl.ANY), _block_spec(N_EXPERTS, ROWS), rows,
                      pl.BlockSpec((None, 6, D_MODEL), mod_map)],
            out_specs=rows,
            scratch_shapes=[pltpu.VMEM((2, MAX_UNITS * WIN, D_MODEL), BF16), pltpu.VMEM((MAX_UNITS * WIN, ROWS), F32),
                            pltpu.VMEM((ROWS, D_MODEL), F32), pltpu.SemaphoreType.DMA((2,))]),
        compiler_params=_params(1), name="combine",
    )(s_tab, y, posm, x1, mod)


def _rope_tables(n_tok):
    rows = n_tok // GRID_W
    row = np.repeat(np.arange(rows, dtype=np.float64), GRID_W)
    col = np.tile(np.arange(GRID_W, dtype=np.float64), rows)
    axis_dim = QK_ROPE // 2
    inv_freq = ROPE_THETA ** (-np.arange(0, axis_dim, 2, dtype=np.float64) / axis_dim)
    ang = np.concatenate([row[:, None] * inv_freq, col[:, None] * inv_freq], axis=-1)
    cos = np.ones((n_tok, LANES))
    sin = np.zeros((n_tok, LANES))
    cos[:, ROPE_LANE0:ROPE_LANE0 + QK_ROPE] = np.repeat(np.cos(ang), 2, axis=1)
    sgn = np.tile(np.array([-1.0, 1.0]), QK_ROPE // 2)
    sin[:, ROPE_LANE0:ROPE_LANE0 + QK_ROPE] = np.repeat(np.sin(ang), 2, axis=1) * sgn
    return jnp.asarray(cos, F32), jnp.asarray(sin, F32)


def _dft_tables(n_tok):
    def cs(n):
        k = np.arange(n)
        ang = 2.0 * np.pi * ((k[:, None] * k[None, :]) % n) / n
        return np.cos(ang), np.sin(ang)
    cc, sc = cs(FNET_CH)
    cn, sn = cs(n_tok)
    return tuple(jnp.asarray(t, F32) for t in (cc, sc, np.concatenate([cn, -sn], axis=1)))


def _pad_heads(w, width):
    lead = w.shape[:-1]
    w = w.reshape(lead + (N_HEADS, width))
    w = jnp.pad(w, [(0, 0)] * len(lead) + [(0, 0), (0, HEAD_PAD - width)])
    return w.reshape(lead + (N_HEADS * HEAD_PAD,))


def _pair_swap_lanes(n_groups):
    perm = np.arange(n_groups * LANES).reshape(n_groups, LANES)
    rot = perm[:, ROPE_LANE0:ROPE_LANE0 + QK_ROPE].reshape(n_groups, QK_ROPE // 2, 2)[:, :, ::-1]
    perm[:, ROPE_LANE0:ROPE_LANE0 + QK_ROPE] = rot.reshape(n_groups, QK_ROPE)
    return perm.reshape(-1)


def _rotary_only(n_groups):
    lane = np.arange(n_groups * LANES) % LANES
    return jnp.asarray((lane >= ROPE_LANE0) & (lane < ROPE_LANE0 + QK_ROPE), F32)


def _head_tables(gain, rope_tabs, n_tok):
    if rope_tabs is None:
        return (jnp.broadcast_to(gain, (n_tok, LANES)),)
    cos, sin = rope_tabs
    return gain * cos, gain[:, _pair_swap_lanes(1)] * sin


def _layer0_weights(norm1, norm2, w_in, q_a_norm, w_q_up, q_norm, kv_a_norm, w_kv_up, k_norm, conv_w, w_o):
    c0 = Q_LORA + KV_LORA
    rope_cols = jnp.pad(w_in[:, c0:c0 + QK_ROPE], ((0, 0), (ROPE_LANE0, LANES - ROPE_LANE0 - QK_ROPE)))
    w_in_pad = jnp.concatenate([w_in[:, :c0], rope_cols, rope_cols[:, _pair_swap_lanes(1)],
                                w_in[:, c0 + QK_ROPE:]], axis=1)
    kv = w_kv_up.reshape(KV_LORA, N_HEADS, QK_NOPE + V_HEAD)
    w_q = _pad_heads(w_q_up, QK_DIM)
    head_gain = lambda g: jnp.pad(g, (0, HEAD_PAD - QK_DIM)).reshape(1, -1)
    return dict(
        norm1=norm1.reshape(1, -1), norm2=norm2.reshape(1, -1), w_in=w_in_pad.astype(BF16),
        q_a_norm=q_a_norm.reshape(1, -1), w_q=w_q.astype(BF16),
        w_q_swap=(w_q[:, _pair_swap_lanes(N_HEADS)] * _rotary_only(N_HEADS)).astype(BF16),
        q_gain=head_gain(q_norm) * (QK_DIM ** -0.5), kv_a_norm=kv_a_norm.reshape(1, -1),
        w_k=_pad_heads(kv[:, :, :QK_NOPE].reshape(KV_LORA, -1), QK_NOPE).astype(BF16),
        w_v=_pad_heads(kv[:, :, QK_NOPE:].reshape(KV_LORA, -1), V_HEAD).astype(BF16), k_gain=head_gain(k_norm),
        conv_w=conv_w, w_o=w_o.astype(BF16))


def _router_weights(w_router):
    hi, lo = _split_hi_lo(jnp.pad(w_router, ((0, 0), (0, LANES - N_EXPERTS))))
    return dict(wr_hi=hi, wr_lo=lo)


def _moe(sets, tri, w):
    routed = []
    for x1, h2, aff, mod, n_tok in sets:
        cap = CAPACITY_FACTOR * x1.shape[0] // N_EXPERTS
        posm, s_tab = _route(aff, tri, cap)
        routed.append((posm, s_tab, _dispatch(h2, posm, aff, s_tab, cap)))
    ys = _ffn([r[2] for r in routed], [r[1] for r in routed], w["wg"], w["wu"], w["wd"])
    return [_combine(y, posm, s_tab, x1, mod, n_tok)
            for y, (posm, s_tab, _), (x1, _, _, mod, n_tok) in zip(ys, routed, sets)]


def _mixer0(x, mod, l0, ctx, rope_tabs):
    n_b, n_tok, _ = x.shape
    q, k, v, conv, ckv, kr = _front0(x, mod, l0, rope_tabs)
    if ctx is not None:
        kc, vc = _ctx_kv(ctx[0], ctx[1], l0)
        attn = _attention(q, k, v, kc, vc)
    else:
        attn = _attention(q, k, v)
    flat = lambda a: a.reshape(n_b * n_tok, a.shape[-1])
    return _post0(flat(x), flat(attn), flat(conv), mod, l0, n_tok), (ckv, kr[:, :, ROPE_LANE0:ROPE_LANE0 + QK_ROPE])


def kernel(x_prompt, x_sample, c, cache_c_kv_l0, cache_k_rope_l0, c_ctx, norm1_l0, norm2_l0, w_mod_l0, b_mod_l0, w_in_l0, q_a_norm_l0, w_q_up_l0, q_norm_l0, kv_a_norm_l0, w_kv_up_l0, k_norm_l0, conv_w_l0, w_o_l0, w_router_l0, w_gate_l0, w_up_l0, w_down_l0, norm1_l1, norm2_l1, w_mod_l1, b_mod_l1, w_f_l1, w_router_l1, w_gate_l1, w_up_l1, w_down_l1):
    n_dec = c.shape[0]
    cond = jnp.concatenate([c_ctx[None, :], c, jnp.zeros((16 - 1 - n_dec, D_MODEL), F32)], axis=0)
    m0 = _modulation(cond, w_mod_l0, b_mod_l0)
    m1 = _modulation(cond, w_mod_l1, b_mod_l1)
    mods_prompt = (m0[0:1], m1[0:1])
    mods_sample = (m0[1:1 + n_dec], m1[1:1 + n_dec])

    l0 = _layer0_weights(norm1_l0, norm2_l0, w_in_l0, q_a_norm_l0, w_q_up_l0, q_norm_l0, kv_a_norm_l0,
                         w_kv_up_l0, k_norm_l0, conv_w_l0, w_o_l0)
    l0.update(_router_weights(w_router_l0))
    l0.update(wg=w_gate_l0, wu=w_up_l0, wd=w_down_l0)
    l1 = dict(norm1=norm1_l1.reshape(1, -1), norm2=norm2_l1.reshape(1, -1), w_f=w_f_l1.astype(BF16))
    l1.update(_router_weights(w_router_l1))
    l1.update(wg=w_gate_l1, wu=w_up_l1, wd=w_down_l1)

    tri = jnp.asarray(np.triu(np.ones((ROWS, ROWS)), 1), BF16)
    k_rope_pad = jnp.pad(cache_k_rope_l0, ((0, 0), (0, 0), (ROPE_LANE0, LANES - ROPE_LANE0 - QK_ROPE)))
    xs = (x_prompt, x_sample)
    mods = (mods_prompt, mods_sample)
    n_toks = tuple(x.shape[1] for x in xs)

    (front_p, (new_c_kv, new_k_rope)) = _mixer0(x_prompt, mods_prompt[0], l0, None, None)
    (front_s, _) = _mixer0(x_sample, mods_sample[0], l0, (cache_c_kv_l0, k_rope_pad), _rope_tables(n_toks[1]))
    ys = _moe([tuple(front) + (mod[0], n_tok) for front, mod, n_tok in zip((front_p, front_s), mods, n_toks)],
              tri, l0)

    sets = []
    for y, x, mod, n_tok in zip(ys, xs, mods, n_toks):
        x1, h2, aff = _fnet(y.reshape(x.shape), mod[1], l1, _dft_tables(n_tok))
        sets.append((x1.reshape(-1, D_MODEL), h2.reshape(-1, D_MODEL), aff, mod[1], n_tok))
    y_prompt, y_sample = (y.reshape(x.shape) for y, x in zip(_moe(sets, tri, l1), xs))
    return (y_prompt, y_sample, new_c_kv, new_k_rope)
```

```python
import functools

import jax
import jax.numpy as jnp
import numpy as np
from jax import lax
from jax.experimental import pallas as pl
from jax.experimental.pallas import tpu as pltpu

D_MODEL = 1024
GRID_W = 64
N_HEADS = 8
QK_NOPE = 64
QK_ROPE = 32
QK_DIM = QK_NOPE + QK_ROPE
V_HEAD = 64
Q_LORA = 384
KV_LORA = 256
CONV_CH = 512
FNET_GROUPS = 4
FNET_CH = D_MODEL // FNET_GROUPS
N_EXPERTS = 16
EXPERT_FF = 512
CAPACITY_FACTOR = 2
ROPE_THETA = 10000.0
EPS = 1e-6

LANES = 128
HEAD_PAD = LANES
ROWS = 256
WIDE_ROWS = 512
ALIGN = 8
WIN = 64
MAX_UNITS = N_EXPERTS * (ROWS // WIN)
COMMON_ROWS = N_EXPERTS * WIN
XE_W = D_MODEL + LANES
IN0_PAD = Q_LORA + KV_LORA + 2 * LANES + 3 * CONV_CH
ROPE_LANE0 = QK_NOPE
VMEM_LIMIT = 56 * 1024 * 1024

F32 = jnp.float32
BF16 = jnp.bfloat16


def _dot(a, b):
    return jnp.dot(a, b, preferred_element_type=F32)


def _dot_nt(a, b):
    return lax.dot_general(a, b, (((1,), (1,)), ((), ())), preferred_element_type=F32)


def _split_hi_lo(x):
    hi = x.astype(BF16)
    lo = (x - hi.astype(F32)).astype(BF16)
    return hi, lo


def _params(n_axes):
    return pltpu.CompilerParams(dimension_semantics=("arbitrary",) * n_axes,
                                vmem_limit_bytes=VMEM_LIMIT)


def _rms(x, gain):
    return x * lax.rsqrt(jnp.mean(x * x, axis=-1, keepdims=True) + EPS) * gain


def _mod_kernel(cond_ref, w_ref, b_ref, o_ref):
    c = cond_ref[...]
    s = c * (1.0 / (1.0 + jnp.exp(-c)))
    s_hi, s_lo = _split_hi_lo(s)
    w = w_ref[...].astype(BF16)
    o_ref[...] = _dot(s_hi, w) + _dot(s_lo, w) + b_ref[...]


def _modulation(cond, w_mod, b_mod):
    n_rows = cond.shape[0]
    tn = 1536
    out = pl.pallas_call(
        _mod_kernel,
        out_shape=jax.ShapeDtypeStruct((n_rows, 6 * D_MODEL), F32),
        grid=(6 * D_MODEL // tn,),
        in_specs=[pl.BlockSpec((n_rows, D_MODEL), lambda i: (0, 0)),
                  pl.BlockSpec((D_MODEL, tn), lambda i: (0, i)),
                  pl.BlockSpec((1, tn), lambda i: (0, i))],
        out_specs=pl.BlockSpec((n_rows, tn), lambda i: (0, i)),
        compiler_params=_params(1),
        name="modulation",
    )(cond, w_mod, b_mod.reshape(1, -1))
    return out.reshape(n_rows, 6, D_MODEL)


def _head_norm_rope(xh, gain_cos, swapped_sin):
    ss = jnp.sum(xh * xh, axis=-1, keepdims=True) * (1.0 / QK_DIM)
    y = xh * gain_cos
    if swapped_sin is not None:
        y = y + swapped_sin
    return y * lax.rsqrt(ss + EPS)


def _expand_kv(ckv_bf, r, swapped_sin, wk_ref, wv_ref, gain_cos, k_ref, v_ref, rows):
    kf = _dot(ckv_bf, wk_ref[...])
    lane = lax.broadcasted_iota(jnp.int32, (1, N_HEADS * HEAD_PAD), 1)
    ones = jnp.where((lane & (HEAD_PAD - 1)) == V_HEAD, 1.0, 0.0)
    v_ref[rows, :] = (_dot(ckv_bf, wv_ref[...]) + ones).astype(BF16)
    for h in range(N_HEADS):
        sl = slice(h * HEAD_PAD, (h + 1) * HEAD_PAD)
        k_ref[rows, sl] = _head_norm_rope(kf[:, sl] + r, gain_cos, swapped_sin).astype(BF16)


def _front0_kernel(*refs, n_tok, rope):
    (x_ref, mod_ref, n1_ref, win_ref, qan_ref, wq_ref, kvan_ref, wk_ref, wv_ref, cw_ref,
     qa_ref, ka_ref) = refs[:12]
    pos = 12
    if rope:
        wqs_ref, qb_ref, kb_ref = refs[pos:pos + 3]
        pos += 3
    q_ref, k_ref, v_ref, conv_ref, ckv_ref, kr_ref, cu_s, gb_s = refs[pos:]
    sh1, sc1 = mod_ref[0:1, :], mod_ref[1:2, :]
    c_rope = Q_LORA + KV_LORA
    wide = min(n_tok, WIDE_ROWS)
    for c in range(n_tok // ROWS):
        rows = pl.ds(c * ROWS, ROWS)
        if (c * ROWS) % wide == 0:
            wide_rows = pl.ds(c * ROWS, wide)
            h = _rms(x_ref[wide_rows, :], n1_ref[...]) * (1.0 + sc1) + sh1
            proj_wide = _dot(h.astype(BF16), win_ref[...])
        off = (c * ROWS) % wide
        proj = proj_wide[off:off + ROWS, :]
        cq = _rms(proj[:, :Q_LORA], qan_ref[...]).astype(BF16)
        qf = _dot(cq, wq_ref[...])
        qfs = _dot(cq, wqs_ref[...]) if rope else None
        for hd in range(N_HEADS):
            sl = slice(hd * HEAD_PAD, (hd + 1) * HEAD_PAD)
            q_sin = qfs[:, sl] * qb_ref[rows, :] if rope else None
            q_ref[rows, sl] = _head_norm_rope(qf[:, sl], qa_ref[rows, :], q_sin).astype(BF16)
        ckv = _rms(proj[:, Q_LORA:c_rope], kvan_ref[...])
        ckv_ref[rows, :] = ckv
        r = proj[:, c_rope:c_rope + LANES]
        kr_ref[rows, :] = r
        k_sin = proj[:, c_rope + LANES:c_rope + 2 * LANES] * kb_ref[rows, :] if rope else None
        _expand_kv(ckv.astype(BF16), r, k_sin, wk_ref, wv_ref, ka_ref[rows, :], k_ref, v_ref, rows)
        c0 = c_rope + 2 * LANES
        gb_s[rows, :] = proj[:, c0:c0 + CONV_CH]
        cu_s[rows, :] = proj[:, c0 + CONV_CH:c0 + 2 * CONV_CH] * proj[:, c0 + 2 * CONV_CH:c0 + 3 * CONV_CH]
    cu = cu_s[...]
    row = lax.broadcasted_iota(jnp.int32, cu.shape, 0)
    prev = jnp.where(row == 0, 0.0, pltpu.roll(cu, 1, 0))
    nxt = jnp.where(row == n_tok - 1, 0.0, pltpu.roll(cu, n_tok - 1, 0))
    conv = gb_s[...] * (cw_ref[0:1, :] * prev + cw_ref[1:2, :] * cu + cw_ref[2:3, :] * nxt)
    conv_ref[...] = conv.astype(BF16)


def _const_spec(shape):
    return pl.BlockSpec(shape, lambda b: (0,) * len(shape))


def _front0(x, mod, w, rope_tabs):
    n_b, n_tok, _ = x.shape
    shared_mod = mod.shape[0] == 1
    seq = lambda width: pl.BlockSpec((None, n_tok, width), lambda b: (b, 0, 0))
    in_specs = [seq(D_MODEL),
                pl.BlockSpec((None, 6, D_MODEL), (lambda b: (0, 0, 0)) if shared_mod else (lambda b: (b, 0, 0))),
                _const_spec((1, D_MODEL)), _const_spec((D_MODEL, IN0_PAD)), _const_spec((1, Q_LORA)),
                _const_spec((Q_LORA, N_HEADS * HEAD_PAD)),
                _const_spec((1, KV_LORA)), _const_spec((KV_LORA, N_HEADS * HEAD_PAD)),
                _const_spec((KV_LORA, N_HEADS * HEAD_PAD)), _const_spec((3, CONV_CH)),
                _const_spec((n_tok, LANES)), _const_spec((n_tok, LANES))]
    q_tabs, k_tabs = _head_tables(w["q_gain"], rope_tabs, n_tok), _head_tables(w["k_gain"], rope_tabs, n_tok)
    args = [x, mod, w["norm1"], w["w_in"], w["q_a_norm"], w["w_q"], w["kv_a_norm"],
            w["w_k"], w["w_v"], w["conv_w"], q_tabs[0], k_tabs[0]]
    if rope_tabs is not None:
        in_specs += [_const_spec((Q_LORA, N_HEADS * HEAD_PAD)), _const_spec((n_tok, LANES)), _const_spec((n_tok, LANES))]
        args += [w["w_q_swap"], q_tabs[1], k_tabs[1]]
    out_shape = [jax.ShapeDtypeStruct((n_b, n_tok, N_HEADS * HEAD_PAD), BF16),
                 jax.ShapeDtypeStruct((n_b, n_tok, N_HEADS * HEAD_PAD), BF16),
                 jax.ShapeDtypeStruct((n_b, n_tok, N_HEADS * HEAD_PAD), BF16),
                 jax.ShapeDtypeStruct((n_b, n_tok, CONV_CH), BF16),
                 jax.ShapeDtypeStruct((n_b, n_tok, KV_LORA), F32),
                 jax.ShapeDtypeStruct((n_b, n_tok, LANES), F32)]
    out_specs = [seq(N_HEADS * HEAD_PAD), seq(N_HEADS * HEAD_PAD), seq(N_HEADS * HEAD_PAD), seq(CONV_CH),
                 seq(KV_LORA), seq(LANES)]
    return pl.pallas_call(
        functools.partial(_front0_kernel, n_tok=n_tok, rope=rope_tabs is not None),
        out_shape=out_shape, grid=(n_b,), in_specs=in_specs, out_specs=out_specs,
        scratch_shapes=[pltpu.VMEM((n_tok, CONV_CH), F32), pltpu.VMEM((n_tok, CONV_CH), F32)],
        compiler_params=_params(1), name="front0",
    )(*args)


def _ctx_kv_kernel(ckv_ref, r_ref, wk_ref, wv_ref, ka_ref, k_ref, v_ref):
    rows = pl.ds(0, ckv_ref.shape[0])
    _expand_kv(ckv_ref[...].astype(BF16), r_ref[...], None, wk_ref, wv_ref, ka_ref[...], k_ref, v_ref, rows)


def _ctx_kv(cache_c_kv, cache_k_rope_pad, w):
    n_b, n_ctx, _ = cache_c_kv.shape
    seq = lambda width: pl.BlockSpec((None, n_ctx, width), lambda b: (b, 0, 0))
    return pl.pallas_call(
        _ctx_kv_kernel,
        out_shape=[jax.ShapeDtypeStruct((n_b, n_ctx, N_HEADS * HEAD_PAD), BF16),
                   jax.ShapeDtypeStruct((n_b, n_ctx, N_HEADS * HEAD_PAD), BF16)],
        grid=(n_b,),
        in_specs=[seq(KV_LORA), seq(LANES), _const_spec((KV_LORA, N_HEADS * HEAD_PAD)),
                  _const_spec((KV_LORA, N_HEADS * HEAD_PAD)), _const_spec((n_ctx, LANES))],
        out_specs=[seq(N_HEADS * HEAD_PAD), seq(N_HEADS * HEAD_PAD)],
        compiler_params=_params(1), name="ctx_kv",
    )(cache_c_kv, cache_k_rope_pad, w["w_k"], w["w_v"], _head_tables(w["k_gain"], None, n_ctx)[0])


def _attn_kernel(*refs, with_ctx):
    if with_ctx:
        q_ref, k_ref, v_ref, kc_ref, vc_ref, o_ref = refs
    else:
        q_ref, k_ref, v_ref, o_ref = refs
    lane = lax.broadcasted_iota(jnp.int32, (q_ref.shape[1], LANES), 1)
    for seq in range(q_ref.shape[0]):
        for pair in range(N_HEADS // 2):
            outs = []
            for hd in (2 * pair, 2 * pair + 1):
                sl = slice(hd * HEAD_PAD, (hd + 1) * HEAD_PAD)
                qh = q_ref[seq, :, sl]
                s = _dot_nt(qh, k_ref[seq, :, sl])
                m = jnp.max(s, axis=-1, keepdims=True)
                if with_ctx:
                    sc = _dot_nt(qh, kc_ref[seq, :, sl])
                    m = jnp.maximum(m, jnp.max(sc, axis=-1, keepdims=True))
                o = _dot(jnp.exp(s - m).astype(BF16), v_ref[seq, :, sl])
                if with_ctx:
                    o = o + _dot(jnp.exp(sc - m).astype(BF16), vc_ref[seq, :, sl])
                outs.append(o / o[:, V_HEAD:V_HEAD + 1])
            both = jnp.where(lane < V_HEAD, outs[0], pltpu.roll(outs[1], V_HEAD, 1))
            o_ref[seq, :, pair * LANES:(pair + 1) * LANES] = both.astype(BF16)


def _attention(q, k, v, kc=None, vc=None):
    n_b, n_tok, _ = q.shape
    with_ctx = kc is not None
    tq = min(n_tok, WIDE_ROWS)
    n_seq = WIDE_ROWS // tq
    qspec = lambda width: pl.BlockSpec((n_seq, tq, width), lambda b, i: (b, i, 0))
    kvspec = lambda n: pl.BlockSpec((n_seq, n, N_HEADS * HEAD_PAD), lambda b, i: (b, 0, 0))
    in_specs = [qspec(N_HEADS * HEAD_PAD), kvspec(n_tok), kvspec(n_tok)]
    args = [q, k, v]
    if with_ctx:
        in_specs += [kvspec(kc.shape[1])] * 2
        args += [kc, vc]
    return pl.pallas_call(
        functools.partial(_attn_kernel, with_ctx=with_ctx),
        out_shape=jax.ShapeDtypeStruct((n_b, n_tok, N_HEADS * V_HEAD), BF16),
        grid=(n_b // n_seq, n_tok // tq), in_specs=in_specs, out_specs=qspec(N_HEADS * V_HEAD),
        compiler_params=_params(2), name="attention",
    )(*args)


def _moe_front(x1, mod_ref, n2_ref, wrh_ref, wrl_ref):
    sh2, sc2 = mod_ref[3:4, :], mod_ref[4:5, :]
    h2 = _rms(x1, n2_ref[...]) * (1.0 + sc2) + sh2
    h_hi, h_lo = _split_hi_lo(h2)
    logit = _dot(h_hi, wrh_ref[...]) + _dot(h_lo, wrh_ref[...]) + _dot(h_hi, wrl_ref[...])
    logit = logit.T[0:N_EXPERTS, :]
    e = jnp.exp(logit - jnp.max(logit, axis=0, keepdims=True))
    return h_hi, e / jnp.sum(e, axis=0, keepdims=True)


def _post0_kernel(x_ref, attn_ref, conv_ref, wo_ref, mod_ref, n2_ref, wrh_ref, wrl_ref,
                  x1_ref, h2_ref, aff_ref):
    n_attn = N_HEADS * V_HEAD
    mix = _dot(attn_ref[...], wo_ref[0:n_attn, :]) + _dot(conv_ref[...], wo_ref[n_attn:, :])
    for t in range(x_ref.shape[0] // ROWS):
        rows = pl.ds(t * ROWS, ROWS)
        x1 = x_ref[rows, :] + mod_ref[2:3, :] * mix[t * ROWS:(t + 1) * ROWS, :]
        x1_ref[rows, :] = x1
        h2, aff = _moe_front(x1, mod_ref, n2_ref, wrh_ref, wrl_ref)
        h2_ref[rows, :] = h2
        aff_ref[t] = aff


def _post0(x, attn, conv, mod, w, n_tok):
    n_all = x.shape[0]
    rows = min(n_tok, WIDE_ROWS)
    tiles_per_seq = n_tok // rows
    shared_mod = mod.shape[0] == 1
    tile = lambda width: pl.BlockSpec((rows, width), lambda i: (i, 0))
    const = lambda shape: pl.BlockSpec(shape, lambda i: (0,) * len(shape))
    mod_map = (lambda i: (0, 0, 0)) if shared_mod else (lambda i: (i // tiles_per_seq, 0, 0))
    return pl.pallas_call(
        _post0_kernel,
        out_shape=[jax.ShapeDtypeStruct((n_all, D_MODEL), F32),
                   jax.ShapeDtypeStruct((n_all, D_MODEL), BF16),
                   jax.ShapeDtypeStruct((n_all // ROWS, N_EXPERTS, ROWS), F32)],
        grid=(n_all // rows,),
        in_specs=[tile(D_MODEL), tile(N_HEADS * V_HEAD), tile(CONV_CH), const((D_MODEL, D_MODEL)),
                  pl.BlockSpec((None, 6, D_MODEL), mod_map), const((1, D_MODEL)),
                  const((D_MODEL, LANES)), const((D_MODEL, LANES))],
        out_specs=[tile(D_MODEL), tile(D_MODEL),
                   pl.BlockSpec((rows // ROWS, N_EXPERTS, ROWS), lambda i: (i, 0, 0))],
        compiler_params=_params(1), name="post0",
    )(x, attn, conv, w["w_o"], mod, w["norm2"], w["wr_hi"], w["wr_lo"])


def _fnet_kernel(x_ref, mod_ref, n1_ref, cc_ref, sc_ref, dft_ref, wf_ref, n2_ref, wrh_ref, wrl_ref,
                 x1_ref, h2_ref, aff_ref, y_s, *, n_tok):
    sh1, sc1, g1 = mod_ref[0:1, :], mod_ref[1:2, :], mod_ref[2:3, :]
    n_chunks = n_tok // ROWS
    cc, sc = cc_ref[...].astype(BF16), sc_ref[...].astype(BF16)
    for c in range(n_chunks):
        rows = pl.ds(c * ROWS, ROWS)
        h = (_rms(x_ref[rows, :], n1_ref[...]) * (1.0 + sc1) + sh1).astype(BF16)
        for g in range(FNET_GROUPS):
            sl = slice(g * FNET_CH, (g + 1) * FNET_CH)
            y_s[pl.ds(c * ROWS, ROWS), sl] = _dot(h[:, sl], cc).astype(BF16)
            y_s[pl.ds(n_tok + c * ROWS, ROWS), sl] = _dot(h[:, sl], sc).astype(BF16)
    scale = 1.0 / float(np.sqrt(n_tok * FNET_CH))
    wide = min(n_tok, WIDE_ROWS)
    for c in range(n_chunks):
        rows = pl.ds(c * ROWS, ROWS)
        if (c * ROWS) % wide == 0:
            f = _dot(dft_ref[pl.ds(c * ROWS, wide), :].astype(BF16), y_s[...]) * scale
            mix = _dot(f.astype(BF16), wf_ref[...])
        off = (c * ROWS) % wide
        x1 = x_ref[rows, :] + g1 * mix[off:off + ROWS, :]
        x1_ref[rows, :] = x1
        h2, aff = _moe_front(x1, mod_ref, n2_ref, wrh_ref, wrl_ref)
        h2_ref[rows, :] = h2
        aff_ref[c] = aff


def _fnet(x, mod, w, dft):
    n_b, n_tok, _ = x.shape
    shared_mod = mod.shape[0] == 1
    seq = lambda width: pl.BlockSpec((None, n_tok, width), lambda b: (b, 0, 0))
    cc, sc, dft_n = dft
    tiles = n_tok // ROWS
    return pl.pallas_call(
        functools.partial(_fnet_kernel, n_tok=n_tok),
        out_shape=[jax.ShapeDtypeStruct((n_b, n_tok, D_MODEL), F32),
                   jax.ShapeDtypeStruct((n_b, n_tok, D_MODEL), BF16),
                   jax.ShapeDtypeStruct((n_b * tiles, N_EXPERTS, ROWS), F32)],
        grid=(n_b,),
        in_specs=[seq(D_MODEL),
                  pl.BlockSpec((None, 6, D_MODEL), (lambda b: (0, 0, 0)) if shared_mod else (lambda b: (b, 0, 0))),
                  _const_spec((1, D_MODEL)), _const_spec((FNET_CH, FNET_CH)), _const_spec((FNET_CH, FNET_CH)),
                  pl.BlockSpec((n_tok, 2 * n_tok), lambda b: (0, 0), pipeline_mode=pl.Buffered(1)),
                  _const_spec((D_MODEL, D_MODEL)), _const_spec((1, D_MODEL)),
                  _const_spec((D_MODEL, LANES)), _const_spec((D_MODEL, LANES))],
        out_specs=[seq(D_MODEL), seq(D_MODEL), pl.BlockSpec((tiles, N_EXPERTS, ROWS), lambda b: (b, 0, 0))],
        scratch_shapes=[pltpu.VMEM((2 * n_tok, D_MODEL), BF16)],
        compiler_params=_params(1), name="fnet",
    )(x, mod, w["norm1"], cc, sc, dft_n, w["w_f"], w["norm2"], w["wr_hi"], w["wr_lo"])


def _route_kernel(aff_ref, tri_ref, posm_ref, s_ref, *, n_blk, cap):
    def count(pred):
        acc = jnp.zeros((N_EXPERTS, ROWS), F32)
        for b in range(n_blk):
            acc = acc + jnp.where(pred(aff_ref[b]), 1.0, 0.0)
        return jnp.sum(acc, axis=1, keepdims=True)

    def as_f32(bits):
        return pltpu.bitcast(bits, F32)

    def search(_, carry):
        lo, hi = carry
        mid = lo + ((hi - lo) >> 1)
        mid_f = as_f32(mid)
        ok = count(lambda a: a >= mid_f) >= cap
        return jnp.where(ok, mid, lo), jnp.where(ok, hi, mid)

    one_bits = 0x3F800000
    lo0 = jnp.zeros((N_EXPERTS, 1), jnp.int32)
    hi0 = jnp.full((N_EXPERTS, 1), one_bits + 1, jnp.int32)
    lo, hi = lax.fori_loop(0, 31, search, (lo0, hi0))
    lo_f, ub = as_f32(lo), as_f32(hi)
    thr = lo_f
    pending = jnp.ones((N_EXPERTS, 1), F32)
    for _ in range(3):
        cur = jnp.full((N_EXPERTS, ROWS), -1.0, F32)
        for b in range(n_blk):
            a = aff_ref[b]
            cur = jnp.maximum(cur, jnp.where((a >= lo_f) & (a < ub), a, -1.0))
        cur = jnp.max(cur, axis=1, keepdims=True)
        take = (count(lambda a: a >= cur) >= cap) & (pending > 0.0)
        thr = jnp.where(take, cur, thr)
        pending = jnp.where(take, 0.0, pending)
        ub = cur
    need = cap - count(lambda a: a > thr)

    lane = lax.broadcasted_iota(jnp.int32, (N_EXPERTS, LANES), 1)
    carry_tie = jnp.zeros((N_EXPERTS, 1), F32)
    carry_pos = jnp.zeros((N_EXPERTS, 1), F32)
    s_acc = jnp.zeros((N_EXPERTS, LANES), jnp.int32)
    for b in range(n_blk):
        v = aff_ref[b]
        eq = jnp.where(v == thr, 1.0, 0.0)
        tie_rank = _dot(eq.astype(BF16), tri_ref[...]) + carry_tie
        sel = jnp.where((v > thr) | ((v == thr) & (tie_rank < need)), 1.0, 0.0)
        pos = _dot(sel.astype(BF16), tri_ref[...]) + carry_pos
        posm_ref[b] = jnp.where(sel > 0.0, pos, -1.0)
        s_acc = jnp.where(lane == b, carry_pos.astype(jnp.int32), s_acc)
        carry_tie = carry_tie + jnp.sum(eq, axis=1, keepdims=True)
        n_sel = jnp.sum(sel, axis=1, keepdims=True)
        carry_pos = carry_pos + jnp.floor((n_sel + (ALIGN - 1)) * (1.0 / ALIGN)) * ALIGN
    s_ref[...] = jnp.where(lane == n_blk, carry_pos.astype(jnp.int32), s_acc)


def _route(aff, tri, cap):
    n_blk = aff.shape[0]
    assert n_blk < LANES
    full = lambda shape: pl.BlockSpec(shape, lambda i: (0,) * len(shape))
    posm, s_tab = pl.pallas_call(
        functools.partial(_route_kernel, n_blk=n_blk, cap=cap),
        out_shape=[jax.ShapeDtypeStruct((n_blk, N_EXPERTS, ROWS), F32),
                   jax.ShapeDtypeStruct((N_EXPERTS, LANES), jnp.int32)],
        grid=(1,),
        in_specs=[full((n_blk, N_EXPERTS, ROWS)), full((ROWS, ROWS))],
        out_specs=[full((n_blk, N_EXPERTS, ROWS)), full((N_EXPERTS, LANES))],
        compiler_params=_params(1), name="route",
    )(aff, tri)
    return posm, s_tab[:, :n_blk + 1]


def _ceil_div_pow2(x, d):
    return lax.shift_right_logical(x + (d - 1), d.bit_length() - 1)


def _block_units(s_ref, b):
    base = [s_ref[e, b] for e in range(N_EXPERTS)]
    units = [_ceil_div_pow2(s_ref[e, b + 1] - base[e], WIN) for e in range(N_EXPERTS)]
    first_unit, total = [], 0
    for e in range(N_EXPERTS):
        first_unit.append(total)
        total = total + units[e]
    return base, units, first_unit, total


def _for_units(units, fn):
    for e in range(N_EXPERTS):
        def body(u, _, e=e):
            fn(e, u)
            return 0
        lax.fori_loop(0, units[e], body, 0)


def _window_onehot(posm_ref, e, first, lo):
    slot = lax.broadcasted_iota(jnp.int32, (WIN, ROWS), 0) + first
    p = posm_ref[e:e + 1, :].astype(jnp.int32)
    return jnp.where((p == slot) & (p >= lo), 1.0, 0.0)


def _gate_lanes(e):
    lane = lax.broadcasted_iota(jnp.int32, (1, LANES), 1)
    return ((lane & (N_EXPERTS - 1)) == e) & (lane < 3 * N_EXPERTS)


def _dispatch_kernel(s_ref, h_ref, posm_ref, aff_ref, xe_ref, sel_s, x_s, sem, *, n_blk, cap_pad):
    b = pl.program_id(0)
    buf = b % 2
    base, units, first_unit, n_units = _block_units(s_ref, b)

    @pl.when(b == 0)
    def _():
        sel_s[...] = jnp.zeros_like(sel_s)

    a = jnp.concatenate([aff_ref[...], jnp.zeros((LANES - N_EXPERTS, ROWS), F32)], axis=0).T
    a_hi = a.astype(BF16).astype(F32)
    a_mid = (a - a_hi).astype(BF16).astype(F32)
    a_lo = a - a_hi - a_mid
    parts = (a_hi + pltpu.roll(a_mid, N_EXPERTS, 1) + pltpu.roll(a_lo, 2 * N_EXPERTS, 1)).astype(BF16)

    def unit_rows(first_unit_e, u):
        return pl.ds(pl.multiple_of((first_unit_e + u) * WIN, WIN), WIN)

    def pick(e, u):
        first = base[e] + u * WIN
        sel_s[unit_rows(first_unit[e], u), :] = _window_onehot(posm_ref, e, first, first).astype(BF16)

    _for_units(units, pick)

    def gather(rows):
        sel = sel_s[rows, :]
        x_s[buf, rows, 0:D_MODEL] = _dot(sel, h_ref[...]).astype(BF16)
        x_s[buf, rows, D_MODEL:XE_W] = _dot(sel, parts).astype(BF16)

    def gather_chunk(c, _):
        gather(pl.ds(pl.multiple_of(c * ROWS, ROWS), ROWS))
        return 0

    gather(pl.ds(0, COMMON_ROWS))
    lax.fori_loop(COMMON_ROWS // ROWS, _ceil_div_pow2(n_units * WIN, ROWS), gather_chunk, 0)

    def unit_copy(slot, blk_base_e, blk_first_e, e, u):
        dst = pl.ds(pl.multiple_of(blk_base_e + u * WIN, ALIGN), WIN)
        return pltpu.make_async_copy(x_s.at[slot, unit_rows(blk_first_e, u), :], xe_ref.at[e, dst, :],
                                     sem.at[slot, e])

    def for_expert_units(n, fn):
        lax.fori_loop(0, n, lambda u, _: (fn(u), 0)[1], 0)

    prev_base, prev_units, prev_first, _ = _block_units(s_ref, jnp.maximum(b - 1, 0))
    for e in range(N_EXPERTS):
        for_expert_units(jnp.where(b > 0, prev_units[e], 0),
                         lambda u, e=e: unit_copy(1 - buf, prev_base[e], prev_first[e], e, u).wait())
        for_expert_units(units[e], lambda u, e=e: unit_copy(buf, base[e], first_unit[e], e, u).start())

    @pl.when(b == n_blk - 1)
    def _():
        for e in range(N_EXPERTS):
            for_expert_units(units[e], lambda u, e=e: unit_copy(buf, base[e], first_unit[e], e, u).wait())
        x_s[buf, 0:WIN, :] = jnp.zeros((WIN, XE_W), BF16)

        def tail(fn):
            for e in range(N_EXPERTS):
                total = s_ref[e, n_blk]
                n_win = (cap_pad - total) // WIN

                def wide(c, _):
                    row = pl.multiple_of(total + c * WIN, ALIGN)
                    fn(pltpu.make_async_copy(x_s.at[buf, pl.ds(0, WIN), :],
                                             xe_ref.at[e, pl.ds(row, WIN), :], sem.at[buf, e]))
                    return 0

                def narrow(c, _):
                    row = pl.multiple_of(total + n_win * WIN + c * ALIGN, ALIGN)
                    fn(pltpu.make_async_copy(x_s.at[buf, pl.ds(0, ALIGN), :],
                                             xe_ref.at[e, pl.ds(row, ALIGN), :], sem.at[buf, e]))
                    return 0

                lax.fori_loop(0, n_win, wide, 0)
                lax.fori_loop(0, (cap_pad - total - n_win * WIN) // ALIGN, narrow, 0)

        tail(lambda c: c.start())
        tail(lambda c: c.wait())


def _ffn_kernel(*refs, n_blks):
    n_sets = len(n_blks)
    s_refs, x_refs = refs[:n_sets], refs[n_sets:2 * n_sets]
    wg_ref, wu_ref, wd_ref = refs[2 * n_sets:2 * n_sets + 3]
    y_refs = refs[2 * n_sets + 3:3 * n_sets + 3]
    wg_s, wu_s, wd_s = refs[3 * n_sets + 3:]
    e = pl.program_id(0)
    wg_s[...] = wg_ref[...].astype(BF16)
    wu_s[...] = wu_ref[...].astype(BF16)
    wd_s[...] = wd_ref[...].astype(BF16)
    mine = _gate_lanes(e)

    half = ROWS // 2
    for s_ref, x_ref, y_ref, n_blk in zip(s_refs, x_refs, y_refs, n_blks):
        n_half = _ceil_div_pow2(s_ref[e, n_blk], half)

        def tile(rows, x_ref=x_ref, y_ref=y_ref):
            x = x_ref[rows, 0:D_MODEL]
            pieces = x_ref[rows, D_MODEL:XE_W].astype(F32)
            gate = jnp.sum(jnp.where(mine, pieces, 0.0), axis=1, keepdims=True)
            a = _dot(x, wg_s[...])
            u = _dot(x, wu_s[...])
            hid = a * (1.0 / (1.0 + jnp.exp(-a))) * u * gate
            y_ref[rows, :] = _dot(hid.astype(BF16), wd_s[...]).astype(BF16)

        def full(j, _, tile=tile):
            tile(pl.ds(pl.multiple_of(j * ROWS, ROWS), ROWS))
            return 0

        def blank(j, _, y_ref=y_ref):
            y_ref[pl.ds(pl.multiple_of(j * half, half), half), :] = jnp.zeros((half, D_MODEL), BF16)
            return 0

        lax.fori_loop(0, lax.shift_right_logical(n_half, 1), full, 0)

        @pl.when((n_half & 1) == 1)
        def _(tile=tile, n_half=n_half):
            tile(pl.ds(pl.multiple_of((n_half - 1) * half, half), half))

        lax.fori_loop(n_half, y_ref.shape[0] // half, blank, 0)


def _combine_kernel(s_ref, y_ref, posm_ref, x1_ref, mod_ref, o_ref, yw_s, sel_s, acc_s, sem,
                    *, n_blk, cap_pad):
    b = pl.program_id(0)
    buf = b % 2
    base, units, first_unit, n_units = _block_units(s_ref, b)

    def unit_rows(first_unit_e, u):
        return pl.ds(pl.multiple_of((first_unit_e + u) * WIN, WIN), WIN)

    def bounds(base_e, u):
        lo = base_e + u * WIN
        return lo, jnp.minimum(lo, cap_pad - WIN)

    def copies(slot, blk, fn):
        blk_base, blk_units, blk_first, _ = _block_units(s_ref, blk)

        def one(e, u):
            src = pl.ds(pl.multiple_of(bounds(blk_base[e], u)[1], ALIGN), WIN)
            fn(pltpu.make_async_copy(y_ref.at[e, src, :], yw_s.at[slot, unit_rows(blk_first[e], u), :],
                                     sem.at[slot]))

        _for_units(blk_units, one)

    @pl.when(b == 0)
    def _():
        yw_s[...] = jnp.zeros_like(yw_s)
        copies(0, 0, lambda c: c.start())

    @pl.when(b + 1 < n_blk)
    def _():
        copies(1 - buf, b + 1, lambda c: c.start())

    def pick(e, u):
        lo, first = bounds(base[e], u)
        sel_s[unit_rows(first_unit[e], u), :] = _window_onehot(posm_ref, e, first, lo)

    _for_units(units, pick)
    n_chunks = jnp.maximum(_ceil_div_pow2(n_units * WIN, ROWS), COMMON_ROWS // ROWS)

    def blank(u, _):
        sel_s[unit_rows(u, 0), :] = jnp.zeros((WIN, ROWS), F32)
        return 0

    lax.fori_loop(n_units, n_chunks * (ROWS // WIN), blank, 0)
    copies(buf, b, lambda c: c.wait())

    def weighted(rows):
        return _dot(sel_s[rows, :].T.astype(BF16), yw_s[buf, rows, :])

    def add_chunk(c, _):
        acc_s[...] += weighted(pl.ds(pl.multiple_of(c * ROWS, ROWS), ROWS))
        return 0

    acc_s[...] = weighted(pl.ds(0, COMMON_ROWS))
    lax.fori_loop(COMMON_ROWS // ROWS, n_chunks, add_chunk, 0)
    o_ref[...] = x1_ref[...] + mod_ref[5:6, :] * acc_s[...]


def _padded_capacity(cap, n_blk):
    return -(-(cap + (ALIGN - 1) * n_blk + WIN) // ROWS) * ROWS


def _block_spec(*shape):
    return pl.BlockSpec((None,) + shape, lambda b, s: (b,) + (0,) * len(shape))


def _dispatch(h2, posm, aff, s_tab, cap):
    n_blk = h2.shape[0] // ROWS
    cap_pad = _padded_capacity(cap, n_blk)
    return pl.pallas_call(
        functools.partial(_dispatch_kernel, n_blk=n_blk, cap_pad=cap_pad),
        out_shape=jax.ShapeDtypeStruct((N_EXPERTS, cap_pad, XE_W), BF16),
        grid_spec=pltpu.PrefetchScalarGridSpec(
            num_scalar_prefetch=1, grid=(n_blk,),
            in_specs=[pl.BlockSpec((ROWS, D_MODEL), lambda b, s: (b, 0)), _block_spec(N_EXPERTS, ROWS),
                      _block_spec(N_EXPERTS, ROWS)],
            out_specs=pl.BlockSpec(memory_space=pl.ANY),
            scratch_shapes=[pltpu.VMEM((MAX_UNITS * WIN, ROWS), BF16), pltpu.VMEM((2, MAX_UNITS * WIN, XE_W), BF16),
                            pltpu.SemaphoreType.DMA((2, N_EXPERTS))]),
        compiler_params=_params(1), name="dispatch",
    )(s_tab, h2, posm, aff)


def _ffn(xes, s_tabs, wg, wu, wd):
    n_sets = len(xes)
    per_expert = lambda rows, cols: pl.BlockSpec((None, rows, cols), lambda e, *s: (e, 0, 0))
    return pl.pallas_call(
        functools.partial(_ffn_kernel, n_blks=tuple(s.shape[1] - 1 for s in s_tabs)),
        out_shape=[jax.ShapeDtypeStruct(xe.shape[:2] + (D_MODEL,), BF16) for xe in xes],
        grid_spec=pltpu.PrefetchScalarGridSpec(
            num_scalar_prefetch=n_sets, grid=(N_EXPERTS,),
            in_specs=[per_expert(xe.shape[1], XE_W) for xe in xes]
            + [per_expert(D_MODEL, EXPERT_FF), per_expert(D_MODEL, EXPERT_FF), per_expert(EXPERT_FF, D_MODEL)],
            out_specs=[per_expert(xe.shape[1], D_MODEL) for xe in xes],
            scratch_shapes=[pltpu.VMEM((D_MODEL, EXPERT_FF), BF16), pltpu.VMEM((D_MODEL, EXPERT_FF), BF16),
                            pltpu.VMEM((EXPERT_FF, D_MODEL), BF16)]),
        compiler_params=_params(1), name="ffn",
    )(*s_tabs, *xes, wg, wu, wd)


def _combine(y, posm, s_tab, x1, mod, n_tok):
    n_all = x1.shape[0]
    n_blk = n_all // ROWS
    tiles_per_seq = n_tok // ROWS
    mod_map = (lambda b, s: (0, 0, 0)) if mod.shape[0] == 1 else (lambda b, s: (b // tiles_per_seq, 0, 0))
    rows = pl.BlockSpec((ROWS, D_MODEL), lambda b, s: (b, 0))
    return pl.pallas_call(
        functools.partial(_combine_kernel, n_blk=n_blk, cap_pad=y.shape[1]),
        out_shape=jax.ShapeDtypeStruct((n_all, D_MODEL), F32),
        grid_spec=pltpu.PrefetchScalarGridSpec(
            num_scalar_prefetch=1, grid=(n_blk,),
            in_specs=[pl.BlockSpec(memory_space=pl.ANY), _block_spec(N_EXPERTS, ROWS), rows,
                      pl.BlockSpec((None, 6, D_MODEL), mod_map)],
            out_specs=rows,
            scratch_shapes=[pltpu.VMEM((2, MAX_UNITS * WIN, D_MODEL), BF16), pltpu.VMEM((MAX_UNITS * WIN, ROWS), F32),
                            pltpu.VMEM((ROWS, D_MODEL), F32), pltpu.SemaphoreType.DMA((2,))]),
        compiler_params=_params(1), name="combine",
    )(s_tab, y, posm, x1, mod)


def _rope_tables(n_tok):
    rows = n_tok // GRID_W
    row = np.repeat(np.arange(rows, dtype=np.float64), GRID_W)
    col = np.tile(np.arange(GRID_W, dtype=np.float64), rows)
    axis_dim = QK_ROPE // 2
    inv_freq = ROPE_THETA ** (-np.arange(0, axis_dim, 2, dtype=np.float64) / axis_dim)
    ang = np.concatenate([row[:, None] * inv_freq, col[:, None] * inv_freq], axis=-1)
    cos = np.ones((n_tok, LANES))
    sin = np.zeros((n_tok, LANES))
    cos[:, ROPE_LANE0:ROPE_LANE0 + QK_ROPE] = np.repeat(np.cos(ang), 2, axis=1)
    sgn = np.tile(np.array([-1.0, 1.0]), QK_ROPE // 2)
    sin[:, ROPE_LANE0:ROPE_LANE0 + QK_ROPE] = np.repeat(np.sin(ang), 2, axis=1) * sgn
    return jnp.asarray(cos, F32), jnp.asarray(sin, F32)


def _dft_tables(n_tok):
    def cs(n):
        k = np.arange(n)
        ang = 2.0 * np.pi * ((k[:, None] * k[None, :]) % n) / n
        return np.cos(ang), np.sin(ang)
    cc, sc = cs(FNET_CH)
    cn, sn = cs(n_tok)
    return tuple(jnp.asarray(t, F32) for t in (cc, sc, np.concatenate([cn, -sn], axis=1)))


def _pad_heads(w, width):
    lead = w.shape[:-1]
    w = w.reshape(lead + (N_HEADS, width))
    w = jnp.pad(w, [(0, 0)] * len(lead) + [(0, 0), (0, HEAD_PAD - width)])
    return w.reshape(lead + (N_HEADS * HEAD_PAD,))


def _pair_swap_lanes(n_groups):
    perm = np.arange(n_groups * LANES).reshape(n_groups, LANES)
    rot = perm[:, ROPE_LANE0:ROPE_LANE0 + QK_ROPE].reshape(n_groups, QK_ROPE // 2, 2)[:, :, ::-1]
    perm[:, ROPE_LANE0:ROPE_LANE0 + QK_ROPE] = rot.reshape(n_groups, QK_ROPE)
    return perm.reshape(-1)


def _rotary_only(n_groups):
    lane = np.arange(n_groups * LANES) % LANES
    return jnp.asarray((lane >= ROPE_LANE0) & (lane < ROPE_LANE0 + QK_ROPE), F32)


def _head_tables(gain, rope_tabs, n_tok):
    if rope_tabs is None:
        return (jnp.broadcast_to(gain, (n_tok, LANES)),)
    cos, sin = rope_tabs
    return gain * cos, gain[:, _pair_swap_lanes(1)] * sin


def _layer0_weights(norm1, norm2, w_in, q_a_norm, w_q_up, q_norm, kv_a_norm, w_kv_up, k_norm, conv_w, w_o):
    c0 = Q_LORA + KV_LORA
    rope_cols = jnp.pad(w_in[:, c0:c0 + QK_ROPE], ((0, 0), (ROPE_LANE0, LANES - ROPE_LANE0 - QK_ROPE)))
    w_in_pad = jnp.concatenate([w_in[:, :c0], rope_cols, rope_cols[:, _pair_swap_lanes(1)],
                                w_in[:, c0 + QK_ROPE:]], axis=1)
    kv = w_kv_up.reshape(KV_LORA, N_HEADS, QK_NOPE + V_HEAD)
    w_q = _pad_heads(w_q_up, QK_DIM)
    head_gain = lambda g: jnp.pad(g, (0, HEAD_PAD - QK_DIM)).reshape(1, -1)
    return dict(
        norm1=norm1.reshape(1, -1), norm2=norm2.reshape(1, -1), w_in=w_in_pad.astype(BF16),
        q_a_norm=q_a_norm.reshape(1, -1), w_q=w_q.astype(BF16),
        w_q_swap=(w_q[:, _pair_swap_lanes(N_HEADS)] * _rotary_only(N_HEADS)).astype(BF16),
        q_gain=head_gain(q_norm) * (QK_DIM ** -0.5), kv_a_norm=kv_a_norm.reshape(1, -1),
        w_k=_pad_heads(kv[:, :, :QK_NOPE].reshape(KV_LORA, -1), QK_NOPE).astype(BF16),
        w_v=_pad_heads(kv[:, :, QK_NOPE:].reshape(KV_LORA, -1), V_HEAD).astype(BF16), k_gain=head_gain(k_norm),
        conv_w=conv_w, w_o=w_o.astype(BF16))


def _router_weights(w_router):
    hi, lo = _split_hi_lo(jnp.pad(w_router, ((0, 0), (0, LANES - N_EXPERTS))))
    return dict(wr_hi=hi, wr_lo=lo)


def _moe(sets, tri, w):
    routed = []
    for x1, h2, aff, mod, n_tok in sets:
        cap = CAPACITY_FACTOR * x1.shape[0] // N_EXPERTS
        posm, s_tab = _route(aff, tri, cap)
        routed.append((posm, s_tab, _dispatch(h2, posm, aff, s_tab, cap)))
    ys = _ffn([r[2] for r in routed], [r[1] for r in routed], w["wg"], w["wu"], w["wd"])
    return [_combine(y, posm, s_tab, x1, mod, n_tok)
            for y, (posm, s_tab, _), (x1, _, _, mod, n_tok) in zip(ys, routed, sets)]


def _mixer0(x, mod, l0, ctx, rope_tabs):
    n_b, n_tok, _ = x.shape
    q, k, v, conv, ckv, kr = _front0(x, mod, l0, rope_tabs)
    if ctx is not None:
        kc, vc = _ctx_kv(ctx[0], ctx[1], l0)
        attn = _attention(q, k, v, kc, vc)
    else:
        attn = _attention(q, k, v)
    flat = lambda a: a.reshape(n_b * n_tok, a.shape[-1])
    return _post0(flat(x), flat(attn), flat(conv), mod, l0, n_tok), (ckv, kr[:, :, ROPE_LANE0:ROPE_LANE0 + QK_ROPE])


def kernel(x_prompt, x_sample, c, cache_c_kv_l0, cache_k_rope_l0, c_ctx, norm1_l0, norm2_l0, w_mod_l0, b_mod_l0, w_in_l0, q_a_norm_l0, w_q_up_l0, q_norm_l0, kv_a_norm_l0, w_kv_up_l0, k_norm_l0, conv_w_l0, w_o_l0, w_router_l0, w_gate_l0, w_up_l0, w_down_l0, norm1_l1, norm2_l1, w_mod_l1, b_mod_l1, w_f_l1, w_router_l1, w_gate_l1, w_up_l1, w_down_l1):
    n_dec = c.shape[0]
    cond = jnp.concatenate([c_ctx[None, :], c, jnp.zeros((16 - 1 - n_dec, D_MODEL), F32)], axis=0)
    m0 = _modulation(cond, w_mod_l0, b_mod_l0)
    m1 = _modulation(cond, w_mod_l1, b_mod_l1)
    mods_prompt = (m0[0:1], m1[0:1])
    mods_sample = (m0[1:1 + n_dec], m1[1:1 + n_dec])

    l0 = _layer0_weights(norm1_l0, norm2_l0, w_in_l0, q_a_norm_l0, w_q_up_l0, q_norm_l0, kv_a_norm_l0,
                         w_kv_up_l0, k_norm_l0, conv_w_l0, w_o_l0)
    l0.update(_router_weights(w_router_l0))
    l0.update(wg=w_gate_l0, wu=w_up_l0, wd=w_down_l0)
    l1 = dict(norm1=norm1_l1.reshape(1, -1), norm2=norm2_l1.reshape(1, -1), w_f=w_f_l1.astype(BF16))
    l1.update(_router_weights(w_router_l1))
    l1.update(wg=w_gate_l1, wu=w_up_l1, wd=w_down_l1)

    tri = jnp.asarray(np.triu(np.ones((ROWS, ROWS)), 1), BF16)
    k_rope_pad = jnp.pad(cache_k_rope_l0, ((0, 0), (0, 0), (ROPE_LANE0, LANES - ROPE_LANE0 - QK_ROPE)))
    xs = (x_prompt, x_sample)
    mods = (mods_prompt, mods_sample)
    n_toks = tuple(x.shape[1] for x in xs)

    (front_p, (new_c_kv, new_k_rope)) = _mixer0(x_prompt, mods_prompt[0], l0, None, None)
    (front_s, _) = _mixer0(x_sample, mods_sample[0], l0, (cache_c_kv_l0, k_rope_pad), _rope_tables(n_toks[1]))
    ys = _moe([tuple(front) + (mod[0], n_tok) for front, mod, n_tok in zip((front_p, front_s), mods, n_toks)],
              tri, l0)

    sets = []
    for y, x, mod, n_tok in zip(ys, xs, mods, n_toks):
        x1, h2, aff = _fnet(y.reshape(x.shape), mod[1], l1, _dft_tables(n_tok))
        sets.append((x1.reshape(-1, D_MODEL), h2.reshape(-1, D_MODEL), aff, mod[1], n_tok))
    y_prompt, y_sample = (y.reshape(x.shape) for y, x in zip(_moe(sets, tri, l1), xs))
    return (y_prompt, y_sample, new_c_kv, new_k_rope)
```

```python
import functools

import jax
import jax.numpy as jnp
import numpy as np
from jax import lax
from jax.experimental import pallas as pl
from jax.experimental.pallas import tpu as pltpu

D_MODEL = 1024
GRID_W = 64
N_HEADS = 8
QK_NOPE = 64
QK_ROPE = 32
QK_DIM = QK_NOPE + QK_ROPE
V_HEAD = 64
Q_LORA = 384
KV_LORA = 256
CONV_CH = 512
FNET_GROUPS = 4
FNET_CH = D_MODEL // FNET_GROUPS
N_EXPERTS = 16
EXPERT_FF = 512
CAPACITY_FACTOR = 2
ROPE_THETA = 10000.0
EPS = 1e-6

LANES = 128
HEAD_PAD = LANES
ROWS = 256
WIDE_ROWS = 512
ALIGN = 8
WIN = 64
MAX_UNITS = N_EXPERTS * (ROWS // WIN)
COMMON_ROWS = N_EXPERTS * WIN
XE_W = D_MODEL + LANES
IN0_PAD = Q_LORA + KV_LORA + 2 * LANES + 3 * CONV_CH
ROPE_LANE0 = QK_NOPE
VMEM_LIMIT = 56 * 1024 * 1024

F32 = jnp.float32
BF16 = jnp.bfloat16


def _dot(a, b):
    return jnp.dot(a, b, preferred_element_type=F32)


def _dot_nt(a, b):
    return lax.dot_general(a, b, (((1,), (1,)), ((), ())), preferred_element_type=F32)


def _split_hi_lo(x):
    hi = x.astype(BF16)
    lo = (x - hi.astype(F32)).astype(BF16)
    return hi, lo


def _params(n_axes):
    return pltpu.CompilerParams(dimension_semantics=("arbitrary",) * n_axes,
                                vmem_limit_bytes=VMEM_LIMIT)


def _rms(x, gain):
    return x * lax.rsqrt(jnp.mean(x * x, axis=-1, keepdims=True) + EPS) * gain


def _mod_kernel(cond_ref, w_ref, b_ref, o_ref):
    c = cond_ref[...]
    s = c * (1.0 / (1.0 + jnp.exp(-c)))
    s_hi, s_lo = _split_hi_lo(s)
    w = w_ref[...].astype(BF16)
    o_ref[...] = _dot(s_hi, w) + _dot(s_lo, w) + b_ref[...]


def _modulation(cond, w_mod, b_mod):
    n_rows = cond.shape[0]
    tn = 1536
    out = pl.pallas_call(
        _mod_kernel,
        out_shape=jax.ShapeDtypeStruct((n_rows, 6 * D_MODEL), F32),
        grid=(6 * D_MODEL // tn,),
        in_specs=[pl.BlockSpec((n_rows, D_MODEL), lambda i: (0, 0)),
                  pl.BlockSpec((D_MODEL, tn), lambda i: (0, i)),
                  pl.BlockSpec((1, tn), lambda i: (0, i))],
        out_specs=pl.BlockSpec((n_rows, tn), lambda i: (0, i)),
        compiler_params=_params(1),
        name="modulation",
    )(cond, w_mod, b_mod.reshape(1, -1))
    return out.reshape(n_rows, 6, D_MODEL)


def _head_norm_rope(xh, gain_cos, swapped_sin):
    ss = jnp.sum(xh * xh, axis=-1, keepdims=True) * (1.0 / QK_DIM)
    y = xh * gain_cos
    if swapped_sin is not None:
        y = y + swapped_sin
    return y * lax.rsqrt(ss + EPS)


def _expand_kv(ckv_bf, r, swapped_sin, wk_ref, wv_ref, gain_cos, k_ref, v_ref, rows):
    kf = _dot(ckv_bf, wk_ref[...])
    vf = _dot(ckv_bf, wv_ref[...])
    if v_ref.shape[-1] == N_HEADS * HEAD_PAD:
        lane = lax.broadcasted_iota(jnp.int32, (1, N_HEADS * HEAD_PAD), 1)
        vf = vf + jnp.where((lane & (HEAD_PAD - 1)) == V_HEAD, 1.0, 0.0)
    v_ref[rows, :] = vf.astype(BF16)
    for h in range(N_HEADS):
        sl = slice(h * HEAD_PAD, (h + 1) * HEAD_PAD)
        k_ref[rows, sl] = _head_norm_rope(kf[:, sl] + r, gain_cos, swapped_sin).astype(BF16)


def _front0_kernel(*refs, n_tok, rope):
    (x_ref, mod_ref, n1_ref, win_ref, qan_ref, wq_ref, kvan_ref, wk_ref, wv_ref, cw_ref,
     qa_ref, ka_ref) = refs[:12]
    pos = 12
    if rope:
        wqs_ref, qb_ref, kb_ref = refs[pos:pos + 3]
        pos += 3
    q_ref, k_ref, v_ref, conv_ref, ckv_ref, kr_ref, cu_s, gb_s = refs[pos:]
    sh1, sc1 = mod_ref[0:1, :], mod_ref[1:2, :]
    c_rope = Q_LORA + KV_LORA
    wide = min(n_tok, WIDE_ROWS)
    for c in range(n_tok // ROWS):
        rows = pl.ds(c * ROWS, ROWS)
        if (c * ROWS) % wide == 0:
            wide_rows = pl.ds(c * ROWS, wide)
            h = _rms(x_ref[wide_rows, :], n1_ref[...]) * (1.0 + sc1) + sh1
            proj_wide = _dot(h.astype(BF16), win_ref[...])
        off = (c * ROWS) % wide
        proj = proj_wide[off:off + ROWS, :]
        cq = _rms(proj[:, :Q_LORA], qan_ref[...]).astype(BF16)
        qf = _dot(cq, wq_ref[...])
        qfs = _dot(cq, wqs_ref[...]) if rope else None
        for hd in range(N_HEADS):
            sl = slice(hd * HEAD_PAD, (hd + 1) * HEAD_PAD)
            q_sin = qfs[:, sl] * qb_ref[rows, :] if rope else None
            q_ref[rows, sl] = _head_norm_rope(qf[:, sl], qa_ref[rows, :], q_sin).astype(BF16)
        ckv = _rms(proj[:, Q_LORA:c_rope], kvan_ref[...])
        ckv_ref[rows, :] = ckv
        r = proj[:, c_rope:c_rope + LANES]
        kr_ref[rows, :] = r
        k_sin = proj[:, c_rope + LANES:c_rope + 2 * LANES] * kb_ref[rows, :] if rope else None
        _expand_kv(ckv.astype(BF16), r, k_sin, wk_ref, wv_ref, ka_ref[rows, :], k_ref, v_ref, rows)
        c0 = c_rope + 2 * LANES
        gb_s[rows, :] = proj[:, c0:c0 + CONV_CH]
        cu_s[rows, :] = proj[:, c0 + CONV_CH:c0 + 2 * CONV_CH] * proj[:, c0 + 2 * CONV_CH:c0 + 3 * CONV_CH]
    cu = cu_s[...]
    row = lax.broadcasted_iota(jnp.int32, cu.shape, 0)
    prev = jnp.where(row == 0, 0.0, pltpu.roll(cu, 1, 0))
    nxt = jnp.where(row == n_tok - 1, 0.0, pltpu.roll(cu, n_tok - 1, 0))
    conv = gb_s[...] * (cw_ref[0:1, :] * prev + cw_ref[1:2, :] * cu + cw_ref[2:3, :] * nxt)
    conv_ref[...] = conv.astype(BF16)


def _const_spec(shape):
    return pl.BlockSpec(shape, lambda b: (0,) * len(shape))


def _front0(x, mod, w, rope_tabs, w_v):
    n_b, n_tok, _ = x.shape
    v_width = w_v.shape[1]
    shared_mod = mod.shape[0] == 1
    seq = lambda width: pl.BlockSpec((None, n_tok, width), lambda b: (b, 0, 0))
    in_specs = [seq(D_MODEL),
                pl.BlockSpec((None, 6, D_MODEL), (lambda b: (0, 0, 0)) if shared_mod else (lambda b: (b, 0, 0))),
                _const_spec((1, D_MODEL)), _const_spec((D_MODEL, IN0_PAD)), _const_spec((1, Q_LORA)),
                _const_spec((Q_LORA, N_HEADS * HEAD_PAD)),
                _const_spec((1, KV_LORA)), _const_spec((KV_LORA, N_HEADS * HEAD_PAD)),
                _const_spec((KV_LORA, v_width)), _const_spec((3, CONV_CH)),
                _const_spec((n_tok, LANES)), _const_spec((n_tok, LANES))]
    q_tabs, k_tabs = _head_tables(w["q_gain"], rope_tabs, n_tok), _head_tables(w["k_gain"], rope_tabs, n_tok)
    args = [x, mod, w["norm1"], w["w_in"], w["q_a_norm"], w["w_q"], w["kv_a_norm"],
            w["w_k"], w_v, w["conv_w"], q_tabs[0], k_tabs[0]]
    if rope_tabs is not None:
        in_specs += [_const_spec((Q_LORA, N_HEADS * HEAD_PAD)), _const_spec((n_tok, LANES)), _const_spec((n_tok, LANES))]
        args += [w["w_q_swap"], q_tabs[1], k_tabs[1]]
    out_shape = [jax.ShapeDtypeStruct((n_b, n_tok, N_HEADS * HEAD_PAD), BF16),
                 jax.ShapeDtypeStruct((n_b, n_tok, N_HEADS * HEAD_PAD), BF16),
                 jax.ShapeDtypeStruct((n_b, n_tok, v_width), BF16),
                 jax.ShapeDtypeStruct((n_b, n_tok, CONV_CH), BF16),
                 jax.ShapeDtypeStruct((n_b, n_tok, KV_LORA), F32),
                 jax.ShapeDtypeStruct((n_b, n_tok, LANES), F32)]
    out_specs = [seq(N_HEADS * HEAD_PAD), seq(N_HEADS * HEAD_PAD), seq(v_width), seq(CONV_CH),
                 seq(KV_LORA), seq(LANES)]
    return pl.pallas_call(
        functools.partial(_front0_kernel, n_tok=n_tok, rope=rope_tabs is not None),
        out_shape=out_shape, grid=(n_b,), in_specs=in_specs, out_specs=out_specs,
        scratch_shapes=[pltpu.VMEM((n_tok, CONV_CH), F32), pltpu.VMEM((n_tok, CONV_CH), F32)],
        compiler_params=_params(1), name="front0",
    )(*args)


def _ctx_kv_kernel(ckv_ref, r_ref, wk_ref, wv_ref, ka_ref, k_ref, v_ref):
    rows = pl.ds(0, ckv_ref.shape[0])
    _expand_kv(ckv_ref[...].astype(BF16), r_ref[...], None, wk_ref, wv_ref, ka_ref[...], k_ref, v_ref, rows)


def _ctx_kv(cache_c_kv, cache_k_rope_pad, w, w_v):
    n_b, n_ctx, _ = cache_c_kv.shape
    v_width = w_v.shape[1]
    seq = lambda width: pl.BlockSpec((None, n_ctx, width), lambda b: (b, 0, 0))
    return pl.pallas_call(
        _ctx_kv_kernel,
        out_shape=[jax.ShapeDtypeStruct((n_b, n_ctx, N_HEADS * HEAD_PAD), BF16),
                   jax.ShapeDtypeStruct((n_b, n_ctx, v_width), BF16)],
        grid=(n_b,),
        in_specs=[seq(KV_LORA), seq(LANES), _const_spec((KV_LORA, N_HEADS * HEAD_PAD)),
                  _const_spec((KV_LORA, v_width)), _const_spec((n_ctx, LANES))],
        out_specs=[seq(N_HEADS * HEAD_PAD), seq(v_width)],
        compiler_params=_params(1), name="ctx_kv",
    )(cache_c_kv, cache_k_rope_pad, w["w_k"], w_v, _head_tables(w["k_gain"], None, n_ctx)[0])


def _attn_kernel(*refs, with_ctx):
    if with_ctx:
        q_ref, k_ref, v_ref, kc_ref, vc_ref, o_ref = refs
    else:
        q_ref, k_ref, v_ref, o_ref = refs
    ones_lane = v_ref.shape[-1] == N_HEADS * HEAD_PAD
    lane = lax.broadcasted_iota(jnp.int32, (q_ref.shape[0], LANES), 1)
    for pair in range(N_HEADS // 2):
        outs = []
        for hd in (2 * pair, 2 * pair + 1):
            sl = slice(hd * HEAD_PAD, (hd + 1) * HEAD_PAD)
            vsl = sl if ones_lane else slice(pair * LANES, (pair + 1) * LANES)
            qh = q_ref[:, sl]
            s = _dot_nt(qh, k_ref[:, sl])
            m = jnp.max(s, axis=-1, keepdims=True)
            if with_ctx:
                sc = _dot_nt(qh, kc_ref[:, sl])
                m = jnp.maximum(m, jnp.max(sc, axis=-1, keepdims=True))
            p = jnp.exp(s - m)
            o = _dot(p.astype(BF16), v_ref[:, vsl])
            if with_ctx:
                pc = jnp.exp(sc - m)
                o = o + _dot(pc.astype(BF16), vc_ref[:, vsl])
            if ones_lane:
                den = o[:, V_HEAD:V_HEAD + 1]
            else:
                den = jnp.sum(p, axis=-1, keepdims=True)
                if with_ctx:
                    den = den + jnp.sum(pc, axis=-1, keepdims=True)
            outs.append(o / den)
        odd = pltpu.roll(outs[1], V_HEAD, 1) if ones_lane else outs[1]
        o_ref[:, pair * LANES:(pair + 1) * LANES] = jnp.where(lane < V_HEAD, outs[0], odd).astype(BF16)


def _attention(q, k, v, kc=None, vc=None):
    n_b, n_tok, _ = q.shape
    with_ctx = kc is not None
    tq = min(n_tok, WIDE_ROWS)
    qspec = lambda width: pl.BlockSpec((None, tq, width), lambda b, i: (b, i, 0))
    kvspec = lambda a: pl.BlockSpec((None,) + a.shape[1:], lambda b, i: (b, 0, 0))
    in_specs = [qspec(N_HEADS * HEAD_PAD), kvspec(k), kvspec(v)]
    args = [q, k, v]
    if with_ctx:
        in_specs += [kvspec(kc), kvspec(vc)]
        args += [kc, vc]
    return pl.pallas_call(
        functools.partial(_attn_kernel, with_ctx=with_ctx),
        out_shape=jax.ShapeDtypeStruct((n_b, n_tok, N_HEADS * V_HEAD), BF16),
        grid=(n_b, n_tok // tq), in_specs=in_specs, out_specs=qspec(N_HEADS * V_HEAD),
        compiler_params=_params(2), name="attention",
    )(*args)


def _moe_front(x1, mod_ref, n2_ref, wrh_ref, wrl_ref):
    sh2, sc2 = mod_ref[3:4, :], mod_ref[4:5, :]
    h2 = _rms(x1, n2_ref[...]) * (1.0 + sc2) + sh2
    h_hi, h_lo = _split_hi_lo(h2)
    logit = _dot(h_hi, wrh_ref[...]) + _dot(h_lo, wrh_ref[...]) + _dot(h_hi, wrl_ref[...])
    logit = logit.T[0:N_EXPERTS, :]
    e = jnp.exp(logit - jnp.max(logit, axis=0, keepdims=True))
    return h_hi, e / jnp.sum(e, axis=0, keepdims=True)


def _post0_kernel(x_ref, attn_ref, conv_ref, wo_ref, mod_ref, n2_ref, wrh_ref, wrl_ref,
                  x1_ref, h2_ref, aff_ref):
    n_attn = N_HEADS * V_HEAD
    mix = _dot(attn_ref[...], wo_ref[0:n_attn, :]) + _dot(conv_ref[...], wo_ref[n_attn:, :])
    for t in range(x_ref.shape[0] // ROWS):
        rows = pl.ds(t * ROWS, ROWS)
        x1 = x_ref[rows, :] + mod_ref[2:3, :] * mix[t * ROWS:(t + 1) * ROWS, :]
        x1_ref[rows, :] = x1
        h2, aff = _moe_front(x1, mod_ref, n2_ref, wrh_ref, wrl_ref)
        h2_ref[rows, :] = h2
        aff_ref[t] = aff


def _post0(x, attn, conv, mod, w, n_tok):
    n_all = x.shape[0]
    rows = min(n_tok, WIDE_ROWS)
    tiles_per_seq = n_tok // rows
    shared_mod = mod.shape[0] == 1
    tile = lambda width: pl.BlockSpec((rows, width), lambda i: (i, 0))
    const = lambda shape: pl.BlockSpec(shape, lambda i: (0,) * len(shape))
    mod_map = (lambda i: (0, 0, 0)) if shared_mod else (lambda i: (i // tiles_per_seq, 0, 0))
    return pl.pallas_call(
        _post0_kernel,
        out_shape=[jax.ShapeDtypeStruct((n_all, D_MODEL), F32),
                   jax.ShapeDtypeStruct((n_all, D_MODEL), BF16),
                   jax.ShapeDtypeStruct((n_all // ROWS, N_EXPERTS, ROWS), F32)],
        grid=(n_all // rows,),
        in_specs=[tile(D_MODEL), tile(N_HEADS * V_HEAD), tile(CONV_CH), const((D_MODEL, D_MODEL)),
                  pl.BlockSpec((None, 6, D_MODEL), mod_map), const((1, D_MODEL)),
                  const((D_MODEL, LANES)), const((D_MODEL, LANES))],
        out_specs=[tile(D_MODEL), tile(D_MODEL),
                   pl.BlockSpec((rows // ROWS, N_EXPERTS, ROWS), lambda i: (i, 0, 0))],
        compiler_params=_params(1), name="post0",
    )(x, attn, conv, w["w_o"], mod, w["norm2"], w["wr_hi"], w["wr_lo"])


def _fnet_kernel(x_ref, mod_ref, n1_ref, cc_ref, sc_ref, dft_ref, wf_ref, n2_ref, wrh_ref, wrl_ref,
                 x1_ref, h2_ref, aff_ref, y_s, *, n_tok):
    sh1, sc1, g1 = mod_ref[0:1, :], mod_ref[1:2, :], mod_ref[2:3, :]
    n_chunks = n_tok // ROWS
    cc, sc = cc_ref[...].astype(BF16), sc_ref[...].astype(BF16)
    for c in range(n_chunks):
        rows = pl.ds(c * ROWS, ROWS)
        h = (_rms(x_ref[rows, :], n1_ref[...]) * (1.0 + sc1) + sh1).astype(BF16)
        for g in range(FNET_GROUPS):
            sl = slice(g * FNET_CH, (g + 1) * FNET_CH)
            y_s[pl.ds(c * ROWS, ROWS), sl] = _dot(h[:, sl], cc).astype(BF16)
            y_s[pl.ds(n_tok + c * ROWS, ROWS), sl] = _dot(h[:, sl], sc).astype(BF16)
    scale = 1.0 / float(np.sqrt(n_tok * FNET_CH))
    wide = min(n_tok, WIDE_ROWS)
    for c in range(n_chunks):
        rows = pl.ds(c * ROWS, ROWS)
        if (c * ROWS) % wide == 0:
            f = _dot(dft_ref[pl.ds(c * ROWS, wide), :].astype(BF16), y_s[...]) * scale
            mix = _dot(f.astype(BF16), wf_ref[...])
        off = (c * ROWS) % wide
        x1 = x_ref[rows, :] + g1 * mix[off:off + ROWS, :]
        x1_ref[rows, :] = x1
        h2, aff = _moe_front(x1, mod_ref, n2_ref, wrh_ref, wrl_ref)
        h2_ref[rows, :] = h2
        aff_ref[c] = aff


def _fnet(x, mod, w, dft):
    n_b, n_tok, _ = x.shape
    shared_mod = mod.shape[0] == 1
    seq = lambda width: pl.BlockSpec((None, n_tok, width), lambda b: (b, 0, 0))
    cc, sc, dft_n = dft
    tiles = n_tok // ROWS
    return pl.pallas_call(
        functools.partial(_fnet_kernel, n_tok=n_tok),
        out_shape=[jax.ShapeDtypeStruct((n_b, n_tok, D_MODEL), F32),
                   jax.ShapeDtypeStruct((n_b, n_tok, D_MODEL), BF16),
                   jax.ShapeDtypeStruct((n_b * tiles, N_EXPERTS, ROWS), F32)],
        grid=(n_b,),
        in_specs=[seq(D_MODEL),
                  pl.BlockSpec((None, 6, D_MODEL), (lambda b: (0, 0, 0)) if shared_mod else (lambda b: (b, 0, 0))),
                  _const_spec((1, D_MODEL)), _const_spec((FNET_CH, FNET_CH)), _const_spec((FNET_CH, FNET_CH)),
                  pl.BlockSpec((n_tok, 2 * n_tok), lambda b: (0, 0), pipeline_mode=pl.Buffered(1)),
                  _const_spec((D_MODEL, D_MODEL)), _const_spec((1, D_MODEL)),
                  _const_spec((D_MODEL, LANES)), _const_spec((D_MODEL, LANES))],
        out_specs=[seq(D_MODEL), seq(D_MODEL), pl.BlockSpec((tiles, N_EXPERTS, ROWS), lambda b: (b, 0, 0))],
        scratch_shapes=[pltpu.VMEM((2 * n_tok, D_MODEL), BF16)],
        compiler_params=_params(1), name="fnet",
    )(x, mod, w["norm1"], cc, sc, dft_n, w["w_f"], w["norm2"], w["wr_hi"], w["wr_lo"])


def _route_kernel(aff_ref, tri_ref, posm_ref, s_ref, *, n_blk, cap):
    def count(pred):
        acc = jnp.zeros((N_EXPERTS, ROWS), F32)
        for b in range(n_blk):
            acc = acc + jnp.where(pred(aff_ref[b]), 1.0, 0.0)
        return jnp.sum(acc, axis=1, keepdims=True)

    def as_f32(bits):
        return pltpu.bitcast(bits, F32)

    def search(_, carry):
        lo, hi = carry
        mid = lo + ((hi - lo) >> 1)
        mid_f = as_f32(mid)
        ok = count(lambda a: a >= mid_f) >= cap
        return jnp.where(ok, mid, lo), jnp.where(ok, hi, mid)

    one_bits = 0x3F800000
    lo0 = jnp.zeros((N_EXPERTS, 1), jnp.int32)
    hi0 = jnp.full((N_EXPERTS, 1), one_bits + 1, jnp.int32)
    lo, hi = lax.fori_loop(0, 31, search, (lo0, hi0))
    lo_f, ub = as_f32(lo), as_f32(hi)
    thr = lo_f
    pending = jnp.ones((N_EXPERTS, 1), F32)
    for _ in range(3):
        cur = jnp.full((N_EXPERTS, ROWS), -1.0, F32)
        for b in range(n_blk):
            a = aff_ref[b]
            cur = jnp.maximum(cur, jnp.where((a >= lo_f) & (a < ub), a, -1.0))
        cur = jnp.max(cur, axis=1, keepdims=True)
        take = (count(lambda a: a >= cur) >= cap) & (pending > 0.0)
        thr = jnp.where(take, cur, thr)
        pending = jnp.where(take, 0.0, pending)
        ub = cur
    need = cap - count(lambda a: a > thr)

    lane = lax.broadcasted_iota(jnp.int32, (N_EXPERTS, LANES), 1)
    carry_tie = jnp.zeros((N_EXPERTS, 1), F32)
    carry_pos = jnp.zeros((N_EXPERTS, 1), F32)
    s_acc = jnp.zeros((N_EXPERTS, LANES), jnp.int32)
    for b in range(n_blk):
        v = aff_ref[b]
        eq = jnp.where(v == thr, 1.0, 0.0)
        tie_rank = _dot(eq.astype(BF16), tri_ref[...]) + carry_tie
        sel = jnp.where((v > thr) | ((v == thr) & (tie_rank < need)), 1.0, 0.0)
        pos = _dot(sel.astype(BF16), tri_ref[...]) + carry_pos
        posm_ref[b] = jnp.where(sel > 0.0, pos, -1.0)
        s_acc = jnp.where(lane == b, carry_pos.astype(jnp.int32), s_acc)
        carry_tie = carry_tie + jnp.sum(eq, axis=1, keepdims=True)
        n_sel = jnp.sum(sel, axis=1, keepdims=True)
        carry_pos = carry_pos + jnp.floor((n_sel + (ALIGN - 1)) * (1.0 / ALIGN)) * ALIGN
    s_ref[...] = jnp.where(lane == n_blk, carry_pos.astype(jnp.int32), s_acc)


def _route(aff, tri, cap):
    n_blk = aff.shape[0]
    assert n_blk < LANES
    full = lambda shape: pl.BlockSpec(shape, lambda i: (0,) * len(shape))
    posm, s_tab = pl.pallas_call(
        functools.partial(_route_kernel, n_blk=n_blk, cap=cap),
        out_shape=[jax.ShapeDtypeStruct((n_blk, N_EXPERTS, ROWS), F32),
                   jax.ShapeDtypeStruct((N_EXPERTS, LANES), jnp.int32)],
        grid=(1,),
        in_specs=[full((n_blk, N_EXPERTS, ROWS)), full((ROWS, ROWS))],
        out_specs=[full((n_blk, N_EXPERTS, ROWS)), full((N_EXPERTS, LANES))],
        compiler_params=_params(1), name="route",
    )(aff, tri)
    return posm, s_tab[:, :n_blk + 1]


def _ceil_div_pow2(x, d):
    return lax.shift_right_logical(x + (d - 1), d.bit_length() - 1)


def _block_units(s_ref, b):
    base = [s_ref[e, b] for e in range(N_EXPERTS)]
    units = [_ceil_div_pow2(s_ref[e, b + 1] - base[e], WIN) for e in range(N_EXPERTS)]
    first_unit, total = [], 0
    for e in range(N_EXPERTS):
        first_unit.append(total)
        total = total + units[e]
    return base, units, first_unit, total


def _for_units(units, fn):
    for e in range(N_EXPERTS):
        def body(u, _, e=e):
            fn(e, u)
            return 0
        lax.fori_loop(0, units[e], body, 0)


def _window_onehot(posm_ref, e, first, lo):
    slot = lax.broadcasted_iota(jnp.int32, (WIN, ROWS), 0) + first
    p = posm_ref[e:e + 1, :].astype(jnp.int32)
    return jnp.where((p == slot) & (p >= lo), 1.0, 0.0)


def _gate_lanes(e):
    lane = lax.broadcasted_iota(jnp.int32, (1, LANES), 1)
    return ((lane & (N_EXPERTS - 1)) == e) & (lane < 3 * N_EXPERTS)


def _dispatch_kernel(s_ref, h_ref, posm_ref, aff_ref, xe_ref, sel_s, x_s, sem, *, n_blk, cap_pad):
    b = pl.program_id(0)
    buf = b % 2
    base, units, first_unit, n_units = _block_units(s_ref, b)

    @pl.when(b == 0)
    def _():
        sel_s[...] = jnp.zeros_like(sel_s)

    a = jnp.concatenate([aff_ref[...], jnp.zeros((LANES - N_EXPERTS, ROWS), F32)], axis=0).T
    a_hi = a.astype(BF16).astype(F32)
    a_mid = (a - a_hi).astype(BF16).astype(F32)
    a_lo = a - a_hi - a_mid
    parts = (a_hi + pltpu.roll(a_mid, N_EXPERTS, 1) + pltpu.roll(a_lo, 2 * N_EXPERTS, 1)).astype(BF16)

    def unit_rows(first_unit_e, u):
        return pl.ds(pl.multiple_of((first_unit_e + u) * WIN, WIN), WIN)

    def pick(e, u):
        first = base[e] + u * WIN
        sel_s[unit_rows(first_unit[e], u), :] = _window_onehot(posm_ref, e, first, first).astype(BF16)

    _for_units(units, pick)

    def gather(rows):
        sel = sel_s[rows, :]
        x_s[buf, rows, 0:D_MODEL] = _dot(sel, h_ref[...]).astype(BF16)
        x_s[buf, rows, D_MODEL:XE_W] = _dot(sel, parts).astype(BF16)

    def gather_chunk(c, _):
        gather(pl.ds(pl.multiple_of(c * ROWS, ROWS), ROWS))
        return 0

    gather(pl.ds(0, COMMON_ROWS))
    lax.fori_loop(COMMON_ROWS // ROWS, _ceil_div_pow2(n_units * WIN, ROWS), gather_chunk, 0)

    def unit_copy(slot, blk_base_e, blk_first_e, e, u):
        dst = pl.ds(pl.multiple_of(blk_base_e + u * WIN, ALIGN), WIN)
        return pltpu.make_async_copy(x_s.at[slot, unit_rows(blk_first_e, u), :], xe_ref.at[e, dst, :],
                                     sem.at[slot, e])

    def for_expert_units(n, fn):
        lax.fori_loop(0, n, lambda u, _: (fn(u), 0)[1], 0)

    prev_base, prev_units, prev_first, _ = _block_units(s_ref, jnp.maximum(b - 1, 0))
    for e in range(N_EXPERTS):
        for_expert_units(jnp.where(b > 0, prev_units[e], 0),
                         lambda u, e=e: unit_copy(1 - buf, prev_base[e], prev_first[e], e, u).wait())
        for_expert_units(units[e], lambda u, e=e: unit_copy(buf, base[e], first_unit[e], e, u).start())

    @pl.when(b == n_blk - 1)
    def _():
        for e in range(N_EXPERTS):
            for_expert_units(units[e], lambda u, e=e: unit_copy(buf, base[e], first_unit[e], e, u).wait())
        x_s[buf, 0:WIN, :] = jnp.zeros((WIN, XE_W), BF16)

        def tail(fn):
            for e in range(N_EXPERTS):
                total = s_ref[e, n_blk]
                n_win = (cap_pad - total) // WIN

                def wide(c, _):
                    row = pl.multiple_of(total + c * WIN, ALIGN)
                    fn(pltpu.make_async_copy(x_s.at[buf, pl.ds(0, WIN), :],
                                             xe_ref.at[e, pl.ds(row, WIN), :], sem.at[buf, e]))
                    return 0

                def narrow(c, _):
                    row = pl.multiple_of(total + n_win * WIN + c * ALIGN, ALIGN)
                    fn(pltpu.make_async_copy(x_s.at[buf, pl.ds(0, ALIGN), :],
                                             xe_ref.at[e, pl.ds(row, ALIGN), :], sem.at[buf, e]))
                    return 0

                lax.fori_loop(0, n_win, wide, 0)
                lax.fori_loop(0, (cap_pad - total - n_win * WIN) // ALIGN, narrow, 0)

        tail(lambda c: c.start())
        tail(lambda c: c.wait())


def _ffn_kernel(*refs, n_blks):
    n_sets = len(n_blks)
    s_refs, x_refs = refs[:n_sets], refs[n_sets:2 * n_sets]
    wg_ref, wu_ref, wd_ref = refs[2 * n_sets:2 * n_sets + 3]
    y_refs = refs[2 * n_sets + 3:3 * n_sets + 3]
    wg_s, wu_s, wd_s = refs[3 * n_sets + 3:]
    e = pl.program_id(0)
    wg_s[...] = wg_ref[...].astype(BF16)
    wu_s[...] = wu_ref[...].astype(BF16)
    wd_s[...] = wd_ref[...].astype(BF16)
    mine = _gate_lanes(e)

    half = ROWS // 2
    for s_ref, x_ref, y_ref, n_blk in zip(s_refs, x_refs, y_refs, n_blks):
        n_half = _ceil_div_pow2(s_ref[e, n_blk], half)

        def tile(rows, x_ref=x_ref, y_ref=y_ref):
            x = x_ref[rows, 0:D_MODEL]
            pieces = x_ref[rows, D_MODEL:XE_W].astype(F32)
            gate = jnp.sum(jnp.where(mine, pieces, 0.0), axis=1, keepdims=True)
            a = _dot(x, wg_s[...])
            u = _dot(x, wu_s[...])
            hid = a * (1.0 / (1.0 + jnp.exp(-a))) * u * gate
            y_ref[rows, :] = _dot(hid.astype(BF16), wd_s[...]).astype(BF16)

        def full(j, _, tile=tile):
            tile(pl.ds(pl.multiple_of(j * ROWS, ROWS), ROWS))
            return 0

        def blank(j, _, y_ref=y_ref):
            y_ref[pl.ds(pl.multiple_of(j * half, half), half), :] = jnp.zeros((half, D_MODEL), BF16)
            return 0

        lax.fori_loop(0, lax.shift_right_logical(n_half, 1), full, 0)

        @pl.when((n_half & 1) == 1)
        def _(tile=tile, n_half=n_half):
            tile(pl.ds(pl.multiple_of((n_half - 1) * half, half), half))

        lax.fori_loop(n_half, y_ref.shape[0] // half, blank, 0)


def _combine_kernel(s_ref, y_ref, posm_ref, x1_ref, mod_ref, o_ref, yw_s, sel_s, acc_s, sem,
                    *, n_blk, cap_pad):
    b = pl.program_id(0)
    buf = b % 2
    base, units, first_unit, n_units = _block_units(s_ref, b)

    def unit_rows(first_unit_e, u):
        return pl.ds(pl.multiple_of((first_unit_e + u) * WIN, WIN), WIN)

    def bounds(base_e, u):
        lo = base_e + u * WIN
        return lo, jnp.minimum(lo, cap_pad - WIN)

    def copies(slot, blk, fn):
        blk_base, blk_units, blk_first, _ = _block_units(s_ref, blk)

        def one(e, u):
            src = pl.ds(pl.multiple_of(bounds(blk_base[e], u)[1], ALIGN), WIN)
            fn(pltpu.make_async_copy(y_ref.at[e, src, :], yw_s.at[slot, unit_rows(blk_first[e], u), :],
                                     sem.at[slot]))

        _for_units(blk_units, one)

    @pl.when(b == 0)
    def _():
        yw_s[...] = jnp.zeros_like(yw_s)
        copies(0, 0, lambda c: c.start())

    @pl.when(b + 1 < n_blk)
    def _():
        copies(1 - buf, b + 1, lambda c: c.start())

    def pick(e, u):
        lo, first = bounds(base[e], u)
        sel_s[unit_rows(first_unit[e], u), :] = _window_onehot(posm_ref, e, first, lo)

    _for_units(units, pick)
    n_chunks = jnp.maximum(_ceil_div_pow2(n_units * WIN, ROWS), COMMON_ROWS // ROWS)

    def blank(u, _):
        sel_s[unit_rows(u, 0), :] = jnp.zeros((WIN, ROWS), F32)
        return 0

    lax.fori_loop(n_units, n_chunks * (ROWS // WIN), blank, 0)
    copies(buf, b, lambda c: c.wait())

    def weighted(rows):
        return _dot(sel_s[rows, :].T.astype(BF16), yw_s[buf, rows, :])

    def add_chunk(c, _):
        acc_s[...] += weighted(pl.ds(pl.multiple_of(c * ROWS, ROWS), ROWS))
        return 0

    acc_s[...] = weighted(pl.ds(0, COMMON_ROWS))
    lax.fori_loop(COMMON_ROWS // ROWS, n_chunks, add_chunk, 0)
    o_ref[...] = x1_ref[...] + mod_ref[5:6, :] * acc_s[...]


def _padded_capacity(cap, n_blk):
    return -(-(cap + (ALIGN - 1) * n_blk + WIN) // ROWS) * ROWS


def _block_spec(*shape):
    return pl.BlockSpec((None,) + shape, lambda b, s: (b,) + (0,) * len(shape))


def _dispatch(h2, posm, aff, s_tab, cap):
    n_blk = h2.shape[0] // ROWS
    cap_pad = _padded_capacity(cap, n_blk)
    return pl.pallas_call(
        functools.partial(_dispatch_kernel, n_blk=n_blk, cap_pad=cap_pad),
        out_shape=jax.ShapeDtypeStruct((N_EXPERTS, cap_pad, XE_W), BF16),
        grid_spec=pltpu.PrefetchScalarGridSpec(
            num_scalar_prefetch=1, grid=(n_blk,),
            in_specs=[pl.BlockSpec((ROWS, D_MODEL), lambda b, s: (b, 0)), _block_spec(N_EXPERTS, ROWS),
                      _block_spec(N_EXPERTS, ROWS)],
            out_specs=pl.BlockSpec(memory_space=pl.ANY),
            scratch_shapes=[pltpu.VMEM((MAX_UNITS * WIN, ROWS), BF16), pltpu.VMEM((2, MAX_UNITS * WIN, XE_W), BF16),
                            pltpu.SemaphoreType.DMA((2, N_EXPERTS))]),
        compiler_params=_params(1), name="dispatch",
    )(s_tab, h2, posm, aff)


def _ffn(xes, s_tabs, wg, wu, wd):
    n_sets = len(xes)
    per_expert = lambda rows, cols: pl.BlockSpec((None, rows, cols), lambda e, *s: (e, 0, 0))
    return pl.pallas_call(
        functools.partial(_ffn_kernel, n_blks=tuple(s.shape[1] - 1 for s in s_tabs)),
        out_shape=[jax.ShapeDtypeStruct(xe.shape[:2] + (D_MODEL,), BF16) for xe in xes],
        grid_spec=pltpu.PrefetchScalarGridSpec(
            num_scalar_prefetch=n_sets, grid=(N_EXPERTS,),
            in_specs=[per_expert(xe.shape[1], XE_W) for xe in xes]
            + [per_expert(D_MODEL, EXPERT_FF), per_expert(D_MODEL, EXPERT_FF), per_expert(EXPERT_FF, D_MODEL)],
            out_specs=[per_expert(xe.shape[1], D_MODEL) for xe in xes],
            scratch_shapes=[pltpu.VMEM((D_MODEL, EXPERT_FF), BF16), pltpu.VMEM((D_MODEL, EXPERT_FF), BF16),
                            pltpu.VMEM((EXPERT_FF, D_MODEL), BF16)]),
        compiler_params=_params(1), name="ffn",
    )(*s_tabs, *xes, wg, wu, wd)


def _combine(y, posm, s_tab, x1, mod, n_tok):
    n_all = x1.shape[0]
    n_blk = n_all // ROWS
    tiles_per_seq = n_tok // ROWS
    mod_map = (lambda b, s: (0, 0, 0)) if mod.shape[0] == 1 else (lambda b, s: (b // tiles_per_seq, 0, 0))
    rows = pl.BlockSpec((ROWS, D_MODEL), lambda b, s: (b, 0))
    return pl.pallas_call(
        functools.partial(_combine_kernel, n_blk=n_blk, cap_pad=y.shape[1]),
        out_shape=jax.ShapeDtypeStruct((n_all, D_MODEL), F32),
        grid_spec=pltpu.PrefetchScalarGridSpec(
            num_scalar_prefetch=1, grid=(n_blk,),
            in_specs=[pl.BlockSpec(memory_space=pl.ANY), _block_spec(N_EXPERTS, ROWS), rows,
                      pl.BlockSpec((None, 6, D_MODEL), mod_map)],
            out_specs=rows,
            scratch_shapes=[pltpu.VMEM((2, MAX_UNITS * WIN, D_MODEL), BF16), pltpu.VMEM((MAX_UNITS * WIN, ROWS), F32),
                            pltpu.VMEM((ROWS, D_MODEL), F32), pltpu.SemaphoreType.DMA((2,))]),
        compiler_params=_params(1), name="combine",
    )(s_tab, y, posm, x1, mod)


def _rope_tables(n_tok):
    rows = n_tok // GRID_W
    row = np.repeat(np.arange(rows, dtype=np.float64), GRID_W)
    col = np.tile(np.arange(GRID_W, dtype=np.float64), rows)
    axis_dim = QK_ROPE // 2
    inv_freq = ROPE_THETA ** (-np.arange(0, axis_dim, 2, dtype=np.float64) / axis_dim)
    ang = np.concatenate([row[:, None] * inv_freq, col[:, None] * inv_freq], axis=-1)
    cos = np.ones((n_tok, LANES))
    sin = np.zeros((n_tok, LANES))
    cos[:, ROPE_LANE0:ROPE_LANE0 + QK_ROPE] = np.repeat(np.cos(ang), 2, axis=1)
    sgn = np.tile(np.array([-1.0, 1.0]), QK_ROPE // 2)
    sin[:, ROPE_LANE0:ROPE_LANE0 + QK_ROPE] = np.repeat(np.sin(ang), 2, axis=1) * sgn
    return jnp.asarray(cos, F32), jnp.asarray(sin, F32)


def _dft_tables(n_tok):
    def cs(n):
        k = np.arange(n)
        ang = 2.0 * np.pi * ((k[:, None] * k[None, :]) % n) / n
        return np.cos(ang), np.sin(ang)
    cc, sc = cs(FNET_CH)
    cn, sn = cs(n_tok)
    return tuple(jnp.asarray(t, F32) for t in (cc, sc, np.concatenate([cn, -sn], axis=1)))


def _pad_heads(w, width):
    lead = w.shape[:-1]
    w = w.reshape(lead + (N_HEADS, width))
    w = jnp.pad(w, [(0, 0)] * len(lead) + [(0, 0), (0, HEAD_PAD - width)])
    return w.reshape(lead + (N_HEADS * HEAD_PAD,))


def _pair_swap_lanes(n_groups):
    perm = np.arange(n_groups * LANES).reshape(n_groups, LANES)
    rot = perm[:, ROPE_LANE0:ROPE_LANE0 + QK_ROPE].reshape(n_groups, QK_ROPE // 2, 2)[:, :, ::-1]
    perm[:, ROPE_LANE0:ROPE_LANE0 + QK_ROPE] = rot.reshape(n_groups, QK_ROPE)
    return perm.reshape(-1)


def _rotary_only(n_groups):
    lane = np.arange(n_groups * LANES) % LANES
    return jnp.asarray((lane >= ROPE_LANE0) & (lane < ROPE_LANE0 + QK_ROPE), F32)


def _head_tables(gain, rope_tabs, n_tok):
    if rope_tabs is None:
        return (jnp.broadcast_to(gain, (n_tok, LANES)),)
    cos, sin = rope_tabs
    return gain * cos, gain[:, _pair_swap_lanes(1)] * sin


def _layer0_weights(norm1, norm2, w_in, q_a_norm, w_q_up, q_norm, kv_a_norm, w_kv_up, k_norm, conv_w, w_o):
    c0 = Q_LORA + KV_LORA
    rope_cols = jnp.pad(w_in[:, c0:c0 + QK_ROPE], ((0, 0), (ROPE_LANE0, LANES - ROPE_LANE0 - QK_ROPE)))
    w_in_pad = jnp.concatenate([w_in[:, :c0], rope_cols, rope_cols[:, _pair_swap_lanes(1)],
                                w_in[:, c0 + QK_ROPE:]], axis=1)
    kv = w_kv_up.reshape(KV_LORA, N_HEADS, QK_NOPE + V_HEAD)
    w_q = _pad_heads(w_q_up, QK_DIM)
    head_gain = lambda g: jnp.pad(g, (0, HEAD_PAD - QK_DIM)).reshape(1, -1)
    return dict(
        norm1=norm1.reshape(1, -1), norm2=norm2.reshape(1, -1), w_in=w_in_pad.astype(BF16),
        q_a_norm=q_a_norm.reshape(1, -1), w_q=w_q.astype(BF16),
        w_q_swap=(w_q[:, _pair_swap_lanes(N_HEADS)] * _rotary_only(N_HEADS)).astype(BF16),
        q_gain=head_gain(q_norm) * (QK_DIM ** -0.5), kv_a_norm=kv_a_norm.reshape(1, -1),
        w_k=_pad_heads(kv[:, :, :QK_NOPE].reshape(KV_LORA, -1), QK_NOPE).astype(BF16),
        w_v=kv[:, :, QK_NOPE:].reshape(KV_LORA, -1).astype(BF16),
        w_v_wide=_pad_heads(kv[:, :, QK_NOPE:].reshape(KV_LORA, -1), V_HEAD).astype(BF16), k_gain=head_gain(k_norm),
        conv_w=conv_w, w_o=w_o.astype(BF16))


def _router_weights(w_router):
    hi, lo = _split_hi_lo(jnp.pad(w_router, ((0, 0), (0, LANES - N_EXPERTS))))
    return dict(wr_hi=hi, wr_lo=lo)


def _moe(sets, tri, w):
    routed = []
    for x1, h2, aff, mod, n_tok in sets:
        cap = CAPACITY_FACTOR * x1.shape[0] // N_EXPERTS
        posm, s_tab = _route(aff, tri, cap)
        routed.append((posm, s_tab, _dispatch(h2, posm, aff, s_tab, cap)))
    ys = _ffn([r[2] for r in routed], [r[1] for r in routed], w["wg"], w["wu"], w["wd"])
    return [_combine(y, posm, s_tab, x1, mod, n_tok)
            for y, (posm, s_tab, _), (x1, _, _, mod, n_tok) in zip(ys, routed, sets)]


def _mixer0(x, mod, l0, ctx, rope_tabs):
    n_b, n_tok, _ = x.shape
    n_keys = n_tok + (0 if ctx is None else ctx[0].shape[1])
    w_v = l0["w_v_wide"] if n_keys > WIDE_ROWS else l0["w_v"]
    q, k, v, conv, ckv, kr = _front0(x, mod, l0, rope_tabs, w_v)
    if ctx is not None:
        kc, vc = _ctx_kv(ctx[0], ctx[1], l0, w_v)
        attn = _attention(q, k, v, kc, vc)
    else:
        attn = _attention(q, k, v)
    flat = lambda a: a.reshape(n_b * n_tok, a.shape[-1])
    return _post0(flat(x), flat(attn), flat(conv), mod, l0, n_tok), (ckv, kr[:, :, ROPE_LANE0:ROPE_LANE0 + QK_ROPE])


def kernel(x_prompt, x_sample, c, cache_c_kv_l0, cache_k_rope_l0, c_ctx, norm1_l0, norm2_l0, w_mod_l0, b_mod_l0, w_in_l0, q_a_norm_l0, w_q_up_l0, q_norm_l0, kv_a_norm_l0, w_kv_up_l0, k_norm_l0, conv_w_l0, w_o_l0, w_router_l0, w_gate_l0, w_up_l0, w_down_l0, norm1_l1, norm2_l1, w_mod_l1, b_mod_l1, w_f_l1, w_router_l1, w_gate_l1, w_up_l1, w_down_l1):
    n_dec = c.shape[0]
    cond = jnp.concatenate([c_ctx[None, :], c, jnp.zeros((16 - 1 - n_dec, D_MODEL), F32)], axis=0)
    m0 = _modulation(cond, w_mod_l0, b_mod_l0)
    m1 = _modulation(cond, w_mod_l1, b_mod_l1)
    mods_prompt = (m0[0:1], m1[0:1])
    mods_sample = (m0[1:1 + n_dec], m1[1:1 + n_dec])

    l0 = _layer0_weights(norm1_l0, norm2_l0, w_in_l0, q_a_norm_l0, w_q_up_l0, q_norm_l0, kv_a_norm_l0,
                         w_kv_up_l0, k_norm_l0, conv_w_l0, w_o_l0)
    l0.update(_router_weights(w_router_l0))
    l0.update(wg=w_gate_l0, wu=w_up_l0, wd=w_down_l0)
    l1 = dict(norm1=norm1_l1.reshape(1, -1), norm2=norm2_l1.reshape(1, -1), w_f=w_f_l1.astype(BF16))
    l1.update(_router_weights(w_router_l1))
    l1.update(wg=w_gate_l1, wu=w_up_l1, wd=w_down_l1)

    tri = jnp.asarray(np.triu(np.ones((ROWS, ROWS)), 1), BF16)
    k_rope_pad = jnp.pad(cache_k_rope_l0, ((0, 0), (0, 0), (ROPE_LANE0, LANES - ROPE_LANE0 - QK_ROPE)))
    xs = (x_prompt, x_sample)
    mods = (mods_prompt, mods_sample)
    n_toks = tuple(x.shape[1] for x in xs)

    (front_p, (new_c_kv, new_k_rope)) = _mixer0(x_prompt, mods_prompt[0], l0, None, None)
    (front_s, _) = _mixer0(x_sample, mods_sample[0], l0, (cache_c_kv_l0, k_rope_pad), _rope_tables(n_toks[1]))
    ys = _moe([tuple(front) + (mod[0], n_tok) for front, mod, n_tok in zip((front_p, front_s), mods, n_toks)],
              tri, l0)

    sets = []
    for y, x, mod, n_tok in zip(ys, xs, mods, n_toks):
        x1, h2, aff = _fnet(y.reshape(x.shape), mod[1], l1, _dft_tables(n_tok))
        sets.append((x1.reshape(-1, D_MODEL), h2.reshape(-1, D_MODEL), aff, mod[1], n_tok))
    y_prompt, y_sample = (y.reshape(x.shape) for y, x in zip(_moe(sets, tri, l1), xs))
    return (y_prompt, y_sample, new_c_kv, new_k_rope)
```

```python
import functools

import jax
import jax.numpy as jnp
import numpy as np
from jax import lax
from jax.experimental import pallas as pl
from jax.experimental.pallas import tpu as pltpu

D_MODEL = 1024
GRID_W = 64
N_HEADS = 8
QK_NOPE = 64
QK_ROPE = 32
QK_DIM = QK_NOPE + QK_ROPE
V_HEAD = 64
Q_LORA = 384
KV_LORA = 256
CONV_CH = 512
FNET_GROUPS = 4
FNET_CH = D_MODEL // FNET_GROUPS
N_EXPERTS = 16
EXPERT_FF = 512
CAPACITY_FACTOR = 2
ROPE_THETA = 10000.0
EPS = 1e-6

LANES = 128
HEAD_PAD = LANES
ROWS = 256
WIDE_ROWS = 512
ALIGN = 8
WIN = 64
MAX_UNITS = N_EXPERTS * (ROWS // WIN)
COMMON_ROWS = N_EXPERTS * WIN
XE_W = D_MODEL + LANES
IN0_PAD = Q_LORA + KV_LORA + 2 * LANES + 3 * CONV_CH
ROPE_LANE0 = QK_NOPE
VMEM_LIMIT = 56 * 1024 * 1024

F32 = jnp.float32
BF16 = jnp.bfloat16


def _dot(a, b):
    return jnp.dot(a, b, preferred_element_type=F32)


def _dot_nt(a, b):
    return lax.dot_general(a, b, (((1,), (1,)), ((), ())), preferred_element_type=F32)


def _split_hi_lo(x):
    hi = x.astype(BF16)
    lo = (x - hi.astype(F32)).astype(BF16)
    return hi, lo


def _params(n_axes):
    return pltpu.CompilerParams(dimension_semantics=("arbitrary",) * n_axes,
                                vmem_limit_bytes=VMEM_LIMIT)


def _rms(x, gain):
    return x * lax.rsqrt(jnp.mean(x * x, axis=-1, keepdims=True) + EPS) * gain


def _mod_kernel(cond_ref, w_ref, b_ref, o_ref):
    c = cond_ref[...]
    s = c * (1.0 / (1.0 + jnp.exp(-c)))
    s_hi, s_lo = _split_hi_lo(s)
    w = w_ref[...].astype(BF16)
    o_ref[...] = _dot(s_hi, w) + _dot(s_lo, w) + b_ref[...]


def _modulation(cond, w_mod, b_mod):
    n_rows = cond.shape[0]
    tn = 1536
    out = pl.pallas_call(
        _mod_kernel,
        out_shape=jax.ShapeDtypeStruct((n_rows, 6 * D_MODEL), F32),
        grid=(6 * D_MODEL // tn,),
        in_specs=[pl.BlockSpec((n_rows, D_MODEL), lambda i: (0, 0)),
                  pl.BlockSpec((D_MODEL, tn), lambda i: (0, i)),
                  pl.BlockSpec((1, tn), lambda i: (0, i))],
        out_specs=pl.BlockSpec((n_rows, tn), lambda i: (0, i)),
        compiler_params=_params(1),
        name="modulation",
    )(cond, w_mod, b_mod.reshape(1, -1))
    return out.reshape(n_rows, 6, D_MODEL)


def _head_norm_rope(xh, gain_cos, swapped_sin):
    lane = lax.broadcasted_iota(jnp.int32, (1, HEAD_PAD), 1)
    ss = jnp.sum(jnp.where(lane < QK_DIM, xh * xh, 0.0), axis=-1, keepdims=True) * (1.0 / QK_DIM)
    y = xh * gain_cos
    if swapped_sin is not None:
        y = y + swapped_sin
    return y * lax.rsqrt(ss + EPS)


def _expand_kv(ckv_bf, r, swapped_sin, wk_ref, wv_ref, gain_cos, k_ref, v_ref, rows):
    kf = _dot(ckv_bf, wk_ref[...])
    vf = _dot(ckv_bf, wv_ref[...])
    if v_ref.shape[-1] == N_HEADS * HEAD_PAD:
        lane = lax.broadcasted_iota(jnp.int32, (1, N_HEADS * HEAD_PAD), 1)
        vf = vf + jnp.where((lane & (HEAD_PAD - 1)) == V_HEAD, 1.0, 0.0)
    v_ref[rows, :] = vf.astype(BF16)
    for h in range(N_HEADS):
        sl = slice(h * HEAD_PAD, (h + 1) * HEAD_PAD)
        k_ref[rows, sl] = _head_norm_rope(kf[:, sl] + r, gain_cos, swapped_sin).astype(BF16)


def _front0_kernel(*refs, n_tok, rope):
    (x_ref, mod_ref, n1_ref, win_ref, qan_ref, wq_ref, kvan_ref, wk_ref, wv_ref, cw_ref,
     qa_ref, ka_ref) = refs[:12]
    pos = 12
    if rope:
        kb_ref = refs[pos]
        pos += 1
    q_ref, k_ref, v_ref, conv_ref, ckv_ref, kr_ref, cu_s, gb_s = refs[pos:]
    sh1, sc1 = mod_ref[0:1, :], mod_ref[1:2, :]
    c_rope = Q_LORA + KV_LORA
    wide = min(n_tok, WIDE_ROWS)
    for c in range(n_tok // ROWS):
        rows = pl.ds(c * ROWS, ROWS)
        if (c * ROWS) % wide == 0:
            wide_rows = pl.ds(c * ROWS, wide)
            h = _rms(x_ref[wide_rows, :], n1_ref[...]) * (1.0 + sc1) + sh1
            proj_wide = _dot(h.astype(BF16), win_ref[...])
        off = (c * ROWS) % wide
        proj = proj_wide[off:off + ROWS, :]
        cq = _rms(proj[:, :Q_LORA], qan_ref[...]).astype(BF16)
        qf = _dot(cq, wq_ref[...])
        for hd in range(N_HEADS):
            sl = slice(hd * HEAD_PAD, (hd + 1) * HEAD_PAD)
            q_ref[rows, sl] = _head_norm_rope(qf[:, sl], qa_ref[rows, :], None).astype(BF16)
        ckv = _rms(proj[:, Q_LORA:c_rope], kvan_ref[...])
        ckv_ref[rows, :] = ckv
        r = proj[:, c_rope:c_rope + LANES]
        kr_ref[rows, :] = r
        k_sin = proj[:, c_rope + LANES:c_rope + 2 * LANES] * kb_ref[rows, :] if rope else None
        _expand_kv(ckv.astype(BF16), r, k_sin, wk_ref, wv_ref, ka_ref[rows, :], k_ref, v_ref, rows)
        c0 = c_rope + 2 * LANES
        gb_s[rows, :] = proj[:, c0:c0 + CONV_CH]
        cu_s[rows, :] = proj[:, c0 + CONV_CH:c0 + 2 * CONV_CH] * proj[:, c0 + 2 * CONV_CH:c0 + 3 * CONV_CH]
    cu = cu_s[...]
    row = lax.broadcasted_iota(jnp.int32, cu.shape, 0)
    prev = jnp.where(row == 0, 0.0, pltpu.roll(cu, 1, 0))
    nxt = jnp.where(row == n_tok - 1, 0.0, pltpu.roll(cu, n_tok - 1, 0))
    conv = gb_s[...] * (cw_ref[0:1, :] * prev + cw_ref[1:2, :] * cu + cw_ref[2:3, :] * nxt)
    conv_ref[...] = conv.astype(BF16)


def _const_spec(shape):
    return pl.BlockSpec(shape, lambda b: (0,) * len(shape))


def _front0(x, mod, w, rope_tabs, w_v):
    n_b, n_tok, _ = x.shape
    v_width = w_v.shape[1]
    shared_mod = mod.shape[0] == 1
    seq = lambda width: pl.BlockSpec((None, n_tok, width), lambda b: (b, 0, 0))
    in_specs = [seq(D_MODEL),
                pl.BlockSpec((None, 6, D_MODEL), (lambda b: (0, 0, 0)) if shared_mod else (lambda b: (b, 0, 0))),
                _const_spec((1, D_MODEL)), _const_spec((D_MODEL, IN0_PAD)), _const_spec((1, Q_LORA)),
                _const_spec((Q_LORA, N_HEADS * HEAD_PAD)),
                _const_spec((1, KV_LORA)), _const_spec((KV_LORA, N_HEADS * HEAD_PAD)),
                _const_spec((KV_LORA, v_width)), _const_spec((3, CONV_CH)),
                _const_spec((n_tok, LANES)), _const_spec((n_tok, LANES))]
    q_tabs = _head_tables(w["q_gain"], rope_tabs, n_tok, True)
    k_tabs = _head_tables(w["k_gain"], rope_tabs, n_tok, False)
    args = [x, mod, w["norm1"], w["w_in"], w["q_a_norm"], w["w_q"], w["kv_a_norm"],
            w["w_k"], w_v, w["conv_w"], q_tabs[0], k_tabs[0]]
    if rope_tabs is not None:
        in_specs += [_const_spec((n_tok, LANES))]
        args += [k_tabs[1]]
    out_shape = [jax.ShapeDtypeStruct((n_b, n_tok, N_HEADS * HEAD_PAD), BF16),
                 jax.ShapeDtypeStruct((n_b, n_tok, N_HEADS * HEAD_PAD), BF16),
                 jax.ShapeDtypeStruct((n_b, n_tok, v_width), BF16),
                 jax.ShapeDtypeStruct((n_b, n_tok, CONV_CH), BF16),
                 jax.ShapeDtypeStruct((n_b, n_tok, KV_LORA), F32),
                 jax.ShapeDtypeStruct((n_b, n_tok, LANES), F32)]
    out_specs = [seq(N_HEADS * HEAD_PAD), seq(N_HEADS * HEAD_PAD), seq(v_width), seq(CONV_CH),
                 seq(KV_LORA), seq(LANES)]
    return pl.pallas_call(
        functools.partial(_front0_kernel, n_tok=n_tok, rope=rope_tabs is not None),
        out_shape=out_shape, grid=(n_b,), in_specs=in_specs, out_specs=out_specs,
        scratch_shapes=[pltpu.VMEM((n_tok, CONV_CH), F32), pltpu.VMEM((n_tok, CONV_CH), F32)],
        compiler_params=_params(1), name="front0",
    )(*args)


def _ctx_kv_kernel(ckv_ref, r_ref, wk_ref, wv_ref, ka_ref, k_ref, v_ref):
    rows = pl.ds(0, ckv_ref.shape[0])
    _expand_kv(ckv_ref[...].astype(BF16), r_ref[...], None, wk_ref, wv_ref, ka_ref[...], k_ref, v_ref, rows)


def _ctx_kv(cache_c_kv, cache_k_rope_pad, w, w_v):
    n_b, n_ctx, _ = cache_c_kv.shape
    v_width = w_v.shape[1]
    seq = lambda width: pl.BlockSpec((None, n_ctx, width), lambda b: (b, 0, 0))
    return pl.pallas_call(
        _ctx_kv_kernel,
        out_shape=[jax.ShapeDtypeStruct((n_b, n_ctx, N_HEADS * HEAD_PAD), BF16),
                   jax.ShapeDtypeStruct((n_b, n_ctx, v_width), BF16)],
        grid=(n_b,),
        in_specs=[seq(KV_LORA), seq(LANES), _const_spec((KV_LORA, N_HEADS * HEAD_PAD)),
                  _const_spec((KV_LORA, v_width)), _const_spec((n_ctx, LANES))],
        out_specs=[seq(N_HEADS * HEAD_PAD), seq(v_width)],
        compiler_params=_params(1), name="ctx_kv",
    )(cache_c_kv, cache_k_rope_pad, w["w_k"], w_v, _head_tables(w["k_gain"], None, n_ctx, False)[0])


def _attn_kernel(*refs, with_ctx):
    if with_ctx:
        q_ref, k_ref, v_ref, kc_ref, vc_ref, o_ref = refs
    else:
        q_ref, k_ref, v_ref, o_ref = refs
    ones_lane = v_ref.shape[-1] == N_HEADS * HEAD_PAD
    lane = lax.broadcasted_iota(jnp.int32, (q_ref.shape[0], LANES), 1)
    for pair in range(N_HEADS // 2):
        outs = []
        for hd in (2 * pair, 2 * pair + 1):
            sl = slice(hd * HEAD_PAD, (hd + 1) * HEAD_PAD)
            vsl = sl if ones_lane else slice(pair * LANES, (pair + 1) * LANES)
            qh = q_ref[:, sl]
            s = _dot_nt(qh, k_ref[:, sl])
            m = jnp.max(s, axis=-1, keepdims=True)
            if with_ctx:
                sc = _dot_nt(qh, kc_ref[:, sl])
                m = jnp.maximum(m, jnp.max(sc, axis=-1, keepdims=True))
            p = jnp.exp(s - m)
            o = _dot(p.astype(BF16), v_ref[:, vsl])
            if with_ctx:
                pc = jnp.exp(sc - m)
                o = o + _dot(pc.astype(BF16), vc_ref[:, vsl])
            if ones_lane:
                den = o[:, V_HEAD:V_HEAD + 1]
            else:
                den = jnp.sum(p, axis=-1, keepdims=True)
                if with_ctx:
                    den = den + jnp.sum(pc, axis=-1, keepdims=True)
            outs.append(o / den)
        odd = pltpu.roll(outs[1], V_HEAD, 1) if ones_lane else outs[1]
        o_ref[:, pair * LANES:(pair + 1) * LANES] = jnp.where(lane < V_HEAD, outs[0], odd).astype(BF16)


def _attention(q, k, v, kc=None, vc=None):
    n_b, n_tok, _ = q.shape
    with_ctx = kc is not None
    tq = min(n_tok, WIDE_ROWS)
    qspec = lambda width: pl.BlockSpec((None, tq, width), lambda b, i: (b, i, 0))
    kvspec = lambda a: pl.BlockSpec((None,) + a.shape[1:], lambda b, i: (b, 0, 0))
    in_specs = [qspec(N_HEADS * HEAD_PAD), kvspec(k), kvspec(v)]
    args = [q, k, v]
    if with_ctx:
        in_specs += [kvspec(kc), kvspec(vc)]
        args += [kc, vc]
    return pl.pallas_call(
        functools.partial(_attn_kernel, with_ctx=with_ctx),
        out_shape=jax.ShapeDtypeStruct((n_b, n_tok, N_HEADS * V_HEAD), BF16),
        grid=(n_b, n_tok // tq), in_specs=in_specs, out_specs=qspec(N_HEADS * V_HEAD),
        compiler_params=_params(2), name="attention",
    )(*args)


def _moe_front(x1, mod_ref, n2_ref, wrh_ref, wrl_ref, h2_ref, aff_ref, rows):
    sh2, sc2 = mod_ref[3:4, :], mod_ref[4:5, :]
    h2 = _rms(x1, n2_ref[...]) * (1.0 + sc2) + sh2
    h_hi, h_lo = _split_hi_lo(h2)
    logit = _dot(h_hi, wrh_ref[...]) + _dot(h_lo, wrh_ref[...]) + _dot(h_hi, wrl_ref[...])
    real = lax.broadcasted_iota(jnp.int32, (1, LANES), 1) < N_EXPERTS
    top = jnp.max(jnp.where(real, logit, -jnp.inf), axis=1, keepdims=True)
    e = jnp.where(real, jnp.exp(logit - top), 0.0)
    a = e / jnp.sum(e, axis=1, keepdims=True)
    a_hi = a.astype(BF16).astype(F32)
    a_mid = (a - a_hi).astype(BF16).astype(F32)
    a_lo = a - a_hi - a_mid
    h2_ref[rows, 0:D_MODEL] = h_hi
    h2_ref[rows, D_MODEL:XE_W] = (a_hi + pltpu.roll(a_mid, N_EXPERTS, 1)
                                  + pltpu.roll(a_lo, 2 * N_EXPERTS, 1)).astype(BF16)
    aff_ref[...] = a.T[0:N_EXPERTS, :]


def _post0_kernel(x_ref, attn_ref, conv_ref, wo_ref, mod_ref, n2_ref, wrh_ref, wrl_ref,
                  x1_ref, h2_ref, aff_ref):
    n_attn = N_HEADS * V_HEAD
    mix = _dot(attn_ref[...], wo_ref[0:n_attn, :]) + _dot(conv_ref[...], wo_ref[n_attn:, :])
    for t in range(x_ref.shape[0] // ROWS):
        rows = pl.ds(t * ROWS, ROWS)
        x1 = x_ref[rows, :] + mod_ref[2:3, :] * mix[t * ROWS:(t + 1) * ROWS, :]
        x1_ref[rows, :] = x1
        _moe_front(x1, mod_ref, n2_ref, wrh_ref, wrl_ref, h2_ref, aff_ref.at[t], rows)


def _post0(x, attn, conv, mod, w, n_tok):
    n_all = x.shape[0]
    rows = min(n_tok, WIDE_ROWS)
    tiles_per_seq = n_tok // rows
    shared_mod = mod.shape[0] == 1
    tile = lambda width: pl.BlockSpec((rows, width), lambda i: (i, 0))
    const = lambda shape: pl.BlockSpec(shape, lambda i: (0,) * len(shape))
    mod_map = (lambda i: (0, 0, 0)) if shared_mod else (lambda i: (i // tiles_per_seq, 0, 0))
    return pl.pallas_call(
        _post0_kernel,
        out_shape=[jax.ShapeDtypeStruct((n_all, D_MODEL), F32),
                   jax.ShapeDtypeStruct((n_all, XE_W), BF16),
                   jax.ShapeDtypeStruct((n_all // ROWS, N_EXPERTS, ROWS), F32)],
        grid=(n_all // rows,),
        in_specs=[tile(D_MODEL), tile(N_HEADS * V_HEAD), tile(CONV_CH), const((D_MODEL, D_MODEL)),
                  pl.BlockSpec((None, 6, D_MODEL), mod_map), const((1, D_MODEL)),
                  const((D_MODEL, LANES)), const((D_MODEL, LANES))],
        out_specs=[tile(D_MODEL), tile(XE_W),
                   pl.BlockSpec((rows // ROWS, N_EXPERTS, ROWS), lambda i: (i, 0, 0))],
        compiler_params=_params(1), name="post0",
    )(x, attn, conv, w["w_o"], mod, w["norm2"], w["wr_hi"], w["wr_lo"])


def _fnet_kernel(x_ref, mod_ref, n1_ref, cc_ref, sc_ref, dft_ref, wf_ref, n2_ref, wrh_ref, wrl_ref,
                 x1_ref, h2_ref, aff_ref, y_s, *, n_tok):
    sh1, sc1, g1 = mod_ref[0:1, :], mod_ref[1:2, :], mod_ref[2:3, :]
    n_chunks = n_tok // ROWS
    cc, sc = cc_ref[...].astype(BF16), sc_ref[...].astype(BF16)
    for c in range(n_chunks):
        rows = pl.ds(c * ROWS, ROWS)
        h = (_rms(x_ref[rows, :], n1_ref[...]) * (1.0 + sc1) + sh1).astype(BF16)
        for g in range(FNET_GROUPS):
            sl = slice(g * FNET_CH, (g + 1) * FNET_CH)
            y_s[pl.ds(c * ROWS, ROWS), sl] = _dot(h[:, sl], cc).astype(BF16)
            y_s[pl.ds(n_tok + c * ROWS, ROWS), sl] = _dot(h[:, sl], sc).astype(BF16)
    scale = 1.0 / float(np.sqrt(n_tok * FNET_CH))
    wide = min(n_tok, WIDE_ROWS)
    for c in range(n_chunks):
        rows = pl.ds(c * ROWS, ROWS)
        if (c * ROWS) % wide == 0:
            f = _dot(dft_ref[pl.ds(c * ROWS, wide), :].astype(BF16), y_s[...]) * scale
            mix = _dot(f.astype(BF16), wf_ref[...])
        off = (c * ROWS) % wide
        x1 = x_ref[rows, :] + g1 * mix[off:off + ROWS, :]
        x1_ref[rows, :] = x1
        _moe_front(x1, mod_ref, n2_ref, wrh_ref, wrl_ref, h2_ref, aff_ref.at[c], rows)


def _fnet(x, mod, w, dft):
    n_b, n_tok, _ = x.shape
    shared_mod = mod.shape[0] == 1
    seq = lambda width: pl.BlockSpec((None, n_tok, width), lambda b: (b, 0, 0))
    cc, sc, dft_n = dft
    tiles = n_tok // ROWS
    return pl.pallas_call(
        functools.partial(_fnet_kernel, n_tok=n_tok),
        out_shape=[jax.ShapeDtypeStruct((n_b, n_tok, D_MODEL), F32),
                   jax.ShapeDtypeStruct((n_b, n_tok, XE_W), BF16),
                   jax.ShapeDtypeStruct((n_b * tiles, N_EXPERTS, ROWS), F32)],
        grid=(n_b,),
        in_specs=[seq(D_MODEL),
                  pl.BlockSpec((None, 6, D_MODEL), (lambda b: (0, 0, 0)) if shared_mod else (lambda b: (b, 0, 0))),
                  _const_spec((1, D_MODEL)), _const_spec((FNET_CH, FNET_CH)), _const_spec((FNET_CH, FNET_CH)),
                  pl.BlockSpec((n_tok, 2 * n_tok), lambda b: (0, 0), pipeline_mode=pl.Buffered(1)),
                  _const_spec((D_MODEL, D_MODEL)), _const_spec((1, D_MODEL)),
                  _const_spec((D_MODEL, LANES)), _const_spec((D_MODEL, LANES))],
        out_specs=[seq(D_MODEL), seq(XE_W), pl.BlockSpec((tiles, N_EXPERTS, ROWS), lambda b: (b, 0, 0))],
        scratch_shapes=[pltpu.VMEM((2 * n_tok, D_MODEL), BF16)],
        compiler_params=_params(1), name="fnet",
    )(x, mod, w["norm1"], cc, sc, dft_n, w["w_f"], w["norm2"], w["wr_hi"], w["wr_lo"])


def _route_kernel(aff_ref, tri_ref, posm_ref, s_ref, *, n_blk, cap):
    def count(pred):
        acc = jnp.zeros((N_EXPERTS, ROWS), F32)
        for b in range(n_blk):
            acc = acc + jnp.where(pred(aff_ref[b]), 1.0, 0.0)
        return jnp.sum(acc, axis=1, keepdims=True)

    def as_f32(bits):
        return pltpu.bitcast(bits, F32)

    def search(_, carry):
        lo, hi = carry
        mid = lo + ((hi - lo) >> 1)
        mid_f = as_f32(mid)
        ok = count(lambda a: a >= mid_f) >= cap
        return jnp.where(ok, mid, lo), jnp.where(ok, hi, mid)

    one_bits = 0x3F800000
    lo0 = jnp.zeros((N_EXPERTS, 1), jnp.int32)
    hi0 = jnp.full((N_EXPERTS, 1), one_bits + 1, jnp.int32)
    lo, hi = lax.fori_loop(0, 31, search, (lo0, hi0))
    lo_f, ub = as_f32(lo), as_f32(hi)
    thr = lo_f
    pending = jnp.ones((N_EXPERTS, 1), F32)
    for _ in range(3):
        cur = jnp.full((N_EXPERTS, ROWS), -1.0, F32)
        for b in range(n_blk):
            a = aff_ref[b]
            cur = jnp.maximum(cur, jnp.where((a >= lo_f) & (a < ub), a, -1.0))
        cur = jnp.max(cur, axis=1, keepdims=True)
        take = (count(lambda a: a >= cur) >= cap) & (pending > 0.0)
        thr = jnp.where(take, cur, thr)
        pending = jnp.where(take, 0.0, pending)
        ub = cur
    need = cap - count(lambda a: a > thr)

    lane = lax.broadcasted_iota(jnp.int32, (N_EXPERTS, LANES), 1)
    carry_tie = jnp.zeros((N_EXPERTS, 1), F32)
    carry_pos = jnp.zeros((N_EXPERTS, 1), F32)
    s_acc = jnp.zeros((N_EXPERTS, LANES), jnp.int32)
    for b in range(n_blk):
        v = aff_ref[b]
        eq = jnp.where(v == thr, 1.0, 0.0)
        tie_rank = _dot(eq.astype(BF16), tri_ref[...]) + carry_tie
        sel = jnp.where((v > thr) | ((v == thr) & (tie_rank < need)), 1.0, 0.0)
        pos = _dot(sel.astype(BF16), tri_ref[...]) + carry_pos
        posm_ref[b] = jnp.where(sel > 0.0, pos, -1.0)
        s_acc = jnp.where(lane == b, carry_pos.astype(jnp.int32), s_acc)
        carry_tie = carry_tie + jnp.sum(eq, axis=1, keepdims=True)
        n_sel = jnp.sum(sel, axis=1, keepdims=True)
        carry_pos = carry_pos + jnp.floor((n_sel + (ALIGN - 1)) * (1.0 / ALIGN)) * ALIGN
    s_ref[...] = jnp.where(lane == n_blk, carry_pos.astype(jnp.int32), s_acc)


def _route(aff, tri, cap):
    n_blk = aff.shape[0]
    assert n_blk < LANES
    full = lambda shape: pl.BlockSpec(shape, lambda i: (0,) * len(shape))
    posm, s_tab = pl.pallas_call(
        functools.partial(_route_kernel, n_blk=n_blk, cap=cap),
        out_shape=[jax.ShapeDtypeStruct((n_blk, N_EXPERTS, ROWS), F32),
                   jax.ShapeDtypeStruct((N_EXPERTS, LANES), jnp.int32)],
        grid=(1,),
        in_specs=[full((n_blk, N_EXPERTS, ROWS)), full((ROWS, ROWS))],
        out_specs=[full((n_blk, N_EXPERTS, ROWS)), full((N_EXPERTS, LANES))],
        compiler_params=_params(1), name="route",
    )(aff, tri)
    return posm, s_tab[:, :n_blk + 1]


def _ceil_div_pow2(x, d):
    return lax.shift_right_logical(x + (d - 1), d.bit_length() - 1)


def _block_units(s_ref, b):
    base = [s_ref[e, b] for e in range(N_EXPERTS)]
    units = [_ceil_div_pow2(s_ref[e, b + 1] - base[e], WIN) for e in range(N_EXPERTS)]
    first_unit, total = [], 0
    for e in range(N_EXPERTS):
        first_unit.append(total)
        total = total + units[e]
    return base, units, first_unit, total


def _for_units(units, fn):
    for e in range(N_EXPERTS):
        def body(u, _, e=e):
            fn(e, u)
            return 0
        lax.fori_loop(0, units[e], body, 0)


def _window_onehot(posm_ref, e, first, lo):
    slot = lax.broadcasted_iota(jnp.int32, (WIN, ROWS), 0) + first
    p = posm_ref[e:e + 1, :].astype(jnp.int32)
    return jnp.where((p == slot) & (p >= lo), 1.0, 0.0)


def _gate_lanes(e):
    lane = lax.broadcasted_iota(jnp.int32, (1, LANES), 1)
    return ((lane & (N_EXPERTS - 1)) == e) & (lane < 3 * N_EXPERTS)


def _dispatch_kernel(s_ref, h_ref, posm_ref, xe_ref, sel_s, x_s, sem, *, n_blk, cap_pad):
    b = pl.program_id(0)
    buf = b % 2
    base, units, first_unit, n_units = _block_units(s_ref, b)

    @pl.when(b == 0)
    def _():
        sel_s[...] = jnp.zeros_like(sel_s)

    def unit_rows(first_unit_e, u):
        return pl.ds(pl.multiple_of((first_unit_e + u) * WIN, WIN), WIN)

    def pick(e, u):
        first = base[e] + u * WIN
        sel_s[unit_rows(first_unit[e], u), :] = _window_onehot(posm_ref, e, first, first).astype(BF16)

    _for_units(units, pick)

    def gather(rows):
        x_s[buf, rows, :] = _dot(sel_s[rows, :], h_ref[...]).astype(BF16)

    def gather_chunk(c, _):
        gather(pl.ds(pl.multiple_of(c * ROWS, ROWS), ROWS))
        return 0

    gather(pl.ds(0, COMMON_ROWS))
    lax.fori_loop(COMMON_ROWS // ROWS, _ceil_div_pow2(n_units * WIN, ROWS), gather_chunk, 0)

    def unit_copy(slot, blk_base_e, blk_first_e, e, u):
        dst = pl.ds(pl.multiple_of(blk_base_e + u * WIN, ALIGN), WIN)
        return pltpu.make_async_copy(x_s.at[slot, unit_rows(blk_first_e, u), :], xe_ref.at[e, dst, :],
                                     sem.at[slot, e])

    def for_expert_units(n, fn):
        lax.fori_loop(0, n, lambda u, _: (fn(u), 0)[1], 0)

    prev_base, prev_units, prev_first, _ = _block_units(s_ref, jnp.maximum(b - 1, 0))
    for e in range(N_EXPERTS):
        for_expert_units(jnp.where(b > 0, prev_units[e], 0),
                         lambda u, e=e: unit_copy(1 - buf, prev_base[e], prev_first[e], e, u).wait())
        for_expert_units(units[e], lambda u, e=e: unit_copy(buf, base[e], first_unit[e], e, u).start())

    @pl.when(b == n_blk - 1)
    def _():
        for e in range(N_EXPERTS):
            for_expert_units(units[e], lambda u, e=e: unit_copy(buf, base[e], first_unit[e], e, u).wait())
        x_s[buf, 0:WIN, :] = jnp.zeros((WIN, XE_W), BF16)

        def tail(fn):
            for e in range(N_EXPERTS):
                total = s_ref[e, n_blk]
                n_win = (cap_pad - total) // WIN

                def wide(c, _):
                    row = pl.multiple_of(total + c * WIN, ALIGN)
                    fn(pltpu.make_async_copy(x_s.at[buf, pl.ds(0, WIN), :],
                                             xe_ref.at[e, pl.ds(row, WIN), :], sem.at[buf, e]))
                    return 0

                def narrow(c, _):
                    row = pl.multiple_of(total + n_win * WIN + c * ALIGN, ALIGN)
                    fn(pltpu.make_async_copy(x_s.at[buf, pl.ds(0, ALIGN), :],
                                             xe_ref.at[e, pl.ds(row, ALIGN), :], sem.at[buf, e]))
                    return 0

                lax.fori_loop(0, n_win, wide, 0)
                lax.fori_loop(0, (cap_pad - total - n_win * WIN) // ALIGN, narrow, 0)

        tail(lambda c: c.start())
        tail(lambda c: c.wait())


def _ffn_kernel(*refs, n_blks):
    n_sets = len(n_blks)
    s_refs, x_refs = refs[:n_sets], refs[n_sets:2 * n_sets]
    wg_ref, wu_ref, wd_ref = refs[2 * n_sets:2 * n_sets + 3]
    y_refs = refs[2 * n_sets + 3:3 * n_sets + 3]
    wg_s, wu_s, wd_s = refs[3 * n_sets + 3:]
    e = pl.program_id(0)
    wg_s[...] = wg_ref[...].astype(BF16)
    wu_s[...] = wu_ref[...].astype(BF16)
    wd_s[...] = wd_ref[...].astype(BF16)
    mine = _gate_lanes(e)

    half = ROWS // 2
    for s_ref, x_ref, y_ref, n_blk in zip(s_refs, x_refs, y_refs, n_blks):
        n_half = _ceil_div_pow2(s_ref[e, n_blk], half)

        def tile(rows, x_ref=x_ref, y_ref=y_ref):
            x = x_ref[rows, 0:D_MODEL]
            pieces = x_ref[rows, D_MODEL:XE_W].astype(F32)
            gate = jnp.sum(jnp.where(mine, pieces, 0.0), axis=1, keepdims=True)
            a = _dot(x, wg_s[...])
            u = _dot(x, wu_s[...])
            hid = a * (1.0 / (1.0 + jnp.exp(-a))) * u * gate
            y_ref[rows, :] = _dot(hid.astype(BF16), wd_s[...]).astype(BF16)

        def full(j, _, tile=tile):
            tile(pl.ds(pl.multiple_of(j * ROWS, ROWS), ROWS))
            return 0

        def blank(j, _, y_ref=y_ref):
            y_ref[pl.ds(pl.multiple_of(j * half, half), half), :] = jnp.zeros((half, D_MODEL), BF16)
            return 0

        lax.fori_loop(0, lax.shift_right_logical(n_half, 1), full, 0)

        @pl.when((n_half & 1) == 1)
        def _(tile=tile, n_half=n_half):
            tile(pl.ds(pl.multiple_of((n_half - 1) * half, half), half))

        lax.fori_loop(n_half, y_ref.shape[0] // half, blank, 0)


def _combine_kernel(s_ref, y_ref, posm_ref, x1_ref, mod_ref, o_ref, yw_s, sel_s, acc_s, sem,
                    *, n_blk, cap_pad):
    b = pl.program_id(0)
    buf = b % 2
    base, units, first_unit, n_units = _block_units(s_ref, b)

    def unit_rows(first_unit_e, u):
        return pl.ds(pl.multiple_of((first_unit_e + u) * WIN, WIN), WIN)

    def bounds(base_e, u):
        lo = base_e + u * WIN
        return lo, jnp.minimum(lo, cap_pad - WIN)

    def copies(slot, blk, fn):
        blk_base, blk_units, blk_first, _ = _block_units(s_ref, blk)

        def one(e, u):
            src = pl.ds(pl.multiple_of(bounds(blk_base[e], u)[1], ALIGN), WIN)
            fn(pltpu.make_async_copy(y_ref.at[e, src, :], yw_s.at[slot, unit_rows(blk_first[e], u), :],
                                     sem.at[slot]))

        _for_units(blk_units, one)

    @pl.when(b == 0)
    def _():
        yw_s[...] = jnp.zeros_like(yw_s)
        copies(0, 0, lambda c: c.start())

    @pl.when(b + 1 < n_blk)
    def _():
        copies(1 - buf, b + 1, lambda c: c.start())

    def pick(e, u):
        lo, first = bounds(base[e], u)
        sel_s[unit_rows(first_unit[e], u), :] = _window_onehot(posm_ref, e, first, lo)

    _for_units(units, pick)
    n_chunks = jnp.maximum(_ceil_div_pow2(n_units * WIN, ROWS), COMMON_ROWS // ROWS)

    def blank(u, _):
        sel_s[unit_rows(u, 0), :] = jnp.zeros((WIN, ROWS), F32)
        return 0

    lax.fori_loop(n_units, n_chunks * (ROWS // WIN), blank, 0)
    copies(buf, b, lambda c: c.wait())

    def weighted(rows):
        return _dot(sel_s[rows, :].T.astype(BF16), yw_s[buf, rows, :])

    def add_chunk(c, _):
        acc_s[...] += weighted(pl.ds(pl.multiple_of(c * ROWS, ROWS), ROWS))
        return 0

    acc_s[...] = weighted(pl.ds(0, COMMON_ROWS))
    lax.fori_loop(COMMON_ROWS // ROWS, n_chunks, add_chunk, 0)
    o_ref[...] = x1_ref[...] + mod_ref[5:6, :] * acc_s[...]


def _padded_capacity(cap, n_blk):
    return -(-(cap + (ALIGN - 1) * n_blk + WIN) // ROWS) * ROWS


def _block_spec(*shape):
    return pl.BlockSpec((None,) + shape, lambda b, s: (b,) + (0,) * len(shape))


def _dispatch(h2, posm, s_tab, cap):
    n_blk = h2.shape[0] // ROWS
    cap_pad = _padded_capacity(cap, n_blk)
    return pl.pallas_call(
        functools.partial(_dispatch_kernel, n_blk=n_blk, cap_pad=cap_pad),
        out_shape=jax.ShapeDtypeStruct((N_EXPERTS, cap_pad, XE_W), BF16),
        grid_spec=pltpu.PrefetchScalarGridSpec(
            num_scalar_prefetch=1, grid=(n_blk,),
            in_specs=[pl.BlockSpec((ROWS, XE_W), lambda b, s: (b, 0)), _block_spec(N_EXPERTS, ROWS)],
            out_specs=pl.BlockSpec(memory_space=pl.ANY),
            scratch_shapes=[pltpu.VMEM((MAX_UNITS * WIN, ROWS), BF16), pltpu.VMEM((2, MAX_UNITS * WIN, XE_W), BF16),
                            pltpu.SemaphoreType.DMA((2, N_EXPERTS))]),
        compiler_params=_params(1), name="dispatch",
    )(s_tab, h2, posm)


def _ffn(xes, s_tabs, wg, wu, wd):
    n_sets = len(xes)
    per_expert = lambda rows, cols: pl.BlockSpec((None, rows, cols), lambda e, *s: (e, 0, 0))
    return pl.pallas_call(
        functools.partial(_ffn_kernel, n_blks=tuple(s.shape[1] - 1 for s in s_tabs)),
        out_shape=[jax.ShapeDtypeStruct(xe.shape[:2] + (D_MODEL,), BF16) for xe in xes],
        grid_spec=pltpu.PrefetchScalarGridSpec(
            num_scalar_prefetch=n_sets, grid=(N_EXPERTS,),
            in_specs=[per_expert(xe.shape[1], XE_W) for xe in xes]
            + [per_expert(D_MODEL, EXPERT_FF), per_expert(D_MODEL, EXPERT_FF), per_expert(EXPERT_FF, D_MODEL)],
            out_specs=[per_expert(xe.shape[1], D_MODEL) for xe in xes],
            scratch_shapes=[pltpu.VMEM((D_MODEL, EXPERT_FF), BF16), pltpu.VMEM((D_MODEL, EXPERT_FF), BF16),
                            pltpu.VMEM((EXPERT_FF, D_MODEL), BF16)]),
        compiler_params=_params(1), name="ffn",
    )(*s_tabs, *xes, wg, wu, wd)


def _combine(y, posm, s_tab, x1, mod, n_tok):
    n_all = x1.shape[0]
    n_blk = n_all // ROWS
    tiles_per_seq = n_tok // ROWS
    mod_map = (lambda b, s: (0, 0, 0)) if mod.shape[0] == 1 else (lambda b, s: (b // tiles_per_seq, 0, 0))
    rows = pl.BlockSpec((ROWS, D_MODEL), lambda b, s: (b, 0))
    return pl.pallas_call(
        functools.partial(_combine_kernel, n_blk=n_blk, cap_pad=y.shape[1]),
        out_shape=jax.ShapeDtypeStruct((n_all, D_MODEL), F32),
        grid_spec=pltpu.PrefetchScalarGridSpec(
            num_scalar_prefetch=1, grid=(n_blk,),
            in_specs=[pl.BlockSpec(memory_space=pl.ANY), _block_spec(N_EXPERTS, ROWS), rows,
                      pl.BlockSpec((None, 6, D_MODEL), mod_map)],
            out_specs=rows,
            scratch_shapes=[pltpu.VMEM((2, MAX_UNITS * WIN, D_MODEL), BF16), pltpu.VMEM((MAX_UNITS * WIN, ROWS), F32),
                            pltpu.VMEM((ROWS, D_MODEL), F32), pltpu.SemaphoreType.DMA((2,))]),
        compiler_params=_params(1), name="combine",
    )(s_tab, y, posm, x1, mod)


def _rope_tables(n_tok):
    rows = n_tok // GRID_W
    row = np.repeat(np.arange(rows, dtype=np.float64), GRID_W)
    col = np.tile(np.arange(GRID_W, dtype=np.float64), rows)
    axis_dim = QK_ROPE // 2
    inv_freq = ROPE_THETA ** (-np.arange(0, axis_dim, 2, dtype=np.float64) / axis_dim)
    ang = np.concatenate([row[:, None] * inv_freq, col[:, None] * inv_freq], axis=-1)
    cos = np.ones((n_tok, LANES))
    sin = np.zeros((n_tok, LANES))
    cos[:, ROPE_LANE0:ROPE_LANE0 + QK_ROPE] = np.repeat(np.cos(ang), 2, axis=1)
    sgn = np.tile(np.array([-1.0, 1.0]), QK_ROPE // 2)
    sin[:, ROPE_LANE0:ROPE_LANE0 + QK_ROPE] = np.repeat(np.sin(ang), 2, axis=1) * sgn
    return jnp.asarray(cos, F32), jnp.asarray(sin, F32)


def _dft_tables(n_tok):
    def cs(n):
        k = np.arange(n)
        ang = 2.0 * np.pi * ((k[:, None] * k[None, :]) % n) / n
        return np.cos(ang), np.sin(ang)
    cc, sc = cs(FNET_CH)
    cn, sn = cs(n_tok)
    return tuple(jnp.asarray(t, F32) for t in (cc, sc, np.concatenate([cn, -sn], axis=1)))


def _pad_heads(w, width):
    lead = w.shape[:-1]
    w = w.reshape(lead + (N_HEADS, width))
    w = jnp.pad(w, [(0, 0)] * len(lead) + [(0, 0), (0, HEAD_PAD - width)])
    return w.reshape(lead + (N_HEADS * HEAD_PAD,))


def _pair_swap_lanes(n_groups):
    perm = np.arange(n_groups * LANES).reshape(n_groups, LANES)
    rot = perm[:, ROPE_LANE0:ROPE_LANE0 + QK_ROPE].reshape(n_groups, QK_ROPE // 2, 2)[:, :, ::-1]
    perm[:, ROPE_LANE0:ROPE_LANE0 + QK_ROPE] = rot.reshape(n_groups, QK_ROPE)
    return perm.reshape(-1)


def _spare_from_rotary(t, src=None):
    src = t if src is None else src
    lead = t.shape[:-1]
    t, src = t.reshape(lead + (-1, LANES)), src.reshape(lead + (-1, LANES))
    t = jnp.concatenate([t[..., :QK_DIM], src[..., ROPE_LANE0:ROPE_LANE0 + QK_ROPE]], axis=-1)
    return t.reshape(lead + (-1,))


def _head_tables(gain, rope_tabs, n_tok, query):
    if rope_tabs is None:
        tab = jnp.broadcast_to(gain, (n_tok, LANES))
        return (tab if query else _spare_from_rotary(tab),)
    cos, sin = rope_tabs
    a, b = gain * cos, gain[:, _pair_swap_lanes(1)] * sin
    return (_spare_from_rotary(a, b),) if query else (_spare_from_rotary(a), _spare_from_rotary(b))


def _layer0_weights(norm1, norm2, w_in, q_a_norm, w_q_up, q_norm, kv_a_norm, w_kv_up, k_norm, conv_w, w_o):
    c0 = Q_LORA + KV_LORA
    rope_cols = jnp.pad(w_in[:, c0:c0 + QK_ROPE], ((0, 0), (ROPE_LANE0, LANES - ROPE_LANE0 - QK_ROPE)))
    w_in_pad = jnp.concatenate([w_in[:, :c0], _spare_from_rotary(rope_cols),
                                _spare_from_rotary(rope_cols[:, _pair_swap_lanes(1)]),
                                w_in[:, c0 + QK_ROPE:]], axis=1)
    kv = w_kv_up.reshape(KV_LORA, N_HEADS, QK_NOPE + V_HEAD)
    w_q = _pad_heads(w_q_up, QK_DIM)
    w_q = _spare_from_rotary(w_q, w_q[:, _pair_swap_lanes(N_HEADS)])
    head_gain = lambda g: jnp.pad(g, (0, HEAD_PAD - QK_DIM)).reshape(1, -1)
    return dict(
        norm1=norm1.reshape(1, -1), norm2=norm2.reshape(1, -1), w_in=w_in_pad.astype(BF16),
        q_a_norm=q_a_norm.reshape(1, -1), w_q=w_q.astype(BF16),
        q_gain=head_gain(q_norm) * (QK_DIM ** -0.5), kv_a_norm=kv_a_norm.reshape(1, -1),
        w_k=_pad_heads(kv[:, :, :QK_NOPE].reshape(KV_LORA, -1), QK_NOPE).astype(BF16),
        w_v=kv[:, :, QK_NOPE:].reshape(KV_LORA, -1).astype(BF16),
        w_v_wide=_pad_heads(kv[:, :, QK_NOPE:].reshape(KV_LORA, -1), V_HEAD).astype(BF16), k_gain=head_gain(k_norm),
        conv_w=conv_w, w_o=w_o.astype(BF16))


def _router_weights(w_router):
    hi, lo = _split_hi_lo(jnp.pad(w_router, ((0, 0), (0, LANES - N_EXPERTS))))
    return dict(wr_hi=hi, wr_lo=lo)


def _moe(sets, tri, w):
    routed = []
    for x1, h2, aff, mod, n_tok in sets:
        cap = CAPACITY_FACTOR * x1.shape[0] // N_EXPERTS
        posm, s_tab = _route(aff, tri, cap)
        routed.append((posm, s_tab, _dispatch(h2, posm, s_tab, cap)))
    ys = _ffn([r[2] for r in routed], [r[1] for r in routed], w["wg"], w["wu"], w["wd"])
    return [_combine(y, posm, s_tab, x1, mod, n_tok)
            for y, (posm, s_tab, _), (x1, _, _, mod, n_tok) in zip(ys, routed, sets)]


def _mixer0(x, mod, l0, ctx, rope_tabs):
    n_b, n_tok, _ = x.shape
    n_keys = n_tok + (0 if ctx is None else ctx[0].shape[1])
    w_v = l0["w_v_wide"] if n_keys > WIDE_ROWS else l0["w_v"]
    q, k, v, conv, ckv, kr = _front0(x, mod, l0, rope_tabs, w_v)
    if ctx is not None:
        kc, vc = _ctx_kv(ctx[0], ctx[1], l0, w_v)
        attn = _attention(q, k, v, kc, vc)
    else:
        attn = _attention(q, k, v)
    flat = lambda a: a.reshape(n_b * n_tok, a.shape[-1])
    return _post0(flat(x), flat(attn), flat(conv), mod, l0, n_tok), (ckv, kr[:, :, ROPE_LANE0:ROPE_LANE0 + QK_ROPE])


def kernel(x_prompt, x_sample, c, cache_c_kv_l0, cache_k_rope_l0, c_ctx, norm1_l0, norm2_l0, w_mod_l0, b_mod_l0, w_in_l0, q_a_norm_l0, w_q_up_l0, q_norm_l0, kv_a_norm_l0, w_kv_up_l0, k_norm_l0, conv_w_l0, w_o_l0, w_router_l0, w_gate_l0, w_up_l0, w_down_l0, norm1_l1, norm2_l1, w_mod_l1, b_mod_l1, w_f_l1, w_router_l1, w_gate_l1, w_up_l1, w_down_l1):
    n_dec = c.shape[0]
    cond = jnp.concatenate([c_ctx[None, :], c, jnp.zeros((16 - 1 - n_dec, D_MODEL), F32)], axis=0)
    m0 = _modulation(cond, w_mod_l0, b_mod_l0)
    m1 = _modulation(cond, w_mod_l1, b_mod_l1)
    mods_prompt = (m0[0:1], m1[0:1])
    mods_sample = (m0[1:1 + n_dec], m1[1:1 + n_dec])

    l0 = _layer0_weights(norm1_l0, norm2_l0, w_in_l0, q_a_norm_l0, w_q_up_l0, q_norm_l0, kv_a_norm_l0,
                         w_kv_up_l0, k_norm_l0, conv_w_l0, w_o_l0)
    l0.update(_router_weights(w_router_l0))
    l0.update(wg=w_gate_l0, wu=w_up_l0, wd=w_down_l0)
    l1 = dict(norm1=norm1_l1.reshape(1, -1), norm2=norm2_l1.reshape(1, -1), w_f=w_f_l1.astype(BF16))
    l1.update(_router_weights(w_router_l1))
    l1.update(wg=w_gate_l1, wu=w_up_l1, wd=w_down_l1)

    tri = jnp.asarray(np.triu(np.ones((ROWS, ROWS)), 1), BF16)
    k_rope_pad = _spare_from_rotary(
        jnp.pad(cache_k_rope_l0, ((0, 0), (0, 0), (ROPE_LANE0, LANES - ROPE_LANE0 - QK_ROPE))))
    xs = (x_prompt, x_sample)
    mods = (mods_prompt, mods_sample)
    n_toks = tuple(x.shape[1] for x in xs)

    (front_p, (new_c_kv, new_k_rope)) = _mixer0(x_prompt, mods_prompt[0], l0, None, None)
    (front_s, _) = _mixer0(x_sample, mods_sample[0], l0, (cache_c_kv_l0, k_rope_pad), _rope_tables(n_toks[1]))
    ys = _moe([tuple(front) + (mod[0], n_tok) for front, mod, n_tok in zip((front_p, front_s), mods, n_toks)],
              tri, l0)

    sets = []
    for y, x, mod, n_tok in zip(ys, xs, mods, n_toks):
        x1, h2, aff = _fnet(y.reshape(x.shape), mod[1], l1, _dft_tables(n_tok))
        sets.append((x1.reshape(-1, D_MODEL), h2.reshape(-1, XE_W), aff, mod[1], n_tok))
    y_prompt, y_sample = (y.reshape(x.shape) for y, x in zip(_moe(sets, tri, l1), xs))
    return (y_prompt, y_sample, new_c_kv, new_k_rope)
```

```python
import functools

import jax
import jax.numpy as jnp
import numpy as np
from jax import lax
from jax.experimental import pallas as pl
from jax.experimental.pallas import tpu as pltpu

D_MODEL = 1024
GRID_W = 64
N_HEADS = 8
QK_NOPE = 64
QK_ROPE = 32
QK_DIM = QK_NOPE + QK_ROPE
V_HEAD = 64
Q_LORA = 384
KV_LORA = 256
CONV_CH = 512
FNET_GROUPS = 4
FNET_CH = D_MODEL // FNET_GROUPS
N_EXPERTS = 16
EXPERT_FF = 512
CAPACITY_FACTOR = 2
ROPE_THETA = 10000.0
EPS = 1e-6

LANES = 128
HEAD_PAD = LANES
ROWS = 256
WIDE_ROWS = 512
ALIGN = 8
WIN = 64
MAX_UNITS = N_EXPERTS * (ROWS // WIN)
COMMON_ROWS = N_EXPERTS * WIN
XE_W = D_MODEL + LANES
IN0_PAD = Q_LORA + KV_LORA + 2 * LANES + 3 * CONV_CH
ROPE_LANE0 = QK_NOPE
VMEM_LIMIT = 56 * 1024 * 1024

F32 = jnp.float32
BF16 = jnp.bfloat16


def _dot(a, b):
    return jnp.dot(a, b, preferred_element_type=F32)


def _dot_nt(a, b):
    return lax.dot_general(a, b, (((1,), (1,)), ((), ())), preferred_element_type=F32)


def _split_hi_lo(x):
    hi = x.astype(BF16)
    lo = (x - hi.astype(F32)).astype(BF16)
    return hi, lo


def _params(n_axes):
    return pltpu.CompilerParams(dimension_semantics=("arbitrary",) * n_axes,
                                vmem_limit_bytes=VMEM_LIMIT)


def _rms(x, gain):
    return x * lax.rsqrt(jnp.mean(x * x, axis=-1, keepdims=True) + EPS) * gain


def _mod_kernel(cond_ref, w_ref, b_ref, o_ref):
    c = cond_ref[...]
    s = c * (1.0 / (1.0 + jnp.exp(-c)))
    s_hi, s_lo = _split_hi_lo(s)
    w = w_ref[...].astype(BF16)
    o_ref[...] = _dot(s_hi, w) + _dot(s_lo, w) + b_ref[...]


def _modulation(cond, w_mod, b_mod):
    n_rows = cond.shape[0]
    tn = 1536
    out = pl.pallas_call(
        _mod_kernel,
        out_shape=jax.ShapeDtypeStruct((n_rows, 6 * D_MODEL), F32),
        grid=(6 * D_MODEL // tn,),
        in_specs=[pl.BlockSpec((n_rows, D_MODEL), lambda i: (0, 0)),
                  pl.BlockSpec((D_MODEL, tn), lambda i: (0, i)),
                  pl.BlockSpec((1, tn), lambda i: (0, i))],
        out_specs=pl.BlockSpec((n_rows, tn), lambda i: (0, i)),
        compiler_params=_params(1),
        name="modulation",
    )(cond, w_mod, b_mod.reshape(1, -1))
    return out.reshape(n_rows, 6, D_MODEL)


def _head_norm_rope(xh, gain_cos, swapped_sin):
    lane = lax.broadcasted_iota(jnp.int32, (1, HEAD_PAD), 1)
    ss = jnp.sum(jnp.where(lane < QK_DIM, xh * xh, 0.0), axis=-1, keepdims=True) * (1.0 / QK_DIM)
    y = xh * gain_cos
    if swapped_sin is not None:
        y = y + swapped_sin
    return y * lax.rsqrt(ss + EPS)


def _expand_kv(ckv_bf, r, swapped_sin, wk_ref, wv_ref, gain_cos, k_ref, v_ref, rows):
    kf = _dot(ckv_bf, wk_ref[...])
    vf = _dot(ckv_bf, wv_ref[...])
    if v_ref.shape[-1] == N_HEADS * HEAD_PAD:
        lane = lax.broadcasted_iota(jnp.int32, (1, N_HEADS * HEAD_PAD), 1)
        vf = vf + jnp.where((lane & (HEAD_PAD - 1)) == V_HEAD, 1.0, 0.0)
    v_ref[rows, :] = vf.astype(BF16)
    for h in range(N_HEADS):
        sl = slice(h * HEAD_PAD, (h + 1) * HEAD_PAD)
        k_ref[rows, sl] = _head_norm_rope(kf[:, sl] + r, gain_cos, swapped_sin).astype(BF16)


def _front0_kernel(*refs, n_tok, rope):
    (x_ref, mod_ref, n1_ref, win_ref, qan_ref, wq_ref, kvan_ref, wk_ref, wv_ref, cw_ref,
     qa_ref, ka_ref) = refs[:12]
    pos = 12
    if rope:
        kb_ref = refs[pos]
        pos += 1
    q_ref, k_ref, v_ref, conv_ref, ckv_ref, kr_ref, cu_s, gb_s = refs[pos:]
    sh1, sc1 = mod_ref[0:1, :], mod_ref[1:2, :]
    c_rope = Q_LORA + KV_LORA
    wide = min(n_tok, WIDE_ROWS)
    for c in range(n_tok // ROWS):
        rows = pl.ds(c * ROWS, ROWS)
        if (c * ROWS) % wide == 0:
            wide_rows = pl.ds(c * ROWS, wide)
            h = _rms(x_ref[wide_rows, :], n1_ref[...]) * (1.0 + sc1) + sh1
            proj_wide = _dot(h.astype(BF16), win_ref[...])
        off = (c * ROWS) % wide
        proj = proj_wide[off:off + ROWS, :]
        cq = _rms(proj[:, :Q_LORA], qan_ref[...]).astype(BF16)
        qf = _dot(cq, wq_ref[...])
        for hd in range(N_HEADS):
            sl = slice(hd * HEAD_PAD, (hd + 1) * HEAD_PAD)
            q_ref[rows, sl] = _head_norm_rope(qf[:, sl], qa_ref[rows, :], None).astype(BF16)
        ckv = _rms(proj[:, Q_LORA:c_rope], kvan_ref[...])
        ckv_ref[rows, :] = ckv
        r = proj[:, c_rope:c_rope + LANES]
        kr_ref[rows, :] = r
        k_sin = proj[:, c_rope + LANES:c_rope + 2 * LANES] * kb_ref[rows, :] if rope else None
        _expand_kv(ckv.astype(BF16), r, k_sin, wk_ref, wv_ref, ka_ref[rows, :], k_ref, v_ref, rows)
        c0 = c_rope + 2 * LANES
        gb_s[rows, :] = proj[:, c0:c0 + CONV_CH]
        cu_s[rows, :] = proj[:, c0 + CONV_CH:c0 + 2 * CONV_CH] * proj[:, c0 + 2 * CONV_CH:c0 + 3 * CONV_CH]
    cu = cu_s[...]
    row = lax.broadcasted_iota(jnp.int32, cu.shape, 0)
    prev = jnp.where(row == 0, 0.0, pltpu.roll(cu, 1, 0))
    nxt = jnp.where(row == n_tok - 1, 0.0, pltpu.roll(cu, n_tok - 1, 0))
    conv = gb_s[...] * (cw_ref[0:1, :] * prev + cw_ref[1:2, :] * cu + cw_ref[2:3, :] * nxt)
    conv_ref[...] = conv.astype(BF16)


def _const_spec(shape):
    return pl.BlockSpec(shape, lambda b: (0,) * len(shape))


def _front0(x, mod, w, rope_tabs, w_v):
    n_b, n_tok, _ = x.shape
    v_width = w_v.shape[1]
    shared_mod = mod.shape[0] == 1
    seq = lambda width: pl.BlockSpec((None, n_tok, width), lambda b: (b, 0, 0))
    in_specs = [seq(D_MODEL),
                pl.BlockSpec((None, 6, D_MODEL), (lambda b: (0, 0, 0)) if shared_mod else (lambda b: (b, 0, 0))),
                _const_spec((1, D_MODEL)), _const_spec((D_MODEL, IN0_PAD)), _const_spec((1, Q_LORA)),
                _const_spec((Q_LORA, N_HEADS * HEAD_PAD)),
                _const_spec((1, KV_LORA)), _const_spec((KV_LORA, N_HEADS * HEAD_PAD)),
                _const_spec((KV_LORA, v_width)), _const_spec((3, CONV_CH)),
                _const_spec((n_tok, LANES)), _const_spec((n_tok, LANES))]
    q_tabs = _head_tables(w["q_gain"], rope_tabs, n_tok, True)
    k_tabs = _head_tables(w["k_gain"], rope_tabs, n_tok, False)
    args = [x, mod, w["norm1"], w["w_in"], w["q_a_norm"], w["w_q"], w["kv_a_norm"],
            w["w_k"], w_v, w["conv_w"], q_tabs[0], k_tabs[0]]
    if rope_tabs is not None:
        in_specs += [_const_spec((n_tok, LANES))]
        args += [k_tabs[1]]
    out_shape = [jax.ShapeDtypeStruct((n_b, n_tok, N_HEADS * HEAD_PAD), BF16),
                 jax.ShapeDtypeStruct((n_b, n_tok, N_HEADS * HEAD_PAD), BF16),
                 jax.ShapeDtypeStruct((n_b, n_tok, v_width), BF16),
                 jax.ShapeDtypeStruct((n_b, n_tok, CONV_CH), BF16),
                 jax.ShapeDtypeStruct((n_b, n_tok, KV_LORA), F32),
                 jax.ShapeDtypeStruct((n_b, n_tok, LANES), F32)]
    out_specs = [seq(N_HEADS * HEAD_PAD), seq(N_HEADS * HEAD_PAD), seq(v_width), seq(CONV_CH),
                 seq(KV_LORA), seq(LANES)]
    return pl.pallas_call(
        functools.partial(_front0_kernel, n_tok=n_tok, rope=rope_tabs is not None),
        out_shape=out_shape, grid=(n_b,), in_specs=in_specs, out_specs=out_specs,
        scratch_shapes=[pltpu.VMEM((n_tok, CONV_CH), F32), pltpu.VMEM((n_tok, CONV_CH), F32)],
        compiler_params=_params(1), name="front0",
    )(*args)


def _ctx_kv_kernel(ckv_ref, r_ref, wk_ref, wv_ref, ka_ref, k_ref, v_ref):
    rows = pl.ds(0, ckv_ref.shape[0])
    _expand_kv(ckv_ref[...].astype(BF16), r_ref[...], None, wk_ref, wv_ref, ka_ref[...], k_ref, v_ref, rows)


def _ctx_kv(cache_c_kv, cache_k_rope_pad, w, w_v):
    n_b, n_ctx, _ = cache_c_kv.shape
    v_width = w_v.shape[1]
    seq = lambda width: pl.BlockSpec((None, n_ctx, width), lambda b: (b, 0, 0))
    return pl.pallas_call(
        _ctx_kv_kernel,
        out_shape=[jax.ShapeDtypeStruct((n_b, n_ctx, N_HEADS * HEAD_PAD), BF16),
                   jax.ShapeDtypeStruct((n_b, n_ctx, v_width), BF16)],
        grid=(n_b,),
        in_specs=[seq(KV_LORA), seq(LANES), _const_spec((KV_LORA, N_HEADS * HEAD_PAD)),
                  _const_spec((KV_LORA, v_width)), _const_spec((n_ctx, LANES))],
        out_specs=[seq(N_HEADS * HEAD_PAD), seq(v_width)],
        compiler_params=_params(1), name="ctx_kv",
    )(cache_c_kv, cache_k_rope_pad, w["w_k"], w_v, _head_tables(w["k_gain"], None, n_ctx, False)[0])


def _attn_kernel(*refs, with_ctx):
    if with_ctx:
        q_ref, k_ref, v_ref, kc_ref, vc_ref, o_ref = refs
    else:
        q_ref, k_ref, v_ref, o_ref = refs
    ones_lane = v_ref.shape[-1] == N_HEADS * HEAD_PAD
    lane = lax.broadcasted_iota(jnp.int32, (q_ref.shape[0], LANES), 1)
    for pair in range(N_HEADS // 2):
        outs = []
        for hd in (2 * pair, 2 * pair + 1):
            sl = slice(hd * HEAD_PAD, (hd + 1) * HEAD_PAD)
            vsl = sl if ones_lane else slice(pair * LANES, (pair + 1) * LANES)
            qh = q_ref[:, sl]
            s = _dot_nt(qh, k_ref[:, sl])
            m = jnp.max(s, axis=-1, keepdims=True)
            if with_ctx:
                sc = _dot_nt(qh, kc_ref[:, sl])
                m = jnp.maximum(m, jnp.max(sc, axis=-1, keepdims=True))
            p = jnp.exp(s - m)
            o = _dot(p.astype(BF16), v_ref[:, vsl])
            if with_ctx:
                pc = jnp.exp(sc - m)
                o = o + _dot(pc.astype(BF16), vc_ref[:, vsl])
            if ones_lane:
                den = o[:, V_HEAD:V_HEAD + 1]
            else:
                den = jnp.sum(p, axis=-1, keepdims=True)
                if with_ctx:
                    den = den + jnp.sum(pc, axis=-1, keepdims=True)
            outs.append(o / den)
        odd = pltpu.roll(outs[1], V_HEAD, 1) if ones_lane else outs[1]
        o_ref[:, pair * LANES:(pair + 1) * LANES] = jnp.where(lane < V_HEAD, outs[0], odd).astype(BF16)


def _attention(q, k, v, kc=None, vc=None):
    n_b, n_tok, _ = q.shape
    with_ctx = kc is not None
    tq = min(n_tok, WIDE_ROWS)
    qspec = lambda width: pl.BlockSpec((None, tq, width), lambda b, i: (b, i, 0))
    kvspec = lambda a: pl.BlockSpec((None,) + a.shape[1:], lambda b, i: (b, 0, 0))
    in_specs = [qspec(N_HEADS * HEAD_PAD), kvspec(k), kvspec(v)]
    args = [q, k, v]
    if with_ctx:
        in_specs += [kvspec(kc), kvspec(vc)]
        args += [kc, vc]
    return pl.pallas_call(
        functools.partial(_attn_kernel, with_ctx=with_ctx),
        out_shape=jax.ShapeDtypeStruct((n_b, n_tok, N_HEADS * V_HEAD), BF16),
        grid=(n_b, n_tok // tq), in_specs=in_specs, out_specs=qspec(N_HEADS * V_HEAD),
        compiler_params=_params(2), name="attention",
    )(*args)


def _moe_front(x1, mod_ref, n2_ref, wrh_ref, wrl_ref, h2_ref, aff_ref, rows):
    sh2, sc2 = mod_ref[3:4, :], mod_ref[4:5, :]
    h2 = _rms(x1, n2_ref[...]) * (1.0 + sc2) + sh2
    h_hi, h_lo = _split_hi_lo(h2)
    logit = _dot(h_hi, wrh_ref[...]) + _dot(h_lo, wrh_ref[...]) + _dot(h_hi, wrl_ref[...])
    real = lax.broadcasted_iota(jnp.int32, (1, LANES), 1) < N_EXPERTS
    top = jnp.max(jnp.where(real, logit, -jnp.inf), axis=1, keepdims=True)
    e = jnp.where(real, jnp.exp(logit - top), 0.0)
    a = e / jnp.sum(e, axis=1, keepdims=True)
    a_hi = a.astype(BF16).astype(F32)
    a_mid = (a - a_hi).astype(BF16).astype(F32)
    a_lo = a - a_hi - a_mid
    h2_ref[rows, 0:D_MODEL] = h_hi
    h2_ref[rows, D_MODEL:XE_W] = (a_hi + pltpu.roll(a_mid, N_EXPERTS, 1)
                                  + pltpu.roll(a_lo, 2 * N_EXPERTS, 1)).astype(BF16)
    aff_ref[...] = a.T[0:N_EXPERTS, :]


def _post0_kernel(x_ref, attn_ref, conv_ref, wo_ref, mod_ref, n2_ref, wrh_ref, wrl_ref,
                  x1_ref, h2_ref, aff_ref):
    n_attn = N_HEADS * V_HEAD
    mix = _dot(attn_ref[...], wo_ref[0:n_attn, :]) + _dot(conv_ref[...], wo_ref[n_attn:, :])
    for t in range(x_ref.shape[0] // ROWS):
        rows = pl.ds(t * ROWS, ROWS)
        x1 = x_ref[rows, :] + mod_ref[2:3, :] * mix[t * ROWS:(t + 1) * ROWS, :]
        x1_ref[rows, :] = x1
        _moe_front(x1, mod_ref, n2_ref, wrh_ref, wrl_ref, h2_ref, aff_ref.at[t], rows)


def _post0(x, attn, conv, mod, w, n_tok):
    n_all = x.shape[0]
    rows = min(n_tok, WIDE_ROWS)
    tiles_per_seq = n_tok // rows
    shared_mod = mod.shape[0] == 1
    tile = lambda width: pl.BlockSpec((rows, width), lambda i: (i, 0))
    const = lambda shape: pl.BlockSpec(shape, lambda i: (0,) * len(shape))
    mod_map = (lambda i: (0, 0, 0)) if shared_mod else (lambda i: (i // tiles_per_seq, 0, 0))
    return pl.pallas_call(
        _post0_kernel,
        out_shape=[jax.ShapeDtypeStruct((n_all, D_MODEL), F32),
                   jax.ShapeDtypeStruct((n_all, XE_W), BF16),
                   jax.ShapeDtypeStruct((n_all // ROWS, N_EXPERTS, ROWS), F32)],
        grid=(n_all // rows,),
        in_specs=[tile(D_MODEL), tile(N_HEADS * V_HEAD), tile(CONV_CH), const((D_MODEL, D_MODEL)),
                  pl.BlockSpec((None, 6, D_MODEL), mod_map), const((1, D_MODEL)),
                  const((D_MODEL, LANES)), const((D_MODEL, LANES))],
        out_specs=[tile(D_MODEL), tile(XE_W),
                   pl.BlockSpec((rows // ROWS, N_EXPERTS, ROWS), lambda i: (i, 0, 0))],
        compiler_params=_params(1), name="post0",
    )(x, attn, conv, w["w_o"], mod, w["norm2"], w["wr_hi"], w["wr_lo"])


def _fnet_kernel(x_ref, mod_ref, n1_ref, cc_ref, sc_ref, dft_ref, wf_ref, n2_ref, wrh_ref, wrl_ref,
                 x1_ref, h2_ref, aff_ref, y_s, *, n_tok):
    sh1, sc1, g1 = mod_ref[0:1, :], mod_ref[1:2, :], mod_ref[2:3, :]
    n_chunks = n_tok // ROWS
    cc, sc = cc_ref[...].astype(BF16), sc_ref[...].astype(BF16)
    for c in range(n_chunks):
        rows = pl.ds(c * ROWS, ROWS)
        h = (_rms(x_ref[rows, :], n1_ref[...]) * (1.0 + sc1) + sh1).astype(BF16)
        for g in range(FNET_GROUPS):
            sl = slice(g * FNET_CH, (g + 1) * FNET_CH)
            y_s[pl.ds(c * ROWS, ROWS), sl] = _dot(h[:, sl], cc).astype(BF16)
            y_s[pl.ds(n_tok + c * ROWS, ROWS), sl] = _dot(h[:, sl], sc).astype(BF16)
    scale = 1.0 / float(np.sqrt(n_tok * FNET_CH))
    wide = min(n_tok, WIDE_ROWS)
    for c in range(n_chunks):
        rows = pl.ds(c * ROWS, ROWS)
        if (c * ROWS) % wide == 0:
            f = _dot(dft_ref[pl.ds(c * ROWS, wide), :].astype(BF16), y_s[...]) * scale
            mix = _dot(f.astype(BF16), wf_ref[...])
        off = (c * ROWS) % wide
        x1 = x_ref[rows, :] + g1 * mix[off:off + ROWS, :]
        x1_ref[rows, :] = x1
        _moe_front(x1, mod_ref, n2_ref, wrh_ref, wrl_ref, h2_ref, aff_ref.at[c], rows)


def _fnet(x, mod, w, dft):
    n_b, n_tok, _ = x.shape
    shared_mod = mod.shape[0] == 1
    seq = lambda width: pl.BlockSpec((None, n_tok, width), lambda b: (b, 0, 0))
    cc, sc, dft_n = dft
    tiles = n_tok // ROWS
    return pl.pallas_call(
        functools.partial(_fnet_kernel, n_tok=n_tok),
        out_shape=[jax.ShapeDtypeStruct((n_b, n_tok, D_MODEL), F32),
                   jax.ShapeDtypeStruct((n_b, n_tok, XE_W), BF16),
                   jax.ShapeDtypeStruct((n_b * tiles, N_EXPERTS, ROWS), F32)],
        grid=(n_b,),
        in_specs=[seq(D_MODEL),
                  pl.BlockSpec((None, 6, D_MODEL), (lambda b: (0, 0, 0)) if shared_mod else (lambda b: (b, 0, 0))),
                  _const_spec((1, D_MODEL)), _const_spec((FNET_CH, FNET_CH)), _const_spec((FNET_CH, FNET_CH)),
                  pl.BlockSpec((n_tok, 2 * n_tok), lambda b: (0, 0), pipeline_mode=pl.Buffered(1)),
                  _const_spec((D_MODEL, D_MODEL)), _const_spec((1, D_MODEL)),
                  _const_spec((D_MODEL, LANES)), _const_spec((D_MODEL, LANES))],
        out_specs=[seq(D_MODEL), seq(XE_W), pl.BlockSpec((tiles, N_EXPERTS, ROWS), lambda b: (b, 0, 0))],
        scratch_shapes=[pltpu.VMEM((2 * n_tok, D_MODEL), BF16)],
        compiler_params=_params(1), name="fnet",
    )(x, mod, w["norm1"], cc, sc, dft_n, w["w_f"], w["norm2"], w["wr_hi"], w["wr_lo"])


def _route_kernel(aff_ref, tri_ref, posm_ref, s_ref, *, n_blk, cap):
    def count(pred):
        acc = jnp.zeros((N_EXPERTS, ROWS), F32)
        for b in range(n_blk):
            acc = acc + jnp.where(pred(aff_ref[b]), 1.0, 0.0)
        return jnp.sum(acc, axis=1, keepdims=True)

    def as_f32(bits):
        return pltpu.bitcast(bits, F32)

    def search(_, carry):
        lo, hi = carry
        mid = lo + ((hi - lo) >> 1)
        mid_f = as_f32(mid)
        ok = count(lambda a: a >= mid_f) >= cap
        return jnp.where(ok, mid, lo), jnp.where(ok, hi, mid)

    one_bits = 0x3F800000
    lo0 = jnp.zeros((N_EXPERTS, 1), jnp.int32)
    hi0 = jnp.full((N_EXPERTS, 1), one_bits + 1, jnp.int32)
    lo, hi = lax.fori_loop(0, 31, search, (lo0, hi0))
    lo_f, ub = as_f32(lo), as_f32(hi)
    thr = lo_f
    pending = jnp.ones((N_EXPERTS, 1), F32)
    for _ in range(3):
        cur = jnp.full((N_EXPERTS, ROWS), -1.0, F32)
        for b in range(n_blk):
            a = aff_ref[b]
            cur = jnp.maximum(cur, jnp.where((a >= lo_f) & (a < ub), a, -1.0))
        cur = jnp.max(cur, axis=1, keepdims=True)
        take = (count(lambda a: a >= cur) >= cap) & (pending > 0.0)
        thr = jnp.where(take, cur, thr)
        pending = jnp.where(take, 0.0, pending)
        ub = cur
    need = cap - count(lambda a: a > thr)

    lane = lax.broadcasted_iota(jnp.int32, (N_EXPERTS, LANES), 1)
    carry_tie = jnp.zeros((N_EXPERTS, 1), F32)
    carry_pos = jnp.zeros((N_EXPERTS, 1), F32)
    s_acc = jnp.zeros((N_EXPERTS, LANES), jnp.int32)
    for b in range(n_blk):
        v = aff_ref[b]
        eq = jnp.where(v == thr, 1.0, 0.0)
        tie_rank = _dot(eq.astype(BF16), tri_ref[...]) + carry_tie
        sel = jnp.where((v > thr) | ((v == thr) & (tie_rank < need)), 1.0, 0.0)
        pos = _dot(sel.astype(BF16), tri_ref[...]) + carry_pos
        posm_ref[b] = jnp.where(sel > 0.0, pos, -1.0)
        s_acc = jnp.where(lane == b, carry_pos.astype(jnp.int32), s_acc)
        carry_tie = carry_tie + jnp.sum(eq, axis=1, keepdims=True)
        n_sel = jnp.sum(sel, axis=1, keepdims=True)
        carry_pos = carry_pos + jnp.floor((n_sel + (ALIGN - 1)) * (1.0 / ALIGN)) * ALIGN
    s_ref[...] = jnp.where(lane == n_blk, carry_pos.astype(jnp.int32), s_acc)


def _route(aff, tri, cap):
    n_blk = aff.shape[0]
    assert n_blk < LANES
    full = lambda shape: pl.BlockSpec(shape, lambda i: (0,) * len(shape))
    posm, s_tab = pl.pallas_call(
        functools.partial(_route_kernel, n_blk=n_blk, cap=cap),
        out_shape=[jax.ShapeDtypeStruct((n_blk, N_EXPERTS, ROWS), F32),
                   jax.ShapeDtypeStruct((N_EXPERTS, LANES), jnp.int32)],
        grid=(1,),
        in_specs=[full((n_blk, N_EXPERTS, ROWS)), full((ROWS, ROWS))],
        out_specs=[full((n_blk, N_EXPERTS, ROWS)), full((N_EXPERTS, LANES))],
        compiler_params=_params(1), name="route",
    )(aff, tri)
    return posm, s_tab[:, :n_blk + 1]


def _ceil_div_pow2(x, d):
    return lax.shift_right_logical(x + (d - 1), d.bit_length() - 1)


def _block_units(s_ref, b):
    base = [s_ref[e, b] for e in range(N_EXPERTS)]
    units = [_ceil_div_pow2(s_ref[e, b + 1] - base[e], WIN) for e in range(N_EXPERTS)]
    first_unit, total = [], 0
    for e in range(N_EXPERTS):
        first_unit.append(total)
        total = total + units[e]
    return base, units, first_unit, total


def _for_units(units, fn):
    for e in range(N_EXPERTS):
        def body(u, _, e=e):
            fn(e, u)
            return 0
        lax.fori_loop(0, units[e], body, 0)


def _window_onehot(posm_ref, e, first, lo):
    slot = lax.broadcasted_iota(jnp.int32, (WIN, ROWS), 0) + first
    p = posm_ref[e:e + 1, :].astype(jnp.int32)
    return jnp.where((p == slot) & (p >= lo), 1.0, 0.0)


def _gate_lanes(e):
    lane = lax.broadcasted_iota(jnp.int32, (1, LANES), 1)
    return ((lane & (N_EXPERTS - 1)) == e) & (lane < 3 * N_EXPERTS)


def _dispatch_kernel(s_ref, h_ref, posm_ref, xe_ref, sel_s, x_s, sem, *, n_blk, cap_pad):
    b = pl.program_id(0)
    buf = b % 2
    base, units, first_unit, n_units = _block_units(s_ref, b)

    @pl.when(b == 0)
    def _():
        sel_s[...] = jnp.zeros_like(sel_s)

    def unit_rows(first_unit_e, u):
        return pl.ds(pl.multiple_of((first_unit_e + u) * WIN, WIN), WIN)

    def pick(e, u):
        first = base[e] + u * WIN
        sel_s[unit_rows(first_unit[e], u), :] = _window_onehot(posm_ref, e, first, first).astype(BF16)

    _for_units(units, pick)

    def gather(rows):
        x_s[buf, rows, :] = _dot(sel_s[rows, :], h_ref[...]).astype(BF16)

    def gather_chunk(c, _):
        gather(pl.ds(pl.multiple_of(c * ROWS, ROWS), ROWS))
        return 0

    gather(pl.ds(0, COMMON_ROWS))
    lax.fori_loop(COMMON_ROWS // ROWS, _ceil_div_pow2(n_units * WIN, ROWS), gather_chunk, 0)

    def unit_copy(slot, blk_base_e, blk_first_e, e, u):
        dst = pl.ds(pl.multiple_of(blk_base_e + u * WIN, ALIGN), WIN)
        return pltpu.make_async_copy(x_s.at[slot, unit_rows(blk_first_e, u), :], xe_ref.at[e, dst, :],
                                     sem.at[slot, e])

    def for_expert_units(n, fn):
        lax.fori_loop(0, n, lambda u, _: (fn(u), 0)[1], 0)

    prev_base, prev_units, prev_first, _ = _block_units(s_ref, jnp.maximum(b - 1, 0))
    for e in range(N_EXPERTS):
        for_expert_units(jnp.where(b > 0, prev_units[e], 0),
                         lambda u, e=e: unit_copy(1 - buf, prev_base[e], prev_first[e], e, u).wait())
        for_expert_units(units[e], lambda u, e=e: unit_copy(buf, base[e], first_unit[e], e, u).start())

    @pl.when(b == n_blk - 1)
    def _():
        for e in range(N_EXPERTS):
            for_expert_units(units[e], lambda u, e=e: unit_copy(buf, base[e], first_unit[e], e, u).wait())
        x_s[buf, 0:WIN, :] = jnp.zeros((WIN, XE_W), BF16)

        def tail(fn):
            for e in range(N_EXPERTS):
                total = s_ref[e, n_blk]
                n_win = (cap_pad - total) // WIN

                def wide(c, _):
                    row = pl.multiple_of(total + c * WIN, ALIGN)
                    fn(pltpu.make_async_copy(x_s.at[buf, pl.ds(0, WIN), :],
                                             xe_ref.at[e, pl.ds(row, WIN), :], sem.at[buf, e]))
                    return 0

                def narrow(c, _):
                    row = pl.multiple_of(total + n_win * WIN + c * ALIGN, ALIGN)
                    fn(pltpu.make_async_copy(x_s.at[buf, pl.ds(0, ALIGN), :],
                                             xe_ref.at[e, pl.ds(row, ALIGN), :], sem.at[buf, e]))
                    return 0

                lax.fori_loop(0, n_win, wide, 0)
                lax.fori_loop(0, (cap_pad - total - n_win * WIN) // ALIGN, narrow, 0)

        tail(lambda c: c.start())
        tail(lambda c: c.wait())


def _ffn_kernel(*refs, n_blks):
    n_sets = len(n_blks)
    s_refs, x_refs = refs[:n_sets], refs[n_sets:2 * n_sets]
    wg_ref, wu_ref, wd_ref = refs[2 * n_sets:2 * n_sets + 3]
    y_refs = refs[2 * n_sets + 3:3 * n_sets + 3]
    wg_s, wu_s, wd_s = refs[3 * n_sets + 3:]
    e = pl.program_id(0)
    wg_s[...] = wg_ref[...].astype(BF16)
    wu_s[...] = wu_ref[...].astype(BF16)
    wd_s[...] = wd_ref[...].astype(BF16)
    mine = _gate_lanes(e)

    half = ROWS // 2
    for s_ref, x_ref, y_ref, n_blk in zip(s_refs, x_refs, y_refs, n_blks):
        n_half = _ceil_div_pow2(s_ref[e, n_blk], half)

        def tile(rows, x_ref=x_ref, y_ref=y_ref):
            x = x_ref[rows, 0:D_MODEL]
            pieces = x_ref[rows, D_MODEL:XE_W].astype(F32)
            gate = jnp.sum(jnp.where(mine, pieces, 0.0), axis=1, keepdims=True)
            a = _dot(x, wg_s[...])
            u = _dot(x, wu_s[...])
            hid = a * (1.0 / (1.0 + jnp.exp(-a))) * u * gate
            y_ref[rows, :] = _dot(hid.astype(BF16), wd_s[...]).astype(BF16)

        def full(j, _, tile=tile):
            tile(pl.ds(pl.multiple_of(j * WIDE_ROWS, WIDE_ROWS), WIDE_ROWS))
            return 0

        def blank(j, _, y_ref=y_ref):
            y_ref[pl.ds(pl.multiple_of(j * half, half), half), :] = jnp.zeros((half, D_MODEL), BF16)
            return 0

        lax.fori_loop(0, lax.shift_right_logical(n_half, 2), full, 0)

        @pl.when((n_half & 2) != 0)
        def _(tile=tile, n_half=n_half):
            tile(pl.ds(pl.multiple_of((n_half & -4) * half, WIDE_ROWS), ROWS))

        @pl.when((n_half & 1) != 0)
        def _(tile=tile, n_half=n_half):
            tile(pl.ds(pl.multiple_of((n_half - 1) * half, half), half))

        lax.fori_loop(n_half, y_ref.shape[0] // half, blank, 0)


def _combine_kernel(s_ref, y_ref, posm_ref, x1_ref, mod_ref, o_ref, yw_s, sel_s, acc_s, sem,
                    *, n_blk, cap_pad):
    b = pl.program_id(0)
    buf = b % 2
    base, units, first_unit, n_units = _block_units(s_ref, b)

    def unit_rows(first_unit_e, u):
        return pl.ds(pl.multiple_of((first_unit_e + u) * WIN, WIN), WIN)

    def bounds(base_e, u):
        lo = base_e + u * WIN
        return lo, jnp.minimum(lo, cap_pad - WIN)

    def copies(slot, blk, fn):
        blk_base, blk_units, blk_first, _ = _block_units(s_ref, blk)

        def one(e, u):
            src = pl.ds(pl.multiple_of(bounds(blk_base[e], u)[1], ALIGN), WIN)
            fn(pltpu.make_async_copy(y_ref.at[e, src, :], yw_s.at[slot, unit_rows(blk_first[e], u), :],
                                     sem.at[slot]))

        _for_units(blk_units, one)

    @pl.when(b == 0)
    def _():
        yw_s[...] = jnp.zeros_like(yw_s)
        copies(0, 0, lambda c: c.start())

    @pl.when(b + 1 < n_blk)
    def _():
        copies(1 - buf, b + 1, lambda c: c.start())

    def pick(e, u):
        lo, first = bounds(base[e], u)
        sel_s[unit_rows(first_unit[e], u), :] = _window_onehot(posm_ref, e, first, lo)

    _for_units(units, pick)
    n_chunks = jnp.maximum(_ceil_div_pow2(n_units * WIN, ROWS), COMMON_ROWS // ROWS)

    def blank(u, _):
        sel_s[unit_rows(u, 0), :] = jnp.zeros((WIN, ROWS), F32)
        return 0

    lax.fori_loop(n_units, n_chunks * (ROWS // WIN), blank, 0)
    copies(buf, b, lambda c: c.wait())

    def weighted(rows):
        return _dot(sel_s[rows, :].T.astype(BF16), yw_s[buf, rows, :])

    def add_chunk(c, _):
        acc_s[...] += weighted(pl.ds(pl.multiple_of(c * ROWS, ROWS), ROWS))
        return 0

    acc_s[...] = weighted(pl.ds(0, COMMON_ROWS))
    lax.fori_loop(COMMON_ROWS // ROWS, n_chunks, add_chunk, 0)
    o_ref[...] = x1_ref[...] + mod_ref[5:6, :] * acc_s[...]


def _padded_capacity(cap, n_blk):
    return -(-(cap + (ALIGN - 1) * n_blk + WIN) // ROWS) * ROWS


def _block_spec(*shape):
    return pl.BlockSpec((None,) + shape, lambda b, s: (b,) + (0,) * len(shape))


def _dispatch(h2, posm, s_tab, cap):
    n_blk = h2.shape[0] // ROWS
    cap_pad = _padded_capacity(cap, n_blk)
    return pl.pallas_call(
        functools.partial(_dispatch_kernel, n_blk=n_blk, cap_pad=cap_pad),
        out_shape=jax.ShapeDtypeStruct((N_EXPERTS, cap_pad, XE_W), BF16),
        grid_spec=pltpu.PrefetchScalarGridSpec(
            num_scalar_prefetch=1, grid=(n_blk,),
            in_specs=[pl.BlockSpec((ROWS, XE_W), lambda b, s: (b, 0)), _block_spec(N_EXPERTS, ROWS)],
            out_specs=pl.BlockSpec(memory_space=pl.ANY),
            scratch_shapes=[pltpu.VMEM((MAX_UNITS * WIN, ROWS), BF16), pltpu.VMEM((2, MAX_UNITS * WIN, XE_W), BF16),
                            pltpu.SemaphoreType.DMA((2, N_EXPERTS))]),
        compiler_params=_params(1), name="dispatch",
    )(s_tab, h2, posm)


def _ffn(xes, s_tabs, wg, wu, wd):
    n_sets = len(xes)
    per_expert = lambda rows, cols: pl.BlockSpec((None, rows, cols), lambda e, *s: (e, 0, 0))
    return pl.pallas_call(
        functools.partial(_ffn_kernel, n_blks=tuple(s.shape[1] - 1 for s in s_tabs)),
        out_shape=[jax.ShapeDtypeStruct(xe.shape[:2] + (D_MODEL,), BF16) for xe in xes],
        grid_spec=pltpu.PrefetchScalarGridSpec(
            num_scalar_prefetch=n_sets, grid=(N_EXPERTS,),
            in_specs=[per_expert(xe.shape[1], XE_W) for xe in xes]
            + [per_expert(D_MODEL, EXPERT_FF), per_expert(D_MODEL, EXPERT_FF), per_expert(EXPERT_FF, D_MODEL)],
            out_specs=[per_expert(xe.shape[1], D_MODEL) for xe in xes],
            scratch_shapes=[pltpu.VMEM((D_MODEL, EXPERT_FF), BF16), pltpu.VMEM((D_MODEL, EXPERT_FF), BF16),
                            pltpu.VMEM((EXPERT_FF, D_MODEL), BF16)]),
        compiler_params=_params(1), name="ffn",
    )(*s_tabs, *xes, wg, wu, wd)


def _combine(y, posm, s_tab, x1, mod, n_tok):
    n_all = x1.shape[0]
    n_blk = n_all // ROWS
    tiles_per_seq = n_tok // ROWS
    mod_map = (lambda b, s: (0, 0, 0)) if mod.shape[0] == 1 else (lambda b, s: (b // tiles_per_seq, 0, 0))
    rows = pl.BlockSpec((ROWS, D_MODEL), lambda b, s: (b, 0))
    return pl.pallas_call(
        functools.partial(_combine_kernel, n_blk=n_blk, cap_pad=y.shape[1]),
        out_shape=jax.ShapeDtypeStruct((n_all, D_MODEL), F32),
        grid_spec=pltpu.PrefetchScalarGridSpec(
            num_scalar_prefetch=1, grid=(n_blk,),
            in_specs=[pl.BlockSpec(memory_space=pl.ANY), _block_spec(N_EXPERTS, ROWS), rows,
                      pl.BlockSpec((None, 6, D_MODEL), mod_map)],
            out_specs=rows,
            scratch_shapes=[pltpu.VMEM((2, MAX_UNITS * WIN, D_MODEL), BF16), pltpu.VMEM((MAX_UNITS * WIN, ROWS), F32),
                            pltpu.VMEM((ROWS, D_MODEL), F32), pltpu.SemaphoreType.DMA((2,))]),
        compiler_params=_params(1), name="combine",
    )(s_tab, y, posm, x1, mod)


def _rope_tables(n_tok):
    rows = n_tok // GRID_W
    row = np.repeat(np.arange(rows, dtype=np.float64), GRID_W)
    col = np.tile(np.arange(GRID_W, dtype=np.float64), rows)
    axis_dim = QK_ROPE // 2
    inv_freq = ROPE_THETA ** (-np.arange(0, axis_dim, 2, dtype=np.float64) / axis_dim)
    ang = np.concatenate([row[:, None] * inv_freq, col[:, None] * inv_freq], axis=-1)
    cos = np.ones((n_tok, LANES))
    sin = np.zeros((n_tok, LANES))
    cos[:, ROPE_LANE0:ROPE_LANE0 + QK_ROPE] = np.repeat(np.cos(ang), 2, axis=1)
    sgn = np.tile(np.array([-1.0, 1.0]), QK_ROPE // 2)
    sin[:, ROPE_LANE0:ROPE_LANE0 + QK_ROPE] = np.repeat(np.sin(ang), 2, axis=1) * sgn
    return jnp.asarray(cos, F32), jnp.asarray(sin, F32)


def _dft_tables(n_tok):
    def cs(n):
        k = np.arange(n)
        ang = 2.0 * np.pi * ((k[:, None] * k[None, :]) % n) / n
        return np.cos(ang), np.sin(ang)
    cc, sc = cs(FNET_CH)
    cn, sn = cs(n_tok)
    return tuple(jnp.asarray(t, F32) for t in (cc, sc, np.concatenate([cn, -sn], axis=1)))


def _pad_heads(w, width):
    lead = w.shape[:-1]
    w = w.reshape(lead + (N_HEADS, width))
    w = jnp.pad(w, [(0, 0)] * len(lead) + [(0, 0), (0, HEAD_PAD - width)])
    return w.reshape(lead + (N_HEADS * HEAD_PAD,))


def _pair_swap_lanes(n_groups):
    perm = np.arange(n_groups * LANES).reshape(n_groups, LANES)
    rot = perm[:, ROPE_LANE0:ROPE_LANE0 + QK_ROPE].reshape(n_groups, QK_ROPE // 2, 2)[:, :, ::-1]
    perm[:, ROPE_LANE0:ROPE_LANE0 + QK_ROPE] = rot.reshape(n_groups, QK_ROPE)
    return perm.reshape(-1)


def _spare_from_rotary(t, src=None):
    src = t if src is None else src
    lead = t.shape[:-1]
    t, src = t.reshape(lead + (-1, LANES)), src.reshape(lead + (-1, LANES))
    t = jnp.concatenate([t[..., :QK_DIM], src[..., ROPE_LANE0:ROPE_LANE0 + QK_ROPE]], axis=-1)
    return t.reshape(lead + (-1,))


def _head_tables(gain, rope_tabs, n_tok, query):
    if rope_tabs is None:
        tab = jnp.broadcast_to(gain, (n_tok, LANES))
        return (tab if query else _spare_from_rotary(tab),)
    cos, sin = rope_tabs
    a, b = gain * cos, gain[:, _pair_swap_lanes(1)] * sin
    return (_spare_from_rotary(a, b),) if query else (_spare_from_rotary(a), _spare_from_rotary(b))


def _layer0_weights(norm1, norm2, w_in, q_a_norm, w_q_up, q_norm, kv_a_norm, w_kv_up, k_norm, conv_w, w_o):
    c0 = Q_LORA + KV_LORA
    rope_cols = jnp.pad(w_in[:, c0:c0 + QK_ROPE], ((0, 0), (ROPE_LANE0, LANES - ROPE_LANE0 - QK_ROPE)))
    w_in_pad = jnp.concatenate([w_in[:, :c0], _spare_from_rotary(rope_cols),
                                _spare_from_rotary(rope_cols[:, _pair_swap_lanes(1)]),
                                w_in[:, c0 + QK_ROPE:]], axis=1)
    kv = w_kv_up.reshape(KV_LORA, N_HEADS, QK_NOPE + V_HEAD)
    w_q = _pad_heads(w_q_up, QK_DIM)
    w_q = _spare_from_rotary(w_q, w_q[:, _pair_swap_lanes(N_HEADS)])
    head_gain = lambda g: jnp.pad(g, (0, HEAD_PAD - QK_DIM)).reshape(1, -1)
    return dict(
        norm1=norm1.reshape(1, -1), norm2=norm2.reshape(1, -1), w_in=w_in_pad.astype(BF16),
        q_a_norm=q_a_norm.reshape(1, -1), w_q=w_q.astype(BF16),
        q_gain=head_gain(q_norm) * (QK_DIM ** -0.5), kv_a_norm=kv_a_norm.reshape(1, -1),
        w_k=_pad_heads(kv[:, :, :QK_NOPE].reshape(KV_LORA, -1), QK_NOPE).astype(BF16),
        w_v=kv[:, :, QK_NOPE:].reshape(KV_LORA, -1).astype(BF16),
        w_v_wide=_pad_heads(kv[:, :, QK_NOPE:].reshape(KV_LORA, -1), V_HEAD).astype(BF16), k_gain=head_gain(k_norm),
        conv_w=conv_w, w_o=w_o.astype(BF16))


def _router_weights(w_router):
    hi, lo = _split_hi_lo(jnp.pad(w_router, ((0, 0), (0, LANES - N_EXPERTS))))
    return dict(wr_hi=hi, wr_lo=lo)


def _moe(sets, tri, w):
    routed = []
    for x1, h2, aff, mod, n_tok in sets:
        cap = CAPACITY_FACTOR * x1.shape[0] // N_EXPERTS
        posm, s_tab = _route(aff, tri, cap)
        routed.append((posm, s_tab, _dispatch(h2, posm, s_tab, cap)))
    ys = _ffn([r[2] for r in routed], [r[1] for r in routed], w["wg"], w["wu"], w["wd"])
    return [_combine(y, posm, s_tab, x1, mod, n_tok)
            for y, (posm, s_tab, _), (x1, _, _, mod, n_tok) in zip(ys, routed, sets)]


def _mixer0(x, mod, l0, ctx, rope_tabs):
    n_b, n_tok, _ = x.shape
    n_keys = n_tok + (0 if ctx is None else ctx[0].shape[1])
    w_v = l0["w_v_wide"] if n_keys > WIDE_ROWS else l0["w_v"]
    q, k, v, conv, ckv, kr = _front0(x, mod, l0, rope_tabs, w_v)
    if ctx is not None:
        kc, vc = _ctx_kv(ctx[0], ctx[1], l0, w_v)
        attn = _attention(q, k, v, kc, vc)
    else:
        attn = _attention(q, k, v)
    flat = lambda a: a.reshape(n_b * n_tok, a.shape[-1])
    return _post0(flat(x), flat(attn), flat(conv), mod, l0, n_tok), (ckv, kr[:, :, ROPE_LANE0:ROPE_LANE0 + QK_ROPE])


def kernel(x_prompt, x_sample, c, cache_c_kv_l0, cache_k_rope_l0, c_ctx, norm1_l0, norm2_l0, w_mod_l0, b_mod_l0, w_in_l0, q_a_norm_l0, w_q_up_l0, q_norm_l0, kv_a_norm_l0, w_kv_up_l0, k_norm_l0, conv_w_l0, w_o_l0, w_router_l0, w_gate_l0, w_up_l0, w_down_l0, norm1_l1, norm2_l1, w_mod_l1, b_mod_l1, w_f_l1, w_router_l1, w_gate_l1, w_up_l1, w_down_l1):
    n_dec = c.shape[0]
    cond = jnp.concatenate([c_ctx[None, :], c, jnp.zeros((16 - 1 - n_dec, D_MODEL), F32)], axis=0)
    m0 = _modulation(cond, w_mod_l0, b_mod_l0)
    m1 = _modulation(cond, w_mod_l1, b_mod_l1)
    mods_prompt = (m0[0:1], m1[0:1])
    mods_sample = (m0[1:1 + n_dec], m1[1:1 + n_dec])

    l0 = _layer0_weights(norm1_l0, norm2_l0, w_in_l0, q_a_norm_l0, w_q_up_l0, q_norm_l0, kv_a_norm_l0,
                         w_kv_up_l0, k_norm_l0, conv_w_l0, w_o_l0)
    l0.update(_router_weights(w_router_l0))
    l0.update(wg=w_gate_l0, wu=w_up_l0, wd=w_down_l0)
    l1 = dict(norm1=norm1_l1.reshape(1, -1), norm2=norm2_l1.reshape(1, -1), w_f=w_f_l1.astype(BF16))
    l1.update(_router_weights(w_router_l1))
    l1.update(wg=w_gate_l1, wu=w_up_l1, wd=w_down_l1)

    tri = jnp.asarray(np.triu(np.ones((ROWS, ROWS)), 1), BF16)
    k_rope_pad = _spare_from_rotary(
        jnp.pad(cache_k_rope_l0, ((0, 0), (0, 0), (ROPE_LANE0, LANES - ROPE_LANE0 - QK_ROPE))))
    xs = (x_prompt, x_sample)
    mods = (mods_prompt, mods_sample)
    n_toks = tuple(x.shape[1] for x in xs)

    (front_p, (new_c_kv, new_k_rope)) = _mixer0(x_prompt, mods_prompt[0], l0, None, None)
    (front_s, _) = _mixer0(x_sample, mods_sample[0], l0, (cache_c_kv_l0, k_rope_pad), _rope_tables(n_toks[1]))
    ys = _moe([tuple(front) + (mod[0], n_tok) for front, mod, n_tok in zip((front_p, front_s), mods, n_toks)],
              tri, l0)

    sets = []
    for y, x, mod, n_tok in zip(ys, xs, mods, n_toks):
        x1, h2, aff = _fnet(y.reshape(x.shape), mod[1], l1, _dft_tables(n_tok))
        sets.append((x1.reshape(-1, D_MODEL), h2.reshape(-1, XE_W), aff, mod[1], n_tok))
    y_prompt, y_sample = (y.reshape(x.shape) for y, x in zip(_moe(sets, tri, l1), xs))
    return (y_prompt, y_sample, new_c_kv, new_k_rope)
```

```python
import functools

import jax
import jax.numpy as jnp
import numpy as np
from jax import lax
from jax.experimental import pallas as pl
from jax.experimental.pallas import tpu as pltpu

D_MODEL = 1024
GRID_W = 64
N_HEADS = 8
QK_NOPE = 64
QK_ROPE = 32
QK_DIM = QK_NOPE + QK_ROPE
V_HEAD = 64
Q_LORA = 384
KV_LORA = 256
CONV_CH = 512
FNET_GROUPS = 4
FNET_CH = D_MODEL // FNET_GROUPS
N_EXPERTS = 16
EXPERT_FF = 512
CAPACITY_FACTOR = 2
ROPE_THETA = 10000.0
EPS = 1e-6

LANES = 128
HEAD_PAD = LANES
ROWS = 256
WIDE_ROWS = 512
ALIGN = 8
WIN = 64
MAX_UNITS = N_EXPERTS * (ROWS // WIN)
COMMON_ROWS = N_EXPERTS * WIN
XE_W = D_MODEL + LANES
IN0_PAD = Q_LORA + KV_LORA + 2 * LANES + 3 * CONV_CH
ROPE_LANE0 = QK_NOPE
VMEM_LIMIT = 56 * 1024 * 1024

F32 = jnp.float32
BF16 = jnp.bfloat16


def _dot(a, b):
    return jnp.dot(a, b, preferred_element_type=F32)


def _dot_nt(a, b):
    return lax.dot_general(a, b, (((1,), (1,)), ((), ())), preferred_element_type=F32)


def _split_hi_lo(x):
    hi = x.astype(BF16)
    lo = (x - hi.astype(F32)).astype(BF16)
    return hi, lo


def _params(n_axes):
    return pltpu.CompilerParams(dimension_semantics=("arbitrary",) * n_axes,
                                vmem_limit_bytes=VMEM_LIMIT)


def _rms(x, gain):
    return x * lax.rsqrt(jnp.mean(x * x, axis=-1, keepdims=True) + EPS) * gain


def _mod_kernel(cond_ref, w_ref, b_ref, o_ref):
    c = cond_ref[...]
    s = c * (1.0 / (1.0 + jnp.exp(-c)))
    s_hi, s_lo = _split_hi_lo(s)
    w = w_ref[...].astype(BF16)
    o_ref[...] = _dot(s_hi, w) + _dot(s_lo, w) + b_ref[...]


def _modulation(cond, w_mod, b_mod):
    n_rows = cond.shape[0]
    tn = 1536
    out = pl.pallas_call(
        _mod_kernel,
        out_shape=jax.ShapeDtypeStruct((n_rows, 6 * D_MODEL), F32),
        grid=(6 * D_MODEL // tn,),
        in_specs=[pl.BlockSpec((n_rows, D_MODEL), lambda i: (0, 0)),
                  pl.BlockSpec((D_MODEL, tn), lambda i: (0, i)),
                  pl.BlockSpec((1, tn), lambda i: (0, i))],
        out_specs=pl.BlockSpec((n_rows, tn), lambda i: (0, i)),
        compiler_params=_params(1),
        name="modulation",
    )(cond, w_mod, b_mod.reshape(1, -1))
    return out.reshape(n_rows, 6, D_MODEL)


def _head_norm_rope(xh, gain_cos, swapped_sin):
    lane = lax.broadcasted_iota(jnp.int32, (1, HEAD_PAD), 1)
    ss = jnp.sum(jnp.where(lane < QK_DIM, xh * xh, 0.0), axis=-1, keepdims=True) * (1.0 / QK_DIM)
    y = xh * gain_cos
    if swapped_sin is not None:
        y = y + swapped_sin
    return y * lax.rsqrt(ss + EPS)


def _expand_kv(ckv_bf, r, swapped_sin, wk_ref, wv_ref, gain_cos, k_ref, v_ref, rows):
    kf = _dot(ckv_bf, wk_ref[...])
    vf = _dot(ckv_bf, wv_ref[...])
    if v_ref.shape[-1] == N_HEADS * HEAD_PAD:
        lane = lax.broadcasted_iota(jnp.int32, (1, N_HEADS * HEAD_PAD), 1)
        vf = vf + jnp.where((lane & (HEAD_PAD - 1)) == V_HEAD, 1.0, 0.0)
    v_ref[rows, :] = vf.astype(BF16)
    for h in range(N_HEADS):
        sl = slice(h * HEAD_PAD, (h + 1) * HEAD_PAD)
        k_ref[rows, sl] = _head_norm_rope(kf[:, sl] + r, gain_cos, swapped_sin).astype(BF16)


def _front0_kernel(*refs, n_tok, rope):
    (x_ref, mod_ref, n1_ref, win_ref, qan_ref, wq_ref, kvan_ref, wk_ref, wv_ref, cw_ref,
     qa_ref, ka_ref) = refs[:12]
    pos = 12
    if rope:
        kb_ref = refs[pos]
        pos += 1
    q_ref, k_ref, v_ref, conv_ref, ckv_ref, kr_ref, cu_s, gb_s = refs[pos:]
    sh1, sc1 = mod_ref[0:1, :], mod_ref[1:2, :]
    c_rope = Q_LORA + KV_LORA
    wide = min(n_tok, WIDE_ROWS)
    for c in range(n_tok // ROWS):
        rows = pl.ds(c * ROWS, ROWS)
        if (c * ROWS) % wide == 0:
            wide_rows = pl.ds(c * ROWS, wide)
            h = _rms(x_ref[wide_rows, :], n1_ref[...]) * (1.0 + sc1) + sh1
            proj_wide = _dot(h.astype(BF16), win_ref[...])
        off = (c * ROWS) % wide
        proj = proj_wide[off:off + ROWS, :]
        cq = _rms(proj[:, :Q_LORA], qan_ref[...]).astype(BF16)
        qf = _dot(cq, wq_ref[...])
        for hd in range(N_HEADS):
            sl = slice(hd * HEAD_PAD, (hd + 1) * HEAD_PAD)
            q_ref[rows, sl] = _head_norm_rope(qf[:, sl], qa_ref[rows, :], None).astype(BF16)
        ckv = _rms(proj[:, Q_LORA:c_rope], kvan_ref[...])
        ckv_ref[rows, :] = ckv
        r = proj[:, c_rope:c_rope + LANES]
        kr_ref[rows, :] = r
        k_sin = proj[:, c_rope + LANES:c_rope + 2 * LANES] * kb_ref[rows, :] if rope else None
        _expand_kv(ckv.astype(BF16), r, k_sin, wk_ref, wv_ref, ka_ref[rows, :], k_ref, v_ref, rows)
        c0 = c_rope + 2 * LANES
        gb_s[rows, :] = proj[:, c0:c0 + CONV_CH]
        cu_s[rows, :] = proj[:, c0 + CONV_CH:c0 + 2 * CONV_CH] * proj[:, c0 + 2 * CONV_CH:c0 + 3 * CONV_CH]
    cu = cu_s[...]
    row = lax.broadcasted_iota(jnp.int32, cu.shape, 0)
    prev = jnp.where(row == 0, 0.0, pltpu.roll(cu, 1, 0))
    nxt = jnp.where(row == n_tok - 1, 0.0, pltpu.roll(cu, n_tok - 1, 0))
    conv = gb_s[...] * (cw_ref[0:1, :] * prev + cw_ref[1:2, :] * cu + cw_ref[2:3, :] * nxt)
    conv_ref[...] = conv.astype(BF16)


def _const_spec(shape):
    return pl.BlockSpec(shape, lambda b: (0,) * len(shape))


def _front0(x, mod, w, rope_tabs, w_v):
    n_b, n_tok, _ = x.shape
    v_width = w_v.shape[1]
    shared_mod = mod.shape[0] == 1
    seq = lambda width: pl.BlockSpec((None, n_tok, width), lambda b: (b, 0, 0))
    in_specs = [seq(D_MODEL),
                pl.BlockSpec((None, 6, D_MODEL), (lambda b: (0, 0, 0)) if shared_mod else (lambda b: (b, 0, 0))),
                _const_spec((1, D_MODEL)), _const_spec((D_MODEL, IN0_PAD)), _const_spec((1, Q_LORA)),
                _const_spec((Q_LORA, N_HEADS * HEAD_PAD)),
                _const_spec((1, KV_LORA)), _const_spec((KV_LORA, N_HEADS * HEAD_PAD)),
                _const_spec((KV_LORA, v_width)), _const_spec((3, CONV_CH)),
                _const_spec((n_tok, LANES)), _const_spec((n_tok, LANES))]
    q_tabs = _head_tables(w["q_gain"], rope_tabs, n_tok, True)
    k_tabs = _head_tables(w["k_gain"], rope_tabs, n_tok, False)
    args = [x, mod, w["norm1"], w["w_in"], w["q_a_norm"], w["w_q"], w["kv_a_norm"],
            w["w_k"], w_v, w["conv_w"], q_tabs[0], k_tabs[0]]
    if rope_tabs is not None:
        in_specs += [_const_spec((n_tok, LANES))]
        args += [k_tabs[1]]
    out_shape = [jax.ShapeDtypeStruct((n_b, n_tok, N_HEADS * HEAD_PAD), BF16),
                 jax.ShapeDtypeStruct((n_b, n_tok, N_HEADS * HEAD_PAD), BF16),
                 jax.ShapeDtypeStruct((n_b, n_tok, v_width), BF16),
                 jax.ShapeDtypeStruct((n_b, n_tok, CONV_CH), BF16),
                 jax.ShapeDtypeStruct((n_b, n_tok, KV_LORA), F32),
                 jax.ShapeDtypeStruct((n_b, n_tok, LANES), F32)]
    out_specs = [seq(N_HEADS * HEAD_PAD), seq(N_HEADS * HEAD_PAD), seq(v_width), seq(CONV_CH),
                 seq(KV_LORA), seq(LANES)]
    return pl.pallas_call(
        functools.partial(_front0_kernel, n_tok=n_tok, rope=rope_tabs is not None),
        out_shape=out_shape, grid=(n_b,), in_specs=in_specs, out_specs=out_specs,
        scratch_shapes=[pltpu.VMEM((n_tok, CONV_CH), F32), pltpu.VMEM((n_tok, CONV_CH), F32)],
        compiler_params=_params(1), name="front0",
    )(*args)


def _ctx_kv_kernel(ckv_ref, r_ref, wk_ref, wv_ref, ka_ref, k_ref, v_ref):
    rows = pl.ds(0, ckv_ref.shape[0])
    _expand_kv(ckv_ref[...].astype(BF16), r_ref[...], None, wk_ref, wv_ref, ka_ref[...], k_ref, v_ref, rows)


def _ctx_kv(cache_c_kv, cache_k_rope_pad, w, w_v):
    n_b, n_ctx, _ = cache_c_kv.shape
    v_width = w_v.shape[1]
    seq = lambda width: pl.BlockSpec((None, n_ctx, width), lambda b: (b, 0, 0))
    return pl.pallas_call(
        _ctx_kv_kernel,
        out_shape=[jax.ShapeDtypeStruct((n_b, n_ctx, N_HEADS * HEAD_PAD), BF16),
                   jax.ShapeDtypeStruct((n_b, n_ctx, v_width), BF16)],
        grid=(n_b,),
        in_specs=[seq(KV_LORA), seq(LANES), _const_spec((KV_LORA, N_HEADS * HEAD_PAD)),
                  _const_spec((KV_LORA, v_width)), _const_spec((n_ctx, LANES))],
        out_specs=[seq(N_HEADS * HEAD_PAD), seq(v_width)],
        compiler_params=_params(1), name="ctx_kv",
    )(cache_c_kv, cache_k_rope_pad, w["w_k"], w_v, _head_tables(w["k_gain"], None, n_ctx, False)[0])


def _attn_kernel(*refs, with_ctx):
    if with_ctx:
        q_ref, k_ref, v_ref, kc_ref, vc_ref, o_ref = refs
    else:
        q_ref, k_ref, v_ref, o_ref = refs
    ones_lane = v_ref.shape[-1] == N_HEADS * HEAD_PAD
    lane = lax.broadcasted_iota(jnp.int32, (q_ref.shape[0], LANES), 1)
    for pair in range(N_HEADS // 2):
        outs = []
        for hd in (2 * pair, 2 * pair + 1):
            sl = slice(hd * HEAD_PAD, (hd + 1) * HEAD_PAD)
            vsl = sl if ones_lane else slice(pair * LANES, (pair + 1) * LANES)
            qh = q_ref[:, sl]
            s = _dot_nt(qh, k_ref[:, sl])
            m = jnp.max(s, axis=-1, keepdims=True)
            if with_ctx:
                sc = _dot_nt(qh, kc_ref[:, sl])
                m = jnp.maximum(m, jnp.max(sc, axis=-1, keepdims=True))
            p = jnp.exp(s - m)
            o = _dot(p.astype(BF16), v_ref[:, vsl])
            if with_ctx:
                pc = jnp.exp(sc - m)
                o = o + _dot(pc.astype(BF16), vc_ref[:, vsl])
            if ones_lane:
                den = o[:, V_HEAD:V_HEAD + 1]
            else:
                den = jnp.sum(p, axis=-1, keepdims=True)
                if with_ctx:
                    den = den + jnp.sum(pc, axis=-1, keepdims=True)
            outs.append(o / den)
        odd = pltpu.roll(outs[1], V_HEAD, 1) if ones_lane else outs[1]
        o_ref[:, pair * LANES:(pair + 1) * LANES] = jnp.where(lane < V_HEAD, outs[0], odd).astype(BF16)


def _attention(q, k, v, kc=None, vc=None):
    n_b, n_tok, _ = q.shape
    with_ctx = kc is not None
    tq = min(n_tok, WIDE_ROWS)
    qspec = lambda width: pl.BlockSpec((None, tq, width), lambda b, i: (b, i, 0))
    kvspec = lambda a: pl.BlockSpec((None,) + a.shape[1:], lambda b, i: (b, 0, 0))
    in_specs = [qspec(N_HEADS * HEAD_PAD), kvspec(k), kvspec(v)]
    args = [q, k, v]
    if with_ctx:
        in_specs += [kvspec(kc), kvspec(vc)]
        args += [kc, vc]
    return pl.pallas_call(
        functools.partial(_attn_kernel, with_ctx=with_ctx),
        out_shape=jax.ShapeDtypeStruct((n_b, n_tok, N_HEADS * V_HEAD), BF16),
        grid=(n_b, n_tok // tq), in_specs=in_specs, out_specs=qspec(N_HEADS * V_HEAD),
        compiler_params=_params(2), name="attention",
    )(*args)


def _moe_front(x1, mod_ref, n2_ref, wrh_ref, wrl_ref, h2_ref, aff_ref, rows):
    sh2, sc2 = mod_ref[3:4, :], mod_ref[4:5, :]
    h2 = _rms(x1, n2_ref[...]) * (1.0 + sc2) + sh2
    h_hi, h_lo = _split_hi_lo(h2)
    logit = _dot(h_hi, wrh_ref[...]) + _dot(h_lo, wrh_ref[...]) + _dot(h_hi, wrl_ref[...])
    real = lax.broadcasted_iota(jnp.int32, (1, LANES), 1) < N_EXPERTS
    top = jnp.max(jnp.where(real, logit, -jnp.inf), axis=1, keepdims=True)
    e = jnp.where(real, jnp.exp(logit - top), 0.0)
    a = e / jnp.sum(e, axis=1, keepdims=True)
    a_hi = a.astype(BF16).astype(F32)
    a_mid = (a - a_hi).astype(BF16).astype(F32)
    a_lo = a - a_hi - a_mid
    h2_ref[rows, 0:D_MODEL] = h_hi
    h2_ref[rows, D_MODEL:XE_W] = (a_hi + pltpu.roll(a_mid, N_EXPERTS, 1)
                                  + pltpu.roll(a_lo, 2 * N_EXPERTS, 1)).astype(BF16)
    aff_ref[...] = a.T[0:N_EXPERTS, :]


def _post0_kernel(x_ref, attn_ref, conv_ref, wo_ref, mod_ref, n2_ref, wrh_ref, wrl_ref,
                  x1_ref, h2_ref, aff_ref):
    n_attn = N_HEADS * V_HEAD
    mix = _dot(attn_ref[...], wo_ref[0:n_attn, :]) + _dot(conv_ref[...], wo_ref[n_attn:, :])
    for t in range(x_ref.shape[0] // ROWS):
        rows = pl.ds(t * ROWS, ROWS)
        x1 = x_ref[rows, :] + mod_ref[2:3, :] * mix[t * ROWS:(t + 1) * ROWS, :]
        x1_ref[rows, :] = x1
        _moe_front(x1, mod_ref, n2_ref, wrh_ref, wrl_ref, h2_ref, aff_ref.at[t], rows)


def _post0(x, attn, conv, mod, w, n_tok):
    n_all = x.shape[0]
    rows = min(n_tok, WIDE_ROWS)
    tiles_per_seq = n_tok // rows
    shared_mod = mod.shape[0] == 1
    tile = lambda width: pl.BlockSpec((rows, width), lambda i: (i, 0))
    const = lambda shape: pl.BlockSpec(shape, lambda i: (0,) * len(shape))
    mod_map = (lambda i: (0, 0, 0)) if shared_mod else (lambda i: (i // tiles_per_seq, 0, 0))
    return pl.pallas_call(
        _post0_kernel,
        out_shape=[jax.ShapeDtypeStruct((n_all, D_MODEL), F32),
                   jax.ShapeDtypeStruct((n_all, XE_W), BF16),
                   jax.ShapeDtypeStruct((n_all // ROWS, N_EXPERTS, ROWS), F32)],
        grid=(n_all // rows,),
        in_specs=[tile(D_MODEL), tile(N_HEADS * V_HEAD), tile(CONV_CH), const((D_MODEL, D_MODEL)),
                  pl.BlockSpec((None, 6, D_MODEL), mod_map), const((1, D_MODEL)),
                  const((D_MODEL, LANES)), const((D_MODEL, LANES))],
        out_specs=[tile(D_MODEL), tile(XE_W),
                   pl.BlockSpec((rows // ROWS, N_EXPERTS, ROWS), lambda i: (i, 0, 0))],
        compiler_params=_params(1), name="post0",
    )(x, attn, conv, w["w_o"], mod, w["norm2"], w["wr_hi"], w["wr_lo"])


def _fnet_kernel(x_ref, mod_ref, n1_ref, cc_ref, sc_ref, dft_ref, flip_ref, wf_ref, n2_ref, wrh_ref, wrl_ref,
                 x1_ref, h2_ref, aff_ref, h_s, r_s, y_s, *, n_tok):
    sh1, sc1, g1 = mod_ref[0:1, :], mod_ref[1:2, :], mod_ref[2:3, :]
    n_chunks = n_tok // ROWS
    half = n_tok // 2
    blk = flip_ref.shape[0]
    cc, sc = cc_ref[...].astype(BF16), sc_ref[...].astype(BF16)
    for c in range(n_chunks):
        rows = pl.ds(c * ROWS, ROWS)
        h_s[rows, :] = (_rms(x_ref[rows, :], n1_ref[...]) * (1.0 + sc1) + sh1).astype(BF16)
    n_flip = half // blk
    for i in range(n_flip):
        src = h_s[pl.ds(half + (n_flip - 1 - i) * blk, blk), :]
        r_s[pl.ds(i * blk, blk), :] = _dot(flip_ref[...], src)
    mirror = pltpu.roll(r_s[...], 1, 0)
    row = lax.broadcasted_iota(jnp.int32, mirror.shape, 0)
    mirror = jnp.where(row == 0, 0.0, mirror)
    low = h_s[0:half, :].astype(F32)
    even, odd = (low + mirror).astype(BF16), (low - mirror).astype(BF16)
    mid = h_s[pl.ds(half, 16), :]
    mids = []
    for g in range(FNET_GROUPS):
        sl = slice(g * FNET_CH, (g + 1) * FNET_CH)
        y_s[0:half, sl] = _dot(even[:, sl], cc).astype(BF16)
        y_s[half:n_tok, sl] = _dot(odd[:, sl], sc).astype(BF16)
        mids.append(_dot(mid[:, sl], cc)[0:1, :])
    mid_row = jnp.concatenate(mids, axis=1)
    scale = 1.0 / float(np.sqrt(n_tok * FNET_CH))
    wide = min(n_tok, WIDE_ROWS)
    for c in range(n_chunks):
        rows = pl.ds(c * ROWS, ROWS)
        if (c * ROWS) % wide == 0:
            parity = lax.broadcasted_iota(jnp.int32, (wide, 1), 0) & 1
            f = _dot(dft_ref[pl.ds(c * ROWS, wide), :].astype(BF16), y_s[...])
            f = (f + jnp.where(parity == 0, 1.0, -1.0) * mid_row) * scale
            mix = _dot(f.astype(BF16), wf_ref[...])
        off = (c * ROWS) % wide
        x1 = x_ref[rows, :] + g1 * mix[off:off + ROWS, :]
        x1_ref[rows, :] = x1
        _moe_front(x1, mod_ref, n2_ref, wrh_ref, wrl_ref, h2_ref, aff_ref.at[c], rows)


def _fnet(x, mod, w, dft):
    n_b, n_tok, _ = x.shape
    shared_mod = mod.shape[0] == 1
    seq = lambda width: pl.BlockSpec((None, n_tok, width), lambda b: (b, 0, 0))
    cc, sc, dft_n, flip = dft
    tiles = n_tok // ROWS
    return pl.pallas_call(
        functools.partial(_fnet_kernel, n_tok=n_tok),
        out_shape=[jax.ShapeDtypeStruct((n_b, n_tok, D_MODEL), F32),
                   jax.ShapeDtypeStruct((n_b, n_tok, XE_W), BF16),
                   jax.ShapeDtypeStruct((n_b * tiles, N_EXPERTS, ROWS), F32)],
        grid=(n_b,),
        in_specs=[seq(D_MODEL),
                  pl.BlockSpec((None, 6, D_MODEL), (lambda b: (0, 0, 0)) if shared_mod else (lambda b: (b, 0, 0))),
                  _const_spec((1, D_MODEL)), _const_spec((FNET_CH, FNET_CH)), _const_spec((FNET_CH, FNET_CH)),
                  pl.BlockSpec((n_tok, n_tok), lambda b: (0, 0), pipeline_mode=pl.Buffered(1)),
                  _const_spec(flip.shape), _const_spec((D_MODEL, D_MODEL)), _const_spec((1, D_MODEL)),
                  _const_spec((D_MODEL, LANES)), _const_spec((D_MODEL, LANES))],
        out_specs=[seq(D_MODEL), seq(XE_W), pl.BlockSpec((tiles, N_EXPERTS, ROWS), lambda b: (b, 0, 0))],
        scratch_shapes=[pltpu.VMEM((n_tok, D_MODEL), BF16), pltpu.VMEM((n_tok // 2, D_MODEL), F32),
                        pltpu.VMEM((n_tok, D_MODEL), BF16)],
        compiler_params=_params(1), name="fnet",
    )(x, mod, w["norm1"], cc, sc, dft_n, flip, w["w_f"], w["norm2"], w["wr_hi"], w["wr_lo"])


def _route_kernel(aff_ref, tri_ref, posm_ref, s_ref, *, n_blk, cap):
    def count(pred):
        acc = jnp.zeros((N_EXPERTS, ROWS), F32)
        for b in range(n_blk):
            acc = acc + jnp.where(pred(aff_ref[b]), 1.0, 0.0)
        return jnp.sum(acc, axis=1, keepdims=True)

    def as_f32(bits):
        return pltpu.bitcast(bits, F32)

    def search(_, carry):
        lo, hi = carry
        mid = lo + ((hi - lo) >> 1)
        mid_f = as_f32(mid)
        ok = count(lambda a: a >= mid_f) >= cap
        return jnp.where(ok, mid, lo), jnp.where(ok, hi, mid)

    one_bits = 0x3F800000
    lo0 = jnp.zeros((N_EXPERTS, 1), jnp.int32)
    hi0 = jnp.full((N_EXPERTS, 1), one_bits + 1, jnp.int32)
    lo, hi = lax.fori_loop(0, 31, search, (lo0, hi0))
    lo_f, ub = as_f32(lo), as_f32(hi)
    thr = lo_f
    pending = jnp.ones((N_EXPERTS, 1), F32)
    for _ in range(3):
        cur = jnp.full((N_EXPERTS, ROWS), -1.0, F32)
        for b in range(n_blk):
            a = aff_ref[b]
            cur = jnp.maximum(cur, jnp.where((a >= lo_f) & (a < ub), a, -1.0))
        cur = jnp.max(cur, axis=1, keepdims=True)
        take = (count(lambda a: a >= cur) >= cap) & (pending > 0.0)
        thr = jnp.where(take, cur, thr)
        pending = jnp.where(take, 0.0, pending)
        ub = cur
    need = cap - count(lambda a: a > thr)

    lane = lax.broadcasted_iota(jnp.int32, (N_EXPERTS, LANES), 1)
    carry_tie = jnp.zeros((N_EXPERTS, 1), F32)
    carry_pos = jnp.zeros((N_EXPERTS, 1), F32)
    s_acc = jnp.zeros((N_EXPERTS, LANES), jnp.int32)
    for b in range(n_blk):
        v = aff_ref[b]
        eq = jnp.where(v == thr, 1.0, 0.0)
        tie_rank = _dot(eq.astype(BF16), tri_ref[...]) + carry_tie
        sel = jnp.where((v > thr) | ((v == thr) & (tie_rank < need)), 1.0, 0.0)
        pos = _dot(sel.astype(BF16), tri_ref[...]) + carry_pos
        posm_ref[b] = jnp.where(sel > 0.0, pos, -1.0)
        s_acc = jnp.where(lane == b, carry_pos.astype(jnp.int32), s_acc)
        carry_tie = carry_tie + jnp.sum(eq, axis=1, keepdims=True)
        n_sel = jnp.sum(sel, axis=1, keepdims=True)
        carry_pos = carry_pos + jnp.floor((n_sel + (ALIGN - 1)) * (1.0 / ALIGN)) * ALIGN
    s_ref[...] = jnp.where(lane == n_blk, carry_pos.astype(jnp.int32), s_acc)


def _route(aff, tri, cap):
    n_blk = aff.shape[0]
    assert n_blk < LANES
    full = lambda shape: pl.BlockSpec(shape, lambda i: (0,) * len(shape))
    posm, s_tab = pl.pallas_call(
        functools.partial(_route_kernel, n_blk=n_blk, cap=cap),
        out_shape=[jax.ShapeDtypeStruct((n_blk, N_EXPERTS, ROWS), F32),
                   jax.ShapeDtypeStruct((N_EXPERTS, LANES), jnp.int32)],
        grid=(1,),
        in_specs=[full((n_blk, N_EXPERTS, ROWS)), full((ROWS, ROWS))],
        out_specs=[full((n_blk, N_EXPERTS, ROWS)), full((N_EXPERTS, LANES))],
        compiler_params=_params(1), name="route",
    )(aff, tri)
    return posm, s_tab[:, :n_blk + 1]


def _ceil_div_pow2(x, d):
    return lax.shift_right_logical(x + (d - 1), d.bit_length() - 1)


def _block_units(s_ref, b):
    base = [s_ref[e, b] for e in range(N_EXPERTS)]
    units = [_ceil_div_pow2(s_ref[e, b + 1] - base[e], WIN) for e in range(N_EXPERTS)]
    first_unit, total = [], 0
    for e in range(N_EXPERTS):
        first_unit.append(total)
        total = total + units[e]
    return base, units, first_unit, total


def _for_units(units, fn):
    for e in range(N_EXPERTS):
        def body(u, _, e=e):
            fn(e, u)
            return 0
        lax.fori_loop(0, units[e], body, 0)


def _window_onehot(posm_ref, e, first, lo):
    slot = lax.broadcasted_iota(jnp.int32, (WIN, ROWS), 0) + first
    p = posm_ref[e:e + 1, :].astype(jnp.int32)
    return jnp.where((p == slot) & (p >= lo), 1.0, 0.0)


def _gate_lanes(e):
    lane = lax.broadcasted_iota(jnp.int32, (1, LANES), 1)
    return ((lane & (N_EXPERTS - 1)) == e) & (lane < 3 * N_EXPERTS)


def _dispatch_kernel(s_ref, h_ref, posm_ref, xe_ref, sel_s, x_s, sem, *, n_blk, cap_pad):
    b = pl.program_id(0)
    buf = b % 2
    base, units, first_unit, n_units = _block_units(s_ref, b)

    @pl.when(b == 0)
    def _():
        sel_s[...] = jnp.zeros_like(sel_s)

    def unit_rows(first_unit_e, u):
        return pl.ds(pl.multiple_of((first_unit_e + u) * WIN, WIN), WIN)

    def pick(e, u):
        first = base[e] + u * WIN
        sel_s[unit_rows(first_unit[e], u), :] = _window_onehot(posm_ref, e, first, first).astype(BF16)

    _for_units(units, pick)

    def gather(rows):
        x_s[buf, rows, :] = _dot(sel_s[rows, :], h_ref[...]).astype(BF16)

    def gather_chunk(c, _):
        gather(pl.ds(pl.multiple_of(c * ROWS, ROWS), ROWS))
        return 0

    gather(pl.ds(0, COMMON_ROWS))
    lax.fori_loop(COMMON_ROWS // ROWS, _ceil_div_pow2(n_units * WIN, ROWS), gather_chunk, 0)

    def unit_copy(slot, blk_base_e, blk_first_e, e, u):
        dst = pl.ds(pl.multiple_of(blk_base_e + u * WIN, ALIGN), WIN)
        return pltpu.make_async_copy(x_s.at[slot, unit_rows(blk_first_e, u), :], xe_ref.at[e, dst, :],
                                     sem.at[slot, e])

    def for_expert_units(n, fn):
        lax.fori_loop(0, n, lambda u, _: (fn(u), 0)[1], 0)

    prev_base, prev_units, prev_first, _ = _block_units(s_ref, jnp.maximum(b - 1, 0))
    for e in range(N_EXPERTS):
        for_expert_units(jnp.where(b > 0, prev_units[e], 0),
                         lambda u, e=e: unit_copy(1 - buf, prev_base[e], prev_first[e], e, u).wait())
        for_expert_units(units[e], lambda u, e=e: unit_copy(buf, base[e], first_unit[e], e, u).start())

    @pl.when(b == n_blk - 1)
    def _():
        for e in range(N_EXPERTS):
            for_expert_units(units[e], lambda u, e=e: unit_copy(buf, base[e], first_unit[e], e, u).wait())
        x_s[buf, 0:WIN, :] = jnp.zeros((WIN, XE_W), BF16)

        def tail(fn):
            for e in range(N_EXPERTS):
                total = s_ref[e, n_blk]
                n_win = (cap_pad - total) // WIN

                def wide(c, _):
                    row = pl.multiple_of(total + c * WIN, ALIGN)
                    fn(pltpu.make_async_copy(x_s.at[buf, pl.ds(0, WIN), :],
                                             xe_ref.at[e, pl.ds(row, WIN), :], sem.at[buf, e]))
                    return 0

                def narrow(c, _):
                    row = pl.multiple_of(total + n_win * WIN + c * ALIGN, ALIGN)
                    fn(pltpu.make_async_copy(x_s.at[buf, pl.ds(0, ALIGN), :],
                                             xe_ref.at[e, pl.ds(row, ALIGN), :], sem.at[buf, e]))
                    return 0

                lax.fori_loop(0, n_win, wide, 0)
                lax.fori_loop(0, (cap_pad - total - n_win * WIN) // ALIGN, narrow, 0)

        tail(lambda c: c.start())
        tail(lambda c: c.wait())


def _ffn_kernel(*refs, n_blks):
    n_sets = len(n_blks)
    s_refs, x_refs = refs[:n_sets], refs[n_sets:2 * n_sets]
    wg_ref, wu_ref, wd_ref = refs[2 * n_sets:2 * n_sets + 3]
    y_refs = refs[2 * n_sets + 3:3 * n_sets + 3]
    wg_s, wu_s, wd_s = refs[3 * n_sets + 3:]
    e = pl.program_id(0)
    wg_s[...] = wg_ref[...].astype(BF16)
    wu_s[...] = wu_ref[...].astype(BF16)
    wd_s[...] = wd_ref[...].astype(BF16)
    mine = _gate_lanes(e)

    half = ROWS // 2
    for s_ref, x_ref, y_ref, n_blk in zip(s_refs, x_refs, y_refs, n_blks):
        n_half = _ceil_div_pow2(s_ref[e, n_blk], half)

        def tile(rows, x_ref=x_ref, y_ref=y_ref):
            x = x_ref[rows, 0:D_MODEL]
            pieces = x_ref[rows, D_MODEL:XE_W].astype(F32)
            gate = jnp.sum(jnp.where(mine, pieces, 0.0), axis=1, keepdims=True)
            a = _dot(x, wg_s[...])
            u = _dot(x, wu_s[...])
            hid = a * (1.0 / (1.0 + jnp.exp(-a))) * u * gate
            y_ref[rows, :] = _dot(hid.astype(BF16), wd_s[...]).astype(BF16)

        def full(j, _, tile=tile):
            tile(pl.ds(pl.multiple_of(j * WIDE_ROWS, WIDE_ROWS), WIDE_ROWS))
            return 0

        def blank(j, _, y_ref=y_ref):
            y_ref[pl.ds(pl.multiple_of(j * half, half), half), :] = jnp.zeros((half, D_MODEL), BF16)
            return 0

        lax.fori_loop(0, lax.shift_right_logical(n_half, 2), full, 0)

        @pl.when((n_half & 2) != 0)
        def _(tile=tile, n_half=n_half):
            tile(pl.ds(pl.multiple_of((n_half & -4) * half, WIDE_ROWS), ROWS))

        @pl.when((n_half & 1) != 0)
        def _(tile=tile, n_half=n_half):
            tile(pl.ds(pl.multiple_of((n_half - 1) * half, half), half))

        lax.fori_loop(n_half, y_ref.shape[0] // half, blank, 0)


def _combine_kernel(s_ref, y_ref, posm_ref, x1_ref, mod_ref, o_ref, yw_s, sel_s, acc_s, sem,
                    *, n_blk, cap_pad):
    b = pl.program_id(0)
    buf = b % 2
    base, units, first_unit, n_units = _block_units(s_ref, b)

    def unit_rows(first_unit_e, u):
        return pl.ds(pl.multiple_of((first_unit_e + u) * WIN, WIN), WIN)

    def bounds(base_e, u):
        lo = base_e + u * WIN
        return lo, jnp.minimum(lo, cap_pad - WIN)

    def copies(slot, blk, fn):
        blk_base, blk_units, blk_first, _ = _block_units(s_ref, blk)

        def one(e, u):
            src = pl.ds(pl.multiple_of(bounds(blk_base[e], u)[1], ALIGN), WIN)
            fn(pltpu.make_async_copy(y_ref.at[e, src, :], yw_s.at[slot, unit_rows(blk_first[e], u), :],
                                     sem.at[slot]))

        _for_units(blk_units, one)

    @pl.when(b == 0)
    def _():
        yw_s[...] = jnp.zeros_like(yw_s)
        copies(0, 0, lambda c: c.start())

    @pl.when(b + 1 < n_blk)
    def _():
        copies(1 - buf, b + 1, lambda c: c.start())

    def pick(e, u):
        lo, first = bounds(base[e], u)
        sel_s[unit_rows(first_unit[e], u), :] = _window_onehot(posm_ref, e, first, lo)

    _for_units(units, pick)
    n_chunks = jnp.maximum(_ceil_div_pow2(n_units * WIN, ROWS), COMMON_ROWS // ROWS)

    def blank(u, _):
        sel_s[unit_rows(u, 0), :] = jnp.zeros((WIN, ROWS), F32)
        return 0

    lax.fori_loop(n_units, n_chunks * (ROWS // WIN), blank, 0)
    copies(buf, b, lambda c: c.wait())

    def weighted(rows):
        return _dot(sel_s[rows, :].T.astype(BF16), yw_s[buf, rows, :])

    def add_chunk(c, _):
        acc_s[...] += weighted(pl.ds(pl.multiple_of(c * ROWS, ROWS), ROWS))
        return 0

    acc_s[...] = weighted(pl.ds(0, COMMON_ROWS))
    lax.fori_loop(COMMON_ROWS // ROWS, n_chunks, add_chunk, 0)
    o_ref[...] = x1_ref[...] + mod_ref[5:6, :] * acc_s[...]


def _padded_capacity(cap, n_blk):
    return -(-(cap + (ALIGN - 1) * n_blk + WIN) // ROWS) * ROWS


def _block_spec(*shape):
    return pl.BlockSpec((None,) + shape, lambda b, s: (b,) + (0,) * len(shape))


def _dispatch(h2, posm, s_tab, cap):
    n_blk = h2.shape[0] // ROWS
    cap_pad = _padded_capacity(cap, n_blk)
    return pl.pallas_call(
        functools.partial(_dispatch_kernel, n_blk=n_blk, cap_pad=cap_pad),
        out_shape=jax.ShapeDtypeStruct((N_EXPERTS, cap_pad, XE_W), BF16),
        grid_spec=pltpu.PrefetchScalarGridSpec(
            num_scalar_prefetch=1, grid=(n_blk,),
            in_specs=[pl.BlockSpec((ROWS, XE_W), lambda b, s: (b, 0)), _block_spec(N_EXPERTS, ROWS)],
            out_specs=pl.BlockSpec(memory_space=pl.ANY),
            scratch_shapes=[pltpu.VMEM((MAX_UNITS * WIN, ROWS), BF16), pltpu.VMEM((2, MAX_UNITS * WIN, XE_W), BF16),
                            pltpu.SemaphoreType.DMA((2, N_EXPERTS))]),
        compiler_params=_params(1), name="dispatch",
    )(s_tab, h2, posm)


def _ffn(xes, s_tabs, wg, wu, wd):
    n_sets = len(xes)
    per_expert = lambda rows, cols: pl.BlockSpec((None, rows, cols), lambda e, *s: (e, 0, 0))
    return pl.pallas_call(
        functools.partial(_ffn_kernel, n_blks=tuple(s.shape[1] - 1 for s in s_tabs)),
        out_shape=[jax.ShapeDtypeStruct(xe.shape[:2] + (D_MODEL,), BF16) for xe in xes],
        grid_spec=pltpu.PrefetchScalarGridSpec(
            num_scalar_prefetch=n_sets, grid=(N_EXPERTS,),
            in_specs=[per_expert(xe.shape[1], XE_W) for xe in xes]
            + [per_expert(D_MODEL, EXPERT_FF), per_expert(D_MODEL, EXPERT_FF), per_expert(EXPERT_FF, D_MODEL)],
            out_specs=[per_expert(xe.shape[1], D_MODEL) for xe in xes],
            scratch_shapes=[pltpu.VMEM((D_MODEL, EXPERT_FF), BF16), pltpu.VMEM((D_MODEL, EXPERT_FF), BF16),
                            pltpu.VMEM((EXPERT_FF, D_MODEL), BF16)]),
        compiler_params=_params(1), name="ffn",
    )(*s_tabs, *xes, wg, wu, wd)


def _combine(y, posm, s_tab, x1, mod, n_tok):
    n_all = x1.shape[0]
    n_blk = n_all // ROWS
    tiles_per_seq = n_tok // ROWS
    mod_map = (lambda b, s: (0, 0, 0)) if mod.shape[0] == 1 else (lambda b, s: (b // tiles_per_seq, 0, 0))
    rows = pl.BlockSpec((ROWS, D_MODEL), lambda b, s: (b, 0))
    return pl.pallas_call(
        functools.partial(_combine_kernel, n_blk=n_blk, cap_pad=y.shape[1]),
        out_shape=jax.ShapeDtypeStruct((n_all, D_MODEL), F32),
        grid_spec=pltpu.PrefetchScalarGridSpec(
            num_scalar_prefetch=1, grid=(n_blk,),
            in_specs=[pl.BlockSpec(memory_space=pl.ANY), _block_spec(N_EXPERTS, ROWS), rows,
                      pl.BlockSpec((None, 6, D_MODEL), mod_map)],
            out_specs=rows,
            scratch_shapes=[pltpu.VMEM((2, MAX_UNITS * WIN, D_MODEL), BF16), pltpu.VMEM((MAX_UNITS * WIN, ROWS), F32),
                            pltpu.VMEM((ROWS, D_MODEL), F32), pltpu.SemaphoreType.DMA((2,))]),
        compiler_params=_params(1), name="combine",
    )(s_tab, y, posm, x1, mod)


def _rope_tables(n_tok):
    rows = n_tok // GRID_W
    row = np.repeat(np.arange(rows, dtype=np.float64), GRID_W)
    col = np.tile(np.arange(GRID_W, dtype=np.float64), rows)
    axis_dim = QK_ROPE // 2
    inv_freq = ROPE_THETA ** (-np.arange(0, axis_dim, 2, dtype=np.float64) / axis_dim)
    ang = np.concatenate([row[:, None] * inv_freq, col[:, None] * inv_freq], axis=-1)
    cos = np.ones((n_tok, LANES))
    sin = np.zeros((n_tok, LANES))
    cos[:, ROPE_LANE0:ROPE_LANE0 + QK_ROPE] = np.repeat(np.cos(ang), 2, axis=1)
    sgn = np.tile(np.array([-1.0, 1.0]), QK_ROPE // 2)
    sin[:, ROPE_LANE0:ROPE_LANE0 + QK_ROPE] = np.repeat(np.sin(ang), 2, axis=1) * sgn
    return jnp.asarray(cos, F32), jnp.asarray(sin, F32)


def _dft_tables(n_tok):
    def cs(n):
        k = np.arange(n)
        ang = 2.0 * np.pi * ((k[:, None] * k[None, :]) % n) / n
        return np.cos(ang), np.sin(ang)
    cc, sc = cs(FNET_CH)
    cn, sn = cs(n_tok)
    half = n_tok // 2
    folded = np.concatenate([cn[:, :half], -sn[:, :half]], axis=1)
    flip = np.fliplr(np.eye(min(ROWS, half)))
    return tuple(jnp.asarray(t, F32) for t in (cc, sc, folded)) + (jnp.asarray(flip, BF16),)


def _pad_heads(w, width):
    lead = w.shape[:-1]
    w = w.reshape(lead + (N_HEADS, width))
    w = jnp.pad(w, [(0, 0)] * len(lead) + [(0, 0), (0, HEAD_PAD - width)])
    return w.reshape(lead + (N_HEADS * HEAD_PAD,))


def _pair_swap_lanes(n_groups):
    perm = np.arange(n_groups * LANES).reshape(n_groups, LANES)
    rot = perm[:, ROPE_LANE0:ROPE_LANE0 + QK_ROPE].reshape(n_groups, QK_ROPE // 2, 2)[:, :, ::-1]
    perm[:, ROPE_LANE0:ROPE_LANE0 + QK_ROPE] = rot.reshape(n_groups, QK_ROPE)
    return perm.reshape(-1)


def _spare_from_rotary(t, src=None):
    src = t if src is None else src
    lead = t.shape[:-1]
    t, src = t.reshape(lead + (-1, LANES)), src.reshape(lead + (-1, LANES))
    t = jnp.concatenate([t[..., :QK_DIM], src[..., ROPE_LANE0:ROPE_LANE0 + QK_ROPE]], axis=-1)
    return t.reshape(lead + (-1,))


def _head_tables(gain, rope_tabs, n_tok, query):
    if rope_tabs is None:
        tab = jnp.broadcast_to(gain, (n_tok, LANES))
        return (tab if query else _spare_from_rotary(tab),)
    cos, sin = rope_tabs
    a, b = gain * cos, gain[:, _pair_swap_lanes(1)] * sin
    return (_spare_from_rotary(a, b),) if query else (_spare_from_rotary(a), _spare_from_rotary(b))


def _layer0_weights(norm1, norm2, w_in, q_a_norm, w_q_up, q_norm, kv_a_norm, w_kv_up, k_norm, conv_w, w_o):
    c0 = Q_LORA + KV_LORA
    rope_cols = jnp.pad(w_in[:, c0:c0 + QK_ROPE], ((0, 0), (ROPE_LANE0, LANES - ROPE_LANE0 - QK_ROPE)))
    w_in_pad = jnp.concatenate([w_in[:, :c0], _spare_from_rotary(rope_cols),
                                _spare_from_rotary(rope_cols[:, _pair_swap_lanes(1)]),
                                w_in[:, c0 + QK_ROPE:]], axis=1)
    kv = w_kv_up.reshape(KV_LORA, N_HEADS, QK_NOPE + V_HEAD)
    w_q = _pad_heads(w_q_up, QK_DIM)
    w_q = _spare_from_rotary(w_q, w_q[:, _pair_swap_lanes(N_HEADS)])
    head_gain = lambda g: jnp.pad(g, (0, HEAD_PAD - QK_DIM)).reshape(1, -1)
    return dict(
        norm1=norm1.reshape(1, -1), norm2=norm2.reshape(1, -1), w_in=w_in_pad.astype(BF16),
        q_a_norm=q_a_norm.reshape(1, -1), w_q=w_q.astype(BF16),
        q_gain=head_gain(q_norm) * (QK_DIM ** -0.5), kv_a_norm=kv_a_norm.reshape(1, -1),
        w_k=_pad_heads(kv[:, :, :QK_NOPE].reshape(KV_LORA, -1), QK_NOPE).astype(BF16),
        w_v=kv[:, :, QK_NOPE:].reshape(KV_LORA, -1).astype(BF16),
        w_v_wide=_pad_heads(kv[:, :, QK_NOPE:].reshape(KV_LORA, -1), V_HEAD).astype(BF16), k_gain=head_gain(k_norm),
        conv_w=conv_w, w_o=w_o.astype(BF16))


def _router_weights(w_router):
    hi, lo = _split_hi_lo(jnp.pad(w_router, ((0, 0), (0, LANES - N_EXPERTS))))
    return dict(wr_hi=hi, wr_lo=lo)


def _moe(sets, tri, w):
    routed = []
    for x1, h2, aff, mod, n_tok in sets:
        cap = CAPACITY_FACTOR * x1.shape[0] // N_EXPERTS
        posm, s_tab = _route(aff, tri, cap)
        routed.append((posm, s_tab, _dispatch(h2, posm, s_tab, cap)))
    ys = _ffn([r[2] for r in routed], [r[1] for r in routed], w["wg"], w["wu"], w["wd"])
    return [_combine(y, posm, s_tab, x1, mod, n_tok)
            for y, (posm, s_tab, _), (x1, _, _, mod, n_tok) in zip(ys, routed, sets)]


def _mixer0(x, mod, l0, ctx, rope_tabs):
    n_b, n_tok, _ = x.shape
    n_keys = n_tok + (0 if ctx is None else ctx[0].shape[1])
    w_v = l0["w_v_wide"] if n_keys > WIDE_ROWS else l0["w_v"]
    q, k, v, conv, ckv, kr = _front0(x, mod, l0, rope_tabs, w_v)
    if ctx is not None:
        kc, vc = _ctx_kv(ctx[0], ctx[1], l0, w_v)
        attn = _attention(q, k, v, kc, vc)
    else:
        attn = _attention(q, k, v)
    flat = lambda a: a.reshape(n_b * n_tok, a.shape[-1])
    return _post0(flat(x), flat(attn), flat(conv), mod, l0, n_tok), (ckv, kr[:, :, ROPE_LANE0:ROPE_LANE0 + QK_ROPE])


def kernel(x_prompt, x_sample, c, cache_c_kv_l0, cache_k_rope_l0, c_ctx, norm1_l0, norm2_l0, w_mod_l0, b_mod_l0, w_in_l0, q_a_norm_l0, w_q_up_l0, q_norm_l0, kv_a_norm_l0, w_kv_up_l0, k_norm_l0, conv_w_l0, w_o_l0, w_router_l0, w_gate_l0, w_up_l0, w_down_l0, norm1_l1, norm2_l1, w_mod_l1, b_mod_l1, w_f_l1, w_router_l1, w_gate_l1, w_up_l1, w_down_l1):
    n_dec = c.shape[0]
    cond = jnp.concatenate([c_ctx[None, :], c, jnp.zeros((16 - 1 - n_dec, D_MODEL), F32)], axis=0)
    m0 = _modulation(cond, w_mod_l0, b_mod_l0)
    m1 = _modulation(cond, w_mod_l1, b_mod_l1)
    mods_prompt = (m0[0:1], m1[0:1])
    mods_sample = (m0[1:1 + n_dec], m1[1:1 + n_dec])

    l0 = _layer0_weights(norm1_l0, norm2_l0, w_in_l0, q_a_norm_l0, w_q_up_l0, q_norm_l0, kv_a_norm_l0,
                         w_kv_up_l0, k_norm_l0, conv_w_l0, w_o_l0)
    l0.update(_router_weights(w_router_l0))
    l0.update(wg=w_gate_l0, wu=w_up_l0, wd=w_down_l0)
    l1 = dict(norm1=norm1_l1.reshape(1, -1), norm2=norm2_l1.reshape(1, -1), w_f=w_f_l1.astype(BF16))
    l1.update(_router_weights(w_router_l1))
    l1.update(wg=w_gate_l1, wu=w_up_l1, wd=w_down_l1)

    tri = jnp.asarray(np.triu(np.ones((ROWS, ROWS)), 1), BF16)
    k_rope_pad = _spare_from_rotary(
        jnp.pad(cache_k_rope_l0, ((0, 0), (0, 0), (ROPE_LANE0, LANES - ROPE_LANE0 - QK_ROPE))))
    xs = (x_prompt, x_sample)
    mods = (mods_prompt, mods_sample)
    n_toks = tuple(x.shape[1] for x in xs)

    (front_p, (new_c_kv, new_k_rope)) = _mixer0(x_prompt, mods_prompt[0], l0, None, None)
    (front_s, _) = _mixer0(x_sample, mods_sample[0], l0, (cache_c_kv_l0, k_rope_pad), _rope_tables(n_toks[1]))
    ys = _moe([tuple(front) + (mod[0], n_tok) for front, mod, n_tok in zip((front_p, front_s), mods, n_toks)],
              tri, l0)

    sets = []
    for y, x, mod, n_tok in zip(ys, xs, mods, n_toks):
        x1, h2, aff = _fnet(y.reshape(x.shape), mod[1], l1, _dft_tables(n_tok))
        sets.append((x1.reshape(-1, D_MODEL), h2.reshape(-1, XE_W), aff, mod[1], n_tok))
    y_prompt, y_sample = (y.reshape(x.shape) for y, x in zip(_moe(sets, tri, l1), xs))
    return (y_prompt, y_sample, new_c_kv, new_k_rope)
```

```python
import functools

import jax
import jax.numpy as jnp
import numpy as np
from jax import lax
from jax.experimental import pallas as pl
from jax.experimental.pallas import tpu as pltpu

D_MODEL = 1024
GRID_W = 64
N_HEADS = 8
QK_NOPE = 64
QK_ROPE = 32
QK_DIM = QK_NOPE + QK_ROPE
V_HEAD = 64
Q_LORA = 384
KV_LORA = 256
CONV_CH = 512
FNET_GROUPS = 4
FNET_CH = D_MODEL // FNET_GROUPS
N_EXPERTS = 16
EXPERT_FF = 512
CAPACITY_FACTOR = 2
ROPE_THETA = 10000.0
EPS = 1e-6

LANES = 128
HEAD_PAD = LANES
ROWS = 256
WIDE_ROWS = 512
ALIGN = 8
WIN = 64
MAX_UNITS = N_EXPERTS * (ROWS // WIN)
COMMON_ROWS = N_EXPERTS * WIN
STEP_BLOCKS = 2
XE_W = D_MODEL + LANES
IN0_PAD = Q_LORA + KV_LORA + 2 * LANES + 3 * CONV_CH
ROPE_LANE0 = QK_NOPE
VMEM_LIMIT = 56 * 1024 * 1024

F32 = jnp.float32
BF16 = jnp.bfloat16


def _dot(a, b):
    return jnp.dot(a, b, preferred_element_type=F32)


def _dot_nt(a, b):
    return lax.dot_general(a, b, (((1,), (1,)), ((), ())), preferred_element_type=F32)


def _split_hi_lo(x):
    hi = x.astype(BF16)
    lo = (x - hi.astype(F32)).astype(BF16)
    return hi, lo


def _params(n_axes):
    return pltpu.CompilerParams(dimension_semantics=("arbitrary",) * n_axes,
                                vmem_limit_bytes=VMEM_LIMIT)


def _rms(x, gain):
    return x * lax.rsqrt(jnp.mean(x * x, axis=-1, keepdims=True) + EPS) * gain


def _mod_kernel(cond_ref, w_ref, b_ref, o_ref):
    c = cond_ref[...]
    s = c * (1.0 / (1.0 + jnp.exp(-c)))
    s_hi, s_lo = _split_hi_lo(s)
    w = w_ref[...].astype(BF16)
    o_ref[...] = _dot(s_hi, w) + _dot(s_lo, w) + b_ref[...]


def _modulation(cond, w_mod, b_mod):
    n_rows = cond.shape[0]
    tn = 1536
    out = pl.pallas_call(
        _mod_kernel,
        out_shape=jax.ShapeDtypeStruct((n_rows, 6 * D_MODEL), F32),
        grid=(6 * D_MODEL // tn,),
        in_specs=[pl.BlockSpec((n_rows, D_MODEL), lambda i: (0, 0)),
                  pl.BlockSpec((D_MODEL, tn), lambda i: (0, i)),
                  pl.BlockSpec((1, tn), lambda i: (0, i))],
        out_specs=pl.BlockSpec((n_rows, tn), lambda i: (0, i)),
        compiler_params=_params(1),
        name="modulation",
    )(cond, w_mod, b_mod.reshape(1, -1))
    return out.reshape(n_rows, 6, D_MODEL)


def _head_norm_rope(xh, gain_cos, swapped_sin):
    lane = lax.broadcasted_iota(jnp.int32, (1, HEAD_PAD), 1)
    ss = jnp.sum(jnp.where(lane < QK_DIM, xh * xh, 0.0), axis=-1, keepdims=True) * (1.0 / QK_DIM)
    y = xh * gain_cos
    if swapped_sin is not None:
        y = y + swapped_sin
    return y * lax.rsqrt(ss + EPS)


def _expand_kv(ckv_bf, r, swapped_sin, wk_ref, wv_ref, gain_cos, k_ref, v_ref, rows):
    kf = _dot(ckv_bf, wk_ref[...])
    vf = _dot(ckv_bf, wv_ref[...])
    if v_ref.shape[-1] == N_HEADS * HEAD_PAD:
        lane = lax.broadcasted_iota(jnp.int32, (1, N_HEADS * HEAD_PAD), 1)
        vf = vf + jnp.where((lane & (HEAD_PAD - 1)) == V_HEAD, 1.0, 0.0)
    v_ref[rows, :] = vf.astype(BF16)
    for h in range(N_HEADS):
        sl = slice(h * HEAD_PAD, (h + 1) * HEAD_PAD)
        k_ref[rows, sl] = _head_norm_rope(kf[:, sl] + r, gain_cos, swapped_sin).astype(BF16)


def _front0_kernel(*refs, n_tok, rope):
    (x_ref, mod_ref, n1_ref, win_ref, qan_ref, wq_ref, kvan_ref, wk_ref, wv_ref, cw_ref,
     qa_ref, ka_ref) = refs[:12]
    pos = 12
    if rope:
        kb_ref = refs[pos]
        pos += 1
    q_ref, k_ref, v_ref, conv_ref, ckv_ref, kr_ref, cu_s, gb_s = refs[pos:]
    sh1, sc1 = mod_ref[0:1, :], mod_ref[1:2, :]
    c_rope = Q_LORA + KV_LORA
    wide = min(n_tok, WIDE_ROWS)
    for c in range(n_tok // ROWS):
        rows = pl.ds(c * ROWS, ROWS)
        if (c * ROWS) % wide == 0:
            wide_rows = pl.ds(c * ROWS, wide)
            h = _rms(x_ref[wide_rows, :], n1_ref[...]) * (1.0 + sc1) + sh1
            proj_wide = _dot(h.astype(BF16), win_ref[...])
        off = (c * ROWS) % wide
        proj = proj_wide[off:off + ROWS, :]
        cq = _rms(proj[:, :Q_LORA], qan_ref[...]).astype(BF16)
        qf = _dot(cq, wq_ref[...])
        for hd in range(N_HEADS):
            sl = slice(hd * HEAD_PAD, (hd + 1) * HEAD_PAD)
            q_ref[rows, sl] = _head_norm_rope(qf[:, sl], qa_ref[rows, :], None).astype(BF16)
        ckv = _rms(proj[:, Q_LORA:c_rope], kvan_ref[...])
        ckv_ref[rows, :] = ckv
        r = proj[:, c_rope:c_rope + LANES]
        kr_ref[rows, :] = r
        k_sin = proj[:, c_rope + LANES:c_rope + 2 * LANES] * kb_ref[rows, :] if rope else None
        _expand_kv(ckv.astype(BF16), r, k_sin, wk_ref, wv_ref, ka_ref[rows, :], k_ref, v_ref, rows)
        c0 = c_rope + 2 * LANES
        gb_s[rows, :] = proj[:, c0:c0 + CONV_CH]
        cu_s[rows, :] = proj[:, c0 + CONV_CH:c0 + 2 * CONV_CH] * proj[:, c0 + 2 * CONV_CH:c0 + 3 * CONV_CH]
    cu = cu_s[...]
    row = lax.broadcasted_iota(jnp.int32, cu.shape, 0)
    prev = jnp.where(row == 0, 0.0, pltpu.roll(cu, 1, 0))
    nxt = jnp.where(row == n_tok - 1, 0.0, pltpu.roll(cu, n_tok - 1, 0))
    conv = gb_s[...] * (cw_ref[0:1, :] * prev + cw_ref[1:2, :] * cu + cw_ref[2:3, :] * nxt)
    conv_ref[...] = conv.astype(BF16)


def _const_spec(shape):
    return pl.BlockSpec(shape, lambda b: (0,) * len(shape))


def _front0(x, mod, w, rope_tabs, w_v):
    n_b, n_tok, _ = x.shape
    v_width = w_v.shape[1]
    shared_mod = mod.shape[0] == 1
    seq = lambda width: pl.BlockSpec((None, n_tok, width), lambda b: (b, 0, 0))
    in_specs = [seq(D_MODEL),
                pl.BlockSpec((None, 6, D_MODEL), (lambda b: (0, 0, 0)) if shared_mod else (lambda b: (b, 0, 0))),
                _const_spec((1, D_MODEL)), _const_spec((D_MODEL, IN0_PAD)), _const_spec((1, Q_LORA)),
                _const_spec((Q_LORA, N_HEADS * HEAD_PAD)),
                _const_spec((1, KV_LORA)), _const_spec((KV_LORA, N_HEADS * HEAD_PAD)),
                _const_spec((KV_LORA, v_width)), _const_spec((3, CONV_CH)),
                _const_spec((n_tok, LANES)), _const_spec((n_tok, LANES))]
    q_tabs = _head_tables(w["q_gain"], rope_tabs, n_tok, True)
    k_tabs = _head_tables(w["k_gain"], rope_tabs, n_tok, False)
    args = [x, mod, w["norm1"], w["w_in"], w["q_a_norm"], w["w_q"], w["kv_a_norm"],
            w["w_k"], w_v, w["conv_w"], q_tabs[0], k_tabs[0]]
    if rope_tabs is not None:
        in_specs += [_const_spec((n_tok, LANES))]
        args += [k_tabs[1]]
    out_shape = [jax.ShapeDtypeStruct((n_b, n_tok, N_HEADS * HEAD_PAD), BF16),
                 jax.ShapeDtypeStruct((n_b, n_tok, N_HEADS * HEAD_PAD), BF16),
                 jax.ShapeDtypeStruct((n_b, n_tok, v_width), BF16),
                 jax.ShapeDtypeStruct((n_b, n_tok, CONV_CH), BF16),
                 jax.ShapeDtypeStruct((n_b, n_tok, KV_LORA), F32),
                 jax.ShapeDtypeStruct((n_b, n_tok, LANES), F32)]
    out_specs = [seq(N_HEADS * HEAD_PAD), seq(N_HEADS * HEAD_PAD), seq(v_width), seq(CONV_CH),
                 seq(KV_LORA), seq(LANES)]
    return pl.pallas_call(
        functools.partial(_front0_kernel, n_tok=n_tok, rope=rope_tabs is not None),
        out_shape=out_shape, grid=(n_b,), in_specs=in_specs, out_specs=out_specs,
        scratch_shapes=[pltpu.VMEM((n_tok, CONV_CH), F32), pltpu.VMEM((n_tok, CONV_CH), F32)],
        compiler_params=_params(1), name="front0",
    )(*args)


def _ctx_kv_kernel(ckv_ref, r_ref, wk_ref, wv_ref, ka_ref, k_ref, v_ref):
    rows = pl.ds(0, ckv_ref.shape[0])
    _expand_kv(ckv_ref[...].astype(BF16), r_ref[...], None, wk_ref, wv_ref, ka_ref[...], k_ref, v_ref, rows)


def _ctx_kv(cache_c_kv, cache_k_rope_pad, w, w_v):
    n_b, n_ctx, _ = cache_c_kv.shape
    v_width = w_v.shape[1]
    seq = lambda width: pl.BlockSpec((None, n_ctx, width), lambda b: (b, 0, 0))
    return pl.pallas_call(
        _ctx_kv_kernel,
        out_shape=[jax.ShapeDtypeStruct((n_b, n_ctx, N_HEADS * HEAD_PAD), BF16),
                   jax.ShapeDtypeStruct((n_b, n_ctx, v_width), BF16)],
        grid=(n_b,),
        in_specs=[seq(KV_LORA), seq(LANES), _const_spec((KV_LORA, N_HEADS * HEAD_PAD)),
                  _const_spec((KV_LORA, v_width)), _const_spec((n_ctx, LANES))],
        out_specs=[seq(N_HEADS * HEAD_PAD), seq(v_width)],
        compiler_params=_params(1), name="ctx_kv",
    )(cache_c_kv, cache_k_rope_pad, w["w_k"], w_v, _head_tables(w["k_gain"], None, n_ctx, False)[0])


def _attn_kernel(*refs, with_ctx):
    if with_ctx:
        q_ref, k_ref, v_ref, kc_ref, vc_ref, o_ref = refs
    else:
        q_ref, k_ref, v_ref, o_ref = refs
    ones_lane = v_ref.shape[-1] == N_HEADS * HEAD_PAD
    lane = lax.broadcasted_iota(jnp.int32, (q_ref.shape[0], LANES), 1)
    for pair in range(N_HEADS // 2):
        outs = []
        for hd in (2 * pair, 2 * pair + 1):
            sl = slice(hd * HEAD_PAD, (hd + 1) * HEAD_PAD)
            vsl = sl if ones_lane else slice(pair * LANES, (pair + 1) * LANES)
            qh = q_ref[:, sl]
            s = _dot_nt(qh, k_ref[:, sl])
            m = jnp.max(s, axis=-1, keepdims=True)
            if with_ctx:
                sc = _dot_nt(qh, kc_ref[:, sl])
                m = jnp.maximum(m, jnp.max(sc, axis=-1, keepdims=True))
            p = jnp.exp(s - m)
            o = _dot(p.astype(BF16), v_ref[:, vsl])
            if with_ctx:
                pc = jnp.exp(sc - m)
                o = o + _dot(pc.astype(BF16), vc_ref[:, vsl])
            if ones_lane:
                den = o[:, V_HEAD:V_HEAD + 1]
            else:
                den = jnp.sum(p, axis=-1, keepdims=True)
                if with_ctx:
                    den = den + jnp.sum(pc, axis=-1, keepdims=True)
            outs.append(o / den)
        odd = pltpu.roll(outs[1], V_HEAD, 1) if ones_lane else outs[1]
        o_ref[:, pair * LANES:(pair + 1) * LANES] = jnp.where(lane < V_HEAD, outs[0], odd).astype(BF16)


def _attention(q, k, v, kc=None, vc=None):
    n_b, n_tok, _ = q.shape
    with_ctx = kc is not None
    tq = min(n_tok, WIDE_ROWS)
    qspec = lambda width: pl.BlockSpec((None, tq, width), lambda b, i: (b, i, 0))
    kvspec = lambda a: pl.BlockSpec((None,) + a.shape[1:], lambda b, i: (b, 0, 0))
    in_specs = [qspec(N_HEADS * HEAD_PAD), kvspec(k), kvspec(v)]
    args = [q, k, v]
    if with_ctx:
        in_specs += [kvspec(kc), kvspec(vc)]
        args += [kc, vc]
    return pl.pallas_call(
        functools.partial(_attn_kernel, with_ctx=with_ctx),
        out_shape=jax.ShapeDtypeStruct((n_b, n_tok, N_HEADS * V_HEAD), BF16),
        grid=(n_b, n_tok // tq), in_specs=in_specs, out_specs=qspec(N_HEADS * V_HEAD),
        compiler_params=_params(2), name="attention",
    )(*args)


def _moe_front(x1, mod_ref, n2_ref, wrh_ref, wrl_ref, h2_ref, aff_ref, rows):
    sh2, sc2 = mod_ref[3:4, :], mod_ref[4:5, :]
    h2 = _rms(x1, n2_ref[...]) * (1.0 + sc2) + sh2
    h_hi, h_lo = _split_hi_lo(h2)
    logit = _dot(h_hi, wrh_ref[...]) + _dot(h_lo, wrh_ref[...]) + _dot(h_hi, wrl_ref[...])
    real = lax.broadcasted_iota(jnp.int32, (1, LANES), 1) < N_EXPERTS
    top = jnp.max(jnp.where(real, logit, -jnp.inf), axis=1, keepdims=True)
    e = jnp.where(real, jnp.exp(logit - top), 0.0)
    a = e / jnp.sum(e, axis=1, keepdims=True)
    a_hi = a.astype(BF16).astype(F32)
    a_mid = (a - a_hi).astype(BF16).astype(F32)
    a_lo = a - a_hi - a_mid
    h2_ref[rows, 0:D_MODEL] = h_hi
    h2_ref[rows, D_MODEL:XE_W] = (a_hi + pltpu.roll(a_mid, N_EXPERTS, 1)
                                  + pltpu.roll(a_lo, 2 * N_EXPERTS, 1)).astype(BF16)
    aff_ref[...] = a.T[0:N_EXPERTS, :]


def _post0_kernel(x_ref, attn_ref, conv_ref, wo_ref, mod_ref, n2_ref, wrh_ref, wrl_ref,
                  x1_ref, h2_ref, aff_ref):
    n_attn = N_HEADS * V_HEAD
    mix = _dot(attn_ref[...], wo_ref[0:n_attn, :]) + _dot(conv_ref[...], wo_ref[n_attn:, :])
    for t in range(x_ref.shape[0] // ROWS):
        rows = pl.ds(t * ROWS, ROWS)
        x1 = x_ref[rows, :] + mod_ref[2:3, :] * mix[t * ROWS:(t + 1) * ROWS, :]
        x1_ref[rows, :] = x1
        _moe_front(x1, mod_ref, n2_ref, wrh_ref, wrl_ref, h2_ref, aff_ref.at[t], rows)


def _post0(x, attn, conv, mod, w, n_tok):
    n_all = x.shape[0]
    rows = min(n_tok, WIDE_ROWS)
    tiles_per_seq = n_tok // rows
    shared_mod = mod.shape[0] == 1
    tile = lambda width: pl.BlockSpec((rows, width), lambda i: (i, 0))
    const = lambda shape: pl.BlockSpec(shape, lambda i: (0,) * len(shape))
    mod_map = (lambda i: (0, 0, 0)) if shared_mod else (lambda i: (i // tiles_per_seq, 0, 0))
    return pl.pallas_call(
        _post0_kernel,
        out_shape=[jax.ShapeDtypeStruct((n_all, D_MODEL), F32),
                   jax.ShapeDtypeStruct((n_all, XE_W), BF16),
                   jax.ShapeDtypeStruct((n_all // ROWS, N_EXPERTS, ROWS), F32)],
        grid=(n_all // rows,),
        in_specs=[tile(D_MODEL), tile(N_HEADS * V_HEAD), tile(CONV_CH), const((D_MODEL, D_MODEL)),
                  pl.BlockSpec((None, 6, D_MODEL), mod_map), const((1, D_MODEL)),
                  const((D_MODEL, LANES)), const((D_MODEL, LANES))],
        out_specs=[tile(D_MODEL), tile(XE_W),
                   pl.BlockSpec((rows // ROWS, N_EXPERTS, ROWS), lambda i: (i, 0, 0))],
        compiler_params=_params(1), name="post0",
    )(x, attn, conv, w["w_o"], mod, w["norm2"], w["wr_hi"], w["wr_lo"])


def _fnet_kernel(x_ref, mod_ref, n1_ref, cc_ref, sc_ref, dft_ref, flip_ref, wf_ref, n2_ref, wrh_ref, wrl_ref,
                 x1_ref, h2_ref, aff_ref, h_s, r_s, y_s, *, n_tok):
    sh1, sc1, g1 = mod_ref[0:1, :], mod_ref[1:2, :], mod_ref[2:3, :]
    n_chunks = n_tok // ROWS
    half = n_tok // 2
    blk = flip_ref.shape[0]
    cc, sc = cc_ref[...].astype(BF16), sc_ref[...].astype(BF16)
    for c in range(n_chunks):
        rows = pl.ds(c * ROWS, ROWS)
        h_s[rows, :] = (_rms(x_ref[rows, :], n1_ref[...]) * (1.0 + sc1) + sh1).astype(BF16)
    n_flip = half // blk
    for i in range(n_flip):
        src = h_s[pl.ds(half + (n_flip - 1 - i) * blk, blk), :]
        r_s[pl.ds(i * blk, blk), :] = _dot(flip_ref[...], src)
    mirror = pltpu.roll(r_s[...], 1, 0)
    row = lax.broadcasted_iota(jnp.int32, mirror.shape, 0)
    mirror = jnp.where(row == 0, 0.0, mirror)
    low = h_s[0:half, :].astype(F32)
    even, odd = (low + mirror).astype(BF16), (low - mirror).astype(BF16)
    mid = h_s[pl.ds(half, 16), :]
    mids = []
    for g in range(FNET_GROUPS):
        sl = slice(g * FNET_CH, (g + 1) * FNET_CH)
        y_s[0:half, sl] = _dot(even[:, sl], cc).astype(BF16)
        y_s[half:n_tok, sl] = _dot(odd[:, sl], sc).astype(BF16)
        mids.append(_dot(mid[:, sl], cc)[0:1, :])
    mid_row = jnp.concatenate(mids, axis=1)
    scale = 1.0 / float(np.sqrt(n_tok * FNET_CH))
    wide = min(n_tok, WIDE_ROWS)
    for c in range(n_chunks):
        rows = pl.ds(c * ROWS, ROWS)
        if (c * ROWS) % wide == 0:
            parity = lax.broadcasted_iota(jnp.int32, (wide, 1), 0) & 1
            f = _dot(dft_ref[pl.ds(c * ROWS, wide), :].astype(BF16), y_s[...])
            f = (f + jnp.where(parity == 0, 1.0, -1.0) * mid_row) * scale
            mix = _dot(f.astype(BF16), wf_ref[...])
        off = (c * ROWS) % wide
        x1 = x_ref[rows, :] + g1 * mix[off:off + ROWS, :]
        x1_ref[rows, :] = x1
        _moe_front(x1, mod_ref, n2_ref, wrh_ref, wrl_ref, h2_ref, aff_ref.at[c], rows)


def _fnet(x, mod, w, dft):
    n_b, n_tok, _ = x.shape
    shared_mod = mod.shape[0] == 1
    seq = lambda width: pl.BlockSpec((None, n_tok, width), lambda b: (b, 0, 0))
    cc, sc, dft_n, flip = dft
    tiles = n_tok // ROWS
    return pl.pallas_call(
        functools.partial(_fnet_kernel, n_tok=n_tok),
        out_shape=[jax.ShapeDtypeStruct((n_b, n_tok, D_MODEL), F32),
                   jax.ShapeDtypeStruct((n_b, n_tok, XE_W), BF16),
                   jax.ShapeDtypeStruct((n_b * tiles, N_EXPERTS, ROWS), F32)],
        grid=(n_b,),
        in_specs=[seq(D_MODEL),
                  pl.BlockSpec((None, 6, D_MODEL), (lambda b: (0, 0, 0)) if shared_mod else (lambda b: (b, 0, 0))),
                  _const_spec((1, D_MODEL)), _const_spec((FNET_CH, FNET_CH)), _const_spec((FNET_CH, FNET_CH)),
                  pl.BlockSpec((n_tok, n_tok), lambda b: (0, 0), pipeline_mode=pl.Buffered(1)),
                  _const_spec(flip.shape), _const_spec((D_MODEL, D_MODEL)), _const_spec((1, D_MODEL)),
                  _const_spec((D_MODEL, LANES)), _const_spec((D_MODEL, LANES))],
        out_specs=[seq(D_MODEL), seq(XE_W), pl.BlockSpec((tiles, N_EXPERTS, ROWS), lambda b: (b, 0, 0))],
        scratch_shapes=[pltpu.VMEM((n_tok, D_MODEL), BF16), pltpu.VMEM((n_tok // 2, D_MODEL), F32),
                        pltpu.VMEM((n_tok, D_MODEL), BF16)],
        compiler_params=_params(1), name="fnet",
    )(x, mod, w["norm1"], cc, sc, dft_n, flip, w["w_f"], w["norm2"], w["wr_hi"], w["wr_lo"])


def _route_kernel(aff_ref, tri_ref, posm_ref, s_ref, *, n_blk, cap):
    def count(pred):
        acc = jnp.zeros((N_EXPERTS, ROWS), F32)
        for b in range(n_blk):
            acc = acc + jnp.where(pred(aff_ref[b]), 1.0, 0.0)
        return jnp.sum(acc, axis=1, keepdims=True)

    def as_f32(bits):
        return pltpu.bitcast(bits, F32)

    def search(_, carry):
        lo, hi = carry
        mid = lo + ((hi - lo) >> 1)
        mid_f = as_f32(mid)
        ok = count(lambda a: a >= mid_f) >= cap
        return jnp.where(ok, mid, lo), jnp.where(ok, hi, mid)

    one_bits = 0x3F800000
    lo0 = jnp.zeros((N_EXPERTS, 1), jnp.int32)
    hi0 = jnp.full((N_EXPERTS, 1), one_bits + 1, jnp.int32)
    lo, hi = lax.fori_loop(0, 31, search, (lo0, hi0))
    lo_f, ub = as_f32(lo), as_f32(hi)
    thr = lo_f
    pending = jnp.ones((N_EXPERTS, 1), F32)
    for _ in range(3):
        cur = jnp.full((N_EXPERTS, ROWS), -1.0, F32)
        for b in range(n_blk):
            a = aff_ref[b]
            cur = jnp.maximum(cur, jnp.where((a >= lo_f) & (a < ub), a, -1.0))
        cur = jnp.max(cur, axis=1, keepdims=True)
        take = (count(lambda a: a >= cur) >= cap) & (pending > 0.0)
        thr = jnp.where(take, cur, thr)
        pending = jnp.where(take, 0.0, pending)
        ub = cur
    need = cap - count(lambda a: a > thr)

    lane = lax.broadcasted_iota(jnp.int32, (N_EXPERTS, LANES), 1)
    carry_tie = jnp.zeros((N_EXPERTS, 1), F32)
    carry_pos = jnp.zeros((N_EXPERTS, 1), F32)
    s_acc = jnp.zeros((N_EXPERTS, LANES), jnp.int32)
    for b in range(n_blk):
        v = aff_ref[b]
        eq = jnp.where(v == thr, 1.0, 0.0)
        tie_rank = _dot(eq.astype(BF16), tri_ref[...]) + carry_tie
        sel = jnp.where((v > thr) | ((v == thr) & (tie_rank < need)), 1.0, 0.0)
        pos = _dot(sel.astype(BF16), tri_ref[...]) + carry_pos
        posm_ref[b] = jnp.where(sel > 0.0, pos, -1.0)
        s_acc = jnp.where(lane == b, carry_pos.astype(jnp.int32), s_acc)
        carry_tie = carry_tie + jnp.sum(eq, axis=1, keepdims=True)
        n_sel = jnp.sum(sel, axis=1, keepdims=True)
        carry_pos = carry_pos + jnp.floor((n_sel + (ALIGN - 1)) * (1.0 / ALIGN)) * ALIGN
    s_ref[...] = jnp.where(lane == n_blk, carry_pos.astype(jnp.int32), s_acc)


def _route(aff, tri, cap):
    n_blk = aff.shape[0]
    assert n_blk < LANES
    full = lambda shape: pl.BlockSpec(shape, lambda i: (0,) * len(shape))
    posm, s_tab = pl.pallas_call(
        functools.partial(_route_kernel, n_blk=n_blk, cap=cap),
        out_shape=[jax.ShapeDtypeStruct((n_blk, N_EXPERTS, ROWS), F32),
                   jax.ShapeDtypeStruct((N_EXPERTS, LANES), jnp.int32)],
        grid=(1,),
        in_specs=[full((n_blk, N_EXPERTS, ROWS)), full((ROWS, ROWS))],
        out_specs=[full((n_blk, N_EXPERTS, ROWS)), full((N_EXPERTS, LANES))],
        compiler_params=_params(1), name="route",
    )(aff, tri)
    return posm, s_tab[:, :n_blk + 1]


def _ceil_div_pow2(x, d):
    return lax.shift_right_logical(x + (d - 1), d.bit_length() - 1)


def _block_units(s_ref, b):
    base = [s_ref[e, b] for e in range(N_EXPERTS)]
    units = [_ceil_div_pow2(s_ref[e, b + 1] - base[e], WIN) for e in range(N_EXPERTS)]
    first_unit, total = [], 0
    for e in range(N_EXPERTS):
        first_unit.append(total)
        total = total + units[e]
    return base, units, first_unit, total


def _for_units(units, fn):
    for e in range(N_EXPERTS):
        def body(u, _, e=e):
            fn(e, u)
            return 0
        lax.fori_loop(0, units[e], body, 0)


def _window_onehot(posm_ref, e, first, lo):
    slot = lax.broadcasted_iota(jnp.int32, (WIN, ROWS), 0) + first
    p = posm_ref[e:e + 1, :].astype(jnp.int32)
    return jnp.where((p == slot) & (p >= lo), 1.0, 0.0)


def _gate_lanes(e):
    lane = lax.broadcasted_iota(jnp.int32, (1, LANES), 1)
    return ((lane & (N_EXPERTS - 1)) == e) & (lane < 3 * N_EXPERTS)


def _dispatch_kernel(s_ref, h_ref, posm_ref, xe_ref, sel_s, x_s, sem, *, n_blk, cap_pad):
    for t in range(STEP_BLOCKS):
        _dispatch_block(pl.program_id(0) * STEP_BLOCKS + t, t, s_ref, h_ref.at[pl.ds(t * ROWS, ROWS), :],
                        posm_ref.at[t], xe_ref, sel_s, x_s, sem, n_blk, cap_pad)


def _dispatch_block(b, buf, s_ref, h_ref, posm_ref, xe_ref, sel_s, x_s, sem, n_blk, cap_pad):
    base, units, first_unit, n_units = _block_units(s_ref, b)

    @pl.when(b == 0)
    def _():
        sel_s[...] = jnp.zeros_like(sel_s)

    def unit_rows(first_unit_e, u):
        return pl.ds(pl.multiple_of((first_unit_e + u) * WIN, WIN), WIN)

    def pick(e, u):
        first = base[e] + u * WIN
        sel_s[unit_rows(first_unit[e], u), :] = _window_onehot(posm_ref, e, first, first).astype(BF16)

    _for_units(units, pick)

    def gather(rows):
        x_s[buf, rows, :] = _dot(sel_s[rows, :], h_ref[...]).astype(BF16)

    def gather_chunk(c, _):
        gather(pl.ds(pl.multiple_of(c * ROWS, ROWS), ROWS))
        return 0

    gather(pl.ds(0, COMMON_ROWS))
    lax.fori_loop(COMMON_ROWS // ROWS, _ceil_div_pow2(n_units * WIN, ROWS), gather_chunk, 0)

    def unit_copy(slot, blk_base_e, blk_first_e, e, u):
        dst = pl.ds(pl.multiple_of(blk_base_e + u * WIN, ALIGN), WIN)
        return pltpu.make_async_copy(x_s.at[slot, unit_rows(blk_first_e, u), :], xe_ref.at[e, dst, :],
                                     sem.at[slot, e])

    def for_expert_units(n, fn):
        lax.fori_loop(0, n, lambda u, _: (fn(u), 0)[1], 0)

    prev_base, prev_units, prev_first, _ = _block_units(s_ref, jnp.maximum(b - 1, 0))
    for e in range(N_EXPERTS):
        for_expert_units(jnp.where(b > 0, prev_units[e], 0),
                         lambda u, e=e: unit_copy(1 - buf, prev_base[e], prev_first[e], e, u).wait())
        for_expert_units(units[e], lambda u, e=e: unit_copy(buf, base[e], first_unit[e], e, u).start())

    @pl.when(b == n_blk - 1)
    def _():
        for e in range(N_EXPERTS):
            for_expert_units(units[e], lambda u, e=e: unit_copy(buf, base[e], first_unit[e], e, u).wait())
        x_s[buf, 0:WIN, :] = jnp.zeros((WIN, XE_W), BF16)

        def tail(fn):
            for e in range(N_EXPERTS):
                total = s_ref[e, n_blk]
                n_win = (cap_pad - total) // WIN

                def wide(c, _):
                    row = pl.multiple_of(total + c * WIN, ALIGN)
                    fn(pltpu.make_async_copy(x_s.at[buf, pl.ds(0, WIN), :],
                                             xe_ref.at[e, pl.ds(row, WIN), :], sem.at[buf, e]))
                    return 0

                def narrow(c, _):
                    row = pl.multiple_of(total + n_win * WIN + c * ALIGN, ALIGN)
                    fn(pltpu.make_async_copy(x_s.at[buf, pl.ds(0, ALIGN), :],
                                             xe_ref.at[e, pl.ds(row, ALIGN), :], sem.at[buf, e]))
                    return 0

                lax.fori_loop(0, n_win, wide, 0)
                lax.fori_loop(0, (cap_pad - total - n_win * WIN) // ALIGN, narrow, 0)

        tail(lambda c: c.start())
        tail(lambda c: c.wait())


def _ffn_kernel(*refs, n_blks):
    n_sets = len(n_blks)
    s_refs, x_refs = refs[:n_sets], refs[n_sets:2 * n_sets]
    wg_ref, wu_ref, wd_ref = refs[2 * n_sets:2 * n_sets + 3]
    y_refs = refs[2 * n_sets + 3:3 * n_sets + 3]
    wg_s, wu_s, wd_s = refs[3 * n_sets + 3:]
    e = pl.program_id(0)
    wg_s[...] = wg_ref[...].astype(BF16)
    wu_s[...] = wu_ref[...].astype(BF16)
    wd_s[...] = wd_ref[...].astype(BF16)
    mine = _gate_lanes(e)

    half = ROWS // 2
    for s_ref, x_ref, y_ref, n_blk in zip(s_refs, x_refs, y_refs, n_blks):
        n_half = _ceil_div_pow2(s_ref[e, n_blk], half)

        def tile(rows, x_ref=x_ref, y_ref=y_ref):
            x = x_ref[rows, 0:D_MODEL]
            pieces = x_ref[rows, D_MODEL:XE_W].astype(F32)
            gate = jnp.sum(jnp.where(mine, pieces, 0.0), axis=1, keepdims=True)
            a = _dot(x, wg_s[...])
            u = _dot(x, wu_s[...])
            hid = a * (1.0 / (1.0 + jnp.exp(-a))) * u * gate
            y_ref[rows, :] = _dot(hid.astype(BF16), wd_s[...]).astype(BF16)

        def full(j, _, tile=tile):
            tile(pl.ds(pl.multiple_of(j * WIDE_ROWS, WIDE_ROWS), WIDE_ROWS))
            return 0

        def blank(j, _, y_ref=y_ref):
            y_ref[pl.ds(pl.multiple_of(j * half, half), half), :] = jnp.zeros((half, D_MODEL), BF16)
            return 0

        lax.fori_loop(0, lax.shift_right_logical(n_half, 2), full, 0)

        @pl.when((n_half & 2) != 0)
        def _(tile=tile, n_half=n_half):
            tile(pl.ds(pl.multiple_of((n_half & -4) * half, WIDE_ROWS), ROWS))

        @pl.when((n_half & 1) != 0)
        def _(tile=tile, n_half=n_half):
            tile(pl.ds(pl.multiple_of((n_half - 1) * half, half), half))

        lax.fori_loop(n_half, y_ref.shape[0] // half, blank, 0)


def _combine_kernel(s_ref, y_ref, posm_ref, x1_ref, mod_ref, o_ref, yw_s, sel_s, acc_s, sem,
                    *, n_blk, cap_pad):
    for t in range(STEP_BLOCKS):
        rows = pl.ds(t * ROWS, ROWS)
        _combine_block(pl.program_id(0) * STEP_BLOCKS + t, t, s_ref, y_ref, posm_ref.at[t], x1_ref.at[rows, :],
                       mod_ref, o_ref.at[rows, :], yw_s, sel_s, acc_s, sem, n_blk, cap_pad)


def _combine_block(b, buf, s_ref, y_ref, posm_ref, x1_ref, mod_ref, o_ref, yw_s, sel_s, acc_s, sem, n_blk, cap_pad):
    base, units, first_unit, n_units = _block_units(s_ref, b)

    def unit_rows(first_unit_e, u):
        return pl.ds(pl.multiple_of((first_unit_e + u) * WIN, WIN), WIN)

    def bounds(base_e, u):
        lo = base_e + u * WIN
        return lo, jnp.minimum(lo, cap_pad - WIN)

    def copies(slot, blk, fn):
        blk_base, blk_units, blk_first, _ = _block_units(s_ref, blk)

        def one(e, u):
            src = pl.ds(pl.multiple_of(bounds(blk_base[e], u)[1], ALIGN), WIN)
            fn(pltpu.make_async_copy(y_ref.at[e, src, :], yw_s.at[slot, unit_rows(blk_first[e], u), :],
                                     sem.at[slot]))

        _for_units(blk_units, one)

    @pl.when(b == 0)
    def _():
        yw_s[...] = jnp.zeros_like(yw_s)
        copies(0, 0, lambda c: c.start())

    @pl.when(b + 1 < n_blk)
    def _():
        copies(1 - buf, b + 1, lambda c: c.start())

    def pick(e, u):
        lo, first = bounds(base[e], u)
        sel_s[unit_rows(first_unit[e], u), :] = _window_onehot(posm_ref, e, first, lo)

    _for_units(units, pick)
    n_chunks = jnp.maximum(_ceil_div_pow2(n_units * WIN, ROWS), COMMON_ROWS // ROWS)

    def blank(u, _):
        sel_s[unit_rows(u, 0), :] = jnp.zeros((WIN, ROWS), F32)
        return 0

    lax.fori_loop(n_units, n_chunks * (ROWS // WIN), blank, 0)
    copies(buf, b, lambda c: c.wait())

    def weighted(rows):
        return _dot(sel_s[rows, :].T.astype(BF16), yw_s[buf, rows, :])

    def add_chunk(c, _):
        acc_s[...] += weighted(pl.ds(pl.multiple_of(c * ROWS, ROWS), ROWS))
        return 0

    acc_s[...] = weighted(pl.ds(0, COMMON_ROWS))
    lax.fori_loop(COMMON_ROWS // ROWS, n_chunks, add_chunk, 0)
    o_ref[...] = x1_ref[...] + mod_ref[5:6, :] * acc_s[...]


def _padded_capacity(cap, n_blk):
    return -(-(cap + (ALIGN - 1) * n_blk + WIN) // ROWS) * ROWS


def _step_spec(*shape):
    return pl.BlockSpec((STEP_BLOCKS,) + shape, lambda i, s: (i,) + (0,) * len(shape))


def _dispatch(h2, posm, s_tab, cap):
    n_blk = h2.shape[0] // ROWS
    cap_pad = _padded_capacity(cap, n_blk)
    return pl.pallas_call(
        functools.partial(_dispatch_kernel, n_blk=n_blk, cap_pad=cap_pad),
        out_shape=jax.ShapeDtypeStruct((N_EXPERTS, cap_pad, XE_W), BF16),
        grid_spec=pltpu.PrefetchScalarGridSpec(
            num_scalar_prefetch=1, grid=(n_blk // STEP_BLOCKS,),
            in_specs=[pl.BlockSpec((STEP_BLOCKS * ROWS, XE_W), lambda i, s: (i, 0)), _step_spec(N_EXPERTS, ROWS)],
            out_specs=pl.BlockSpec(memory_space=pl.ANY),
            scratch_shapes=[pltpu.VMEM((MAX_UNITS * WIN, ROWS), BF16), pltpu.VMEM((2, MAX_UNITS * WIN, XE_W), BF16),
                            pltpu.SemaphoreType.DMA((2, N_EXPERTS))]),
        compiler_params=_params(1), name="dispatch",
    )(s_tab, h2, posm)


def _ffn(xes, s_tabs, wg, wu, wd):
    n_sets = len(xes)
    per_expert = lambda rows, cols: pl.BlockSpec((None, rows, cols), lambda e, *s: (e, 0, 0))
    return pl.pallas_call(
        functools.partial(_ffn_kernel, n_blks=tuple(s.shape[1] - 1 for s in s_tabs)),
        out_shape=[jax.ShapeDtypeStruct(xe.shape[:2] + (D_MODEL,), BF16) for xe in xes],
        grid_spec=pltpu.PrefetchScalarGridSpec(
            num_scalar_prefetch=n_sets, grid=(N_EXPERTS,),
            in_specs=[per_expert(xe.shape[1], XE_W) for xe in xes]
            + [per_expert(D_MODEL, EXPERT_FF), per_expert(D_MODEL, EXPERT_FF), per_expert(EXPERT_FF, D_MODEL)],
            out_specs=[per_expert(xe.shape[1], D_MODEL) for xe in xes],
            scratch_shapes=[pltpu.VMEM((D_MODEL, EXPERT_FF), BF16), pltpu.VMEM((D_MODEL, EXPERT_FF), BF16),
                            pltpu.VMEM((EXPERT_FF, D_MODEL), BF16)]),
        compiler_params=_params(1), name="ffn",
    )(*s_tabs, *xes, wg, wu, wd)


def _combine(y, posm, s_tab, x1, mod, n_tok):
    n_all = x1.shape[0]
    n_blk = n_all // ROWS
    shared_mod = mod.shape[0] == 1
    assert shared_mod or n_tok % (STEP_BLOCKS * ROWS) == 0
    steps_per_seq = max(n_tok // (STEP_BLOCKS * ROWS), 1)
    mod_map = (lambda i, s: (0, 0, 0)) if shared_mod else (lambda i, s: (i // steps_per_seq, 0, 0))
    rows = pl.BlockSpec((STEP_BLOCKS * ROWS, D_MODEL), lambda i, s: (i, 0))
    return pl.pallas_call(
        functools.partial(_combine_kernel, n_blk=n_blk, cap_pad=y.shape[1]),
        out_shape=jax.ShapeDtypeStruct((n_all, D_MODEL), F32),
        grid_spec=pltpu.PrefetchScalarGridSpec(
            num_scalar_prefetch=1, grid=(n_blk // STEP_BLOCKS,),
            in_specs=[pl.BlockSpec(memory_space=pl.ANY), _step_spec(N_EXPERTS, ROWS), rows,
                      pl.BlockSpec((None, 6, D_MODEL), mod_map)],
            out_specs=rows,
            scratch_shapes=[pltpu.VMEM((2, MAX_UNITS * WIN, D_MODEL), BF16), pltpu.VMEM((MAX_UNITS * WIN, ROWS), F32),
                            pltpu.VMEM((ROWS, D_MODEL), F32), pltpu.SemaphoreType.DMA((2,))]),
        compiler_params=_params(1), name="combine",
    )(s_tab, y, posm, x1, mod)


def _rope_tables(n_tok):
    rows = n_tok // GRID_W
    row = np.repeat(np.arange(rows, dtype=np.float64), GRID_W)
    col = np.tile(np.arange(GRID_W, dtype=np.float64), rows)
    axis_dim = QK_ROPE // 2
    inv_freq = ROPE_THETA ** (-np.arange(0, axis_dim, 2, dtype=np.float64) / axis_dim)
    ang = np.concatenate([row[:, None] * inv_freq, col[:, None] * inv_freq], axis=-1)
    cos = np.ones((n_tok, LANES))
    sin = np.zeros((n_tok, LANES))
    cos[:, ROPE_LANE0:ROPE_LANE0 + QK_ROPE] = np.repeat(np.cos(ang), 2, axis=1)
    sgn = np.tile(np.array([-1.0, 1.0]), QK_ROPE // 2)
    sin[:, ROPE_LANE0:ROPE_LANE0 + QK_ROPE] = np.repeat(np.sin(ang), 2, axis=1) * sgn
    return jnp.asarray(cos, F32), jnp.asarray(sin, F32)


def _dft_tables(n_tok):
    def cs(n):
        k = np.arange(n)
        ang = 2.0 * np.pi * ((k[:, None] * k[None, :]) % n) / n
        return np.cos(ang), np.sin(ang)
    cc, sc = cs(FNET_CH)
    cn, sn = cs(n_tok)
    half = n_tok // 2
    folded = np.concatenate([cn[:, :half], -sn[:, :half]], axis=1)
    flip = np.fliplr(np.eye(min(ROWS, half)))
    return tuple(jnp.asarray(t, F32) for t in (cc, sc, folded)) + (jnp.asarray(flip, BF16),)


def _pad_heads(w, width):
    lead = w.shape[:-1]
    w = w.reshape(lead + (N_HEADS, width))
    w = jnp.pad(w, [(0, 0)] * len(lead) + [(0, 0), (0, HEAD_PAD - width)])
    return w.reshape(lead + (N_HEADS * HEAD_PAD,))


def _pair_swap_lanes(n_groups):
    perm = np.arange(n_groups * LANES).reshape(n_groups, LANES)
    rot = perm[:, ROPE_LANE0:ROPE_LANE0 + QK_ROPE].reshape(n_groups, QK_ROPE // 2, 2)[:, :, ::-1]
    perm[:, ROPE_LANE0:ROPE_LANE0 + QK_ROPE] = rot.reshape(n_groups, QK_ROPE)
    return perm.reshape(-1)


def _spare_from_rotary(t, src=None):
    src = t if src is None else src
    lead = t.shape[:-1]
    t, src = t.reshape(lead + (-1, LANES)), src.reshape(lead + (-1, LANES))
    t = jnp.concatenate([t[..., :QK_DIM], src[..., ROPE_LANE0:ROPE_LANE0 + QK_ROPE]], axis=-1)
    return t.reshape(lead + (-1,))


def _head_tables(gain, rope_tabs, n_tok, query):
    if rope_tabs is None:
        tab = jnp.broadcast_to(gain, (n_tok, LANES))
        return (tab if query else _spare_from_rotary(tab),)
    cos, sin = rope_tabs
    a, b = gain * cos, gain[:, _pair_swap_lanes(1)] * sin
    return (_spare_from_rotary(a, b),) if query else (_spare_from_rotary(a), _spare_from_rotary(b))


def _layer0_weights(norm1, norm2, w_in, q_a_norm, w_q_up, q_norm, kv_a_norm, w_kv_up, k_norm, conv_w, w_o):
    c0 = Q_LORA + KV_LORA
    rope_cols = jnp.pad(w_in[:, c0:c0 + QK_ROPE], ((0, 0), (ROPE_LANE0, LANES - ROPE_LANE0 - QK_ROPE)))
    w_in_pad = jnp.concatenate([w_in[:, :c0], _spare_from_rotary(rope_cols),
                                _spare_from_rotary(rope_cols[:, _pair_swap_lanes(1)]),
                                w_in[:, c0 + QK_ROPE:]], axis=1)
    kv = w_kv_up.reshape(KV_LORA, N_HEADS, QK_NOPE + V_HEAD)
    w_q = _pad_heads(w_q_up, QK_DIM)
    w_q = _spare_from_rotary(w_q, w_q[:, _pair_swap_lanes(N_HEADS)])
    head_gain = lambda g: jnp.pad(g, (0, HEAD_PAD - QK_DIM)).reshape(1, -1)
    return dict(
        norm1=norm1.reshape(1, -1), norm2=norm2.reshape(1, -1), w_in=w_in_pad.astype(BF16),
        q_a_norm=q_a_norm.reshape(1, -1), w_q=w_q.astype(BF16),
        q_gain=head_gain(q_norm) * (QK_DIM ** -0.5), kv_a_norm=kv_a_norm.reshape(1, -1),
        w_k=_pad_heads(kv[:, :, :QK_NOPE].reshape(KV_LORA, -1), QK_NOPE).astype(BF16),
        w_v=kv[:, :, QK_NOPE:].reshape(KV_LORA, -1).astype(BF16),
        w_v_wide=_pad_heads(kv[:, :, QK_NOPE:].reshape(KV_LORA, -1), V_HEAD).astype(BF16), k_gain=head_gain(k_norm),
        conv_w=conv_w, w_o=w_o.astype(BF16))


def _router_weights(w_router):
    hi, lo = _split_hi_lo(jnp.pad(w_router, ((0, 0), (0, LANES - N_EXPERTS))))
    return dict(wr_hi=hi, wr_lo=lo)


def _moe(sets, tri, w):
    routed = []
    for x1, h2, aff, mod, n_tok in sets:
        cap = CAPACITY_FACTOR * x1.shape[0] // N_EXPERTS
        posm, s_tab = _route(aff, tri, cap)
        routed.append((posm, s_tab, _dispatch(h2, posm, s_tab, cap)))
    ys = _ffn([r[2] for r in routed], [r[1] for r in routed], w["wg"], w["wu"], w["wd"])
    return [_combine(y, posm, s_tab, x1, mod, n_tok)
            for y, (posm, s_tab, _), (x1, _, _, mod, n_tok) in zip(ys, routed, sets)]


def _mixer0(x, mod, l0, ctx, rope_tabs):
    n_b, n_tok, _ = x.shape
    n_keys = n_tok + (0 if ctx is None else ctx[0].shape[1])
    w_v = l0["w_v_wide"] if n_keys > WIDE_ROWS else l0["w_v"]
    q, k, v, conv, ckv, kr = _front0(x, mod, l0, rope_tabs, w_v)
    if ctx is not None:
        kc, vc = _ctx_kv(ctx[0], ctx[1], l0, w_v)
        attn = _attention(q, k, v, kc, vc)
    else:
        attn = _attention(q, k, v)
    flat = lambda a: a.reshape(n_b * n_tok, a.shape[-1])
    return _post0(flat(x), flat(attn), flat(conv), mod, l0, n_tok), (ckv, kr[:, :, ROPE_LANE0:ROPE_LANE0 + QK_ROPE])


def kernel(x_prompt, x_sample, c, cache_c_kv_l0, cache_k_rope_l0, c_ctx, norm1_l0, norm2_l0, w_mod_l0, b_mod_l0, w_in_l0, q_a_norm_l0, w_q_up_l0, q_norm_l0, kv_a_norm_l0, w_kv_up_l0, k_norm_l0, conv_w_l0, w_o_l0, w_router_l0, w_gate_l0, w_up_l0, w_down_l0, norm1_l1, norm2_l1, w_mod_l1, b_mod_l1, w_f_l1, w_router_l1, w_gate_l1, w_up_l1, w_down_l1):
    n_dec = c.shape[0]
    cond = jnp.concatenate([c_ctx[None, :], c, jnp.zeros((16 - 1 - n_dec, D_MODEL), F32)], axis=0)
    m0 = _modulation(cond, w_mod_l0, b_mod_l0)
    m1 = _modulation(cond, w_mod_l1, b_mod_l1)
    mods_prompt = (m0[0:1], m1[0:1])
    mods_sample = (m0[1:1 + n_dec], m1[1:1 + n_dec])

    l0 = _layer0_weights(norm1_l0, norm2_l0, w_in_l0, q_a_norm_l0, w_q_up_l0, q_norm_l0, kv_a_norm_l0,
                         w_kv_up_l0, k_norm_l0, conv_w_l0, w_o_l0)
    l0.update(_router_weights(w_router_l0))
    l0.update(wg=w_gate_l0, wu=w_up_l0, wd=w_down_l0)
    l1 = dict(norm1=norm1_l1.reshape(1, -1), norm2=norm2_l1.reshape(1, -1), w_f=w_f_l1.astype(BF16))
    l1.update(_router_weights(w_router_l1))
    l1.update(wg=w_gate_l1, wu=w_up_l1, wd=w_down_l1)

    tri = jnp.asarray(np.triu(np.ones((ROWS, ROWS)), 1), BF16)
    k_rope_pad = _spare_from_rotary(
        jnp.pad(cache_k_rope_l0, ((0, 0), (0, 0), (ROPE_LANE0, LANES - ROPE_LANE0 - QK_ROPE))))
    xs = (x_prompt, x_sample)
    mods = (mods_prompt, mods_sample)
    n_toks = tuple(x.shape[1] for x in xs)

    (front_p, (new_c_kv, new_k_rope)) = _mixer0(x_prompt, mods_prompt[0], l0, None, None)
    (front_s, _) = _mixer0(x_sample, mods_sample[0], l0, (cache_c_kv_l0, k_rope_pad), _rope_tables(n_toks[1]))
    ys = _moe([tuple(front) + (mod[0], n_tok) for front, mod, n_tok in zip((front_p, front_s), mods, n_toks)],
              tri, l0)

    sets = []
    for y, x, mod, n_tok in zip(ys, xs, mods, n_toks):
        x1, h2, aff = _fnet(y.reshape(x.shape), mod[1], l1, _dft_tables(n_tok))
        sets.append((x1.reshape(-1, D_MODEL), h2.reshape(-1, XE_W), aff, mod[1], n_tok))
    y_prompt, y_sample = (y.reshape(x.shape) for y, x in zip(_moe(sets, tri, l1), xs))
    return (y_prompt, y_sample, new_c_kv, new_k_rope)
```

```python
import functools

import jax
import jax.numpy as jnp
import numpy as np
from jax import lax
from jax.experimental import pallas as pl
from jax.experimental.pallas import tpu as pltpu

D_MODEL = 1024
GRID_W = 64
N_HEADS = 8
QK_NOPE = 64
QK_ROPE = 32
QK_DIM = QK_NOPE + QK_ROPE
V_HEAD = 64
Q_LORA = 384
KV_LORA = 256
CONV_CH = 512
FNET_GROUPS = 4
FNET_CH = D_MODEL // FNET_GROUPS
N_EXPERTS = 16
EXPERT_FF = 512
CAPACITY_FACTOR = 2
ROPE_THETA = 10000.0
EPS = 1e-6

LANES = 128
HEAD_PAD = LANES
ROWS = 256
WIDE_ROWS = 512
ATTN_ROWS = 1024
ALIGN = 8
WIN = 64
MAX_UNITS = N_EXPERTS * (ROWS // WIN)
COMMON_ROWS = N_EXPERTS * WIN
STEP_BLOCKS = 2
XE_W = D_MODEL + LANES
IN0_PAD = Q_LORA + KV_LORA + 2 * LANES + 3 * CONV_CH
ROPE_LANE0 = QK_NOPE
VMEM_LIMIT = 56 * 1024 * 1024

F32 = jnp.float32
BF16 = jnp.bfloat16


def _dot(a, b):
    return jnp.dot(a, b, preferred_element_type=F32)


def _dot_nt(a, b):
    return lax.dot_general(a, b, (((1,), (1,)), ((), ())), preferred_element_type=F32)


def _split_hi_lo(x):
    hi = x.astype(BF16)
    lo = (x - hi.astype(F32)).astype(BF16)
    return hi, lo


def _params(n_axes):
    return pltpu.CompilerParams(dimension_semantics=("arbitrary",) * n_axes,
                                vmem_limit_bytes=VMEM_LIMIT)


def _rms(x, gain):
    return x * lax.rsqrt(jnp.mean(x * x, axis=-1, keepdims=True) + EPS) * gain


def _mod_kernel(cond_ref, w_ref, b_ref, o_ref):
    c = cond_ref[...]
    s = c * (1.0 / (1.0 + jnp.exp(-c)))
    s_hi, s_lo = _split_hi_lo(s)
    w = w_ref[...].astype(BF16)
    o_ref[...] = _dot(s_hi, w) + _dot(s_lo, w) + b_ref[...]


def _modulation(cond, w_mod, b_mod):
    n_rows = cond.shape[0]
    tn = 1536
    out = pl.pallas_call(
        _mod_kernel,
        out_shape=jax.ShapeDtypeStruct((n_rows, 6 * D_MODEL), F32),
        grid=(6 * D_MODEL // tn,),
        in_specs=[pl.BlockSpec((n_rows, D_MODEL), lambda i: (0, 0)),
                  pl.BlockSpec((D_MODEL, tn), lambda i: (0, i)),
                  pl.BlockSpec((1, tn), lambda i: (0, i))],
        out_specs=pl.BlockSpec((n_rows, tn), lambda i: (0, i)),
        compiler_params=_params(1),
        name="modulation",
    )(cond, w_mod, b_mod.reshape(1, -1))
    return out.reshape(n_rows, 6, D_MODEL)


def _head_norm_rope(xh, gain_cos, swapped_sin):
    lane = lax.broadcasted_iota(jnp.int32, (1, HEAD_PAD), 1)
    ss = jnp.sum(jnp.where(lane < QK_DIM, xh * xh, 0.0), axis=-1, keepdims=True) * (1.0 / QK_DIM)
    y = xh * gain_cos
    if swapped_sin is not None:
        y = y + swapped_sin
    return y * lax.rsqrt(ss + EPS)


def _expand_kv(ckv_bf, r, swapped_sin, wk_ref, wv_ref, gain_cos, k_ref, v_ref, rows):
    kf = _dot(ckv_bf, wk_ref[...])
    vf = _dot(ckv_bf, wv_ref[...])
    if v_ref.shape[-1] == N_HEADS * HEAD_PAD:
        lane = lax.broadcasted_iota(jnp.int32, (1, N_HEADS * HEAD_PAD), 1)
        vf = vf + jnp.where((lane & (HEAD_PAD - 1)) == V_HEAD, 1.0, 0.0)
    v_ref[rows, :] = vf.astype(BF16)
    for h in range(N_HEADS):
        sl = slice(h * HEAD_PAD, (h + 1) * HEAD_PAD)
        k_ref[rows, sl] = _head_norm_rope(kf[:, sl] + r, gain_cos, swapped_sin).astype(BF16)


def _front0_kernel(*refs, n_tok, rope, state):
    (x_ref, mod_ref, n1_ref, win_ref, qan_ref, wq_ref, kvan_ref, wk_ref, wv_ref, cw_ref,
     qa_ref, ka_ref) = refs[:12]
    pos = 12
    if rope:
        kb_ref = refs[pos]
        pos += 1
    q_ref, k_ref, v_ref, conv_ref = refs[pos:pos + 4]
    if state:
        ckv_ref, kr_ref = refs[pos + 4:pos + 6]
    cu_s, gb_s = refs[-2:]
    sh1, sc1 = mod_ref[0:1, :], mod_ref[1:2, :]
    c_rope = Q_LORA + KV_LORA
    wide = min(n_tok, WIDE_ROWS)
    for c in range(n_tok // ROWS):
        rows = pl.ds(c * ROWS, ROWS)
        if (c * ROWS) % wide == 0:
            wide_rows = pl.ds(c * ROWS, wide)
            h = _rms(x_ref[wide_rows, :], n1_ref[...]) * (1.0 + sc1) + sh1
            proj_wide = _dot(h.astype(BF16), win_ref[...])
        off = (c * ROWS) % wide
        proj = proj_wide[off:off + ROWS, :]
        cq = _rms(proj[:, :Q_LORA], qan_ref[...]).astype(BF16)
        qf = _dot(cq, wq_ref[...])
        for hd in range(N_HEADS):
            sl = slice(hd * HEAD_PAD, (hd + 1) * HEAD_PAD)
            q_ref[rows, sl] = _head_norm_rope(qf[:, sl], qa_ref[rows, :], None).astype(BF16)
        ckv = _rms(proj[:, Q_LORA:c_rope], kvan_ref[...])
        r = proj[:, c_rope:c_rope + LANES]
        if state:
            ckv_ref[rows, :] = ckv
            kr_ref[rows, :] = r
        k_sin = proj[:, c_rope + LANES:c_rope + 2 * LANES] * kb_ref[rows, :] if rope else None
        _expand_kv(ckv.astype(BF16), r, k_sin, wk_ref, wv_ref, ka_ref[rows, :], k_ref, v_ref, rows)
        c0 = c_rope + 2 * LANES
        gb_s[rows, :] = proj[:, c0:c0 + CONV_CH]
        cu_s[rows, :] = proj[:, c0 + CONV_CH:c0 + 2 * CONV_CH] * proj[:, c0 + 2 * CONV_CH:c0 + 3 * CONV_CH]
    cu = cu_s[...]
    row = lax.broadcasted_iota(jnp.int32, cu.shape, 0)
    prev = jnp.where(row == 0, 0.0, pltpu.roll(cu, 1, 0))
    nxt = jnp.where(row == n_tok - 1, 0.0, pltpu.roll(cu, n_tok - 1, 0))
    conv = gb_s[...] * (cw_ref[0:1, :] * prev + cw_ref[1:2, :] * cu + cw_ref[2:3, :] * nxt)
    conv_ref[...] = conv.astype(BF16)


def _const_spec(shape):
    return pl.BlockSpec(shape, lambda b: (0,) * len(shape))


def _front0(x, mod, w, rope_tabs, w_v, state):
    n_b, n_tok, _ = x.shape
    v_width = w_v.shape[1]
    shared_mod = mod.shape[0] == 1
    seq = lambda width: pl.BlockSpec((None, n_tok, width), lambda b: (b, 0, 0))
    in_specs = [seq(D_MODEL),
                pl.BlockSpec((None, 6, D_MODEL), (lambda b: (0, 0, 0)) if shared_mod else (lambda b: (b, 0, 0))),
                _const_spec((1, D_MODEL)), _const_spec((D_MODEL, IN0_PAD)), _const_spec((1, Q_LORA)),
                _const_spec((Q_LORA, N_HEADS * HEAD_PAD)),
                _const_spec((1, KV_LORA)), _const_spec((KV_LORA, N_HEADS * HEAD_PAD)),
                _const_spec((KV_LORA, v_width)), _const_spec((3, CONV_CH)),
                _const_spec((n_tok, LANES)), _const_spec((n_tok, LANES))]
    q_tabs = _head_tables(w["q_gain"], rope_tabs, n_tok, True)
    k_tabs = _head_tables(w["k_gain"], rope_tabs, n_tok, False)
    args = [x, mod, w["norm1"], w["w_in"], w["q_a_norm"], w["w_q"], w["kv_a_norm"],
            w["w_k"], w_v, w["conv_w"], q_tabs[0], k_tabs[0]]
    if rope_tabs is not None:
        in_specs += [_const_spec((n_tok, LANES))]
        args += [k_tabs[1]]
    out_shape = [jax.ShapeDtypeStruct((n_b, n_tok, N_HEADS * HEAD_PAD), BF16),
                 jax.ShapeDtypeStruct((n_b, n_tok, N_HEADS * HEAD_PAD), BF16),
                 jax.ShapeDtypeStruct((n_b, n_tok, v_width), BF16),
                 jax.ShapeDtypeStruct((n_b, n_tok, CONV_CH), BF16)]
    out_specs = [seq(N_HEADS * HEAD_PAD), seq(N_HEADS * HEAD_PAD), seq(v_width), seq(CONV_CH)]
    if state:
        out_shape += [jax.ShapeDtypeStruct((n_b, n_tok, KV_LORA), F32),
                      jax.ShapeDtypeStruct((n_b, n_tok, LANES), F32)]
        out_specs += [seq(KV_LORA), seq(LANES)]
    return pl.pallas_call(
        functools.partial(_front0_kernel, n_tok=n_tok, rope=rope_tabs is not None, state=state),
        out_shape=out_shape, grid=(n_b,), in_specs=in_specs, out_specs=out_specs,
        scratch_shapes=[pltpu.VMEM((n_tok, CONV_CH), F32), pltpu.VMEM((n_tok, CONV_CH), F32)],
        compiler_params=_params(1), name="front0",
    )(*args)


def _ctx_kv_kernel(ckv_ref, r_ref, wk_ref, wv_ref, ka_ref, k_ref, v_ref):
    rows = pl.ds(0, ckv_ref.shape[0])
    _expand_kv(ckv_ref[...].astype(BF16), r_ref[...], None, wk_ref, wv_ref, ka_ref[...], k_ref, v_ref, rows)


def _ctx_kv(cache_c_kv, cache_k_rope_pad, w, w_v):
    n_b, n_ctx, _ = cache_c_kv.shape
    v_width = w_v.shape[1]
    seq = lambda width: pl.BlockSpec((None, n_ctx, width), lambda b: (b, 0, 0))
    return pl.pallas_call(
        _ctx_kv_kernel,
        out_shape=[jax.ShapeDtypeStruct((n_b, n_ctx, N_HEADS * HEAD_PAD), BF16),
                   jax.ShapeDtypeStruct((n_b, n_ctx, v_width), BF16)],
        grid=(n_b,),
        in_specs=[seq(KV_LORA), seq(LANES), _const_spec((KV_LORA, N_HEADS * HEAD_PAD)),
                  _const_spec((KV_LORA, v_width)), _const_spec((n_ctx, LANES))],
        out_specs=[seq(N_HEADS * HEAD_PAD), seq(v_width)],
        compiler_params=_params(1), name="ctx_kv",
    )(cache_c_kv, cache_k_rope_pad, w["w_k"], w_v, _head_tables(w["k_gain"], None, n_ctx, False)[0])


def _attn_kernel(*refs, with_ctx):
    if with_ctx:
        q_ref, k_ref, v_ref, kc_ref, vc_ref, o_ref = refs
    else:
        q_ref, k_ref, v_ref, o_ref = refs
    ones_lane = v_ref.shape[-1] == N_HEADS * HEAD_PAD
    lane = lax.broadcasted_iota(jnp.int32, (q_ref.shape[0], LANES), 1)
    for pair in range(N_HEADS // 2):
        outs = []
        for hd in (2 * pair, 2 * pair + 1):
            sl = slice(hd * HEAD_PAD, (hd + 1) * HEAD_PAD)
            vsl = sl if ones_lane else slice(pair * LANES, (pair + 1) * LANES)
            qh = q_ref[:, sl]
            s = _dot_nt(qh, k_ref[:, sl])
            m = jnp.max(s, axis=-1, keepdims=True)
            if with_ctx:
                sc = _dot_nt(qh, kc_ref[:, sl])
                m = jnp.maximum(m, jnp.max(sc, axis=-1, keepdims=True))
            p = jnp.exp(s - m)
            o = _dot(p.astype(BF16), v_ref[:, vsl])
            if with_ctx:
                pc = jnp.exp(sc - m)
                o = o + _dot(pc.astype(BF16), vc_ref[:, vsl])
            if ones_lane:
                den = o[:, V_HEAD:V_HEAD + 1]
            else:
                den = jnp.sum(p, axis=-1, keepdims=True)
                if with_ctx:
                    den = den + jnp.sum(pc, axis=-1, keepdims=True)
            outs.append(o / den)
        odd = pltpu.roll(outs[1], V_HEAD, 1) if ones_lane else outs[1]
        o_ref[:, pair * LANES:(pair + 1) * LANES] = jnp.where(lane < V_HEAD, outs[0], odd).astype(BF16)


def _attention(q, k, v, kc=None, vc=None):
    n_b, n_tok, _ = q.shape
    with_ctx = kc is not None
    tq = min(n_tok, ATTN_ROWS)
    qspec = lambda width: pl.BlockSpec((None, tq, width), lambda b, i: (b, i, 0))
    kvspec = lambda a: pl.BlockSpec((None,) + a.shape[1:], lambda b, i: (b, 0, 0))
    in_specs = [qspec(N_HEADS * HEAD_PAD), kvspec(k), kvspec(v)]
    args = [q, k, v]
    if with_ctx:
        in_specs += [kvspec(kc), kvspec(vc)]
        args += [kc, vc]
    return pl.pallas_call(
        functools.partial(_attn_kernel, with_ctx=with_ctx),
        out_shape=jax.ShapeDtypeStruct((n_b, n_tok, N_HEADS * V_HEAD), BF16),
        grid=(n_b, n_tok // tq), in_specs=in_specs, out_specs=qspec(N_HEADS * V_HEAD),
        compiler_params=_params(2), name="attention",
    )(*args)


def _moe_front(x1, mod_ref, n2_ref, wrh_ref, wrl_ref, h2_ref, aff_ref, rows):
    sh2, sc2 = mod_ref[3:4, :], mod_ref[4:5, :]
    h2 = _rms(x1, n2_ref[...]) * (1.0 + sc2) + sh2
    h_hi, h_lo = _split_hi_lo(h2)
    logit = _dot(h_hi, wrh_ref[...]) + _dot(h_lo, wrh_ref[...]) + _dot(h_hi, wrl_ref[...])
    real = lax.broadcasted_iota(jnp.int32, (1, LANES), 1) < N_EXPERTS
    top = jnp.max(jnp.where(real, logit, -jnp.inf), axis=1, keepdims=True)
    e = jnp.where(real, jnp.exp(logit - top), 0.0)
    a = e / jnp.sum(e, axis=1, keepdims=True)
    a_hi = a.astype(BF16).astype(F32)
    a_mid = (a - a_hi).astype(BF16).astype(F32)
    a_lo = a - a_hi - a_mid
    h2_ref[rows, 0:D_MODEL] = h_hi
    h2_ref[rows, D_MODEL:XE_W] = (a_hi + pltpu.roll(a_mid, N_EXPERTS, 1)
                                  + pltpu.roll(a_lo, 2 * N_EXPERTS, 1)).astype(BF16)
    aff_ref[...] = a.T[0:N_EXPERTS, :]


def _post0_kernel(x_ref, attn_ref, conv_ref, wo_ref, mod_ref, n2_ref, wrh_ref, wrl_ref,
                  x1_ref, h2_ref, aff_ref):
    n_attn = N_HEADS * V_HEAD
    mix = _dot(attn_ref[...], wo_ref[0:n_attn, :]) + _dot(conv_ref[...], wo_ref[n_attn:, :])
    for t in range(x_ref.shape[0] // ROWS):
        rows = pl.ds(t * ROWS, ROWS)
        x1 = x_ref[rows, :] + mod_ref[2:3, :] * mix[t * ROWS:(t + 1) * ROWS, :]
        x1_ref[rows, :] = x1
        _moe_front(x1, mod_ref, n2_ref, wrh_ref, wrl_ref, h2_ref, aff_ref.at[t], rows)


def _post0(x, attn, conv, mod, w, n_tok):
    n_all = x.shape[0]
    rows = min(n_tok, WIDE_ROWS)
    tiles_per_seq = n_tok // rows
    shared_mod = mod.shape[0] == 1
    tile = lambda width: pl.BlockSpec((rows, width), lambda i: (i, 0))
    const = lambda shape: pl.BlockSpec(shape, lambda i: (0,) * len(shape))
    mod_map = (lambda i: (0, 0, 0)) if shared_mod else (lambda i: (i // tiles_per_seq, 0, 0))
    return pl.pallas_call(
        _post0_kernel,
        out_shape=[jax.ShapeDtypeStruct((n_all, D_MODEL), F32),
                   jax.ShapeDtypeStruct((n_all, XE_W), BF16),
                   jax.ShapeDtypeStruct((n_all // ROWS, N_EXPERTS, ROWS), F32)],
        grid=(n_all // rows,),
        in_specs=[tile(D_MODEL), tile(N_HEADS * V_HEAD), tile(CONV_CH), const((D_MODEL, D_MODEL)),
                  pl.BlockSpec((None, 6, D_MODEL), mod_map), const((1, D_MODEL)),
                  const((D_MODEL, LANES)), const((D_MODEL, LANES))],
        out_specs=[tile(D_MODEL), tile(XE_W),
                   pl.BlockSpec((rows // ROWS, N_EXPERTS, ROWS), lambda i: (i, 0, 0))],
        compiler_params=_params(1), name="post0",
    )(x, attn, conv, w["w_o"], mod, w["norm2"], w["wr_hi"], w["wr_lo"])


def _fnet_kernel(x_ref, mod_ref, n1_ref, cc_ref, sc_ref, dft_ref, flip_ref, wf_ref, n2_ref, wrh_ref, wrl_ref,
                 x1_ref, h2_ref, aff_ref, h_s, r_s, y_s, *, n_tok):
    sh1, sc1, g1 = mod_ref[0:1, :], mod_ref[1:2, :], mod_ref[2:3, :]
    n_chunks = n_tok // ROWS
    half = n_tok // 2
    blk = flip_ref.shape[0]
    cc, sc = cc_ref[...].astype(BF16), sc_ref[...].astype(BF16)
    for c in range(n_chunks):
        rows = pl.ds(c * ROWS, ROWS)
        h_s[rows, :] = (_rms(x_ref[rows, :], n1_ref[...]) * (1.0 + sc1) + sh1).astype(BF16)
    n_flip = half // blk
    for i in range(n_flip):
        src = h_s[pl.ds(half + (n_flip - 1 - i) * blk, blk), :]
        r_s[pl.ds(i * blk, blk), :] = _dot(flip_ref[...], src)
    mirror = pltpu.roll(r_s[...], 1, 0)
    row = lax.broadcasted_iota(jnp.int32, mirror.shape, 0)
    mirror = jnp.where(row == 0, 0.0, mirror)
    low = h_s[0:half, :].astype(F32)
    even, odd = (low + mirror).astype(BF16), (low - mirror).astype(BF16)
    mid = h_s[pl.ds(half, 16), :]
    mids = []
    for g in range(FNET_GROUPS):
        sl = slice(g * FNET_CH, (g + 1) * FNET_CH)
        y_s[0:half, sl] = _dot(even[:, sl], cc).astype(BF16)
        y_s[half:n_tok, sl] = _dot(odd[:, sl], sc).astype(BF16)
        mids.append(_dot(mid[:, sl], cc)[0:1, :])
    mid_row = jnp.concatenate(mids, axis=1)
    scale = 1.0 / float(np.sqrt(n_tok * FNET_CH))
    wide = min(n_tok, WIDE_ROWS)
    for c in range(n_chunks):
        rows = pl.ds(c * ROWS, ROWS)
        if (c * ROWS) % wide == 0:
            parity = lax.broadcasted_iota(jnp.int32, (wide, 1), 0) & 1
            f = _dot(dft_ref[pl.ds(c * ROWS, wide), :].astype(BF16), y_s[...])
            f = (f + jnp.where(parity == 0, 1.0, -1.0) * mid_row) * scale
            mix = _dot(f.astype(BF16), wf_ref[...])
        off = (c * ROWS) % wide
        x1 = x_ref[rows, :] + g1 * mix[off:off + ROWS, :]
        x1_ref[rows, :] = x1
        _moe_front(x1, mod_ref, n2_ref, wrh_ref, wrl_ref, h2_ref, aff_ref.at[c], rows)


def _fnet(x, mod, w, dft):
    n_b, n_tok, _ = x.shape
    shared_mod = mod.shape[0] == 1
    seq = lambda width: pl.BlockSpec((None, n_tok, width), lambda b: (b, 0, 0))
    cc, sc, dft_n, flip = dft
    tiles = n_tok // ROWS
    return pl.pallas_call(
        functools.partial(_fnet_kernel, n_tok=n_tok),
        out_shape=[jax.ShapeDtypeStruct((n_b, n_tok, D_MODEL), F32),
                   jax.ShapeDtypeStruct((n_b, n_tok, XE_W), BF16),
                   jax.ShapeDtypeStruct((n_b * tiles, N_EXPERTS, ROWS), F32)],
        grid=(n_b,),
        in_specs=[seq(D_MODEL),
                  pl.BlockSpec((None, 6, D_MODEL), (lambda b: (0, 0, 0)) if shared_mod else (lambda b: (b, 0, 0))),
                  _const_spec((1, D_MODEL)), _const_spec((FNET_CH, FNET_CH)), _const_spec((FNET_CH, FNET_CH)),
                  pl.BlockSpec((n_tok, n_tok), lambda b: (0, 0), pipeline_mode=pl.Buffered(1)),
                  _const_spec(flip.shape), _const_spec((D_MODEL, D_MODEL)), _const_spec((1, D_MODEL)),
                  _const_spec((D_MODEL, LANES)), _const_spec((D_MODEL, LANES))],
        out_specs=[seq(D_MODEL), seq(XE_W), pl.BlockSpec((tiles, N_EXPERTS, ROWS), lambda b: (b, 0, 0))],
        scratch_shapes=[pltpu.VMEM((n_tok, D_MODEL), BF16), pltpu.VMEM((n_tok // 2, D_MODEL), F32),
                        pltpu.VMEM((n_tok, D_MODEL), BF16)],
        compiler_params=_params(1), name="fnet",
    )(x, mod, w["norm1"], cc, sc, dft_n, flip, w["w_f"], w["norm2"], w["wr_hi"], w["wr_lo"])


def _route_kernel(aff_ref, tri_ref, posm_ref, s_ref, *, n_blk, cap):
    def count(pred):
        acc = jnp.zeros((N_EXPERTS, ROWS), F32)
        for b in range(n_blk):
            acc = acc + jnp.where(pred(aff_ref[b]), 1.0, 0.0)
        return jnp.sum(acc, axis=1, keepdims=True)

    def as_f32(bits):
        return pltpu.bitcast(bits, F32)

    def search(_, carry):
        lo, hi = carry
        mid = lo + ((hi - lo) >> 1)
        mid_f = as_f32(mid)
        ok = count(lambda a: a >= mid_f) >= cap
        return jnp.where(ok, mid, lo), jnp.where(ok, hi, mid)

    one_bits = 0x3F800000
    lo0 = jnp.zeros((N_EXPERTS, 1), jnp.int32)
    hi0 = jnp.full((N_EXPERTS, 1), one_bits + 1, jnp.int32)
    lo, hi = lax.fori_loop(0, 31, search, (lo0, hi0))
    lo_f, ub = as_f32(lo), as_f32(hi)
    thr = lo_f
    pending = jnp.ones((N_EXPERTS, 1), F32)
    for _ in range(3):
        cur = jnp.full((N_EXPERTS, ROWS), -1.0, F32)
        for b in range(n_blk):
            a = aff_ref[b]
            cur = jnp.maximum(cur, jnp.where((a >= lo_f) & (a < ub), a, -1.0))
        cur = jnp.max(cur, axis=1, keepdims=True)
        take = (count(lambda a: a >= cur) >= cap) & (pending > 0.0)
        thr = jnp.where(take, cur, thr)
        pending = jnp.where(take, 0.0, pending)
        ub = cur
    need = cap - count(lambda a: a > thr)

    lane = lax.broadcasted_iota(jnp.int32, (N_EXPERTS, LANES), 1)
    carry_tie = jnp.zeros((N_EXPERTS, 1), F32)
    carry_pos = jnp.zeros((N_EXPERTS, 1), F32)
    s_acc = jnp.zeros((N_EXPERTS, LANES), jnp.int32)
    for b in range(n_blk):
        v = aff_ref[b]
        eq = jnp.where(v == thr, 1.0, 0.0)
        tie_rank = _dot(eq.astype(BF16), tri_ref[...]) + carry_tie
        sel = jnp.where((v > thr) | ((v == thr) & (tie_rank < need)), 1.0, 0.0)
        pos = _dot(sel.astype(BF16), tri_ref[...]) + carry_pos
        posm_ref[b] = jnp.where(sel > 0.0, pos, -1.0)
        s_acc = jnp.where(lane == b, carry_pos.astype(jnp.int32), s_acc)
        carry_tie = carry_tie + jnp.sum(eq, axis=1, keepdims=True)
        n_sel = jnp.sum(sel, axis=1, keepdims=True)
        carry_pos = carry_pos + jnp.floor((n_sel + (ALIGN - 1)) * (1.0 / ALIGN)) * ALIGN
    s_ref[...] = jnp.where(lane == n_blk, carry_pos.astype(jnp.int32), s_acc)


def _route(aff, tri, cap):
    n_blk = aff.shape[0]
    assert n_blk < LANES
    full = lambda shape: pl.BlockSpec(shape, lambda i: (0,) * len(shape))
    posm, s_tab = pl.pallas_call(
        functools.partial(_route_kernel, n_blk=n_blk, cap=cap),
        out_shape=[jax.ShapeDtypeStruct((n_blk, N_EXPERTS, ROWS), F32),
                   jax.ShapeDtypeStruct((N_EXPERTS, LANES), jnp.int32)],
        grid=(1,),
        in_specs=[full((n_blk, N_EXPERTS, ROWS)), full((ROWS, ROWS))],
        out_specs=[full((n_blk, N_EXPERTS, ROWS)), full((N_EXPERTS, LANES))],
        compiler_params=_params(1), name="route",
    )(aff, tri)
    return posm, s_tab[:, :n_blk + 1]


def _ceil_div_pow2(x, d):
    return lax.shift_right_logical(x + (d - 1), d.bit_length() - 1)


def _block_units(s_ref, b):
    base = [s_ref[e, b] for e in range(N_EXPERTS)]
    units = [_ceil_div_pow2(s_ref[e, b + 1] - base[e], WIN) for e in range(N_EXPERTS)]
    first_unit, total = [], 0
    for e in range(N_EXPERTS):
        first_unit.append(total)
        total = total + units[e]
    return base, units, first_unit, total


def _for_units(units, fn):
    for e in range(N_EXPERTS):
        def body(u, _, e=e):
            fn(e, u)
            return 0
        lax.fori_loop(0, units[e], body, 0)


def _window_onehot(posm_ref, e, first, lo):
    slot = lax.broadcasted_iota(jnp.int32, (WIN, ROWS), 0) + first
    p = posm_ref[e:e + 1, :].astype(jnp.int32)
    return jnp.where((p == slot) & (p >= lo), 1.0, 0.0)


def _gate_lanes(e):
    lane = lax.broadcasted_iota(jnp.int32, (1, LANES), 1)
    return ((lane & (N_EXPERTS - 1)) == e) & (lane < 3 * N_EXPERTS)


def _dispatch_kernel(s_ref, h_ref, posm_ref, xe_ref, sel_s, x_s, sem, *, n_blk, cap_pad):
    for t in range(STEP_BLOCKS):
        _dispatch_block(pl.program_id(0) * STEP_BLOCKS + t, t, s_ref, h_ref.at[pl.ds(t * ROWS, ROWS), :],
                        posm_ref.at[t], xe_ref, sel_s, x_s, sem, n_blk, cap_pad)


def _dispatch_block(b, buf, s_ref, h_ref, posm_ref, xe_ref, sel_s, x_s, sem, n_blk, cap_pad):
    base, units, first_unit, n_units = _block_units(s_ref, b)

    @pl.when(b == 0)
    def _():
        sel_s[...] = jnp.zeros_like(sel_s)

    def unit_rows(first_unit_e, u):
        return pl.ds(pl.multiple_of((first_unit_e + u) * WIN, WIN), WIN)

    def pick(e, u):
        first = base[e] + u * WIN
        sel_s[unit_rows(first_unit[e], u), :] = _window_onehot(posm_ref, e, first, first).astype(BF16)

    _for_units(units, pick)

    def gather(rows):
        x_s[buf, rows, :] = _dot(sel_s[rows, :], h_ref[...]).astype(BF16)

    def gather_chunk(c, _):
        gather(pl.ds(pl.multiple_of(c * ROWS, ROWS), ROWS))
        return 0

    gather(pl.ds(0, COMMON_ROWS))
    lax.fori_loop(COMMON_ROWS // ROWS, _ceil_div_pow2(n_units * WIN, ROWS), gather_chunk, 0)

    def unit_copy(slot, blk_base_e, blk_first_e, e, u):
        dst = pl.ds(pl.multiple_of(blk_base_e + u * WIN, ALIGN), WIN)
        return pltpu.make_async_copy(x_s.at[slot, unit_rows(blk_first_e, u), :], xe_ref.at[e, dst, :],
                                     sem.at[slot, e])

    def for_expert_units(n, fn):
        lax.fori_loop(0, n, lambda u, _: (fn(u), 0)[1], 0)

    prev_base, prev_units, prev_first, _ = _block_units(s_ref, jnp.maximum(b - 1, 0))
    for e in range(N_EXPERTS):
        for_expert_units(jnp.where(b > 0, prev_units[e], 0),
                         lambda u, e=e: unit_copy(1 - buf, prev_base[e], prev_first[e], e, u).wait())
        for_expert_units(units[e], lambda u, e=e: unit_copy(buf, base[e], first_unit[e], e, u).start())

    @pl.when(b == n_blk - 1)
    def _():
        for e in range(N_EXPERTS):
            for_expert_units(units[e], lambda u, e=e: unit_copy(buf, base[e], first_unit[e], e, u).wait())
        x_s[buf, 0:WIN, :] = jnp.zeros((WIN, XE_W), BF16)

        def tail(fn):
            for e in range(N_EXPERTS):
                total = s_ref[e, n_blk]
                n_win = (cap_pad - total) // WIN

                def wide(c, _):
                    row = pl.multiple_of(total + c * WIN, ALIGN)
                    fn(pltpu.make_async_copy(x_s.at[buf, pl.ds(0, WIN), :],
                                             xe_ref.at[e, pl.ds(row, WIN), :], sem.at[buf, e]))
                    return 0

                def narrow(c, _):
                    row = pl.multiple_of(total + n_win * WIN + c * ALIGN, ALIGN)
                    fn(pltpu.make_async_copy(x_s.at[buf, pl.ds(0, ALIGN), :],
                                             xe_ref.at[e, pl.ds(row, ALIGN), :], sem.at[buf, e]))
                    return 0

                lax.fori_loop(0, n_win, wide, 0)
                lax.fori_loop(0, (cap_pad - total - n_win * WIN) // ALIGN, narrow, 0)

        tail(lambda c: c.start())
        tail(lambda c: c.wait())


def _ffn_kernel(*refs, n_blks):
    n_sets = len(n_blks)
    s_refs, x_refs = refs[:n_sets], refs[n_sets:2 * n_sets]
    wg_ref, wu_ref, wd_ref = refs[2 * n_sets:2 * n_sets + 3]
    y_refs = refs[2 * n_sets + 3:3 * n_sets + 3]
    wg_s, wu_s, wd_s = refs[3 * n_sets + 3:]
    e = pl.program_id(0)
    wg_s[...] = wg_ref[...].astype(BF16)
    wu_s[...] = wu_ref[...].astype(BF16)
    wd_s[...] = wd_ref[...].astype(BF16)
    mine = _gate_lanes(e)

    half = ROWS // 2
    for s_ref, x_ref, y_ref, n_blk in zip(s_refs, x_refs, y_refs, n_blks):
        n_half = _ceil_div_pow2(s_ref[e, n_blk], half)

        def tile(rows, x_ref=x_ref, y_ref=y_ref):
            x = x_ref[rows, 0:D_MODEL]
            pieces = x_ref[rows, D_MODEL:XE_W].astype(F32)
            gate = jnp.sum(jnp.where(mine, pieces, 0.0), axis=1, keepdims=True)
            a = _dot(x, wg_s[...])
            u = _dot(x, wu_s[...])
            hid = a * (1.0 / (1.0 + jnp.exp(-a))) * u * gate
            y_ref[rows, :] = _dot(hid.astype(BF16), wd_s[...]).astype(BF16)

        def full(j, _, tile=tile):
            tile(pl.ds(pl.multiple_of(j * WIDE_ROWS, WIDE_ROWS), WIDE_ROWS))
            return 0

        def blank(j, _, y_ref=y_ref):
            y_ref[pl.ds(pl.multiple_of(j * half, half), half), :] = jnp.zeros((half, D_MODEL), BF16)
            return 0

        lax.fori_loop(0, lax.shift_right_logical(n_half, 2), full, 0)

        @pl.when((n_half & 2) != 0)
        def _(tile=tile, n_half=n_half):
            tile(pl.ds(pl.multiple_of((n_half & -4) * half, WIDE_ROWS), ROWS))

        @pl.when((n_half & 1) != 0)
        def _(tile=tile, n_half=n_half):
            tile(pl.ds(pl.multiple_of((n_half - 1) * half, half), half))

        lax.fori_loop(n_half, y_ref.shape[0] // half, blank, 0)


def _combine_kernel(s_ref, y_ref, posm_ref, x1_ref, mod_ref, o_ref, yw_s, sel_s, acc_s, sem,
                    *, n_blk, cap_pad):
    for t in range(STEP_BLOCKS):
        rows = pl.ds(t * ROWS, ROWS)
        _combine_block(pl.program_id(0) * STEP_BLOCKS + t, t, s_ref, y_ref, posm_ref.at[t], x1_ref.at[rows, :],
                       mod_ref, o_ref.at[rows, :], yw_s, sel_s, acc_s, sem, n_blk, cap_pad)


def _combine_block(b, buf, s_ref, y_ref, posm_ref, x1_ref, mod_ref, o_ref, yw_s, sel_s, acc_s, sem, n_blk, cap_pad):
    base, units, first_unit, n_units = _block_units(s_ref, b)

    def unit_rows(first_unit_e, u):
        return pl.ds(pl.multiple_of((first_unit_e + u) * WIN, WIN), WIN)

    def bounds(base_e, u):
        lo = base_e + u * WIN
        return lo, jnp.minimum(lo, cap_pad - WIN)

    def copies(slot, blk, fn):
        blk_base, blk_units, blk_first, _ = _block_units(s_ref, blk)

        def one(e, u):
            src = pl.ds(pl.multiple_of(bounds(blk_base[e], u)[1], ALIGN), WIN)
            fn(pltpu.make_async_copy(y_ref.at[e, src, :], yw_s.at[slot, unit_rows(blk_first[e], u), :],
                                     sem.at[slot]))

        _for_units(blk_units, one)

    @pl.when(b == 0)
    def _():
        yw_s[...] = jnp.zeros_like(yw_s)
        copies(0, 0, lambda c: c.start())

    @pl.when(b + 1 < n_blk)
    def _():
        copies(1 - buf, b + 1, lambda c: c.start())

    def pick(e, u):
        lo, first = bounds(base[e], u)
        sel_s[unit_rows(first_unit[e], u), :] = _window_onehot(posm_ref, e, first, lo)

    _for_units(units, pick)
    n_chunks = jnp.maximum(_ceil_div_pow2(n_units * WIN, ROWS), COMMON_ROWS // ROWS)

    def blank(u, _):
        sel_s[unit_rows(u, 0), :] = jnp.zeros((WIN, ROWS), F32)
        return 0

    lax.fori_loop(n_units, n_chunks * (ROWS // WIN), blank, 0)
    copies(buf, b, lambda c: c.wait())

    def weighted(rows):
        return _dot(sel_s[rows, :].T.astype(BF16), yw_s[buf, rows, :])

    def add_chunk(c, _):
        acc_s[...] += weighted(pl.ds(pl.multiple_of(c * ROWS, ROWS), ROWS))
        return 0

    acc_s[...] = weighted(pl.ds(0, COMMON_ROWS))
    lax.fori_loop(COMMON_ROWS // ROWS, n_chunks, add_chunk, 0)
    o_ref[...] = x1_ref[...] + mod_ref[5:6, :] * acc_s[...]


def _padded_capacity(cap, n_blk):
    return -(-(cap + (ALIGN - 1) * n_blk + WIN) // ROWS) * ROWS


def _step_spec(*shape):
    return pl.BlockSpec((STEP_BLOCKS,) + shape, lambda i, s: (i,) + (0,) * len(shape))


def _dispatch(h2, posm, s_tab, cap):
    n_blk = h2.shape[0] // ROWS
    cap_pad = _padded_capacity(cap, n_blk)
    return pl.pallas_call(
        functools.partial(_dispatch_kernel, n_blk=n_blk, cap_pad=cap_pad),
        out_shape=jax.ShapeDtypeStruct((N_EXPERTS, cap_pad, XE_W), BF16),
        grid_spec=pltpu.PrefetchScalarGridSpec(
            num_scalar_prefetch=1, grid=(n_blk // STEP_BLOCKS,),
            in_specs=[pl.BlockSpec((STEP_BLOCKS * ROWS, XE_W), lambda i, s: (i, 0)), _step_spec(N_EXPERTS, ROWS)],
            out_specs=pl.BlockSpec(memory_space=pl.ANY),
            scratch_shapes=[pltpu.VMEM((MAX_UNITS * WIN, ROWS), BF16), pltpu.VMEM((2, MAX_UNITS * WIN, XE_W), BF16),
                            pltpu.SemaphoreType.DMA((2, N_EXPERTS))]),
        compiler_params=_params(1), name="dispatch",
    )(s_tab, h2, posm)


def _ffn(xes, s_tabs, wg, wu, wd):
    n_sets = len(xes)
    per_expert = lambda rows, cols: pl.BlockSpec((None, rows, cols), lambda e, *s: (e, 0, 0))
    return pl.pallas_call(
        functools.partial(_ffn_kernel, n_blks=tuple(s.shape[1] - 1 for s in s_tabs)),
        out_shape=[jax.ShapeDtypeStruct(xe.shape[:2] + (D_MODEL,), BF16) for xe in xes],
        grid_spec=pltpu.PrefetchScalarGridSpec(
            num_scalar_prefetch=n_sets, grid=(N_EXPERTS,),
            in_specs=[per_expert(xe.shape[1], XE_W) for xe in xes]
            + [per_expert(D_MODEL, EXPERT_FF), per_expert(D_MODEL, EXPERT_FF), per_expert(EXPERT_FF, D_MODEL)],
            out_specs=[per_expert(xe.shape[1], D_MODEL) for xe in xes],
            scratch_shapes=[pltpu.VMEM((D_MODEL, EXPERT_FF), BF16), pltpu.VMEM((D_MODEL, EXPERT_FF), BF16),
                            pltpu.VMEM((EXPERT_FF, D_MODEL), BF16)]),
        compiler_params=_params(1), name="ffn",
    )(*s_tabs, *xes, wg, wu, wd)


def _combine(y, posm, s_tab, x1, mod, n_tok):
    n_all = x1.shape[0]
    n_blk = n_all // ROWS
    shared_mod = mod.shape[0] == 1
    assert shared_mod or n_tok % (STEP_BLOCKS * ROWS) == 0
    steps_per_seq = max(n_tok // (STEP_BLOCKS * ROWS), 1)
    mod_map = (lambda i, s: (0, 0, 0)) if shared_mod else (lambda i, s: (i // steps_per_seq, 0, 0))
    rows = pl.BlockSpec((STEP_BLOCKS * ROWS, D_MODEL), lambda i, s: (i, 0))
    return pl.pallas_call(
        functools.partial(_combine_kernel, n_blk=n_blk, cap_pad=y.shape[1]),
        out_shape=jax.ShapeDtypeStruct((n_all, D_MODEL), F32),
        grid_spec=pltpu.PrefetchScalarGridSpec(
            num_scalar_prefetch=1, grid=(n_blk // STEP_BLOCKS,),
            in_specs=[pl.BlockSpec(memory_space=pl.ANY), _step_spec(N_EXPERTS, ROWS), rows,
                      pl.BlockSpec((None, 6, D_MODEL), mod_map)],
            out_specs=rows,
            scratch_shapes=[pltpu.VMEM((2, MAX_UNITS * WIN, D_MODEL), BF16), pltpu.VMEM((MAX_UNITS * WIN, ROWS), F32),
                            pltpu.VMEM((ROWS, D_MODEL), F32), pltpu.SemaphoreType.DMA((2,))]),
        compiler_params=_params(1), name="combine",
    )(s_tab, y, posm, x1, mod)


def _rope_tables(n_tok):
    rows = n_tok // GRID_W
    row = np.repeat(np.arange(rows, dtype=np.float64), GRID_W)
    col = np.tile(np.arange(GRID_W, dtype=np.float64), rows)
    axis_dim = QK_ROPE // 2
    inv_freq = ROPE_THETA ** (-np.arange(0, axis_dim, 2, dtype=np.float64) / axis_dim)
    ang = np.concatenate([row[:, None] * inv_freq, col[:, None] * inv_freq], axis=-1)
    cos = np.ones((n_tok, LANES))
    sin = np.zeros((n_tok, LANES))
    cos[:, ROPE_LANE0:ROPE_LANE0 + QK_ROPE] = np.repeat(np.cos(ang), 2, axis=1)
    sgn = np.tile(np.array([-1.0, 1.0]), QK_ROPE // 2)
    sin[:, ROPE_LANE0:ROPE_LANE0 + QK_ROPE] = np.repeat(np.sin(ang), 2, axis=1) * sgn
    return jnp.asarray(cos, F32), jnp.asarray(sin, F32)


def _dft_tables(n_tok):
    def cs(n):
        k = np.arange(n)
        ang = 2.0 * np.pi * ((k[:, None] * k[None, :]) % n) / n
        return np.cos(ang), np.sin(ang)
    cc, sc = cs(FNET_CH)
    cn, sn = cs(n_tok)
    half = n_tok // 2
    folded = np.concatenate([cn[:, :half], -sn[:, :half]], axis=1)
    flip = np.fliplr(np.eye(min(ROWS, half)))
    return tuple(jnp.asarray(t, F32) for t in (cc, sc, folded)) + (jnp.asarray(flip, BF16),)


def _pad_heads(w, width):
    lead = w.shape[:-1]
    w = w.reshape(lead + (N_HEADS, width))
    w = jnp.pad(w, [(0, 0)] * len(lead) + [(0, 0), (0, HEAD_PAD - width)])
    return w.reshape(lead + (N_HEADS * HEAD_PAD,))


def _pair_swap_lanes(n_groups):
    perm = np.arange(n_groups * LANES).reshape(n_groups, LANES)
    rot = perm[:, ROPE_LANE0:ROPE_LANE0 + QK_ROPE].reshape(n_groups, QK_ROPE // 2, 2)[:, :, ::-1]
    perm[:, ROPE_LANE0:ROPE_LANE0 + QK_ROPE] = rot.reshape(n_groups, QK_ROPE)
    return perm.reshape(-1)


def _spare_from_rotary(t, src=None):
    src = t if src is None else src
    lead = t.shape[:-1]
    t, src = t.reshape(lead + (-1, LANES)), src.reshape(lead + (-1, LANES))
    t = jnp.concatenate([t[..., :QK_DIM], src[..., ROPE_LANE0:ROPE_LANE0 + QK_ROPE]], axis=-1)
    return t.reshape(lead + (-1,))


def _head_tables(gain, rope_tabs, n_tok, query):
    if rope_tabs is None:
        tab = jnp.broadcast_to(gain, (n_tok, LANES))
        return (tab if query else _spare_from_rotary(tab),)
    cos, sin = rope_tabs
    a, b = gain * cos, gain[:, _pair_swap_lanes(1)] * sin
    return (_spare_from_rotary(a, b),) if query else (_spare_from_rotary(a), _spare_from_rotary(b))


def _layer0_weights(norm1, norm2, w_in, q_a_norm, w_q_up, q_norm, kv_a_norm, w_kv_up, k_norm, conv_w, w_o):
    c0 = Q_LORA + KV_LORA
    rope_cols = jnp.pad(w_in[:, c0:c0 + QK_ROPE], ((0, 0), (ROPE_LANE0, LANES - ROPE_LANE0 - QK_ROPE)))
    w_in_pad = jnp.concatenate([w_in[:, :c0], _spare_from_rotary(rope_cols),
                                _spare_from_rotary(rope_cols[:, _pair_swap_lanes(1)]),
                                w_in[:, c0 + QK_ROPE:]], axis=1)
    kv = w_kv_up.reshape(KV_LORA, N_HEADS, QK_NOPE + V_HEAD)
    w_q = _pad_heads(w_q_up, QK_DIM)
    w_q = _spare_from_rotary(w_q, w_q[:, _pair_swap_lanes(N_HEADS)])
    head_gain = lambda g: jnp.pad(g, (0, HEAD_PAD - QK_DIM)).reshape(1, -1)
    return dict(
        norm1=norm1.reshape(1, -1), norm2=norm2.reshape(1, -1), w_in=w_in_pad.astype(BF16),
        q_a_norm=q_a_norm.reshape(1, -1), w_q=w_q.astype(BF16),
        q_gain=head_gain(q_norm) * (QK_DIM ** -0.5), kv_a_norm=kv_a_norm.reshape(1, -1),
        w_k=_pad_heads(kv[:, :, :QK_NOPE].reshape(KV_LORA, -1), QK_NOPE).astype(BF16),
        w_v=kv[:, :, QK_NOPE:].reshape(KV_LORA, -1).astype(BF16),
        w_v_wide=_pad_heads(kv[:, :, QK_NOPE:].reshape(KV_LORA, -1), V_HEAD).astype(BF16), k_gain=head_gain(k_norm),
        conv_w=conv_w, w_o=w_o.astype(BF16))


def _router_weights(w_router):
    hi, lo = _split_hi_lo(jnp.pad(w_router, ((0, 0), (0, LANES - N_EXPERTS))))
    return dict(wr_hi=hi, wr_lo=lo)


def _moe(sets, tri, w):
    routed = []
    for x1, h2, aff, mod, n_tok in sets:
        cap = CAPACITY_FACTOR * x1.shape[0] // N_EXPERTS
        posm, s_tab = _route(aff, tri, cap)
        routed.append((posm, s_tab, _dispatch(h2, posm, s_tab, cap)))
    ys = _ffn([r[2] for r in routed], [r[1] for r in routed], w["wg"], w["wu"], w["wd"])
    return [_combine(y, posm, s_tab, x1, mod, n_tok)
            for y, (posm, s_tab, _), (x1, _, _, mod, n_tok) in zip(ys, routed, sets)]


def _mixer0(x, mod, l0, ctx, rope_tabs, state):
    n_b, n_tok, _ = x.shape
    n_keys = n_tok + (0 if ctx is None else ctx[0].shape[1])
    w_v = l0["w_v_wide"] if n_keys > WIDE_ROWS else l0["w_v"]
    q, k, v, conv, *new_state = _front0(x, mod, l0, rope_tabs, w_v, state)
    if state:
        new_state = (new_state[0], new_state[1][:, :, ROPE_LANE0:ROPE_LANE0 + QK_ROPE])
    if ctx is not None:
        kc, vc = _ctx_kv(ctx[0], ctx[1], l0, w_v)
        attn = _attention(q, k, v, kc, vc)
    else:
        attn = _attention(q, k, v)
    flat = lambda a: a.reshape(n_b * n_tok, a.shape[-1])
    return _post0(flat(x), flat(attn), flat(conv), mod, l0, n_tok), new_state


def kernel(x_prompt, x_sample, c, cache_c_kv_l0, cache_k_rope_l0, c_ctx, norm1_l0, norm2_l0, w_mod_l0, b_mod_l0, w_in_l0, q_a_norm_l0, w_q_up_l0, q_norm_l0, kv_a_norm_l0, w_kv_up_l0, k_norm_l0, conv_w_l0, w_o_l0, w_router_l0, w_gate_l0, w_up_l0, w_down_l0, norm1_l1, norm2_l1, w_mod_l1, b_mod_l1, w_f_l1, w_router_l1, w_gate_l1, w_up_l1, w_down_l1):
    n_dec = c.shape[0]
    cond = jnp.concatenate([c_ctx[None, :], c, jnp.zeros((16 - 1 - n_dec, D_MODEL), F32)], axis=0)
    m0 = _modulation(cond, w_mod_l0, b_mod_l0)
    m1 = _modulation(cond, w_mod_l1, b_mod_l1)
    mods_prompt = (m0[0:1], m1[0:1])
    mods_sample = (m0[1:1 + n_dec], m1[1:1 + n_dec])

    l0 = _layer0_weights(norm1_l0, norm2_l0, w_in_l0, q_a_norm_l0, w_q_up_l0, q_norm_l0, kv_a_norm_l0,
                         w_kv_up_l0, k_norm_l0, conv_w_l0, w_o_l0)
    l0.update(_router_weights(w_router_l0))
    l0.update(wg=w_gate_l0, wu=w_up_l0, wd=w_down_l0)
    l1 = dict(norm1=norm1_l1.reshape(1, -1), norm2=norm2_l1.reshape(1, -1), w_f=w_f_l1.astype(BF16))
    l1.update(_router_weights(w_router_l1))
    l1.update(wg=w_gate_l1, wu=w_up_l1, wd=w_down_l1)

    tri = jnp.asarray(np.triu(np.ones((ROWS, ROWS)), 1), BF16)
    k_rope_pad = _spare_from_rotary(
        jnp.pad(cache_k_rope_l0, ((0, 0), (0, 0), (ROPE_LANE0, LANES - ROPE_LANE0 - QK_ROPE))))
    xs = (x_prompt, x_sample)
    mods = (mods_prompt, mods_sample)
    n_toks = tuple(x.shape[1] for x in xs)

    (front_p, (new_c_kv, new_k_rope)) = _mixer0(x_prompt, mods_prompt[0], l0, None, None, True)
    (front_s, _) = _mixer0(x_sample, mods_sample[0], l0, (cache_c_kv_l0, k_rope_pad), _rope_tables(n_toks[1]),
                           False)
    ys = _moe([tuple(front) + (mod[0], n_tok) for front, mod, n_tok in zip((front_p, front_s), mods, n_toks)],
              tri, l0)

    sets = []
    for y, x, mod, n_tok in zip(ys, xs, mods, n_toks):
        x1, h2, aff = _fnet(y.reshape(x.shape), mod[1], l1, _dft_tables(n_tok))
        sets.append((x1.reshape(-1, D_MODEL), h2.reshape(-1, XE_W), aff, mod[1], n_tok))
    y_prompt, y_sample = (y.reshape(x.shape) for y, x in zip(_moe(sets, tri, l1), xs))
    return (y_prompt, y_sample, new_c_kv, new_k_rope)
```

```python
import functools

import jax
import jax.numpy as jnp
import numpy as np
from jax import lax
from jax.experimental import pallas as pl
from jax.experimental.pallas import tpu as pltpu

D_MODEL = 1024
GRID_W = 64
N_HEADS = 8
QK_NOPE = 64
QK_ROPE = 32
QK_DIM = QK_NOPE + QK_ROPE
V_HEAD = 64
Q_LORA = 384
KV_LORA = 256
CONV_CH = 512
FNET_GROUPS = 4
FNET_CH = D_MODEL // FNET_GROUPS
N_EXPERTS = 16
EXPERT_FF = 512
CAPACITY_FACTOR = 2
ROPE_THETA = 10000.0
EPS = 1e-6

LANES = 128
HEAD_PAD = LANES
ROWS = 256
WIDE_ROWS = 512
MIX_ROWS = 1024
ATTN_ROWS = 1024
ALIGN = 8
WIN = 64
MAX_UNITS = N_EXPERTS * (ROWS // WIN)
COMMON_ROWS = N_EXPERTS * WIN
STEP_BLOCKS = 2
XE_W = D_MODEL + LANES
IN0_PAD = Q_LORA + KV_LORA + 2 * LANES + 3 * CONV_CH
ROPE_LANE0 = QK_NOPE
VMEM_LIMIT = 56 * 1024 * 1024

F32 = jnp.float32
BF16 = jnp.bfloat16


def _dot(a, b):
    return jnp.dot(a, b, preferred_element_type=F32)


def _dot_nt(a, b):
    return lax.dot_general(a, b, (((1,), (1,)), ((), ())), preferred_element_type=F32)


def _split_hi_lo(x):
    hi = x.astype(BF16)
    lo = (x - hi.astype(F32)).astype(BF16)
    return hi, lo


def _params(n_axes):
    return pltpu.CompilerParams(dimension_semantics=("arbitrary",) * n_axes,
                                vmem_limit_bytes=VMEM_LIMIT)


def _rms(x, gain):
    return x * lax.rsqrt(jnp.mean(x * x, axis=-1, keepdims=True) + EPS) * gain


def _mod_kernel(cond_ref, w_ref, b_ref, o_ref):
    c = cond_ref[...]
    s = c * (1.0 / (1.0 + jnp.exp(-c)))
    s_hi, s_lo = _split_hi_lo(s)
    w = w_ref[...].astype(BF16)
    o_ref[...] = _dot(s_hi, w) + _dot(s_lo, w) + b_ref[...]


def _modulation(cond, w_mod, b_mod):
    n_rows = cond.shape[0]
    tn = 1536
    out = pl.pallas_call(
        _mod_kernel,
        out_shape=jax.ShapeDtypeStruct((n_rows, 6 * D_MODEL), F32),
        grid=(6 * D_MODEL // tn,),
        in_specs=[pl.BlockSpec((n_rows, D_MODEL), lambda i: (0, 0)),
                  pl.BlockSpec((D_MODEL, tn), lambda i: (0, i)),
                  pl.BlockSpec((1, tn), lambda i: (0, i))],
        out_specs=pl.BlockSpec((n_rows, tn), lambda i: (0, i)),
        compiler_params=_params(1),
        name="modulation",
    )(cond, w_mod, b_mod.reshape(1, -1))
    return out.reshape(n_rows, 6, D_MODEL)


def _head_norm_rope(xh, gain_cos, swapped_sin):
    lane = lax.broadcasted_iota(jnp.int32, (1, HEAD_PAD), 1)
    ss = jnp.sum(jnp.where(lane < QK_DIM, xh * xh, 0.0), axis=-1, keepdims=True) * (1.0 / QK_DIM)
    y = xh * gain_cos
    if swapped_sin is not None:
        y = y + swapped_sin
    return y * lax.rsqrt(ss + EPS)


def _expand_kv(ckv_bf, r, swapped_sin, wk_ref, wv_ref, gain_cos, k_ref, v_ref, rows):
    kf = _dot(ckv_bf, wk_ref[...])
    vf = _dot(ckv_bf, wv_ref[...])
    if v_ref.shape[-1] == N_HEADS * HEAD_PAD:
        lane = lax.broadcasted_iota(jnp.int32, (1, N_HEADS * HEAD_PAD), 1)
        vf = vf + jnp.where((lane & (HEAD_PAD - 1)) == V_HEAD, 1.0, 0.0)
    v_ref[rows, :] = vf.astype(BF16)
    for h in range(N_HEADS):
        sl = slice(h * HEAD_PAD, (h + 1) * HEAD_PAD)
        k_ref[rows, sl] = _head_norm_rope(kf[:, sl] + r, gain_cos, swapped_sin).astype(BF16)


def _front0_kernel(*refs, n_tok, rope, state):
    (x_ref, mod_ref, n1_ref, win_ref, qan_ref, wq_ref, kvan_ref, wk_ref, wv_ref, cw_ref,
     qa_ref, ka_ref) = refs[:12]
    pos = 12
    if rope:
        kb_ref = refs[pos]
        pos += 1
    q_ref, k_ref, v_ref, conv_ref = refs[pos:pos + 4]
    if state:
        ckv_ref, kr_ref = refs[pos + 4:pos + 6]
    cu_s, gb_s = refs[-2:]
    sh1, sc1 = mod_ref[0:1, :], mod_ref[1:2, :]
    c_rope = Q_LORA + KV_LORA
    wide = min(n_tok, MIX_ROWS)
    for c in range(n_tok // ROWS):
        rows = pl.ds(c * ROWS, ROWS)
        if (c * ROWS) % wide == 0:
            wide_rows = pl.ds(c * ROWS, wide)
            h = _rms(x_ref[wide_rows, :], n1_ref[...]) * (1.0 + sc1) + sh1
            proj_wide = _dot(h.astype(BF16), win_ref[...])
        off = (c * ROWS) % wide
        proj = proj_wide[off:off + ROWS, :]
        cq = _rms(proj[:, :Q_LORA], qan_ref[...]).astype(BF16)
        qf = _dot(cq, wq_ref[...])
        for hd in range(N_HEADS):
            sl = slice(hd * HEAD_PAD, (hd + 1) * HEAD_PAD)
            q_ref[rows, sl] = _head_norm_rope(qf[:, sl], qa_ref[rows, :], None).astype(BF16)
        ckv = _rms(proj[:, Q_LORA:c_rope], kvan_ref[...])
        r = proj[:, c_rope:c_rope + LANES]
        if state:
            ckv_ref[rows, :] = ckv
            kr_ref[rows, :] = r
        k_sin = proj[:, c_rope + LANES:c_rope + 2 * LANES] * kb_ref[rows, :] if rope else None
        _expand_kv(ckv.astype(BF16), r, k_sin, wk_ref, wv_ref, ka_ref[rows, :], k_ref, v_ref, rows)
        c0 = c_rope + 2 * LANES
        gb_s[rows, :] = proj[:, c0:c0 + CONV_CH]
        cu_s[rows, :] = proj[:, c0 + CONV_CH:c0 + 2 * CONV_CH] * proj[:, c0 + 2 * CONV_CH:c0 + 3 * CONV_CH]
    cu = cu_s[...]
    row = lax.broadcasted_iota(jnp.int32, cu.shape, 0)
    prev = jnp.where(row == 0, 0.0, pltpu.roll(cu, 1, 0))
    nxt = jnp.where(row == n_tok - 1, 0.0, pltpu.roll(cu, n_tok - 1, 0))
    conv = gb_s[...] * (cw_ref[0:1, :] * prev + cw_ref[1:2, :] * cu + cw_ref[2:3, :] * nxt)
    conv_ref[...] = conv.astype(BF16)


def _const_spec(shape):
    return pl.BlockSpec(shape, lambda b: (0,) * len(shape))


def _front0(x, mod, w, rope_tabs, w_v, state):
    n_b, n_tok, _ = x.shape
    v_width = w_v.shape[1]
    shared_mod = mod.shape[0] == 1
    seq = lambda width: pl.BlockSpec((None, n_tok, width), lambda b: (b, 0, 0))
    in_specs = [seq(D_MODEL),
                pl.BlockSpec((None, 6, D_MODEL), (lambda b: (0, 0, 0)) if shared_mod else (lambda b: (b, 0, 0))),
                _const_spec((1, D_MODEL)), _const_spec((D_MODEL, IN0_PAD)), _const_spec((1, Q_LORA)),
                _const_spec((Q_LORA, N_HEADS * HEAD_PAD)),
                _const_spec((1, KV_LORA)), _const_spec((KV_LORA, N_HEADS * HEAD_PAD)),
                _const_spec((KV_LORA, v_width)), _const_spec((3, CONV_CH)),
                _const_spec((n_tok, LANES)), _const_spec((n_tok, LANES))]
    q_tabs = _head_tables(w["q_gain"], rope_tabs, n_tok, True)
    k_tabs = _head_tables(w["k_gain"], rope_tabs, n_tok, False)
    args = [x, mod, w["norm1"], w["w_in"], w["q_a_norm"], w["w_q"], w["kv_a_norm"],
            w["w_k"], w_v, w["conv_w"], q_tabs[0], k_tabs[0]]
    if rope_tabs is not None:
        in_specs += [_const_spec((n_tok, LANES))]
        args += [k_tabs[1]]
    out_shape = [jax.ShapeDtypeStruct((n_b, n_tok, N_HEADS * HEAD_PAD), BF16),
                 jax.ShapeDtypeStruct((n_b, n_tok, N_HEADS * HEAD_PAD), BF16),
                 jax.ShapeDtypeStruct((n_b, n_tok, v_width), BF16),
                 jax.ShapeDtypeStruct((n_b, n_tok, CONV_CH), BF16)]
    out_specs = [seq(N_HEADS * HEAD_PAD), seq(N_HEADS * HEAD_PAD), seq(v_width), seq(CONV_CH)]
    if state:
        out_shape += [jax.ShapeDtypeStruct((n_b, n_tok, KV_LORA), F32),
                      jax.ShapeDtypeStruct((n_b, n_tok, LANES), F32)]
        out_specs += [seq(KV_LORA), seq(LANES)]
    return pl.pallas_call(
        functools.partial(_front0_kernel, n_tok=n_tok, rope=rope_tabs is not None, state=state),
        out_shape=out_shape, grid=(n_b,), in_specs=in_specs, out_specs=out_specs,
        scratch_shapes=[pltpu.VMEM((n_tok, CONV_CH), F32), pltpu.VMEM((n_tok, CONV_CH), F32)],
        compiler_params=_params(1), name="front0",
    )(*args)


def _ctx_kv_kernel(ckv_ref, r_ref, wk_ref, wv_ref, ka_ref, k_ref, v_ref):
    rows = pl.ds(0, ckv_ref.shape[0])
    _expand_kv(ckv_ref[...].astype(BF16), r_ref[...], None, wk_ref, wv_ref, ka_ref[...], k_ref, v_ref, rows)


def _ctx_kv(cache_c_kv, cache_k_rope_pad, w, w_v):
    n_b, n_ctx, _ = cache_c_kv.shape
    v_width = w_v.shape[1]
    seq = lambda width: pl.BlockSpec((None, n_ctx, width), lambda b: (b, 0, 0))
    return pl.pallas_call(
        _ctx_kv_kernel,
        out_shape=[jax.ShapeDtypeStruct((n_b, n_ctx, N_HEADS * HEAD_PAD), BF16),
                   jax.ShapeDtypeStruct((n_b, n_ctx, v_width), BF16)],
        grid=(n_b,),
        in_specs=[seq(KV_LORA), seq(LANES), _const_spec((KV_LORA, N_HEADS * HEAD_PAD)),
                  _const_spec((KV_LORA, v_width)), _const_spec((n_ctx, LANES))],
        out_specs=[seq(N_HEADS * HEAD_PAD), seq(v_width)],
        compiler_params=_params(1), name="ctx_kv",
    )(cache_c_kv, cache_k_rope_pad, w["w_k"], w_v, _head_tables(w["k_gain"], None, n_ctx, False)[0])


def _attn_kernel(*refs, with_ctx):
    if with_ctx:
        q_ref, k_ref, v_ref, kc_ref, vc_ref, o_ref = refs
    else:
        q_ref, k_ref, v_ref, o_ref = refs
    ones_lane = v_ref.shape[-1] == N_HEADS * HEAD_PAD
    lane = lax.broadcasted_iota(jnp.int32, (q_ref.shape[0], LANES), 1)
    for pair in range(N_HEADS // 2):
        outs = []
        for hd in (2 * pair, 2 * pair + 1):
            sl = slice(hd * HEAD_PAD, (hd + 1) * HEAD_PAD)
            vsl = sl if ones_lane else slice(pair * LANES, (pair + 1) * LANES)
            qh = q_ref[:, sl]
            s = _dot_nt(qh, k_ref[:, sl])
            m = jnp.max(s, axis=-1, keepdims=True)
            if with_ctx:
                sc = _dot_nt(qh, kc_ref[:, sl])
                m = jnp.maximum(m, jnp.max(sc, axis=-1, keepdims=True))
            p = jnp.exp(s - m)
            o = _dot(p.astype(BF16), v_ref[:, vsl])
            if with_ctx:
                pc = jnp.exp(sc - m)
                o = o + _dot(pc.astype(BF16), vc_ref[:, vsl])
            if ones_lane:
                den = o[:, V_HEAD:V_HEAD + 1]
            else:
                den = jnp.sum(p, axis=-1, keepdims=True)
                if with_ctx:
                    den = den + jnp.sum(pc, axis=-1, keepdims=True)
            outs.append(o / den)
        odd = pltpu.roll(outs[1], V_HEAD, 1) if ones_lane else outs[1]
        o_ref[:, pair * LANES:(pair + 1) * LANES] = jnp.where(lane < V_HEAD, outs[0], odd).astype(BF16)


def _attention(q, k, v, kc=None, vc=None):
    n_b, n_tok, _ = q.shape
    with_ctx = kc is not None
    tq = min(n_tok, ATTN_ROWS)
    qspec = lambda width: pl.BlockSpec((None, tq, width), lambda b, i: (b, i, 0))
    kvspec = lambda a: pl.BlockSpec((None,) + a.shape[1:], lambda b, i: (b, 0, 0))
    in_specs = [qspec(N_HEADS * HEAD_PAD), kvspec(k), kvspec(v)]
    args = [q, k, v]
    if with_ctx:
        in_specs += [kvspec(kc), kvspec(vc)]
        args += [kc, vc]
    return pl.pallas_call(
        functools.partial(_attn_kernel, with_ctx=with_ctx),
        out_shape=jax.ShapeDtypeStruct((n_b, n_tok, N_HEADS * V_HEAD), BF16),
        grid=(n_b, n_tok // tq), in_specs=in_specs, out_specs=qspec(N_HEADS * V_HEAD),
        compiler_params=_params(2), name="attention",
    )(*args)


def _moe_front(x1, mod_ref, n2_ref, wrh_ref, wrl_ref, h2_ref, aff_ref, rows):
    sh2, sc2 = mod_ref[3:4, :], mod_ref[4:5, :]
    h2 = _rms(x1, n2_ref[...]) * (1.0 + sc2) + sh2
    h_hi, h_lo = _split_hi_lo(h2)
    logit = _dot(h_hi, wrh_ref[...]) + _dot(h_lo, wrh_ref[...]) + _dot(h_hi, wrl_ref[...])
    real = lax.broadcasted_iota(jnp.int32, (1, LANES), 1) < N_EXPERTS
    top = jnp.max(jnp.where(real, logit, -jnp.inf), axis=1, keepdims=True)
    e = jnp.where(real, jnp.exp(logit - top), 0.0)
    a = e / jnp.sum(e, axis=1, keepdims=True)
    a_hi = a.astype(BF16).astype(F32)
    a_mid = (a - a_hi).astype(BF16).astype(F32)
    a_lo = a - a_hi - a_mid
    h2_ref[rows, 0:D_MODEL] = h_hi
    h2_ref[rows, D_MODEL:XE_W] = (a_hi + pltpu.roll(a_mid, N_EXPERTS, 1)
                                  + pltpu.roll(a_lo, 2 * N_EXPERTS, 1)).astype(BF16)
    aff_ref[...] = a.T[0:N_EXPERTS, :]


def _post0_kernel(x_ref, attn_ref, conv_ref, wo_ref, mod_ref, n2_ref, wrh_ref, wrl_ref,
                  x1_ref, h2_ref, aff_ref):
    n_attn = N_HEADS * V_HEAD
    mix = _dot(attn_ref[...], wo_ref[0:n_attn, :]) + _dot(conv_ref[...], wo_ref[n_attn:, :])
    for t in range(x_ref.shape[0] // ROWS):
        rows = pl.ds(t * ROWS, ROWS)
        x1 = x_ref[rows, :] + mod_ref[2:3, :] * mix[t * ROWS:(t + 1) * ROWS, :]
        x1_ref[rows, :] = x1
        _moe_front(x1, mod_ref, n2_ref, wrh_ref, wrl_ref, h2_ref, aff_ref.at[t], rows)


def _post0(x, attn, conv, mod, w, n_tok):
    n_all = x.shape[0]
    rows = min(n_tok, MIX_ROWS)
    tiles_per_seq = n_tok // rows
    shared_mod = mod.shape[0] == 1
    tile = lambda width: pl.BlockSpec((rows, width), lambda i: (i, 0))
    const = lambda shape: pl.BlockSpec(shape, lambda i: (0,) * len(shape))
    mod_map = (lambda i: (0, 0, 0)) if shared_mod else (lambda i: (i // tiles_per_seq, 0, 0))
    return pl.pallas_call(
        _post0_kernel,
        out_shape=[jax.ShapeDtypeStruct((n_all, D_MODEL), F32),
                   jax.ShapeDtypeStruct((n_all, XE_W), BF16),
                   jax.ShapeDtypeStruct((n_all // ROWS, N_EXPERTS, ROWS), F32)],
        grid=(n_all // rows,),
        in_specs=[tile(D_MODEL), tile(N_HEADS * V_HEAD), tile(CONV_CH), const((D_MODEL, D_MODEL)),
                  pl.BlockSpec((None, 6, D_MODEL), mod_map), const((1, D_MODEL)),
                  const((D_MODEL, LANES)), const((D_MODEL, LANES))],
        out_specs=[tile(D_MODEL), tile(XE_W),
                   pl.BlockSpec((rows // ROWS, N_EXPERTS, ROWS), lambda i: (i, 0, 0))],
        compiler_params=_params(1), name="post0",
    )(x, attn, conv, w["w_o"], mod, w["norm2"], w["wr_hi"], w["wr_lo"])


def _fnet_kernel(x_ref, mod_ref, n1_ref, cc_ref, sc_ref, dft_ref, flip_ref, wf_ref, n2_ref, wrh_ref, wrl_ref,
                 x1_ref, h2_ref, aff_ref, h_s, r_s, y_s, *, n_tok):
    sh1, sc1, g1 = mod_ref[0:1, :], mod_ref[1:2, :], mod_ref[2:3, :]
    n_chunks = n_tok // ROWS
    half = n_tok // 2
    blk = flip_ref.shape[0]
    cc, sc = cc_ref[...].astype(BF16), sc_ref[...].astype(BF16)
    for c in range(n_chunks):
        rows = pl.ds(c * ROWS, ROWS)
        h_s[rows, :] = (_rms(x_ref[rows, :], n1_ref[...]) * (1.0 + sc1) + sh1).astype(BF16)
    n_flip = half // blk
    for i in range(n_flip):
        src = h_s[pl.ds(half + (n_flip - 1 - i) * blk, blk), :]
        r_s[pl.ds(i * blk, blk), :] = _dot(flip_ref[...], src)
    mirror = pltpu.roll(r_s[...], 1, 0)
    row = lax.broadcasted_iota(jnp.int32, mirror.shape, 0)
    mirror = jnp.where(row == 0, 0.0, mirror)
    low = h_s[0:half, :].astype(F32)
    even, odd = (low + mirror).astype(BF16), (low - mirror).astype(BF16)
    mid = h_s[pl.ds(half, 16), :]
    mids = []
    for g in range(FNET_GROUPS):
        sl = slice(g * FNET_CH, (g + 1) * FNET_CH)
        y_s[0:half, sl] = _dot(even[:, sl], cc).astype(BF16)
        y_s[half:n_tok, sl] = _dot(odd[:, sl], sc).astype(BF16)
        mids.append(_dot(mid[:, sl], cc)[0:1, :])
    mid_row = jnp.concatenate(mids, axis=1)
    scale = 1.0 / float(np.sqrt(n_tok * FNET_CH))
    wide = min(n_tok, MIX_ROWS)
    for c in range(n_chunks):
        rows = pl.ds(c * ROWS, ROWS)
        if (c * ROWS) % wide == 0:
            parity = lax.broadcasted_iota(jnp.int32, (wide, 1), 0) & 1
            f = _dot(dft_ref[pl.ds(c * ROWS, wide), :].astype(BF16), y_s[...])
            f = (f + jnp.where(parity == 0, 1.0, -1.0) * mid_row) * scale
            mix = _dot(f.astype(BF16), wf_ref[...])
        off = (c * ROWS) % wide
        x1 = x_ref[rows, :] + g1 * mix[off:off + ROWS, :]
        x1_ref[rows, :] = x1
        _moe_front(x1, mod_ref, n2_ref, wrh_ref, wrl_ref, h2_ref, aff_ref.at[c], rows)


def _fnet(x, mod, w, dft):
    n_b, n_tok, _ = x.shape
    shared_mod = mod.shape[0] == 1
    seq = lambda width: pl.BlockSpec((None, n_tok, width), lambda b: (b, 0, 0))
    cc, sc, dft_n, flip = dft
    tiles = n_tok // ROWS
    return pl.pallas_call(
        functools.partial(_fnet_kernel, n_tok=n_tok),
        out_shape=[jax.ShapeDtypeStruct((n_b, n_tok, D_MODEL), F32),
                   jax.ShapeDtypeStruct((n_b, n_tok, XE_W), BF16),
                   jax.ShapeDtypeStruct((n_b * tiles, N_EXPERTS, ROWS), F32)],
        grid=(n_b,),
        in_specs=[seq(D_MODEL),
                  pl.BlockSpec((None, 6, D_MODEL), (lambda b: (0, 0, 0)) if shared_mod else (lambda b: (b, 0, 0))),
                  _const_spec((1, D_MODEL)), _const_spec((FNET_CH, FNET_CH)), _const_spec((FNET_CH, FNET_CH)),
                  pl.BlockSpec((n_tok, n_tok), lambda b: (0, 0), pipeline_mode=pl.Buffered(1)),
                  _const_spec(flip.shape), _const_spec((D_MODEL, D_MODEL)), _const_spec((1, D_MODEL)),
                  _const_spec((D_MODEL, LANES)), _const_spec((D_MODEL, LANES))],
        out_specs=[seq(D_MODEL), seq(XE_W), pl.BlockSpec((tiles, N_EXPERTS, ROWS), lambda b: (b, 0, 0))],
        scratch_shapes=[pltpu.VMEM((n_tok, D_MODEL), BF16), pltpu.VMEM((n_tok // 2, D_MODEL), F32),
                        pltpu.VMEM((n_tok, D_MODEL), BF16)],
        compiler_params=_params(1), name="fnet",
    )(x, mod, w["norm1"], cc, sc, dft_n, flip, w["w_f"], w["norm2"], w["wr_hi"], w["wr_lo"])


def _route_kernel(aff_ref, tri_ref, posm_ref, s_ref, *, n_blk, cap):
    def count(pred):
        acc = jnp.zeros((N_EXPERTS, ROWS), F32)
        for b in range(n_blk):
            acc = acc + jnp.where(pred(aff_ref[b]), 1.0, 0.0)
        return jnp.sum(acc, axis=1, keepdims=True)

    def as_f32(bits):
        return pltpu.bitcast(bits, F32)

    def search(_, carry):
        lo, hi = carry
        mid = lo + ((hi - lo) >> 1)
        mid_f = as_f32(mid)
        ok = count(lambda a: a >= mid_f) >= cap
        return jnp.where(ok, mid, lo), jnp.where(ok, hi, mid)

    one_bits = 0x3F800000
    lo0 = jnp.zeros((N_EXPERTS, 1), jnp.int32)
    hi0 = jnp.full((N_EXPERTS, 1), one_bits + 1, jnp.int32)
    lo, hi = lax.fori_loop(0, 31, search, (lo0, hi0))
    lo_f, ub = as_f32(lo), as_f32(hi)
    thr = lo_f
    pending = jnp.ones((N_EXPERTS, 1), F32)
    for _ in range(3):
        cur = jnp.full((N_EXPERTS, ROWS), -1.0, F32)
        for b in range(n_blk):
            a = aff_ref[b]
            cur = jnp.maximum(cur, jnp.where((a >= lo_f) & (a < ub), a, -1.0))
        cur = jnp.max(cur, axis=1, keepdims=True)
        take = (count(lambda a: a >= cur) >= cap) & (pending > 0.0)
        thr = jnp.where(take, cur, thr)
        pending = jnp.where(take, 0.0, pending)
        ub = cur
    need = cap - count(lambda a: a > thr)

    lane = lax.broadcasted_iota(jnp.int32, (N_EXPERTS, LANES), 1)
    carry_tie = jnp.zeros((N_EXPERTS, 1), F32)
    carry_pos = jnp.zeros((N_EXPERTS, 1), F32)
    s_acc = jnp.zeros((N_EXPERTS, LANES), jnp.int32)
    for b in range(n_blk):
        v = aff_ref[b]
        eq = jnp.where(v == thr, 1.0, 0.0)
        tie_rank = _dot(eq.astype(BF16), tri_ref[...]) + carry_tie
        sel = jnp.where((v > thr) | ((v == thr) & (tie_rank < need)), 1.0, 0.0)
        pos = _dot(sel.astype(BF16), tri_ref[...]) + carry_pos
        posm_ref[b] = jnp.where(sel > 0.0, pos, -1.0)
        s_acc = jnp.where(lane == b, carry_pos.astype(jnp.int32), s_acc)
        carry_tie = carry_tie + jnp.sum(eq, axis=1, keepdims=True)
        n_sel = jnp.sum(sel, axis=1, keepdims=True)
        carry_pos = carry_pos + jnp.floor((n_sel + (ALIGN - 1)) * (1.0 / ALIGN)) * ALIGN
    s_ref[...] = jnp.where(lane == n_blk, carry_pos.astype(jnp.int32), s_acc)


def _route(aff, tri, cap):
    n_blk = aff.shape[0]
    assert n_blk < LANES
    full = lambda shape: pl.BlockSpec(shape, lambda i: (0,) * len(shape))
    posm, s_tab = pl.pallas_call(
        functools.partial(_route_kernel, n_blk=n_blk, cap=cap),
        out_shape=[jax.ShapeDtypeStruct((n_blk, N_EXPERTS, ROWS), F32),
                   jax.ShapeDtypeStruct((N_EXPERTS, LANES), jnp.int32)],
        grid=(1,),
        in_specs=[full((n_blk, N_EXPERTS, ROWS)), full((ROWS, ROWS))],
        out_specs=[full((n_blk, N_EXPERTS, ROWS)), full((N_EXPERTS, LANES))],
        compiler_params=_params(1), name="route",
    )(aff, tri)
    return posm, s_tab[:, :n_blk + 1]


def _ceil_div_pow2(x, d):
    return lax.shift_right_logical(x + (d - 1), d.bit_length() - 1)


def _block_units(s_ref, b):
    base = [s_ref[e, b] for e in range(N_EXPERTS)]
    units = [_ceil_div_pow2(s_ref[e, b + 1] - base[e], WIN) for e in range(N_EXPERTS)]
    first_unit, total = [], 0
    for e in range(N_EXPERTS):
        first_unit.append(total)
        total = total + units[e]
    return base, units, first_unit, total


def _for_units(units, fn):
    for e in range(N_EXPERTS):
        def body(u, _, e=e):
            fn(e, u)
            return 0
        lax.fori_loop(0, units[e], body, 0)


def _window_onehot(posm_ref, e, first, lo):
    slot = lax.broadcasted_iota(jnp.int32, (WIN, ROWS), 0) + first
    p = posm_ref[e:e + 1, :].astype(jnp.int32)
    return jnp.where((p == slot) & (p >= lo), 1.0, 0.0)


def _gate_lanes(e):
    lane = lax.broadcasted_iota(jnp.int32, (1, LANES), 1)
    return ((lane & (N_EXPERTS - 1)) == e) & (lane < 3 * N_EXPERTS)


def _dispatch_kernel(s_ref, h_ref, posm_ref, xe_ref, sel_s, x_s, sem, *, n_blk, cap_pad, common_rows):
    for t in range(STEP_BLOCKS):
        _dispatch_block(pl.program_id(0) * STEP_BLOCKS + t, t, s_ref, h_ref.at[pl.ds(t * ROWS, ROWS), :],
                        posm_ref.at[t], xe_ref, sel_s, x_s, sem, n_blk, cap_pad, common_rows)


def _dispatch_block(b, buf, s_ref, h_ref, posm_ref, xe_ref, sel_s, x_s, sem, n_blk, cap_pad, common_rows):
    base, units, first_unit, n_units = _block_units(s_ref, b)

    @pl.when(b == 0)
    def _():
        sel_s[...] = jnp.zeros_like(sel_s)

    def unit_rows(first_unit_e, u):
        return pl.ds(pl.multiple_of((first_unit_e + u) * WIN, WIN), WIN)

    def pick(e, u):
        first = base[e] + u * WIN
        sel_s[unit_rows(first_unit[e], u), :] = _window_onehot(posm_ref, e, first, first).astype(BF16)

    _for_units(units, pick)

    def gather(rows):
        x_s[buf, rows, :] = _dot(sel_s[rows, :], h_ref[...]).astype(BF16)

    def gather_chunk(c, _):
        gather(pl.ds(pl.multiple_of(c * ROWS, ROWS), ROWS))
        return 0

    gather(pl.ds(0, common_rows))
    lax.fori_loop(common_rows // ROWS, _ceil_div_pow2(n_units * WIN, ROWS), gather_chunk, 0)

    def unit_copy(slot, blk_base_e, blk_first_e, e, u):
        dst = pl.ds(pl.multiple_of(blk_base_e + u * WIN, ALIGN), WIN)
        return pltpu.make_async_copy(x_s.at[slot, unit_rows(blk_first_e, u), :], xe_ref.at[e, dst, :],
                                     sem.at[slot, e])

    def for_expert_units(n, fn):
        lax.fori_loop(0, n, lambda u, _: (fn(u), 0)[1], 0)

    prev_base, prev_units, prev_first, _ = _block_units(s_ref, jnp.maximum(b - 1, 0))
    for e in range(N_EXPERTS):
        for_expert_units(jnp.where(b > 0, prev_units[e], 0),
                         lambda u, e=e: unit_copy(1 - buf, prev_base[e], prev_first[e], e, u).wait())
        for_expert_units(units[e], lambda u, e=e: unit_copy(buf, base[e], first_unit[e], e, u).start())

    @pl.when(b == n_blk - 1)
    def _():
        for e in range(N_EXPERTS):
            for_expert_units(units[e], lambda u, e=e: unit_copy(buf, base[e], first_unit[e], e, u).wait())
        x_s[buf, 0:WIN, :] = jnp.zeros((WIN, XE_W), BF16)

        def tail(fn):
            for e in range(N_EXPERTS):
                total = s_ref[e, n_blk]
                n_win = (cap_pad - total) // WIN

                def wide(c, _):
                    row = pl.multiple_of(total + c * WIN, ALIGN)
                    fn(pltpu.make_async_copy(x_s.at[buf, pl.ds(0, WIN), :],
                                             xe_ref.at[e, pl.ds(row, WIN), :], sem.at[buf, e]))
                    return 0

                def narrow(c, _):
                    row = pl.multiple_of(total + n_win * WIN + c * ALIGN, ALIGN)
                    fn(pltpu.make_async_copy(x_s.at[buf, pl.ds(0, ALIGN), :],
                                             xe_ref.at[e, pl.ds(row, ALIGN), :], sem.at[buf, e]))
                    return 0

                lax.fori_loop(0, n_win, wide, 0)
                lax.fori_loop(0, (cap_pad - total - n_win * WIN) // ALIGN, narrow, 0)

        tail(lambda c: c.start())
        tail(lambda c: c.wait())


def _ffn_kernel(*refs, n_blks):
    n_sets = len(n_blks)
    s_refs, x_refs = refs[:n_sets], refs[n_sets:2 * n_sets]
    wg_ref, wu_ref, wd_ref = refs[2 * n_sets:2 * n_sets + 3]
    y_refs = refs[2 * n_sets + 3:3 * n_sets + 3]
    wg_s, wu_s, wd_s = refs[3 * n_sets + 3:]
    e = pl.program_id(0)
    wg_s[...] = wg_ref[...].astype(BF16)
    wu_s[...] = wu_ref[...].astype(BF16)
    wd_s[...] = wd_ref[...].astype(BF16)
    mine = _gate_lanes(e)

    half = ROWS // 2
    for s_ref, x_ref, y_ref, n_blk in zip(s_refs, x_refs, y_refs, n_blks):
        n_half = _ceil_div_pow2(s_ref[e, n_blk], half)

        def tile(rows, x_ref=x_ref, y_ref=y_ref):
            x = x_ref[rows, 0:D_MODEL]
            pieces = x_ref[rows, D_MODEL:XE_W].astype(F32)
            gate = jnp.sum(jnp.where(mine, pieces, 0.0), axis=1, keepdims=True)
            a = _dot(x, wg_s[...])
            u = _dot(x, wu_s[...])
            hid = a * (1.0 / (1.0 + jnp.exp(-a))) * u * gate
            y_ref[rows, :] = _dot(hid.astype(BF16), wd_s[...]).astype(BF16)

        def full(j, _, tile=tile):
            tile(pl.ds(pl.multiple_of(j * WIDE_ROWS, WIDE_ROWS), WIDE_ROWS))
            return 0

        def blank(j, _, y_ref=y_ref):
            y_ref[pl.ds(pl.multiple_of(j * half, half), half), :] = jnp.zeros((half, D_MODEL), BF16)
            return 0

        lax.fori_loop(0, lax.shift_right_logical(n_half, 2), full, 0)

        @pl.when((n_half & 2) != 0)
        def _(tile=tile, n_half=n_half):
            tile(pl.ds(pl.multiple_of((n_half & -4) * half, WIDE_ROWS), ROWS))

        @pl.when((n_half & 1) != 0)
        def _(tile=tile, n_half=n_half):
            tile(pl.ds(pl.multiple_of((n_half - 1) * half, half), half))

        lax.fori_loop(n_half, y_ref.shape[0] // half, blank, 0)


def _combine_kernel(s_ref, y_ref, posm_ref, x1_ref, mod_ref, o_ref, yw_s, sel_s, acc_s, sem,
                    *, n_blk, cap_pad, common_rows):
    for t in range(STEP_BLOCKS):
        rows = pl.ds(t * ROWS, ROWS)
        _combine_block(pl.program_id(0) * STEP_BLOCKS + t, t, s_ref, y_ref, posm_ref.at[t], x1_ref.at[rows, :],
                       mod_ref, o_ref.at[rows, :], yw_s, sel_s, acc_s, sem, n_blk, cap_pad, common_rows)


def _combine_block(b, buf, s_ref, y_ref, posm_ref, x1_ref, mod_ref, o_ref, yw_s, sel_s, acc_s, sem, n_blk, cap_pad,
                   common_rows):
    base, units, first_unit, n_units = _block_units(s_ref, b)

    def unit_rows(first_unit_e, u):
        return pl.ds(pl.multiple_of((first_unit_e + u) * WIN, WIN), WIN)

    def bounds(base_e, u):
        lo = base_e + u * WIN
        return lo, jnp.minimum(lo, cap_pad - WIN)

    def copies(slot, blk, fn):
        blk_base, blk_units, blk_first, _ = _block_units(s_ref, blk)

        def one(e, u):
            src = pl.ds(pl.multiple_of(bounds(blk_base[e], u)[1], ALIGN), WIN)
            fn(pltpu.make_async_copy(y_ref.at[e, src, :], yw_s.at[slot, unit_rows(blk_first[e], u), :],
                                     sem.at[slot]))

        _for_units(blk_units, one)

    @pl.when(b == 0)
    def _():
        yw_s[...] = jnp.zeros_like(yw_s)
        copies(0, 0, lambda c: c.start())

    @pl.when(b + 1 < n_blk)
    def _():
        copies(1 - buf, b + 1, lambda c: c.start())

    def pick(e, u):
        lo, first = bounds(base[e], u)
        sel_s[unit_rows(first_unit[e], u), :] = _window_onehot(posm_ref, e, first, lo)

    _for_units(units, pick)
    n_chunks = jnp.maximum(_ceil_div_pow2(n_units * WIN, ROWS), common_rows // ROWS)

    def blank(u, _):
        sel_s[unit_rows(u, 0), :] = jnp.zeros((WIN, ROWS), F32)
        return 0

    lax.fori_loop(n_units, n_chunks * (ROWS // WIN), blank, 0)
    copies(buf, b, lambda c: c.wait())

    def weighted(rows):
        return _dot(sel_s[rows, :].T.astype(BF16), yw_s[buf, rows, :])

    def add_chunk(c, _):
        acc_s[...] += weighted(pl.ds(pl.multiple_of(c * ROWS, ROWS), ROWS))
        return 0

    acc_s[...] = weighted(pl.ds(0, common_rows))
    lax.fori_loop(common_rows // ROWS, n_chunks, add_chunk, 0)
    o_ref[...] = x1_ref[...] + mod_ref[5:6, :] * acc_s[...]


def _padded_capacity(cap, n_blk):
    return -(-(cap + (ALIGN - 1) * n_blk + WIN) // ROWS) * ROWS


def _step_spec(*shape):
    return pl.BlockSpec((STEP_BLOCKS,) + shape, lambda i, s: (i,) + (0,) * len(shape))


def _common_rows(n_tok):
    return COMMON_ROWS + (ROWS if n_tok > ROWS else 0)


def _dispatch(h2, posm, s_tab, cap, n_tok):
    n_blk = h2.shape[0] // ROWS
    cap_pad = _padded_capacity(cap, n_blk)
    return pl.pallas_call(
        functools.partial(_dispatch_kernel, n_blk=n_blk, cap_pad=cap_pad, common_rows=_common_rows(n_tok)),
        out_shape=jax.ShapeDtypeStruct((N_EXPERTS, cap_pad, XE_W), BF16),
        grid_spec=pltpu.PrefetchScalarGridSpec(
            num_scalar_prefetch=1, grid=(n_blk // STEP_BLOCKS,),
            in_specs=[pl.BlockSpec((STEP_BLOCKS * ROWS, XE_W), lambda i, s: (i, 0)), _step_spec(N_EXPERTS, ROWS)],
            out_specs=pl.BlockSpec(memory_space=pl.ANY),
            scratch_shapes=[pltpu.VMEM((MAX_UNITS * WIN, ROWS), BF16), pltpu.VMEM((2, MAX_UNITS * WIN, XE_W), BF16),
                            pltpu.SemaphoreType.DMA((2, N_EXPERTS))]),
        compiler_params=_params(1), name="dispatch",
    )(s_tab, h2, posm)


def _ffn(xes, s_tabs, wg, wu, wd):
    n_sets = len(xes)
    per_expert = lambda rows, cols: pl.BlockSpec((None, rows, cols), lambda e, *s: (e, 0, 0))
    return pl.pallas_call(
        functools.partial(_ffn_kernel, n_blks=tuple(s.shape[1] - 1 for s in s_tabs)),
        out_shape=[jax.ShapeDtypeStruct(xe.shape[:2] + (D_MODEL,), BF16) for xe in xes],
        grid_spec=pltpu.PrefetchScalarGridSpec(
            num_scalar_prefetch=n_sets, grid=(N_EXPERTS,),
            in_specs=[per_expert(xe.shape[1], XE_W) for xe in xes]
            + [per_expert(D_MODEL, EXPERT_FF), per_expert(D_MODEL, EXPERT_FF), per_expert(EXPERT_FF, D_MODEL)],
            out_specs=[per_expert(xe.shape[1], D_MODEL) for xe in xes],
            scratch_shapes=[pltpu.VMEM((D_MODEL, EXPERT_FF), BF16), pltpu.VMEM((D_MODEL, EXPERT_FF), BF16),
                            pltpu.VMEM((EXPERT_FF, D_MODEL), BF16)]),
        compiler_params=_params(1), name="ffn",
    )(*s_tabs, *xes, wg, wu, wd)


def _combine(y, posm, s_tab, x1, mod, n_tok):
    n_all = x1.shape[0]
    n_blk = n_all // ROWS
    shared_mod = mod.shape[0] == 1
    assert shared_mod or n_tok % (STEP_BLOCKS * ROWS) == 0
    steps_per_seq = max(n_tok // (STEP_BLOCKS * ROWS), 1)
    mod_map = (lambda i, s: (0, 0, 0)) if shared_mod else (lambda i, s: (i // steps_per_seq, 0, 0))
    rows = pl.BlockSpec((STEP_BLOCKS * ROWS, D_MODEL), lambda i, s: (i, 0))
    return pl.pallas_call(
        functools.partial(_combine_kernel, n_blk=n_blk, cap_pad=y.shape[1], common_rows=_common_rows(n_tok)),
        out_shape=jax.ShapeDtypeStruct((n_all, D_MODEL), F32),
        grid_spec=pltpu.PrefetchScalarGridSpec(
            num_scalar_prefetch=1, grid=(n_blk // STEP_BLOCKS,),
            in_specs=[pl.BlockSpec(memory_space=pl.ANY), _step_spec(N_EXPERTS, ROWS), rows,
                      pl.BlockSpec((None, 6, D_MODEL), mod_map)],
            out_specs=rows,
            scratch_shapes=[pltpu.VMEM((2, MAX_UNITS * WIN, D_MODEL), BF16), pltpu.VMEM((MAX_UNITS * WIN, ROWS), F32),
                            pltpu.VMEM((ROWS, D_MODEL), F32), pltpu.SemaphoreType.DMA((2,))]),
        compiler_params=_params(1), name="combine",
    )(s_tab, y, posm, x1, mod)


def _rope_tables(n_tok):
    rows = n_tok // GRID_W
    row = np.repeat(np.arange(rows, dtype=np.float64), GRID_W)
    col = np.tile(np.arange(GRID_W, dtype=np.float64), rows)
    axis_dim = QK_ROPE // 2
    inv_freq = ROPE_THETA ** (-np.arange(0, axis_dim, 2, dtype=np.float64) / axis_dim)
    ang = np.concatenate([row[:, None] * inv_freq, col[:, None] * inv_freq], axis=-1)
    cos = np.ones((n_tok, LANES))
    sin = np.zeros((n_tok, LANES))
    cos[:, ROPE_LANE0:ROPE_LANE0 + QK_ROPE] = np.repeat(np.cos(ang), 2, axis=1)
    sgn = np.tile(np.array([-1.0, 1.0]), QK_ROPE // 2)
    sin[:, ROPE_LANE0:ROPE_LANE0 + QK_ROPE] = np.repeat(np.sin(ang), 2, axis=1) * sgn
    return jnp.asarray(cos, F32), jnp.asarray(sin, F32)


def _dft_tables(n_tok):
    def cs(n):
        k = np.arange(n)
        ang = 2.0 * np.pi * ((k[:, None] * k[None, :]) % n) / n
        return np.cos(ang), np.sin(ang)
    cc, sc = cs(FNET_CH)
    cn, sn = cs(n_tok)
    half = n_tok // 2
    folded = np.concatenate([cn[:, :half], -sn[:, :half]], axis=1)
    flip = np.fliplr(np.eye(min(ROWS, half)))
    return tuple(jnp.asarray(t, F32) for t in (cc, sc, folded)) + (jnp.asarray(flip, BF16),)


def _pad_heads(w, width):
    lead = w.shape[:-1]
    w = w.reshape(lead + (N_HEADS, width))
    w = jnp.pad(w, [(0, 0)] * len(lead) + [(0, 0), (0, HEAD_PAD - width)])
    return w.reshape(lead + (N_HEADS * HEAD_PAD,))


def _pair_swap_lanes(n_groups):
    perm = np.arange(n_groups * LANES).reshape(n_groups, LANES)
    rot = perm[:, ROPE_LANE0:ROPE_LANE0 + QK_ROPE].reshape(n_groups, QK_ROPE // 2, 2)[:, :, ::-1]
    perm[:, ROPE_LANE0:ROPE_LANE0 + QK_ROPE] = rot.reshape(n_groups, QK_ROPE)
    return perm.reshape(-1)


def _spare_from_rotary(t, src=None):
    src = t if src is None else src
    lead = t.shape[:-1]
    t, src = t.reshape(lead + (-1, LANES)), src.reshape(lead + (-1, LANES))
    t = jnp.concatenate([t[..., :QK_DIM], src[..., ROPE_LANE0:ROPE_LANE0 + QK_ROPE]], axis=-1)
    return t.reshape(lead + (-1,))


def _head_tables(gain, rope_tabs, n_tok, query):
    if rope_tabs is None:
        tab = jnp.broadcast_to(gain, (n_tok, LANES))
        return (tab if query else _spare_from_rotary(tab),)
    cos, sin = rope_tabs
    a, b = gain * cos, gain[:, _pair_swap_lanes(1)] * sin
    return (_spare_from_rotary(a, b),) if query else (_spare_from_rotary(a), _spare_from_rotary(b))


def _layer0_weights(norm1, norm2, w_in, q_a_norm, w_q_up, q_norm, kv_a_norm, w_kv_up, k_norm, conv_w, w_o):
    c0 = Q_LORA + KV_LORA
    rope_cols = jnp.pad(w_in[:, c0:c0 + QK_ROPE], ((0, 0), (ROPE_LANE0, LANES - ROPE_LANE0 - QK_ROPE)))
    w_in_pad = jnp.concatenate([w_in[:, :c0], _spare_from_rotary(rope_cols),
                                _spare_from_rotary(rope_cols[:, _pair_swap_lanes(1)]),
                                w_in[:, c0 + QK_ROPE:]], axis=1)
    kv = w_kv_up.reshape(KV_LORA, N_HEADS, QK_NOPE + V_HEAD)
    w_q = _pad_heads(w_q_up, QK_DIM)
    w_q = _spare_from_rotary(w_q, w_q[:, _pair_swap_lanes(N_HEADS)])
    head_gain = lambda g: jnp.pad(g, (0, HEAD_PAD - QK_DIM)).reshape(1, -1)
    return dict(
        norm1=norm1.reshape(1, -1), norm2=norm2.reshape(1, -1), w_in=w_in_pad.astype(BF16),
        q_a_norm=q_a_norm.reshape(1, -1), w_q=w_q.astype(BF16),
        q_gain=head_gain(q_norm) * (QK_DIM ** -0.5), kv_a_norm=kv_a_norm.reshape(1, -1),
        w_k=_pad_heads(kv[:, :, :QK_NOPE].reshape(KV_LORA, -1), QK_NOPE).astype(BF16),
        w_v=kv[:, :, QK_NOPE:].reshape(KV_LORA, -1).astype(BF16),
        w_v_wide=_pad_heads(kv[:, :, QK_NOPE:].reshape(KV_LORA, -1), V_HEAD).astype(BF16), k_gain=head_gain(k_norm),
        conv_w=conv_w, w_o=w_o.astype(BF16))


def _router_weights(w_router):
    hi, lo = _split_hi_lo(jnp.pad(w_router, ((0, 0), (0, LANES - N_EXPERTS))))
    return dict(wr_hi=hi, wr_lo=lo)


def _moe(sets, tri, w):
    routed = []
    for x1, h2, aff, mod, n_tok in sets:
        cap = CAPACITY_FACTOR * x1.shape[0] // N_EXPERTS
        posm, s_tab = _route(aff, tri, cap)
        routed.append((posm, s_tab, _dispatch(h2, posm, s_tab, cap, n_tok)))
    ys = _ffn([r[2] for r in routed], [r[1] for r in routed], w["wg"], w["wu"], w["wd"])
    return [_combine(y, posm, s_tab, x1, mod, n_tok)
            for y, (posm, s_tab, _), (x1, _, _, mod, n_tok) in zip(ys, routed, sets)]


def _mixer0(x, mod, l0, ctx, rope_tabs, state):
    n_b, n_tok, _ = x.shape
    n_keys = n_tok + (0 if ctx is None else ctx[0].shape[1])
    w_v = l0["w_v_wide"] if n_keys > WIDE_ROWS else l0["w_v"]
    q, k, v, conv, *new_state = _front0(x, mod, l0, rope_tabs, w_v, state)
    if state:
        new_state = (new_state[0], new_state[1][:, :, ROPE_LANE0:ROPE_LANE0 + QK_ROPE])
    if ctx is not None:
        kc, vc = _ctx_kv(ctx[0], ctx[1], l0, w_v)
        attn = _attention(q, k, v, kc, vc)
    else:
        attn = _attention(q, k, v)
    flat = lambda a: a.reshape(n_b * n_tok, a.shape[-1])
    return _post0(flat(x), flat(attn), flat(conv), mod, l0, n_tok), new_state


def kernel(x_prompt, x_sample, c, cache_c_kv_l0, cache_k_rope_l0, c_ctx, norm1_l0, norm2_l0, w_mod_l0, b_mod_l0, w_in_l0, q_a_norm_l0, w_q_up_l0, q_norm_l0, kv_a_norm_l0, w_kv_up_l0, k_norm_l0, conv_w_l0, w_o_l0, w_router_l0, w_gate_l0, w_up_l0, w_down_l0, norm1_l1, norm2_l1, w_mod_l1, b_mod_l1, w_f_l1, w_router_l1, w_gate_l1, w_up_l1, w_down_l1):
    n_dec = c.shape[0]
    cond = jnp.concatenate([c_ctx[None, :], c, jnp.zeros((16 - 1 - n_dec, D_MODEL), F32)], axis=0)
    m0 = _modulation(cond, w_mod_l0, b_mod_l0)
    m1 = _modulation(cond, w_mod_l1, b_mod_l1)
    mods_prompt = (m0[0:1], m1[0:1])
    mods_sample = (m0[1:1 + n_dec], m1[1:1 + n_dec])

    l0 = _layer0_weights(norm1_l0, norm2_l0, w_in_l0, q_a_norm_l0, w_q_up_l0, q_norm_l0, kv_a_norm_l0,
                         w_kv_up_l0, k_norm_l0, conv_w_l0, w_o_l0)
    l0.update(_router_weights(w_router_l0))
    l0.update(wg=w_gate_l0, wu=w_up_l0, wd=w_down_l0)
    l1 = dict(norm1=norm1_l1.reshape(1, -1), norm2=norm2_l1.reshape(1, -1), w_f=w_f_l1.astype(BF16))
    l1.update(_router_weights(w_router_l1))
    l1.update(wg=w_gate_l1, wu=w_up_l1, wd=w_down_l1)

    tri = jnp.asarray(np.triu(np.ones((ROWS, ROWS)), 1), BF16)
    k_rope_pad = _spare_from_rotary(
        jnp.pad(cache_k_rope_l0, ((0, 0), (0, 0), (ROPE_LANE0, LANES - ROPE_LANE0 - QK_ROPE))))
    xs = (x_prompt, x_sample)
    mods = (mods_prompt, mods_sample)
    n_toks = tuple(x.shape[1] for x in xs)

    (front_p, (new_c_kv, new_k_rope)) = _mixer0(x_prompt, mods_prompt[0], l0, None, None, True)
    (front_s, _) = _mixer0(x_sample, mods_sample[0], l0, (cache_c_kv_l0, k_rope_pad), _rope_tables(n_toks[1]),
                           False)
    ys = _moe([tuple(front) + (mod[0], n_tok) for front, mod, n_tok in zip((front_p, front_s), mods, n_toks)],
              tri, l0)

    sets = []
    for y, x, mod, n_tok in zip(ys, xs, mods, n_toks):
        x1, h2, aff = _fnet(y.reshape(x.shape), mod[1], l1, _dft_tables(n_tok))
        sets.append((x1.reshape(-1, D_MODEL), h2.reshape(-1, XE_W), aff, mod[1], n_tok))
    y_prompt, y_sample = (y.reshape(x.shape) for y, x in zip(_moe(sets, tri, l1), xs))
    return (y_prompt, y_sample, new_c_kv, new_k_rope)
```

```python
import functools

import jax
import jax.numpy as jnp
import numpy as np
from jax import lax
from jax.experimental import pallas as pl
from jax.experimental.pallas import tpu as pltpu

D_MODEL = 1024
GRID_W = 64
N_HEADS = 8
QK_NOPE = 64
QK_ROPE = 32
QK_DIM = QK_NOPE + QK_ROPE
V_HEAD = 64
Q_LORA = 384
KV_LORA = 256
CONV_CH = 512
FNET_GROUPS = 4
FNET_CH = D_MODEL // FNET_GROUPS
N_EXPERTS = 16
EXPERT_FF = 512
CAPACITY_FACTOR = 2
ROPE_THETA = 10000.0
EPS = 1e-6

LANES = 128
HEAD_PAD = LANES
ROWS = 256
WIDE_ROWS = 512
MIX_ROWS = 1024
ATTN_ROWS = 1024
ALIGN = 8
WIN = 64
MAX_UNITS = N_EXPERTS * (ROWS // WIN)
COMMON_ROWS = N_EXPERTS * WIN
STEP_BLOCKS = 2
XE_W = D_MODEL + LANES
IN0_PAD = Q_LORA + KV_LORA + 2 * LANES + 3 * CONV_CH
ROPE_LANE0 = QK_NOPE
VMEM_LIMIT = 56 * 1024 * 1024

F32 = jnp.float32
BF16 = jnp.bfloat16


def _dot(a, b):
    return jnp.dot(a, b, preferred_element_type=F32)


def _dot_nt(a, b):
    return lax.dot_general(a, b, (((1,), (1,)), ((), ())), preferred_element_type=F32)


def _split_hi_lo(x):
    hi = x.astype(BF16)
    lo = (x - hi.astype(F32)).astype(BF16)
    return hi, lo


def _params(n_axes):
    return pltpu.CompilerParams(dimension_semantics=("arbitrary",) * n_axes,
                                vmem_limit_bytes=VMEM_LIMIT)


def _rms(x, gain):
    return x * lax.rsqrt(jnp.mean(x * x, axis=-1, keepdims=True) + EPS) * gain


def _mod_kernel(cond_ref, w_ref, b_ref, o_ref):
    c = cond_ref[...]
    s = c * (1.0 / (1.0 + jnp.exp(-c)))
    s_hi, s_lo = _split_hi_lo(s)
    w = w_ref[...].astype(BF16)
    o_ref[...] = _dot(s_hi, w) + _dot(s_lo, w) + b_ref[...]


def _modulation(cond, w_mod, b_mod):
    n_rows = cond.shape[0]
    tn = 1536
    out = pl.pallas_call(
        _mod_kernel,
        out_shape=jax.ShapeDtypeStruct((n_rows, 6 * D_MODEL), F32),
        grid=(6 * D_MODEL // tn,),
        in_specs=[pl.BlockSpec((n_rows, D_MODEL), lambda i: (0, 0)),
                  pl.BlockSpec((D_MODEL, tn), lambda i: (0, i)),
                  pl.BlockSpec((1, tn), lambda i: (0, i))],
        out_specs=pl.BlockSpec((n_rows, tn), lambda i: (0, i)),
        compiler_params=_params(1),
        name="modulation",
    )(cond, w_mod, b_mod.reshape(1, -1))
    return out.reshape(n_rows, 6, D_MODEL)


def _head_norm_rope(xh, gain_cos, swapped_sin):
    lane = lax.broadcasted_iota(jnp.int32, (1, HEAD_PAD), 1)
    ss = jnp.sum(jnp.where(lane < QK_DIM, xh * xh, 0.0), axis=-1, keepdims=True) * (1.0 / QK_DIM)
    y = xh * gain_cos
    if swapped_sin is not None:
        y = y + swapped_sin
    return y * lax.rsqrt(ss + EPS)


def _expand_kv(ckv_bf, r, swapped_sin, wk_ref, wv_ref, gain_cos, k_ref, v_ref, rows):
    kf = _dot(ckv_bf, wk_ref[...])
    vf = _dot(ckv_bf, wv_ref[...])
    if v_ref.shape[-1] == N_HEADS * HEAD_PAD:
        lane = lax.broadcasted_iota(jnp.int32, (1, N_HEADS * HEAD_PAD), 1)
        vf = vf + jnp.where((lane & (HEAD_PAD - 1)) == V_HEAD, 1.0, 0.0)
    v_ref[rows, :] = vf.astype(BF16)
    for h in range(N_HEADS):
        sl = slice(h * HEAD_PAD, (h + 1) * HEAD_PAD)
        k_ref[rows, sl] = _head_norm_rope(kf[:, sl] + r, gain_cos, swapped_sin).astype(BF16)


def _front0_kernel(*refs, n_tok, rope, state):
    (x_ref, mod_ref, n1_ref, win_ref, qan_ref, wq_ref, kvan_ref, wk_ref, wv_ref, cw_ref,
     qa_ref, ka_ref) = refs[:12]
    pos = 12
    if rope:
        kb_ref = refs[pos]
        pos += 1
    q_ref, k_ref, v_ref, conv_ref = refs[pos:pos + 4]
    if state:
        ckv_ref, kr_ref = refs[pos + 4:pos + 6]
    cu_s, gb_s = refs[-2:]
    sh1, sc1 = mod_ref[0:1, :], mod_ref[1:2, :]
    c_rope = Q_LORA + KV_LORA
    wide = min(n_tok, MIX_ROWS)
    for c in range(n_tok // ROWS):
        rows = pl.ds(c * ROWS, ROWS)
        if (c * ROWS) % wide == 0:
            wide_rows = pl.ds(c * ROWS, wide)
            h = _rms(x_ref[wide_rows, :], n1_ref[...]) * (1.0 + sc1) + sh1
            proj_wide = _dot(h.astype(BF16), win_ref[...])
        off = (c * ROWS) % wide
        proj = proj_wide[off:off + ROWS, :]
        cq = _rms(proj[:, :Q_LORA], qan_ref[...]).astype(BF16)
        qf = _dot(cq, wq_ref[...])
        for hd in range(N_HEADS):
            sl = slice(hd * HEAD_PAD, (hd + 1) * HEAD_PAD)
            q_ref[rows, sl] = _head_norm_rope(qf[:, sl], qa_ref[rows, :], None).astype(BF16)
        ckv = _rms(proj[:, Q_LORA:c_rope], kvan_ref[...])
        r = proj[:, c_rope:c_rope + LANES]
        if state:
            ckv_ref[rows, :] = ckv
            kr_ref[rows, :] = r
        k_sin = proj[:, c_rope + LANES:c_rope + 2 * LANES] * kb_ref[rows, :] if rope else None
        _expand_kv(ckv.astype(BF16), r, k_sin, wk_ref, wv_ref, ka_ref[rows, :], k_ref, v_ref, rows)
        c0 = c_rope + 2 * LANES
        gb_s[rows, :] = proj[:, c0:c0 + CONV_CH]
        cu_s[rows, :] = proj[:, c0 + CONV_CH:c0 + 2 * CONV_CH] * proj[:, c0 + 2 * CONV_CH:c0 + 3 * CONV_CH]
    cu = cu_s[...]
    row = lax.broadcasted_iota(jnp.int32, cu.shape, 0)
    prev = jnp.where(row == 0, 0.0, pltpu.roll(cu, 1, 0))
    nxt = jnp.where(row == n_tok - 1, 0.0, pltpu.roll(cu, n_tok - 1, 0))
    conv = gb_s[...] * (cw_ref[0:1, :] * prev + cw_ref[1:2, :] * cu + cw_ref[2:3, :] * nxt)
    conv_ref[...] = conv.astype(BF16)


def _const_spec(shape):
    return pl.BlockSpec(shape, lambda b: (0,) * len(shape))


def _front0(x, mod, w, rope_tabs, w_v, state):
    n_b, n_tok, _ = x.shape
    v_width = w_v.shape[1]
    shared_mod = mod.shape[0] == 1
    seq = lambda width: pl.BlockSpec((None, n_tok, width), lambda b: (b, 0, 0))
    in_specs = [seq(D_MODEL),
                pl.BlockSpec((None, 6, D_MODEL), (lambda b: (0, 0, 0)) if shared_mod else (lambda b: (b, 0, 0))),
                _const_spec((1, D_MODEL)), _const_spec((D_MODEL, IN0_PAD)), _const_spec((1, Q_LORA)),
                _const_spec((Q_LORA, N_HEADS * HEAD_PAD)),
                _const_spec((1, KV_LORA)), _const_spec((KV_LORA, N_HEADS * HEAD_PAD)),
                _const_spec((KV_LORA, v_width)), _const_spec((3, CONV_CH)),
                _const_spec((n_tok, LANES)), _const_spec((n_tok, LANES))]
    q_tabs = _head_tables(w["q_gain"], rope_tabs, n_tok, True)
    k_tabs = _head_tables(w["k_gain"], rope_tabs, n_tok, False)
    args = [x, mod, w["norm1"], w["w_in"], w["q_a_norm"], w["w_q"], w["kv_a_norm"],
            w["w_k"], w_v, w["conv_w"], q_tabs[0], k_tabs[0]]
    if rope_tabs is not None:
        in_specs += [_const_spec((n_tok, LANES))]
        args += [k_tabs[1]]
    out_shape = [jax.ShapeDtypeStruct((n_b, n_tok, N_HEADS * HEAD_PAD), BF16),
                 jax.ShapeDtypeStruct((n_b, n_tok, N_HEADS * HEAD_PAD), BF16),
                 jax.ShapeDtypeStruct((n_b, n_tok, v_width), BF16),
                 jax.ShapeDtypeStruct((n_b, n_tok, CONV_CH), BF16)]
    out_specs = [seq(N_HEADS * HEAD_PAD), seq(N_HEADS * HEAD_PAD), seq(v_width), seq(CONV_CH)]
    if state:
        out_shape += [jax.ShapeDtypeStruct((n_b, n_tok, KV_LORA), F32),
                      jax.ShapeDtypeStruct((n_b, n_tok, LANES), F32)]
        out_specs += [seq(KV_LORA), seq(LANES)]
    return pl.pallas_call(
        functools.partial(_front0_kernel, n_tok=n_tok, rope=rope_tabs is not None, state=state),
        out_shape=out_shape, grid=(n_b,), in_specs=in_specs, out_specs=out_specs,
        scratch_shapes=[pltpu.VMEM((n_tok, CONV_CH), F32), pltpu.VMEM((n_tok, CONV_CH), F32)],
        compiler_params=_params(1), name="front0",
    )(*args)


def _ctx_kv_kernel(ckv_ref, r_ref, wk_ref, wv_ref, ka_ref, k_ref, v_ref):
    rows = pl.ds(0, ckv_ref.shape[0])
    _expand_kv(ckv_ref[...].astype(BF16), r_ref[...], None, wk_ref, wv_ref, ka_ref[...], k_ref, v_ref, rows)


def _ctx_kv(cache_c_kv, cache_k_rope_pad, w, w_v):
    n_b, n_ctx, _ = cache_c_kv.shape
    v_width = w_v.shape[1]
    seq = lambda width: pl.BlockSpec((None, n_ctx, width), lambda b: (b, 0, 0))
    return pl.pallas_call(
        _ctx_kv_kernel,
        out_shape=[jax.ShapeDtypeStruct((n_b, n_ctx, N_HEADS * HEAD_PAD), BF16),
                   jax.ShapeDtypeStruct((n_b, n_ctx, v_width), BF16)],
        grid=(n_b,),
        in_specs=[seq(KV_LORA), seq(LANES), _const_spec((KV_LORA, N_HEADS * HEAD_PAD)),
                  _const_spec((KV_LORA, v_width)), _const_spec((n_ctx, LANES))],
        out_specs=[seq(N_HEADS * HEAD_PAD), seq(v_width)],
        compiler_params=_params(1), name="ctx_kv",
    )(cache_c_kv, cache_k_rope_pad, w["w_k"], w_v, _head_tables(w["k_gain"], None, n_ctx, False)[0])


def _attn_kernel(*refs, with_ctx):
    if with_ctx:
        q_ref, k_ref, v_ref, kc_ref, vc_ref, o_ref = refs
    else:
        q_ref, k_ref, v_ref, o_ref = refs
    ones_lane = v_ref.shape[-1] == N_HEADS * HEAD_PAD
    lane = lax.broadcasted_iota(jnp.int32, (q_ref.shape[0], LANES), 1)
    for pair in range(N_HEADS // 2):
        outs = []
        for hd in (2 * pair, 2 * pair + 1):
            sl = slice(hd * HEAD_PAD, (hd + 1) * HEAD_PAD)
            vsl = sl if ones_lane else slice(pair * LANES, (pair + 1) * LANES)
            qh = q_ref[:, sl]
            s = _dot_nt(qh, k_ref[:, sl])
            m = jnp.max(s, axis=-1, keepdims=True)
            if with_ctx:
                sc = _dot_nt(qh, kc_ref[:, sl])
                m = jnp.maximum(m, jnp.max(sc, axis=-1, keepdims=True))
            p = jnp.exp(s - m)
            o = _dot(p.astype(BF16), v_ref[:, vsl])
            if with_ctx:
                pc = jnp.exp(sc - m)
                o = o + _dot(pc.astype(BF16), vc_ref[:, vsl])
            if ones_lane:
                den = o[:, V_HEAD:V_HEAD + 1]
            else:
                den = jnp.sum(p, axis=-1, keepdims=True)
                if with_ctx:
                    den = den + jnp.sum(pc, axis=-1, keepdims=True)
            outs.append(o / den)
        odd = pltpu.roll(outs[1], V_HEAD, 1) if ones_lane else outs[1]
        o_ref[:, pair * LANES:(pair + 1) * LANES] = jnp.where(lane < V_HEAD, outs[0], odd).astype(BF16)


def _attention(q, k, v, kc=None, vc=None):
    n_b, n_tok, _ = q.shape
    with_ctx = kc is not None
    tq = min(n_tok, ATTN_ROWS)
    qspec = lambda width: pl.BlockSpec((None, tq, width), lambda b, i: (b, i, 0))
    kvspec = lambda a: pl.BlockSpec((None,) + a.shape[1:], lambda b, i: (b, 0, 0))
    in_specs = [qspec(N_HEADS * HEAD_PAD), kvspec(k), kvspec(v)]
    args = [q, k, v]
    if with_ctx:
        in_specs += [kvspec(kc), kvspec(vc)]
        args += [kc, vc]
    return pl.pallas_call(
        functools.partial(_attn_kernel, with_ctx=with_ctx),
        out_shape=jax.ShapeDtypeStruct((n_b, n_tok, N_HEADS * V_HEAD), BF16),
        grid=(n_b, n_tok // tq), in_specs=in_specs, out_specs=qspec(N_HEADS * V_HEAD),
        compiler_params=_params(2), name="attention",
    )(*args)


def _moe_front(x1, mod_ref, n2_ref, wrh_ref, wrl_ref, h2_ref, aff_ref, rows):
    sh2, sc2 = mod_ref[3:4, :], mod_ref[4:5, :]
    h2 = _rms(x1, n2_ref[...]) * (1.0 + sc2) + sh2
    h_hi, h_lo = _split_hi_lo(h2)
    logit = _dot(h_hi, wrh_ref[...]) + _dot(h_lo, wrh_ref[...]) + _dot(h_hi, wrl_ref[...])
    real = lax.broadcasted_iota(jnp.int32, (1, LANES), 1) < N_EXPERTS
    top = jnp.max(jnp.where(real, logit, -jnp.inf), axis=1, keepdims=True)
    e = jnp.where(real, jnp.exp(logit - top), 0.0)
    a = e / jnp.sum(e, axis=1, keepdims=True)
    a_hi = a.astype(BF16).astype(F32)
    a_mid = (a - a_hi).astype(BF16).astype(F32)
    a_lo = a - a_hi - a_mid
    h2_ref[rows, 0:D_MODEL] = h_hi
    h2_ref[rows, D_MODEL:XE_W] = (a_hi + pltpu.roll(a_mid, N_EXPERTS, 1)
                                  + pltpu.roll(a_lo, 2 * N_EXPERTS, 1)).astype(BF16)
    aff_ref[...] = a.T[0:N_EXPERTS, :]


def _post0_kernel(x_ref, attn_ref, conv_ref, wo_ref, mod_ref, n2_ref, wrh_ref, wrl_ref,
                  x1_ref, h2_ref, aff_ref):
    n_attn = N_HEADS * V_HEAD
    mix = _dot(attn_ref[...], wo_ref[0:n_attn, :]) + _dot(conv_ref[...], wo_ref[n_attn:, :])
    for t in range(x_ref.shape[0] // ROWS):
        rows = pl.ds(t * ROWS, ROWS)
        x1 = x_ref[rows, :] + mod_ref[2:3, :] * mix[t * ROWS:(t + 1) * ROWS, :]
        x1_ref[rows, :] = x1
        _moe_front(x1, mod_ref, n2_ref, wrh_ref, wrl_ref, h2_ref, aff_ref.at[t], rows)


def _post0(x, attn, conv, mod, w, n_tok):
    n_all = x.shape[0]
    shared_mod = mod.shape[0] == 1
    rows = MIX_ROWS if shared_mod else min(n_tok, MIX_ROWS)
    tiles_per_seq = max(n_tok // rows, 1)
    tile = lambda width: pl.BlockSpec((rows, width), lambda i: (i, 0))
    const = lambda shape: pl.BlockSpec(shape, lambda i: (0,) * len(shape))
    mod_map = (lambda i: (0, 0, 0)) if shared_mod else (lambda i: (i // tiles_per_seq, 0, 0))
    return pl.pallas_call(
        _post0_kernel,
        out_shape=[jax.ShapeDtypeStruct((n_all, D_MODEL), F32),
                   jax.ShapeDtypeStruct((n_all, XE_W), BF16),
                   jax.ShapeDtypeStruct((n_all // ROWS, N_EXPERTS, ROWS), F32)],
        grid=(n_all // rows,),
        in_specs=[tile(D_MODEL), tile(N_HEADS * V_HEAD), tile(CONV_CH), const((D_MODEL, D_MODEL)),
                  pl.BlockSpec((None, 6, D_MODEL), mod_map), const((1, D_MODEL)),
                  const((D_MODEL, LANES)), const((D_MODEL, LANES))],
        out_specs=[tile(D_MODEL), tile(XE_W),
                   pl.BlockSpec((rows // ROWS, N_EXPERTS, ROWS), lambda i: (i, 0, 0))],
        compiler_params=_params(1), name="post0",
    )(x, attn, conv, w["w_o"], mod, w["norm2"], w["wr_hi"], w["wr_lo"])


def _fnet_kernel(x_ref, mod_ref, n1_ref, cc_ref, sc_ref, dft_ref, flip_ref, wf_ref, n2_ref, wrh_ref, wrl_ref,
                 x1_ref, h2_ref, aff_ref, h_s, r_s, y_s, *, n_tok):
    sh1, sc1, g1 = mod_ref[0:1, :], mod_ref[1:2, :], mod_ref[2:3, :]
    n_chunks = n_tok // ROWS
    half = n_tok // 2
    blk = flip_ref.shape[0]
    cc, sc = cc_ref[...].astype(BF16), sc_ref[...].astype(BF16)
    for c in range(n_chunks):
        rows = pl.ds(c * ROWS, ROWS)
        h_s[rows, :] = (_rms(x_ref[rows, :], n1_ref[...]) * (1.0 + sc1) + sh1).astype(BF16)
    n_flip = half // blk
    for i in range(n_flip):
        src = h_s[pl.ds(half + (n_flip - 1 - i) * blk, blk), :]
        r_s[pl.ds(i * blk, blk), :] = _dot(flip_ref[...], src)
    mirror = pltpu.roll(r_s[...], 1, 0)
    row = lax.broadcasted_iota(jnp.int32, mirror.shape, 0)
    mirror = jnp.where(row == 0, 0.0, mirror)
    low = h_s[0:half, :].astype(F32)
    even, odd = (low + mirror).astype(BF16), (low - mirror).astype(BF16)
    mid = h_s[pl.ds(half, 16), :]
    mids = []
    for g in range(FNET_GROUPS):
        sl = slice(g * FNET_CH, (g + 1) * FNET_CH)
        y_s[0:half, sl] = _dot(even[:, sl], cc).astype(BF16)
        y_s[half:n_tok, sl] = _dot(odd[:, sl], sc).astype(BF16)
        mids.append(_dot(mid[:, sl], cc)[0:1, :])
    mid_row = jnp.concatenate(mids, axis=1)
    scale = 1.0 / float(np.sqrt(n_tok * FNET_CH))
    wide = min(n_tok, MIX_ROWS)
    for c in range(n_chunks):
        rows = pl.ds(c * ROWS, ROWS)
        if (c * ROWS) % wide == 0:
            parity = lax.broadcasted_iota(jnp.int32, (wide, 1), 0) & 1
            f = _dot(dft_ref[pl.ds(c * ROWS, wide), :].astype(BF16), y_s[...])
            f = (f + jnp.where(parity == 0, 1.0, -1.0) * mid_row) * scale
            mix = _dot(f.astype(BF16), wf_ref[...])
        off = (c * ROWS) % wide
        x1 = x_ref[rows, :] + g1 * mix[off:off + ROWS, :]
        x1_ref[rows, :] = x1
        _moe_front(x1, mod_ref, n2_ref, wrh_ref, wrl_ref, h2_ref, aff_ref.at[c], rows)


def _fnet(x, mod, w, dft):
    n_b, n_tok, _ = x.shape
    shared_mod = mod.shape[0] == 1
    seq = lambda width: pl.BlockSpec((None, n_tok, width), lambda b: (b, 0, 0))
    cc, sc, dft_n, flip = dft
    tiles = n_tok // ROWS
    return pl.pallas_call(
        functools.partial(_fnet_kernel, n_tok=n_tok),
        out_shape=[jax.ShapeDtypeStruct((n_b, n_tok, D_MODEL), F32),
                   jax.ShapeDtypeStruct((n_b, n_tok, XE_W), BF16),
                   jax.ShapeDtypeStruct((n_b * tiles, N_EXPERTS, ROWS), F32)],
        grid=(n_b,),
        in_specs=[seq(D_MODEL),
                  pl.BlockSpec((None, 6, D_MODEL), (lambda b: (0, 0, 0)) if shared_mod else (lambda b: (b, 0, 0))),
                  _const_spec((1, D_MODEL)), _const_spec((FNET_CH, FNET_CH)), _const_spec((FNET_CH, FNET_CH)),
                  pl.BlockSpec((n_tok, n_tok), lambda b: (0, 0), pipeline_mode=pl.Buffered(1)),
                  _const_spec(flip.shape), _const_spec((D_MODEL, D_MODEL)), _const_spec((1, D_MODEL)),
                  _const_spec((D_MODEL, LANES)), _const_spec((D_MODEL, LANES))],
        out_specs=[seq(D_MODEL), seq(XE_W), pl.BlockSpec((tiles, N_EXPERTS, ROWS), lambda b: (b, 0, 0))],
        scratch_shapes=[pltpu.VMEM((n_tok, D_MODEL), BF16), pltpu.VMEM((n_tok // 2, D_MODEL), F32),
                        pltpu.VMEM((n_tok, D_MODEL), BF16)],
        compiler_params=_params(1), name="fnet",
    )(x, mod, w["norm1"], cc, sc, dft_n, flip, w["w_f"], w["norm2"], w["wr_hi"], w["wr_lo"])


def _route_kernel(aff_ref, tri_ref, posm_ref, s_ref, *, n_blk, cap):
    def count(pred):
        acc = jnp.zeros((N_EXPERTS, ROWS), F32)
        for b in range(n_blk):
            acc = acc + jnp.where(pred(aff_ref[b]), 1.0, 0.0)
        return jnp.sum(acc, axis=1, keepdims=True)

    def as_f32(bits):
        return pltpu.bitcast(bits, F32)

    def search(_, carry):
        lo, hi = carry
        mid = lo + ((hi - lo) >> 1)
        mid_f = as_f32(mid)
        ok = count(lambda a: a >= mid_f) >= cap
        return jnp.where(ok, mid, lo), jnp.where(ok, hi, mid)

    one_bits = 0x3F800000
    lo0 = jnp.zeros((N_EXPERTS, 1), jnp.int32)
    hi0 = jnp.full((N_EXPERTS, 1), one_bits + 1, jnp.int32)
    lo, hi = lax.fori_loop(0, 31, search, (lo0, hi0))
    lo_f, ub = as_f32(lo), as_f32(hi)
    thr = lo_f
    pending = jnp.ones((N_EXPERTS, 1), F32)
    for _ in range(3):
        cur = jnp.full((N_EXPERTS, ROWS), -1.0, F32)
        for b in range(n_blk):
            a = aff_ref[b]
            cur = jnp.maximum(cur, jnp.where((a >= lo_f) & (a < ub), a, -1.0))
        cur = jnp.max(cur, axis=1, keepdims=True)
        take = (count(lambda a: a >= cur) >= cap) & (pending > 0.0)
        thr = jnp.where(take, cur, thr)
        pending = jnp.where(take, 0.0, pending)
        ub = cur
    need = cap - count(lambda a: a > thr)

    lane = lax.broadcasted_iota(jnp.int32, (N_EXPERTS, LANES), 1)
    carry_tie = jnp.zeros((N_EXPERTS, 1), F32)
    carry_pos = jnp.zeros((N_EXPERTS, 1), F32)
    s_acc = jnp.zeros((N_EXPERTS, LANES), jnp.int32)
    for b in range(n_blk):
        v = aff_ref[b]
        eq = jnp.where(v == thr, 1.0, 0.0)
        tie_rank = _dot(eq.astype(BF16), tri_ref[...]) + carry_tie
        sel = jnp.where((v > thr) | ((v == thr) & (tie_rank < need)), 1.0, 0.0)
        pos = _dot(sel.astype(BF16), tri_ref[...]) + carry_pos
        posm_ref[b] = jnp.where(sel > 0.0, pos, -1.0)
        s_acc = jnp.where(lane == b, carry_pos.astype(jnp.int32), s_acc)
        carry_tie = carry_tie + jnp.sum(eq, axis=1, keepdims=True)
        n_sel = jnp.sum(sel, axis=1, keepdims=True)
        carry_pos = carry_pos + jnp.floor((n_sel + (ALIGN - 1)) * (1.0 / ALIGN)) * ALIGN
    s_ref[...] = jnp.where(lane == n_blk, carry_pos.astype(jnp.int32), s_acc)


def _route(aff, tri, cap):
    n_blk = aff.shape[0]
    assert n_blk < LANES
    full = lambda shape: pl.BlockSpec(shape, lambda i: (0,) * len(shape))
    posm, s_tab = pl.pallas_call(
        functools.partial(_route_kernel, n_blk=n_blk, cap=cap),
        out_shape=[jax.ShapeDtypeStruct((n_blk, N_EXPERTS, ROWS), F32),
                   jax.ShapeDtypeStruct((N_EXPERTS, LANES), jnp.int32)],
        grid=(1,),
        in_specs=[full((n_blk, N_EXPERTS, ROWS)), full((ROWS, ROWS))],
        out_specs=[full((n_blk, N_EXPERTS, ROWS)), full((N_EXPERTS, LANES))],
        compiler_params=_params(1), name="route",
    )(aff, tri)
    return posm, s_tab[:, :n_blk + 1]


def _ceil_div_pow2(x, d):
    return lax.shift_right_logical(x + (d - 1), d.bit_length() - 1)


def _block_units(s_ref, b):
    base = [s_ref[e, b] for e in range(N_EXPERTS)]
    units = [_ceil_div_pow2(s_ref[e, b + 1] - base[e], WIN) for e in range(N_EXPERTS)]
    first_unit, total = [], 0
    for e in range(N_EXPERTS):
        first_unit.append(total)
        total = total + units[e]
    return base, units, first_unit, total


def _for_units(units, fn):
    for e in range(N_EXPERTS):
        def body(u, _, e=e):
            fn(e, u)
            return 0
        lax.fori_loop(0, units[e], body, 0)


def _window_onehot(posm_ref, e, first, lo):
    slot = lax.broadcasted_iota(jnp.int32, (WIN, ROWS), 0) + first
    p = posm_ref[e:e + 1, :].astype(jnp.int32)
    return jnp.where((p == slot) & (p >= lo), 1.0, 0.0)


def _gate_lanes(e):
    lane = lax.broadcasted_iota(jnp.int32, (1, LANES), 1)
    return ((lane & (N_EXPERTS - 1)) == e) & (lane < 3 * N_EXPERTS)


def _dispatch_kernel(s_ref, h_ref, posm_ref, xe_ref, sel_s, x_s, sem, *, n_blk, cap_pad, common_rows):
    for t in range(STEP_BLOCKS):
        _dispatch_block(pl.program_id(0) * STEP_BLOCKS + t, t, s_ref, h_ref.at[pl.ds(t * ROWS, ROWS), :],
                        posm_ref.at[t], xe_ref, sel_s, x_s, sem, n_blk, cap_pad, common_rows)


def _dispatch_block(b, buf, s_ref, h_ref, posm_ref, xe_ref, sel_s, x_s, sem, n_blk, cap_pad, common_rows):
    base, units, first_unit, n_units = _block_units(s_ref, b)

    @pl.when(b == 0)
    def _():
        sel_s[...] = jnp.zeros_like(sel_s)

    def unit_rows(first_unit_e, u):
        return pl.ds(pl.multiple_of((first_unit_e + u) * WIN, WIN), WIN)

    def pick(e, u):
        first = base[e] + u * WIN
        sel_s[unit_rows(first_unit[e], u), :] = _window_onehot(posm_ref, e, first, first).astype(BF16)

    _for_units(units, pick)

    def gather(rows):
        x_s[buf, rows, :] = _dot(sel_s[rows, :], h_ref[...]).astype(BF16)

    def gather_chunk(c, _):
        gather(pl.ds(pl.multiple_of(c * ROWS, ROWS), ROWS))
        return 0

    gather(pl.ds(0, common_rows))
    lax.fori_loop(common_rows // ROWS, _ceil_div_pow2(n_units * WIN, ROWS), gather_chunk, 0)

    def unit_copy(slot, blk_base_e, blk_first_e, e, u):
        dst = pl.ds(pl.multiple_of(blk_base_e + u * WIN, ALIGN), WIN)
        return pltpu.make_async_copy(x_s.at[slot, unit_rows(blk_first_e, u), :], xe_ref.at[e, dst, :],
                                     sem.at[slot, e])

    def for_expert_units(n, fn):
        lax.fori_loop(0, n, lambda u, _: (fn(u), 0)[1], 0)

    prev_base, prev_units, prev_first, _ = _block_units(s_ref, jnp.maximum(b - 1, 0))
    for e in range(N_EXPERTS):
        for_expert_units(jnp.where(b > 0, prev_units[e], 0),
                         lambda u, e=e: unit_copy(1 - buf, prev_base[e], prev_first[e], e, u).wait())
        for_expert_units(units[e], lambda u, e=e: unit_copy(buf, base[e], first_unit[e], e, u).start())

    @pl.when(b == n_blk - 1)
    def _():
        for e in range(N_EXPERTS):
            for_expert_units(units[e], lambda u, e=e: unit_copy(buf, base[e], first_unit[e], e, u).wait())
        x_s[buf, 0:WIN, :] = jnp.zeros((WIN, XE_W), BF16)

        def tail(fn):
            for e in range(N_EXPERTS):
                total = s_ref[e, n_blk]
                n_win = (cap_pad - total) // WIN

                def wide(c, _):
                    row = pl.multiple_of(total + c * WIN, ALIGN)
                    fn(pltpu.make_async_copy(x_s.at[buf, pl.ds(0, WIN), :],
                                             xe_ref.at[e, pl.ds(row, WIN), :], sem.at[buf, e]))
                    return 0

                def narrow(c, _):
                    row = pl.multiple_of(total + n_win * WIN + c * ALIGN, ALIGN)
                    fn(pltpu.make_async_copy(x_s.at[buf, pl.ds(0, ALIGN), :],
                                             xe_ref.at[e, pl.ds(row, ALIGN), :], sem.at[buf, e]))
                    return 0

                lax.fori_loop(0, n_win, wide, 0)
                lax.fori_loop(0, (cap_pad - total - n_win * WIN) // ALIGN, narrow, 0)

        tail(lambda c: c.start())
        tail(lambda c: c.wait())


def _ffn_kernel(*refs, n_blks):
    n_sets = len(n_blks)
    s_refs, x_refs = refs[:n_sets], refs[n_sets:2 * n_sets]
    wg_ref, wu_ref, wd_ref = refs[2 * n_sets:2 * n_sets + 3]
    y_refs = refs[2 * n_sets + 3:3 * n_sets + 3]
    wg_s, wu_s, wd_s = refs[3 * n_sets + 3:]
    e = pl.program_id(0)
    wg_s[...] = wg_ref[...].astype(BF16)
    wu_s[...] = wu_ref[...].astype(BF16)
    wd_s[...] = wd_ref[...].astype(BF16)
    mine = _gate_lanes(e)

    half = ROWS // 2
    for s_ref, x_ref, y_ref, n_blk in zip(s_refs, x_refs, y_refs, n_blks):
        n_half = _ceil_div_pow2(s_ref[e, n_blk], half)

        def tile(rows, x_ref=x_ref, y_ref=y_ref):
            x = x_ref[rows, 0:D_MODEL]
            pieces = x_ref[rows, D_MODEL:XE_W].astype(F32)
            gate = jnp.sum(jnp.where(mine, pieces, 0.0), axis=1, keepdims=True)
            a = _dot(x, wg_s[...])
            u = _dot(x, wu_s[...])
            hid = a * (1.0 / (1.0 + jnp.exp(-a))) * u * gate
            y_ref[rows, :] = _dot(hid.astype(BF16), wd_s[...]).astype(BF16)

        def full(j, _, tile=tile):
            tile(pl.ds(pl.multiple_of(j * WIDE_ROWS, WIDE_ROWS), WIDE_ROWS))
            return 0

        def blank(j, _, y_ref=y_ref):
            y_ref[pl.ds(pl.multiple_of(j * half, half), half), :] = jnp.zeros((half, D_MODEL), BF16)
            return 0

        lax.fori_loop(0, lax.shift_right_logical(n_half, 2), full, 0)

        @pl.when((n_half & 2) != 0)
        def _(tile=tile, n_half=n_half):
            tile(pl.ds(pl.multiple_of((n_half & -4) * half, WIDE_ROWS), ROWS))

        @pl.when((n_half & 1) != 0)
        def _(tile=tile, n_half=n_half):
            tile(pl.ds(pl.multiple_of((n_half - 1) * half, half), half))

        lax.fori_loop(n_half, y_ref.shape[0] // half, blank, 0)


def _combine_kernel(s_ref, y_ref, posm_ref, x1_ref, mod_ref, o_ref, yw_s, sel_s, acc_s, sem,
                    *, n_blk, cap_pad, common_rows):
    for t in range(STEP_BLOCKS):
        rows = pl.ds(t * ROWS, ROWS)
        _combine_block(pl.program_id(0) * STEP_BLOCKS + t, t, s_ref, y_ref, posm_ref.at[t], x1_ref.at[rows, :],
                       mod_ref, o_ref.at[rows, :], yw_s, sel_s, acc_s, sem, n_blk, cap_pad, common_rows)


def _combine_block(b, buf, s_ref, y_ref, posm_ref, x1_ref, mod_ref, o_ref, yw_s, sel_s, acc_s, sem, n_blk, cap_pad,
                   common_rows):
    base, units, first_unit, n_units = _block_units(s_ref, b)

    def unit_rows(first_unit_e, u):
        return pl.ds(pl.multiple_of((first_unit_e + u) * WIN, WIN), WIN)

    def bounds(base_e, u):
        lo = base_e + u * WIN
        return lo, jnp.minimum(lo, cap_pad - WIN)

    def copies(slot, blk, fn):
        blk_base, blk_units, blk_first, _ = _block_units(s_ref, blk)

        def one(e, u):
            src = pl.ds(pl.multiple_of(bounds(blk_base[e], u)[1], ALIGN), WIN)
            fn(pltpu.make_async_copy(y_ref.at[e, src, :], yw_s.at[slot, unit_rows(blk_first[e], u), :],
                                     sem.at[slot]))

        _for_units(blk_units, one)

    @pl.when(b == 0)
    def _():
        yw_s[...] = jnp.zeros_like(yw_s)
        copies(0, 0, lambda c: c.start())

    @pl.when(b + 1 < n_blk)
    def _():
        copies(1 - buf, b + 1, lambda c: c.start())

    def pick(e, u):
        lo, first = bounds(base[e], u)
        sel_s[unit_rows(first_unit[e], u), :] = _window_onehot(posm_ref, e, first, lo).astype(BF16)

    _for_units(units, pick)
    n_chunks = jnp.maximum(_ceil_div_pow2(n_units * WIN, ROWS), common_rows // ROWS)

    def blank(u, _):
        sel_s[unit_rows(u, 0), :] = jnp.zeros((WIN, ROWS), BF16)
        return 0

    lax.fori_loop(n_units, n_chunks * (ROWS // WIN), blank, 0)
    copies(buf, b, lambda c: c.wait())

    def weighted(rows):
        return _dot(sel_s[rows, :].T, yw_s[buf, rows, :])

    def add_chunk(c, _):
        acc_s[...] += weighted(pl.ds(pl.multiple_of(c * ROWS, ROWS), ROWS))
        return 0

    acc_s[...] = weighted(pl.ds(0, common_rows))
    lax.fori_loop(common_rows // ROWS, n_chunks, add_chunk, 0)
    o_ref[...] = x1_ref[...] + mod_ref[5:6, :] * acc_s[...]


def _padded_capacity(cap, n_blk):
    return -(-(cap + (ALIGN - 1) * n_blk + WIN) // ROWS) * ROWS


def _step_spec(*shape):
    return pl.BlockSpec((STEP_BLOCKS,) + shape, lambda i, s: (i,) + (0,) * len(shape))


def _common_rows(n_tok):
    return COMMON_ROWS + (ROWS if n_tok > ROWS else 0)


def _dispatch(h2, posm, s_tab, cap, n_tok):
    n_blk = h2.shape[0] // ROWS
    cap_pad = _padded_capacity(cap, n_blk)
    return pl.pallas_call(
        functools.partial(_dispatch_kernel, n_blk=n_blk, cap_pad=cap_pad, common_rows=_common_rows(n_tok)),
        out_shape=jax.ShapeDtypeStruct((N_EXPERTS, cap_pad, XE_W), BF16),
        grid_spec=pltpu.PrefetchScalarGridSpec(
            num_scalar_prefetch=1, grid=(n_blk // STEP_BLOCKS,),
            in_specs=[pl.BlockSpec((STEP_BLOCKS * ROWS, XE_W), lambda i, s: (i, 0)), _step_spec(N_EXPERTS, ROWS)],
            out_specs=pl.BlockSpec(memory_space=pl.ANY),
            scratch_shapes=[pltpu.VMEM((MAX_UNITS * WIN, ROWS), BF16), pltpu.VMEM((2, MAX_UNITS * WIN, XE_W), BF16),
                            pltpu.SemaphoreType.DMA((2, N_EXPERTS))]),
        compiler_params=_params(1), name="dispatch",
    )(s_tab, h2, posm)


def _ffn(xes, s_tabs, wg, wu, wd):
    n_sets = len(xes)
    per_expert = lambda rows, cols: pl.BlockSpec((None, rows, cols), lambda e, *s: (e, 0, 0))
    return pl.pallas_call(
        functools.partial(_ffn_kernel, n_blks=tuple(s.shape[1] - 1 for s in s_tabs)),
        out_shape=[jax.ShapeDtypeStruct(xe.shape[:2] + (D_MODEL,), BF16) for xe in xes],
        grid_spec=pltpu.PrefetchScalarGridSpec(
            num_scalar_prefetch=n_sets, grid=(N_EXPERTS,),
            in_specs=[per_expert(xe.shape[1], XE_W) for xe in xes]
            + [per_expert(D_MODEL, EXPERT_FF), per_expert(D_MODEL, EXPERT_FF), per_expert(EXPERT_FF, D_MODEL)],
            out_specs=[per_expert(xe.shape[1], D_MODEL) for xe in xes],
            scratch_shapes=[pltpu.VMEM((D_MODEL, EXPERT_FF), BF16), pltpu.VMEM((D_MODEL, EXPERT_FF), BF16),
                            pltpu.VMEM((EXPERT_FF, D_MODEL), BF16)]),
        compiler_params=_params(1), name="ffn",
    )(*s_tabs, *xes, wg, wu, wd)


def _combine(y, posm, s_tab, x1, mod, n_tok):
    n_all = x1.shape[0]
    n_blk = n_all // ROWS
    shared_mod = mod.shape[0] == 1
    assert shared_mod or n_tok % (STEP_BLOCKS * ROWS) == 0
    steps_per_seq = max(n_tok // (STEP_BLOCKS * ROWS), 1)
    mod_map = (lambda i, s: (0, 0, 0)) if shared_mod else (lambda i, s: (i // steps_per_seq, 0, 0))
    rows = pl.BlockSpec((STEP_BLOCKS * ROWS, D_MODEL), lambda i, s: (i, 0))
    return pl.pallas_call(
        functools.partial(_combine_kernel, n_blk=n_blk, cap_pad=y.shape[1], common_rows=_common_rows(n_tok)),
        out_shape=jax.ShapeDtypeStruct((n_all, D_MODEL), F32),
        grid_spec=pltpu.PrefetchScalarGridSpec(
            num_scalar_prefetch=1, grid=(n_blk // STEP_BLOCKS,),
            in_specs=[pl.BlockSpec(memory_space=pl.ANY), _step_spec(N_EXPERTS, ROWS), rows,
                      pl.BlockSpec((None, 6, D_MODEL), mod_map)],
            out_specs=rows,
            scratch_shapes=[pltpu.VMEM((2, MAX_UNITS * WIN, D_MODEL), BF16), pltpu.VMEM((MAX_UNITS * WIN, ROWS), BF16),
                            pltpu.VMEM((ROWS, D_MODEL), F32), pltpu.SemaphoreType.DMA((2,))]),
        compiler_params=_params(1), name="combine",
    )(s_tab, y, posm, x1, mod)


def _rope_tables(n_tok):
    rows = n_tok // GRID_W
    row = np.repeat(np.arange(rows, dtype=np.float64), GRID_W)
    col = np.tile(np.arange(GRID_W, dtype=np.float64), rows)
    axis_dim = QK_ROPE // 2
    inv_freq = ROPE_THETA ** (-np.arange(0, axis_dim, 2, dtype=np.float64) / axis_dim)
    ang = np.concatenate([row[:, None] * inv_freq, col[:, None] * inv_freq], axis=-1)
    cos = np.ones((n_tok, LANES))
    sin = np.zeros((n_tok, LANES))
    cos[:, ROPE_LANE0:ROPE_LANE0 + QK_ROPE] = np.repeat(np.cos(ang), 2, axis=1)
    sgn = np.tile(np.array([-1.0, 1.0]), QK_ROPE // 2)
    sin[:, ROPE_LANE0:ROPE_LANE0 + QK_ROPE] = np.repeat(np.sin(ang), 2, axis=1) * sgn
    return jnp.asarray(cos, F32), jnp.asarray(sin, F32)


def _dft_tables(n_tok):
    def cs(n):
        k = np.arange(n)
        ang = 2.0 * np.pi * ((k[:, None] * k[None, :]) % n) / n
        return np.cos(ang), np.sin(ang)
    cc, sc = cs(FNET_CH)
    cn, sn = cs(n_tok)
    half = n_tok // 2
    folded = np.concatenate([cn[:, :half], -sn[:, :half]], axis=1)
    flip = np.fliplr(np.eye(min(ROWS, half)))
    return tuple(jnp.asarray(t, F32) for t in (cc, sc, folded)) + (jnp.asarray(flip, BF16),)


def _pad_heads(w, width):
    lead = w.shape[:-1]
    w = w.reshape(lead + (N_HEADS, width))
    w = jnp.pad(w, [(0, 0)] * len(lead) + [(0, 0), (0, HEAD_PAD - width)])
    return w.reshape(lead + (N_HEADS * HEAD_PAD,))


def _pair_swap_lanes(n_groups):
    perm = np.arange(n_groups * LANES).reshape(n_groups, LANES)
    rot = perm[:, ROPE_LANE0:ROPE_LANE0 + QK_ROPE].reshape(n_groups, QK_ROPE // 2, 2)[:, :, ::-1]
    perm[:, ROPE_LANE0:ROPE_LANE0 + QK_ROPE] = rot.reshape(n_groups, QK_ROPE)
    return perm.reshape(-1)


def _spare_from_rotary(t, src=None):
    src = t if src is None else src
    lead = t.shape[:-1]
    t, src = t.reshape(lead + (-1, LANES)), src.reshape(lead + (-1, LANES))
    t = jnp.concatenate([t[..., :QK_DIM], src[..., ROPE_LANE0:ROPE_LANE0 + QK_ROPE]], axis=-1)
    return t.reshape(lead + (-1,))


def _head_tables(gain, rope_tabs, n_tok, query):
    if rope_tabs is None:
        tab = jnp.broadcast_to(gain, (n_tok, LANES))
        return (tab if query else _spare_from_rotary(tab),)
    cos, sin = rope_tabs
    a, b = gain * cos, gain[:, _pair_swap_lanes(1)] * sin
    return (_spare_from_rotary(a, b),) if query else (_spare_from_rotary(a), _spare_from_rotary(b))


def _layer0_weights(norm1, norm2, w_in, q_a_norm, w_q_up, q_norm, kv_a_norm, w_kv_up, k_norm, conv_w, w_o):
    c0 = Q_LORA + KV_LORA
    rope_cols = jnp.pad(w_in[:, c0:c0 + QK_ROPE], ((0, 0), (ROPE_LANE0, LANES - ROPE_LANE0 - QK_ROPE)))
    w_in_pad = jnp.concatenate([w_in[:, :c0], _spare_from_rotary(rope_cols),
                                _spare_from_rotary(rope_cols[:, _pair_swap_lanes(1)]),
                                w_in[:, c0 + QK_ROPE:]], axis=1)
    kv = w_kv_up.reshape(KV_LORA, N_HEADS, QK_NOPE + V_HEAD)
    w_q = _pad_heads(w_q_up, QK_DIM)
    w_q = _spare_from_rotary(w_q, w_q[:, _pair_swap_lanes(N_HEADS)])
    head_gain = lambda g: jnp.pad(g, (0, HEAD_PAD - QK_DIM)).reshape(1, -1)
    return dict(
        norm1=norm1.reshape(1, -1), norm2=norm2.reshape(1, -1), w_in=w_in_pad.astype(BF16),
        q_a_norm=q_a_norm.reshape(1, -1), w_q=w_q.astype(BF16),
        q_gain=head_gain(q_norm) * (QK_DIM ** -0.5), kv_a_norm=kv_a_norm.reshape(1, -1),
        w_k=_pad_heads(kv[:, :, :QK_NOPE].reshape(KV_LORA, -1), QK_NOPE).astype(BF16),
        w_v=kv[:, :, QK_NOPE:].reshape(KV_LORA, -1).astype(BF16),
        w_v_wide=_pad_heads(kv[:, :, QK_NOPE:].reshape(KV_LORA, -1), V_HEAD).astype(BF16), k_gain=head_gain(k_norm),
        conv_w=conv_w, w_o=w_o.astype(BF16))


def _router_weights(w_router):
    hi, lo = _split_hi_lo(jnp.pad(w_router, ((0, 0), (0, LANES - N_EXPERTS))))
    return dict(wr_hi=hi, wr_lo=lo)


def _moe(sets, tri, w):
    routed = []
    for x1, h2, aff, mod, n_tok in sets:
        cap = CAPACITY_FACTOR * x1.shape[0] // N_EXPERTS
        posm, s_tab = _route(aff, tri, cap)
        routed.append((posm, s_tab, _dispatch(h2, posm, s_tab, cap, n_tok)))
    ys = _ffn([r[2] for r in routed], [r[1] for r in routed], w["wg"], w["wu"], w["wd"])
    return [_combine(y, posm, s_tab, x1, mod, n_tok)
            for y, (posm, s_tab, _), (x1, _, _, mod, n_tok) in zip(ys, routed, sets)]


def _mixer0(x, mod, l0, ctx, rope_tabs, state):
    n_b, n_tok, _ = x.shape
    n_keys = n_tok + (0 if ctx is None else ctx[0].shape[1])
    w_v = l0["w_v_wide"] if n_keys > WIDE_ROWS else l0["w_v"]
    q, k, v, conv, *new_state = _front0(x, mod, l0, rope_tabs, w_v, state)
    if state:
        new_state = (new_state[0], new_state[1][:, :, ROPE_LANE0:ROPE_LANE0 + QK_ROPE])
    if ctx is not None:
        kc, vc = _ctx_kv(ctx[0], ctx[1], l0, w_v)
        attn = _attention(q, k, v, kc, vc)
    else:
        attn = _attention(q, k, v)
    flat = lambda a: a.reshape(n_b * n_tok, a.shape[-1])
    return _post0(flat(x), flat(attn), flat(conv), mod, l0, n_tok), new_state


def kernel(x_prompt, x_sample, c, cache_c_kv_l0, cache_k_rope_l0, c_ctx, norm1_l0, norm2_l0, w_mod_l0, b_mod_l0, w_in_l0, q_a_norm_l0, w_q_up_l0, q_norm_l0, kv_a_norm_l0, w_kv_up_l0, k_norm_l0, conv_w_l0, w_o_l0, w_router_l0, w_gate_l0, w_up_l0, w_down_l0, norm1_l1, norm2_l1, w_mod_l1, b_mod_l1, w_f_l1, w_router_l1, w_gate_l1, w_up_l1, w_down_l1):
    n_dec = c.shape[0]
    cond = jnp.concatenate([c_ctx[None, :], c, jnp.zeros((16 - 1 - n_dec, D_MODEL), F32)], axis=0)
    m0 = _modulation(cond, w_mod_l0, b_mod_l0)
    m1 = _modulation(cond, w_mod_l1, b_mod_l1)
    mods_prompt = (m0[0:1], m1[0:1])
    mods_sample = (m0[1:1 + n_dec], m1[1:1 + n_dec])

    l0 = _layer0_weights(norm1_l0, norm2_l0, w_in_l0, q_a_norm_l0, w_q_up_l0, q_norm_l0, kv_a_norm_l0,
                         w_kv_up_l0, k_norm_l0, conv_w_l0, w_o_l0)
    l0.update(_router_weights(w_router_l0))
    l0.update(wg=w_gate_l0, wu=w_up_l0, wd=w_down_l0)
    l1 = dict(norm1=norm1_l1.reshape(1, -1), norm2=norm2_l1.reshape(1, -1), w_f=w_f_l1.astype(BF16))
    l1.update(_router_weights(w_router_l1))
    l1.update(wg=w_gate_l1, wu=w_up_l1, wd=w_down_l1)

    tri = jnp.asarray(np.triu(np.ones((ROWS, ROWS)), 1), BF16)
    k_rope_pad = _spare_from_rotary(
        jnp.pad(cache_k_rope_l0, ((0, 0), (0, 0), (ROPE_LANE0, LANES - ROPE_LANE0 - QK_ROPE))))
    xs = (x_prompt, x_sample)
    mods = (mods_prompt, mods_sample)
    n_toks = tuple(x.shape[1] for x in xs)

    (front_p, (new_c_kv, new_k_rope)) = _mixer0(x_prompt, mods_prompt[0], l0, None, None, True)
    (front_s, _) = _mixer0(x_sample, mods_sample[0], l0, (cache_c_kv_l0, k_rope_pad), _rope_tables(n_toks[1]),
                           False)
    ys = _moe([tuple(front) + (mod[0], n_tok) for front, mod, n_tok in zip((front_p, front_s), mods, n_toks)],
              tri, l0)

    sets = []
    for y, x, mod, n_tok in zip(ys, xs, mods, n_toks):
        x1, h2, aff = _fnet(y.reshape(x.shape), mod[1], l1, _dft_tables(n_tok))
        sets.append((x1.reshape(-1, D_MODEL), h2.reshape(-1, XE_W), aff, mod[1], n_tok))
    y_prompt, y_sample = (y.reshape(x.shape) for y, x in zip(_moe(sets, tri, l1), xs))
    return (y_prompt, y_sample, new_c_kv, new_k_rope)
```

```python
import functools

import jax
import jax.numpy as jnp
import numpy as np
from jax import lax
from jax.experimental import pallas as pl
from jax.experimental.pallas import tpu as pltpu

D_MODEL = 1024
GRID_W = 64
N_HEADS = 8
QK_NOPE = 64
QK_ROPE = 32
QK_DIM = QK_NOPE + QK_ROPE
V_HEAD = 64
Q_LORA = 384
KV_LORA = 256
CONV_CH = 512
FNET_GROUPS = 4
FNET_CH = D_MODEL // FNET_GROUPS
N_EXPERTS = 16
EXPERT_FF = 512
CAPACITY_FACTOR = 2
ROPE_THETA = 10000.0
EPS = 1e-6

LANES = 128
HEAD_PAD = LANES
ROWS = 256
WIDE_ROWS = 512
MIX_ROWS = 1024
ATTN_ROWS = 1024
ALIGN = 8
WIN = 64
MAX_UNITS = N_EXPERTS * (ROWS // WIN)
COMMON_ROWS = N_EXPERTS * WIN
STEP_BLOCKS = 2
XE_W = D_MODEL + LANES
IN0_PAD = Q_LORA + KV_LORA + 2 * LANES + 3 * CONV_CH
ROPE_LANE0 = QK_NOPE
VMEM_LIMIT = 56 * 1024 * 1024

F32 = jnp.float32
BF16 = jnp.bfloat16


def _dot(a, b):
    return jnp.dot(a, b, preferred_element_type=F32)


def _dot_nt(a, b):
    return lax.dot_general(a, b, (((1,), (1,)), ((), ())), preferred_element_type=F32)


def _split_hi_lo(x):
    hi = x.astype(BF16)
    lo = (x - hi.astype(F32)).astype(BF16)
    return hi, lo


def _params(n_axes):
    return pltpu.CompilerParams(dimension_semantics=("arbitrary",) * n_axes,
                                vmem_limit_bytes=VMEM_LIMIT)


def _rms(x, gain):
    return x * lax.rsqrt(jnp.mean(x * x, axis=-1, keepdims=True) + EPS) * gain


def _mod_kernel(cond_ref, w_ref, b_ref, o_ref):
    c = cond_ref[...]
    s = c * (1.0 / (1.0 + jnp.exp(-c)))
    s_hi, s_lo = _split_hi_lo(s)
    w = w_ref[...].astype(BF16)
    o_ref[...] = _dot(s_hi, w) + _dot(s_lo, w) + b_ref[...]


def _modulation(cond, w_mod, b_mod):
    n_rows = cond.shape[0]
    tn = 1536
    out = pl.pallas_call(
        _mod_kernel,
        out_shape=jax.ShapeDtypeStruct((n_rows, 6 * D_MODEL), F32),
        grid=(6 * D_MODEL // tn,),
        in_specs=[pl.BlockSpec((n_rows, D_MODEL), lambda i: (0, 0)),
                  pl.BlockSpec((D_MODEL, tn), lambda i: (0, i)),
                  pl.BlockSpec((1, tn), lambda i: (0, i))],
        out_specs=pl.BlockSpec((n_rows, tn), lambda i: (0, i)),
        compiler_params=_params(1),
        name="modulation",
    )(cond, w_mod, b_mod.reshape(1, -1))
    return out.reshape(n_rows, 6, D_MODEL)


def _head_norm_rope(xh, gain_cos, swapped_sin):
    lane = lax.broadcasted_iota(jnp.int32, (1, HEAD_PAD), 1)
    ss = jnp.sum(jnp.where(lane < QK_DIM, xh * xh, 0.0), axis=-1, keepdims=True) * (1.0 / QK_DIM)
    y = xh * gain_cos
    if swapped_sin is not None:
        y = y + swapped_sin
    return y * lax.rsqrt(ss + EPS)


def _expand_kv(ckv_bf, r, swapped_sin, wk_ref, wv_ref, gain_cos, k_ref, v_ref, rows):
    kf = _dot(ckv_bf, wk_ref[...])
    vf = _dot(ckv_bf, wv_ref[...])
    if v_ref.shape[-1] == N_HEADS * HEAD_PAD:
        lane = lax.broadcasted_iota(jnp.int32, (1, N_HEADS * HEAD_PAD), 1)
        vf = vf + jnp.where((lane & (HEAD_PAD - 1)) == V_HEAD, 1.0, 0.0)
    v_ref[rows, :] = vf.astype(BF16)
    for h in range(N_HEADS):
        sl = slice(h * HEAD_PAD, (h + 1) * HEAD_PAD)
        k_ref[rows, sl] = _head_norm_rope(kf[:, sl] + r, gain_cos, swapped_sin).astype(BF16)


def _front0_kernel(*refs, n_tok, rope, state):
    (x_ref, mod_ref, n1_ref, win_ref, qan_ref, wq_ref, kvan_ref, wk_ref, wv_ref, cw_ref,
     qa_ref, ka_ref) = refs[:12]
    pos = 12
    if rope:
        kb_ref = refs[pos]
        pos += 1
    q_ref, k_ref, v_ref, conv_ref = refs[pos:pos + 4]
    if state:
        ckv_ref, kr_ref = refs[pos + 4:pos + 6]
    cu_s, gb_s = refs[-2:]
    sh1, sc1 = mod_ref[0:1, :], mod_ref[1:2, :]
    c_rope = Q_LORA + KV_LORA
    wide = min(n_tok, MIX_ROWS)
    for c in range(n_tok // ROWS):
        rows = pl.ds(c * ROWS, ROWS)
        if (c * ROWS) % wide == 0:
            wide_rows = pl.ds(c * ROWS, wide)
            h = _rms(x_ref[wide_rows, :], n1_ref[...]) * (1.0 + sc1) + sh1
            proj_wide = _dot(h.astype(BF16), win_ref[...])
        off = (c * ROWS) % wide
        proj = proj_wide[off:off + ROWS, :]
        cq = _rms(proj[:, :Q_LORA], qan_ref[...]).astype(BF16)
        qf = _dot(cq, wq_ref[...])
        for hd in range(N_HEADS):
            sl = slice(hd * HEAD_PAD, (hd + 1) * HEAD_PAD)
            q_ref[rows, sl] = _head_norm_rope(qf[:, sl], qa_ref[rows, :], None).astype(BF16)
        ckv = _rms(proj[:, Q_LORA:c_rope], kvan_ref[...])
        r = proj[:, c_rope:c_rope + LANES]
        if state:
            ckv_ref[rows, :] = ckv
            kr_ref[rows, :] = r
        k_sin = proj[:, c_rope + LANES:c_rope + 2 * LANES] * kb_ref[rows, :] if rope else None
        _expand_kv(ckv.astype(BF16), r, k_sin, wk_ref, wv_ref, ka_ref[rows, :], k_ref, v_ref, rows)
        c0 = c_rope + 2 * LANES
        gb_s[rows, :] = proj[:, c0:c0 + CONV_CH]
        cu_s[rows, :] = proj[:, c0 + CONV_CH:c0 + 2 * CONV_CH] * proj[:, c0 + 2 * CONV_CH:c0 + 3 * CONV_CH]
    cu = cu_s[...]
    row = lax.broadcasted_iota(jnp.int32, cu.shape, 0)
    prev = jnp.where(row == 0, 0.0, pltpu.roll(cu, 1, 0))
    nxt = jnp.where(row == n_tok - 1, 0.0, pltpu.roll(cu, n_tok - 1, 0))
    conv = gb_s[...] * (cw_ref[0:1, :] * prev + cw_ref[1:2, :] * cu + cw_ref[2:3, :] * nxt)
    conv_ref[...] = conv.astype(BF16)


def _const_spec(shape):
    return pl.BlockSpec(shape, lambda b: (0,) * len(shape))


def _front0(x, mod, w, rope_tabs, w_v, state):
    n_b, n_tok, _ = x.shape
    v_width = w_v.shape[1]
    shared_mod = mod.shape[0] == 1
    seq = lambda width: pl.BlockSpec((None, n_tok, width), lambda b: (b, 0, 0))
    in_specs = [seq(D_MODEL),
                pl.BlockSpec((None, 6, D_MODEL), (lambda b: (0, 0, 0)) if shared_mod else (lambda b: (b, 0, 0))),
                _const_spec((1, D_MODEL)), _const_spec((D_MODEL, IN0_PAD)), _const_spec((1, Q_LORA)),
                _const_spec((Q_LORA, N_HEADS * HEAD_PAD)),
                _const_spec((1, KV_LORA)), _const_spec((KV_LORA, N_HEADS * HEAD_PAD)),
                _const_spec((KV_LORA, v_width)), _const_spec((3, CONV_CH)),
                _const_spec((n_tok, LANES)), _const_spec((n_tok, LANES))]
    q_tabs = _head_tables(w["q_gain"], rope_tabs, n_tok, True)
    k_tabs = _head_tables(w["k_gain"], rope_tabs, n_tok, False)
    args = [x, mod, w["norm1"], w["w_in"], w["q_a_norm"], w["w_q"], w["kv_a_norm"],
            w["w_k"], w_v, w["conv_w"], q_tabs[0], k_tabs[0]]
    if rope_tabs is not None:
        in_specs += [_const_spec((n_tok, LANES))]
        args += [k_tabs[1]]
    out_shape = [jax.ShapeDtypeStruct((n_b, n_tok, N_HEADS * HEAD_PAD), BF16),
                 jax.ShapeDtypeStruct((n_b, n_tok, N_HEADS * HEAD_PAD), BF16),
                 jax.ShapeDtypeStruct((n_b, n_tok, v_width), BF16),
                 jax.ShapeDtypeStruct((n_b, n_tok, CONV_CH), BF16)]
    out_specs = [seq(N_HEADS * HEAD_PAD), seq(N_HEADS * HEAD_PAD), seq(v_width), seq(CONV_CH)]
    if state:
        out_shape += [jax.ShapeDtypeStruct((n_b, n_tok, KV_LORA), F32),
                      jax.ShapeDtypeStruct((n_b, n_tok, LANES), F32)]
        out_specs += [seq(KV_LORA), seq(LANES)]
    return pl.pallas_call(
        functools.partial(_front0_kernel, n_tok=n_tok, rope=rope_tabs is not None, state=state),
        out_shape=out_shape, grid=(n_b,), in_specs=in_specs, out_specs=out_specs,
        scratch_shapes=[pltpu.VMEM((n_tok, CONV_CH), F32), pltpu.VMEM((n_tok, CONV_CH), F32)],
        compiler_params=_params(1), name="front0",
    )(*args)


def _ctx_kv_kernel(ckv_ref, r_ref, wk_ref, wv_ref, ka_ref, k_ref, v_ref):
    rows = pl.ds(0, ckv_ref.shape[0])
    _expand_kv(ckv_ref[...].astype(BF16), r_ref[...], None, wk_ref, wv_ref, ka_ref[...], k_ref, v_ref, rows)


def _ctx_kv(cache_c_kv, cache_k_rope_pad, w, w_v):
    n_b, n_ctx, _ = cache_c_kv.shape
    v_width = w_v.shape[1]
    seq = lambda width: pl.BlockSpec((None, n_ctx, width), lambda b: (b, 0, 0))
    return pl.pallas_call(
        _ctx_kv_kernel,
        out_shape=[jax.ShapeDtypeStruct((n_b, n_ctx, N_HEADS * HEAD_PAD), BF16),
                   jax.ShapeDtypeStruct((n_b, n_ctx, v_width), BF16)],
        grid=(n_b,),
        in_specs=[seq(KV_LORA), seq(LANES), _const_spec((KV_LORA, N_HEADS * HEAD_PAD)),
                  _const_spec((KV_LORA, v_width)), _const_spec((n_ctx, LANES))],
        out_specs=[seq(N_HEADS * HEAD_PAD), seq(v_width)],
        compiler_params=_params(1), name="ctx_kv",
    )(cache_c_kv, cache_k_rope_pad, w["w_k"], w_v, _head_tables(w["k_gain"], None, n_ctx, False)[0])


def _attn_kernel(*refs, with_ctx):
    if with_ctx:
        q_ref, k_ref, v_ref, kc_ref, vc_ref, o_ref = refs
    else:
        q_ref, k_ref, v_ref, o_ref = refs
    ones_lane = v_ref.shape[-1] == N_HEADS * HEAD_PAD
    lane = lax.broadcasted_iota(jnp.int32, (q_ref.shape[0], LANES), 1)
    for pair in range(N_HEADS // 2):
        outs = []
        for hd in (2 * pair, 2 * pair + 1):
            sl = slice(hd * HEAD_PAD, (hd + 1) * HEAD_PAD)
            vsl = sl if ones_lane else slice(pair * LANES, (pair + 1) * LANES)
            qh = q_ref[:, sl]
            s = _dot_nt(qh, k_ref[:, sl])
            m = jnp.max(s, axis=-1, keepdims=True)
            if with_ctx:
                sc = _dot_nt(qh, kc_ref[:, sl])
                m = jnp.maximum(m, jnp.max(sc, axis=-1, keepdims=True))
            p = jnp.exp(s - m)
            o = _dot(p.astype(BF16), v_ref[:, vsl])
            if with_ctx:
                pc = jnp.exp(sc - m)
                o = o + _dot(pc.astype(BF16), vc_ref[:, vsl])
            if ones_lane:
                den = o[:, V_HEAD:V_HEAD + 1]
            else:
                den = jnp.sum(p, axis=-1, keepdims=True)
                if with_ctx:
                    den = den + jnp.sum(pc, axis=-1, keepdims=True)
            outs.append(o / den)
        odd = pltpu.roll(outs[1], V_HEAD, 1) if ones_lane else outs[1]
        o_ref[:, pair * LANES:(pair + 1) * LANES] = jnp.where(lane < V_HEAD, outs[0], odd).astype(BF16)


def _attention(q, k, v, kc=None, vc=None):
    n_b, n_tok, _ = q.shape
    with_ctx = kc is not None
    tq = min(n_tok, ATTN_ROWS)
    qspec = lambda width: pl.BlockSpec((None, tq, width), lambda b, i: (b, i, 0))
    kvspec = lambda a: pl.BlockSpec((None,) + a.shape[1:], lambda b, i: (b, 0, 0))
    in_specs = [qspec(N_HEADS * HEAD_PAD), kvspec(k), kvspec(v)]
    args = [q, k, v]
    if with_ctx:
        in_specs += [kvspec(kc), kvspec(vc)]
        args += [kc, vc]
    return pl.pallas_call(
        functools.partial(_attn_kernel, with_ctx=with_ctx),
        out_shape=jax.ShapeDtypeStruct((n_b, n_tok, N_HEADS * V_HEAD), BF16),
        grid=(n_b, n_tok // tq), in_specs=in_specs, out_specs=qspec(N_HEADS * V_HEAD),
        compiler_params=_params(2), name="attention",
    )(*args)


def _moe_front(x1, mod_ref, n2_ref, wrh_ref, wrl_ref, h2_ref, aff_ref, rows):
    sh2, sc2 = mod_ref[3:4, :], mod_ref[4:5, :]
    h2 = _rms(x1, n2_ref[...]) * (1.0 + sc2) + sh2
    h_hi, h_lo = _split_hi_lo(h2)
    logit = _dot(h_hi, wrh_ref[...]) + _dot(h_lo, wrh_ref[...]) + _dot(h_hi, wrl_ref[...])
    real = lax.broadcasted_iota(jnp.int32, (1, LANES), 1) < N_EXPERTS
    top = jnp.max(jnp.where(real, logit, -jnp.inf), axis=1, keepdims=True)
    e = jnp.where(real, jnp.exp(logit - top), 0.0)
    a = e / jnp.sum(e, axis=1, keepdims=True)
    a_hi = a.astype(BF16).astype(F32)
    a_mid = (a - a_hi).astype(BF16).astype(F32)
    a_lo = a - a_hi - a_mid
    h2_ref[rows, 0:D_MODEL] = h_hi
    h2_ref[rows, D_MODEL:XE_W] = (a_hi + pltpu.roll(a_mid, N_EXPERTS, 1)
                                  + pltpu.roll(a_lo, 2 * N_EXPERTS, 1)).astype(BF16)
    aff_ref[...] = a.T[0:N_EXPERTS, :]


def _post0_kernel(x_ref, attn_ref, conv_ref, wo_ref, mod_ref, n2_ref, wrh_ref, wrl_ref,
                  x1_ref, h2_ref, aff_ref):
    n_attn = N_HEADS * V_HEAD
    mix = _dot(attn_ref[...], wo_ref[0:n_attn, :]) + _dot(conv_ref[...], wo_ref[n_attn:, :])
    for t in range(x_ref.shape[0] // ROWS):
        rows = pl.ds(t * ROWS, ROWS)
        x1 = x_ref[rows, :] + mod_ref[2:3, :] * mix[t * ROWS:(t + 1) * ROWS, :]
        x1_ref[rows, :] = x1
        _moe_front(x1, mod_ref, n2_ref, wrh_ref, wrl_ref, h2_ref, aff_ref.at[t], rows)


def _post0(x, attn, conv, mod, w, n_tok):
    n_all = x.shape[0]
    shared_mod = mod.shape[0] == 1
    rows = MIX_ROWS if shared_mod else min(n_tok, MIX_ROWS)
    tiles_per_seq = max(n_tok // rows, 1)
    tile = lambda width: pl.BlockSpec((rows, width), lambda i: (i, 0))
    const = lambda shape: pl.BlockSpec(shape, lambda i: (0,) * len(shape))
    mod_map = (lambda i: (0, 0, 0)) if shared_mod else (lambda i: (i // tiles_per_seq, 0, 0))
    return pl.pallas_call(
        _post0_kernel,
        out_shape=[jax.ShapeDtypeStruct((n_all, D_MODEL), F32),
                   jax.ShapeDtypeStruct((n_all, XE_W), BF16),
                   jax.ShapeDtypeStruct((n_all // ROWS, N_EXPERTS, ROWS), F32)],
        grid=(n_all // rows,),
        in_specs=[tile(D_MODEL), tile(N_HEADS * V_HEAD), tile(CONV_CH), const((D_MODEL, D_MODEL)),
                  pl.BlockSpec((None, 6, D_MODEL), mod_map), const((1, D_MODEL)),
                  const((D_MODEL, LANES)), const((D_MODEL, LANES))],
        out_specs=[tile(D_MODEL), tile(XE_W),
                   pl.BlockSpec((rows // ROWS, N_EXPERTS, ROWS), lambda i: (i, 0, 0))],
        compiler_params=_params(1), name="post0",
    )(x, attn, conv, w["w_o"], mod, w["norm2"], w["wr_hi"], w["wr_lo"])


def _fnet_kernel(x_ref, mod_ref, n1_ref, cc_ref, sc_ref, dft_ref, flip_ref, wf_ref, n2_ref, wrh_ref, wrl_ref,
                 x1_ref, h2_ref, aff_ref, h_s, r_s, y_s, *, n_tok):
    sh1, sc1, g1 = mod_ref[0:1, :], mod_ref[1:2, :], mod_ref[2:3, :]
    n_chunks = n_tok // ROWS
    half = n_tok // 2
    blk = flip_ref.shape[0]
    cc, sc = cc_ref[...].astype(BF16), sc_ref[...].astype(BF16)
    for c in range(n_chunks):
        rows = pl.ds(c * ROWS, ROWS)
        h_s[rows, :] = (_rms(x_ref[rows, :], n1_ref[...]) * (1.0 + sc1) + sh1).astype(BF16)
    n_flip = half // blk
    for i in range(n_flip):
        src = h_s[pl.ds(half + (n_flip - 1 - i) * blk, blk), :]
        r_s[pl.ds(i * blk, blk), :] = _dot(flip_ref[...], src)
    mirror = pltpu.roll(r_s[...], 1, 0)
    row = lax.broadcasted_iota(jnp.int32, mirror.shape, 0)
    mirror = jnp.where(row == 0, 0.0, mirror)
    low = h_s[0:half, :].astype(F32)
    even, odd = (low + mirror).astype(BF16), (low - mirror).astype(BF16)
    mid = h_s[pl.ds(half, 16), :]
    mids = []
    for g in range(FNET_GROUPS):
        sl = slice(g * FNET_CH, (g + 1) * FNET_CH)
        y_s[0:half, sl] = _dot(even[:, sl], cc).astype(BF16)
        y_s[half:n_tok, sl] = _dot(odd[:, sl], sc).astype(BF16)
        mids.append(_dot(mid[:, sl], cc)[0:1, :])
    mid_row = jnp.concatenate(mids, axis=1)
    scale = 1.0 / float(np.sqrt(n_tok * FNET_CH))
    wide = min(n_tok, MIX_ROWS)
    for c in range(n_chunks):
        rows = pl.ds(c * ROWS, ROWS)
        if (c * ROWS) % wide == 0:
            parity = lax.broadcasted_iota(jnp.int32, (wide, 1), 0) & 1
            f = _dot(dft_ref[pl.ds(c * ROWS, wide), :].astype(BF16), y_s[...])
            f = (f + jnp.where(parity == 0, 1.0, -1.0) * mid_row) * scale
            mix = _dot(f.astype(BF16), wf_ref[...])
        off = (c * ROWS) % wide
        x1 = x_ref[rows, :] + g1 * mix[off:off + ROWS, :]
        x1_ref[rows, :] = x1
        _moe_front(x1, mod_ref, n2_ref, wrh_ref, wrl_ref, h2_ref, aff_ref.at[c], rows)


def _fnet(x, mod, w, dft):
    n_b, n_tok, _ = x.shape
    shared_mod = mod.shape[0] == 1
    seq = lambda width: pl.BlockSpec((None, n_tok, width), lambda b: (b, 0, 0))
    cc, sc, dft_n, flip = dft
    tiles = n_tok // ROWS
    return pl.pallas_call(
        functools.partial(_fnet_kernel, n_tok=n_tok),
        out_shape=[jax.ShapeDtypeStruct((n_b, n_tok, D_MODEL), F32),
                   jax.ShapeDtypeStruct((n_b, n_tok, XE_W), BF16),
                   jax.ShapeDtypeStruct((n_b * tiles, N_EXPERTS, ROWS), F32)],
        grid=(n_b,),
        in_specs=[seq(D_MODEL),
                  pl.BlockSpec((None, 6, D_MODEL), (lambda b: (0, 0, 0)) if shared_mod else (lambda b: (b, 0, 0))),
                  _const_spec((1, D_MODEL)), _const_spec((FNET_CH, FNET_CH)), _const_spec((FNET_CH, FNET_CH)),
                  pl.BlockSpec((n_tok, n_tok), lambda b: (0, 0), pipeline_mode=pl.Buffered(1)),
                  _const_spec(flip.shape), _const_spec((D_MODEL, D_MODEL)), _const_spec((1, D_MODEL)),
                  _const_spec((D_MODEL, LANES)), _const_spec((D_MODEL, LANES))],
        out_specs=[seq(D_MODEL), seq(XE_W), pl.BlockSpec((tiles, N_EXPERTS, ROWS), lambda b: (b, 0, 0))],
        scratch_shapes=[pltpu.VMEM((n_tok, D_MODEL), BF16), pltpu.VMEM((n_tok // 2, D_MODEL), F32),
                        pltpu.VMEM((n_tok, D_MODEL), BF16)],
        compiler_params=_params(1), name="fnet",
    )(x, mod, w["norm1"], cc, sc, dft_n, flip, w["w_f"], w["norm2"], w["wr_hi"], w["wr_lo"])


def _route_kernel(aff_ref, tri_ref, posm_ref, s_ref, *, n_blk, cap):
    def count(pred):
        acc = jnp.zeros((N_EXPERTS, ROWS), F32)
        for b in range(n_blk):
            acc = acc + jnp.where(pred(aff_ref[b]), 1.0, 0.0)
        return jnp.sum(acc, axis=1, keepdims=True)

    def as_f32(bits):
        return pltpu.bitcast(bits, F32)

    def search(_, carry):
        lo, hi = carry
        mid = lo + ((hi - lo) >> 1)
        mid_f = as_f32(mid)
        ok = count(lambda a: a >= mid_f) >= cap
        return jnp.where(ok, mid, lo), jnp.where(ok, hi, mid)

    one_bits = 0x3F800000
    lo0 = jnp.zeros((N_EXPERTS, 1), jnp.int32)
    hi0 = jnp.full((N_EXPERTS, 1), one_bits + 1, jnp.int32)
    lo, hi = lax.fori_loop(0, 31, search, (lo0, hi0))
    lo_f, ub = as_f32(lo), as_f32(hi)
    thr = lo_f
    pending = jnp.ones((N_EXPERTS, 1), F32)
    for _ in range(3):
        cur = jnp.full((N_EXPERTS, ROWS), -1.0, F32)
        for b in range(n_blk):
            a = aff_ref[b]
            cur = jnp.maximum(cur, jnp.where((a >= lo_f) & (a < ub), a, -1.0))
        cur = jnp.max(cur, axis=1, keepdims=True)
        take = (count(lambda a: a >= cur) >= cap) & (pending > 0.0)
        thr = jnp.where(take, cur, thr)
        pending = jnp.where(take, 0.0, pending)
        ub = cur
    need = cap - count(lambda a: a > thr)

    lane = lax.broadcasted_iota(jnp.int32, (N_EXPERTS, LANES), 1)
    carry_tie = jnp.zeros((N_EXPERTS, 1), F32)
    carry_pos = jnp.zeros((N_EXPERTS, 1), F32)
    s_acc = jnp.zeros((N_EXPERTS, LANES), jnp.int32)
    for b in range(n_blk):
        v = aff_ref[b]
        eq = jnp.where(v == thr, 1.0, 0.0)
        tie_rank = _dot(eq.astype(BF16), tri_ref[...]) + carry_tie
        sel = jnp.where((v > thr) | ((v == thr) & (tie_rank < need)), 1.0, 0.0)
        pos = _dot(sel.astype(BF16), tri_ref[...]) + carry_pos
        posm_ref[b] = jnp.where(sel > 0.0, pos, -1.0)
        s_acc = jnp.where(lane == b, carry_pos.astype(jnp.int32), s_acc)
        carry_tie = carry_tie + jnp.sum(eq, axis=1, keepdims=True)
        n_sel = jnp.sum(sel, axis=1, keepdims=True)
        carry_pos = carry_pos + jnp.floor((n_sel + (ALIGN - 1)) * (1.0 / ALIGN)) * ALIGN
    s_ref[...] = jnp.where(lane == n_blk, carry_pos.astype(jnp.int32), s_acc)


def _route(aff, tri, cap):
    n_blk = aff.shape[0]
    assert n_blk < LANES
    full = lambda shape: pl.BlockSpec(shape, lambda i: (0,) * len(shape))
    posm, s_tab = pl.pallas_call(
        functools.partial(_route_kernel, n_blk=n_blk, cap=cap),
        out_shape=[jax.ShapeDtypeStruct((n_blk, N_EXPERTS, ROWS), F32),
                   jax.ShapeDtypeStruct((N_EXPERTS, LANES), jnp.int32)],
        grid=(1,),
        in_specs=[full((n_blk, N_EXPERTS, ROWS)), full((ROWS, ROWS))],
        out_specs=[full((n_blk, N_EXPERTS, ROWS)), full((N_EXPERTS, LANES))],
        compiler_params=_params(1), name="route",
    )(aff, tri)
    return posm, s_tab[:, :n_blk + 1]


def _ceil_div_pow2(x, d):
    return lax.shift_right_logical(x + (d - 1), d.bit_length() - 1)


def _block_units(s_ref, b):
    base = [s_ref[e, b] for e in range(N_EXPERTS)]
    units = [_ceil_div_pow2(s_ref[e, b + 1] - base[e], WIN) for e in range(N_EXPERTS)]
    first_unit, total = [], 0
    for e in range(N_EXPERTS):
        first_unit.append(total)
        total = total + units[e]
    return base, units, first_unit, total


def _for_units(units, fn):
    for e in range(N_EXPERTS):
        def body(u, _, e=e):
            fn(e, u)
            return 0
        lax.fori_loop(0, units[e], body, 0)


def _window_onehot(posm_ref, e, first, lo):
    slot = lax.broadcasted_iota(jnp.int32, (WIN, ROWS), 0) + first
    p = posm_ref[e:e + 1, :].astype(jnp.int32)
    return jnp.where((p == slot) & (p >= lo), 1.0, 0.0)


def _gate_lanes(e):
    lane = lax.broadcasted_iota(jnp.int32, (1, LANES), 1)
    return ((lane & (N_EXPERTS - 1)) == e) & (lane < 3 * N_EXPERTS)


def _dispatch_kernel(s_ref, h_ref, posm_ref, xe_ref, sel_s, x_s, sem, *, n_blk, cap_pad, common_rows):
    for t in range(STEP_BLOCKS):
        _dispatch_block(pl.program_id(0) * STEP_BLOCKS + t, t, s_ref, h_ref.at[pl.ds(t * ROWS, ROWS), :],
                        posm_ref.at[t], xe_ref, sel_s, x_s, sem, n_blk, cap_pad, common_rows)


def _dispatch_block(b, buf, s_ref, h_ref, posm_ref, xe_ref, sel_s, x_s, sem, n_blk, cap_pad, common_rows):
    base, units, first_unit, n_units = _block_units(s_ref, b)

    @pl.when(b == 0)
    def _():
        sel_s[...] = jnp.zeros_like(sel_s)

    def unit_rows(first_unit_e, u):
        return pl.ds(pl.multiple_of((first_unit_e + u) * WIN, WIN), WIN)

    def pick(e, u):
        first = base[e] + u * WIN
        sel_s[unit_rows(first_unit[e], u), :] = _window_onehot(posm_ref, e, first, first).astype(BF16)

    _for_units(units, pick)

    def gather(rows):
        x_s[buf, rows, :] = _dot(sel_s[rows, :], h_ref[...]).astype(BF16)

    def gather_chunk(c, _):
        gather(pl.ds(pl.multiple_of(c * ROWS, ROWS), ROWS))
        return 0

    gather(pl.ds(0, common_rows))
    lax.fori_loop(common_rows // ROWS, _ceil_div_pow2(n_units * WIN, ROWS), gather_chunk, 0)

    def unit_copy(slot, blk_base_e, blk_first_e, e, u):
        dst = pl.ds(pl.multiple_of(blk_base_e + u * WIN, ALIGN), WIN)
        return pltpu.make_async_copy(x_s.at[slot, unit_rows(blk_first_e, u), :], xe_ref.at[e, dst, :],
                                     sem.at[slot, e])

    def for_expert_units(n, fn):
        lax.fori_loop(0, n, lambda u, _: (fn(u), 0)[1], 0)

    prev_base, prev_units, prev_first, _ = _block_units(s_ref, jnp.maximum(b - 1, 0))
    for e in range(N_EXPERTS):
        for_expert_units(jnp.where(b > 0, prev_units[e], 0),
                         lambda u, e=e: unit_copy(1 - buf, prev_base[e], prev_first[e], e, u).wait())
        for_expert_units(units[e], lambda u, e=e: unit_copy(buf, base[e], first_unit[e], e, u).start())

    @pl.when(b == n_blk - 1)
    def _():
        for e in range(N_EXPERTS):
            for_expert_units(units[e], lambda u, e=e: unit_copy(buf, base[e], first_unit[e], e, u).wait())
        x_s[buf, 0:WIN, :] = jnp.zeros((WIN, XE_W), BF16)

        def tail(fn):
            for e in range(N_EXPERTS):
                total = s_ref[e, n_blk]
                n_win = (cap_pad - total) // WIN

                def wide(c, _):
                    row = pl.multiple_of(total + c * WIN, ALIGN)
                    fn(pltpu.make_async_copy(x_s.at[buf, pl.ds(0, WIN), :],
                                             xe_ref.at[e, pl.ds(row, WIN), :], sem.at[buf, e]))
                    return 0

                def narrow(c, _):
                    row = pl.multiple_of(total + n_win * WIN + c * ALIGN, ALIGN)
                    fn(pltpu.make_async_copy(x_s.at[buf, pl.ds(0, ALIGN), :],
                                             xe_ref.at[e, pl.ds(row, ALIGN), :], sem.at[buf, e]))
                    return 0

                lax.fori_loop(0, n_win, wide, 0)
                lax.fori_loop(0, (cap_pad - total - n_win * WIN) // ALIGN, narrow, 0)

        tail(lambda c: c.start())
        tail(lambda c: c.wait())


def _ffn_kernel(*refs, n_blks):
    n_sets = len(n_blks)
    s_refs, x_refs = refs[:n_sets], refs[n_sets:2 * n_sets]
    wg_ref, wu_ref, wd_ref = refs[2 * n_sets:2 * n_sets + 3]
    y_refs = refs[2 * n_sets + 3:3 * n_sets + 3]
    wg_s, wu_s, wd_s = refs[3 * n_sets + 3:]
    e = pl.program_id(0)
    wg_s[...] = wg_ref[...].astype(BF16)
    wu_s[...] = wu_ref[...].astype(BF16)
    wd_s[...] = wd_ref[...].astype(BF16)
    mine = _gate_lanes(e)

    half = ROWS // 2
    for s_ref, x_ref, y_ref, n_blk in zip(s_refs, x_refs, y_refs, n_blks):
        n_half = _ceil_div_pow2(s_ref[e, n_blk], half)

        def tile(rows, x_ref=x_ref, y_ref=y_ref):
            x = x_ref[rows, 0:D_MODEL]
            pieces = x_ref[rows, D_MODEL:XE_W].astype(F32)
            gate = jnp.sum(jnp.where(mine, pieces, 0.0), axis=1, keepdims=True)
            a = _dot(x, wg_s[...])
            u = _dot(x, wu_s[...])
            hid = a * (1.0 / (1.0 + jnp.exp(-a))) * u * gate
            y_ref[rows, :] = _dot(hid.astype(BF16), wd_s[...]).astype(BF16)

        def full(j, _, tile=tile):
            tile(pl.ds(pl.multiple_of(j * WIDE_ROWS, WIDE_ROWS), WIDE_ROWS))
            return 0

        def blank(j, _, y_ref=y_ref):
            y_ref[pl.ds(pl.multiple_of(j * half, half), half), :] = jnp.zeros((half, D_MODEL), BF16)
            return 0

        lax.fori_loop(0, lax.shift_right_logical(n_half, 2), full, 0)

        @pl.when((n_half & 2) != 0)
        def _(tile=tile, n_half=n_half):
            tile(pl.ds(pl.multiple_of((n_half & -4) * half, WIDE_ROWS), ROWS))

        @pl.when((n_half & 1) != 0)
        def _(tile=tile, n_half=n_half):
            tile(pl.ds(pl.multiple_of((n_half - 1) * half, half), half))

        lax.fori_loop(n_half, y_ref.shape[0] // half, blank, 0)


def _combine_kernel(s_ref, y_ref, posm_ref, x1_ref, mod_ref, o_ref, yw_s, sel_s, acc_s, sem,
                    *, n_blk, cap_pad, common_rows):
    for t in range(STEP_BLOCKS):
        rows = pl.ds(t * ROWS, ROWS)
        _combine_block(pl.program_id(0) * STEP_BLOCKS + t, t, s_ref, y_ref, posm_ref.at[t], x1_ref.at[rows, :],
                       mod_ref, o_ref.at[rows, :], yw_s, sel_s, acc_s, sem, n_blk, cap_pad, common_rows)


def _combine_block(b, buf, s_ref, y_ref, posm_ref, x1_ref, mod_ref, o_ref, yw_s, sel_s, acc_s, sem, n_blk, cap_pad,
                   common_rows):
    base, units, first_unit, n_units = _block_units(s_ref, b)

    def unit_rows(first_unit_e, u):
        return pl.ds(pl.multiple_of((first_unit_e + u) * WIN, WIN), WIN)

    def bounds(base_e, u):
        lo = base_e + u * WIN
        return lo, jnp.minimum(lo, cap_pad - WIN)

    def copies(slot, blk, fn):
        blk_base, blk_units, blk_first, _ = _block_units(s_ref, blk)

        def one(e, u):
            src = pl.ds(pl.multiple_of(bounds(blk_base[e], u)[1], ALIGN), WIN)
            fn(pltpu.make_async_copy(y_ref.at[e, src, :], yw_s.at[slot, unit_rows(blk_first[e], u), :],
                                     sem.at[slot]))

        _for_units(blk_units, one)

    @pl.when(b == 0)
    def _():
        copies(0, 0, lambda c: c.start())

    @pl.when(b + 1 < n_blk)
    def _():
        copies(1 - buf, b + 1, lambda c: c.start())

    def pick(e, u):
        lo, first = bounds(base[e], u)
        sel_s[unit_rows(first_unit[e], u), :] = _window_onehot(posm_ref, e, first, lo).astype(BF16)

    _for_units(units, pick)
    n_chunks = jnp.maximum(_ceil_div_pow2(n_units * WIN, ROWS), common_rows // ROWS)

    def blank(u, _):
        sel_s[unit_rows(u, 0), :] = jnp.zeros((WIN, ROWS), BF16)
        yw_s[buf, unit_rows(u, 0), :] = jnp.zeros((WIN, D_MODEL), BF16)
        return 0

    copies(buf, b, lambda c: c.wait())
    lax.fori_loop(n_units, n_chunks * (ROWS // WIN), blank, 0)

    def weighted(rows):
        return _dot(sel_s[rows, :].T, yw_s[buf, rows, :])

    def add_chunk(c, _):
        acc_s[...] += weighted(pl.ds(pl.multiple_of(c * ROWS, ROWS), ROWS))
        return 0

    acc_s[...] = weighted(pl.ds(0, common_rows))
    lax.fori_loop(common_rows // ROWS, n_chunks, add_chunk, 0)
    o_ref[...] = x1_ref[...] + mod_ref[5:6, :] * acc_s[...]


def _padded_capacity(cap, n_blk):
    return -(-(cap + (ALIGN - 1) * n_blk + WIN) // ROWS) * ROWS


def _step_spec(*shape):
    return pl.BlockSpec((STEP_BLOCKS,) + shape, lambda i, s: (i,) + (0,) * len(shape))


def _common_rows(n_tok):
    return COMMON_ROWS + (ROWS if n_tok > ROWS else 0)


def _dispatch(h2, posm, s_tab, cap, n_tok):
    n_blk = h2.shape[0] // ROWS
    cap_pad = _padded_capacity(cap, n_blk)
    return pl.pallas_call(
        functools.partial(_dispatch_kernel, n_blk=n_blk, cap_pad=cap_pad, common_rows=_common_rows(n_tok)),
        out_shape=jax.ShapeDtypeStruct((N_EXPERTS, cap_pad, XE_W), BF16),
        grid_spec=pltpu.PrefetchScalarGridSpec(
            num_scalar_prefetch=1, grid=(n_blk // STEP_BLOCKS,),
            in_specs=[pl.BlockSpec((STEP_BLOCKS * ROWS, XE_W), lambda i, s: (i, 0)), _step_spec(N_EXPERTS, ROWS)],
            out_specs=pl.BlockSpec(memory_space=pl.ANY),
            scratch_shapes=[pltpu.VMEM((MAX_UNITS * WIN, ROWS), BF16), pltpu.VMEM((2, MAX_UNITS * WIN, XE_W), BF16),
                            pltpu.SemaphoreType.DMA((2, N_EXPERTS))]),
        compiler_params=_params(1), name="dispatch",
    )(s_tab, h2, posm)


def _ffn(xes, s_tabs, wg, wu, wd):
    n_sets = len(xes)
    per_expert = lambda rows, cols: pl.BlockSpec((None, rows, cols), lambda e, *s: (e, 0, 0))
    return pl.pallas_call(
        functools.partial(_ffn_kernel, n_blks=tuple(s.shape[1] - 1 for s in s_tabs)),
        out_shape=[jax.ShapeDtypeStruct(xe.shape[:2] + (D_MODEL,), BF16) for xe in xes],
        grid_spec=pltpu.PrefetchScalarGridSpec(
            num_scalar_prefetch=n_sets, grid=(N_EXPERTS,),
            in_specs=[per_expert(xe.shape[1], XE_W) for xe in xes]
            + [per_expert(D_MODEL, EXPERT_FF), per_expert(D_MODEL, EXPERT_FF), per_expert(EXPERT_FF, D_MODEL)],
            out_specs=[per_expert(xe.shape[1], D_MODEL) for xe in xes],
            scratch_shapes=[pltpu.VMEM((D_MODEL, EXPERT_FF), BF16), pltpu.VMEM((D_MODEL, EXPERT_FF), BF16),
                            pltpu.VMEM((EXPERT_FF, D_MODEL), BF16)]),
        compiler_params=_params(1), name="ffn",
    )(*s_tabs, *xes, wg, wu, wd)


def _combine(y, posm, s_tab, x1, mod, n_tok):
    n_all = x1.shape[0]
    n_blk = n_all // ROWS
    shared_mod = mod.shape[0] == 1
    assert shared_mod or n_tok % (STEP_BLOCKS * ROWS) == 0
    steps_per_seq = max(n_tok // (STEP_BLOCKS * ROWS), 1)
    mod_map = (lambda i, s: (0, 0, 0)) if shared_mod else (lambda i, s: (i // steps_per_seq, 0, 0))
    rows = pl.BlockSpec((STEP_BLOCKS * ROWS, D_MODEL), lambda i, s: (i, 0))
    return pl.pallas_call(
        functools.partial(_combine_kernel, n_blk=n_blk, cap_pad=y.shape[1], common_rows=_common_rows(n_tok)),
        out_shape=jax.ShapeDtypeStruct((n_all, D_MODEL), F32),
        grid_spec=pltpu.PrefetchScalarGridSpec(
            num_scalar_prefetch=1, grid=(n_blk // STEP_BLOCKS,),
            in_specs=[pl.BlockSpec(memory_space=pl.ANY), _step_spec(N_EXPERTS, ROWS), rows,
                      pl.BlockSpec((None, 6, D_MODEL), mod_map)],
            out_specs=rows,
            scratch_shapes=[pltpu.VMEM((2, MAX_UNITS * WIN, D_MODEL), BF16), pltpu.VMEM((MAX_UNITS * WIN, ROWS), BF16),
                            pltpu.VMEM((ROWS, D_MODEL), F32), pltpu.SemaphoreType.DMA((2,))]),
        compiler_params=_params(1), name="combine",
    )(s_tab, y, posm, x1, mod)


def _rope_tables(n_tok):
    rows = n_tok // GRID_W
    row = np.repeat(np.arange(rows, dtype=np.float64), GRID_W)
    col = np.tile(np.arange(GRID_W, dtype=np.float64), rows)
    axis_dim = QK_ROPE // 2
    inv_freq = ROPE_THETA ** (-np.arange(0, axis_dim, 2, dtype=np.float64) / axis_dim)
    ang = np.concatenate([row[:, None] * inv_freq, col[:, None] * inv_freq], axis=-1)
    cos = np.ones((n_tok, LANES))
    sin = np.zeros((n_tok, LANES))
    cos[:, ROPE_LANE0:ROPE_LANE0 + QK_ROPE] = np.repeat(np.cos(ang), 2, axis=1)
    sgn = np.tile(np.array([-1.0, 1.0]), QK_ROPE // 2)
    sin[:, ROPE_LANE0:ROPE_LANE0 + QK_ROPE] = np.repeat(np.sin(ang), 2, axis=1) * sgn
    return jnp.asarray(cos, F32), jnp.asarray(sin, F32)


def _dft_tables(n_tok):
    def cs(n):
        k = np.arange(n)
        ang = 2.0 * np.pi * ((k[:, None] * k[None, :]) % n) / n
        return np.cos(ang), np.sin(ang)
    cc, sc = cs(FNET_CH)
    cn, sn = cs(n_tok)
    half = n_tok // 2
    folded = np.concatenate([cn[:, :half], -sn[:, :half]], axis=1)
    flip = np.fliplr(np.eye(min(ROWS, half)))
    return tuple(jnp.asarray(t, F32) for t in (cc, sc, folded)) + (jnp.asarray(flip, BF16),)


def _pad_heads(w, width):
    lead = w.shape[:-1]
    w = w.reshape(lead + (N_HEADS, width))
    w = jnp.pad(w, [(0, 0)] * len(lead) + [(0, 0), (0, HEAD_PAD - width)])
    return w.reshape(lead + (N_HEADS * HEAD_PAD,))


def _pair_swap_lanes(n_groups):
    perm = np.arange(n_groups * LANES).reshape(n_groups, LANES)
    rot = perm[:, ROPE_LANE0:ROPE_LANE0 + QK_ROPE].reshape(n_groups, QK_ROPE // 2, 2)[:, :, ::-1]
    perm[:, ROPE_LANE0:ROPE_LANE0 + QK_ROPE] = rot.reshape(n_groups, QK_ROPE)
    return perm.reshape(-1)


def _spare_from_rotary(t, src=None):
    src = t if src is None else src
    lead = t.shape[:-1]
    t, src = t.reshape(lead + (-1, LANES)), src.reshape(lead + (-1, LANES))
    t = jnp.concatenate([t[..., :QK_DIM], src[..., ROPE_LANE0:ROPE_LANE0 + QK_ROPE]], axis=-1)
    return t.reshape(lead + (-1,))


def _head_tables(gain, rope_tabs, n_tok, query):
    if rope_tabs is None:
        tab = jnp.broadcast_to(gain, (n_tok, LANES))
        return (tab if query else _spare_from_rotary(tab),)
    cos, sin = rope_tabs
    a, b = gain * cos, gain[:, _pair_swap_lanes(1)] * sin
    return (_spare_from_rotary(a, b),) if query else (_spare_from_rotary(a), _spare_from_rotary(b))


def _layer0_weights(norm1, norm2, w_in, q_a_norm, w_q_up, q_norm, kv_a_norm, w_kv_up, k_norm, conv_w, w_o):
    c0 = Q_LORA + KV_LORA
    rope_cols = jnp.pad(w_in[:, c0:c0 + QK_ROPE], ((0, 0), (ROPE_LANE0, LANES - ROPE_LANE0 - QK_ROPE)))
    w_in_pad = jnp.concatenate([w_in[:, :c0], _spare_from_rotary(rope_cols),
                                _spare_from_rotary(rope_cols[:, _pair_swap_lanes(1)]),
                                w_in[:, c0 + QK_ROPE:]], axis=1)
    kv = w_kv_up.reshape(KV_LORA, N_HEADS, QK_NOPE + V_HEAD)
    w_q = _pad_heads(w_q_up, QK_DIM)
    w_q = _spare_from_rotary(w_q, w_q[:, _pair_swap_lanes(N_HEADS)])
    head_gain = lambda g: jnp.pad(g, (0, HEAD_PAD - QK_DIM)).reshape(1, -1)
    return dict(
        norm1=norm1.reshape(1, -1), norm2=norm2.reshape(1, -1), w_in=w_in_pad.astype(BF16),
        q_a_norm=q_a_norm.reshape(1, -1), w_q=w_q.astype(BF16),
        q_gain=head_gain(q_norm) * (QK_DIM ** -0.5), kv_a_norm=kv_a_norm.reshape(1, -1),
        w_k=_pad_heads(kv[:, :, :QK_NOPE].reshape(KV_LORA, -1), QK_NOPE).astype(BF16),
        w_v=kv[:, :, QK_NOPE:].reshape(KV_LORA, -1).astype(BF16),
        w_v_wide=_pad_heads(kv[:, :, QK_NOPE:].reshape(KV_LORA, -1), V_HEAD).astype(BF16), k_gain=head_gain(k_norm),
        conv_w=conv_w, w_o=w_o.astype(BF16))


def _router_weights(w_router):
    hi, lo = _split_hi_lo(jnp.pad(w_router, ((0, 0), (0, LANES - N_EXPERTS))))
    return dict(wr_hi=hi, wr_lo=lo)


def _moe(sets, tri, w):
    routed = []
    for x1, h2, aff, mod, n_tok in sets:
        cap = CAPACITY_FACTOR * x1.shape[0] // N_EXPERTS
        posm, s_tab = _route(aff, tri, cap)
        routed.append((posm, s_tab, _dispatch(h2, posm, s_tab, cap, n_tok)))
    ys = _ffn([r[2] for r in routed], [r[1] for r in routed], w["wg"], w["wu"], w["wd"])
    return [_combine(y, posm, s_tab, x1, mod, n_tok)
            for y, (posm, s_tab, _), (x1, _, _, mod, n_tok) in zip(ys, routed, sets)]


def _mixer0(x, mod, l0, ctx, rope_tabs, state):
    n_b, n_tok, _ = x.shape
    n_keys = n_tok + (0 if ctx is None else ctx[0].shape[1])
    w_v = l0["w_v_wide"] if n_keys > WIDE_ROWS else l0["w_v"]
    q, k, v, conv, *new_state = _front0(x, mod, l0, rope_tabs, w_v, state)
    if state:
        new_state = (new_state[0], new_state[1][:, :, ROPE_LANE0:ROPE_LANE0 + QK_ROPE])
    if ctx is not None:
        kc, vc = _ctx_kv(ctx[0], ctx[1], l0, w_v)
        attn = _attention(q, k, v, kc, vc)
    else:
        attn = _attention(q, k, v)
    flat = lambda a: a.reshape(n_b * n_tok, a.shape[-1])
    return _post0(flat(x), flat(attn), flat(conv), mod, l0, n_tok), new_state


def kernel(x_prompt, x_sample, c, cache_c_kv_l0, cache_k_rope_l0, c_ctx, norm1_l0, norm2_l0, w_mod_l0, b_mod_l0, w_in_l0, q_a_norm_l0, w_q_up_l0, q_norm_l0, kv_a_norm_l0, w_kv_up_l0, k_norm_l0, conv_w_l0, w_o_l0, w_router_l0, w_gate_l0, w_up_l0, w_down_l0, norm1_l1, norm2_l1, w_mod_l1, b_mod_l1, w_f_l1, w_router_l1, w_gate_l1, w_up_l1, w_down_l1):
    n_dec = c.shape[0]
    cond = jnp.concatenate([c_ctx[None, :], c, jnp.zeros((16 - 1 - n_dec, D_MODEL), F32)], axis=0)
    m0 = _modulation(cond, w_mod_l0, b_mod_l0)
    m1 = _modulation(cond, w_mod_l1, b_mod_l1)
    mods_prompt = (m0[0:1], m1[0:1])
    mods_sample = (m0[1:1 + n_dec], m1[1:1 + n_dec])

    l0 = _layer0_weights(norm1_l0, norm2_l0, w_in_l0, q_a_norm_l0, w_q_up_l0, q_norm_l0, kv_a_norm_l0,
                         w_kv_up_l0, k_norm_l0, conv_w_l0, w_o_l0)
    l0.update(_router_weights(w_router_l0))
    l0.update(wg=w_gate_l0, wu=w_up_l0, wd=w_down_l0)
    l1 = dict(norm1=norm1_l1.reshape(1, -1), norm2=norm2_l1.reshape(1, -1), w_f=w_f_l1.astype(BF16))
    l1.update(_router_weights(w_router_l1))
    l1.update(wg=w_gate_l1, wu=w_up_l1, wd=w_down_l1)

    tri = jnp.asarray(np.triu(np.ones((ROWS, ROWS)), 1), BF16)
    k_rope_pad = _spare_from_rotary(
        jnp.pad(cache_k_rope_l0, ((0, 0), (0, 0), (ROPE_LANE0, LANES - ROPE_LANE0 - QK_ROPE))))
    xs = (x_prompt, x_sample)
    mods = (mods_prompt, mods_sample)
    n_toks = tuple(x.shape[1] for x in xs)

    (front_p, (new_c_kv, new_k_rope)) = _mixer0(x_prompt, mods_prompt[0], l0, None, None, True)
    (front_s, _) = _mixer0(x_sample, mods_sample[0], l0, (cache_c_kv_l0, k_rope_pad), _rope_tables(n_toks[1]),
                           False)
    ys = _moe([tuple(front) + (mod[0], n_tok) for front, mod, n_tok in zip((front_p, front_s), mods, n_toks)],
              tri, l0)

    sets = []
    for y, x, mod, n_tok in zip(ys, xs, mods, n_toks):
        x1, h2, aff = _fnet(y.reshape(x.shape), mod[1], l1, _dft_tables(n_tok))
        sets.append((x1.reshape(-1, D_MODEL), h2.reshape(-1, XE_W), aff, mod[1], n_tok))
    y_prompt, y_sample = (y.reshape(x.shape) for y, x in zip(_moe(sets, tri, l1), xs))
    return (y_prompt, y_sample, new_c_kv, new_k_rope)
```

```python
import functools

import jax
import jax.numpy as jnp
import numpy as np
from jax import lax
from jax.experimental import pallas as pl
from jax.experimental.pallas import tpu as pltpu

D_MODEL = 1024
GRID_W = 64
N_HEADS = 8
QK_NOPE = 64
QK_ROPE = 32
QK_DIM = QK_NOPE + QK_ROPE
V_HEAD = 64
Q_LORA = 384
KV_LORA = 256
CONV_CH = 512
FNET_GROUPS = 4
FNET_CH = D_MODEL // FNET_GROUPS
N_EXPERTS = 16
EXPERT_FF = 512
CAPACITY_FACTOR = 2
ROPE_THETA = 10000.0
EPS = 1e-6

LANES = 128
HEAD_PAD = LANES
ROWS = 256
WIDE_ROWS = 512
MIX_ROWS = 1024
ATTN_ROWS = 1024
ALIGN = 8
WIN = 64
MAX_UNITS = N_EXPERTS * (ROWS // WIN)
COMMON_ROWS = N_EXPERTS * WIN
STEP_BLOCKS = 2
XE_W = D_MODEL + LANES
IN0_PAD = Q_LORA + KV_LORA + 2 * LANES + 3 * CONV_CH
ROPE_LANE0 = QK_NOPE
VMEM_LIMIT = 56 * 1024 * 1024

F32 = jnp.float32
BF16 = jnp.bfloat16


def _dot(a, b):
    return jnp.dot(a, b, preferred_element_type=F32)


def _dot_nt(a, b):
    return lax.dot_general(a, b, (((1,), (1,)), ((), ())), preferred_element_type=F32)


def _split_hi_lo(x):
    hi = x.astype(BF16)
    lo = (x - hi.astype(F32)).astype(BF16)
    return hi, lo


def _params(n_axes):
    return pltpu.CompilerParams(dimension_semantics=("arbitrary",) * n_axes,
                                vmem_limit_bytes=VMEM_LIMIT)


def _rms(x, gain):
    return x * lax.rsqrt(jnp.mean(x * x, axis=-1, keepdims=True) + EPS) * gain


def _mod_kernel(cond_ref, w_ref, b_ref, o_ref):
    c = cond_ref[...]
    s = c * (1.0 / (1.0 + jnp.exp(-c)))
    s_hi, s_lo = _split_hi_lo(s)
    w = w_ref[...].astype(BF16)
    o_ref[...] = _dot(s_hi, w) + _dot(s_lo, w) + b_ref[...]


def _modulation(cond, w_mod, b_mod):
    n_rows = cond.shape[0]
    tn = 1536
    out = pl.pallas_call(
        _mod_kernel,
        out_shape=jax.ShapeDtypeStruct((n_rows, 6 * D_MODEL), F32),
        grid=(6 * D_MODEL // tn,),
        in_specs=[pl.BlockSpec((n_rows, D_MODEL), lambda i: (0, 0)),
                  pl.BlockSpec((D_MODEL, tn), lambda i: (0, i)),
                  pl.BlockSpec((1, tn), lambda i: (0, i))],
        out_specs=pl.BlockSpec((n_rows, tn), lambda i: (0, i)),
        compiler_params=_params(1),
        name="modulation",
    )(cond, w_mod, b_mod.reshape(1, -1))
    return out.reshape(n_rows, 6, D_MODEL)


def _head_norm_rope(xh, gain_cos, swapped_sin):
    lane = lax.broadcasted_iota(jnp.int32, (1, HEAD_PAD), 1)
    ss = jnp.sum(jnp.where(lane < QK_DIM, xh * xh, 0.0), axis=-1, keepdims=True) * (1.0 / QK_DIM)
    y = xh * gain_cos
    if swapped_sin is not None:
        y = y + swapped_sin
    return y * lax.rsqrt(ss + EPS)


def _expand_kv(ckv_bf, r, swapped_sin, wk_ref, wv_ref, gain_cos, k_ref, v_ref, rows):
    kf = _dot(ckv_bf, wk_ref[...])
    vf = _dot(ckv_bf, wv_ref[...])
    if v_ref.shape[-1] == N_HEADS * HEAD_PAD:
        lane = lax.broadcasted_iota(jnp.int32, (1, N_HEADS * HEAD_PAD), 1)
        vf = vf + jnp.where((lane & (HEAD_PAD - 1)) == V_HEAD, 1.0, 0.0)
    v_ref[rows, :] = vf.astype(BF16)
    for h in range(N_HEADS):
        sl = slice(h * HEAD_PAD, (h + 1) * HEAD_PAD)
        k_ref[rows, sl] = _head_norm_rope(kf[:, sl] + r, gain_cos, swapped_sin).astype(BF16)


def _front0_kernel(*refs, n_tok, rope, state):
    (x_ref, mod_ref, n1_ref, win_ref, qan_ref, wq_ref, kvan_ref, wk_ref, wv_ref, cw_ref,
     qa_ref, ka_ref) = refs[:12]
    pos = 12
    if rope:
        kb_ref = refs[pos]
        pos += 1
    q_ref, k_ref, v_ref, conv_ref = refs[pos:pos + 4]
    if state:
        ckv_ref, kr_ref = refs[pos + 4:pos + 6]
    cu_s, gb_s = refs[-2:]
    sh1, sc1 = mod_ref[0:1, :], mod_ref[1:2, :]
    c_rope = Q_LORA + KV_LORA
    wide = min(n_tok, MIX_ROWS)
    for c in range(n_tok // ROWS):
        rows = pl.ds(c * ROWS, ROWS)
        if (c * ROWS) % wide == 0:
            wide_rows = pl.ds(c * ROWS, wide)
            h = _rms(x_ref[wide_rows, :], n1_ref[...]) * (1.0 + sc1) + sh1
            proj_wide = _dot(h.astype(BF16), win_ref[...])
        off = (c * ROWS) % wide
        proj = proj_wide[off:off + ROWS, :]
        cq = _rms(proj[:, :Q_LORA], qan_ref[...]).astype(BF16)
        qf = _dot(cq, wq_ref[...])
        for hd in range(N_HEADS):
            sl = slice(hd * HEAD_PAD, (hd + 1) * HEAD_PAD)
            q_ref[rows, sl] = _head_norm_rope(qf[:, sl], qa_ref[rows, :], None).astype(BF16)
        ckv = _rms(proj[:, Q_LORA:c_rope], kvan_ref[...])
        r = proj[:, c_rope:c_rope + LANES]
        if state:
            ckv_ref[rows, :] = ckv
            kr_ref[rows, :] = r
        k_sin = proj[:, c_rope + LANES:c_rope + 2 * LANES] * kb_ref[rows, :] if rope else None
        _expand_kv(ckv.astype(BF16), r, k_sin, wk_ref, wv_ref, ka_ref[rows, :], k_ref, v_ref, rows)
        c0 = c_rope + 2 * LANES
        gb_s[rows, :] = proj[:, c0:c0 + CONV_CH]
        cu_s[rows, :] = proj[:, c0 + CONV_CH:c0 + 2 * CONV_CH] * proj[:, c0 + 2 * CONV_CH:c0 + 3 * CONV_CH]
    cu = cu_s[...]
    row = lax.broadcasted_iota(jnp.int32, cu.shape, 0)
    prev = jnp.where(row == 0, 0.0, pltpu.roll(cu, 1, 0))
    nxt = jnp.where(row == n_tok - 1, 0.0, pltpu.roll(cu, n_tok - 1, 0))
    conv = gb_s[...] * (cw_ref[0:1, :] * prev + cw_ref[1:2, :] * cu + cw_ref[2:3, :] * nxt)
    conv_ref[...] = conv.astype(BF16)


def _const_spec(shape):
    return pl.BlockSpec(shape, lambda b: (0,) * len(shape))


def _front0(x, mod, w, rope_tabs, w_v, state):
    n_b, n_tok, _ = x.shape
    v_width = w_v.shape[1]
    shared_mod = mod.shape[0] == 1
    seq = lambda width: pl.BlockSpec((None, n_tok, width), lambda b: (b, 0, 0))
    in_specs = [seq(D_MODEL),
                pl.BlockSpec((None, 6, D_MODEL), (lambda b: (0, 0, 0)) if shared_mod else (lambda b: (b, 0, 0))),
                _const_spec((1, D_MODEL)), _const_spec((D_MODEL, IN0_PAD)), _const_spec((1, Q_LORA)),
                _const_spec((Q_LORA, N_HEADS * HEAD_PAD)),
                _const_spec((1, KV_LORA)), _const_spec((KV_LORA, N_HEADS * HEAD_PAD)),
                _const_spec((KV_LORA, v_width)), _const_spec((3, CONV_CH)),
                _const_spec((n_tok, LANES)), _const_spec((n_tok, LANES))]
    q_tabs = _head_tables(w["q_gain"], rope_tabs, n_tok, True)
    k_tabs = _head_tables(w["k_gain"], rope_tabs, n_tok, False)
    args = [x, mod, w["norm1"], w["w_in"], w["q_a_norm"], w["w_q"], w["kv_a_norm"],
            w["w_k"], w_v, w["conv_w"], q_tabs[0], k_tabs[0]]
    if rope_tabs is not None:
        in_specs += [_const_spec((n_tok, LANES))]
        args += [k_tabs[1]]
    out_shape = [jax.ShapeDtypeStruct((n_b, n_tok, N_HEADS * HEAD_PAD), BF16),
                 jax.ShapeDtypeStruct((n_b, n_tok, N_HEADS * HEAD_PAD), BF16),
                 jax.ShapeDtypeStruct((n_b, n_tok, v_width), BF16),
                 jax.ShapeDtypeStruct((n_b, n_tok, CONV_CH), BF16)]
    out_specs = [seq(N_HEADS * HEAD_PAD), seq(N_HEADS * HEAD_PAD), seq(v_width), seq(CONV_CH)]
    if state:
        out_shape += [jax.ShapeDtypeStruct((n_b, n_tok, KV_LORA), F32),
                      jax.ShapeDtypeStruct((n_b, n_tok, LANES), F32)]
        out_specs += [seq(KV_LORA), seq(LANES)]
    return pl.pallas_call(
        functools.partial(_front0_kernel, n_tok=n_tok, rope=rope_tabs is not None, state=state),
        out_shape=out_shape, grid=(n_b,), in_specs=in_specs, out_specs=out_specs,
        scratch_shapes=[pltpu.VMEM((n_tok, CONV_CH), F32), pltpu.VMEM((n_tok, CONV_CH), F32)],
        compiler_params=_params(1), name="front0",
    )(*args)


def _ctx_kv_kernel(ckv_ref, r_ref, wk_ref, wv_ref, ka_ref, k_ref, v_ref):
    rows = pl.ds(0, ckv_ref.shape[0])
    _expand_kv(ckv_ref[...].astype(BF16), r_ref[...], None, wk_ref, wv_ref, ka_ref[...], k_ref, v_ref, rows)


def _ctx_kv(cache_c_kv, cache_k_rope_pad, w, w_v):
    n_b, n_ctx, _ = cache_c_kv.shape
    v_width = w_v.shape[1]
    seq = lambda width: pl.BlockSpec((None, n_ctx, width), lambda b: (b, 0, 0))
    return pl.pallas_call(
        _ctx_kv_kernel,
        out_shape=[jax.ShapeDtypeStruct((n_b, n_ctx, N_HEADS * HEAD_PAD), BF16),
                   jax.ShapeDtypeStruct((n_b, n_ctx, v_width), BF16)],
        grid=(n_b,),
        in_specs=[seq(KV_LORA), seq(LANES), _const_spec((KV_LORA, N_HEADS * HEAD_PAD)),
                  _const_spec((KV_LORA, v_width)), _const_spec((n_ctx, LANES))],
        out_specs=[seq(N_HEADS * HEAD_PAD), seq(v_width)],
        compiler_params=_params(1), name="ctx_kv",
    )(cache_c_kv, cache_k_rope_pad, w["w_k"], w_v, _head_tables(w["k_gain"], None, n_ctx, False)[0])


def _attn_kernel(*refs, with_ctx):
    if with_ctx:
        q_ref, k_ref, v_ref, kc_ref, vc_ref, o_ref = refs
    else:
        q_ref, k_ref, v_ref, o_ref = refs
    ones_lane = v_ref.shape[-1] == N_HEADS * HEAD_PAD
    lane = lax.broadcasted_iota(jnp.int32, (q_ref.shape[0], LANES), 1)
    for pair in range(N_HEADS // 2):
        outs = []
        for hd in (2 * pair, 2 * pair + 1):
            sl = slice(hd * HEAD_PAD, (hd + 1) * HEAD_PAD)
            vsl = sl if ones_lane else slice(pair * LANES, (pair + 1) * LANES)
            qh = q_ref[:, sl]
            s = _dot_nt(qh, k_ref[:, sl])
            m = jnp.max(s, axis=-1, keepdims=True)
            if with_ctx:
                sc = _dot_nt(qh, kc_ref[:, sl])
                m = jnp.maximum(m, jnp.max(sc, axis=-1, keepdims=True))
            p = jnp.exp(s - m)
            o = _dot(p.astype(BF16), v_ref[:, vsl])
            if with_ctx:
                pc = jnp.exp(sc - m)
                o = o + _dot(pc.astype(BF16), vc_ref[:, vsl])
            if ones_lane:
                den = o[:, V_HEAD:V_HEAD + 1]
            else:
                den = jnp.sum(p, axis=-1, keepdims=True)
                if with_ctx:
                    den = den + jnp.sum(pc, axis=-1, keepdims=True)
            outs.append(o / den)
        odd = pltpu.roll(outs[1], V_HEAD, 1) if ones_lane else outs[1]
        o_ref[:, pair * LANES:(pair + 1) * LANES] = jnp.where(lane < V_HEAD, outs[0], odd).astype(BF16)


def _attention(q, k, v, kc=None, vc=None):
    n_b, n_tok, _ = q.shape
    with_ctx = kc is not None
    tq = min(n_tok, ATTN_ROWS)
    qspec = lambda width: pl.BlockSpec((None, tq, width), lambda b, i: (b, i, 0))
    kvspec = lambda a: pl.BlockSpec((None,) + a.shape[1:], lambda b, i: (b, 0, 0))
    in_specs = [qspec(N_HEADS * HEAD_PAD), kvspec(k), kvspec(v)]
    args = [q, k, v]
    if with_ctx:
        in_specs += [kvspec(kc), kvspec(vc)]
        args += [kc, vc]
    return pl.pallas_call(
        functools.partial(_attn_kernel, with_ctx=with_ctx),
        out_shape=jax.ShapeDtypeStruct((n_b, n_tok, N_HEADS * V_HEAD), BF16),
        grid=(n_b, n_tok // tq), in_specs=in_specs, out_specs=qspec(N_HEADS * V_HEAD),
        compiler_params=_params(2), name="attention",
    )(*args)


def _moe_front(x1, mod_ref, n2_ref, wrh_ref, wrl_ref, h2_ref, aff_ref, rows):
    sh2, sc2 = mod_ref[3:4, :], mod_ref[4:5, :]
    h2 = _rms(x1, n2_ref[...]) * (1.0 + sc2) + sh2
    h_hi, h_lo = _split_hi_lo(h2)
    logit = _dot(h_hi, wrh_ref[...]) + _dot(h_lo, wrh_ref[...]) + _dot(h_hi, wrl_ref[...])
    real = lax.broadcasted_iota(jnp.int32, (1, LANES), 1) < N_EXPERTS
    top = jnp.max(jnp.where(real, logit, -jnp.inf), axis=1, keepdims=True)
    e = jnp.where(real, jnp.exp(logit - top), 0.0)
    a = e / jnp.sum(e, axis=1, keepdims=True)
    a_hi = a.astype(BF16).astype(F32)
    a_mid = (a - a_hi).astype(BF16).astype(F32)
    a_lo = a - a_hi - a_mid
    h2_ref[rows, 0:D_MODEL] = h_hi
    h2_ref[rows, D_MODEL:XE_W] = (a_hi + pltpu.roll(a_mid, N_EXPERTS, 1)
                                  + pltpu.roll(a_lo, 2 * N_EXPERTS, 1)).astype(BF16)
    aff_ref[...] = a.T[0:N_EXPERTS, :]


def _post0_kernel(x_ref, attn_ref, conv_ref, wo_ref, mod_ref, n2_ref, wrh_ref, wrl_ref,
                  x1_ref, h2_ref, aff_ref):
    n_attn = N_HEADS * V_HEAD
    mix = _dot(attn_ref[...], wo_ref[0:n_attn, :]) + _dot(conv_ref[...], wo_ref[n_attn:, :])
    for t in range(x_ref.shape[0] // ROWS):
        rows = pl.ds(t * ROWS, ROWS)
        x1 = x_ref[rows, :] + mod_ref[2:3, :] * mix[t * ROWS:(t + 1) * ROWS, :]
        x1_ref[rows, :] = x1
        _moe_front(x1, mod_ref, n2_ref, wrh_ref, wrl_ref, h2_ref, aff_ref.at[t], rows)


def _post0(x, attn, conv, mod, w, n_tok):
    n_all = x.shape[0]
    shared_mod = mod.shape[0] == 1
    rows = MIX_ROWS if shared_mod else min(n_tok, MIX_ROWS)
    tiles_per_seq = max(n_tok // rows, 1)
    tile = lambda width: pl.BlockSpec((rows, width), lambda i: (i, 0))
    const = lambda shape: pl.BlockSpec(shape, lambda i: (0,) * len(shape))
    mod_map = (lambda i: (0, 0, 0)) if shared_mod else (lambda i: (i // tiles_per_seq, 0, 0))
    return pl.pallas_call(
        _post0_kernel,
        out_shape=[jax.ShapeDtypeStruct((n_all, D_MODEL), F32),
                   jax.ShapeDtypeStruct((n_all, XE_W), BF16),
                   jax.ShapeDtypeStruct((n_all // ROWS, N_EXPERTS, ROWS), F32)],
        grid=(n_all // rows,),
        in_specs=[tile(D_MODEL), tile(N_HEADS * V_HEAD), tile(CONV_CH), const((D_MODEL, D_MODEL)),
                  pl.BlockSpec((None, 6, D_MODEL), mod_map), const((1, D_MODEL)),
                  const((D_MODEL, LANES)), const((D_MODEL, LANES))],
        out_specs=[tile(D_MODEL), tile(XE_W),
                   pl.BlockSpec((rows // ROWS, N_EXPERTS, ROWS), lambda i: (i, 0, 0))],
        compiler_params=_params(1), name="post0",
    )(x, attn, conv, w["w_o"], mod, w["norm2"], w["wr_hi"], w["wr_lo"])


def _fnet_kernel(x_ref, mod_ref, n1_ref, cc_ref, sc_ref, dft_ref, flip_ref, wf_ref, n2_ref, wrh_ref, wrl_ref,
                 x1_ref, h2_ref, aff_ref, h_s, r_s, y_s, *, n_tok):
    sh1, sc1, g1 = mod_ref[0:1, :], mod_ref[1:2, :], mod_ref[2:3, :]
    n_chunks = n_tok // ROWS
    half = n_tok // 2
    blk = flip_ref.shape[0]
    cc, sc = cc_ref[...].astype(BF16), sc_ref[...].astype(BF16)
    for c in range(n_chunks):
        rows = pl.ds(c * ROWS, ROWS)
        h_s[rows, :] = (_rms(x_ref[rows, :], n1_ref[...]) * (1.0 + sc1) + sh1).astype(BF16)
    n_flip = half // blk if n_tok > ROWS else 0
    for i in range(n_flip):
        src = h_s[pl.ds(half + (n_flip - 1 - i) * blk, blk), :]
        r_s[pl.ds(i * blk, blk), :] = _dot(flip_ref[...], src)
    if n_flip:
        mirror = pltpu.roll(r_s[...], 1, 0)
        row = lax.broadcasted_iota(jnp.int32, mirror.shape, 0)
        mirror = jnp.where(row == 0, 0.0, mirror)
        low = h_s[0:half, :].astype(F32)
        even, odd = (low + mirror).astype(BF16), (low - mirror).astype(BF16)
        mid = h_s[pl.ds(half, 16), :]
    else:
        even = odd = h_s[...]
    fold_rows = even.shape[0]
    mids = []
    for g in range(FNET_GROUPS):
        sl = slice(g * FNET_CH, (g + 1) * FNET_CH)
        y_s[0:fold_rows, sl] = _dot(even[:, sl], cc).astype(BF16)
        y_s[fold_rows:2 * fold_rows, sl] = _dot(odd[:, sl], sc).astype(BF16)
        if n_flip:
            mids.append(_dot(mid[:, sl], cc)[0:1, :])
    mid_row = jnp.concatenate(mids, axis=1) if n_flip else None
    scale = 1.0 / float(np.sqrt(n_tok * FNET_CH))
    wide = min(n_tok, MIX_ROWS)
    for c in range(n_chunks):
        rows = pl.ds(c * ROWS, ROWS)
        if (c * ROWS) % wide == 0:
            parity = lax.broadcasted_iota(jnp.int32, (wide, 1), 0) & 1
            f = _dot(dft_ref[pl.ds(c * ROWS, wide), :].astype(BF16), y_s[...])
            if n_flip:
                f = f + jnp.where(parity == 0, 1.0, -1.0) * mid_row
            f = f * scale
            mix = _dot(f.astype(BF16), wf_ref[...])
        off = (c * ROWS) % wide
        x1 = x_ref[rows, :] + g1 * mix[off:off + ROWS, :]
        x1_ref[rows, :] = x1
        _moe_front(x1, mod_ref, n2_ref, wrh_ref, wrl_ref, h2_ref, aff_ref.at[c], rows)


def _fnet(x, mod, w, dft):
    n_b, n_tok, _ = x.shape
    shared_mod = mod.shape[0] == 1
    seq = lambda width: pl.BlockSpec((None, n_tok, width), lambda b: (b, 0, 0))
    cc, sc, dft_n, flip = dft
    tiles = n_tok // ROWS
    return pl.pallas_call(
        functools.partial(_fnet_kernel, n_tok=n_tok),
        out_shape=[jax.ShapeDtypeStruct((n_b, n_tok, D_MODEL), F32),
                   jax.ShapeDtypeStruct((n_b, n_tok, XE_W), BF16),
                   jax.ShapeDtypeStruct((n_b * tiles, N_EXPERTS, ROWS), F32)],
        grid=(n_b,),
        in_specs=[seq(D_MODEL),
                  pl.BlockSpec((None, 6, D_MODEL), (lambda b: (0, 0, 0)) if shared_mod else (lambda b: (b, 0, 0))),
                  _const_spec((1, D_MODEL)), _const_spec((FNET_CH, FNET_CH)), _const_spec((FNET_CH, FNET_CH)),
                  pl.BlockSpec(dft_n.shape, lambda b: (0, 0), pipeline_mode=pl.Buffered(1)),
                  _const_spec(flip.shape), _const_spec((D_MODEL, D_MODEL)), _const_spec((1, D_MODEL)),
                  _const_spec((D_MODEL, LANES)), _const_spec((D_MODEL, LANES))],
        out_specs=[seq(D_MODEL), seq(XE_W), pl.BlockSpec((tiles, N_EXPERTS, ROWS), lambda b: (b, 0, 0))],
        scratch_shapes=[pltpu.VMEM((n_tok, D_MODEL), BF16), pltpu.VMEM((n_tok // 2, D_MODEL), F32),
                        pltpu.VMEM((dft_n.shape[1], D_MODEL), BF16)],
        compiler_params=_params(1), name="fnet",
    )(x, mod, w["norm1"], cc, sc, dft_n, flip, w["w_f"], w["norm2"], w["wr_hi"], w["wr_lo"])


def _route_kernel(aff_ref, tri_ref, posm_ref, s_ref, *, n_blk, cap):
    def count(pred):
        acc = jnp.zeros((N_EXPERTS, ROWS), F32)
        for b in range(n_blk):
            acc = acc + jnp.where(pred(aff_ref[b]), 1.0, 0.0)
        return jnp.sum(acc, axis=1, keepdims=True)

    def as_f32(bits):
        return pltpu.bitcast(bits, F32)

    def search(_, carry):
        lo, hi = carry
        mid = lo + ((hi - lo) >> 1)
        mid_f = as_f32(mid)
        ok = count(lambda a: a >= mid_f) >= cap
        return jnp.where(ok, mid, lo), jnp.where(ok, hi, mid)

    one_bits = 0x3F800000
    lo0 = jnp.zeros((N_EXPERTS, 1), jnp.int32)
    hi0 = jnp.full((N_EXPERTS, 1), one_bits + 1, jnp.int32)
    lo, hi = lax.fori_loop(0, 31, search, (lo0, hi0))
    lo_f, ub = as_f32(lo), as_f32(hi)
    thr = lo_f
    pending = jnp.ones((N_EXPERTS, 1), F32)
    for _ in range(3):
        cur = jnp.full((N_EXPERTS, ROWS), -1.0, F32)
        for b in range(n_blk):
            a = aff_ref[b]
            cur = jnp.maximum(cur, jnp.where((a >= lo_f) & (a < ub), a, -1.0))
        cur = jnp.max(cur, axis=1, keepdims=True)
        take = (count(lambda a: a >= cur) >= cap) & (pending > 0.0)
        thr = jnp.where(take, cur, thr)
        pending = jnp.where(take, 0.0, pending)
        ub = cur
    need = cap - count(lambda a: a > thr)

    lane = lax.broadcasted_iota(jnp.int32, (N_EXPERTS, LANES), 1)
    carry_tie = jnp.zeros((N_EXPERTS, 1), F32)
    carry_pos = jnp.zeros((N_EXPERTS, 1), F32)
    s_acc = jnp.zeros((N_EXPERTS, LANES), jnp.int32)
    for b in range(n_blk):
        v = aff_ref[b]
        eq = jnp.where(v == thr, 1.0, 0.0)
        tie_rank = _dot(eq.astype(BF16), tri_ref[...]) + carry_tie
        sel = jnp.where((v > thr) | ((v == thr) & (tie_rank < need)), 1.0, 0.0)
        pos = _dot(sel.astype(BF16), tri_ref[...]) + carry_pos
        posm_ref[b] = jnp.where(sel > 0.0, pos, -1.0)
        s_acc = jnp.where(lane == b, carry_pos.astype(jnp.int32), s_acc)
        carry_tie = carry_tie + jnp.sum(eq, axis=1, keepdims=True)
        n_sel = jnp.sum(sel, axis=1, keepdims=True)
        carry_pos = carry_pos + jnp.floor((n_sel + (ALIGN - 1)) * (1.0 / ALIGN)) * ALIGN
    s_ref[...] = jnp.where(lane == n_blk, carry_pos.astype(jnp.int32), s_acc)


def _route(aff, tri, cap):
    n_blk = aff.shape[0]
    assert n_blk < LANES
    full = lambda shape: pl.BlockSpec(shape, lambda i: (0,) * len(shape))
    posm, s_tab = pl.pallas_call(
        functools.partial(_route_kernel, n_blk=n_blk, cap=cap),
        out_shape=[jax.ShapeDtypeStruct((n_blk, N_EXPERTS, ROWS), F32),
                   jax.ShapeDtypeStruct((N_EXPERTS, LANES), jnp.int32)],
        grid=(1,),
        in_specs=[full((n_blk, N_EXPERTS, ROWS)), full((ROWS, ROWS))],
        out_specs=[full((n_blk, N_EXPERTS, ROWS)), full((N_EXPERTS, LANES))],
        compiler_params=_params(1), name="route",
    )(aff, tri)
    return posm, s_tab[:, :n_blk + 1]


def _ceil_div_pow2(x, d):
    return lax.shift_right_logical(x + (d - 1), d.bit_length() - 1)


def _block_units(s_ref, b):
    base = [s_ref[e, b] for e in range(N_EXPERTS)]
    units = [_ceil_div_pow2(s_ref[e, b + 1] - base[e], WIN) for e in range(N_EXPERTS)]
    first_unit, total = [], 0
    for e in range(N_EXPERTS):
        first_unit.append(total)
        total = total + units[e]
    return base, units, first_unit, total


def _for_units(units, fn):
    for e in range(N_EXPERTS):
        def body(u, _, e=e):
            fn(e, u)
            return 0
        lax.fori_loop(0, units[e], body, 0)


def _window_onehot(posm_ref, e, first, lo):
    slot = lax.broadcasted_iota(jnp.int32, (WIN, ROWS), 0) + first
    p = posm_ref[e:e + 1, :].astype(jnp.int32)
    return jnp.where((p == slot) & (p >= lo), 1.0, 0.0)


def _gate_lanes(e):
    lane = lax.broadcasted_iota(jnp.int32, (1, LANES), 1)
    return ((lane & (N_EXPERTS - 1)) == e) & (lane < 3 * N_EXPERTS)


def _dispatch_kernel(s_ref, h_ref, posm_ref, xe_ref, sel_s, x_s, sem, *, n_blk, cap_pad, common_rows):
    for t in range(STEP_BLOCKS):
        _dispatch_block(pl.program_id(0) * STEP_BLOCKS + t, t, s_ref, h_ref.at[pl.ds(t * ROWS, ROWS), :],
                        posm_ref.at[t], xe_ref, sel_s, x_s, sem, n_blk, cap_pad, common_rows)


def _dispatch_block(b, buf, s_ref, h_ref, posm_ref, xe_ref, sel_s, x_s, sem, n_blk, cap_pad, common_rows):
    base, units, first_unit, n_units = _block_units(s_ref, b)

    @pl.when(b == 0)
    def _():
        sel_s[...] = jnp.zeros_like(sel_s)

    def unit_rows(first_unit_e, u):
        return pl.ds(pl.multiple_of((first_unit_e + u) * WIN, WIN), WIN)

    def pick(e, u):
        first = base[e] + u * WIN
        sel_s[unit_rows(first_unit[e], u), :] = _window_onehot(posm_ref, e, first, first).astype(BF16)

    _for_units(units, pick)

    def gather(rows):
        x_s[buf, rows, :] = _dot(sel_s[rows, :], h_ref[...]).astype(BF16)

    def gather_chunk(c, _):
        gather(pl.ds(pl.multiple_of(c * ROWS, ROWS), ROWS))
        return 0

    gather(pl.ds(0, common_rows))
    lax.fori_loop(common_rows // ROWS, _ceil_div_pow2(n_units * WIN, ROWS), gather_chunk, 0)

    def unit_copy(slot, blk_base_e, blk_first_e, e, u):
        dst = pl.ds(pl.multiple_of(blk_base_e + u * WIN, ALIGN), WIN)
        return pltpu.make_async_copy(x_s.at[slot, unit_rows(blk_first_e, u), :], xe_ref.at[e, dst, :],
                                     sem.at[slot, e])

    def for_expert_units(n, fn):
        lax.fori_loop(0, n, lambda u, _: (fn(u), 0)[1], 0)

    prev_base, prev_units, prev_first, _ = _block_units(s_ref, jnp.maximum(b - 1, 0))
    for e in range(N_EXPERTS):
        for_expert_units(jnp.where(b > 0, prev_units[e], 0),
                         lambda u, e=e: unit_copy(1 - buf, prev_base[e], prev_first[e], e, u).wait())
        for_expert_units(units[e], lambda u, e=e: unit_copy(buf, base[e], first_unit[e], e, u).start())

    @pl.when(b == n_blk - 1)
    def _():
        for e in range(N_EXPERTS):
            for_expert_units(units[e], lambda u, e=e: unit_copy(buf, base[e], first_unit[e], e, u).wait())
        x_s[buf, 0:WIN, :] = jnp.zeros((WIN, XE_W), BF16)

        def tail(fn):
            for e in range(N_EXPERTS):
                total = s_ref[e, n_blk]
                n_win = (cap_pad - total) // WIN

                def wide(c, _):
                    row = pl.multiple_of(total + c * WIN, ALIGN)
                    fn(pltpu.make_async_copy(x_s.at[buf, pl.ds(0, WIN), :],
                                             xe_ref.at[e, pl.ds(row, WIN), :], sem.at[buf, e]))
                    return 0

                def narrow(c, _):
                    row = pl.multiple_of(total + n_win * WIN + c * ALIGN, ALIGN)
                    fn(pltpu.make_async_copy(x_s.at[buf, pl.ds(0, ALIGN), :],
                                             xe_ref.at[e, pl.ds(row, ALIGN), :], sem.at[buf, e]))
                    return 0

                lax.fori_loop(0, n_win, wide, 0)
                lax.fori_loop(0, (cap_pad - total - n_win * WIN) // ALIGN, narrow, 0)

        tail(lambda c: c.start())
        tail(lambda c: c.wait())


def _ffn_kernel(*refs, n_blks):
    n_sets = len(n_blks)
    s_refs, x_refs = refs[:n_sets], refs[n_sets:2 * n_sets]
    wg_ref, wu_ref, wd_ref = refs[2 * n_sets:2 * n_sets + 3]
    y_refs = refs[2 * n_sets + 3:3 * n_sets + 3]
    wg_s, wu_s, wd_s = refs[3 * n_sets + 3:]
    e = pl.program_id(0)
    wg_s[...] = wg_ref[...].astype(BF16)
    wu_s[...] = wu_ref[...].astype(BF16)
    wd_s[...] = wd_ref[...].astype(BF16)
    mine = _gate_lanes(e)

    half = ROWS // 2
    for s_ref, x_ref, y_ref, n_blk in zip(s_refs, x_refs, y_refs, n_blks):
        n_half = _ceil_div_pow2(s_ref[e, n_blk], half)

        def tile(rows, x_ref=x_ref, y_ref=y_ref):
            x = x_ref[rows, 0:D_MODEL]
            pieces = x_ref[rows, D_MODEL:XE_W].astype(F32)
            gate = jnp.sum(jnp.where(mine, pieces, 0.0), axis=1, keepdims=True)
            a = _dot(x, wg_s[...])
            u = _dot(x, wu_s[...])
            hid = a * (1.0 / (1.0 + jnp.exp(-a))) * u * gate
            y_ref[rows, :] = _dot(hid.astype(BF16), wd_s[...]).astype(BF16)

        def full(j, _, tile=tile):
            tile(pl.ds(pl.multiple_of(j * WIDE_ROWS, WIDE_ROWS), WIDE_ROWS))
            return 0

        def blank(j, _, y_ref=y_ref):
            y_ref[pl.ds(pl.multiple_of(j * half, half), half), :] = jnp.zeros((half, D_MODEL), BF16)
            return 0

        lax.fori_loop(0, lax.shift_right_logical(n_half, 2), full, 0)

        @pl.when((n_half & 2) != 0)
        def _(tile=tile, n_half=n_half):
            tile(pl.ds(pl.multiple_of((n_half & -4) * half, WIDE_ROWS), ROWS))

        @pl.when((n_half & 1) != 0)
        def _(tile=tile, n_half=n_half):
            tile(pl.ds(pl.multiple_of((n_half - 1) * half, half), half))

        lax.fori_loop(n_half, y_ref.shape[0] // half, blank, 0)


def _combine_kernel(s_ref, y_ref, posm_ref, x1_ref, mod_ref, o_ref, yw_s, sel_s, acc_s, sem,
                    *, n_blk, cap_pad, common_rows):
    for t in range(STEP_BLOCKS):
        rows = pl.ds(t * ROWS, ROWS)
        _combine_block(pl.program_id(0) * STEP_BLOCKS + t, t, s_ref, y_ref, posm_ref.at[t], x1_ref.at[rows, :],
                       mod_ref, o_ref.at[rows, :], yw_s, sel_s, acc_s, sem, n_blk, cap_pad, common_rows)


def _combine_block(b, buf, s_ref, y_ref, posm_ref, x1_ref, mod_ref, o_ref, yw_s, sel_s, acc_s, sem, n_blk, cap_pad,
                   common_rows):
    base, units, first_unit, n_units = _block_units(s_ref, b)

    def unit_rows(first_unit_e, u):
        return pl.ds(pl.multiple_of((first_unit_e + u) * WIN, WIN), WIN)

    def bounds(base_e, u):
        lo = base_e + u * WIN
        return lo, jnp.minimum(lo, cap_pad - WIN)

    def copies(slot, blk, fn):
        blk_base, blk_units, blk_first, _ = _block_units(s_ref, blk)

        def one(e, u):
            src = pl.ds(pl.multiple_of(bounds(blk_base[e], u)[1], ALIGN), WIN)
            fn(pltpu.make_async_copy(y_ref.at[e, src, :], yw_s.at[slot, unit_rows(blk_first[e], u), :],
                                     sem.at[slot]))

        _for_units(blk_units, one)

    @pl.when(b == 0)
    def _():
        copies(0, 0, lambda c: c.start())

    @pl.when(b + 1 < n_blk)
    def _():
        copies(1 - buf, b + 1, lambda c: c.start())

    def pick(e, u):
        lo, first = bounds(base[e], u)
        sel_s[unit_rows(first_unit[e], u), :] = _window_onehot(posm_ref, e, first, lo).astype(BF16)

    _for_units(units, pick)
    n_chunks = jnp.maximum(_ceil_div_pow2(n_units * WIN, ROWS), common_rows // ROWS)

    def blank(u, _):
        sel_s[unit_rows(u, 0), :] = jnp.zeros((WIN, ROWS), BF16)
        yw_s[buf, unit_rows(u, 0), :] = jnp.zeros((WIN, D_MODEL), BF16)
        return 0

    copies(buf, b, lambda c: c.wait())
    lax.fori_loop(n_units, n_chunks * (ROWS // WIN), blank, 0)

    def weighted(rows):
        return _dot(sel_s[rows, :].T, yw_s[buf, rows, :])

    def add_chunk(c, _):
        acc_s[...] += weighted(pl.ds(pl.multiple_of(c * ROWS, ROWS), ROWS))
        return 0

    acc_s[...] = weighted(pl.ds(0, common_rows))
    lax.fori_loop(common_rows // ROWS, n_chunks, add_chunk, 0)
    o_ref[...] = x1_ref[...] + mod_ref[5:6, :] * acc_s[...]


def _padded_capacity(cap, n_blk):
    return -(-(cap + (ALIGN - 1) * n_blk + WIN) // ROWS) * ROWS


def _step_spec(*shape):
    return pl.BlockSpec((STEP_BLOCKS,) + shape, lambda i, s: (i,) + (0,) * len(shape))


def _common_rows(n_tok):
    return COMMON_ROWS + (ROWS if n_tok > ROWS else 0)


def _dispatch(h2, posm, s_tab, cap, n_tok):
    n_blk = h2.shape[0] // ROWS
    cap_pad = _padded_capacity(cap, n_blk)
    return pl.pallas_call(
        functools.partial(_dispatch_kernel, n_blk=n_blk, cap_pad=cap_pad, common_rows=_common_rows(n_tok)),
        out_shape=jax.ShapeDtypeStruct((N_EXPERTS, cap_pad, XE_W), BF16),
        grid_spec=pltpu.PrefetchScalarGridSpec(
            num_scalar_prefetch=1, grid=(n_blk // STEP_BLOCKS,),
            in_specs=[pl.BlockSpec((STEP_BLOCKS * ROWS, XE_W), lambda i, s: (i, 0)), _step_spec(N_EXPERTS, ROWS)],
            out_specs=pl.BlockSpec(memory_space=pl.ANY),
            scratch_shapes=[pltpu.VMEM((MAX_UNITS * WIN, ROWS), BF16), pltpu.VMEM((2, MAX_UNITS * WIN, XE_W), BF16),
                            pltpu.SemaphoreType.DMA((2, N_EXPERTS))]),
        compiler_params=_params(1), name="dispatch",
    )(s_tab, h2, posm)


def _ffn(xes, s_tabs, wg, wu, wd):
    n_sets = len(xes)
    per_expert = lambda rows, cols: pl.BlockSpec((None, rows, cols), lambda e, *s: (e, 0, 0))
    return pl.pallas_call(
        functools.partial(_ffn_kernel, n_blks=tuple(s.shape[1] - 1 for s in s_tabs)),
        out_shape=[jax.ShapeDtypeStruct(xe.shape[:2] + (D_MODEL,), BF16) for xe in xes],
        grid_spec=pltpu.PrefetchScalarGridSpec(
            num_scalar_prefetch=n_sets, grid=(N_EXPERTS,),
            in_specs=[per_expert(xe.shape[1], XE_W) for xe in xes]
            + [per_expert(D_MODEL, EXPERT_FF), per_expert(D_MODEL, EXPERT_FF), per_expert(EXPERT_FF, D_MODEL)],
            out_specs=[per_expert(xe.shape[1], D_MODEL) for xe in xes],
            scratch_shapes=[pltpu.VMEM((D_MODEL, EXPERT_FF), BF16), pltpu.VMEM((D_MODEL, EXPERT_FF), BF16),
                            pltpu.VMEM((EXPERT_FF, D_MODEL), BF16)]),
        compiler_params=_params(1), name="ffn",
    )(*s_tabs, *xes, wg, wu, wd)


def _combine(y, posm, s_tab, x1, mod, n_tok):
    n_all = x1.shape[0]
    n_blk = n_all // ROWS
    shared_mod = mod.shape[0] == 1
    assert shared_mod or n_tok % (STEP_BLOCKS * ROWS) == 0
    steps_per_seq = max(n_tok // (STEP_BLOCKS * ROWS), 1)
    mod_map = (lambda i, s: (0, 0, 0)) if shared_mod else (lambda i, s: (i // steps_per_seq, 0, 0))
    rows = pl.BlockSpec((STEP_BLOCKS * ROWS, D_MODEL), lambda i, s: (i, 0))
    return pl.pallas_call(
        functools.partial(_combine_kernel, n_blk=n_blk, cap_pad=y.shape[1], common_rows=_common_rows(n_tok)),
        out_shape=jax.ShapeDtypeStruct((n_all, D_MODEL), F32),
        grid_spec=pltpu.PrefetchScalarGridSpec(
            num_scalar_prefetch=1, grid=(n_blk // STEP_BLOCKS,),
            in_specs=[pl.BlockSpec(memory_space=pl.ANY), _step_spec(N_EXPERTS, ROWS), rows,
                      pl.BlockSpec((None, 6, D_MODEL), mod_map)],
            out_specs=rows,
            scratch_shapes=[pltpu.VMEM((2, MAX_UNITS * WIN, D_MODEL), BF16), pltpu.VMEM((MAX_UNITS * WIN, ROWS), BF16),
                            pltpu.VMEM((ROWS, D_MODEL), F32), pltpu.SemaphoreType.DMA((2,))]),
        compiler_params=_params(1), name="combine",
    )(s_tab, y, posm, x1, mod)


def _rope_tables(n_tok):
    rows = n_tok // GRID_W
    row = np.repeat(np.arange(rows, dtype=np.float64), GRID_W)
    col = np.tile(np.arange(GRID_W, dtype=np.float64), rows)
    axis_dim = QK_ROPE // 2
    inv_freq = ROPE_THETA ** (-np.arange(0, axis_dim, 2, dtype=np.float64) / axis_dim)
    ang = np.concatenate([row[:, None] * inv_freq, col[:, None] * inv_freq], axis=-1)
    cos = np.ones((n_tok, LANES))
    sin = np.zeros((n_tok, LANES))
    cos[:, ROPE_LANE0:ROPE_LANE0 + QK_ROPE] = np.repeat(np.cos(ang), 2, axis=1)
    sgn = np.tile(np.array([-1.0, 1.0]), QK_ROPE // 2)
    sin[:, ROPE_LANE0:ROPE_LANE0 + QK_ROPE] = np.repeat(np.sin(ang), 2, axis=1) * sgn
    return jnp.asarray(cos, F32), jnp.asarray(sin, F32)


def _dft_tables(n_tok):
    def cs(n):
        k = np.arange(n)
        ang = 2.0 * np.pi * ((k[:, None] * k[None, :]) % n) / n
        return np.cos(ang), np.sin(ang)
    cc, sc = cs(FNET_CH)
    cn, sn = cs(n_tok)
    half = n_tok // 2
    used = half if n_tok > ROWS else n_tok
    folded = np.concatenate([cn[:, :used], -sn[:, :used]], axis=1)
    flip = np.fliplr(np.eye(min(ROWS, half)))
    return tuple(jnp.asarray(t, F32) for t in (cc, sc, folded)) + (jnp.asarray(flip, BF16),)


def _pad_heads(w, width):
    lead = w.shape[:-1]
    w = w.reshape(lead + (N_HEADS, width))
    w = jnp.pad(w, [(0, 0)] * len(lead) + [(0, 0), (0, HEAD_PAD - width)])
    return w.reshape(lead + (N_HEADS * HEAD_PAD,))


def _pair_swap_lanes(n_groups):
    perm = np.arange(n_groups * LANES).reshape(n_groups, LANES)
    rot = perm[:, ROPE_LANE0:ROPE_LANE0 + QK_ROPE].reshape(n_groups, QK_ROPE // 2, 2)[:, :, ::-1]
    perm[:, ROPE_LANE0:ROPE_LANE0 + QK_ROPE] = rot.reshape(n_groups, QK_ROPE)
    return perm.reshape(-1)


def _spare_from_rotary(t, src=None):
    src = t if src is None else src
    lead = t.shape[:-1]
    t, src = t.reshape(lead + (-1, LANES)), src.reshape(lead + (-1, LANES))
    t = jnp.concatenate([t[..., :QK_DIM], src[..., ROPE_LANE0:ROPE_LANE0 + QK_ROPE]], axis=-1)
    return t.reshape(lead + (-1,))


def _head_tables(gain, rope_tabs, n_tok, query):
    if rope_tabs is None:
        tab = jnp.broadcast_to(gain, (n_tok, LANES))
        return (tab if query else _spare_from_rotary(tab),)
    cos, sin = rope_tabs
    a, b = gain * cos, gain[:, _pair_swap_lanes(1)] * sin
    return (_spare_from_rotary(a, b),) if query else (_spare_from_rotary(a), _spare_from_rotary(b))


def _layer0_weights(norm1, norm2, w_in, q_a_norm, w_q_up, q_norm, kv_a_norm, w_kv_up, k_norm, conv_w, w_o):
    c0 = Q_LORA + KV_LORA
    rope_cols = jnp.pad(w_in[:, c0:c0 + QK_ROPE], ((0, 0), (ROPE_LANE0, LANES - ROPE_LANE0 - QK_ROPE)))
    w_in_pad = jnp.concatenate([w_in[:, :c0], _spare_from_rotary(rope_cols),
                                _spare_from_rotary(rope_cols[:, _pair_swap_lanes(1)]),
                                w_in[:, c0 + QK_ROPE:]], axis=1)
    kv = w_kv_up.reshape(KV_LORA, N_HEADS, QK_NOPE + V_HEAD)
    w_q = _pad_heads(w_q_up, QK_DIM)
    w_q = _spare_from_rotary(w_q, w_q[:, _pair_swap_lanes(N_HEADS)])
    head_gain = lambda g: jnp.pad(g, (0, HEAD_PAD - QK_DIM)).reshape(1, -1)
    return dict(
        norm1=norm1.reshape(1, -1), norm2=norm2.reshape(1, -1), w_in=w_in_pad.astype(BF16),
        q_a_norm=q_a_norm.reshape(1, -1), w_q=w_q.astype(BF16),
        q_gain=head_gain(q_norm) * (QK_DIM ** -0.5), kv_a_norm=kv_a_norm.reshape(1, -1),
        w_k=_pad_heads(kv[:, :, :QK_NOPE].reshape(KV_LORA, -1), QK_NOPE).astype(BF16),
        w_v=kv[:, :, QK_NOPE:].reshape(KV_LORA, -1).astype(BF16),
        w_v_wide=_pad_heads(kv[:, :, QK_NOPE:].reshape(KV_LORA, -1), V_HEAD).astype(BF16), k_gain=head_gain(k_norm),
        conv_w=conv_w, w_o=w_o.astype(BF16))


def _router_weights(w_router):
    hi, lo = _split_hi_lo(jnp.pad(w_router, ((0, 0), (0, LANES - N_EXPERTS))))
    return dict(wr_hi=hi, wr_lo=lo)


def _moe(sets, tri, w):
    routed = []
    for x1, h2, aff, mod, n_tok in sets:
        cap = CAPACITY_FACTOR * x1.shape[0] // N_EXPERTS
        posm, s_tab = _route(aff, tri, cap)
        routed.append((posm, s_tab, _dispatch(h2, posm, s_tab, cap, n_tok)))
    ys = _ffn([r[2] for r in routed], [r[1] for r in routed], w["wg"], w["wu"], w["wd"])
    return [_combine(y, posm, s_tab, x1, mod, n_tok)
            for y, (posm, s_tab, _), (x1, _, _, mod, n_tok) in zip(ys, routed, sets)]


def _mixer0(x, mod, l0, ctx, rope_tabs, state):
    n_b, n_tok, _ = x.shape
    n_keys = n_tok + (0 if ctx is None else ctx[0].shape[1])
    w_v = l0["w_v_wide"] if n_keys > WIDE_ROWS else l0["w_v"]
    q, k, v, conv, *new_state = _front0(x, mod, l0, rope_tabs, w_v, state)
    if state:
        new_state = (new_state[0], new_state[1][:, :, ROPE_LANE0:ROPE_LANE0 + QK_ROPE])
    if ctx is not None:
        kc, vc = _ctx_kv(ctx[0], ctx[1], l0, w_v)
        attn = _attention(q, k, v, kc, vc)
    else:
        attn = _attention(q, k, v)
    flat = lambda a: a.reshape(n_b * n_tok, a.shape[-1])
    return _post0(flat(x), flat(attn), flat(conv), mod, l0, n_tok), new_state


def kernel(x_prompt, x_sample, c, cache_c_kv_l0, cache_k_rope_l0, c_ctx, norm1_l0, norm2_l0, w_mod_l0, b_mod_l0, w_in_l0, q_a_norm_l0, w_q_up_l0, q_norm_l0, kv_a_norm_l0, w_kv_up_l0, k_norm_l0, conv_w_l0, w_o_l0, w_router_l0, w_gate_l0, w_up_l0, w_down_l0, norm1_l1, norm2_l1, w_mod_l1, b_mod_l1, w_f_l1, w_router_l1, w_gate_l1, w_up_l1, w_down_l1):
    n_dec = c.shape[0]
    cond = jnp.concatenate([c_ctx[None, :], c, jnp.zeros((16 - 1 - n_dec, D_MODEL), F32)], axis=0)
    m0 = _modulation(cond, w_mod_l0, b_mod_l0)
    m1 = _modulation(cond, w_mod_l1, b_mod_l1)
    mods_prompt = (m0[0:1], m1[0:1])
    mods_sample = (m0[1:1 + n_dec], m1[1:1 + n_dec])

    l0 = _layer0_weights(norm1_l0, norm2_l0, w_in_l0, q_a_norm_l0, w_q_up_l0, q_norm_l0, kv_a_norm_l0,
                         w_kv_up_l0, k_norm_l0, conv_w_l0, w_o_l0)
    l0.update(_router_weights(w_router_l0))
    l0.update(wg=w_gate_l0, wu=w_up_l0, wd=w_down_l0)
    l1 = dict(norm1=norm1_l1.reshape(1, -1), norm2=norm2_l1.reshape(1, -1), w_f=w_f_l1.astype(BF16))
    l1.update(_router_weights(w_router_l1))
    l1.update(wg=w_gate_l1, wu=w_up_l1, wd=w_down_l1)

    tri = jnp.asarray(np.triu(np.ones((ROWS, ROWS)), 1), BF16)
    k_rope_pad = _spare_from_rotary(
        jnp.pad(cache_k_rope_l0, ((0, 0), (0, 0), (ROPE_LANE0, LANES - ROPE_LANE0 - QK_ROPE))))
    xs = (x_prompt, x_sample)
    mods = (mods_prompt, mods_sample)
    n_toks = tuple(x.shape[1] for x in xs)

    (front_p, (new_c_kv, new_k_rope)) = _mixer0(x_prompt, mods_prompt[0], l0, None, None, True)
    (front_s, _) = _mixer0(x_sample, mods_sample[0], l0, (cache_c_kv_l0, k_rope_pad), _rope_tables(n_toks[1]),
                           False)
    ys = _moe([tuple(front) + (mod[0], n_tok) for front, mod, n_tok in zip((front_p, front_s), mods, n_toks)],
              tri, l0)

    sets = []
    for y, x, mod, n_tok in zip(ys, xs, mods, n_toks):
        x1, h2, aff = _fnet(y.reshape(x.shape), mod[1], l1, _dft_tables(n_tok))
        sets.append((x1.reshape(-1, D_MODEL), h2.reshape(-1, XE_W), aff, mod[1], n_tok))
    y_prompt, y_sample = (y.reshape(x.shape) for y, x in zip(_moe(sets, tri, l1), xs))
    return (y_prompt, y_sample, new_c_kv, new_k_rope)
```

```python
import functools

import jax
import jax.numpy as jnp
import numpy as np
from jax import lax
from jax.experimental import pallas as pl
from jax.experimental.pallas import tpu as pltpu

D_MODEL = 1024
GRID_W = 64
N_HEADS = 8
QK_NOPE = 64
QK_ROPE = 32
QK_DIM = QK_NOPE + QK_ROPE
V_HEAD = 64
Q_LORA = 384
KV_LORA = 256
CONV_CH = 512
FNET_GROUPS = 4
FNET_CH = D_MODEL // FNET_GROUPS
N_EXPERTS = 16
EXPERT_FF = 512
CAPACITY_FACTOR = 2
ROPE_THETA = 10000.0
EPS = 1e-6

LANES = 128
HEAD_PAD = LANES
ROWS = 256
WIDE_ROWS = 512
MIX_ROWS = 1024
ATTN_ROWS = 1024
ALIGN = 8
WIN = 64
MAX_UNITS = N_EXPERTS * (ROWS // WIN)
COMMON_ROWS = N_EXPERTS * WIN
STEP_BLOCKS = 2
XE_W = D_MODEL + LANES
IN0_PAD = Q_LORA + KV_LORA + 2 * LANES + 3 * CONV_CH
ROPE_LANE0 = QK_NOPE
VMEM_LIMIT = 56 * 1024 * 1024

F32 = jnp.float32
BF16 = jnp.bfloat16


def _dot(a, b):
    return jnp.dot(a, b, preferred_element_type=F32)


def _dot_nt(a, b):
    return lax.dot_general(a, b, (((1,), (1,)), ((), ())), preferred_element_type=F32)


def _split_hi_lo(x):
    hi = x.astype(BF16)
    lo = (x - hi.astype(F32)).astype(BF16)
    return hi, lo


def _params(n_axes):
    return pltpu.CompilerParams(dimension_semantics=("arbitrary",) * n_axes,
                                vmem_limit_bytes=VMEM_LIMIT)


def _rms(x, gain):
    return x * lax.rsqrt(jnp.mean(x * x, axis=-1, keepdims=True) + EPS) * gain


def _mod_kernel(cond_ref, w_ref, b_ref, o_ref):
    c = cond_ref[...]
    s = c * (1.0 / (1.0 + jnp.exp(-c)))
    s_hi, s_lo = _split_hi_lo(s)
    w = w_ref[...].astype(BF16)
    o_ref[...] = _dot(s_hi, w) + _dot(s_lo, w) + b_ref[...]


def _modulation(cond, w_mod, b_mod):
    n_rows = cond.shape[0]
    tn = 3072
    out = pl.pallas_call(
        _mod_kernel,
        out_shape=jax.ShapeDtypeStruct((n_rows, 6 * D_MODEL), F32),
        grid=(6 * D_MODEL // tn,),
        in_specs=[pl.BlockSpec((n_rows, D_MODEL), lambda i: (0, 0)),
                  pl.BlockSpec((D_MODEL, tn), lambda i: (0, i)),
                  pl.BlockSpec((1, tn), lambda i: (0, i))],
        out_specs=pl.BlockSpec((n_rows, tn), lambda i: (0, i)),
        compiler_params=_params(1),
        name="modulation",
    )(cond, w_mod, b_mod.reshape(1, -1))
    return out.reshape(n_rows, 6, D_MODEL)


def _head_norm_rope(xh, gain_cos, swapped_sin):
    lane = lax.broadcasted_iota(jnp.int32, (1, HEAD_PAD), 1)
    ss = jnp.sum(jnp.where(lane < QK_DIM, xh * xh, 0.0), axis=-1, keepdims=True) * (1.0 / QK_DIM)
    y = xh * gain_cos
    if swapped_sin is not None:
        y = y + swapped_sin
    return y * lax.rsqrt(ss + EPS)


def _expand_kv(ckv_bf, r, swapped_sin, wk_ref, wv_ref, gain_cos, k_ref, v_ref, rows):
    kf = _dot(ckv_bf, wk_ref[...])
    vf = _dot(ckv_bf, wv_ref[...])
    if v_ref.shape[-1] == N_HEADS * HEAD_PAD:
        lane = lax.broadcasted_iota(jnp.int32, (1, N_HEADS * HEAD_PAD), 1)
        vf = vf + jnp.where((lane & (HEAD_PAD - 1)) == V_HEAD, 1.0, 0.0)
    v_ref[rows, :] = vf.astype(BF16)
    for h in range(N_HEADS):
        sl = slice(h * HEAD_PAD, (h + 1) * HEAD_PAD)
        k_ref[rows, sl] = _head_norm_rope(kf[:, sl] + r, gain_cos, swapped_sin).astype(BF16)


def _front0_kernel(*refs, n_tok, rope, state):
    (x_ref, mod_ref, n1_ref, win_ref, qan_ref, wq_ref, kvan_ref, wk_ref, wv_ref, cw_ref,
     qa_ref, ka_ref) = refs[:12]
    pos = 12
    if rope:
        kb_ref = refs[pos]
        pos += 1
    q_ref, k_ref, v_ref, conv_ref = refs[pos:pos + 4]
    if state:
        ckv_ref, kr_ref = refs[pos + 4:pos + 6]
    cu_s, gb_s = refs[-2:]
    sh1, sc1 = mod_ref[0:1, :], mod_ref[1:2, :]
    c_rope = Q_LORA + KV_LORA
    wide = min(n_tok, MIX_ROWS)
    for c in range(n_tok // ROWS):
        rows = pl.ds(c * ROWS, ROWS)
        if (c * ROWS) % wide == 0:
            wide_rows = pl.ds(c * ROWS, wide)
            h = _rms(x_ref[wide_rows, :], n1_ref[...]) * (1.0 + sc1) + sh1
            proj_wide = _dot(h.astype(BF16), win_ref[...])
        off = (c * ROWS) % wide
        proj = proj_wide[off:off + ROWS, :]
        cq = _rms(proj[:, :Q_LORA], qan_ref[...]).astype(BF16)
        qf = _dot(cq, wq_ref[...])
        for hd in range(N_HEADS):
            sl = slice(hd * HEAD_PAD, (hd + 1) * HEAD_PAD)
            q_ref[rows, sl] = _head_norm_rope(qf[:, sl], qa_ref[rows, :], None).astype(BF16)
        ckv = _rms(proj[:, Q_LORA:c_rope], kvan_ref[...])
        r = proj[:, c_rope:c_rope + LANES]
        if state:
            ckv_ref[rows, :] = ckv
            kr_ref[rows, :] = r
        k_sin = proj[:, c_rope + LANES:c_rope + 2 * LANES] * kb_ref[rows, :] if rope else None
        _expand_kv(ckv.astype(BF16), r, k_sin, wk_ref, wv_ref, ka_ref[rows, :], k_ref, v_ref, rows)
        c0 = c_rope + 2 * LANES
        gb_s[rows, :] = proj[:, c0:c0 + CONV_CH]
        cu_s[rows, :] = proj[:, c0 + CONV_CH:c0 + 2 * CONV_CH] * proj[:, c0 + 2 * CONV_CH:c0 + 3 * CONV_CH]
    cu = cu_s[...]
    row = lax.broadcasted_iota(jnp.int32, cu.shape, 0)
    prev = jnp.where(row == 0, 0.0, pltpu.roll(cu, 1, 0))
    nxt = jnp.where(row == n_tok - 1, 0.0, pltpu.roll(cu, n_tok - 1, 0))
    conv = gb_s[...] * (cw_ref[0:1, :] * prev + cw_ref[1:2, :] * cu + cw_ref[2:3, :] * nxt)
    conv_ref[...] = conv.astype(BF16)


def _const_spec(shape):
    return pl.BlockSpec(shape, lambda b: (0,) * len(shape))


def _front0(x, mod, w, rope_tabs, w_v, state):
    n_b, n_tok, _ = x.shape
    v_width = w_v.shape[1]
    shared_mod = mod.shape[0] == 1
    seq = lambda width: pl.BlockSpec((None, n_tok, width), lambda b: (b, 0, 0))
    in_specs = [seq(D_MODEL),
                pl.BlockSpec((None, 6, D_MODEL), (lambda b: (0, 0, 0)) if shared_mod else (lambda b: (b, 0, 0))),
                _const_spec((1, D_MODEL)), _const_spec((D_MODEL, IN0_PAD)), _const_spec((1, Q_LORA)),
                _const_spec((Q_LORA, N_HEADS * HEAD_PAD)),
                _const_spec((1, KV_LORA)), _const_spec((KV_LORA, N_HEADS * HEAD_PAD)),
                _const_spec((KV_LORA, v_width)), _const_spec((3, CONV_CH)),
                _const_spec((n_tok, LANES)), _const_spec((n_tok, LANES))]
    q_tabs = _head_tables(w["q_gain"], rope_tabs, n_tok, True)
    k_tabs = _head_tables(w["k_gain"], rope_tabs, n_tok, False)
    args = [x, mod, w["norm1"], w["w_in"], w["q_a_norm"], w["w_q"], w["kv_a_norm"],
            w["w_k"], w_v, w["conv_w"], q_tabs[0], k_tabs[0]]
    if rope_tabs is not None:
        in_specs += [_const_spec((n_tok, LANES))]
        args += [k_tabs[1]]
    out_shape = [jax.ShapeDtypeStruct((n_b, n_tok, N_HEADS * HEAD_PAD), BF16),
                 jax.ShapeDtypeStruct((n_b, n_tok, N_HEADS * HEAD_PAD), BF16),
                 jax.ShapeDtypeStruct((n_b, n_tok, v_width), BF16),
                 jax.ShapeDtypeStruct((n_b, n_tok, CONV_CH), BF16)]
    out_specs = [seq(N_HEADS * HEAD_PAD), seq(N_HEADS * HEAD_PAD), seq(v_width), seq(CONV_CH)]
    if state:
        out_shape += [jax.ShapeDtypeStruct((n_b, n_tok, KV_LORA), F32),
                      jax.ShapeDtypeStruct((n_b, n_tok, LANES), F32)]
        out_specs += [seq(KV_LORA), seq(LANES)]
    return pl.pallas_call(
        functools.partial(_front0_kernel, n_tok=n_tok, rope=rope_tabs is not None, state=state),
        out_shape=out_shape, grid=(n_b,), in_specs=in_specs, out_specs=out_specs,
        scratch_shapes=[pltpu.VMEM((n_tok, CONV_CH), F32), pltpu.VMEM((n_tok, CONV_CH), F32)],
        compiler_params=_params(1), name="front0",
    )(*args)


def _ctx_kv_kernel(ckv_ref, r_ref, wk_ref, wv_ref, ka_ref, k_ref, v_ref):
    rows = pl.ds(0, ckv_ref.shape[0])
    _expand_kv(ckv_ref[...].astype(BF16), r_ref[...], None, wk_ref, wv_ref, ka_ref[...], k_ref, v_ref, rows)


def _ctx_kv(cache_c_kv, cache_k_rope_pad, w, w_v):
    n_b, n_ctx, _ = cache_c_kv.shape
    v_width = w_v.shape[1]
    seq = lambda width: pl.BlockSpec((None, n_ctx, width), lambda b: (b, 0, 0))
    return pl.pallas_call(
        _ctx_kv_kernel,
        out_shape=[jax.ShapeDtypeStruct((n_b, n_ctx, N_HEADS * HEAD_PAD), BF16),
                   jax.ShapeDtypeStruct((n_b, n_ctx, v_width), BF16)],
        grid=(n_b,),
        in_specs=[seq(KV_LORA), seq(LANES), _const_spec((KV_LORA, N_HEADS * HEAD_PAD)),
                  _const_spec((KV_LORA, v_width)), _const_spec((n_ctx, LANES))],
        out_specs=[seq(N_HEADS * HEAD_PAD), seq(v_width)],
        compiler_params=_params(1), name="ctx_kv",
    )(cache_c_kv, cache_k_rope_pad, w["w_k"], w_v, _head_tables(w["k_gain"], None, n_ctx, False)[0])


def _attn_kernel(*refs, with_ctx):
    if with_ctx:
        q_ref, k_ref, v_ref, kc_ref, vc_ref, o_ref = refs
    else:
        q_ref, k_ref, v_ref, o_ref = refs
    ones_lane = v_ref.shape[-1] == N_HEADS * HEAD_PAD
    lane = lax.broadcasted_iota(jnp.int32, (q_ref.shape[0], LANES), 1)
    for pair in range(N_HEADS // 2):
        outs = []
        for hd in (2 * pair, 2 * pair + 1):
            sl = slice(hd * HEAD_PAD, (hd + 1) * HEAD_PAD)
            vsl = sl if ones_lane else slice(pair * LANES, (pair + 1) * LANES)
            qh = q_ref[:, sl]
            s = _dot_nt(qh, k_ref[:, sl])
            m = jnp.max(s, axis=-1, keepdims=True)
            if with_ctx:
                sc = _dot_nt(qh, kc_ref[:, sl])
                m = jnp.maximum(m, jnp.max(sc, axis=-1, keepdims=True))
            p = jnp.exp(s - m)
            o = _dot(p.astype(BF16), v_ref[:, vsl])
            if with_ctx:
                pc = jnp.exp(sc - m)
                o = o + _dot(pc.astype(BF16), vc_ref[:, vsl])
            if ones_lane:
                den = o[:, V_HEAD:V_HEAD + 1]
            else:
                den = jnp.sum(p, axis=-1, keepdims=True)
                if with_ctx:
                    den = den + jnp.sum(pc, axis=-1, keepdims=True)
            outs.append(o / den)
        odd = pltpu.roll(outs[1], V_HEAD, 1) if ones_lane else outs[1]
        o_ref[:, pair * LANES:(pair + 1) * LANES] = jnp.where(lane < V_HEAD, outs[0], odd).astype(BF16)


def _attention(q, k, v, kc=None, vc=None):
    n_b, n_tok, _ = q.shape
    with_ctx = kc is not None
    tq = min(n_tok, ATTN_ROWS)
    qspec = lambda width: pl.BlockSpec((None, tq, width), lambda b, i: (b, i, 0))
    kvspec = lambda a: pl.BlockSpec((None,) + a.shape[1:], lambda b, i: (b, 0, 0))
    in_specs = [qspec(N_HEADS * HEAD_PAD), kvspec(k), kvspec(v)]
    args = [q, k, v]
    if with_ctx:
        in_specs += [kvspec(kc), kvspec(vc)]
        args += [kc, vc]
    return pl.pallas_call(
        functools.partial(_attn_kernel, with_ctx=with_ctx),
        out_shape=jax.ShapeDtypeStruct((n_b, n_tok, N_HEADS * V_HEAD), BF16),
        grid=(n_b, n_tok // tq), in_specs=in_specs, out_specs=qspec(N_HEADS * V_HEAD),
        compiler_params=_params(2), name="attention",
    )(*args)


def _moe_front(x1, mod_ref, n2_ref, wrh_ref, wrl_ref, h2_ref, aff_ref, rows):
    sh2, sc2 = mod_ref[3:4, :], mod_ref[4:5, :]
    h2 = _rms(x1, n2_ref[...]) * (1.0 + sc2) + sh2
    h_hi, h_lo = _split_hi_lo(h2)
    logit = _dot(h_hi, wrh_ref[...]) + _dot(h_lo, wrh_ref[...]) + _dot(h_hi, wrl_ref[...])
    real = lax.broadcasted_iota(jnp.int32, (1, LANES), 1) < N_EXPERTS
    top = jnp.max(jnp.where(real, logit, -jnp.inf), axis=1, keepdims=True)
    e = jnp.where(real, jnp.exp(logit - top), 0.0)
    a = e / jnp.sum(e, axis=1, keepdims=True)
    a_hi = a.astype(BF16).astype(F32)
    a_mid = (a - a_hi).astype(BF16).astype(F32)
    a_lo = a - a_hi - a_mid
    h2_ref[rows, 0:D_MODEL] = h_hi
    h2_ref[rows, D_MODEL:XE_W] = (a_hi + pltpu.roll(a_mid, N_EXPERTS, 1)
                                  + pltpu.roll(a_lo, 2 * N_EXPERTS, 1)).astype(BF16)
    aff_ref[...] = a.T[0:N_EXPERTS, :]


def _post0_kernel(x_ref, attn_ref, conv_ref, wo_ref, mod_ref, n2_ref, wrh_ref, wrl_ref,
                  x1_ref, h2_ref, aff_ref):
    n_attn = N_HEADS * V_HEAD
    mix = _dot(attn_ref[...], wo_ref[0:n_attn, :]) + _dot(conv_ref[...], wo_ref[n_attn:, :])
    for t in range(x_ref.shape[0] // ROWS):
        rows = pl.ds(t * ROWS, ROWS)
        x1 = x_ref[rows, :] + mod_ref[2:3, :] * mix[t * ROWS:(t + 1) * ROWS, :]
        x1_ref[rows, :] = x1
        _moe_front(x1, mod_ref, n2_ref, wrh_ref, wrl_ref, h2_ref, aff_ref.at[t], rows)


def _post0(x, attn, conv, mod, w, n_tok):
    n_all = x.shape[0]
    shared_mod = mod.shape[0] == 1
    rows = MIX_ROWS if shared_mod else min(n_tok, MIX_ROWS)
    tiles_per_seq = max(n_tok // rows, 1)
    tile = lambda width: pl.BlockSpec((rows, width), lambda i: (i, 0))
    const = lambda shape: pl.BlockSpec(shape, lambda i: (0,) * len(shape))
    mod_map = (lambda i: (0, 0, 0)) if shared_mod else (lambda i: (i // tiles_per_seq, 0, 0))
    return pl.pallas_call(
        _post0_kernel,
        out_shape=[jax.ShapeDtypeStruct((n_all, D_MODEL), F32),
                   jax.ShapeDtypeStruct((n_all, XE_W), BF16),
                   jax.ShapeDtypeStruct((n_all // ROWS, N_EXPERTS, ROWS), F32)],
        grid=(n_all // rows,),
        in_specs=[tile(D_MODEL), tile(N_HEADS * V_HEAD), tile(CONV_CH), const((D_MODEL, D_MODEL)),
                  pl.BlockSpec((None, 6, D_MODEL), mod_map), const((1, D_MODEL)),
                  const((D_MODEL, LANES)), const((D_MODEL, LANES))],
        out_specs=[tile(D_MODEL), tile(XE_W),
                   pl.BlockSpec((rows // ROWS, N_EXPERTS, ROWS), lambda i: (i, 0, 0))],
        compiler_params=_params(1), name="post0",
    )(x, attn, conv, w["w_o"], mod, w["norm2"], w["wr_hi"], w["wr_lo"])


def _fnet_kernel(x_ref, mod_ref, n1_ref, cc_ref, sc_ref, dft_ref, flip_ref, wf_ref, n2_ref, wrh_ref, wrl_ref,
                 x1_ref, h2_ref, aff_ref, h_s, r_s, y_s, *, n_tok):
    sh1, sc1, g1 = mod_ref[0:1, :], mod_ref[1:2, :], mod_ref[2:3, :]
    n_chunks = n_tok // ROWS
    half = n_tok // 2
    blk = flip_ref.shape[0]
    cc, sc = cc_ref[...].astype(BF16), sc_ref[...].astype(BF16)
    for c in range(n_chunks):
        rows = pl.ds(c * ROWS, ROWS)
        h_s[rows, :] = (_rms(x_ref[rows, :], n1_ref[...]) * (1.0 + sc1) + sh1).astype(BF16)
    n_flip = half // blk if n_tok > ROWS else 0
    for i in range(n_flip):
        src = h_s[pl.ds(half + (n_flip - 1 - i) * blk, blk), :]
        r_s[pl.ds(i * blk, blk), :] = _dot(flip_ref[...], src)
    if n_flip:
        mirror = pltpu.roll(r_s[...], 1, 0)
        row = lax.broadcasted_iota(jnp.int32, mirror.shape, 0)
        mirror = jnp.where(row == 0, 0.0, mirror)
        low = h_s[0:half, :].astype(F32)
        even, odd = (low + mirror).astype(BF16), (low - mirror).astype(BF16)
        mid = h_s[pl.ds(half, 16), :]
    else:
        even = odd = h_s[...]
    fold_rows = even.shape[0]
    mids = []
    for g in range(FNET_GROUPS):
        sl = slice(g * FNET_CH, (g + 1) * FNET_CH)
        y_s[0:fold_rows, sl] = _dot(even[:, sl], cc).astype(BF16)
        y_s[fold_rows:2 * fold_rows, sl] = _dot(odd[:, sl], sc).astype(BF16)
        if n_flip:
            mids.append(_dot(mid[:, sl], cc)[0:1, :])
    mid_row = jnp.concatenate(mids, axis=1) if n_flip else None
    scale = 1.0 / float(np.sqrt(n_tok * FNET_CH))
    wide = min(n_tok, MIX_ROWS)
    for c in range(n_chunks):
        rows = pl.ds(c * ROWS, ROWS)
        if (c * ROWS) % wide == 0:
            parity = lax.broadcasted_iota(jnp.int32, (wide, 1), 0) & 1
            f = _dot(dft_ref[pl.ds(c * ROWS, wide), :].astype(BF16), y_s[...])
            if n_flip:
                f = f + jnp.where(parity == 0, 1.0, -1.0) * mid_row
            f = f * scale
            mix = _dot(f.astype(BF16), wf_ref[...])
        off = (c * ROWS) % wide
        x1 = x_ref[rows, :] + g1 * mix[off:off + ROWS, :]
        x1_ref[rows, :] = x1
        _moe_front(x1, mod_ref, n2_ref, wrh_ref, wrl_ref, h2_ref, aff_ref.at[c], rows)


def _fnet(x, mod, w, dft):
    n_b, n_tok, _ = x.shape
    shared_mod = mod.shape[0] == 1
    seq = lambda width: pl.BlockSpec((None, n_tok, width), lambda b: (b, 0, 0))
    cc, sc, dft_n, flip = dft
    tiles = n_tok // ROWS
    return pl.pallas_call(
        functools.partial(_fnet_kernel, n_tok=n_tok),
        out_shape=[jax.ShapeDtypeStruct((n_b, n_tok, D_MODEL), F32),
                   jax.ShapeDtypeStruct((n_b, n_tok, XE_W), BF16),
                   jax.ShapeDtypeStruct((n_b * tiles, N_EXPERTS, ROWS), F32)],
        grid=(n_b,),
        in_specs=[seq(D_MODEL),
                  pl.BlockSpec((None, 6, D_MODEL), (lambda b: (0, 0, 0)) if shared_mod else (lambda b: (b, 0, 0))),
                  _const_spec((1, D_MODEL)), _const_spec((FNET_CH, FNET_CH)), _const_spec((FNET_CH, FNET_CH)),
                  pl.BlockSpec(dft_n.shape, lambda b: (0, 0), pipeline_mode=pl.Buffered(1)),
                  _const_spec(flip.shape), _const_spec((D_MODEL, D_MODEL)), _const_spec((1, D_MODEL)),
                  _const_spec((D_MODEL, LANES)), _const_spec((D_MODEL, LANES))],
        out_specs=[seq(D_MODEL), seq(XE_W), pl.BlockSpec((tiles, N_EXPERTS, ROWS), lambda b: (b, 0, 0))],
        scratch_shapes=[pltpu.VMEM((n_tok, D_MODEL), BF16), pltpu.VMEM((n_tok // 2, D_MODEL), F32),
                        pltpu.VMEM((dft_n.shape[1], D_MODEL), BF16)],
        compiler_params=_params(1), name="fnet",
    )(x, mod, w["norm1"], cc, sc, dft_n, flip, w["w_f"], w["norm2"], w["wr_hi"], w["wr_lo"])


def _route_kernel(aff_ref, tri_ref, posm_ref, s_ref, *, n_blk, cap):
    def count(pred):
        acc = jnp.zeros((N_EXPERTS, ROWS), F32)
        for b in range(n_blk):
            acc = acc + jnp.where(pred(aff_ref[b]), 1.0, 0.0)
        return jnp.sum(acc, axis=1, keepdims=True)

    def as_f32(bits):
        return pltpu.bitcast(bits, F32)

    def search(_, carry):
        lo, hi = carry
        m2 = lo + ((hi - lo) >> 1)
        m1 = lo + ((m2 - lo) >> 1)
        m3 = m2 + ((hi - m2) >> 1)
        f1, f2, f3 = as_f32(m1), as_f32(m2), as_f32(m3)
        acc = [jnp.zeros((N_EXPERTS, ROWS), F32) for _ in range(3)]
        for b in range(n_blk):
            a = aff_ref[b]
            acc = [acc_i + jnp.where(a >= f, 1.0, 0.0) for acc_i, f in zip(acc, (f1, f2, f3))]
        ok1, ok2, ok3 = (jnp.sum(acc_i, axis=1, keepdims=True) >= cap for acc_i in acc)
        new_lo = jnp.where(ok3, m3, jnp.where(ok2, m2, jnp.where(ok1, m1, lo)))
        new_hi = jnp.where(ok3, hi, jnp.where(ok2, m3, jnp.where(ok1, m2, m1)))
        return new_lo, new_hi

    one_bits = 0x3F800000
    lo0 = jnp.zeros((N_EXPERTS, 1), jnp.int32)
    hi0 = jnp.full((N_EXPERTS, 1), one_bits + 1, jnp.int32)
    lo, hi = lax.fori_loop(0, 17, search, (lo0, hi0))
    lo_f, ub = as_f32(lo), as_f32(hi)
    thr = lo_f
    pending = jnp.ones((N_EXPERTS, 1), F32)
    for _ in range(3):
        cur = jnp.full((N_EXPERTS, ROWS), -1.0, F32)
        for b in range(n_blk):
            a = aff_ref[b]
            cur = jnp.maximum(cur, jnp.where((a >= lo_f) & (a < ub), a, -1.0))
        cur = jnp.max(cur, axis=1, keepdims=True)
        take = (count(lambda a: a >= cur) >= cap) & (pending > 0.0)
        thr = jnp.where(take, cur, thr)
        pending = jnp.where(take, 0.0, pending)
        ub = cur
    need = cap - count(lambda a: a > thr)

    lane = lax.broadcasted_iota(jnp.int32, (N_EXPERTS, LANES), 1)
    carry_tie = jnp.zeros((N_EXPERTS, 1), F32)
    carry_pos = jnp.zeros((N_EXPERTS, 1), F32)
    s_acc = jnp.zeros((N_EXPERTS, LANES), jnp.int32)
    for b in range(n_blk):
        v = aff_ref[b]
        eq = jnp.where(v == thr, 1.0, 0.0)
        tie_rank = _dot(eq.astype(BF16), tri_ref[...]) + carry_tie
        sel = jnp.where((v > thr) | ((v == thr) & (tie_rank < need)), 1.0, 0.0)
        pos = _dot(sel.astype(BF16), tri_ref[...]) + carry_pos
        posm_ref[b] = jnp.where(sel > 0.0, pos, -1.0)
        s_acc = jnp.where(lane == b, carry_pos.astype(jnp.int32), s_acc)
        carry_tie = carry_tie + jnp.sum(eq, axis=1, keepdims=True)
        n_sel = jnp.sum(sel, axis=1, keepdims=True)
        carry_pos = carry_pos + jnp.floor((n_sel + (ALIGN - 1)) * (1.0 / ALIGN)) * ALIGN
    s_ref[...] = jnp.where(lane == n_blk, carry_pos.astype(jnp.int32), s_acc)


def _route(aff, tri, cap):
    n_blk = aff.shape[0]
    assert n_blk < LANES
    full = lambda shape: pl.BlockSpec(shape, lambda i: (0,) * len(shape))
    posm, s_tab = pl.pallas_call(
        functools.partial(_route_kernel, n_blk=n_blk, cap=cap),
        out_shape=[jax.ShapeDtypeStruct((n_blk, N_EXPERTS, ROWS), F32),
                   jax.ShapeDtypeStruct((N_EXPERTS, LANES), jnp.int32)],
        grid=(1,),
        in_specs=[full((n_blk, N_EXPERTS, ROWS)), full((ROWS, ROWS))],
        out_specs=[full((n_blk, N_EXPERTS, ROWS)), full((N_EXPERTS, LANES))],
        compiler_params=_params(1), name="route",
    )(aff, tri)
    return posm, s_tab[:, :n_blk + 1]


def _ceil_div_pow2(x, d):
    return lax.shift_right_logical(x + (d - 1), d.bit_length() - 1)


def _block_units(s_ref, b):
    base = [s_ref[e, b] for e in range(N_EXPERTS)]
    units = [_ceil_div_pow2(s_ref[e, b + 1] - base[e], WIN) for e in range(N_EXPERTS)]
    first_unit, total = [], 0
    for e in range(N_EXPERTS):
        first_unit.append(total)
        total = total + units[e]
    return base, units, first_unit, total


def _for_units(units, fn):
    for e in range(N_EXPERTS):
        def body(u, _, e=e):
            fn(e, u)
            return 0
        lax.fori_loop(0, units[e], body, 0)


def _window_onehot(posm_ref, e, first, lo):
    slot = lax.broadcasted_iota(jnp.int32, (WIN, ROWS), 0) + first
    p = posm_ref[e:e + 1, :].astype(jnp.int32)
    return jnp.where((p == slot) & (p >= lo), 1.0, 0.0)


def _gate_lanes(e):
    lane = lax.broadcasted_iota(jnp.int32, (1, LANES), 1)
    return ((lane & (N_EXPERTS - 1)) == e) & (lane < 3 * N_EXPERTS)


def _dispatch_kernel(s_ref, h_ref, posm_ref, xe_ref, sel_s, x_s, sem, *, n_blk, cap_pad, common_rows):
    for t in range(STEP_BLOCKS):
        _dispatch_block(pl.program_id(0) * STEP_BLOCKS + t, t, s_ref, h_ref.at[pl.ds(t * ROWS, ROWS), :],
                        posm_ref.at[t], xe_ref, sel_s, x_s, sem, n_blk, cap_pad, common_rows)


def _dispatch_block(b, buf, s_ref, h_ref, posm_ref, xe_ref, sel_s, x_s, sem, n_blk, cap_pad, common_rows):
    base, units, first_unit, n_units = _block_units(s_ref, b)

    @pl.when(b == 0)
    def _():
        sel_s[...] = jnp.zeros_like(sel_s)

    def unit_rows(first_unit_e, u):
        return pl.ds(pl.multiple_of((first_unit_e + u) * WIN, WIN), WIN)

    def pick(e, u):
        first = base[e] + u * WIN
        sel_s[unit_rows(first_unit[e], u), :] = _window_onehot(posm_ref, e, first, first).astype(BF16)

    _for_units(units, pick)

    def gather(rows):
        x_s[buf, rows, :] = _dot(sel_s[rows, :], h_ref[...]).astype(BF16)

    def gather_chunk(c, _):
        gather(pl.ds(pl.multiple_of(c * ROWS, ROWS), ROWS))
        return 0

    gather(pl.ds(0, common_rows))
    lax.fori_loop(common_rows // ROWS, _ceil_div_pow2(n_units * WIN, ROWS), gather_chunk, 0)

    def unit_copy(slot, blk_base_e, blk_first_e, e, u):
        dst = pl.ds(pl.multiple_of(blk_base_e + u * WIN, ALIGN), WIN)
        return pltpu.make_async_copy(x_s.at[slot, unit_rows(blk_first_e, u), :], xe_ref.at[e, dst, :],
                                     sem.at[slot, e])

    def for_expert_units(n, fn):
        lax.fori_loop(0, n, lambda u, _: (fn(u), 0)[1], 0)

    prev_base, prev_units, prev_first, _ = _block_units(s_ref, jnp.maximum(b - 1, 0))
    for e in range(N_EXPERTS):
        for_expert_units(jnp.where(b > 0, prev_units[e], 0),
                         lambda u, e=e: unit_copy(1 - buf, prev_base[e], prev_first[e], e, u).wait())
        for_expert_units(units[e], lambda u, e=e: unit_copy(buf, base[e], first_unit[e], e, u).start())

    @pl.when(b == n_blk - 1)
    def _():
        for e in range(N_EXPERTS):
            for_expert_units(units[e], lambda u, e=e: unit_copy(buf, base[e], first_unit[e], e, u).wait())
        x_s[buf, 0:WIN, :] = jnp.zeros((WIN, XE_W), BF16)

        def tail(fn):
            for e in range(N_EXPERTS):
                total = s_ref[e, n_blk]
                n_win = (cap_pad - total) // WIN

                def wide(c, _):
                    row = pl.multiple_of(total + c * WIN, ALIGN)
                    fn(pltpu.make_async_copy(x_s.at[buf, pl.ds(0, WIN), :],
                                             xe_ref.at[e, pl.ds(row, WIN), :], sem.at[buf, e]))
                    return 0

                def narrow(c, _):
                    row = pl.multiple_of(total + n_win * WIN + c * ALIGN, ALIGN)
                    fn(pltpu.make_async_copy(x_s.at[buf, pl.ds(0, ALIGN), :],
                                             xe_ref.at[e, pl.ds(row, ALIGN), :], sem.at[buf, e]))
                    return 0

                lax.fori_loop(0, n_win, wide, 0)
                lax.fori_loop(0, (cap_pad - total - n_win * WIN) // ALIGN, narrow, 0)

        tail(lambda c: c.start())
        tail(lambda c: c.wait())


def _ffn_kernel(*refs, n_blks):
    n_sets = len(n_blks)
    s_refs, x_refs = refs[:n_sets], refs[n_sets:2 * n_sets]
    wg_ref, wu_ref, wd_ref = refs[2 * n_sets:2 * n_sets + 3]
    y_refs = refs[2 * n_sets + 3:3 * n_sets + 3]
    wg_s, wu_s, wd_s = refs[3 * n_sets + 3:]
    e = pl.program_id(0)
    wg_s[...] = wg_ref[...].astype(BF16)
    wu_s[...] = wu_ref[...].astype(BF16)
    wd_s[...] = wd_ref[...].astype(BF16)
    mine = _gate_lanes(e)

    half = ROWS // 2
    for s_ref, x_ref, y_ref, n_blk in zip(s_refs, x_refs, y_refs, n_blks):
        n_half = _ceil_div_pow2(s_ref[e, n_blk], half)

        def tile(rows, x_ref=x_ref, y_ref=y_ref):
            x = x_ref[rows, 0:D_MODEL]
            pieces = x_ref[rows, D_MODEL:XE_W].astype(F32)
            gate = jnp.sum(jnp.where(mine, pieces, 0.0), axis=1, keepdims=True)
            a = _dot(x, wg_s[...])
            u = _dot(x, wu_s[...])
            hid = a * (1.0 / (1.0 + jnp.exp(-a))) * u * gate
            y_ref[rows, :] = _dot(hid.astype(BF16), wd_s[...]).astype(BF16)

        def full(j, _, tile=tile):
            tile(pl.ds(pl.multiple_of(j * WIDE_ROWS, WIDE_ROWS), WIDE_ROWS))
            return 0

        def blank(j, _, y_ref=y_ref):
            y_ref[pl.ds(pl.multiple_of(j * half, half), half), :] = jnp.zeros((half, D_MODEL), BF16)
            return 0

        lax.fori_loop(0, lax.shift_right_logical(n_half, 2), full, 0)

        @pl.when((n_half & 2) != 0)
        def _(tile=tile, n_half=n_half):
            tile(pl.ds(pl.multiple_of((n_half & -4) * half, WIDE_ROWS), ROWS))

        @pl.when((n_half & 1) != 0)
        def _(tile=tile, n_half=n_half):
            tile(pl.ds(pl.multiple_of((n_half - 1) * half, half), half))

        lax.fori_loop(n_half, y_ref.shape[0] // half, blank, 0)


def _combine_kernel(s_ref, y_ref, posm_ref, x1_ref, mod_ref, o_ref, yw_s, sel_s, acc_s, sem,
                    *, n_blk, cap_pad, common_rows):
    for t in range(STEP_BLOCKS):
        rows = pl.ds(t * ROWS, ROWS)
        _combine_block(pl.program_id(0) * STEP_BLOCKS + t, t, s_ref, y_ref, posm_ref.at[t], x1_ref.at[rows, :],
                       mod_ref, o_ref.at[rows, :], yw_s, sel_s, acc_s, sem, n_blk, cap_pad, common_rows)


def _combine_block(b, buf, s_ref, y_ref, posm_ref, x1_ref, mod_ref, o_ref, yw_s, sel_s, acc_s, sem, n_blk, cap_pad,
                   common_rows):
    base, units, first_unit, n_units = _block_units(s_ref, b)

    def unit_rows(first_unit_e, u):
        return pl.ds(pl.multiple_of((first_unit_e + u) * WIN, WIN), WIN)

    def bounds(base_e, u):
        lo = base_e + u * WIN
        return lo, jnp.minimum(lo, cap_pad - WIN)

    def copies(slot, blk, fn):
        blk_base, blk_units, blk_first, _ = _block_units(s_ref, blk)

        def one(e, u):
            src = pl.ds(pl.multiple_of(bounds(blk_base[e], u)[1], ALIGN), WIN)
            fn(pltpu.make_async_copy(y_ref.at[e, src, :], yw_s.at[slot, unit_rows(blk_first[e], u), :],
                                     sem.at[slot]))

        _for_units(blk_units, one)

    @pl.when(b == 0)
    def _():
        copies(0, 0, lambda c: c.start())

    @pl.when(b + 1 < n_blk)
    def _():
        copies(1 - buf, b + 1, lambda c: c.start())

    def pick(e, u):
        lo, first = bounds(base[e], u)
        sel_s[unit_rows(first_unit[e], u), :] = _window_onehot(posm_ref, e, first, lo).astype(BF16)

    _for_units(units, pick)
    n_chunks = jnp.maximum(_ceil_div_pow2(n_units * WIN, ROWS), common_rows // ROWS)

    def blank(u, _):
        sel_s[unit_rows(u, 0), :] = jnp.zeros((WIN, ROWS), BF16)
        yw_s[buf, unit_rows(u, 0), :] = jnp.zeros((WIN, D_MODEL), BF16)
        return 0

    copies(buf, b, lambda c: c.wait())
    lax.fori_loop(n_units, n_chunks * (ROWS // WIN), blank, 0)

    def weighted(rows):
        return _dot(sel_s[rows, :].T, yw_s[buf, rows, :])

    def add_chunk(c, _):
        acc_s[...] += weighted(pl.ds(pl.multiple_of(c * ROWS, ROWS), ROWS))
        return 0

    acc_s[...] = weighted(pl.ds(0, common_rows))
    lax.fori_loop(common_rows // ROWS, n_chunks, add_chunk, 0)
    o_ref[...] = x1_ref[...] + mod_ref[5:6, :] * acc_s[...]


def _padded_capacity(cap, n_blk):
    return -(-(cap + (ALIGN - 1) * n_blk + WIN) // ROWS) * ROWS


def _step_spec(*shape):
    return pl.BlockSpec((STEP_BLOCKS,) + shape, lambda i, s: (i,) + (0,) * len(shape))


def _common_rows(n_tok):
    return COMMON_ROWS + (ROWS if n_tok > ROWS else 0)


def _dispatch(h2, posm, s_tab, cap, n_tok):
    n_blk = h2.shape[0] // ROWS
    cap_pad = _padded_capacity(cap, n_blk)
    return pl.pallas_call(
        functools.partial(_dispatch_kernel, n_blk=n_blk, cap_pad=cap_pad, common_rows=_common_rows(n_tok)),
        out_shape=jax.ShapeDtypeStruct((N_EXPERTS, cap_pad, XE_W), BF16),
        grid_spec=pltpu.PrefetchScalarGridSpec(
            num_scalar_prefetch=1, grid=(n_blk // STEP_BLOCKS,),
            in_specs=[pl.BlockSpec((STEP_BLOCKS * ROWS, XE_W), lambda i, s: (i, 0)), _step_spec(N_EXPERTS, ROWS)],
            out_specs=pl.BlockSpec(memory_space=pl.ANY),
            scratch_shapes=[pltpu.VMEM((MAX_UNITS * WIN, ROWS), BF16), pltpu.VMEM((2, MAX_UNITS * WIN, XE_W), BF16),
                            pltpu.SemaphoreType.DMA((2, N_EXPERTS))]),
        compiler_params=_params(1), name="dispatch",
    )(s_tab, h2, posm)


def _ffn(xes, s_tabs, wg, wu, wd):
    n_sets = len(xes)
    per_expert = lambda rows, cols: pl.BlockSpec((None, rows, cols), lambda e, *s: (e, 0, 0))
    return pl.pallas_call(
        functools.partial(_ffn_kernel, n_blks=tuple(s.shape[1] - 1 for s in s_tabs)),
        out_shape=[jax.ShapeDtypeStruct(xe.shape[:2] + (D_MODEL,), BF16) for xe in xes],
        grid_spec=pltpu.PrefetchScalarGridSpec(
            num_scalar_prefetch=n_sets, grid=(N_EXPERTS,),
            in_specs=[per_expert(xe.shape[1], XE_W) for xe in xes]
            + [per_expert(D_MODEL, EXPERT_FF), per_expert(D_MODEL, EXPERT_FF), per_expert(EXPERT_FF, D_MODEL)],
            out_specs=[per_expert(xe.shape[1], D_MODEL) for xe in xes],
            scratch_shapes=[pltpu.VMEM((D_MODEL, EXPERT_FF), BF16), pltpu.VMEM((D_MODEL, EXPERT_FF), BF16),
                            pltpu.VMEM((EXPERT_FF, D_MODEL), BF16)]),
        compiler_params=_params(1), name="ffn",
    )(*s_tabs, *xes, wg, wu, wd)


def _combine(y, posm, s_tab, x1, mod, n_tok):
    n_all = x1.shape[0]
    n_blk = n_all // ROWS
    shared_mod = mod.shape[0] == 1
    assert shared_mod or n_tok % (STEP_BLOCKS * ROWS) == 0
    steps_per_seq = max(n_tok // (STEP_BLOCKS * ROWS), 1)
    mod_map = (lambda i, s: (0, 0, 0)) if shared_mod else (lambda i, s: (i // steps_per_seq, 0, 0))
    rows = pl.BlockSpec((STEP_BLOCKS * ROWS, D_MODEL), lambda i, s: (i, 0))
    return pl.pallas_call(
        functools.partial(_combine_kernel, n_blk=n_blk, cap_pad=y.shape[1], common_rows=_common_rows(n_tok)),
        out_shape=jax.ShapeDtypeStruct((n_all, D_MODEL), F32),
        grid_spec=pltpu.PrefetchScalarGridSpec(
            num_scalar_prefetch=1, grid=(n_blk // STEP_BLOCKS,),
            in_specs=[pl.BlockSpec(memory_space=pl.ANY), _step_spec(N_EXPERTS, ROWS), rows,
                      pl.BlockSpec((None, 6, D_MODEL), mod_map)],
            out_specs=rows,
            scratch_shapes=[pltpu.VMEM((2, MAX_UNITS * WIN, D_MODEL), BF16), pltpu.VMEM((MAX_UNITS * WIN, ROWS), BF16),
                            pltpu.VMEM((ROWS, D_MODEL), F32), pltpu.SemaphoreType.DMA((2,))]),
        compiler_params=_params(1), name="combine",
    )(s_tab, y, posm, x1, mod)


def _rope_tables(n_tok):
    rows = n_tok // GRID_W
    row = np.repeat(np.arange(rows, dtype=np.float64), GRID_W)
    col = np.tile(np.arange(GRID_W, dtype=np.float64), rows)
    axis_dim = QK_ROPE // 2
    inv_freq = ROPE_THETA ** (-np.arange(0, axis_dim, 2, dtype=np.float64) / axis_dim)
    ang = np.concatenate([row[:, None] * inv_freq, col[:, None] * inv_freq], axis=-1)
    cos = np.ones((n_tok, LANES))
    sin = np.zeros((n_tok, LANES))
    cos[:, ROPE_LANE0:ROPE_LANE0 + QK_ROPE] = np.repeat(np.cos(ang), 2, axis=1)
    sgn = np.tile(np.array([-1.0, 1.0]), QK_ROPE // 2)
    sin[:, ROPE_LANE0:ROPE_LANE0 + QK_ROPE] = np.repeat(np.sin(ang), 2, axis=1) * sgn
    return jnp.asarray(cos, F32), jnp.asarray(sin, F32)


def _dft_tables(n_tok):
    def cs(n):
        k = np.arange(n)
        ang = 2.0 * np.pi * ((k[:, None] * k[None, :]) % n) / n
        return np.cos(ang), np.sin(ang)
    cc, sc = cs(FNET_CH)
    cn, sn = cs(n_tok)
    half = n_tok // 2
    used = half if n_tok > ROWS else n_tok
    folded = np.concatenate([cn[:, :used], -sn[:, :used]], axis=1)
    flip = np.fliplr(np.eye(min(ROWS, half)))
    return tuple(jnp.asarray(t, F32) for t in (cc, sc, folded)) + (jnp.asarray(flip, BF16),)


def _pad_heads(w, width):
    lead = w.shape[:-1]
    w = w.reshape(lead + (N_HEADS, width))
    w = jnp.pad(w, [(0, 0)] * len(lead) + [(0, 0), (0, HEAD_PAD - width)])
    return w.reshape(lead + (N_HEADS * HEAD_PAD,))


def _pair_swap_lanes(n_groups):
    perm = np.arange(n_groups * LANES).reshape(n_groups, LANES)
    rot = perm[:, ROPE_LANE0:ROPE_LANE0 + QK_ROPE].reshape(n_groups, QK_ROPE // 2, 2)[:, :, ::-1]
    perm[:, ROPE_LANE0:ROPE_LANE0 + QK_ROPE] = rot.reshape(n_groups, QK_ROPE)
    return perm.reshape(-1)


def _spare_from_rotary(t, src=None):
    src = t if src is None else src
    lead = t.shape[:-1]
    t, src = t.reshape(lead + (-1, LANES)), src.reshape(lead + (-1, LANES))
    t = jnp.concatenate([t[..., :QK_DIM], src[..., ROPE_LANE0:ROPE_LANE0 + QK_ROPE]], axis=-1)
    return t.reshape(lead + (-1,))


def _head_tables(gain, rope_tabs, n_tok, query):
    if rope_tabs is None:
        tab = jnp.broadcast_to(gain, (n_tok, LANES))
        return (tab if query else _spare_from_rotary(tab),)
    cos, sin = rope_tabs
    a, b = gain * cos, gain[:, _pair_swap_lanes(1)] * sin
    return (_spare_from_rotary(a, b),) if query else (_spare_from_rotary(a), _spare_from_rotary(b))


def _layer0_weights(norm1, norm2, w_in, q_a_norm, w_q_up, q_norm, kv_a_norm, w_kv_up, k_norm, conv_w, w_o):
    c0 = Q_LORA + KV_LORA
    rope_cols = jnp.pad(w_in[:, c0:c0 + QK_ROPE], ((0, 0), (ROPE_LANE0, LANES - ROPE_LANE0 - QK_ROPE)))
    w_in_pad = jnp.concatenate([w_in[:, :c0], _spare_from_rotary(rope_cols),
                                _spare_from_rotary(rope_cols[:, _pair_swap_lanes(1)]),
                                w_in[:, c0 + QK_ROPE:]], axis=1)
    kv = w_kv_up.reshape(KV_LORA, N_HEADS, QK_NOPE + V_HEAD)
    w_q = _pad_heads(w_q_up, QK_DIM)
    w_q = _spare_from_rotary(w_q, w_q[:, _pair_swap_lanes(N_HEADS)])
    head_gain = lambda g: jnp.pad(g, (0, HEAD_PAD - QK_DIM)).reshape(1, -1)
    return dict(
        norm1=norm1.reshape(1, -1), norm2=norm2.reshape(1, -1), w_in=w_in_pad.astype(BF16),
        q_a_norm=q_a_norm.reshape(1, -1), w_q=w_q.astype(BF16),
        q_gain=head_gain(q_norm) * (QK_DIM ** -0.5), kv_a_norm=kv_a_norm.reshape(1, -1),
        w_k=_pad_heads(kv[:, :, :QK_NOPE].reshape(KV_LORA, -1), QK_NOPE).astype(BF16),
        w_v=kv[:, :, QK_NOPE:].reshape(KV_LORA, -1).astype(BF16),
        w_v_wide=_pad_heads(kv[:, :, QK_NOPE:].reshape(KV_LORA, -1), V_HEAD).astype(BF16), k_gain=head_gain(k_norm),
        conv_w=conv_w, w_o=w_o.astype(BF16))


def _router_weights(w_router):
    hi, lo = _split_hi_lo(jnp.pad(w_router, ((0, 0), (0, LANES - N_EXPERTS))))
    return dict(wr_hi=hi, wr_lo=lo)


def _moe(sets, tri, w):
    routed = []
    for x1, h2, aff, mod, n_tok in sets:
        cap = CAPACITY_FACTOR * x1.shape[0] // N_EXPERTS
        posm, s_tab = _route(aff, tri, cap)
        routed.append((posm, s_tab, _dispatch(h2, posm, s_tab, cap, n_tok)))
    ys = _ffn([r[2] for r in routed], [r[1] for r in routed], w["wg"], w["wu"], w["wd"])
    return [_combine(y, posm, s_tab, x1, mod, n_tok)
            for y, (posm, s_tab, _), (x1, _, _, mod, n_tok) in zip(ys, routed, sets)]


def _mixer0(x, mod, l0, ctx, rope_tabs, state):
    n_b, n_tok, _ = x.shape
    n_keys = n_tok + (0 if ctx is None else ctx[0].shape[1])
    w_v = l0["w_v_wide"] if n_keys > WIDE_ROWS else l0["w_v"]
    q, k, v, conv, *new_state = _front0(x, mod, l0, rope_tabs, w_v, state)
    if state:
        new_state = (new_state[0], new_state[1][:, :, ROPE_LANE0:ROPE_LANE0 + QK_ROPE])
    if ctx is not None:
        kc, vc = _ctx_kv(ctx[0], ctx[1], l0, w_v)
        attn = _attention(q, k, v, kc, vc)
    else:
        attn = _attention(q, k, v)
    flat = lambda a: a.reshape(n_b * n_tok, a.shape[-1])
    return _post0(flat(x), flat(attn), flat(conv), mod, l0, n_tok), new_state


def kernel(x_prompt, x_sample, c, cache_c_kv_l0, cache_k_rope_l0, c_ctx, norm1_l0, norm2_l0, w_mod_l0, b_mod_l0, w_in_l0, q_a_norm_l0, w_q_up_l0, q_norm_l0, kv_a_norm_l0, w_kv_up_l0, k_norm_l0, conv_w_l0, w_o_l0, w_router_l0, w_gate_l0, w_up_l0, w_down_l0, norm1_l1, norm2_l1, w_mod_l1, b_mod_l1, w_f_l1, w_router_l1, w_gate_l1, w_up_l1, w_down_l1):
    n_dec = c.shape[0]
    cond = jnp.concatenate([c_ctx[None, :], c, jnp.zeros((16 - 1 - n_dec, D_MODEL), F32)], axis=0)
    m0 = _modulation(cond, w_mod_l0, b_mod_l0)
    m1 = _modulation(cond, w_mod_l1, b_mod_l1)
    mods_prompt = (m0[0:1], m1[0:1])
    mods_sample = (m0[1:1 + n_dec], m1[1:1 + n_dec])

    l0 = _layer0_weights(norm1_l0, norm2_l0, w_in_l0, q_a_norm_l0, w_q_up_l0, q_norm_l0, kv_a_norm_l0,
                         w_kv_up_l0, k_norm_l0, conv_w_l0, w_o_l0)
    l0.update(_router_weights(w_router_l0))
    l0.update(wg=w_gate_l0, wu=w_up_l0, wd=w_down_l0)
    l1 = dict(norm1=norm1_l1.reshape(1, -1), norm2=norm2_l1.reshape(1, -1), w_f=w_f_l1.astype(BF16))
    l1.update(_router_weights(w_router_l1))
    l1.update(wg=w_gate_l1, wu=w_up_l1, wd=w_down_l1)

    tri = jnp.asarray(np.triu(np.ones((ROWS, ROWS)), 1), BF16)
    k_rope_pad = _spare_from_rotary(
        jnp.pad(cache_k_rope_l0, ((0, 0), (0, 0), (ROPE_LANE0, LANES - ROPE_LANE0 - QK_ROPE))))
    xs = (x_prompt, x_sample)
    mods = (mods_prompt, mods_sample)
    n_toks = tuple(x.shape[1] for x in xs)

    (front_p, (new_c_kv, new_k_rope)) = _mixer0(x_prompt, mods_prompt[0], l0, None, None, True)
    (front_s, _) = _mixer0(x_sample, mods_sample[0], l0, (cache_c_kv_l0, k_rope_pad), _rope_tables(n_toks[1]),
                           False)
    ys = _moe([tuple(front) + (mod[0], n_tok) for front, mod, n_tok in zip((front_p, front_s), mods, n_toks)],
              tri, l0)

    sets = []
    for y, x, mod, n_tok in zip(ys, xs, mods, n_toks):
        x1, h2, aff = _fnet(y.reshape(x.shape), mod[1], l1, _dft_tables(n_tok))
        sets.append((x1.reshape(-1, D_MODEL), h2.reshape(-1, XE_W), aff, mod[1], n_tok))
    y_prompt, y_sample = (y.reshape(x.shape) for y, x in zip(_moe(sets, tri, l1), xs))
    return (y_prompt, y_sample, new_c_kv, new_k_rope)
```

```python
import functools

import jax
import jax.numpy as jnp
import numpy as np
from jax import lax
from jax.experimental import pallas as pl
from jax.experimental.pallas import tpu as pltpu

D_MODEL = 1024
GRID_W = 64
N_HEADS = 8
QK_NOPE = 64
QK_ROPE = 32
QK_DIM = QK_NOPE + QK_ROPE
V_HEAD = 64
Q_LORA = 384
KV_LORA = 256
CONV_CH = 512
FNET_GROUPS = 4
FNET_CH = D_MODEL // FNET_GROUPS
N_EXPERTS = 16
EXPERT_FF = 512
CAPACITY_FACTOR = 2
ROPE_THETA = 10000.0
EPS = 1e-6

LANES = 128
HEAD_PAD = LANES
ROWS = 256
WIDE_ROWS = 512
MIX_ROWS = 1024
ATTN_ROWS = 1024
ALIGN = 8
WIN = 64
MAX_UNITS = N_EXPERTS * (ROWS // WIN)
COMMON_ROWS = N_EXPERTS * WIN
STEP_BLOCKS = 2
XE_W = D_MODEL + LANES
IN0_PAD = Q_LORA + KV_LORA + 2 * LANES + 3 * CONV_CH
ROPE_LANE0 = QK_NOPE
VMEM_LIMIT = 56 * 1024 * 1024

F32 = jnp.float32
BF16 = jnp.bfloat16


def _dot(a, b):
    return jnp.dot(a, b, preferred_element_type=F32)


def _dot_nt(a, b):
    return lax.dot_general(a, b, (((1,), (1,)), ((), ())), preferred_element_type=F32)


def _split_hi_lo(x):
    hi = x.astype(BF16)
    lo = (x - hi.astype(F32)).astype(BF16)
    return hi, lo


def _params(n_axes):
    return pltpu.CompilerParams(dimension_semantics=("arbitrary",) * n_axes,
                                vmem_limit_bytes=VMEM_LIMIT)


def _rms(x, gain):
    return x * lax.rsqrt(jnp.mean(x * x, axis=-1, keepdims=True) + EPS) * gain


def _mod_kernel(cond_ref, w_ref, b_ref, o_ref):
    c = cond_ref[...]
    s = c * (1.0 / (1.0 + jnp.exp(-c)))
    s_hi, s_lo = _split_hi_lo(s)
    w = w_ref[...].astype(BF16)
    o_ref[...] = _dot(s_hi, w) + _dot(s_lo, w) + b_ref[...]


def _modulation(cond, w_mod, b_mod):
    n_rows = cond.shape[0]
    tn = 1536
    out = pl.pallas_call(
        _mod_kernel,
        out_shape=jax.ShapeDtypeStruct((n_rows, 6 * D_MODEL), F32),
        grid=(6 * D_MODEL // tn,),
        in_specs=[pl.BlockSpec((n_rows, D_MODEL), lambda i: (0, 0)),
                  pl.BlockSpec((D_MODEL, tn), lambda i: (0, i)),
                  pl.BlockSpec((1, tn), lambda i: (0, i))],
        out_specs=pl.BlockSpec((n_rows, tn), lambda i: (0, i)),
        compiler_params=_params(1),
        name="modulation",
    )(cond, w_mod, b_mod.reshape(1, -1))
    return out.reshape(n_rows, 6, D_MODEL)


def _head_norm_rope(xh, gain_cos, swapped_sin):
    lane = lax.broadcasted_iota(jnp.int32, (1, HEAD_PAD), 1)
    ss = jnp.sum(jnp.where(lane < QK_DIM, xh * xh, 0.0), axis=-1, keepdims=True) * (1.0 / QK_DIM)
    y = xh * gain_cos
    if swapped_sin is not None:
        y = y + swapped_sin
    return y * lax.rsqrt(ss + EPS)


def _expand_kv(ckv_bf, r, swapped_sin, wk_ref, wv_ref, gain_cos, k_ref, v_ref, rows):
    kf = _dot(ckv_bf, wk_ref[...])
    vf = _dot(ckv_bf, wv_ref[...])
    if v_ref.shape[-1] == N_HEADS * HEAD_PAD:
        lane = lax.broadcasted_iota(jnp.int32, (1, N_HEADS * HEAD_PAD), 1)
        vf = vf + jnp.where((lane & (HEAD_PAD - 1)) == V_HEAD, 1.0, 0.0)
    v_ref[rows, :] = vf.astype(BF16)
    for h in range(N_HEADS):
        sl = slice(h * HEAD_PAD, (h + 1) * HEAD_PAD)
        k_ref[rows, sl] = _head_norm_rope(kf[:, sl] + r, gain_cos, swapped_sin).astype(BF16)


def _front0_kernel(*refs, n_tok, rope, state):
    (x_ref, mod_ref, n1_ref, win_ref, qan_ref, wq_ref, kvan_ref, wk_ref, wv_ref, cw_ref,
     qa_ref, ka_ref) = refs[:12]
    pos = 12
    if rope:
        kb_ref = refs[pos]
        pos += 1
    q_ref, k_ref, v_ref, conv_ref = refs[pos:pos + 4]
    if state:
        ckv_ref, kr_ref = refs[pos + 4:pos + 6]
    cu_s, gb_s = refs[-2:]
    sh1, sc1 = mod_ref[0:1, :], mod_ref[1:2, :]
    c_rope = Q_LORA + KV_LORA
    wide = min(n_tok, MIX_ROWS)
    for c in range(n_tok // ROWS):
        rows = pl.ds(c * ROWS, ROWS)
        if (c * ROWS) % wide == 0:
            wide_rows = pl.ds(c * ROWS, wide)
            h = _rms(x_ref[wide_rows, :], n1_ref[...]) * (1.0 + sc1) + sh1
            proj_wide = _dot(h.astype(BF16), win_ref[...])
        off = (c * ROWS) % wide
        proj = proj_wide[off:off + ROWS, :]
        cq = _rms(proj[:, :Q_LORA], qan_ref[...]).astype(BF16)
        qf = _dot(cq, wq_ref[...])
        for hd in range(N_HEADS):
            sl = slice(hd * HEAD_PAD, (hd + 1) * HEAD_PAD)
            q_ref[rows, sl] = _head_norm_rope(qf[:, sl], qa_ref[rows, :], None).astype(BF16)
        ckv = _rms(proj[:, Q_LORA:c_rope], kvan_ref[...])
        r = proj[:, c_rope:c_rope + LANES]
        if state:
            ckv_ref[rows, :] = ckv
            kr_ref[rows, :] = r
        k_sin = proj[:, c_rope + LANES:c_rope + 2 * LANES] * kb_ref[rows, :] if rope else None
        _expand_kv(ckv.astype(BF16), r, k_sin, wk_ref, wv_ref, ka_ref[rows, :], k_ref, v_ref, rows)
        c0 = c_rope + 2 * LANES
        gb_s[rows, :] = proj[:, c0:c0 + CONV_CH]
        cu_s[rows, :] = proj[:, c0 + CONV_CH:c0 + 2 * CONV_CH] * proj[:, c0 + 2 * CONV_CH:c0 + 3 * CONV_CH]
    cu = cu_s[...]
    row = lax.broadcasted_iota(jnp.int32, cu.shape, 0)
    prev = jnp.where(row == 0, 0.0, pltpu.roll(cu, 1, 0))
    nxt = jnp.where(row == n_tok - 1, 0.0, pltpu.roll(cu, n_tok - 1, 0))
    conv = gb_s[...] * (cw_ref[0:1, :] * prev + cw_ref[1:2, :] * cu + cw_ref[2:3, :] * nxt)
    conv_ref[...] = conv.astype(BF16)


def _const_spec(shape):
    return pl.BlockSpec(shape, lambda b: (0,) * len(shape))


def _front0(x, mod, w, rope_tabs, w_v, state):
    n_b, n_tok, _ = x.shape
    v_width = w_v.shape[1]
    shared_mod = mod.shape[0] == 1
    seq = lambda width: pl.BlockSpec((None, n_tok, width), lambda b: (b, 0, 0))
    in_specs = [seq(D_MODEL),
                pl.BlockSpec((None, 6, D_MODEL), (lambda b: (0, 0, 0)) if shared_mod else (lambda b: (b, 0, 0))),
                _const_spec((1, D_MODEL)), _const_spec((D_MODEL, IN0_PAD)), _const_spec((1, Q_LORA)),
                _const_spec((Q_LORA, N_HEADS * HEAD_PAD)),
                _const_spec((1, KV_LORA)), _const_spec((KV_LORA, N_HEADS * HEAD_PAD)),
                _const_spec((KV_LORA, v_width)), _const_spec((3, CONV_CH)),
                _const_spec((n_tok, LANES)), _const_spec((n_tok, LANES))]
    q_tabs = _head_tables(w["q_gain"], rope_tabs, n_tok, True)
    k_tabs = _head_tables(w["k_gain"], rope_tabs, n_tok, False)
    args = [x, mod, w["norm1"], w["w_in"], w["q_a_norm"], w["w_q"], w["kv_a_norm"],
            w["w_k"], w_v, w["conv_w"], q_tabs[0], k_tabs[0]]
    if rope_tabs is not None:
        in_specs += [_const_spec((n_tok, LANES))]
        args += [k_tabs[1]]
    out_shape = [jax.ShapeDtypeStruct((n_b, n_tok, N_HEADS * HEAD_PAD), BF16),
                 jax.ShapeDtypeStruct((n_b, n_tok, N_HEADS * HEAD_PAD), BF16),
                 jax.ShapeDtypeStruct((n_b, n_tok, v_width), BF16),
                 jax.ShapeDtypeStruct((n_b, n_tok, CONV_CH), BF16)]
    out_specs = [seq(N_HEADS * HEAD_PAD), seq(N_HEADS * HEAD_PAD), seq(v_width), seq(CONV_CH)]
    if state:
        out_shape += [jax.ShapeDtypeStruct((n_b, n_tok, KV_LORA), F32),
                      jax.ShapeDtypeStruct((n_b, n_tok, LANES), F32)]
        out_specs += [seq(KV_LORA), seq(LANES)]
    return pl.pallas_call(
        functools.partial(_front0_kernel, n_tok=n_tok, rope=rope_tabs is not None, state=state),
        out_shape=out_shape, grid=(n_b,), in_specs=in_specs, out_specs=out_specs,
        scratch_shapes=[pltpu.VMEM((n_tok, CONV_CH), F32), pltpu.VMEM((n_tok, CONV_CH), F32)],
        compiler_params=_params(1), name="front0",
    )(*args)


def _ctx_kv_kernel(ckv_ref, r_ref, wk_ref, wv_ref, ka_ref, k_ref, v_ref):
    rows = pl.ds(0, ckv_ref.shape[0])
    _expand_kv(ckv_ref[...].astype(BF16), r_ref[...], None, wk_ref, wv_ref, ka_ref[...], k_ref, v_ref, rows)


def _ctx_kv(cache_c_kv, cache_k_rope_pad, w, w_v):
    n_b, n_ctx, _ = cache_c_kv.shape
    v_width = w_v.shape[1]
    seq = lambda width: pl.BlockSpec((None, n_ctx, width), lambda b: (b, 0, 0))
    return pl.pallas_call(
        _ctx_kv_kernel,
        out_shape=[jax.ShapeDtypeStruct((n_b, n_ctx, N_HEADS * HEAD_PAD), BF16),
                   jax.ShapeDtypeStruct((n_b, n_ctx, v_width), BF16)],
        grid=(n_b,),
        in_specs=[seq(KV_LORA), seq(LANES), _const_spec((KV_LORA, N_HEADS * HEAD_PAD)),
                  _const_spec((KV_LORA, v_width)), _const_spec((n_ctx, LANES))],
        out_specs=[seq(N_HEADS * HEAD_PAD), seq(v_width)],
        compiler_params=_params(1), name="ctx_kv",
    )(cache_c_kv, cache_k_rope_pad, w["w_k"], w_v, _head_tables(w["k_gain"], None, n_ctx, False)[0])


def _attn_kernel(*refs, with_ctx):
    if with_ctx:
        q_ref, k_ref, v_ref, kc_ref, vc_ref, o_ref = refs
    else:
        q_ref, k_ref, v_ref, o_ref = refs
    ones_lane = v_ref.shape[-1] == N_HEADS * HEAD_PAD
    lane = lax.broadcasted_iota(jnp.int32, (q_ref.shape[0], LANES), 1)
    for pair in range(N_HEADS // 2):
        outs = []
        for hd in (2 * pair, 2 * pair + 1):
            sl = slice(hd * HEAD_PAD, (hd + 1) * HEAD_PAD)
            vsl = sl if ones_lane else slice(pair * LANES, (pair + 1) * LANES)
            qh = q_ref[:, sl]
            s = _dot_nt(qh, k_ref[:, sl])
            m = jnp.max(s, axis=-1, keepdims=True)
            if with_ctx:
                sc = _dot_nt(qh, kc_ref[:, sl])
                m = jnp.maximum(m, jnp.max(sc, axis=-1, keepdims=True))
            p = jnp.exp(s - m)
            o = _dot(p.astype(BF16), v_ref[:, vsl])
            if with_ctx:
                pc = jnp.exp(sc - m)
                o = o + _dot(pc.astype(BF16), vc_ref[:, vsl])
            if ones_lane:
                den = o[:, V_HEAD:V_HEAD + 1]
            else:
                den = jnp.sum(p, axis=-1, keepdims=True)
                if with_ctx:
                    den = den + jnp.sum(pc, axis=-1, keepdims=True)
            outs.append(o / den)
        odd = pltpu.roll(outs[1], V_HEAD, 1) if ones_lane else outs[1]
        o_ref[:, pair * LANES:(pair + 1) * LANES] = jnp.where(lane < V_HEAD, outs[0], odd).astype(BF16)


def _attention(q, k, v, kc=None, vc=None):
    n_b, n_tok, _ = q.shape
    with_ctx = kc is not None
    tq = min(n_tok, ATTN_ROWS)
    qspec = lambda width: pl.BlockSpec((None, tq, width), lambda b, i: (b, i, 0))
    kvspec = lambda a: pl.BlockSpec((None,) + a.shape[1:], lambda b, i: (b, 0, 0))
    in_specs = [qspec(N_HEADS * HEAD_PAD), kvspec(k), kvspec(v)]
    args = [q, k, v]
    if with_ctx:
        in_specs += [kvspec(kc), kvspec(vc)]
        args += [kc, vc]
    return pl.pallas_call(
        functools.partial(_attn_kernel, with_ctx=with_ctx),
        out_shape=jax.ShapeDtypeStruct((n_b, n_tok, N_HEADS * V_HEAD), BF16),
        grid=(n_b, n_tok // tq), in_specs=in_specs, out_specs=qspec(N_HEADS * V_HEAD),
        compiler_params=_params(2), name="attention",
    )(*args)


def _moe_front(x1, mod_ref, n2_ref, wrh_ref, wrl_ref, h2_ref, aff_ref, rows):
    sh2, sc2 = mod_ref[3:4, :], mod_ref[4:5, :]
    h2 = _rms(x1, n2_ref[...]) * (1.0 + sc2) + sh2
    h_hi, h_lo = _split_hi_lo(h2)
    logit = _dot(h_hi, wrh_ref[...]) + _dot(h_lo, wrh_ref[...]) + _dot(h_hi, wrl_ref[...])
    real = lax.broadcasted_iota(jnp.int32, (1, LANES), 1) < N_EXPERTS
    top = jnp.max(jnp.where(real, logit, -jnp.inf), axis=1, keepdims=True)
    e = jnp.where(real, jnp.exp(logit - top), 0.0)
    a = e / jnp.sum(e, axis=1, keepdims=True)
    a_hi = a.astype(BF16).astype(F32)
    a_mid = (a - a_hi).astype(BF16).astype(F32)
    a_lo = a - a_hi - a_mid
    h2_ref[rows, 0:D_MODEL] = h_hi
    h2_ref[rows, D_MODEL:XE_W] = (a_hi + pltpu.roll(a_mid, N_EXPERTS, 1)
                                  + pltpu.roll(a_lo, 2 * N_EXPERTS, 1)).astype(BF16)
    aff_ref[...] = a.T[0:N_EXPERTS, :]


def _post0_kernel(x_ref, attn_ref, conv_ref, wo_ref, mod_ref, n2_ref, wrh_ref, wrl_ref,
                  x1_ref, h2_ref, aff_ref):
    n_attn = N_HEADS * V_HEAD
    mix = _dot(attn_ref[...], wo_ref[0:n_attn, :]) + _dot(conv_ref[...], wo_ref[n_attn:, :])
    for t in range(x_ref.shape[0] // ROWS):
        rows = pl.ds(t * ROWS, ROWS)
        x1 = x_ref[rows, :] + mod_ref[2:3, :] * mix[t * ROWS:(t + 1) * ROWS, :]
        x1_ref[rows, :] = x1
        _moe_front(x1, mod_ref, n2_ref, wrh_ref, wrl_ref, h2_ref, aff_ref.at[t], rows)


def _post0(x, attn, conv, mod, w, n_tok):
    n_all = x.shape[0]
    shared_mod = mod.shape[0] == 1
    rows = MIX_ROWS if shared_mod else min(n_tok, MIX_ROWS)
    tiles_per_seq = max(n_tok // rows, 1)
    tile = lambda width: pl.BlockSpec((rows, width), lambda i: (i, 0))
    const = lambda shape: pl.BlockSpec(shape, lambda i: (0,) * len(shape))
    mod_map = (lambda i: (0, 0, 0)) if shared_mod else (lambda i: (i // tiles_per_seq, 0, 0))
    return pl.pallas_call(
        _post0_kernel,
        out_shape=[jax.ShapeDtypeStruct((n_all, D_MODEL), F32),
                   jax.ShapeDtypeStruct((n_all, XE_W), BF16),
                   jax.ShapeDtypeStruct((n_all // ROWS, N_EXPERTS, ROWS), F32)],
        grid=(n_all // rows,),
        in_specs=[tile(D_MODEL), tile(N_HEADS * V_HEAD), tile(CONV_CH), const((D_MODEL, D_MODEL)),
                  pl.BlockSpec((None, 6, D_MODEL), mod_map), const((1, D_MODEL)),
                  const((D_MODEL, LANES)), const((D_MODEL, LANES))],
        out_specs=[tile(D_MODEL), tile(XE_W),
                   pl.BlockSpec((rows // ROWS, N_EXPERTS, ROWS), lambda i: (i, 0, 0))],
        compiler_params=_params(1), name="post0",
    )(x, attn, conv, w["w_o"], mod, w["norm2"], w["wr_hi"], w["wr_lo"])


def _fnet_kernel(x_ref, mod_ref, n1_ref, cc_ref, sc_ref, dft_ref, flip_ref, wf_ref, n2_ref, wrh_ref, wrl_ref,
                 x1_ref, h2_ref, aff_ref, h_s, r_s, y_s, *, n_tok):
    sh1, sc1, g1 = mod_ref[0:1, :], mod_ref[1:2, :], mod_ref[2:3, :]
    n_chunks = n_tok // ROWS
    half = n_tok // 2
    blk = flip_ref.shape[0]
    cc, sc = cc_ref[...].astype(BF16), sc_ref[...].astype(BF16)
    for c in range(n_chunks):
        rows = pl.ds(c * ROWS, ROWS)
        h_s[rows, :] = (_rms(x_ref[rows, :], n1_ref[...]) * (1.0 + sc1) + sh1).astype(BF16)
    n_flip = half // blk if n_tok > ROWS else 0
    for i in range(n_flip):
        src = h_s[pl.ds(half + (n_flip - 1 - i) * blk, blk), :]
        r_s[pl.ds(i * blk, blk), :] = _dot(flip_ref[...], src)
    if n_flip:
        mirror = pltpu.roll(r_s[...], 1, 0)
        row = lax.broadcasted_iota(jnp.int32, mirror.shape, 0)
        mirror = jnp.where(row == 0, 0.0, mirror)
        low = h_s[0:half, :].astype(F32)
        even, odd = (low + mirror).astype(BF16), (low - mirror).astype(BF16)
        mid = h_s[pl.ds(half, 16), :]
    else:
        even = odd = h_s[...]
    fold_rows = even.shape[0]
    mids = []
    for g in range(FNET_GROUPS):
        sl = slice(g * FNET_CH, (g + 1) * FNET_CH)
        y_s[0:fold_rows, sl] = _dot(even[:, sl], cc).astype(BF16)
        y_s[fold_rows:2 * fold_rows, sl] = _dot(odd[:, sl], sc).astype(BF16)
        if n_flip:
            mids.append(_dot(mid[:, sl], cc)[0:1, :])
    mid_row = jnp.concatenate(mids, axis=1) if n_flip else None
    scale = 1.0 / float(np.sqrt(n_tok * FNET_CH))
    wide = min(n_tok, MIX_ROWS)
    for c in range(n_chunks):
        rows = pl.ds(c * ROWS, ROWS)
        if (c * ROWS) % wide == 0:
            parity = lax.broadcasted_iota(jnp.int32, (wide, 1), 0) & 1
            f = _dot(dft_ref[pl.ds(c * ROWS, wide), :].astype(BF16), y_s[...])
            if n_flip:
                f = f + jnp.where(parity == 0, 1.0, -1.0) * mid_row
            f = f * scale
            mix = _dot(f.astype(BF16), wf_ref[...])
        off = (c * ROWS) % wide
        x1 = x_ref[rows, :] + g1 * mix[off:off + ROWS, :]
        x1_ref[rows, :] = x1
        _moe_front(x1, mod_ref, n2_ref, wrh_ref, wrl_ref, h2_ref, aff_ref.at[c], rows)


def _fnet(x, mod, w, dft):
    n_b, n_tok, _ = x.shape
    shared_mod = mod.shape[0] == 1
    seq = lambda width: pl.BlockSpec((None, n_tok, width), lambda b: (b, 0, 0))
    cc, sc, dft_n, flip = dft
    tiles = n_tok // ROWS
    return pl.pallas_call(
        functools.partial(_fnet_kernel, n_tok=n_tok),
        out_shape=[jax.ShapeDtypeStruct((n_b, n_tok, D_MODEL), F32),
                   jax.ShapeDtypeStruct((n_b, n_tok, XE_W), BF16),
                   jax.ShapeDtypeStruct((n_b * tiles, N_EXPERTS, ROWS), F32)],
        grid=(n_b,),
        in_specs=[seq(D_MODEL),
                  pl.BlockSpec((None, 6, D_MODEL), (lambda b: (0, 0, 0)) if shared_mod else (lambda b: (b, 0, 0))),
                  _const_spec((1, D_MODEL)), _const_spec((FNET_CH, FNET_CH)), _const_spec((FNET_CH, FNET_CH)),
                  pl.BlockSpec(dft_n.shape, lambda b: (0, 0), pipeline_mode=pl.Buffered(1)),
                  _const_spec(flip.shape), _const_spec((D_MODEL, D_MODEL)), _const_spec((1, D_MODEL)),
                  _const_spec((D_MODEL, LANES)), _const_spec((D_MODEL, LANES))],
        out_specs=[seq(D_MODEL), seq(XE_W), pl.BlockSpec((tiles, N_EXPERTS, ROWS), lambda b: (b, 0, 0))],
        scratch_shapes=[pltpu.VMEM((n_tok, D_MODEL), BF16), pltpu.VMEM((n_tok // 2, D_MODEL), F32),
                        pltpu.VMEM((dft_n.shape[1], D_MODEL), BF16)],
        compiler_params=_params(1), name="fnet",
    )(x, mod, w["norm1"], cc, sc, dft_n, flip, w["w_f"], w["norm2"], w["wr_hi"], w["wr_lo"])


def _route_kernel(aff_ref, tri_ref, posm_ref, s_ref, *, n_blk, cap):
    def count(pred):
        acc = jnp.zeros((N_EXPERTS, ROWS), F32)
        for b in range(n_blk):
            acc = acc + jnp.where(pred(aff_ref[b]), 1.0, 0.0)
        return jnp.sum(acc, axis=1, keepdims=True)

    def as_f32(bits):
        return pltpu.bitcast(bits, F32)

    def search(_, carry):
        lo, hi = carry
        mid = lo + ((hi - lo) >> 1)
        mid_f = as_f32(mid)
        ok = count(lambda a: a >= mid_f) >= cap
        return jnp.where(ok, mid, lo), jnp.where(ok, hi, mid)

    one_bits = 0x3F800000
    lo0 = jnp.zeros((N_EXPERTS, 1), jnp.int32)
    hi0 = jnp.full((N_EXPERTS, 1), one_bits + 1, jnp.int32)
    lo, hi = lax.fori_loop(0, 31, search, (lo0, hi0))
    lo_f, ub = as_f32(lo), as_f32(hi)
    thr = lo_f
    pending = jnp.ones((N_EXPERTS, 1), F32)
    for _ in range(3):
        cur = jnp.full((N_EXPERTS, ROWS), -1.0, F32)
        for b in range(n_blk):
            a = aff_ref[b]
            cur = jnp.maximum(cur, jnp.where((a >= lo_f) & (a < ub), a, -1.0))
        cur = jnp.max(cur, axis=1, keepdims=True)
        take = (count(lambda a: a >= cur) >= cap) & (pending > 0.0)
        thr = jnp.where(take, cur, thr)
        pending = jnp.where(take, 0.0, pending)
        ub = cur
    need = cap - count(lambda a: a > thr)

    lane = lax.broadcasted_iota(jnp.int32, (N_EXPERTS, LANES), 1)
    carry_tie = jnp.zeros((N_EXPERTS, 1), F32)
    carry_pos = jnp.zeros((N_EXPERTS, 1), F32)
    s_acc = jnp.zeros((N_EXPERTS, LANES), jnp.int32)
    for b in range(n_blk):
        v = aff_ref[b]
        eq = jnp.where(v == thr, 1.0, 0.0)
        tie_rank = _dot(eq.astype(BF16), tri_ref[...]) + carry_tie
        sel = jnp.where((v > thr) | ((v == thr) & (tie_rank < need)), 1.0, 0.0)
        pos = _dot(sel.astype(BF16), tri_ref[...]) + carry_pos
        posm_ref[b] = jnp.where(sel > 0.0, pos, -1.0)
        s_acc = jnp.where(lane == b, carry_pos.astype(jnp.int32), s_acc)
        carry_tie = carry_tie + jnp.sum(eq, axis=1, keepdims=True)
        n_sel = jnp.sum(sel, axis=1, keepdims=True)
        carry_pos = carry_pos + jnp.floor((n_sel + (ALIGN - 1)) * (1.0 / ALIGN)) * ALIGN
    s_ref[...] = jnp.where(lane == n_blk, carry_pos.astype(jnp.int32), s_acc)


def _route(aff, tri, cap):
    n_blk = aff.shape[0]
    assert n_blk < LANES
    full = lambda shape: pl.BlockSpec(shape, lambda i: (0,) * len(shape))
    posm, s_tab = pl.pallas_call(
        functools.partial(_route_kernel, n_blk=n_blk, cap=cap),
        out_shape=[jax.ShapeDtypeStruct((n_blk, N_EXPERTS, ROWS), F32),
                   jax.ShapeDtypeStruct((N_EXPERTS, LANES), jnp.int32)],
        grid=(1,),
        in_specs=[full((n_blk, N_EXPERTS, ROWS)), full((ROWS, ROWS))],
        out_specs=[full((n_blk, N_EXPERTS, ROWS)), full((N_EXPERTS, LANES))],
        compiler_params=_params(1), name="route",
    )(aff, tri)
    return posm, s_tab[:, :n_blk + 1]


def _ceil_div_pow2(x, d):
    return lax.shift_right_logical(x + (d - 1), d.bit_length() - 1)


def _block_units(s_ref, b):
    base = [s_ref[e, b] for e in range(N_EXPERTS)]
    units = [_ceil_div_pow2(s_ref[e, b + 1] - base[e], WIN) for e in range(N_EXPERTS)]
    first_unit, total = [], 0
    for e in range(N_EXPERTS):
        first_unit.append(total)
        total = total + units[e]
    return base, units, first_unit, total


def _for_units(units, fn):
    for e in range(N_EXPERTS):
        def body(u, _, e=e):
            fn(e, u)
            return 0
        lax.fori_loop(0, units[e], body, 0)


def _window_onehot(posm_ref, e, first, lo):
    slot = lax.broadcasted_iota(jnp.int32, (WIN, ROWS), 0) + first
    p = posm_ref[e:e + 1, :].astype(jnp.int32)
    return jnp.where((p == slot) & (p >= lo), 1.0, 0.0)


def _gate_lanes(e):
    lane = lax.broadcasted_iota(jnp.int32, (1, LANES), 1)
    return ((lane & (N_EXPERTS - 1)) == e) & (lane < 3 * N_EXPERTS)


def _dispatch_kernel(s_ref, h_ref, posm_ref, xe_ref, sel_s, x_s, sem, *, n_blk, cap_pad, common_rows):
    for t in range(STEP_BLOCKS):
        _dispatch_block(pl.program_id(0) * STEP_BLOCKS + t, t, s_ref, h_ref.at[pl.ds(t * ROWS, ROWS), :],
                        posm_ref.at[t], xe_ref, sel_s, x_s, sem, n_blk, cap_pad, common_rows)


def _dispatch_block(b, buf, s_ref, h_ref, posm_ref, xe_ref, sel_s, x_s, sem, n_blk, cap_pad, common_rows):
    base, units, first_unit, n_units = _block_units(s_ref, b)

    @pl.when(b == 0)
    def _():
        sel_s[...] = jnp.zeros_like(sel_s)

    def unit_rows(first_unit_e, u):
        return pl.ds(pl.multiple_of((first_unit_e + u) * WIN, WIN), WIN)

    def pick(e, u):
        first = base[e] + u * WIN
        sel_s[unit_rows(first_unit[e], u), :] = _window_onehot(posm_ref, e, first, first).astype(BF16)

    _for_units(units, pick)

    def gather(rows):
        x_s[buf, rows, :] = _dot(sel_s[rows, :], h_ref[...]).astype(BF16)

    def gather_chunk(c, _):
        gather(pl.ds(pl.multiple_of(c * ROWS, ROWS), ROWS))
        return 0

    gather(pl.ds(0, common_rows))
    lax.fori_loop(common_rows // ROWS, _ceil_div_pow2(n_units * WIN, ROWS), gather_chunk, 0)

    def unit_copy(slot, blk_base_e, blk_first_e, e, u):
        dst = pl.ds(pl.multiple_of(blk_base_e + u * WIN, ALIGN), WIN)
        return pltpu.make_async_copy(x_s.at[slot, unit_rows(blk_first_e, u), :], xe_ref.at[e, dst, :],
                                     sem.at[slot, e])

    def for_expert_units(n, fn):
        lax.fori_loop(0, n, lambda u, _: (fn(u), 0)[1], 0)

    prev_base, prev_units, prev_first, _ = _block_units(s_ref, jnp.maximum(b - 1, 0))
    for e in range(N_EXPERTS):
        for_expert_units(jnp.where(b > 0, prev_units[e], 0),
                         lambda u, e=e: unit_copy(1 - buf, prev_base[e], prev_first[e], e, u).wait())
        for_expert_units(units[e], lambda u, e=e: unit_copy(buf, base[e], first_unit[e], e, u).start(priority=e % 2))

    @pl.when(b == n_blk - 1)
    def _():
        for e in range(N_EXPERTS):
            for_expert_units(units[e], lambda u, e=e: unit_copy(buf, base[e], first_unit[e], e, u).wait())
        x_s[buf, 0:WIN, :] = jnp.zeros((WIN, XE_W), BF16)

        def tail(fn):
            for e in range(N_EXPERTS):
                total = s_ref[e, n_blk]
                n_win = (cap_pad - total) // WIN

                def wide(c, _):
                    row = pl.multiple_of(total + c * WIN, ALIGN)
                    fn(pltpu.make_async_copy(x_s.at[buf, pl.ds(0, WIN), :],
                                             xe_ref.at[e, pl.ds(row, WIN), :], sem.at[buf, e]))
                    return 0

                def narrow(c, _):
                    row = pl.multiple_of(total + n_win * WIN + c * ALIGN, ALIGN)
                    fn(pltpu.make_async_copy(x_s.at[buf, pl.ds(0, ALIGN), :],
                                             xe_ref.at[e, pl.ds(row, ALIGN), :], sem.at[buf, e]))
                    return 0

                lax.fori_loop(0, n_win, wide, 0)
                lax.fori_loop(0, (cap_pad - total - n_win * WIN) // ALIGN, narrow, 0)

        tail(lambda c: c.start())
        tail(lambda c: c.wait())


def _ffn_kernel(*refs, n_blks):
    n_sets = len(n_blks)
    s_refs, x_refs = refs[:n_sets], refs[n_sets:2 * n_sets]
    wg_ref, wu_ref, wd_ref = refs[2 * n_sets:2 * n_sets + 3]
    y_refs = refs[2 * n_sets + 3:3 * n_sets + 3]
    wg_s, wu_s, wd_s = refs[3 * n_sets + 3:]
    e = pl.program_id(0)
    wg_s[...] = wg_ref[...].astype(BF16)
    wu_s[...] = wu_ref[...].astype(BF16)
    wd_s[...] = wd_ref[...].astype(BF16)
    mine = _gate_lanes(e)

    half = ROWS // 2
    for s_ref, x_ref, y_ref, n_blk in zip(s_refs, x_refs, y_refs, n_blks):
        n_half = _ceil_div_pow2(s_ref[e, n_blk], half)

        def tile(rows, x_ref=x_ref, y_ref=y_ref):
            x = x_ref[rows, 0:D_MODEL]
            pieces = x_ref[rows, D_MODEL:XE_W].astype(F32)
            gate = jnp.sum(jnp.where(mine, pieces, 0.0), axis=1, keepdims=True)
            a = _dot(x, wg_s[...])
            u = _dot(x, wu_s[...])
            hid = a * (1.0 / (1.0 + jnp.exp(-a))) * u * gate
            y_ref[rows, :] = _dot(hid.astype(BF16), wd_s[...]).astype(BF16)

        def full(j, _, tile=tile):
            tile(pl.ds(pl.multiple_of(j * WIDE_ROWS, WIDE_ROWS), WIDE_ROWS))
            return 0

        def blank(j, _, y_ref=y_ref):
            y_ref[pl.ds(pl.multiple_of(j * half, half), half), :] = jnp.zeros((half, D_MODEL), BF16)
            return 0

        lax.fori_loop(0, lax.shift_right_logical(n_half, 2), full, 0)

        @pl.when((n_half & 2) != 0)
        def _(tile=tile, n_half=n_half):
            tile(pl.ds(pl.multiple_of((n_half & -4) * half, WIDE_ROWS), ROWS))

        @pl.when((n_half & 1) != 0)
        def _(tile=tile, n_half=n_half):
            tile(pl.ds(pl.multiple_of((n_half - 1) * half, half), half))

        lax.fori_loop(n_half, y_ref.shape[0] // half, blank, 0)


def _combine_kernel(s_ref, y_ref, posm_ref, x1_ref, mod_ref, o_ref, yw_s, sel_s, acc_s, sem,
                    *, n_blk, cap_pad, common_rows):
    for t in range(STEP_BLOCKS):
        rows = pl.ds(t * ROWS, ROWS)
        _combine_block(pl.program_id(0) * STEP_BLOCKS + t, t, s_ref, y_ref, posm_ref.at[t], x1_ref.at[rows, :],
                       mod_ref, o_ref.at[rows, :], yw_s, sel_s, acc_s, sem, n_blk, cap_pad, common_rows)


def _combine_block(b, buf, s_ref, y_ref, posm_ref, x1_ref, mod_ref, o_ref, yw_s, sel_s, acc_s, sem, n_blk, cap_pad,
                   common_rows):
    base, units, first_unit, n_units = _block_units(s_ref, b)

    def unit_rows(first_unit_e, u):
        return pl.ds(pl.multiple_of((first_unit_e + u) * WIN, WIN), WIN)

    def bounds(base_e, u):
        lo = base_e + u * WIN
        return lo, jnp.minimum(lo, cap_pad - WIN)

    def copies(slot, blk, fn):
        blk_base, blk_units, blk_first, _ = _block_units(s_ref, blk)

        def one(e, u):
            src = pl.ds(pl.multiple_of(bounds(blk_base[e], u)[1], ALIGN), WIN)
            fn(pltpu.make_async_copy(y_ref.at[e, src, :], yw_s.at[slot, unit_rows(blk_first[e], u), :],
                                     sem.at[slot]), e)

        _for_units(blk_units, one)

    @pl.when(b == 0)
    def _():
        copies(0, 0, lambda c, e: c.start(priority=e % 2))

    @pl.when(b + 1 < n_blk)
    def _():
        copies(1 - buf, b + 1, lambda c, e: c.start(priority=e % 2))

    def pick(e, u):
        lo, first = bounds(base[e], u)
        sel_s[unit_rows(first_unit[e], u), :] = _window_onehot(posm_ref, e, first, lo).astype(BF16)

    _for_units(units, pick)
    n_chunks = jnp.maximum(_ceil_div_pow2(n_units * WIN, ROWS), common_rows // ROWS)

    def blank(u, _):
        sel_s[unit_rows(u, 0), :] = jnp.zeros((WIN, ROWS), BF16)
        yw_s[buf, unit_rows(u, 0), :] = jnp.zeros((WIN, D_MODEL), BF16)
        return 0

    copies(buf, b, lambda c, e: c.wait())
    lax.fori_loop(n_units, n_chunks * (ROWS // WIN), blank, 0)

    def weighted(rows):
        return _dot(sel_s[rows, :].T, yw_s[buf, rows, :])

    def add_chunk(c, _):
        acc_s[...] += weighted(pl.ds(pl.multiple_of(c * ROWS, ROWS), ROWS))
        return 0

    acc_s[...] = weighted(pl.ds(0, common_rows))
    lax.fori_loop(common_rows // ROWS, n_chunks, add_chunk, 0)
    o_ref[...] = x1_ref[...] + mod_ref[5:6, :] * acc_s[...]


def _padded_capacity(cap, n_blk):
    return -(-(cap + (ALIGN - 1) * n_blk + WIN) // ROWS) * ROWS


def _step_spec(*shape):
    return pl.BlockSpec((STEP_BLOCKS,) + shape, lambda i, s: (i,) + (0,) * len(shape))


def _common_rows(n_tok):
    return COMMON_ROWS + (ROWS if n_tok > ROWS else 0)


def _dispatch(h2, posm, s_tab, cap, n_tok):
    n_blk = h2.shape[0] // ROWS
    cap_pad = _padded_capacity(cap, n_blk)
    return pl.pallas_call(
        functools.partial(_dispatch_kernel, n_blk=n_blk, cap_pad=cap_pad, common_rows=_common_rows(n_tok)),
        out_shape=jax.ShapeDtypeStruct((N_EXPERTS, cap_pad, XE_W), BF16),
        grid_spec=pltpu.PrefetchScalarGridSpec(
            num_scalar_prefetch=1, grid=(n_blk // STEP_BLOCKS,),
            in_specs=[pl.BlockSpec((STEP_BLOCKS * ROWS, XE_W), lambda i, s: (i, 0)), _step_spec(N_EXPERTS, ROWS)],
            out_specs=pl.BlockSpec(memory_space=pl.ANY),
            scratch_shapes=[pltpu.VMEM((MAX_UNITS * WIN, ROWS), BF16), pltpu.VMEM((2, MAX_UNITS * WIN, XE_W), BF16),
                            pltpu.SemaphoreType.DMA((2, N_EXPERTS))]),
        compiler_params=_params(1), name="dispatch",
    )(s_tab, h2, posm)


def _ffn(xes, s_tabs, wg, wu, wd):
    n_sets = len(xes)
    per_expert = lambda rows, cols: pl.BlockSpec((None, rows, cols), lambda e, *s: (e, 0, 0))
    return pl.pallas_call(
        functools.partial(_ffn_kernel, n_blks=tuple(s.shape[1] - 1 for s in s_tabs)),
        out_shape=[jax.ShapeDtypeStruct(xe.shape[:2] + (D_MODEL,), BF16) for xe in xes],
        grid_spec=pltpu.PrefetchScalarGridSpec(
            num_scalar_prefetch=n_sets, grid=(N_EXPERTS,),
            in_specs=[per_expert(xe.shape[1], XE_W) for xe in xes]
            + [per_expert(D_MODEL, EXPERT_FF), per_expert(D_MODEL, EXPERT_FF), per_expert(EXPERT_FF, D_MODEL)],
            out_specs=[per_expert(xe.shape[1], D_MODEL) for xe in xes],
            scratch_shapes=[pltpu.VMEM((D_MODEL, EXPERT_FF), BF16), pltpu.VMEM((D_MODEL, EXPERT_FF), BF16),
                            pltpu.VMEM((EXPERT_FF, D_MODEL), BF16)]),
        compiler_params=_params(1), name="ffn",
    )(*s_tabs, *xes, wg, wu, wd)


def _combine(y, posm, s_tab, x1, mod, n_tok):
    n_all = x1.shape[0]
    n_blk = n_all // ROWS
    shared_mod = mod.shape[0] == 1
    assert shared_mod or n_tok % (STEP_BLOCKS * ROWS) == 0
    steps_per_seq = max(n_tok // (STEP_BLOCKS * ROWS), 1)
    mod_map = (lambda i, s: (0, 0, 0)) if shared_mod else (lambda i, s: (i // steps_per_seq, 0, 0))
    rows = pl.BlockSpec((STEP_BLOCKS * ROWS, D_MODEL), lambda i, s: (i, 0))
    return pl.pallas_call(
        functools.partial(_combine_kernel, n_blk=n_blk, cap_pad=y.shape[1], common_rows=_common_rows(n_tok)),
        out_shape=jax.ShapeDtypeStruct((n_all, D_MODEL), F32),
        grid_spec=pltpu.PrefetchScalarGridSpec(
            num_scalar_prefetch=1, grid=(n_blk // STEP_BLOCKS,),
            in_specs=[pl.BlockSpec(memory_space=pl.ANY), _step_spec(N_EXPERTS, ROWS), rows,
                      pl.BlockSpec((None, 6, D_MODEL), mod_map)],
            out_specs=rows,
            scratch_shapes=[pltpu.VMEM((2, MAX_UNITS * WIN, D_MODEL), BF16), pltpu.VMEM((MAX_UNITS * WIN, ROWS), BF16),
                            pltpu.VMEM((ROWS, D_MODEL), F32), pltpu.SemaphoreType.DMA((2,))]),
        compiler_params=_params(1), name="combine",
    )(s_tab, y, posm, x1, mod)


def _rope_tables(n_tok):
    rows = n_tok // GRID_W
    row = np.repeat(np.arange(rows, dtype=np.float64), GRID_W)
    col = np.tile(np.arange(GRID_W, dtype=np.float64), rows)
    axis_dim = QK_ROPE // 2
    inv_freq = ROPE_THETA ** (-np.arange(0, axis_dim, 2, dtype=np.float64) / axis_dim)
    ang = np.concatenate([row[:, None] * inv_freq, col[:, None] * inv_freq], axis=-1)
    cos = np.ones((n_tok, LANES))
    sin = np.zeros((n_tok, LANES))
    cos[:, ROPE_LANE0:ROPE_LANE0 + QK_ROPE] = np.repeat(np.cos(ang), 2, axis=1)
    sgn = np.tile(np.array([-1.0, 1.0]), QK_ROPE // 2)
    sin[:, ROPE_LANE0:ROPE_LANE0 + QK_ROPE] = np.repeat(np.sin(ang), 2, axis=1) * sgn
    return jnp.asarray(cos, F32), jnp.asarray(sin, F32)


def _dft_tables(n_tok):
    def cs(n):
        k = np.arange(n)
        ang = 2.0 * np.pi * ((k[:, None] * k[None, :]) % n) / n
        return np.cos(ang), np.sin(ang)
    cc, sc = cs(FNET_CH)
    cn, sn = cs(n_tok)
    half = n_tok // 2
    used = half if n_tok > ROWS else n_tok
    folded = np.concatenate([cn[:, :used], -sn[:, :used]], axis=1)
    flip = np.fliplr(np.eye(min(ROWS, half)))
    return tuple(jnp.asarray(t, F32) for t in (cc, sc, folded)) + (jnp.asarray(flip, BF16),)


def _pad_heads(w, width):
    lead = w.shape[:-1]
    w = w.reshape(lead + (N_HEADS, width))
    w = jnp.pad(w, [(0, 0)] * len(lead) + [(0, 0), (0, HEAD_PAD - width)])
    return w.reshape(lead + (N_HEADS * HEAD_PAD,))


def _pair_swap_lanes(n_groups):
    perm = np.arange(n_groups * LANES).reshape(n_groups, LANES)
    rot = perm[:, ROPE_LANE0:ROPE_LANE0 + QK_ROPE].reshape(n_groups, QK_ROPE // 2, 2)[:, :, ::-1]
    perm[:, ROPE_LANE0:ROPE_LANE0 + QK_ROPE] = rot.reshape(n_groups, QK_ROPE)
    return perm.reshape(-1)


def _spare_from_rotary(t, src=None):
    src = t if src is None else src
    lead = t.shape[:-1]
    t, src = t.reshape(lead + (-1, LANES)), src.reshape(lead + (-1, LANES))
    t = jnp.concatenate([t[..., :QK_DIM], src[..., ROPE_LANE0:ROPE_LANE0 + QK_ROPE]], axis=-1)
    return t.reshape(lead + (-1,))


def _head_tables(gain, rope_tabs, n_tok, query):
    if rope_tabs is None:
        tab = jnp.broadcast_to(gain, (n_tok, LANES))
        return (tab if query else _spare_from_rotary(tab),)
    cos, sin = rope_tabs
    a, b = gain * cos, gain[:, _pair_swap_lanes(1)] * sin
    return (_spare_from_rotary(a, b),) if query else (_spare_from_rotary(a), _spare_from_rotary(b))


def _layer0_weights(norm1, norm2, w_in, q_a_norm, w_q_up, q_norm, kv_a_norm, w_kv_up, k_norm, conv_w, w_o):
    c0 = Q_LORA + KV_LORA
    rope_cols = jnp.pad(w_in[:, c0:c0 + QK_ROPE], ((0, 0), (ROPE_LANE0, LANES - ROPE_LANE0 - QK_ROPE)))
    w_in_pad = jnp.concatenate([w_in[:, :c0], _spare_from_rotary(rope_cols),
                                _spare_from_rotary(rope_cols[:, _pair_swap_lanes(1)]),
                                w_in[:, c0 + QK_ROPE:]], axis=1)
    kv = w_kv_up.reshape(KV_LORA, N_HEADS, QK_NOPE + V_HEAD)
    w_q = _pad_heads(w_q_up, QK_DIM)
    w_q = _spare_from_rotary(w_q, w_q[:, _pair_swap_lanes(N_HEADS)])
    head_gain = lambda g: jnp.pad(g, (0, HEAD_PAD - QK_DIM)).reshape(1, -1)
    return dict(
        norm1=norm1.reshape(1, -1), norm2=norm2.reshape(1, -1), w_in=w_in_pad.astype(BF16),
        q_a_norm=q_a_norm.reshape(1, -1), w_q=w_q.astype(BF16),
        q_gain=head_gain(q_norm) * (QK_DIM ** -0.5), kv_a_norm=kv_a_norm.reshape(1, -1),
        w_k=_pad_heads(kv[:, :, :QK_NOPE].reshape(KV_LORA, -1), QK_NOPE).astype(BF16),
        w_v=kv[:, :, QK_NOPE:].reshape(KV_LORA, -1).astype(BF16),
        w_v_wide=_pad_heads(kv[:, :, QK_NOPE:].reshape(KV_LORA, -1), V_HEAD).astype(BF16), k_gain=head_gain(k_norm),
        conv_w=conv_w, w_o=w_o.astype(BF16))


def _router_weights(w_router):
    hi, lo = _split_hi_lo(jnp.pad(w_router, ((0, 0), (0, LANES - N_EXPERTS))))
    return dict(wr_hi=hi, wr_lo=lo)


def _moe(sets, tri, w):
    routed = []
    for x1, h2, aff, mod, n_tok in sets:
        cap = CAPACITY_FACTOR * x1.shape[0] // N_EXPERTS
        posm, s_tab = _route(aff, tri, cap)
        routed.append((posm, s_tab, _dispatch(h2, posm, s_tab, cap, n_tok)))
    ys = _ffn([r[2] for r in routed], [r[1] for r in routed], w["wg"], w["wu"], w["wd"])
    return [_combine(y, posm, s_tab, x1, mod, n_tok)
            for y, (posm, s_tab, _), (x1, _, _, mod, n_tok) in zip(ys, routed, sets)]


def _mixer0(x, mod, l0, ctx, rope_tabs, state):
    n_b, n_tok, _ = x.shape
    n_keys = n_tok + (0 if ctx is None else ctx[0].shape[1])
    w_v = l0["w_v_wide"] if n_keys > WIDE_ROWS else l0["w_v"]
    q, k, v, conv, *new_state = _front0(x, mod, l0, rope_tabs, w_v, state)
    if state:
        new_state = (new_state[0], new_state[1][:, :, ROPE_LANE0:ROPE_LANE0 + QK_ROPE])
    if ctx is not None:
        kc, vc = _ctx_kv(ctx[0], ctx[1], l0, w_v)
        attn = _attention(q, k, v, kc, vc)
    else:
        attn = _attention(q, k, v)
    flat = lambda a: a.reshape(n_b * n_tok, a.shape[-1])
    return _post0(flat(x), flat(attn), flat(conv), mod, l0, n_tok), new_state


def kernel(x_prompt, x_sample, c, cache_c_kv_l0, cache_k_rope_l0, c_ctx, norm1_l0, norm2_l0, w_mod_l0, b_mod_l0, w_in_l0, q_a_norm_l0, w_q_up_l0, q_norm_l0, kv_a_norm_l0, w_kv_up_l0, k_norm_l0, conv_w_l0, w_o_l0, w_router_l0, w_gate_l0, w_up_l0, w_down_l0, norm1_l1, norm2_l1, w_mod_l1, b_mod_l1, w_f_l1, w_router_l1, w_gate_l1, w_up_l1, w_down_l1):
    n_dec = c.shape[0]
    cond = jnp.concatenate([c_ctx[None, :], c, jnp.zeros((16 - 1 - n_dec, D_MODEL), F32)], axis=0)
    m0 = _modulation(cond, w_mod_l0, b_mod_l0)
    m1 = _modulation(cond, w_mod_l1, b_mod_l1)
    mods_prompt = (m0[0:1], m1[0:1])
    mods_sample = (m0[1:1 + n_dec], m1[1:1 + n_dec])

    l0 = _layer0_weights(norm1_l0, norm2_l0, w_in_l0, q_a_norm_l0, w_q_up_l0, q_norm_l0, kv_a_norm_l0,
                         w_kv_up_l0, k_norm_l0, conv_w_l0, w_o_l0)
    l0.update(_router_weights(w_router_l0))
    l0.update(wg=w_gate_l0, wu=w_up_l0, wd=w_down_l0)
    l1 = dict(norm1=norm1_l1.reshape(1, -1), norm2=norm2_l1.reshape(1, -1), w_f=w_f_l1.astype(BF16))
    l1.update(_router_weights(w_router_l1))
    l1.update(wg=w_gate_l1, wu=w_up_l1, wd=w_down_l1)

    tri = jnp.asarray(np.triu(np.ones((ROWS, ROWS)), 1), BF16)
    k_rope_pad = _spare_from_rotary(
        jnp.pad(cache_k_rope_l0, ((0, 0), (0, 0), (ROPE_LANE0, LANES - ROPE_LANE0 - QK_ROPE))))
    xs = (x_prompt, x_sample)
    mods = (mods_prompt, mods_sample)
    n_toks = tuple(x.shape[1] for x in xs)

    (front_p, (new_c_kv, new_k_rope)) = _mixer0(x_prompt, mods_prompt[0], l0, None, None, True)
    (front_s, _) = _mixer0(x_sample, mods_sample[0], l0, (cache_c_kv_l0, k_rope_pad), _rope_tables(n_toks[1]),
                           False)
    ys = _moe([tuple(front) + (mod[0], n_tok) for front, mod, n_tok in zip((front_p, front_s), mods, n_toks)],
              tri, l0)

    sets = []
    for y, x, mod, n_tok in zip(ys, xs, mods, n_toks):
        x1, h2, aff = _fnet(y.reshape(x.shape), mod[1], l1, _dft_tables(n_tok))
        sets.append((x1.reshape(-1, D_MODEL), h2.reshape(-1, XE_W), aff, mod[1], n_tok))
    y_prompt, y_sample = (y.reshape(x.shape) for y, x in zip(_moe(sets, tri, l1), xs))
    return (y_prompt, y_sample, new_c_kv, new_k_rope)
```
